```python
import math
import jax, jax.numpy as jnp
from jax import lax
import numpy as np


D_MODEL = 1024
BATCH = 8
SEQ = 4096
DEPTH = 2

CHUNK = 64
D_MIX = D_MODEL
D_A = 3 * D_MIX // 8
A_BLOCK_DIM = 64
A_BLOCKS = D_A // A_BLOCK_DIM
CONV_WIDTH = 4
RG_C = 8.0
B_HEAD_DIM = 64
D_B = 3 * D_MIX // 8
B_HEADS = D_B // B_HEAD_DIM
Q_BLOCK = 128
D_C = D_MIX - D_A - D_B
C_GROUP = 16
C_GROUPS = D_C // C_GROUP
C_STATE = 64
N_IN = 2 * D_A + 3 * D_B + B_HEADS + D_C
D_FF = 2816
ALPHA = (2 * DEPTH) ** 0.25
BETA = (8 * DEPTH) ** -0.25
LN_EPS = 1e-5
RMS_EPS = 1e-6

kernel_name = "hybrid_rglru_fox_s5_macaron_deepnorm"


def layer_norm(x, g, b):
    xf = x.astype(jnp.float32)
    mu = jnp.mean(xf, axis=-1, keepdims=True)
    var = jnp.mean(jnp.square(xf - mu), axis=-1, keepdims=True)
    return ((xf - mu) * lax.rsqrt(var + LN_EPS) * g + b).astype(x.dtype)


def rms_norm(x, g):
    xf = x.astype(jnp.float32)
    return xf * lax.rsqrt(jnp.mean(xf * xf, axis=-1, keepdims=True) + RMS_EPS) * g


def swiglu(x, w_gate, w_up, w_down):
    return (jax.nn.silu(x @ w_gate) * (x @ w_up)) @ w_down


def causal_depthwise_conv(x, w, b):
    s = x.shape[1]
    xp = jnp.pad(x, ((0, 0), (CONV_WIDTH - 1, 0), (0, 0)))
    out = b
    for k in range(CONV_WIDTH):
        out = out + w[k] * xp[:, k:k + s]
    return out


def linear_scan(a, b):
    def combine(left, right):
        a_l, b_l = left
        a_r, b_r = right
        return a_l * a_r, a_r * b_l + b_r
    return lax.associative_scan(combine, (a, b), axis=1)[1]


def rg_lru(x, w_a, b_a, w_x, b_x, lam):
    bsz, s, _ = x.shape
    xb = x.reshape(bsz, s, A_BLOCKS, A_BLOCK_DIM)
    r = jax.nn.sigmoid(jnp.einsum('bshi,hij->bshj', xb, w_a).reshape(bsz, s, D_A) + b_a)
    i = jax.nn.sigmoid(jnp.einsum('bshi,hij->bshj', xb, w_x).reshape(bsz, s, D_A) + b_x)
    log_a = -RG_C * r * jax.nn.softplus(-lam)
    a = jnp.exp(log_a)
    gated = jnp.sqrt(-jnp.expm1(2.0 * log_a)) * (i * x)
    return linear_scan(a, gated)


def forgetting_attention(q, k, v, log_f):
    s = q.shape[2]
    scale = B_HEAD_DIM ** -0.5
    c = jnp.cumsum(log_f, axis=-1)
    outs = []
    for blk in range(s // Q_BLOCK):
        q0 = blk * Q_BLOCK
        q1 = q0 + Q_BLOCK
        qb = q[:, :, q0:q1]
        kb = k[:, :, :q1]
        vb = v[:, :, :q1]
        logits = jnp.einsum('bhqd,bhkd->bhqk', qb, kb) * scale
        logits = logits + (c[:, :, q0:q1, None] - c[:, :, None, :q1])
        mask = (q0 + jnp.arange(Q_BLOCK))[:, None] >= jnp.arange(q1)[None, :]
        logits = jnp.where(mask, logits, -jnp.inf)
        p = jax.nn.softmax(logits, axis=-1)
        outs.append(jnp.einsum('bhqk,bhkd->bhqd', p, vb))
    return jnp.concatenate(outs, axis=2)


def s5_ssm(u, a_re, a_im, log_dt, b_re, b_im, c_re, c_im, d, w_glu):
    f32 = jnp.float32
    bsz, s, _ = u.shape
    lam = lax.complex(a_re.astype(f32), a_im.astype(f32))
    dt = jnp.exp(log_dt.astype(f32))[:, None]
    a_bar = jnp.exp(lam * dt)
    b_bar = ((a_bar - 1.0) / lam)[:, :, None] * lax.complex(b_re.astype(f32), b_im.astype(f32))
    c_mat = lax.complex(c_re.astype(f32), c_im.astype(f32))
    ug = u.astype(f32).reshape(bsz, s, C_GROUPS, C_GROUP)
    bu = jnp.einsum('bsgc,gpc->bsgp', ug.astype(jnp.complex64), b_bar)
    a_seq = jnp.broadcast_to(a_bar, (1, s, C_GROUPS, C_STATE))
    h = linear_scan(a_seq, bu)
    y = jnp.einsum('bsgp,gcp->bsgc', h, c_mat).real + d.astype(f32).reshape(C_GROUPS, C_GROUP) * ug
    y = jax.nn.gelu(y.reshape(bsz, s, D_C))
    return y * jax.nn.sigmoid(y @ w_glu.astype(f32))


def hybrid_mixer(x, w_in, conv_w, conv_b, rg_w_a, rg_b_a, rg_w_x, rg_b_x, rg_lambda, fox_b_f,
                 s5_a_re, s5_a_im, s5_log_dt, s5_b_re, s5_b_im, s5_c_re, s5_c_im, s5_d, s5_w_glu,
                 mix_norm_g, w_out):
    f32 = jnp.float32
    bsz, s, _ = x.shape
    z = x @ w_in
    cuts = [int(v) for v in np.cumsum([D_A, D_A, D_B, D_B, D_B, B_HEADS])]
    a_x, a_gate, q, k, v, f_logit, c_u = jnp.split(z, cuts, axis=-1)

    xa = causal_depthwise_conv(a_x, conv_w, conv_b).astype(f32)
    out_a = jax.nn.gelu(a_gate.astype(f32)) * rg_lru(xa, rg_w_a, rg_b_a, rg_w_x, rg_b_x, rg_lambda)

    def heads(t):
        return t.astype(f32).reshape(bsz, s, B_HEADS, B_HEAD_DIM).transpose(0, 2, 1, 3)
    log_f = jax.nn.log_sigmoid(f_logit.astype(f32) + fox_b_f).transpose(0, 2, 1)
    out_b = forgetting_attention(heads(q), heads(k), heads(v), log_f)
    out_b = out_b.transpose(0, 2, 1, 3).reshape(bsz, s, D_B)

    out_c = s5_ssm(c_u, s5_a_re, s5_a_im, s5_log_dt, s5_b_re, s5_b_im, s5_c_re, s5_c_im, s5_d, s5_w_glu)

    g_a, g_b, g_c = jnp.split(mix_norm_g, [D_A, D_A + D_B])
    o = jnp.concatenate([rms_norm(out_a, g_a), rms_norm(out_b, g_b), rms_norm(out_c, g_c)], axis=-1)
    return o.astype(x.dtype) @ w_out


def _fwd_setup_inputs(seed: int = 0) -> dict:
    key = jax.random.key(seed)
    ks = jax.random.split(key, 40)
    L = DEPTH
    f32 = jnp.float32

    def nrm(k, shape, scale):
        return scale * jax.random.normal(k, shape, f32)

    u_a = jax.random.uniform(ks[11], (L, D_A), f32, minval=0.9, maxval=0.999)
    a0 = u_a ** (1.0 / RG_C)
    rg_lambda = jnp.log(a0) - jnp.log1p(-a0)
    s5_a_im = jnp.pi * jnp.arange(C_STATE, dtype=f32)[None, None, :] + nrm(ks[14], (L, C_GROUPS, C_STATE), 0.01)
    s5_log_dt = jax.random.uniform(ks[15], (L, C_GROUPS), f32, minval=math.log(1e-3), maxval=math.log(1e-1))
    return {
        'x': jax.random.normal(ks[0], (BATCH, SEQ, D_MODEL), f32),
        'ffn1_w_gate': nrm(ks[1], (L, D_MODEL, D_FF), D_MODEL ** -0.5),
        'ffn1_w_up': nrm(ks[2], (L, D_MODEL, D_FF), D_MODEL ** -0.5),
        'ffn1_w_down': nrm(ks[3], (L, D_FF, D_MODEL), BETA * D_FF ** -0.5),
        'ln1_g': 1.0 + nrm(ks[4], (L, D_MODEL), 0.02),
        'ln1_b': nrm(ks[5], (L, D_MODEL), 0.02),
        'w_in': nrm(ks[6], (L, D_MODEL, N_IN), D_MODEL ** -0.5),
        'conv_w': nrm(ks[7], (L, CONV_WIDTH, D_A), CONV_WIDTH ** -0.5),
        'conv_b': nrm(ks[8], (L, D_A), 0.02),
        'rg_w_a': nrm(ks[9], (L, A_BLOCKS, A_BLOCK_DIM, A_BLOCK_DIM), A_BLOCK_DIM ** -0.5),
        'rg_b_a': nrm(ks[10], (L, D_A), 0.02),
        'rg_w_x': nrm(ks[12], (L, A_BLOCKS, A_BLOCK_DIM, A_BLOCK_DIM), A_BLOCK_DIM ** -0.5),
        'rg_b_x': nrm(ks[16], (L, D_A), 0.02),
        'rg_lambda': rg_lambda,
        'fox_b_f': 2.0 + nrm(ks[17], (L, B_HEADS), 0.1),
        's5_a_re': -0.5 + nrm(ks[13], (L, C_GROUPS, C_STATE), 0.01),
        's5_a_im': s5_a_im,
        's5_log_dt': s5_log_dt,
        's5_b_re': nrm(ks[18], (L, C_GROUPS, C_STATE, C_GROUP), (2 * C_GROUP) ** -0.5),
        's5_b_im': nrm(ks[19], (L, C_GROUPS, C_STATE, C_GROUP), (2 * C_GROUP) ** -0.5),
        's5_c_re': nrm(ks[20], (L, C_GROUPS, C_GROUP, C_STATE), (2 * C_STATE) ** -0.5),
        's5_c_im': nrm(ks[21], (L, C_GROUPS, C_GROUP, C_STATE), (2 * C_STATE) ** -0.5),
        's5_d': nrm(ks[22], (L, D_C), 1.0),
        's5_w_glu': nrm(ks[23], (L, D_C, D_C), D_C ** -0.5),
        'mix_norm_g': 1.0 + nrm(ks[24], (L, D_MIX), 0.02),
        'w_out': nrm(ks[25], (L, D_MIX, D_MODEL), BETA * D_MIX ** -0.5),
        'ln2_g': 1.0 + nrm(ks[26], (L, D_MODEL), 0.02),
        'ln2_b': nrm(ks[27], (L, D_MODEL), 0.02),
        'ffn2_w_gate': nrm(ks[28], (L, D_MODEL, D_FF), D_MODEL ** -0.5),
        'ffn2_w_up': nrm(ks[29], (L, D_MODEL, D_FF), D_MODEL ** -0.5),
        'ffn2_w_down': nrm(ks[30], (L, D_FF, D_MODEL), BETA * D_FF ** -0.5),
        'ln3_g': 1.0 + nrm(ks[31], (L, D_MODEL), 0.02),
        'ln3_b': nrm(ks[32], (L, D_MODEL), 0.02),
    }


def _fwd_reference(x, ffn1_w_gate, ffn1_w_up, ffn1_w_down, ln1_g, ln1_b,
              w_in, conv_w, conv_b, rg_w_a, rg_b_a, rg_w_x, rg_b_x, rg_lambda, fox_b_f,
              s5_a_re, s5_a_im, s5_log_dt, s5_b_re, s5_b_im, s5_c_re, s5_c_im, s5_d, s5_w_glu,
              mix_norm_g, w_out, ln2_g, ln2_b,
              ffn2_w_gate, ffn2_w_up, ffn2_w_down, ln3_g, ln3_b):
    for l in range(DEPTH):
        x = layer_norm(ALPHA * x + 0.5 * swiglu(x, ffn1_w_gate[l], ffn1_w_up[l], ffn1_w_down[l]),
                       ln1_g[l], ln1_b[l])
        mix = hybrid_mixer(x, w_in[l], conv_w[l], conv_b[l], rg_w_a[l], rg_b_a[l], rg_w_x[l], rg_b_x[l],
                           rg_lambda[l], fox_b_f[l],
                           s5_a_re[l], s5_a_im[l], s5_log_dt[l], s5_b_re[l], s5_b_im[l],
                           s5_c_re[l], s5_c_im[l], s5_d[l], s5_w_glu[l],
                           mix_norm_g[l], w_out[l])
        x = layer_norm(ALPHA * x + mix, ln2_g[l], ln2_b[l])
        x = layer_norm(ALPHA * x + 0.5 * swiglu(x, ffn2_w_gate[l], ffn2_w_up[l], ffn2_w_down[l]),
                       ln3_g[l], ln3_b[l])
    return x


import jax as _jax
import jax.numpy as _jnp

TWIN_FORMAT = 'train_step'
FWD_PARAMS = ['x', 'ffn1_w_gate', 'ffn1_w_up', 'ffn1_w_down', 'ln1_g', 'ln1_b', 'w_in', 'conv_w', 'conv_b', 'rg_w_a', 'rg_b_a', 'rg_w_x', 'rg_b_x', 'rg_lambda', 'fox_b_f', 's5_a_re', 's5_a_im', 's5_log_dt', 's5_b_re', 's5_b_im', 's5_c_re', 's5_c_im', 's5_d', 's5_w_glu', 'mix_norm_g', 'w_out', 'ln2_g', 'ln2_b', 'ffn2_w_gate', 'ffn2_w_up', 'ffn2_w_down', 'ln3_g', 'ln3_b']
TWIN_WEIGHTS = ['ffn1_w_gate', 'ffn1_w_up', 'ffn1_w_down', 'ln1_g', 'ln1_b', 'w_in', 'conv_w', 'conv_b', 'rg_w_a', 'rg_b_a', 'rg_w_x', 'rg_b_x', 'rg_lambda', 'fox_b_f', 's5_a_re', 's5_a_im', 's5_log_dt', 's5_b_re', 's5_b_im', 's5_c_re', 's5_c_im', 's5_d', 's5_w_glu', 'mix_norm_g', 'w_out', 'ln2_g', 'ln2_b', 'ffn2_w_gate', 'ffn2_w_up', 'ffn2_w_down', 'ln3_g', 'ln3_b']
TWIN_DIFF_INPUT = 'x'
TWIN_INPUTS = ['x', 'ffn1_w_gate', 'ffn1_w_up', 'ffn1_w_down', 'ln1_g', 'ln1_b', 'w_in', 'conv_w', 'conv_b', 'rg_w_a', 'rg_b_a', 'rg_w_x', 'rg_b_x', 'rg_lambda', 'fox_b_f', 's5_a_re', 's5_a_im', 's5_log_dt', 's5_b_re', 's5_b_im', 's5_c_re', 's5_c_im', 's5_d', 's5_w_glu', 'mix_norm_g', 'w_out', 'ln2_g', 'ln2_b', 'ffn2_w_gate', 'ffn2_w_up', 'ffn2_w_down', 'ln3_g', 'ln3_b', 'loss_target', 'm_ffn1_w_gate', 'm_ffn1_w_up', 'm_ffn1_w_down', 'm_ln1_g', 'm_ln1_b', 'm_w_in', 'm_conv_w', 'm_conv_b', 'm_rg_w_a', 'm_rg_b_a', 'm_rg_w_x', 'm_rg_b_x', 'm_rg_lambda', 'm_fox_b_f', 'm_s5_a_re', 'm_s5_a_im', 'm_s5_log_dt', 'm_s5_b_re', 'm_s5_b_im', 'm_s5_c_re', 'm_s5_c_im', 'm_s5_d', 'm_s5_w_glu', 'm_mix_norm_g', 'm_w_out', 'm_ln2_g', 'm_ln2_b', 'm_ffn2_w_gate', 'm_ffn2_w_up', 'm_ffn2_w_down', 'm_ln3_g', 'm_ln3_b', 'v_ffn1_w_gate', 'v_ffn1_w_up', 'v_ffn1_w_down', 'v_ln1_g', 'v_ln1_b', 'v_w_in', 'v_conv_w', 'v_conv_b', 'v_rg_w_a', 'v_rg_b_a', 'v_rg_w_x', 'v_rg_b_x', 'v_rg_lambda', 'v_fox_b_f', 'v_s5_a_re', 'v_s5_a_im', 'v_s5_log_dt', 'v_s5_b_re', 'v_s5_b_im', 'v_s5_c_re', 'v_s5_c_im', 'v_s5_d', 'v_s5_w_glu', 'v_mix_norm_g', 'v_w_out', 'v_ln2_g', 'v_ln2_b', 'v_ffn2_w_gate', 'v_ffn2_w_up', 'v_ffn2_w_down', 'v_ln3_g', 'v_ln3_b']
TWIN_OUTPUTS = ['loss', 'grad_x', 'grad_ffn1_w_gate', 'grad_ffn1_w_up', 'grad_ffn1_w_down', 'grad_ln1_g', 'grad_ln1_b', 'grad_w_in', 'grad_conv_w', 'grad_conv_b', 'grad_rg_w_a', 'grad_rg_b_a', 'grad_rg_w_x', 'grad_rg_b_x', 'grad_rg_lambda', 'grad_fox_b_f', 'grad_s5_a_re', 'grad_s5_a_im', 'grad_s5_log_dt', 'grad_s5_b_re', 'grad_s5_b_im', 'grad_s5_c_re', 'grad_s5_c_im', 'grad_s5_d', 'grad_s5_w_glu', 'grad_mix_norm_g', 'grad_w_out', 'grad_ln2_g', 'grad_ln2_b', 'grad_ffn2_w_gate', 'grad_ffn2_w_up', 'grad_ffn2_w_down', 'grad_ln3_g', 'grad_ln3_b', 'delta_ffn1_w_gate', 'delta_ffn1_w_up', 'delta_ffn1_w_down', 'delta_ln1_g', 'delta_ln1_b', 'delta_w_in', 'delta_conv_w', 'delta_conv_b', 'delta_rg_w_a', 'delta_rg_b_a', 'delta_rg_w_x', 'delta_rg_b_x', 'delta_rg_lambda', 'delta_fox_b_f', 'delta_s5_a_re', 'delta_s5_a_im', 'delta_s5_log_dt', 'delta_s5_b_re', 'delta_s5_b_im', 'delta_s5_c_re', 'delta_s5_c_im', 'delta_s5_d', 'delta_s5_w_glu', 'delta_mix_norm_g', 'delta_w_out', 'delta_ln2_g', 'delta_ln2_b', 'delta_ffn2_w_gate', 'delta_ffn2_w_up', 'delta_ffn2_w_down', 'delta_ln3_g', 'delta_ln3_b', 'new_m_ffn1_w_gate', 'new_m_ffn1_w_up', 'new_m_ffn1_w_down', 'new_m_ln1_g', 'new_m_ln1_b', 'new_m_w_in', 'new_m_conv_w', 'new_m_conv_b', 'new_m_rg_w_a', 'new_m_rg_b_a', 'new_m_rg_w_x', 'new_m_rg_b_x', 'new_m_rg_lambda', 'new_m_fox_b_f', 'new_m_s5_a_re', 'new_m_s5_a_im', 'new_m_s5_log_dt', 'new_m_s5_b_re', 'new_m_s5_b_im', 'new_m_s5_c_re', 'new_m_s5_c_im', 'new_m_s5_d', 'new_m_s5_w_glu', 'new_m_mix_norm_g', 'new_m_w_out', 'new_m_ln2_g', 'new_m_ln2_b', 'new_m_ffn2_w_gate', 'new_m_ffn2_w_up', 'new_m_ffn2_w_down', 'new_m_ln3_g', 'new_m_ln3_b', 'new_v_ffn1_w_gate', 'new_v_ffn1_w_up', 'new_v_ffn1_w_down', 'new_v_ln1_g', 'new_v_ln1_b', 'new_v_w_in', 'new_v_conv_w', 'new_v_conv_b', 'new_v_rg_w_a', 'new_v_rg_b_a', 'new_v_rg_w_x', 'new_v_rg_b_x', 'new_v_rg_lambda', 'new_v_fox_b_f', 'new_v_s5_a_re', 'new_v_s5_a_im', 'new_v_s5_log_dt', 'new_v_s5_b_re', 'new_v_s5_b_im', 'new_v_s5_c_re', 'new_v_s5_c_im', 'new_v_s5_d', 'new_v_s5_w_glu', 'new_v_mix_norm_g', 'new_v_w_out', 'new_v_ln2_g', 'new_v_ln2_b', 'new_v_ffn2_w_gate', 'new_v_ffn2_w_up', 'new_v_ffn2_w_down', 'new_v_ln3_g', 'new_v_ln3_b']
TWIN_LEAF_KINDS = {'loss': 'loss', 'grad_x': 'grad_x', 'grad_ffn1_w_gate': 'grad_w', 'grad_ffn1_w_up': 'grad_w', 'grad_ffn1_w_down': 'grad_w', 'grad_ln1_g': 'grad_w', 'grad_ln1_b': 'grad_w', 'grad_w_in': 'grad_w', 'grad_conv_w': 'grad_w', 'grad_conv_b': 'grad_w', 'grad_rg_w_a': 'grad_w', 'grad_rg_b_a': 'grad_w', 'grad_rg_w_x': 'grad_w', 'grad_rg_b_x': 'grad_w', 'grad_rg_lambda': 'grad_w', 'grad_fox_b_f': 'grad_w', 'grad_s5_a_re': 'grad_w', 'grad_s5_a_im': 'grad_w', 'grad_s5_log_dt': 'grad_w', 'grad_s5_b_re': 'grad_w', 'grad_s5_b_im': 'grad_w', 'grad_s5_c_re': 'grad_w', 'grad_s5_c_im': 'grad_w', 'grad_s5_d': 'grad_w', 'grad_s5_w_glu': 'grad_w', 'grad_mix_norm_g': 'grad_w', 'grad_w_out': 'grad_w', 'grad_ln2_g': 'grad_w', 'grad_ln2_b': 'grad_w', 'grad_ffn2_w_gate': 'grad_w', 'grad_ffn2_w_up': 'grad_w', 'grad_ffn2_w_down': 'grad_w', 'grad_ln3_g': 'grad_w', 'grad_ln3_b': 'grad_w', 'delta_ffn1_w_gate': 'delta_w', 'delta_ffn1_w_up': 'delta_w', 'delta_ffn1_w_down': 'delta_w', 'delta_ln1_g': 'delta_w', 'delta_ln1_b': 'delta_w', 'delta_w_in': 'delta_w', 'delta_conv_w': 'delta_w', 'delta_conv_b': 'delta_w', 'delta_rg_w_a': 'delta_w', 'delta_rg_b_a': 'delta_w', 'delta_rg_w_x': 'delta_w', 'delta_rg_b_x': 'delta_w', 'delta_rg_lambda': 'delta_w', 'delta_fox_b_f': 'delta_w', 'delta_s5_a_re': 'delta_w', 'delta_s5_a_im': 'delta_w', 'delta_s5_log_dt': 'delta_w', 'delta_s5_b_re': 'delta_w', 'delta_s5_b_im': 'delta_w', 'delta_s5_c_re': 'delta_w', 'delta_s5_c_im': 'delta_w', 'delta_s5_d': 'delta_w', 'delta_s5_w_glu': 'delta_w', 'delta_mix_norm_g': 'delta_w', 'delta_w_out': 'delta_w', 'delta_ln2_g': 'delta_w', 'delta_ln2_b': 'delta_w', 'delta_ffn2_w_gate': 'delta_w', 'delta_ffn2_w_up': 'delta_w', 'delta_ffn2_w_down': 'delta_w', 'delta_ln3_g': 'delta_w', 'delta_ln3_b': 'delta_w', 'new_m_ffn1_w_gate': 'new_m', 'new_m_ffn1_w_up': 'new_m', 'new_m_ffn1_w_down': 'new_m', 'new_m_ln1_g': 'new_m', 'new_m_ln1_b': 'new_m', 'new_m_w_in': 'new_m', 'new_m_conv_w': 'new_m', 'new_m_conv_b': 'new_m', 'new_m_rg_w_a': 'new_m', 'new_m_rg_b_a': 'new_m', 'new_m_rg_w_x': 'new_m', 'new_m_rg_b_x': 'new_m', 'new_m_rg_lambda': 'new_m', 'new_m_fox_b_f': 'new_m', 'new_m_s5_a_re': 'new_m', 'new_m_s5_a_im': 'new_m', 'new_m_s5_log_dt': 'new_m', 'new_m_s5_b_re': 'new_m', 'new_m_s5_b_im': 'new_m', 'new_m_s5_c_re': 'new_m', 'new_m_s5_c_im': 'new_m', 'new_m_s5_d': 'new_m', 'new_m_s5_w_glu': 'new_m', 'new_m_mix_norm_g': 'new_m', 'new_m_w_out': 'new_m', 'new_m_ln2_g': 'new_m', 'new_m_ln2_b': 'new_m', 'new_m_ffn2_w_gate': 'new_m', 'new_m_ffn2_w_up': 'new_m', 'new_m_ffn2_w_down': 'new_m', 'new_m_ln3_g': 'new_m', 'new_m_ln3_b': 'new_m', 'new_v_ffn1_w_gate': 'new_v', 'new_v_ffn1_w_up': 'new_v', 'new_v_ffn1_w_down': 'new_v', 'new_v_ln1_g': 'new_v', 'new_v_ln1_b': 'new_v', 'new_v_w_in': 'new_v', 'new_v_conv_w': 'new_v', 'new_v_conv_b': 'new_v', 'new_v_rg_w_a': 'new_v', 'new_v_rg_b_a': 'new_v', 'new_v_rg_w_x': 'new_v', 'new_v_rg_b_x': 'new_v', 'new_v_rg_lambda': 'new_v', 'new_v_fox_b_f': 'new_v', 'new_v_s5_a_re': 'new_v', 'new_v_s5_a_im': 'new_v', 'new_v_s5_log_dt': 'new_v', 'new_v_s5_b_re': 'new_v', 'new_v_s5_b_im': 'new_v', 'new_v_s5_c_re': 'new_v', 'new_v_s5_c_im': 'new_v', 'new_v_s5_d': 'new_v', 'new_v_s5_w_glu': 'new_v', 'new_v_mix_norm_g': 'new_v', 'new_v_w_out': 'new_v', 'new_v_ln2_g': 'new_v', 'new_v_ln2_b': 'new_v', 'new_v_ffn2_w_gate': 'new_v', 'new_v_ffn2_w_up': 'new_v', 'new_v_ffn2_w_down': 'new_v', 'new_v_ln3_g': 'new_v', 'new_v_ln3_b': 'new_v'}


def _forward(args):
    return _fwd_reference(*[args[k] for k in FWD_PARAMS])


def _output_shape():
    out = _jax.eval_shape(lambda: _forward(_fwd_setup_inputs(0)))
    return out.shape, out.dtype

N_MICROBATCH = 1
ADAM_LR = 0.001
ADAM_B1 = 0.9
ADAM_B2 = 0.999
ADAM_EPS = 1e-08
ADAM_WD = 0.01
ADAM_STEP = 10
PER_EXAMPLE_BATCH_AXIS = {'x': 0, 'loss_target': 0}
SHARED_INPUTS = []
_WEIGHT_DTYPES = {'ffn1_w_gate': _jnp.float32, 'ffn1_w_up': _jnp.float32, 'ffn1_w_down': _jnp.float32, 'ln1_g': _jnp.float32, 'ln1_b': _jnp.float32, 'w_in': _jnp.float32, 'conv_w': _jnp.float32, 'conv_b': _jnp.float32, 'rg_w_a': _jnp.float32, 'rg_b_a': _jnp.float32, 'rg_w_x': _jnp.float32, 'rg_b_x': _jnp.float32, 'rg_lambda': _jnp.float32, 'fox_b_f': _jnp.float32, 's5_a_re': _jnp.float32, 's5_a_im': _jnp.float32, 's5_log_dt': _jnp.float32, 's5_b_re': _jnp.float32, 's5_b_im': _jnp.float32, 's5_c_re': _jnp.float32, 's5_c_im': _jnp.float32, 's5_d': _jnp.float32, 's5_w_glu': _jnp.float32, 'mix_norm_g': _jnp.float32, 'w_out': _jnp.float32, 'ln2_g': _jnp.float32, 'ln2_b': _jnp.float32, 'ffn2_w_gate': _jnp.float32, 'ffn2_w_up': _jnp.float32, 'ffn2_w_down': _jnp.float32, 'ln3_g': _jnp.float32, 'ln3_b': _jnp.float32}
MOMENT_SCALE = {'ffn1_w_gate': 1.244043e-02, 'ffn1_w_up': 1.204339e-02, 'ffn1_w_down': 3.992389e-02, 'ln1_g': 8.110916e-01, 'ln1_b': 6.861037e-01, 'w_in': 5.819628e-02, 'conv_w': 6.981387e-02, 'conv_b': 7.421974e-01, 'rg_w_a': 2.680205e-02, 'rg_b_a': 1.733409e-02, 'rg_w_x': 4.642320e-02, 'rg_b_x': 2.244424e-02, 'rg_lambda': 3.695581e-02, 'fox_b_f': 3.211571e-01, 's5_a_re': 4.314408e-03, 's5_a_im': 4.342900e-03, 's5_log_dt': 3.963368e+00, 's5_b_re': 2.463139e-03, 's5_b_im': 2.654700e-03, 's5_c_re': 4.875553e-03, 's5_c_im': 5.290316e-03, 's5_d': 1.006641e-01, 's5_w_glu': 1.926627e-02, 'mix_norm_g': 8.385998e-02, 'w_out': 1.509574e-01, 'ln2_g': 9.458933e-01, 'ln2_b': 5.320581e-01, 'ffn2_w_gate': 1.181916e-02, 'ffn2_w_up': 1.147013e-02, 'ffn2_w_down': 3.807694e-02, 'ln3_g': 2.267206e+01, 'ln3_b': 3.206993e+00}


def _to_microbatches(a, axis):
    t = _jnp.moveaxis(a, axis, 0)
    t = t.reshape((N_MICROBATCH, t.shape[0] // N_MICROBATCH) + t.shape[1:])
    return _jnp.moveaxis(t, 1, axis + 1)


def setup_inputs(seed: int = 0) -> dict:
    inp = _fwd_setup_inputs(seed)
    key = _jax.random.fold_in(_jax.random.key(seed), 7919)
    shape, _ = _output_shape()
    out = dict(inp)
    out["loss_target"] = _jax.random.normal(_jax.random.fold_in(key, 0), shape, _jnp.float32)
    for i, name in enumerate(TWIN_WEIGHTS):
        w = inp[name].astype(_jnp.float32)
        if MOMENT_SCALE is None:
            s = _jnp.sqrt(_jnp.mean(_jnp.square(w)) + 1e-30)
        else:
            s = MOMENT_SCALE[name]
        km, kv = _jax.random.split(_jax.random.fold_in(key, i + 1))
        out[name] = w
        out["m_" + name] = s * _jax.random.normal(km, w.shape, _jnp.float32)
        out["v_" + name] = (s * s) * _jax.random.uniform(kv, w.shape, _jnp.float32, 0.5, 1.5)
    if N_MICROBATCH > 1:
        for name, axis in PER_EXAMPLE_BATCH_AXIS.items():
            out[name] = _to_microbatches(out[name], axis)
    return {'x': out['x'], 'ffn1_w_gate': out['ffn1_w_gate'], 'ffn1_w_up': out['ffn1_w_up'], 'ffn1_w_down': out['ffn1_w_down'], 'ln1_g': out['ln1_g'], 'ln1_b': out['ln1_b'], 'w_in': out['w_in'], 'conv_w': out['conv_w'], 'conv_b': out['conv_b'], 'rg_w_a': out['rg_w_a'], 'rg_b_a': out['rg_b_a'], 'rg_w_x': out['rg_w_x'], 'rg_b_x': out['rg_b_x'], 'rg_lambda': out['rg_lambda'], 'fox_b_f': out['fox_b_f'], 's5_a_re': out['s5_a_re'], 's5_a_im': out['s5_a_im'], 's5_log_dt': out['s5_log_dt'], 's5_b_re': out['s5_b_re'], 's5_b_im': out['s5_b_im'], 's5_c_re': out['s5_c_re'], 's5_c_im': out['s5_c_im'], 's5_d': out['s5_d'], 's5_w_glu': out['s5_w_glu'], 'mix_norm_g': out['mix_norm_g'], 'w_out': out['w_out'], 'ln2_g': out['ln2_g'], 'ln2_b': out['ln2_b'], 'ffn2_w_gate': out['ffn2_w_gate'], 'ffn2_w_up': out['ffn2_w_up'], 'ffn2_w_down': out['ffn2_w_down'], 'ln3_g': out['ln3_g'], 'ln3_b': out['ln3_b'], 'loss_target': out['loss_target'], 'm_ffn1_w_gate': out['m_ffn1_w_gate'], 'm_ffn1_w_up': out['m_ffn1_w_up'], 'm_ffn1_w_down': out['m_ffn1_w_down'], 'm_ln1_g': out['m_ln1_g'], 'm_ln1_b': out['m_ln1_b'], 'm_w_in': out['m_w_in'], 'm_conv_w': out['m_conv_w'], 'm_conv_b': out['m_conv_b'], 'm_rg_w_a': out['m_rg_w_a'], 'm_rg_b_a': out['m_rg_b_a'], 'm_rg_w_x': out['m_rg_w_x'], 'm_rg_b_x': out['m_rg_b_x'], 'm_rg_lambda': out['m_rg_lambda'], 'm_fox_b_f': out['m_fox_b_f'], 'm_s5_a_re': out['m_s5_a_re'], 'm_s5_a_im': out['m_s5_a_im'], 'm_s5_log_dt': out['m_s5_log_dt'], 'm_s5_b_re': out['m_s5_b_re'], 'm_s5_b_im': out['m_s5_b_im'], 'm_s5_c_re': out['m_s5_c_re'], 'm_s5_c_im': out['m_s5_c_im'], 'm_s5_d': out['m_s5_d'], 'm_s5_w_glu': out['m_s5_w_glu'], 'm_mix_norm_g': out['m_mix_norm_g'], 'm_w_out': out['m_w_out'], 'm_ln2_g': out['m_ln2_g'], 'm_ln2_b': out['m_ln2_b'], 'm_ffn2_w_gate': out['m_ffn2_w_gate'], 'm_ffn2_w_up': out['m_ffn2_w_up'], 'm_ffn2_w_down': out['m_ffn2_w_down'], 'm_ln3_g': out['m_ln3_g'], 'm_ln3_b': out['m_ln3_b'], 'v_ffn1_w_gate': out['v_ffn1_w_gate'], 'v_ffn1_w_up': out['v_ffn1_w_up'], 'v_ffn1_w_down': out['v_ffn1_w_down'], 'v_ln1_g': out['v_ln1_g'], 'v_ln1_b': out['v_ln1_b'], 'v_w_in': out['v_w_in'], 'v_conv_w': out['v_conv_w'], 'v_conv_b': out['v_conv_b'], 'v_rg_w_a': out['v_rg_w_a'], 'v_rg_b_a': out['v_rg_b_a'], 'v_rg_w_x': out['v_rg_w_x'], 'v_rg_b_x': out['v_rg_b_x'], 'v_rg_lambda': out['v_rg_lambda'], 'v_fox_b_f': out['v_fox_b_f'], 'v_s5_a_re': out['v_s5_a_re'], 'v_s5_a_im': out['v_s5_a_im'], 'v_s5_log_dt': out['v_s5_log_dt'], 'v_s5_b_re': out['v_s5_b_re'], 'v_s5_b_im': out['v_s5_b_im'], 'v_s5_c_re': out['v_s5_c_re'], 'v_s5_c_im': out['v_s5_c_im'], 'v_s5_d': out['v_s5_d'], 'v_s5_w_glu': out['v_s5_w_glu'], 'v_mix_norm_g': out['v_mix_norm_g'], 'v_w_out': out['v_w_out'], 'v_ln2_g': out['v_ln2_g'], 'v_ln2_b': out['v_ln2_b'], 'v_ffn2_w_gate': out['v_ffn2_w_gate'], 'v_ffn2_w_up': out['v_ffn2_w_up'], 'v_ffn2_w_down': out['v_ffn2_w_down'], 'v_ln3_g': out['v_ln3_g'], 'v_ln3_b': out['v_ln3_b']}


def _loss(weights, diff, rest, loss_target):
    with _jax.named_scope("forward"):
        args = {**rest, TWIN_DIFF_INPUT: diff, **{k: w.astype(_WEIGHT_DTYPES[k]) for k, w in weights.items()}}
        y = _forward(args)
    with _jax.named_scope("loss_head"):
        err = _jnp.square(y.astype(_jnp.float32) - loss_target)
        return 0.5 * _jnp.sum(_jnp.mean(err, axis=-1)) if err.ndim else 0.5 * err


def _adamw(w, g, m, v):
    m = ADAM_B1 * m + (1.0 - ADAM_B1) * g
    v = ADAM_B2 * v + (1.0 - ADAM_B2) * _jnp.square(g)
    m_hat = m / (1.0 - ADAM_B1 ** ADAM_STEP)
    v_hat = v / (1.0 - ADAM_B2 ** ADAM_STEP)
    delta = -ADAM_LR * (m_hat / (_jnp.sqrt(v_hat) + ADAM_EPS) + ADAM_WD * w)
    return delta, m, v


def reference(x, ffn1_w_gate, ffn1_w_up, ffn1_w_down, ln1_g, ln1_b, w_in, conv_w, conv_b, rg_w_a, rg_b_a, rg_w_x, rg_b_x, rg_lambda, fox_b_f, s5_a_re, s5_a_im, s5_log_dt, s5_b_re, s5_b_im, s5_c_re, s5_c_im, s5_d, s5_w_glu, mix_norm_g, w_out, ln2_g, ln2_b, ffn2_w_gate, ffn2_w_up, ffn2_w_down, ln3_g, ln3_b, loss_target, m_ffn1_w_gate, m_ffn1_w_up, m_ffn1_w_down, m_ln1_g, m_ln1_b, m_w_in, m_conv_w, m_conv_b, m_rg_w_a, m_rg_b_a, m_rg_w_x, m_rg_b_x, m_rg_lambda, m_fox_b_f, m_s5_a_re, m_s5_a_im, m_s5_log_dt, m_s5_b_re, m_s5_b_im, m_s5_c_re, m_s5_c_im, m_s5_d, m_s5_w_glu, m_mix_norm_g, m_w_out, m_ln2_g, m_ln2_b, m_ffn2_w_gate, m_ffn2_w_up, m_ffn2_w_down, m_ln3_g, m_ln3_b, v_ffn1_w_gate, v_ffn1_w_up, v_ffn1_w_down, v_ln1_g, v_ln1_b, v_w_in, v_conv_w, v_conv_b, v_rg_w_a, v_rg_b_a, v_rg_w_x, v_rg_b_x, v_rg_lambda, v_fox_b_f, v_s5_a_re, v_s5_a_im, v_s5_log_dt, v_s5_b_re, v_s5_b_im, v_s5_c_re, v_s5_c_im, v_s5_d, v_s5_w_glu, v_mix_norm_g, v_w_out, v_ln2_g, v_ln2_b, v_ffn2_w_gate, v_ffn2_w_up, v_ffn2_w_down, v_ln3_g, v_ln3_b):
    given = dict(x=x, ffn1_w_gate=ffn1_w_gate, ffn1_w_up=ffn1_w_up, ffn1_w_down=ffn1_w_down, ln1_g=ln1_g, ln1_b=ln1_b, w_in=w_in, conv_w=conv_w, conv_b=conv_b, rg_w_a=rg_w_a, rg_b_a=rg_b_a, rg_w_x=rg_w_x, rg_b_x=rg_b_x, rg_lambda=rg_lambda, fox_b_f=fox_b_f, s5_a_re=s5_a_re, s5_a_im=s5_a_im, s5_log_dt=s5_log_dt, s5_b_re=s5_b_re, s5_b_im=s5_b_im, s5_c_re=s5_c_re, s5_c_im=s5_c_im, s5_d=s5_d, s5_w_glu=s5_w_glu, mix_norm_g=mix_norm_g, w_out=w_out, ln2_g=ln2_g, ln2_b=ln2_b, ffn2_w_gate=ffn2_w_gate, ffn2_w_up=ffn2_w_up, ffn2_w_down=ffn2_w_down, ln3_g=ln3_g, ln3_b=ln3_b, loss_target=loss_target, m_ffn1_w_gate=m_ffn1_w_gate, m_ffn1_w_up=m_ffn1_w_up, m_ffn1_w_down=m_ffn1_w_down, m_ln1_g=m_ln1_g, m_ln1_b=m_ln1_b, m_w_in=m_w_in, m_conv_w=m_conv_w, m_conv_b=m_conv_b, m_rg_w_a=m_rg_w_a, m_rg_b_a=m_rg_b_a, m_rg_w_x=m_rg_w_x, m_rg_b_x=m_rg_b_x, m_rg_lambda=m_rg_lambda, m_fox_b_f=m_fox_b_f, m_s5_a_re=m_s5_a_re, m_s5_a_im=m_s5_a_im, m_s5_log_dt=m_s5_log_dt, m_s5_b_re=m_s5_b_re, m_s5_b_im=m_s5_b_im, m_s5_c_re=m_s5_c_re, m_s5_c_im=m_s5_c_im, m_s5_d=m_s5_d, m_s5_w_glu=m_s5_w_glu, m_mix_norm_g=m_mix_norm_g, m_w_out=m_w_out, m_ln2_g=m_ln2_g, m_ln2_b=m_ln2_b, m_ffn2_w_gate=m_ffn2_w_gate, m_ffn2_w_up=m_ffn2_w_up, m_ffn2_w_down=m_ffn2_w_down, m_ln3_g=m_ln3_g, m_ln3_b=m_ln3_b, v_ffn1_w_gate=v_ffn1_w_gate, v_ffn1_w_up=v_ffn1_w_up, v_ffn1_w_down=v_ffn1_w_down, v_ln1_g=v_ln1_g, v_ln1_b=v_ln1_b, v_w_in=v_w_in, v_conv_w=v_conv_w, v_conv_b=v_conv_b, v_rg_w_a=v_rg_w_a, v_rg_b_a=v_rg_b_a, v_rg_w_x=v_rg_w_x, v_rg_b_x=v_rg_b_x, v_rg_lambda=v_rg_lambda, v_fox_b_f=v_fox_b_f, v_s5_a_re=v_s5_a_re, v_s5_a_im=v_s5_a_im, v_s5_log_dt=v_s5_log_dt, v_s5_b_re=v_s5_b_re, v_s5_b_im=v_s5_b_im, v_s5_c_re=v_s5_c_re, v_s5_c_im=v_s5_c_im, v_s5_d=v_s5_d, v_s5_w_glu=v_s5_w_glu, v_mix_norm_g=v_mix_norm_g, v_w_out=v_w_out, v_ln2_g=v_ln2_g, v_ln2_b=v_ln2_b, v_ffn2_w_gate=v_ffn2_w_gate, v_ffn2_w_up=v_ffn2_w_up, v_ffn2_w_down=v_ffn2_w_down, v_ln3_g=v_ln3_g, v_ln3_b=v_ln3_b)
    weights = {n: given[n] for n in TWIN_WEIGHTS}
    shared = {n: given[n] for n in SHARED_INPUTS}
    per_example = {n: given[n] for n in ['x']}
    grad_fn = _jax.value_and_grad(_loss, argnums=(0, 1))

    def one_microbatch(ex, loss_target):
        ex = dict(ex)
        diff = ex.pop(TWIN_DIFF_INPUT)
        return grad_fn(weights, diff, {**shared, **ex}, loss_target)

    if N_MICROBATCH == 1:
        loss, (grad_w, grad_x) = one_microbatch(per_example, given["loss_target"])
    else:
        def body(carry, xs):
            loss_sum, grad_sum = carry
            l_k, (gw_k, gx_k) = one_microbatch(xs[0], xs[1])
            with _jax.named_scope("update"):
                return (loss_sum + l_k, _jax.tree.map(_jnp.add, grad_sum, gw_k)), gx_k

        init = (_jnp.zeros((), _jnp.float32), _jax.tree.map(_jnp.zeros_like, weights))
        (loss, grad_w), grad_x = _jax.lax.scan(body, init, (per_example, given["loss_target"]))
    with _jax.named_scope("update"):
        delta_w, new_m, new_v = {}, {}, {}
        for n in TWIN_WEIGHTS:
            delta_w[n], new_m[n], new_v[n] = _adamw(weights[n], grad_w[n], given["m_" + n], given["v_" + n])
    return (loss, grad_x, *[grad_w[n] for n in TWIN_WEIGHTS], *[delta_w[n] for n in TWIN_WEIGHTS],
            *[new_m[n] for n in TWIN_WEIGHTS], *[new_v[n] for n in TWIN_WEIGHTS])
```

```python
import functools
import math

import jax
import jax.numpy as jnp
import numpy as np
from jax import lax
from jax.experimental import pallas as pl
from jax.experimental.pallas import tpu as pltpu

F32 = jnp.float32
BF16 = jnp.bfloat16
MESH = pl.DeviceIdType.MESH

DEPTH = 2
ALPHA = (2 * DEPTH) ** 0.25
LN_EPS = 1e-5
RMS_EPS = 1e-6
RG_C = 8.0
CONV_WIDTH = 4
HEAD_DIM = 64
C_GROUP = 16
C_STATE = 64
D_A = 384
D_B = 384
D_C = 256
N_HEADS = D_B // HEAD_DIM
N_GROUPS = D_C // C_GROUP
N_STATE = N_GROUPS * C_STATE
Z_F = 2 * D_A + 3 * D_B
Z_U = Z_F + 128
Z_W = Z_U + D_C
N_IN = Z_F + N_HEADS + D_C
ADAM_LR, ADAM_B1, ADAM_B2, ADAM_EPS, ADAM_WD, ADAM_STEP = 0.001, 0.9, 0.999, 1e-08, 0.01, 10
LANES = 128
NEG = -1e30

WEIGHTS = ['ffn1_w_gate', 'ffn1_w_up', 'ffn1_w_down', 'ln1_g', 'ln1_b', 'w_in', 'conv_w', 'conv_b', 'rg_w_a', 'rg_b_a',
           'rg_w_x', 'rg_b_x', 'rg_lambda', 'fox_b_f', 's5_a_re', 's5_a_im', 's5_log_dt', 's5_b_re', 's5_b_im', 's5_c_re',
           's5_c_im', 's5_d', 's5_w_glu', 'mix_norm_g', 'w_out', 'ln2_g', 'ln2_b', 'ffn2_w_gate', 'ffn2_w_up', 'ffn2_w_down',
           'ln3_g', 'ln3_b']
BIG = ['ffn1_w_gate', 'ffn1_w_up', 'ffn1_w_down', 'w_in', 's5_w_glu', 'w_out', 'ffn2_w_gate', 'ffn2_w_up', 'ffn2_w_down']
SMALL = [n for n in WEIGHTS if n not in BIG and n != 'conv_w']


def _sig(x):
    return 1.0 / (1.0 + jnp.exp(-x))


def _gelu(x):
    return 0.5 * x * (1.0 + jnp.tanh(math.sqrt(2.0 / math.pi) * (x + 0.044715 * (x * x * x))))


def _softplus(x):
    return jnp.maximum(x, 0.0) + jnp.log(1.0 + jnp.exp(jnp.minimum(x, -x)))


def _dot(a, b, dims):
    return lax.dot_general(a.astype(BF16), b.astype(BF16), (dims, ((), ())), preferred_element_type=F32)


NN = ((1,), (0,))
NT = ((1,), (1,))
TN = ((0,), (0,))


@jax.custom_vjp
def _bdot(a, w):
    return _dot(a, w, NN)


def _bdot_fwd(a, w):
    return _dot(a, w, NN), (a, w)


def _bdot_bwd(res, ct):
    a, w = res
    return _dot(ct, w, NT), _dot(a, ct, TN)


_bdot.defvjp(_bdot_fwd, _bdot_bwd)


def _ln(pre, g, b):
    mu = jnp.mean(pre, axis=-1, keepdims=True)
    xc = pre - mu
    var = jnp.mean(xc * xc, axis=-1, keepdims=True)
    return xc * lax.rsqrt(var + LN_EPS) * g + b


def _rms(x, g):
    return x * lax.rsqrt(jnp.mean(x * x, axis=-1, keepdims=True) + RMS_EPS) * g


def _tile(n, want):
    return want if n % want == 0 else n


def _mm(a, b, dims, out_dtype, tm, tn, tk, name, add=None):
    if dims == 'nn':
        (m, k), n = a.shape, b.shape[1]
        a_spec = pl.BlockSpec((tm, tk), lambda i, j, q: (i, q))
        b_spec = pl.BlockSpec((tk, tn), lambda i, j, q: (q, j))
        dn = NN
    elif dims == 'nt':
        (m, k), n = a.shape, b.shape[0]
        a_spec = pl.BlockSpec((tm, tk), lambda i, j, q: (i, q))
        b_spec = pl.BlockSpec((tn, tk), lambda i, j, q: (j, q))
        dn = NT
    else:
        (k, m), n = a.shape, b.shape[1]
        a_spec = pl.BlockSpec((tk, tm), lambda i, j, q: (q, i))
        b_spec = pl.BlockSpec((tk, tn), lambda i, j, q: (q, j))
        dn = TN
    nk = k // tk
    o_spec = pl.BlockSpec((tm, tn), lambda i, j, q: (i, j))

    def body(*refs):
        if add is None:
            a_ref, b_ref, o_ref, acc_ref = refs
        else:
            a_ref, b_ref, add_ref, o_ref, acc_ref = refs
        q = pl.program_id(2)
        part = _dot(a_ref[...], b_ref[...], dn)

        @pl.when(q == 0)
        def _():
            acc_ref[...] = part

        @pl.when(q > 0)
        def _():
            acc_ref[...] += part

        @pl.when(q == nk - 1)
        def _():
            r = acc_ref[...]
            if add is not None:
                r = r + add_ref[...]
            o_ref[...] = r.astype(o_ref.dtype)

    ins = [a, b] + ([] if add is None else [add])
    specs = [a_spec, b_spec] + ([] if add is None else [o_spec])
    return pl.pallas_call(
        body, grid=(m // tm, n // tn, nk), in_specs=specs, out_specs=o_spec,
        out_shape=jax.ShapeDtypeStruct((m, n), out_dtype), scratch_shapes=[pltpu.VMEM((tm, tn), F32)],
        compiler_params=pltpu.CompilerParams(dimension_semantics=("parallel", "parallel", "arbitrary")), name=name)(*ins)


def _rowwise(fn, rows, params, outs, tm, name):
    t = rows[0].shape[0]
    nr, npar = len(rows), len(params)

    def body(*refs):
        r = [x[...] for x in refs[:nr]]
        p = [x[...] for x in refs[nr:nr + npar]]
        res = fn(*r, *p)
        for o_ref, o in zip(refs[nr + npar:], res):
            o_ref[...] = o.astype(o_ref.dtype)

    in_specs = ([pl.BlockSpec((tm, a.shape[1]), lambda i: (i, 0)) for a in rows]
                + [pl.BlockSpec(p.shape, lambda i: (0, 0)) for p in params])
    return pl.pallas_call(
        body, grid=(t // tm,), in_specs=in_specs,
        out_specs=[pl.BlockSpec((tm, c), lambda i: (i, 0)) for c, _ in outs],
        out_shape=[jax.ShapeDtypeStruct((t, c), d) for c, d in outs],
        compiler_params=pltpu.CompilerParams(dimension_semantics=("parallel",)), name=name)(*rows, *params)


def _rowwise_vjp(fn, rows, params, cots, tm, name):
    t = rows[0].shape[0]
    nr, npar, nc = len(rows), len(params), len(cots)

    def body(*refs):
        r = [x[...] for x in refs[:nr]]
        p = [x[...] for x in refs[nr:nr + npar]]
        c = [x[...] for x in refs[nr + npar:nr + npar + nc]]
        o_refs = refs[nr + npar + nc:]
        _, pull = jax.vjp(fn, *r, *p)
        grads = pull(tuple(c))
        for o_ref, g in zip(o_refs[:nr], grads[:nr]):
            o_ref[...] = g
        i = pl.program_id(0)

        @pl.when(i == 0)
        def _():
            for o_ref, g in zip(o_refs[nr:], grads[nr:]):
                o_ref[...] = g

        @pl.when(i > 0)
        def _():
            for o_ref, g in zip(o_refs[nr:], grads[nr:]):
                o_ref[...] += g

    row_spec = lambda a: pl.BlockSpec((tm, a.shape[1]), lambda i: (i, 0))
    par_spec = lambda p: pl.BlockSpec(p.shape, lambda i: (0, 0))
    res = pl.pallas_call(
        body, grid=(t // tm,),
        in_specs=[row_spec(a) for a in rows] + [par_spec(p) for p in params] + [row_spec(a) for a in cots],
        out_specs=[row_spec(a) for a in rows] + [par_spec(p) for p in params],
        out_shape=[jax.ShapeDtypeStruct(a.shape, F32) for a in rows] + [jax.ShapeDtypeStruct(p.shape, F32) for p in params],
        compiler_params=pltpu.CompilerParams(dimension_semantics=("arbitrary",)), name=name)(*rows, *params, *cots)
    return res[:nr], res[nr:]


def _ffn_fwd(x, wgt, wut, wd, ln_g, ln_b, tm, tf):
    t, d = x.shape
    f = wgt.shape[0]
    nj = f // tf

    def body(x_ref, wg_ref, wu_ref, wd_ref, g_ref, b_ref, y_ref, pre_ref, gs_ref, us_ref, acc_ref):
        j = pl.program_id(1)
        xv = x_ref[...]
        xb = xv.astype(BF16)
        g = _dot(xb, wg_ref[...], NT)
        u = _dot(xb, wu_ref[...], NT)
        gs_ref[...] = g.astype(BF16)
        us_ref[...] = u.astype(BF16)
        part = _dot(g * _sig(g) * u, wd_ref[...], NN)

        @pl.when(j == 0)
        def _():
            acc_ref[...] = part

        @pl.when(j > 0)
        def _():
            acc_ref[...] += part

        @pl.when(j == nj - 1)
        def _():
            pre = ALPHA * xv + 0.5 * acc_ref[...]
            pre_ref[...] = pre
            y_ref[...] = _ln(pre, g_ref[...], b_ref[...])

    w_spec = pl.BlockSpec((tf, d), lambda i, j: (j, 0))
    x_spec = pl.BlockSpec((tm, d), lambda i, j: (i, 0))
    v_spec = pl.BlockSpec((1, d), lambda i, j: (0, 0))
    h_spec = pl.BlockSpec((tm, tf), lambda i, j: (i, j))
    return pl.pallas_call(
        body, grid=(t // tm, nj), in_specs=[x_spec, w_spec, w_spec, w_spec, v_spec, v_spec],
        out_specs=[x_spec, x_spec, h_spec, h_spec],
        out_shape=[jax.ShapeDtypeStruct((t, d), F32), jax.ShapeDtypeStruct((t, d), F32),
                   jax.ShapeDtypeStruct((t, f), BF16), jax.ShapeDtypeStruct((t, f), BF16)],
        scratch_shapes=[pltpu.VMEM((tm, d), F32)],
        compiler_params=pltpu.CompilerParams(dimension_semantics=("parallel", "arbitrary")), name="ffn_fwd")(
            x, wgt, wut, wd, ln_g, ln_b)


def _ffn_bwd(dpre, gs, us, wgt, wut, wd, tm, tf):
    t, d = dpre.shape
    f = wgt.shape[0]
    nj = f // tf

    def body(dp_ref, gs_ref, us_ref, wg_ref, wu_ref, wd_ref, dx_ref, dg_ref, du_ref, hh_ref, acc_ref):
        j = pl.program_id(1)
        dp = dp_ref[...]
        dh = _dot(0.5 * dp, wd_ref[...], NT)
        g = gs_ref[...].astype(F32)
        u = us_ref[...].astype(F32)
        s = _sig(g)
        sl = g * s
        dg = (dh * u * (s * (1.0 + g * (1.0 - s)))).astype(BF16)
        du = (dh * sl).astype(BF16)
        dg_ref[...] = dg
        du_ref[...] = du
        hh_ref[...] = (0.5 * sl * u).astype(BF16)
        part = _dot(dg, wg_ref[...], NN) + _dot(du, wu_ref[...], NN)

        @pl.when(j == 0)
        def _():
            acc_ref[...] = part

        @pl.when(j > 0)
        def _():
            acc_ref[...] += part

        @pl.when(j == nj - 1)
        def _():
            dx_ref[...] = ALPHA * dp + acc_ref[...]

    w_spec = pl.BlockSpec((tf, d), lambda i, j: (j, 0))
    x_spec = pl.BlockSpec((tm, d), lambda i, j: (i, 0))
    h_spec = pl.BlockSpec((tm, tf), lambda i, j: (i, j))
    return pl.pallas_call(
        body, grid=(t // tm, nj), in_specs=[x_spec, h_spec, h_spec, w_spec, w_spec, w_spec],
        out_specs=[x_spec, h_spec, h_spec, h_spec],
        out_shape=[jax.ShapeDtypeStruct((t, d), F32)] + [jax.ShapeDtypeStruct((t, f), BF16)] * 3,
        scratch_shapes=[pltpu.VMEM((tm, d), F32)],
        compiler_params=pltpu.CompilerParams(dimension_semantics=("parallel", "arbitrary")), name="ffn_bwd")(
            dpre, gs, us, wgt, wut, wd)


def _rg_local(xa, wa, ba, wx, bx, lam):
    r = _sig(_bdot(xa, wa) + ba)
    i = _sig(_bdot(xa, wx) + bx)
    log_a = -RG_C * r * _softplus(-lam)
    a = jnp.exp(log_a)
    mult = jnp.sqrt(-jnp.tanh(log_a) * (a * a + 1.0))
    return a, mult * (i * xa)


def _conv_taps(ext, n):
    return [ext[8:, :]] + [pltpu.roll(ext, s, 0)[8:, :] for s in (1, 2, 3)]


def _rg_fwd(z, cw, cb, wa, ba, wx, bx, lam):
    t = z.shape[0]
    cr = _tile(t, 256)
    nb = D_A // LANES

    def body(ax_ref, ag_ref, cw_ref, cb_ref, wa_ref, ba_ref, wx_ref, bx_ref, lam_ref, out_ref, h_ref, axp, a_s, b_s):
        axp[pl.ds(0, 8), :] = jnp.zeros((8, LANES), F32)
        pltpu.sync_copy(ax_ref, axp.at[pl.ds(8, t)])
        w = [cw_ref[pl.ds(k, 1), :] for k in range(CONV_WIDTH)]

        def chunk(c, carry):
            t0 = pl.multiple_of(c * cr, cr)
            taps = _conv_taps(axp[pl.ds(t0, cr + 8), :], cr)
            xa = cb_ref[...] + w[3] * taps[0] + w[2] * taps[1] + w[1] * taps[2] + w[0] * taps[3]
            a, gated = _rg_local(xa, wa_ref[...], ba_ref[...], wx_ref[...], bx_ref[...], lam_ref[...])
            a_s[pl.ds(t0, cr), :] = a
            b_s[pl.ds(t0, cr), :] = gated
            return carry

        lax.fori_loop(0, t // cr, chunk, 0)

        def step(i, h):
            h = a_s[pl.ds(i, 1), :] * h + b_s[pl.ds(i, 1), :]
            h_ref[pl.ds(i, 1), :] = h
            return h

        lax.fori_loop(0, t, step, jnp.zeros((1, LANES), F32), unroll=8)

        def fin(c, carry):
            t0 = pl.multiple_of(c * cr, cr)
            out_ref[pl.ds(t0, cr), :] = _gelu(ag_ref[pl.ds(t0, cr), :]) * h_ref[pl.ds(t0, cr), :]
            return carry

        lax.fori_loop(0, t // cr, fin, 0)

    col = lambda off: pl.BlockSpec((t, LANES), lambda b: (0, off + b))
    vec = pl.BlockSpec((1, LANES), lambda b: (0, b))
    mat = pl.BlockSpec((None, LANES, LANES), lambda b: (b, 0, 0))
    return pl.pallas_call(
        body, grid=(nb,),
        in_specs=[col(0), col(nb), pl.BlockSpec((CONV_WIDTH, LANES), lambda b: (0, b)), vec, mat, vec, mat, vec, vec],
        out_specs=[col(0), col(0)],
        out_shape=[jax.ShapeDtypeStruct((t, D_A), F32), jax.ShapeDtypeStruct((t, D_A), F32)],
        scratch_shapes=[pltpu.VMEM((t + 8, LANES), F32), pltpu.VMEM((t, LANES), F32), pltpu.VMEM((t, LANES), F32)],
        compiler_params=pltpu.CompilerParams(dimension_semantics=("arbitrary",)), name="rglru_fwd")(
            z, z, cw, cb, wa, ba, wx, bx, lam)


def _rg_bwd(z, h, dout, cw, cb, wa, ba, wx, bx, lam):
    t = z.shape[0]
    cr = _tile(t, 256)
    nb = D_A // LANES

    def body(ax_ref, ag_ref, h_ref, do_ref, cw_ref, cb_ref, wa_ref, ba_ref, wx_ref, bx_ref, lam_ref,
             dax_ref, dag_ref, dcw_ref, dcb_ref, dwa_ref, dba_ref, dwx_ref, dbx_ref, dlam_ref,
             axp, hp, xa_s, a_s, g_s, dxa_s):
        zero8 = jnp.zeros((8, LANES), F32)
        axp[pl.ds(0, 8), :] = zero8
        hp[pl.ds(0, 8), :] = zero8
        dxa_s[pl.ds(t, 8), :] = zero8
        pltpu.sync_copy(ax_ref, axp.at[pl.ds(8, t)])
        pltpu.sync_copy(h_ref, hp.at[pl.ds(8, t)])
        w = [cw_ref[pl.ds(k, 1), :] for k in range(CONV_WIDTH)]
        for ref in (dcw_ref, dcb_ref, dwa_ref, dba_ref, dwx_ref, dbx_ref, dlam_ref):
            ref[...] = jnp.zeros(ref.shape, F32)

        def p1(c, carry):
            t0 = pl.multiple_of(c * cr, cr)
            taps = _conv_taps(axp[pl.ds(t0, cr + 8), :], cr)
            xa = cb_ref[...] + w[3] * taps[0] + w[2] * taps[1] + w[1] * taps[2] + w[0] * taps[3]
            a, _ = _rg_local(xa, wa_ref[...], ba_ref[...], wx_ref[...], bx_ref[...], lam_ref[...])
            xa_s[pl.ds(t0, cr), :] = xa
            a_s[pl.ds(t0, cr), :] = a
            ag = ag_ref[pl.ds(t0, cr), :]
            dov = do_ref[pl.ds(t0, cr), :]
            gel, pull = jax.vjp(_gelu, ag)
            g_s[pl.ds(t0, cr), :] = dov * gel
            dag_ref[pl.ds(t0, cr), :] = pull(dov * h_ref[pl.ds(t0, cr), :])[0]
            return carry

        lax.fori_loop(0, t // cr, p1, 0)

        def step(k, carry):
            i = t - 1 - k
            g = g_s[pl.ds(i, 1), :] + carry
            g_s[pl.ds(i, 1), :] = g
            return g * a_s[pl.ds(i, 1), :]

        lax.fori_loop(0, t, step, jnp.zeros((1, LANES), F32), unroll=8)

        def p3(c, carry):
            t0 = pl.multiple_of(c * cr, cr)
            g = g_s[pl.ds(t0, cr), :]
            h_prev = pltpu.roll(hp[pl.ds(t0, cr + 8), :], 1, 0)[8:, :]
            _, pull = jax.vjp(_rg_local, xa_s[pl.ds(t0, cr), :], wa_ref[...], ba_ref[...], wx_ref[...], bx_ref[...],
                              lam_ref[...])
            dxa, dwa, dba, dwx, dbx, dlam = pull((g * h_prev, g))
            dxa_s[pl.ds(t0, cr), :] = dxa
            dwa_ref[...] += dwa
            dba_ref[...] += dba
            dwx_ref[...] += dwx
            dbx_ref[...] += dbx
            dlam_ref[...] += dlam
            return carry

        lax.fori_loop(0, t // cr, p3, 0)

        def p4(c, carry):
            t0 = pl.multiple_of(c * cr, cr)
            ext = dxa_s[pl.ds(t0, cr + 8), :]
            n = cr + 8
            ahead = [ext[:cr, :]] + [pltpu.roll(ext, n - s, 0)[:cr, :] for s in (1, 2, 3)]
            dax_ref[pl.ds(t0, cr), :] = w[3] * ahead[0] + w[2] * ahead[1] + w[1] * ahead[2] + w[0] * ahead[3]
            taps = _conv_taps(axp[pl.ds(t0, cr + 8), :], cr)
            dxa = ahead[0]
            for k in range(CONV_WIDTH):
                dcw_ref[pl.ds(k, 1), :] += jnp.sum(dxa * taps[CONV_WIDTH - 1 - k], axis=0, keepdims=True)
            dcb_ref[...] += jnp.sum(dxa, axis=0, keepdims=True)
            return carry

        lax.fori_loop(0, t // cr, p4, 0)

    col = lambda off: pl.BlockSpec((t, LANES), lambda b: (0, off + b))
    vec = pl.BlockSpec((1, LANES), lambda b: (0, b))
    mat = pl.BlockSpec((None, LANES, LANES), lambda b: (b, 0, 0))
    cws = pl.BlockSpec((CONV_WIDTH, LANES), lambda b: (0, b))
    sds = jax.ShapeDtypeStruct
    return pl.pallas_call(
        body, grid=(nb,),
        in_specs=[col(0), col(nb), col(0), col(0), cws, vec, mat, vec, mat, vec, vec],
        out_specs=[col(0), col(0), cws, vec, mat, vec, mat, vec, vec],
        out_shape=[sds((t, D_A), F32), sds((t, D_A), F32), sds((CONV_WIDTH, D_A), F32), sds((1, D_A), F32),
                   sds((nb, LANES, LANES), F32), sds((1, D_A), F32), sds((nb, LANES, LANES), F32), sds((1, D_A), F32),
                   sds((1, D_A), F32)],
        scratch_shapes=[pltpu.VMEM((t + 8, LANES), F32), pltpu.VMEM((t + 8, LANES), F32), pltpu.VMEM((t, LANES), F32),
                        pltpu.VMEM((t, LANES), F32), pltpu.VMEM((t, LANES), F32), pltpu.VMEM((t + 8, LANES), F32)],
        compiler_params=pltpu.CompilerParams(dimension_semantics=("arbitrary",)), name="rglru_bwd")(
            z, z, h, dout, cw, cb, wa, ba, wx, bx, lam)


def _fgate_fwd(z, bf):
    t = z.shape[0]

    def body(zf_ref, bf_ref, c_ref):
        c_ref[...] = -_softplus(-(zf_ref[...] + bf_ref[...]))

        def step(i, c):
            c = c + c_ref[pl.ds(i, 1), :]
            c_ref[pl.ds(i, 1), :] = c
            return c

        lax.fori_loop(0, t, step, jnp.zeros((1, LANES), F32), unroll=8)

    return pl.pallas_call(
        body, grid=(1,), in_specs=[pl.BlockSpec((t, LANES), lambda i: (0, Z_F // LANES)), pl.BlockSpec((1, LANES), lambda i: (0, 0))],
        out_specs=pl.BlockSpec((t, LANES), lambda i: (0, 0)), out_shape=jax.ShapeDtypeStruct((t, LANES), F32),
        compiler_params=pltpu.CompilerParams(dimension_semantics=("arbitrary",)), name="fgate_fwd")(z, bf)


def _fgate_bwd(z, bf, dc):
    t = z.shape[0]

    def body(zf_ref, bf_ref, dc_ref, dz_ref, db_ref):
        def step(k, carry):
            i = t - 1 - k
            carry = carry + dc_ref[pl.ds(i, 1), :]
            dz_ref[pl.ds(i, 1), :] = carry
            return carry

        lax.fori_loop(0, t, step, jnp.zeros((1, LANES), F32), unroll=8)
        dz = dz_ref[...] * _sig(-(zf_ref[...] + bf_ref[...]))
        dz_ref[...] = dz
        db_ref[...] = jnp.sum(dz, axis=0, keepdims=True)

    return pl.pallas_call(
        body, grid=(1,),
        in_specs=[pl.BlockSpec((t, LANES), lambda i: (0, Z_F // LANES)), pl.BlockSpec((1, LANES), lambda i: (0, 0)),
                  pl.BlockSpec((t, LANES), lambda i: (0, 0))],
        out_specs=[pl.BlockSpec((t, LANES), lambda i: (0, 0)), pl.BlockSpec((1, LANES), lambda i: (0, 0))],
        out_shape=[jax.ShapeDtypeStruct((t, LANES), F32), jax.ShapeDtypeStruct((1, LANES), F32)],
        compiler_params=pltpu.CompilerParams(dimension_semantics=("arbitrary",)), name="fgate_bwd")(z, bf, dc)


def _attn_fwd(z, ccol, crow):
    t = z.shape[0]
    tq = _tile(t, 256)
    nq = t // tq
    scale = HEAD_DIM ** -0.5

    def body(q_ref, k_ref, v_ref, cc_ref, cr_ref, o_ref, lse_ref):
        lane = lax.broadcasted_iota(jnp.int32, (1, LANES), 1)
        rows = lax.broadcasted_iota(jnp.int32, (tq, tq), 0)
        cols = lax.broadcasted_iota(jnp.int32, (tq, tq), 1)

        def qblock(i, carry):
            q0 = pl.multiple_of(i * tq, tq)
            qv = q_ref[pl.ds(q0, tq), :] * scale
            out = jnp.zeros((tq, LANES), F32)
            for hh in range(2):
                mine = (lane // HEAD_DIM) == hh
                qa = jnp.where(mine, qv, 0.0).astype(BF16)
                ci = cc_ref[hh, pl.ds(q0, tq), :]

                def kblock(j, st):
                    m, l, acc = st
                    k0 = pl.multiple_of(j * tq, tq)
                    s = _dot(qa, k_ref[pl.ds(k0, tq), :], NT) + (ci - cr_ref[hh, :, pl.ds(k0, tq)])
                    s = jnp.where(rows + (q0 - k0) >= cols, s, NEG)
                    m_new = jnp.maximum(m, jnp.max(s, axis=-1, keepdims=True))
                    p = jnp.exp(s - m_new)
                    corr = jnp.exp(m - m_new)
                    l = corr * l + jnp.sum(p, axis=-1, keepdims=True)
                    acc = corr * acc + _dot(p, v_ref[pl.ds(k0, tq), :], NN)
                    return m_new, l, acc

                m, l, acc = lax.fori_loop(0, i + 1, kblock, (jnp.full((tq, 1), NEG, F32), jnp.zeros((tq, 1), F32),
                                                             jnp.zeros((tq, LANES), F32)))
                out = jnp.where(mine, acc / l, out)
                lse_ref[hh, pl.ds(q0, tq), :] = m + jnp.log(l)
            o_ref[pl.ds(q0, tq), :] = out
            return carry

        lax.fori_loop(0, nq, qblock, 0)

    base = 2 * D_A // LANES
    nh = D_B // LANES
    col = lambda off: pl.BlockSpec((t, LANES), lambda p: (0, off + p))
    return pl.pallas_call(
        body, grid=(nh,),
        in_specs=[col(base), col(base + nh), col(base + 2 * nh), pl.BlockSpec((2, t, 1), lambda p: (p, 0, 0)),
                  pl.BlockSpec((2, 1, t), lambda p: (p, 0, 0))],
        out_specs=[col(0), pl.BlockSpec((2, t, 1), lambda p: (p, 0, 0))],
        out_shape=[jax.ShapeDtypeStruct((t, D_B), F32), jax.ShapeDtypeStruct((N_HEADS, t, 1), F32)],
        compiler_params=pltpu.CompilerParams(dimension_semantics=("parallel",)), name="attn_fwd")(z, z, z, ccol, crow)


def _attn_bwd(z, ccol, crow, lse, do):
    t = z.shape[0]
    tq = _tile(t, 256)
    nq = t // tq
    scale = HEAD_DIM ** -0.5

    def body(q_ref, k_ref, v_ref, cc_ref, cr_ref, lse_ref, do_ref, dq_ref, dk_ref, dv_ref, dc_ref, dl_s):
        lane = lax.broadcasted_iota(jnp.int32, (1, LANES), 1)
        rows = lax.broadcasted_iota(jnp.int32, (tq, tq), 0)
        cols = lax.broadcasted_iota(jnp.int32, (tq, tq), 1)
        dq_ref[...] = jnp.zeros((t, LANES), F32)

        def pre(i, carry):
            q0 = pl.multiple_of(i * tq, tq)
            dl = []
            for hh in range(2):
                mine = (lane // HEAD_DIM) == hh
                qa = jnp.where(mine, q_ref[pl.ds(q0, tq), :] * scale, 0.0).astype(BF16)
                doa = jnp.where(mine, do_ref[pl.ds(q0, tq), :], 0.0).astype(BF16)
                ci = cc_ref[hh, pl.ds(q0, tq), :]
                lse_i = lse_ref[hh, pl.ds(q0, tq), :]

                def kb(j, acc):
                    k0 = pl.multiple_of(j * tq, tq)
                    s = _dot(qa, k_ref[pl.ds(k0, tq), :], NT) + (ci - cr_ref[hh, :, pl.ds(k0, tq)])
                    p = jnp.where(rows + (q0 - k0) >= cols, jnp.exp(s - lse_i), 0.0)
                    return acc + jnp.sum(p * _dot(doa, v_ref[pl.ds(k0, tq), :], NT), axis=-1, keepdims=True)

                dl.append(lax.fori_loop(0, i + 1, kb, jnp.zeros((tq, 1), F32)))
            dl_s[pl.ds(q0, tq), :] = jnp.where(lane == 0, dl[0], jnp.where(lane == 1, dl[1], 0.0))
            return carry

        lax.fori_loop(0, nq, pre, 0)

        def kblock(j, carry):
            k0 = pl.multiple_of(j * tq, tq)
            kv = k_ref[pl.ds(k0, tq), :]
            vv = v_ref[pl.ds(k0, tq), :]
            dk = jnp.zeros((tq, LANES), F32)
            dv = jnp.zeros((tq, LANES), F32)
            for hh in range(2):
                mine = (lane // HEAD_DIM) == hh
                cj = cr_ref[hh, :, pl.ds(k0, tq)]

                def qblock(i, st):
                    dk_h, dv_h, dcj = st
                    q0 = pl.multiple_of(i * tq, tq)
                    qa = jnp.where(mine, q_ref[pl.ds(q0, tq), :] * scale, 0.0).astype(BF16)
                    doa = jnp.where(mine, do_ref[pl.ds(q0, tq), :], 0.0).astype(BF16)
                    s = _dot(qa, kv, NT) + (cc_ref[hh, pl.ds(q0, tq), :] - cj)
                    p = jnp.where(rows + (q0 - k0) >= cols, jnp.exp(s - lse_ref[hh, pl.ds(q0, tq), :]), 0.0)
                    dlt = jnp.sum(jnp.where(lane == hh, dl_s[pl.ds(q0, tq), :], 0.0), axis=-1, keepdims=True)
                    ds = p * (_dot(doa, vv, NT) - dlt)
                    dv_h = dv_h + _dot(p, doa, TN)
                    dk_h = dk_h + _dot(ds, qa, TN)
                    dq_ref[pl.ds(q0, tq), :] += jnp.where(mine, _dot(ds, kv, NN) * scale, 0.0)
                    return dk_h, dv_h, dcj - jnp.sum(ds, axis=0, keepdims=True)

                zero = jnp.zeros((tq, LANES), F32)
                dk_h, dv_h, dcj = lax.fori_loop(j, nq, qblock, (zero, zero, jnp.zeros((1, tq), F32)))
                dk = jnp.where(mine, dk_h, dk)
                dv = jnp.where(mine, dv_h, dv)
                dc_ref[hh, :, pl.ds(k0, tq)] = dcj
            dk_ref[pl.ds(k0, tq), :] = dk
            dv_ref[pl.ds(k0, tq), :] = dv
            return carry

        lax.fori_loop(0, nq, kblock, 0)

    base = 2 * D_A // LANES
    nh = D_B // LANES
    col = lambda off: pl.BlockSpec((t, LANES), lambda p: (0, off + p))
    ccs = pl.BlockSpec((2, t, 1), lambda p: (p, 0, 0))
    crs = pl.BlockSpec((2, 1, t), lambda p: (p, 0, 0))
    return pl.pallas_call(
        body, grid=(nh,), in_specs=[col(base), col(base + nh), col(base + 2 * nh), ccs, crs, ccs, col(0)],
        out_specs=[col(0), col(0), col(0), crs],
        out_shape=[jax.ShapeDtypeStruct((t, D_B), F32)] * 3 + [jax.ShapeDtypeStruct((N_HEADS, 1, t), F32)],
        scratch_shapes=[pltpu.VMEM((t, LANES), F32)],
        compiler_params=pltpu.CompilerParams(dimension_semantics=("parallel",)), name="attn_bwd")(
            z, z, z, ccol, crow, lse, do)


def _s5_disc(a_re, a_im, log_dt, b_re, b_im):
    dt = jnp.exp(log_dt)
    mag = jnp.exp(a_re * dt)
    ar = mag * jnp.cos(a_im * dt)
    ai = mag * jnp.sin(a_im * dt)
    den = a_re * a_re + a_im * a_im
    kr = ((ar - 1.0) * a_re + ai * a_im) / den
    ki = (ai * a_re - (ar - 1.0) * a_im) / den
    kr3, ki3 = kr[:, None, :], ki[:, None, :]
    return ar, ai, kr3 * b_re - ki3 * b_im, kr3 * b_im + ki3 * b_re


def _s5_prep(a_re, a_im, log_dt, b_re, b_im):
    g, p = a_re.shape
    gc = b_re.shape[1]

    def body(*refs):
        res = _s5_disc(*[r[...] for r in refs[:5]])
        for o_ref, v in zip(refs[5:], res):
            o_ref[...] = v

    sds = jax.ShapeDtypeStruct
    return pl.pallas_call(body, out_shape=[sds((g, p), F32), sds((g, p), F32), sds((g, gc, p), F32), sds((g, gc, p), F32)],
                          name="s5_prep")(a_re, a_im, log_dt, b_re, b_im)


def _s5_prep_bwd(a_re, a_im, log_dt, b_re, b_im, d_ar, d_ai, d_br, d_bi):
    ins = (a_re, a_im, log_dt, b_re, b_im)

    def body(*refs):
        vals = [r[...] for r in refs[:5]]
        cts = tuple(r[...] for r in refs[5:9])
        _, pull = jax.vjp(_s5_disc, *vals)
        for o_ref, v in zip(refs[9:], pull(cts)):
            o_ref[...] = v

    return pl.pallas_call(body, out_shape=[jax.ShapeDtypeStruct(a.shape, F32) for a in ins], name="s5_prep_bwd")(
        *ins, d_ar, d_ai, d_br, d_bi)


def _s5_scan_rows(t, ar, ai, hr_s, hi_s, off, reverse):
    n = ar.shape[1]

    def step(k, carry):
        cr, ci = carry
        i = off + (t - 1 - k if reverse else k)
        if reverse:
            nr = ar * cr + ai * ci + hr_s[pl.ds(i, 1), :]
            ni = ar * ci - ai * cr + hi_s[pl.ds(i, 1), :]
        else:
            nr = ar * cr - ai * ci + hr_s[pl.ds(i, 1), :]
            ni = ar * ci + ai * cr + hi_s[pl.ds(i, 1), :]
        hr_s[pl.ds(i, 1), :] = nr
        hi_s[pl.ds(i, 1), :] = ni
        return nr, ni

    zero = jnp.zeros((1, n), F32)
    lax.fori_loop(0, t, step, (zero, zero), unroll=4)


def _s5_fwd(z, bd_re, bd_im, ab_re, ab_im, cd_re, cd_im, dvec):
    t = z.shape[0]
    cr = _tile(t, 256)
    ns = N_STATE // 2

    def body(u_ref, br_ref, bi_ref, ar_ref, ai_ref, cre_ref, cim_ref, d_ref, y_ref, hr_s, hi_s):
        def p1(c, carry):
            t0 = pl.multiple_of(c * cr, cr)
            u = u_ref[pl.ds(t0, cr), :]
            hr_s[pl.ds(t0, cr), :] = _dot(u, br_ref[...], NN)
            hi_s[pl.ds(t0, cr), :] = _dot(u, bi_ref[...], NN)
            return carry

        lax.fori_loop(0, t // cr, p1, 0)
        _s5_scan_rows(t, ar_ref[...], ai_ref[...], hr_s, hi_s, 0, False)

        def p3(c, carry):
            t0 = pl.multiple_of(c * cr, cr)
            y_ref[pl.ds(t0, cr), :] = (_dot(hr_s[pl.ds(t0, cr), :], cre_ref[...], NN)
                                       - _dot(hi_s[pl.ds(t0, cr), :], cim_ref[...], NN)
                                       + d_ref[...] * u_ref[pl.ds(t0, cr), :])
            return carry

        lax.fori_loop(0, t // cr, p3, 0)

    blk = lambda r, c: pl.BlockSpec((None, r, c), lambda b: (b, 0, 0))
    return pl.pallas_call(
        body, grid=(2,),
        in_specs=[pl.BlockSpec((t, LANES), lambda b: (0, Z_U // LANES + b)), blk(LANES, ns), blk(LANES, ns), blk(1, ns),
                  blk(1, ns), blk(ns, LANES), blk(ns, LANES), pl.BlockSpec((1, LANES), lambda b: (0, b))],
        out_specs=pl.BlockSpec((t, LANES), lambda b: (0, b)), out_shape=jax.ShapeDtypeStruct((t, D_C), F32),
        scratch_shapes=[pltpu.VMEM((t, ns), F32), pltpu.VMEM((t, ns), F32)],
        compiler_params=pltpu.CompilerParams(dimension_semantics=("arbitrary",)), name="s5_fwd")(
            z, bd_re, bd_im, ab_re, ab_im, cd_re, cd_im, dvec)


def _s5_bwd(z, dy, bd_re, bd_im, ab_re, ab_im, cd_re, cd_im, dvec):
    t = z.shape[0]
    cr = _tile(t, 256)
    ns = N_STATE // 2

    def body(u_ref, dy_ref, br_ref, bi_ref, ar_ref, ai_ref, cre_ref, cim_ref, d_ref,
             du_ref, dbr_ref, dbi_ref, dar_ref, dai_ref, dcre_ref, dcim_ref, dd_ref, hr_s, hi_s, gr_s, gi_s):
        zero8 = jnp.zeros((8, ns), F32)
        hr_s[pl.ds(0, 8), :] = zero8
        hi_s[pl.ds(0, 8), :] = zero8
        for ref in (dbr_ref, dbi_ref, dar_ref, dai_ref, dcre_ref, dcim_ref, dd_ref):
            ref[...] = jnp.zeros(ref.shape, F32)

        def p1(c, carry):
            t0 = pl.multiple_of(c * cr, cr)
            u = u_ref[pl.ds(t0, cr), :]
            hr_s[pl.ds(t0 + 8, cr), :] = _dot(u, br_ref[...], NN)
            hi_s[pl.ds(t0 + 8, cr), :] = _dot(u, bi_ref[...], NN)
            return carry

        lax.fori_loop(0, t // cr, p1, 0)
        _s5_scan_rows(t, ar_ref[...], ai_ref[...], hr_s, hi_s, 8, False)

        def p3(c, carry):
            t0 = pl.multiple_of(c * cr, cr)
            dyv = dy_ref[pl.ds(t0, cr), :]
            u = u_ref[pl.ds(t0, cr), :]
            gr_s[pl.ds(t0, cr), :] = _dot(dyv, cre_ref[...], NT)
            gi_s[pl.ds(t0, cr), :] = -_dot(dyv, cim_ref[...], NT)
            dcre_ref[...] += _dot(hr_s[pl.ds(t0 + 8, cr), :], dyv, TN)
            dcim_ref[...] -= _dot(hi_s[pl.ds(t0 + 8, cr), :], dyv, TN)
            dd_ref[...] += jnp.sum(dyv * u, axis=0, keepdims=True)
            du_ref[pl.ds(t0, cr), :] = dyv * d_ref[...]
            return carry

        lax.fori_loop(0, t // cr, p3, 0)
        _s5_scan_rows(t, ar_ref[...], ai_ref[...], gr_s, gi_s, 0, True)

        def p5(c, carry):
            t0 = pl.multiple_of(c * cr, cr)
            u = u_ref[pl.ds(t0, cr), :]
            gr = gr_s[pl.ds(t0, cr), :]
            gi = gi_s[pl.ds(t0, cr), :]
            dbr_ref[...] += _dot(u, gr, TN)
            dbi_ref[...] += _dot(u, gi, TN)
            du_ref[pl.ds(t0, cr), :] += _dot(gr, br_ref[...], NT) + _dot(gi, bi_ref[...], NT)
            hpr = pltpu.roll(hr_s[pl.ds(t0, cr + 8), :], 1, 0)[8:, :]
            hpi = pltpu.roll(hi_s[pl.ds(t0, cr + 8), :], 1, 0)[8:, :]
            dar_ref[...] += jnp.sum(gr * hpr + gi * hpi, axis=0, keepdims=True)
            dai_ref[...] += jnp.sum(gi * hpr - gr * hpi, axis=0, keepdims=True)
            return carry

        lax.fori_loop(0, t // cr, p5, 0)

    blk = lambda r, c: pl.BlockSpec((None, r, c), lambda b: (b, 0, 0))
    ucol = pl.BlockSpec((t, LANES), lambda b: (0, Z_U // LANES + b))
    ycol = pl.BlockSpec((t, LANES), lambda b: (0, b))
    dsp = pl.BlockSpec((1, LANES), lambda b: (0, b))
    sds = jax.ShapeDtypeStruct
    return pl.pallas_call(
        body, grid=(2,),
        in_specs=[ucol, ycol, blk(LANES, ns), blk(LANES, ns), blk(1, ns), blk(1, ns), blk(ns, LANES), blk(ns, LANES), dsp],
        out_specs=[ycol, blk(LANES, ns), blk(LANES, ns), blk(1, ns), blk(1, ns), blk(ns, LANES), blk(ns, LANES), dsp],
        out_shape=[sds((t, D_C), F32), sds((2, LANES, ns), F32), sds((2, LANES, ns), F32), sds((2, 1, ns), F32),
                   sds((2, 1, ns), F32), sds((2, ns, LANES), F32), sds((2, ns, LANES), F32), sds((1, D_C), F32)],
        scratch_shapes=[pltpu.VMEM((t + 8, ns), F32), pltpu.VMEM((t + 8, ns), F32), pltpu.VMEM((t, ns), F32),
                        pltpu.VMEM((t, ns), F32)],
        compiler_params=pltpu.CompilerParams(dimension_semantics=("arbitrary",)), name="s5_bwd")(
            z, dy, bd_re, bd_im, ab_re, ab_im, cd_re, cd_im, dvec)


def _mix_out(out_a, out_b, yc, x1, ga, gb, gc, wglu, wout, ln_g, ln_b):
    yg = _gelu(yc)
    out_c = yg * _sig(_bdot(yg, wglu))
    o = jnp.concatenate([_rms(out_a, ga), _rms(out_b, gb), _rms(out_c, gc)], axis=-1)
    return (_ln(ALPHA * x1 + _bdot(o, wout), ln_g, ln_b),)


def _ln_only(pre, g, b):
    return (_ln(pre, g, b),)


def _loss_head(y, target, tm):
    t, d = y.shape

    def body(y_ref, t_ref, dy_ref, l_ref):
        i = pl.program_id(0)
        e = y_ref[...] - t_ref[...]
        dy_ref[...] = e * (1.0 / d)
        part = 0.5 * jnp.sum(jnp.sum(e * e, axis=-1, keepdims=True) * (1.0 / d), axis=0, keepdims=True)
        row = jnp.where(lax.broadcasted_iota(jnp.int32, (1, LANES), 1) == 0, part, 0.0)

        @pl.when(i == 0)
        def _():
            l_ref[...] = row

        @pl.when(i > 0)
        def _():
            l_ref[...] += row

    spec = pl.BlockSpec((tm, d), lambda i: (i, 0))
    return pl.pallas_call(
        body, grid=(t // tm,), in_specs=[spec, spec], out_specs=[spec, pl.BlockSpec((1, LANES), lambda i: (0, 0))],
        out_shape=[jax.ShapeDtypeStruct((t, d), F32), jax.ShapeDtypeStruct((1, LANES), F32)],
        compiler_params=pltpu.CompilerParams(dimension_semantics=("arbitrary",)), name="loss_head")(y, target)


def _adamw(w, g, m, v, name):
    r, c = w.shape
    tr = r
    for cand in (512, 256, 352, 128):
        if r % cand == 0:
            tr = cand
            break

    def body(w_ref, g_ref, m_ref, v_ref, d_ref, nm_ref, nv_ref):
        gv = g_ref[...]
        mn = ADAM_B1 * m_ref[...] + (1.0 - ADAM_B1) * gv
        vn = ADAM_B2 * v_ref[...] + (1.0 - ADAM_B2) * (gv * gv)
        m_hat = mn / (1.0 - ADAM_B1 ** ADAM_STEP)
        v_hat = vn / (1.0 - ADAM_B2 ** ADAM_STEP)
        d_ref[...] = -ADAM_LR * (m_hat / (jnp.sqrt(v_hat) + ADAM_EPS) + ADAM_WD * w_ref[...])
        nm_ref[...] = mn
        nv_ref[...] = vn

    spec = pl.BlockSpec((tr, c), lambda i: (i, 0))
    return pl.pallas_call(
        body, grid=(r // tr,), in_specs=[spec] * 4, out_specs=[spec] * 3,
        out_shape=[jax.ShapeDtypeStruct((r, c), F32)] * 3,
        compiler_params=pltpu.CompilerParams(dimension_semantics=("parallel",)), name=name)(w, g, m, v)


def _exchange(name, ins, out_shapes, aliases, plan, n_remote, n_local):
    n_in = len(ins)
    n_out = len(out_shapes)

    def body(*refs):
        in_refs, out_refs = refs[:n_in], refs[n_in:n_in + n_out]
        send_sems, recv_sems, loc_sems = refs[n_in + n_out:]
        x, y, c = lax.axis_index("x"), lax.axis_index("y"), lax.axis_index("c")
        remote, local = plan(x, y, c, in_refs, out_refs)
        loc = [pltpu.make_async_copy(s, d, loc_sems.at[k]) for k, (s, d) in enumerate(local)]
        rem = [pltpu.make_async_remote_copy(src_ref=s, dst_ref=d, send_sem=send_sems.at[k], recv_sem=recv_sems.at[k],
                                            device_id=peer, device_id_type=MESH) for k, (s, d, peer) in enumerate(remote)]
        for cp in loc + rem:
            cp.start()
        for cp in loc + rem:
            cp.wait()

    hbm = pl.BlockSpec(memory_space=pl.ANY)
    return pl.pallas_call(
        body, in_specs=[hbm] * n_in, out_specs=[hbm] * n_out, out_shape=out_shapes, input_output_aliases=aliases,
        scratch_shapes=[pltpu.SemaphoreType.DMA((n_remote,)), pltpu.SemaphoreType.DMA((n_remote,)),
                        pltpu.SemaphoreType.DMA((max(n_local, 1),))],
        compiler_params=pltpu.CompilerParams(has_side_effects=True), name=name)(*ins)[0]


def _all_gather_rows(shard):
    r, w = shard.shape
    out = jax.ShapeDtypeStruct((8, r, w), shard.dtype)

    def plan1(x, y, c, ins, outs):
        me = 4 * x + 2 * y + c
        dst = outs[0].at[me]
        chips = [(1 - x, y), (x, 1 - y), (1 - x, 1 - y)]
        return [(ins[0], dst, (px, py, c)) for px, py in chips], [(ins[0], dst)]

    g = _exchange("ag_chips", [shard], [out], {}, plan1, 3, 1)

    def plan2(x, y, c, ins, outs):
        rem = []
        for q in range(4):
            blk = outs[0].at[2 * q + c]
            rem.append((blk, blk, (x, y, 1 - c)))
        return rem, []

    return _exchange("ag_sibling", [g], [out], {0: 0}, plan2, 4, 0)


def _pair_add(a, b, idx, name, out_dtype):
    _, _, r, w = a.shape
    tr = _tile(r, 512) if r % 512 == 0 else _tile(r, 256)

    def body(i_ref, a_ref, b_ref, o_ref):
        o_ref[...] = (a_ref[...].astype(F32) + b_ref[...].astype(F32)).astype(o_ref.dtype)

    grid_spec = pltpu.PrefetchScalarGridSpec(
        num_scalar_prefetch=1, grid=(4, r // tr),
        in_specs=[pl.BlockSpec((None, None, tr, w), lambda q, i, s: (q, s[0], i, 0)),
                  pl.BlockSpec((None, tr, w), lambda q, i, s: (q, i, 0))],
        out_specs=pl.BlockSpec((None, tr, w), lambda q, i, s: (q, i, 0)))
    return pl.pallas_call(body, grid_spec=grid_spec, out_shape=jax.ShapeDtypeStruct((4, r, w), out_dtype),
                          compiler_params=pltpu.CompilerParams(dimension_semantics=("parallel", "parallel")), name=name)(
                              idx, a, b)


def _quad_add(p, rb, idx, name):
    _, r, w = p.shape
    tr = _tile(r, 512) if r % 512 == 0 else _tile(r, 256)

    def body(i_ref, p_ref, r0, r1, r2, o_ref):
        o_ref[...] = ((p_ref[...].astype(F32) + r0[...].astype(F32)) + r1[...].astype(F32)) + r2[...].astype(F32)

    grid_spec = pltpu.PrefetchScalarGridSpec(
        num_scalar_prefetch=1, grid=(r // tr,),
        in_specs=[pl.BlockSpec((None, tr, w), lambda i, s: (s[0], i, 0))]
        + [pl.BlockSpec((None, tr, w), functools.partial(lambda i, s, k: (k, i, 0), k=k)) for k in range(3)],
        out_specs=pl.BlockSpec((tr, w), lambda i, s: (i, 0)))
    return pl.pallas_call(body, grid_spec=grid_spec, out_shape=jax.ShapeDtypeStruct((r, w), F32),
                          compiler_params=pltpu.CompilerParams(dimension_semantics=("parallel",)), name=name)(
                              idx, p, rb, rb, rb)


def _reduce_scatter_rows(full, x, y, c):
    _, r, w = full.shape

    def plan_a(px, py, pc, ins, outs):
        rem = [(ins[0].at[2 * q + (1 - pc)], outs[0].at[q], (px, py, 1 - pc)) for q in range(4)]
        return rem, []

    ra = _exchange("rs_sibling", [full], [jax.ShapeDtypeStruct((4, r, w), full.dtype)], {}, plan_a, 4, 0)
    part = _pair_add(full.reshape(4, 2, r, w), ra, jnp.reshape(c, (1,)).astype(jnp.int32), "rs_add_sibling", full.dtype)

    def plan_b(px, py, pc, ins, outs):
        rels = [(1, 0), (0, 1), (1, 1)]
        rem = []
        for k, (dx, dy) in enumerate(rels):
            tx, ty = px ^ dx, py ^ dy
            rem.append((ins[0].at[2 * tx + ty], outs[0].at[k], (tx, ty, pc)))
        return rem, []

    rb = _exchange("rs_chips", [part], [jax.ShapeDtypeStruct((3, r, w), full.dtype)], {}, plan_b, 3, 0)
    return _quad_add(part, rb, jnp.reshape(2 * x + y, (1,)).astype(jnp.int32), "rs_add_chips")


def _all_reduce_small(buf):
    r, w = buf.shape

    def plan(x, y, c, ins, outs):
        me = 4 * x + 2 * y + c
        dst = outs[0].at[me]
        rem = []
        for rel in range(1, 8):
            dx, dy, dc = rel >> 2, (rel >> 1) & 1, rel & 1
            rem.append((ins[0], dst, (x ^ dx, y ^ dy, c ^ dc)))
        return rem, [(ins[0], dst)]

    allb = _exchange("ar_small", [buf], [jax.ShapeDtypeStruct((8, r, w), buf.dtype)], {}, plan, 7, 1)

    def body(a_ref, o_ref):
        s = a_ref[0]
        for k in range(1, 8):
            s = s + a_ref[k]
        o_ref[...] = s

    return pl.pallas_call(body, out_shape=jax.ShapeDtypeStruct((r, w), F32), name="ar_small_sum")(allb)


def _pad_rows(a, rows):
    return jnp.pad(a, ((0, rows - a.shape[0]), (0, 0)))


def _pack_small(arrs):
    flat = [jnp.pad(a.reshape(-1), (0, (-a.size) % LANES)) for a in arrs]
    buf = jnp.concatenate(flat).reshape(-1, LANES)
    return _pad_rows(buf, buf.shape[0] + (-buf.shape[0]) % 8)


def _unpack_small(buf, shapes):
    out, off = [], 0
    flat = buf.reshape(-1)
    for s in shapes:
        n = int(np.prod(s))
        out.append(flat[off:off + n].reshape(s))
        off += n + (-n) % LANES
    return out


def _block_diag(blocks):
    n, r, c = blocks.shape
    eye = jnp.eye(n, dtype=blocks.dtype)
    return (blocks[:, :, None, :] * eye[:, None, :, None]).reshape(n * r, n * c)


def _diag_blocks(dense, n):
    r, c = dense.shape[0] // n, dense.shape[1] // n
    d4 = dense.reshape(n, r, n, c)
    return jnp.stack([d4[i, :, i, :] for i in range(n)])


def kernel(x, ffn1_w_gate, ffn1_w_up, ffn1_w_down, ln1_g, ln1_b, w_in, conv_w, conv_b, rg_w_a, rg_b_a, rg_w_x, rg_b_x, rg_lambda, fox_b_f, s5_a_re, s5_a_im, s5_log_dt, s5_b_re, s5_b_im, s5_c_re, s5_c_im, s5_d, s5_w_glu, mix_norm_g, w_out, ln2_g, ln2_b, ffn2_w_gate, ffn2_w_up, ffn2_w_down, ln3_g, ln3_b, loss_target, m_ffn1_w_gate, m_ffn1_w_up, m_ffn1_w_down, m_ln1_g, m_ln1_b, m_w_in, m_conv_w, m_conv_b, m_rg_w_a, m_rg_b_a, m_rg_w_x, m_rg_b_x, m_rg_lambda, m_fox_b_f, m_s5_a_re, m_s5_a_im, m_s5_log_dt, m_s5_b_re, m_s5_b_im, m_s5_c_re, m_s5_c_im, m_s5_d, m_s5_w_glu, m_mix_norm_g, m_w_out, m_ln2_g, m_ln2_b, m_ffn2_w_gate, m_ffn2_w_up, m_ffn2_w_down, m_ln3_g, m_ln3_b, v_ffn1_w_gate, v_ffn1_w_up, v_ffn1_w_down, v_ln1_g, v_ln1_b, v_w_in, v_conv_w, v_conv_b, v_rg_w_a, v_rg_b_a, v_rg_w_x, v_rg_b_x, v_rg_lambda, v_fox_b_f, v_s5_a_re, v_s5_a_im, v_s5_log_dt, v_s5_b_re, v_s5_b_im, v_s5_c_re, v_s5_c_im, v_s5_d, v_s5_w_glu, v_mix_norm_g, v_w_out, v_ln2_g, v_ln2_b, v_ffn2_w_gate, v_ffn2_w_up, v_ffn2_w_down, v_ln3_g, v_ln3_b):
    a = dict(locals())
    w = {n: a[n] for n in WEIGHTS}
    t, d = x.shape[1], x.shape[2]
    f = ffn1_w_down.shape[1] * 8
    fs, ds = f // 8, d // 8
    mx, my, mc = lax.axis_index("x"), lax.axis_index("y"), lax.axis_index("c")
    me = 4 * mx + 2 * my + mc
    tm = _tile(t, 512)
    tf = f // 2
    win_rows = ds * Z_W // d

    def shard_rows(l):
        wi = w['w_in'][l]
        win_p = jnp.concatenate([wi[:, :Z_F + N_HEADS], jnp.zeros((ds, Z_U - Z_F - N_HEADS), F32), wi[:, Z_F + N_HEADS:]], axis=1)
        conv_bits = lax.bitcast_convert_type(w['conv_w'][l], BF16).reshape(1, -1)
        segs = [w['ffn1_w_gate'][l].T, w['ffn1_w_up'][l].T, w['ffn1_w_down'][l],
                w['ffn2_w_gate'][l].T, w['ffn2_w_up'][l].T, w['ffn2_w_down'][l],
                win_p.reshape(win_rows, d), w['w_out'][l], _pad_rows(w['s5_w_glu'][l].reshape(-1, d), 16)]
        segs = [s.astype(BF16) for s in segs]
        segs.append(_pad_rows(jnp.pad(conv_bits, ((0, 0), (0, d - conv_bits.shape[1]))), 16))
        return jnp.concatenate(segs, axis=0)

    seg_rows = [fs] * 6 + [win_rows, ds, 16, 16]
    seg_off = [int(v) for v in np.cumsum([0] + seg_rows)]
    rl = seg_off[-1]
    gathered = _all_gather_rows(jnp.concatenate([shard_rows(l) for l in range(DEPTH)], axis=0))

    def seg(l, k):
        return gathered[:, l * rl + seg_off[k]: l * rl + seg_off[k + 1]]

    xs = x[0]
    saved = []
    cur = xs
    for l in range(DEPTH):
        g1, u1, d1, g2, u2, d2 = [seg(l, k).reshape(f, d) for k in range(6)]
        win = seg(l, 6).reshape(d, Z_W)
        wout = seg(l, 7).reshape(d, d).astype(F32)
        wglu = seg(l, 8)[:, :D_C * D_C // (8 * d)].reshape(D_C, D_C).astype(F32)
        conv_full = lax.bitcast_convert_type(seg(l, 9)[:, 0, :2 * CONV_WIDTH * D_A // 8].reshape(8, CONV_WIDTH, D_A // 8, 2), F32)
        conv_full = conv_full.transpose(1, 0, 2).reshape(CONV_WIDTH, D_A)
        row = lambda n: w[n][l].reshape(1, -1)
        wa = _block_diag(w['rg_w_a'][l]).reshape(3, LANES, 3, LANES)
        wa = jnp.stack([wa[i, :, i, :] for i in range(3)])
        wx = _block_diag(w['rg_w_x'][l]).reshape(3, LANES, 3, LANES)
        wx = jnp.stack([wx[i, :, i, :] for i in range(3)])
        bf = jnp.pad(row('fox_b_f'), ((0, 0), (0, LANES - N_HEADS)))
        s5p = (w['s5_a_re'][l], w['s5_a_im'][l], w['s5_log_dt'][l].reshape(-1, 1),
               w['s5_b_re'][l].transpose(0, 2, 1), w['s5_b_im'][l].transpose(0, 2, 1))
        ab_re, ab_im, bb_re, bb_im = _s5_prep(*s5p)
        bd_re = _block_diag(bb_re).reshape(2, LANES, 2, N_STATE // 2)
        bd_re = jnp.stack([bd_re[i, :, i, :] for i in range(2)])
        bd_im = _block_diag(bb_im).reshape(2, LANES, 2, N_STATE // 2)
        bd_im = jnp.stack([bd_im[i, :, i, :] for i in range(2)])
        cd_re = _block_diag(w['s5_c_re'][l].transpose(0, 2, 1)).reshape(2, N_STATE // 2, 2, LANES)
        cd_re = jnp.stack([cd_re[i, :, i, :] for i in range(2)])
        cd_im = _block_diag(w['s5_c_im'][l].transpose(0, 2, 1)).reshape(2, N_STATE // 2, 2, LANES)
        cd_im = jnp.stack([cd_im[i, :, i, :] for i in range(2)])
        abr, abi = ab_re.reshape(2, 1, N_STATE // 2), ab_im.reshape(2, 1, N_STATE // 2)
        gm = row('mix_norm_g')
        ga, gb, gc = gm[:, :D_A], gm[:, D_A:D_A + D_B], gm[:, D_A + D_B:]

        x0 = cur
        x1, pre1, gs1, us1 = _ffn_fwd(x0, g1, u1, d1, row('ln1_g'), row('ln1_b'), tm, tf)
        z = _mm(x1, win, 'nn', F32, tm, Z_W, d, "mix_in")
        out_a, h_a = _rg_fwd(z, conv_full, row('conv_b'), wa, row('rg_b_a'), wx, row('rg_b_x'), row('rg_lambda'))
        cs = _fgate_fwd(z, bf)
        ccol = cs[:, :N_HEADS].T.reshape(N_HEADS, t, 1)
        crow = cs[:, :N_HEADS].T.reshape(N_HEADS, 1, t)
        out_b, lse = _attn_fwd(z, ccol, crow)
        yc = _s5_fwd(z, bd_re, bd_im, abr, abi, cd_re, cd_im, row('s5_d'))
        mix_params = [ga, gb, gc, wglu, wout, row('ln2_g'), row('ln2_b')]
        (x2,) = _rowwise(_mix_out, [out_a, out_b, yc, x1], mix_params, [(d, F32)], _tile(t, 256), "mix_out")
        x3, pre3, gs2, us2 = _ffn_fwd(x2, g2, u2, d2, row('ln3_g'), row('ln3_b'), tm, tf)
        saved.append(dict(x0=x0, x1=x1, pre1=pre1, gs1=gs1, us1=us1, z=z, out_a=out_a, h_a=h_a, ccol=ccol, crow=crow,
                          out_b=out_b, lse=lse, yc=yc, x2=x2, pre3=pre3, gs2=gs2, us2=us2, mix_params=mix_params,
                          ffn1=(g1, u1, d1), ffn2=(g2, u2, d2), win=win, conv_full=conv_full, wa=wa, wx=wx, bf=bf, s5p=s5p,
                          s5m=(bd_re, bd_im, abr, abi, cd_re, cd_im)))
        cur = x3

    dy, loss_row = _loss_head(cur, loss_target[0], tm)
    loss = lax.psum(loss_row[0, 0], ("x", "y", "c"))

    big_grads = [None] * DEPTH
    small_grads = {}
    grad_rows = [fs] * 6 + [win_rows, ds, 16, 16]
    for l in reversed(range(DEPTH)):
        s = saved[l]
        row = lambda n: w[n][l].reshape(1, -1)
        wg_tiles = dict(tm=tf, tn=d, tk=tm)

        def ffn_back(dyv, pre, gs, us, wts, xin, ln_g, ln_b):
            (dpre,), (dlg, dlb) = _rowwise_vjp(_ln_only, [pre], [ln_g, ln_b], [dyv], _tile(t, 256), "ln_bwd")
            dx, dg, du, hh = _ffn_bwd(dpre, gs, us, *wts, tm, tf)
            dwg = _mm(dg, xin, 'tn', BF16, name="ffn_dw_gate", **wg_tiles)
            dwu = _mm(du, xin, 'tn', BF16, name="ffn_dw_up", **wg_tiles)
            dwd = _mm(hh, dpre, 'tn', BF16, name="ffn_dw_down", **wg_tiles)
            return dx, dwg, dwu, dwd, dlg, dlb

        dx2, dwg2, dwu2, dwd2, dl3g, dl3b = ffn_back(dy, s['pre3'], s['gs2'], s['us2'], s['ffn2'], s['x2'], row('ln3_g'), row('ln3_b'))
        (d_oa, d_ob, d_yc, d_x1), (dga, dgb, dgc, dwglu, dwout, dl2g, dl2b) = _rowwise_vjp(
            _mix_out, [s['out_a'], s['out_b'], s['yc'], s['x1']], s['mix_params'], [dx2], _tile(t, 256), "mix_out_bwd")
        (d_ax, d_ag, dcw, dcb, dwa, dba, dwx, dbx, dlam) = _rg_bwd(
            s['z'], s['h_a'], d_oa, s['conv_full'], row('conv_b'), s['wa'], row('rg_b_a'), s['wx'], row('rg_b_x'), row('rg_lambda'))
        dq, dk, dv, dcrow = _attn_bwd(s['z'], s['ccol'], s['crow'], s['lse'], d_ob)
        dc_pad = jnp.pad(dcrow.reshape(N_HEADS, t).T, ((0, 0), (0, LANES - N_HEADS)))
        dzf, dbf = _fgate_bwd(s['z'], s['bf'], dc_pad)
        du_c, dbd_re, dbd_im, dabr, dabi, dcd_re, dcd_im, dd = _s5_bwd(s['z'], d_yc, *s['s5m'], row('s5_d'))
        dz = jnp.concatenate([d_ax, d_ag, dq, dk, dv, dzf, du_c], axis=1)
        dx1 = _mm(dz, s['win'], 'nt', F32, tm, d, Z_W, "mix_in_dx", add=d_x1)
        dwin = _mm(s['x1'], dz, 'tn', BF16, d, Z_W, tm, "mix_in_dw")
        dx0, dwg1, dwu1, dwd1, dl1g, dl1b = ffn_back(dx1, s['pre1'], s['gs1'], s['us1'], s['ffn1'], s['x0'], row('ln1_g'), row('ln1_b'))
        dy = dx0

        half = N_STATE // 2
        dbb_re = _diag_blocks(jnp.concatenate([jnp.pad(dbd_re[0], ((0, 0), (0, half))), jnp.pad(dbd_re[1], ((0, 0), (half, 0)))]), N_GROUPS)
        dbb_im = _diag_blocks(jnp.concatenate([jnp.pad(dbd_im[0], ((0, 0), (0, half))), jnp.pad(dbd_im[1], ((0, 0), (half, 0)))]), N_GROUPS)
        dcm_re = _diag_blocks(jnp.concatenate([jnp.pad(dcd_re[0], ((0, 0), (0, LANES))), jnp.pad(dcd_re[1], ((0, 0), (LANES, 0)))]), N_GROUPS)
        dcm_im = _diag_blocks(jnp.concatenate([jnp.pad(dcd_im[0], ((0, 0), (0, LANES))), jnp.pad(dcd_im[1], ((0, 0), (LANES, 0)))]), N_GROUPS)
        da_re, da_im, dlog_dt, db_re, db_im = _s5_prep_bwd(*s['s5p'], dabr.reshape(N_GROUPS, C_STATE), dabi.reshape(N_GROUPS, C_STATE), dbb_re, dbb_im)

        sg = dict(ln1_g=dl1g, ln1_b=dl1b, conv_w=dcw, conv_b=dcb,
                  rg_w_a=_diag_blocks(jnp.concatenate([jnp.pad(dwa[i], ((0, 0), (i * LANES, (2 - i) * LANES))) for i in range(3)]), 6),
                  rg_b_a=dba,
                  rg_w_x=_diag_blocks(jnp.concatenate([jnp.pad(dwx[i], ((0, 0), (i * LANES, (2 - i) * LANES))) for i in range(3)]), 6),
                  rg_b_x=dbx, rg_lambda=dlam, fox_b_f=dbf[:, :N_HEADS],
                  s5_a_re=da_re, s5_a_im=da_im, s5_log_dt=dlog_dt, s5_b_re=db_re.transpose(0, 2, 1), s5_b_im=db_im.transpose(0, 2, 1),
                  s5_c_re=dcm_re.transpose(0, 2, 1), s5_c_im=dcm_im.transpose(0, 2, 1), s5_d=dd,
                  mix_norm_g=jnp.concatenate([dga, dgb, dgc], axis=1), ln2_g=dl2g, ln2_b=dl2b, ln3_g=dl3g, ln3_b=dl3b)
        small_grads[l] = sg
        segs = [dwg1, dwu1, dwd1, dwg2, dwu2, dwd2, dwin.reshape(8 * win_rows, d), dwout.astype(BF16),
                jnp.pad(dwglu.astype(BF16).reshape(8, -1, d), ((0, 0), (0, 32 - D_C * D_C // (8 * d)), (0, 0))).reshape(-1, d)]
        big_grads[l] = jnp.concatenate([sgm.reshape(8, -1, d) for sgm in segs], axis=1)

    grad_x = dy.reshape(x.shape)
    goff = [int(v) for v in np.cumsum([0] + grad_rows)]
    rg = goff[-1]
    gsh = _reduce_scatter_rows(jnp.concatenate(big_grads, axis=1), mx, my, mc)

    def gseg(l, k):
        return gsh[l * rg + goff[k]: l * rg + goff[k + 1]]

    grads = {}
    for k, n in enumerate(['ffn1_w_gate', 'ffn1_w_up', 'ffn1_w_down', 'ffn2_w_gate', 'ffn2_w_up', 'ffn2_w_down']):
        grads[n] = jnp.stack([gseg(l, k).T if 'down' not in n else gseg(l, k) for l in range(DEPTH)])
    gwin = jnp.stack([gseg(l, 6).reshape(ds, Z_W) for l in range(DEPTH)])
    grads['w_in'] = jnp.concatenate([gwin[:, :, :Z_F + N_HEADS], gwin[:, :, Z_U:]], axis=2)
    grads['w_out'] = jnp.stack([gseg(l, 7) for l in range(DEPTH)])
    grads['s5_w_glu'] = jnp.stack([gseg(l, 8)[:D_C * D_C // (8 * d)].reshape(D_C // 8, D_C) for l in range(DEPTH)])

    small_names = SMALL + ['conv_w']
    small_shapes = [((DEPTH, CONV_WIDTH, D_A) if n == 'conv_w' else w[n].shape) for n in small_names]
    packed = _pack_small([jnp.stack([small_grads[l][n].reshape(sh[1:]) for l in range(DEPTH)])
                          for n, sh in zip(small_names, small_shapes)])
    summed = _all_reduce_small(packed)
    for n, g in zip(small_names, _unpack_small(summed, small_shapes)):
        grads[n] = g
    grads['conv_w'] = lax.dynamic_slice_in_dim(grads['conv_w'], me * (D_A // 8), D_A // 8, axis=2)

    delta, new_m, new_v = {}, {}, {}
    for n in BIG + ['conv_w']:
        sh = w[n].shape
        two = lambda v: v.reshape(-1, sh[-1])
        dl, nm, nv = _adamw(two(w[n]), two(grads[n]), two(a['m_' + n]), two(a['v_' + n]), "adamw_" + n)
        delta[n], new_m[n], new_v[n] = dl.reshape(sh), nm.reshape(sh), nv.reshape(sh)
    sshapes = [w[n].shape for n in SMALL]
    dl, nm, nv = _adamw(_pack_small([w[n] for n in SMALL]), _pack_small([grads[n] for n in SMALL]),
                        _pack_small([a['m_' + n] for n in SMALL]), _pack_small([a['v_' + n] for n in SMALL]), "adamw_small")
    for n, v1, v2, v3 in zip(SMALL, _unpack_small(dl, sshapes), _unpack_small(nm, sshapes), _unpack_small(nv, sshapes)):
        delta[n], new_m[n], new_v[n] = v1, v2, v3

    return (loss, grad_x, *[grads[n] for n in WEIGHTS], *[delta[n] for n in WEIGHTS], *[new_m[n] for n in WEIGHTS],
            *[new_v[n] for n in WEIGHTS])
```

```python
import functools
import math

import jax
import jax.numpy as jnp
import numpy as np
from jax import lax
from jax.experimental import pallas as pl
from jax.experimental.pallas import tpu as pltpu

F32 = jnp.float32
BF16 = jnp.bfloat16
MESH = pl.DeviceIdType.MESH

DEPTH = 2
ALPHA = (2 * DEPTH) ** 0.25
LN_EPS = 1e-5
RMS_EPS = 1e-6
RG_C = 8.0
CONV_WIDTH = 4
HEAD_DIM = 64
C_GROUP = 16
C_STATE = 64
D_A = 384
D_B = 384
D_C = 256
N_HEADS = D_B // HEAD_DIM
N_GROUPS = D_C // C_GROUP
N_STATE = N_GROUPS * C_STATE
Z_F = 2 * D_A + 3 * D_B
Z_U = Z_F + 128
Z_W = Z_U + D_C
N_IN = Z_F + N_HEADS + D_C
ADAM_LR, ADAM_B1, ADAM_B2, ADAM_EPS, ADAM_WD, ADAM_STEP = 0.001, 0.9, 0.999, 1e-08, 0.01, 10
LANES = 128
NEG = -1e30

WEIGHTS = ['ffn1_w_gate', 'ffn1_w_up', 'ffn1_w_down', 'ln1_g', 'ln1_b', 'w_in', 'conv_w', 'conv_b', 'rg_w_a', 'rg_b_a',
           'rg_w_x', 'rg_b_x', 'rg_lambda', 'fox_b_f', 's5_a_re', 's5_a_im', 's5_log_dt', 's5_b_re', 's5_b_im', 's5_c_re',
           's5_c_im', 's5_d', 's5_w_glu', 'mix_norm_g', 'w_out', 'ln2_g', 'ln2_b', 'ffn2_w_gate', 'ffn2_w_up', 'ffn2_w_down',
           'ln3_g', 'ln3_b']
BIG = ['ffn1_w_gate', 'ffn1_w_up', 'ffn1_w_down', 'w_in', 's5_w_glu', 'w_out', 'ffn2_w_gate', 'ffn2_w_up', 'ffn2_w_down']
SMALL = [n for n in WEIGHTS if n not in BIG and n != 'conv_w']


def _sig(x):
    return 1.0 / (1.0 + jnp.exp(-x))


def _gelu(x):
    return 0.5 * x * (1.0 + jnp.tanh(math.sqrt(2.0 / math.pi) * (x + 0.044715 * (x * x * x))))


def _softplus(x):
    return jnp.maximum(x, 0.0) + jnp.log(1.0 + jnp.exp(jnp.minimum(x, -x)))


def _dot(a, b, dims):
    return lax.dot_general(a.astype(BF16), b.astype(BF16), (dims, ((), ())), preferred_element_type=F32)


NN = ((1,), (0,))
NT = ((1,), (1,))
TN = ((0,), (0,))


@jax.custom_vjp
def _bdot(a, w):
    return _dot(a, w, NN)


def _bdot_fwd(a, w):
    return _dot(a, w, NN), (a, w)


def _bdot_bwd(res, ct):
    a, w = res
    return _dot(ct, w, NT), _dot(a, ct, TN)


_bdot.defvjp(_bdot_fwd, _bdot_bwd)


def _ln(pre, g, b):
    mu = jnp.mean(pre, axis=-1, keepdims=True)
    xc = pre - mu
    var = jnp.mean(xc * xc, axis=-1, keepdims=True)
    return xc * lax.rsqrt(var + LN_EPS) * g + b


def _rms(x, g):
    return x * lax.rsqrt(jnp.mean(x * x, axis=-1, keepdims=True) + RMS_EPS) * g


def _tile(n, want):
    return want if n % want == 0 else n


def _mm(a, b, dims, out_dtype, tm, tn, tk, name, add=None):
    if dims == 'nn':
        (m, k), n = a.shape, b.shape[1]
        a_spec = pl.BlockSpec((tm, tk), lambda i, j, q: (i, q))
        b_spec = pl.BlockSpec((tk, tn), lambda i, j, q: (q, j))
        dn = NN
    elif dims == 'nt':
        (m, k), n = a.shape, b.shape[0]
        a_spec = pl.BlockSpec((tm, tk), lambda i, j, q: (i, q))
        b_spec = pl.BlockSpec((tn, tk), lambda i, j, q: (j, q))
        dn = NT
    else:
        (k, m), n = a.shape, b.shape[1]
        a_spec = pl.BlockSpec((tk, tm), lambda i, j, q: (q, i))
        b_spec = pl.BlockSpec((tk, tn), lambda i, j, q: (q, j))
        dn = TN
    nk = k // tk
    o_spec = pl.BlockSpec((tm, tn), lambda i, j, q: (i, j))

    def body(*refs):
        if add is None:
            a_ref, b_ref, o_ref, acc_ref = refs
        else:
            a_ref, b_ref, add_ref, o_ref, acc_ref = refs
        q = pl.program_id(2)
        part = _dot(a_ref[...], b_ref[...], dn)

        @pl.when(q == 0)
        def _():
            acc_ref[...] = part

        @pl.when(q > 0)
        def _():
            acc_ref[...] += part

        @pl.when(q == nk - 1)
        def _():
            r = acc_ref[...]
            if add is not None:
                r = r + add_ref[...]
            o_ref[...] = r.astype(o_ref.dtype)

    ins = [a, b] + ([] if add is None else [add])
    specs = [a_spec, b_spec] + ([] if add is None else [o_spec])
    return pl.pallas_call(
        body, grid=(m // tm, n // tn, nk), in_specs=specs, out_specs=o_spec,
        out_shape=jax.ShapeDtypeStruct((m, n), out_dtype), scratch_shapes=[pltpu.VMEM((tm, tn), F32)],
        compiler_params=pltpu.CompilerParams(dimension_semantics=("parallel", "parallel", "arbitrary")), name=name)(*ins)


def _rowwise(fn, rows, params, outs, tm, name):
    t = rows[0].shape[0]
    nr, npar = len(rows), len(params)

    def body(*refs):
        r = [x[...] for x in refs[:nr]]
        p = [x[...] for x in refs[nr:nr + npar]]
        res = fn(*r, *p)
        for o_ref, o in zip(refs[nr + npar:], res):
            o_ref[...] = o.astype(o_ref.dtype)

    in_specs = ([pl.BlockSpec((tm, a.shape[1]), lambda i: (i, 0)) for a in rows]
                + [pl.BlockSpec(p.shape, lambda i: (0, 0)) for p in params])
    return pl.pallas_call(
        body, grid=(t // tm,), in_specs=in_specs,
        out_specs=[pl.BlockSpec((tm, c), lambda i: (i, 0)) for c, _ in outs],
        out_shape=[jax.ShapeDtypeStruct((t, c), d) for c, d in outs],
        compiler_params=pltpu.CompilerParams(dimension_semantics=("parallel",)), name=name)(*rows, *params)


def _rowwise_vjp(fn, rows, params, cots, tm, name):
    t = rows[0].shape[0]
    nr, npar, nc = len(rows), len(params), len(cots)

    def body(*refs):
        r = [x[...] for x in refs[:nr]]
        p = [x[...] for x in refs[nr:nr + npar]]
        c = [x[...] for x in refs[nr + npar:nr + npar + nc]]
        o_refs = refs[nr + npar + nc:]
        _, pull = jax.vjp(fn, *r, *p)
        grads = pull(tuple(c))
        for o_ref, g in zip(o_refs[:nr], grads[:nr]):
            o_ref[...] = g
        i = pl.program_id(0)

        @pl.when(i == 0)
        def _():
            for o_ref, g in zip(o_refs[nr:], grads[nr:]):
                o_ref[...] = g

        @pl.when(i > 0)
        def _():
            for o_ref, g in zip(o_refs[nr:], grads[nr:]):
                o_ref[...] += g

    row_spec = lambda a: pl.BlockSpec((tm, a.shape[1]), lambda i: (i, 0))
    par_spec = lambda p: pl.BlockSpec(p.shape, lambda i: (0, 0))
    res = pl.pallas_call(
        body, grid=(t // tm,),
        in_specs=[row_spec(a) for a in rows] + [par_spec(p) for p in params] + [row_spec(a) for a in cots],
        out_specs=[row_spec(a) for a in rows] + [par_spec(p) for p in params],
        out_shape=[jax.ShapeDtypeStruct(a.shape, F32) for a in rows] + [jax.ShapeDtypeStruct(p.shape, F32) for p in params],
        compiler_params=pltpu.CompilerParams(dimension_semantics=("arbitrary",)), name=name)(*rows, *params, *cots)
    return res[:nr], res[nr:]


def _ffn_fwd(x, wgt, wut, wd, ln_g, ln_b, tm, tf):
    t, d = x.shape
    f = wgt.shape[0]
    nj = f // tf

    def body(x_ref, wg_ref, wu_ref, wd_ref, g_ref, b_ref, y_ref, pre_ref, gs_ref, us_ref, acc_ref):
        j = pl.program_id(1)
        xv = x_ref[...]
        xb = xv.astype(BF16)
        g = _dot(xb, wg_ref[...], NT)
        u = _dot(xb, wu_ref[...], NT)
        gs_ref[...] = g.astype(BF16)
        us_ref[...] = u.astype(BF16)
        part = _dot(g * _sig(g) * u, wd_ref[...], NN)

        @pl.when(j == 0)
        def _():
            acc_ref[...] = part

        @pl.when(j > 0)
        def _():
            acc_ref[...] += part

        @pl.when(j == nj - 1)
        def _():
            pre = ALPHA * xv + 0.5 * acc_ref[...]
            pre_ref[...] = pre
            y_ref[...] = _ln(pre, g_ref[...], b_ref[...])

    w_spec = pl.BlockSpec((tf, d), lambda i, j: (j, 0))
    x_spec = pl.BlockSpec((tm, d), lambda i, j: (i, 0))
    v_spec = pl.BlockSpec((1, d), lambda i, j: (0, 0))
    h_spec = pl.BlockSpec((tm, tf), lambda i, j: (i, j))
    return pl.pallas_call(
        body, grid=(t // tm, nj), in_specs=[x_spec, w_spec, w_spec, w_spec, v_spec, v_spec],
        out_specs=[x_spec, x_spec, h_spec, h_spec],
        out_shape=[jax.ShapeDtypeStruct((t, d), F32), jax.ShapeDtypeStruct((t, d), F32),
                   jax.ShapeDtypeStruct((t, f), BF16), jax.ShapeDtypeStruct((t, f), BF16)],
        scratch_shapes=[pltpu.VMEM((tm, d), F32)],
        compiler_params=pltpu.CompilerParams(dimension_semantics=("parallel", "arbitrary")), name="ffn_fwd")(
            x, wgt, wut, wd, ln_g, ln_b)


def _ffn_bwd(dpre, gs, us, wgt, wut, wd, tm, tf):
    t, d = dpre.shape
    f = wgt.shape[0]
    nj = f // tf

    def body(dp_ref, gs_ref, us_ref, wg_ref, wu_ref, wd_ref, dx_ref, dg_ref, du_ref, hh_ref, acc_ref):
        j = pl.program_id(1)
        dp = dp_ref[...]
        dh = _dot(0.5 * dp, wd_ref[...], NT)
        g = gs_ref[...].astype(F32)
        u = us_ref[...].astype(F32)
        s = _sig(g)
        sl = g * s
        dg = (dh * u * (s * (1.0 + g * (1.0 - s)))).astype(BF16)
        du = (dh * sl).astype(BF16)
        dg_ref[...] = dg
        du_ref[...] = du
        hh_ref[...] = (0.5 * sl * u).astype(BF16)
        part = _dot(dg, wg_ref[...], NN) + _dot(du, wu_ref[...], NN)

        @pl.when(j == 0)
        def _():
            acc_ref[...] = part

        @pl.when(j > 0)
        def _():
            acc_ref[...] += part

        @pl.when(j == nj - 1)
        def _():
            dx_ref[...] = ALPHA * dp + acc_ref[...]

    w_spec = pl.BlockSpec((tf, d), lambda i, j: (j, 0))
    x_spec = pl.BlockSpec((tm, d), lambda i, j: (i, 0))
    h_spec = pl.BlockSpec((tm, tf), lambda i, j: (i, j))
    return pl.pallas_call(
        body, grid=(t // tm, nj), in_specs=[x_spec, h_spec, h_spec, w_spec, w_spec, w_spec],
        out_specs=[x_spec, h_spec, h_spec, h_spec],
        out_shape=[jax.ShapeDtypeStruct((t, d), F32)] + [jax.ShapeDtypeStruct((t, f), BF16)] * 3,
        scratch_shapes=[pltpu.VMEM((tm, d), F32)],
        compiler_params=pltpu.CompilerParams(dimension_semantics=("parallel", "arbitrary")), name="ffn_bwd")(
            dpre, gs, us, wgt, wut, wd)


def _rg_local(xa, wa, ba, wx, bx, lam):
    r = _sig(_bdot(xa, wa) + ba)
    i = _sig(_bdot(xa, wx) + bx)
    log_a = -RG_C * r * _softplus(-lam)
    a = jnp.exp(log_a)
    mult = jnp.sqrt(-jnp.tanh(log_a) * (a * a + 1.0))
    return a, mult * (i * xa)


def _conv_taps(ext, n):
    return [ext[8:, :]] + [pltpu.roll(ext, s, 0)[8:, :] for s in (1, 2, 3)]


def _rg_fwd(z, cw, cb, wa, ba, wx, bx, lam):
    t = z.shape[0]
    cr = _tile(t, 256)
    nb = D_A // LANES

    def body(ax_ref, ag_ref, cw_ref, cb_ref, wa_ref, ba_ref, wx_ref, bx_ref, lam_ref, out_ref, h_ref, axp, a_s, b_s):
        axp[pl.ds(0, 8), :] = jnp.zeros((8, LANES), F32)
        pltpu.sync_copy(ax_ref, axp.at[pl.ds(8, t)])
        w = [cw_ref[pl.ds(k, 1), :] for k in range(CONV_WIDTH)]

        def chunk(c, carry):
            t0 = pl.multiple_of(c * cr, cr)
            taps = _conv_taps(axp[pl.ds(t0, cr + 8), :], cr)
            xa = cb_ref[...] + w[3] * taps[0] + w[2] * taps[1] + w[1] * taps[2] + w[0] * taps[3]
            a, gated = _rg_local(xa, wa_ref[...], ba_ref[...], wx_ref[...], bx_ref[...], lam_ref[...])
            a_s[pl.ds(t0, cr), :] = a
            b_s[pl.ds(t0, cr), :] = gated
            return carry

        lax.fori_loop(0, t // cr, chunk, 0)

        def step(i, h):
            h = a_s[pl.ds(i, 1), :] * h + b_s[pl.ds(i, 1), :]
            h_ref[pl.ds(i, 1), :] = h
            return h

        lax.fori_loop(0, t, step, jnp.zeros((1, LANES), F32), unroll=8)

        def fin(c, carry):
            t0 = pl.multiple_of(c * cr, cr)
            out_ref[pl.ds(t0, cr), :] = _gelu(ag_ref[pl.ds(t0, cr), :]) * h_ref[pl.ds(t0, cr), :]
            return carry

        lax.fori_loop(0, t // cr, fin, 0)

    col = lambda off: pl.BlockSpec((t, LANES), lambda b: (0, off + b))
    vec = pl.BlockSpec((1, LANES), lambda b: (0, b))
    mat = pl.BlockSpec((None, LANES, LANES), lambda b: (b, 0, 0))
    return pl.pallas_call(
        body, grid=(nb,),
        in_specs=[col(0), col(nb), pl.BlockSpec((CONV_WIDTH, LANES), lambda b: (0, b)), vec, mat, vec, mat, vec, vec],
        out_specs=[col(0), col(0)],
        out_shape=[jax.ShapeDtypeStruct((t, D_A), F32), jax.ShapeDtypeStruct((t, D_A), F32)],
        scratch_shapes=[pltpu.VMEM((t + 8, LANES), F32), pltpu.VMEM((t, LANES), F32), pltpu.VMEM((t, LANES), F32)],
        compiler_params=pltpu.CompilerParams(dimension_semantics=("arbitrary",)), name="rglru_fwd")(
            z, z, cw, cb, wa, ba, wx, bx, lam)


def _rg_bwd(z, h, dout, cw, cb, wa, ba, wx, bx, lam):
    t = z.shape[0]
    cr = _tile(t, 256)
    nb = D_A // LANES

    def body(ax_ref, ag_ref, h_ref, do_ref, cw_ref, cb_ref, wa_ref, ba_ref, wx_ref, bx_ref, lam_ref,
             dax_ref, dag_ref, dcw_ref, dcb_ref, dwa_ref, dba_ref, dwx_ref, dbx_ref, dlam_ref,
             axp, hp, xa_s, a_s, g_s, dxa_s):
        zero8 = jnp.zeros((8, LANES), F32)
        axp[pl.ds(0, 8), :] = zero8
        hp[pl.ds(0, 8), :] = zero8
        dxa_s[pl.ds(t, 8), :] = zero8
        pltpu.sync_copy(ax_ref, axp.at[pl.ds(8, t)])
        pltpu.sync_copy(h_ref, hp.at[pl.ds(8, t)])
        w = [cw_ref[pl.ds(k, 1), :] for k in range(CONV_WIDTH)]
        for ref in (dcw_ref, dcb_ref, dwa_ref, dba_ref, dwx_ref, dbx_ref, dlam_ref):
            ref[...] = jnp.zeros(ref.shape, F32)

        def p1(c, carry):
            t0 = pl.multiple_of(c * cr, cr)
            taps = _conv_taps(axp[pl.ds(t0, cr + 8), :], cr)
            xa = cb_ref[...] + w[3] * taps[0] + w[2] * taps[1] + w[1] * taps[2] + w[0] * taps[3]
            a, _ = _rg_local(xa, wa_ref[...], ba_ref[...], wx_ref[...], bx_ref[...], lam_ref[...])
            xa_s[pl.ds(t0, cr), :] = xa
            a_s[pl.ds(t0, cr), :] = a
            ag = ag_ref[pl.ds(t0, cr), :]
            dov = do_ref[pl.ds(t0, cr), :]
            gel, pull = jax.vjp(_gelu, ag)
            g_s[pl.ds(t0, cr), :] = dov * gel
            dag_ref[pl.ds(t0, cr), :] = pull(dov * h_ref[pl.ds(t0, cr), :])[0]
            return carry

        lax.fori_loop(0, t // cr, p1, 0)

        def step(k, carry):
            i = t - 1 - k
            g = g_s[pl.ds(i, 1), :] + carry
            g_s[pl.ds(i, 1), :] = g
            return g * a_s[pl.ds(i, 1), :]

        lax.fori_loop(0, t, step, jnp.zeros((1, LANES), F32), unroll=8)

        def p3(c, carry):
            t0 = pl.multiple_of(c * cr, cr)
            g = g_s[pl.ds(t0, cr), :]
            h_prev = pltpu.roll(hp[pl.ds(t0, cr + 8), :], 1, 0)[8:, :]
            _, pull = jax.vjp(_rg_local, xa_s[pl.ds(t0, cr), :], wa_ref[...], ba_ref[...], wx_ref[...], bx_ref[...],
                              lam_ref[...])
            dxa, dwa, dba, dwx, dbx, dlam = pull((g * h_prev, g))
            dxa_s[pl.ds(t0, cr), :] = dxa
            dwa_ref[...] += dwa
            dba_ref[...] += dba
            dwx_ref[...] += dwx
            dbx_ref[...] += dbx
            dlam_ref[...] += dlam
            return carry

        lax.fori_loop(0, t // cr, p3, 0)

        def p4(c, carry):
            t0 = pl.multiple_of(c * cr, cr)
            ext = dxa_s[pl.ds(t0, cr + 8), :]
            n = cr + 8
            ahead = [ext[:cr, :]] + [pltpu.roll(ext, n - s, 0)[:cr, :] for s in (1, 2, 3)]
            dax_ref[pl.ds(t0, cr), :] = w[3] * ahead[0] + w[2] * ahead[1] + w[1] * ahead[2] + w[0] * ahead[3]
            taps = _conv_taps(axp[pl.ds(t0, cr + 8), :], cr)
            dxa = ahead[0]
            for k in range(CONV_WIDTH):
                dcw_ref[pl.ds(k, 1), :] += jnp.sum(dxa * taps[CONV_WIDTH - 1 - k], axis=0, keepdims=True)
            dcb_ref[...] += jnp.sum(dxa, axis=0, keepdims=True)
            return carry

        lax.fori_loop(0, t // cr, p4, 0)

    col = lambda off: pl.BlockSpec((t, LANES), lambda b: (0, off + b))
    vec = pl.BlockSpec((1, LANES), lambda b: (0, b))
    mat = pl.BlockSpec((None, LANES, LANES), lambda b: (b, 0, 0))
    cws = pl.BlockSpec((CONV_WIDTH, LANES), lambda b: (0, b))
    sds = jax.ShapeDtypeStruct
    return pl.pallas_call(
        body, grid=(nb,),
        in_specs=[col(0), col(nb), col(0), col(0), cws, vec, mat, vec, mat, vec, vec],
        out_specs=[col(0), col(0), cws, vec, mat, vec, mat, vec, vec],
        out_shape=[sds((t, D_A), F32), sds((t, D_A), F32), sds((CONV_WIDTH, D_A), F32), sds((1, D_A), F32),
                   sds((nb, LANES, LANES), F32), sds((1, D_A), F32), sds((nb, LANES, LANES), F32), sds((1, D_A), F32),
                   sds((1, D_A), F32)],
        scratch_shapes=[pltpu.VMEM((t + 8, LANES), F32), pltpu.VMEM((t + 8, LANES), F32), pltpu.VMEM((t, LANES), F32),
                        pltpu.VMEM((t, LANES), F32), pltpu.VMEM((t, LANES), F32), pltpu.VMEM((t + 8, LANES), F32)],
        compiler_params=pltpu.CompilerParams(dimension_semantics=("arbitrary",)), name="rglru_bwd")(
            z, z, h, dout, cw, cb, wa, ba, wx, bx, lam)


def _fgate_fwd(z, bf):
    t = z.shape[0]

    def body(zf_ref, bf_ref, c_ref):
        c_ref[...] = -_softplus(-(zf_ref[...] + bf_ref[...]))

        def step(i, c):
            c = c + c_ref[pl.ds(i, 1), :]
            c_ref[pl.ds(i, 1), :] = c
            return c

        lax.fori_loop(0, t, step, jnp.zeros((1, LANES), F32), unroll=8)

    return pl.pallas_call(
        body, grid=(1,), in_specs=[pl.BlockSpec((t, LANES), lambda i: (0, Z_F // LANES)), pl.BlockSpec((1, LANES), lambda i: (0, 0))],
        out_specs=pl.BlockSpec((t, LANES), lambda i: (0, 0)), out_shape=jax.ShapeDtypeStruct((t, LANES), F32),
        compiler_params=pltpu.CompilerParams(dimension_semantics=("arbitrary",)), name="fgate_fwd")(z, bf)


def _fgate_bwd(z, bf, dc):
    t = z.shape[0]

    def body(zf_ref, bf_ref, dc_ref, dz_ref, db_ref):
        def step(k, carry):
            i = t - 1 - k
            carry = carry + dc_ref[pl.ds(i, 1), :]
            dz_ref[pl.ds(i, 1), :] = carry
            return carry

        lax.fori_loop(0, t, step, jnp.zeros((1, LANES), F32), unroll=8)
        dz = dz_ref[...] * _sig(-(zf_ref[...] + bf_ref[...]))
        dz_ref[...] = dz
        db_ref[...] = jnp.sum(dz, axis=0, keepdims=True)

    return pl.pallas_call(
        body, grid=(1,),
        in_specs=[pl.BlockSpec((t, LANES), lambda i: (0, Z_F // LANES)), pl.BlockSpec((1, LANES), lambda i: (0, 0)),
                  pl.BlockSpec((t, LANES), lambda i: (0, 0))],
        out_specs=[pl.BlockSpec((t, LANES), lambda i: (0, 0)), pl.BlockSpec((1, LANES), lambda i: (0, 0))],
        out_shape=[jax.ShapeDtypeStruct((t, LANES), F32), jax.ShapeDtypeStruct((1, LANES), F32)],
        compiler_params=pltpu.CompilerParams(dimension_semantics=("arbitrary",)), name="fgate_bwd")(z, bf, dc)


def _cast_rows(src_ref, dst_ref, t, rows, fn):
    def cp(c, carry):
        r0 = pl.multiple_of(c * rows, rows)
        dst_ref[pl.ds(r0, rows), :] = fn(src_ref[pl.ds(r0, rows), :]).astype(dst_ref.dtype)
        return carry

    lax.fori_loop(0, t // rows, cp, 0)


def _attn_fwd(z, crow):
    t = z.shape[0]
    tq = _tile(t, 256)
    nq = t // tq
    scale = HEAD_DIM ** -0.5

    def body(q_ref, k_ref, v_ref, cr_ref, o_ref, lse_ref, kb_s, vb_s):
        lane = lax.broadcasted_iota(jnp.int32, (1, LANES), 1)
        hmask = [(lane // HEAD_DIM) == hh for hh in range(2)]
        tri = lax.broadcasted_iota(jnp.int32, (tq, tq), 0) >= lax.broadcasted_iota(jnp.int32, (tq, tq), 1)
        _cast_rows(k_ref, kb_s, t, tq, lambda v: v)
        _cast_rows(v_ref, vb_s, t, tq, lambda v: v)

        def qblock(i, carry):
            q0 = pl.multiple_of(i * tq, tq)
            qv = q_ref[pl.ds(q0, tq), :] * scale
            qa = [jnp.where(hmask[hh], qv, 0.0).astype(BF16) for hh in range(2)]

            def update(st, k0, masked):
                kb = kb_s[pl.ds(k0, tq), :]
                vb = vb_s[pl.ds(k0, tq), :]
                new = []
                for hh in range(2):
                    m, l, acc = st[hh]
                    s = _dot(qa[hh], kb, NT) - cr_ref[hh, :, pl.ds(k0, tq)]
                    if masked:
                        s = jnp.where(tri, s, NEG)
                    m_new = jnp.maximum(m, jnp.max(s, axis=-1, keepdims=True))
                    p = jnp.exp(s - m_new)
                    corr = jnp.exp(m - m_new)
                    new.append((m_new, corr * l + jnp.sum(p, axis=-1, keepdims=True), corr * acc + _dot(p, vb, NN)))
                return tuple(new)

            one = (jnp.full((tq, 1), NEG, F32), jnp.zeros((tq, 1), F32), jnp.zeros((tq, LANES), F32))
            st = lax.fori_loop(0, i, lambda j, st: update(st, pl.multiple_of(j * tq, tq), False), (one, one))
            st = update(st, q0, True)
            o_ref[pl.ds(q0, tq), :] = jnp.where(hmask[0], st[0][2] / st[0][1], st[1][2] / st[1][1])
            for hh in range(2):
                lse_ref[hh, pl.ds(q0, tq), :] = st[hh][0] + jnp.log(st[hh][1])
            return carry

        lax.fori_loop(0, nq, qblock, 0)

    base = 2 * D_A // LANES
    nh = D_B // LANES
    col = lambda off: pl.BlockSpec((t, LANES), lambda p: (0, off + p))
    return pl.pallas_call(
        body, grid=(nh,),
        in_specs=[col(base), col(base + nh), col(base + 2 * nh), pl.BlockSpec((2, 1, t), lambda p: (p, 0, 0))],
        out_specs=[col(0), pl.BlockSpec((2, t, 1), lambda p: (p, 0, 0))],
        out_shape=[jax.ShapeDtypeStruct((t, D_B), F32), jax.ShapeDtypeStruct((N_HEADS, t, 1), F32)],
        scratch_shapes=[pltpu.VMEM((t, LANES), BF16), pltpu.VMEM((t, LANES), BF16)],
        compiler_params=pltpu.CompilerParams(dimension_semantics=("parallel",)), name="attn_fwd")(z, z, z, crow)


def _attn_bwd(z, crow, lse, do):
    t = z.shape[0]
    tq = _tile(t, 256)
    nq = t // tq
    scale = HEAD_DIM ** -0.5

    def body(q_ref, k_ref, v_ref, cr_ref, lse_ref, do_ref, dq_ref, dk_ref, dv_ref, dc_ref, qa_s, da_s, kb_s, vb_s, dl_s):
        lane = lax.broadcasted_iota(jnp.int32, (1, LANES), 1)
        hmask = [(lane // HEAD_DIM) == hh for hh in range(2)]
        tri = lax.broadcasted_iota(jnp.int32, (tq, tq), 0) >= lax.broadcasted_iota(jnp.int32, (tq, tq), 1)
        _cast_rows(k_ref, kb_s, t, tq, lambda v: v)
        _cast_rows(v_ref, vb_s, t, tq, lambda v: v)
        for hh in range(2):
            _cast_rows(q_ref, qa_s.at[hh], t, tq, lambda v, hh=hh: jnp.where(hmask[hh], v * scale, 0.0))
            _cast_rows(do_ref, da_s.at[hh], t, tq, lambda v, hh=hh: jnp.where(hmask[hh], v, 0.0))
        _cast_rows(q_ref, dq_ref, t, tq, lambda v: jnp.zeros_like(v))

        def probs(hh, q0, k0, masked):
            s = _dot(qa_s[hh, pl.ds(q0, tq), :], kb_s[pl.ds(k0, tq), :], NT) - cr_ref[hh, :, pl.ds(k0, tq)]
            p = jnp.exp(s - lse_ref[hh, pl.ds(q0, tq), :])
            if masked:
                p = jnp.where(tri, p, 0.0)
            return p, _dot(da_s[hh, pl.ds(q0, tq), :], vb_s[pl.ds(k0, tq), :], NT)

        def pre(i, carry):
            q0 = pl.multiple_of(i * tq, tq)

            def add(k0, masked, acc):
                res = []
                for hh in range(2):
                    p, dp = probs(hh, q0, k0, masked)
                    res.append(acc[hh] + jnp.sum(p * dp, axis=-1, keepdims=True))
                return tuple(res)

            zcol = jnp.zeros((tq, 1), F32)
            acc = lax.fori_loop(0, i, lambda j, acc: add(pl.multiple_of(j * tq, tq), False, acc), (zcol, zcol))
            acc = add(q0, True, acc)
            for hh in range(2):
                dl_s[hh, pl.ds(q0, tq), :] = acc[hh]
            return carry

        lax.fori_loop(0, nq, pre, 0)

        def kblock(j, carry):
            k0 = pl.multiple_of(j * tq, tq)
            kb = kb_s[pl.ds(k0, tq), :]

            def upd(q0, masked, st):
                dk, dv, dc = st[0], st[1], [st[2], st[3]]
                dqs = []
                for hh in range(2):
                    p, dp = probs(hh, q0, k0, masked)
                    ds = p * (dp - dl_s[hh, pl.ds(q0, tq), :])
                    dv = dv + _dot(p, da_s[hh, pl.ds(q0, tq), :], TN)
                    dk = dk + _dot(ds, qa_s[hh, pl.ds(q0, tq), :], TN)
                    dqs.append(_dot(ds, kb, NN))
                    dc[hh] = dc[hh] - jnp.sum(ds, axis=0, keepdims=True)
                dq_ref[pl.ds(q0, tq), :] += jnp.where(hmask[0], dqs[0], dqs[1]) * scale
                return dk, dv, dc[0], dc[1]

            zero = jnp.zeros((tq, LANES), F32)
            zrow = jnp.zeros((1, tq), F32)
            st = upd(k0, True, (zero, zero, zrow, zrow))
            st = lax.fori_loop(j + 1, nq, lambda i, st: upd(pl.multiple_of(i * tq, tq), False, st), st)
            dk_ref[pl.ds(k0, tq), :] = st[0]
            dv_ref[pl.ds(k0, tq), :] = st[1]
            for hh in range(2):
                dc_ref[hh, :, pl.ds(k0, tq)] = st[2 + hh]
            return carry

        lax.fori_loop(0, nq, kblock, 0)

    base = 2 * D_A // LANES
    nh = D_B // LANES
    col = lambda off: pl.BlockSpec((t, LANES), lambda p: (0, off + p))
    ccs = pl.BlockSpec((2, t, 1), lambda p: (p, 0, 0))
    crs = pl.BlockSpec((2, 1, t), lambda p: (p, 0, 0))
    return pl.pallas_call(
        body, grid=(nh,), in_specs=[col(base), col(base + nh), col(base + 2 * nh), crs, ccs, col(0)],
        out_specs=[col(0), col(0), col(0), crs],
        out_shape=[jax.ShapeDtypeStruct((t, D_B), F32)] * 3 + [jax.ShapeDtypeStruct((N_HEADS, 1, t), F32)],
        scratch_shapes=[pltpu.VMEM((2, t, LANES), BF16), pltpu.VMEM((2, t, LANES), BF16), pltpu.VMEM((t, LANES), BF16),
                        pltpu.VMEM((t, LANES), BF16), pltpu.VMEM((2, t, 1), F32)],
        compiler_params=pltpu.CompilerParams(dimension_semantics=("parallel",)), name="attn_bwd")(
            z, z, z, crow, lse, do)


def _s5_disc(a_re, a_im, log_dt, b_re, b_im):
    dt = jnp.exp(log_dt)
    mag = jnp.exp(a_re * dt)
    ar = mag * jnp.cos(a_im * dt)
    ai = mag * jnp.sin(a_im * dt)
    den = a_re * a_re + a_im * a_im
    kr = ((ar - 1.0) * a_re + ai * a_im) / den
    ki = (ai * a_re - (ar - 1.0) * a_im) / den
    kr3, ki3 = kr[:, None, :], ki[:, None, :]
    return ar, ai, kr3 * b_re - ki3 * b_im, kr3 * b_im + ki3 * b_re


def _s5_prep(a_re, a_im, log_dt, b_re, b_im):
    g, p = a_re.shape
    gc = b_re.shape[1]

    def body(*refs):
        res = _s5_disc(*[r[...] for r in refs[:5]])
        for o_ref, v in zip(refs[5:], res):
            o_ref[...] = v

    sds = jax.ShapeDtypeStruct
    return pl.pallas_call(body, out_shape=[sds((g, p), F32), sds((g, p), F32), sds((g, gc, p), F32), sds((g, gc, p), F32)],
                          name="s5_prep")(a_re, a_im, log_dt, b_re, b_im)


def _s5_prep_bwd(a_re, a_im, log_dt, b_re, b_im, d_ar, d_ai, d_br, d_bi):
    ins = (a_re, a_im, log_dt, b_re, b_im)

    def body(*refs):
        vals = [r[...] for r in refs[:5]]
        cts = tuple(r[...] for r in refs[5:9])
        _, pull = jax.vjp(_s5_disc, *vals)
        for o_ref, v in zip(refs[9:], pull(cts)):
            o_ref[...] = v

    return pl.pallas_call(body, out_shape=[jax.ShapeDtypeStruct(a.shape, F32) for a in ins], name="s5_prep_bwd")(
        *ins, d_ar, d_ai, d_br, d_bi)


def _s5_scan_rows(t, ar, ai, hr_s, hi_s, off, reverse):
    n = ar.shape[1]

    def step(k, carry):
        cr, ci = carry
        i = off + (t - 1 - k if reverse else k)
        if reverse:
            nr = ar * cr + ai * ci + hr_s[pl.ds(i, 1), :]
            ni = ar * ci - ai * cr + hi_s[pl.ds(i, 1), :]
        else:
            nr = ar * cr - ai * ci + hr_s[pl.ds(i, 1), :]
            ni = ar * ci + ai * cr + hi_s[pl.ds(i, 1), :]
        hr_s[pl.ds(i, 1), :] = nr
        hi_s[pl.ds(i, 1), :] = ni
        return nr, ni

    zero = jnp.zeros((1, n), F32)
    lax.fori_loop(0, t, step, (zero, zero), unroll=4)


def _s5_fwd(z, bd_re, bd_im, ab_re, ab_im, cd_re, cd_im, dvec):
    t = z.shape[0]
    cr = _tile(t, 256)
    ns = N_STATE // 2

    def body(u_ref, br_ref, bi_ref, ar_ref, ai_ref, cre_ref, cim_ref, d_ref, y_ref, hr_s, hi_s):
        def p1(c, carry):
            t0 = pl.multiple_of(c * cr, cr)
            u = u_ref[pl.ds(t0, cr), :]
            hr_s[pl.ds(t0, cr), :] = _dot(u, br_ref[...], NN)
            hi_s[pl.ds(t0, cr), :] = _dot(u, bi_ref[...], NN)
            return carry

        lax.fori_loop(0, t // cr, p1, 0)
        _s5_scan_rows(t, ar_ref[...], ai_ref[...], hr_s, hi_s, 0, False)

        def p3(c, carry):
            t0 = pl.multiple_of(c * cr, cr)
            y_ref[pl.ds(t0, cr), :] = (_dot(hr_s[pl.ds(t0, cr), :], cre_ref[...], NN)
                                       - _dot(hi_s[pl.ds(t0, cr), :], cim_ref[...], NN)
                                       + d_ref[...] * u_ref[pl.ds(t0, cr), :])
            return carry

        lax.fori_loop(0, t // cr, p3, 0)

    blk = lambda r, c: pl.BlockSpec((None, r, c), lambda b: (b, 0, 0))
    return pl.pallas_call(
        body, grid=(2,),
        in_specs=[pl.BlockSpec((t, LANES), lambda b: (0, Z_U // LANES + b)), blk(LANES, ns), blk(LANES, ns), blk(1, ns),
                  blk(1, ns), blk(ns, LANES), blk(ns, LANES), pl.BlockSpec((1, LANES), lambda b: (0, b))],
        out_specs=pl.BlockSpec((t, LANES), lambda b: (0, b)), out_shape=jax.ShapeDtypeStruct((t, D_C), F32),
        scratch_shapes=[pltpu.VMEM((t, ns), F32), pltpu.VMEM((t, ns), F32)],
        compiler_params=pltpu.CompilerParams(dimension_semantics=("arbitrary",)), name="s5_fwd")(
            z, bd_re, bd_im, ab_re, ab_im, cd_re, cd_im, dvec)


def _s5_bwd(z, dy, bd_re, bd_im, ab_re, ab_im, cd_re, cd_im, dvec):
    t = z.shape[0]
    cr = _tile(t, 256)
    ns = N_STATE // 2

    def body(u_ref, dy_ref, br_ref, bi_ref, ar_ref, ai_ref, cre_ref, cim_ref, d_ref,
             du_ref, dbr_ref, dbi_ref, dar_ref, dai_ref, dcre_ref, dcim_ref, dd_ref, hr_s, hi_s, gr_s, gi_s):
        zero8 = jnp.zeros((8, ns), F32)
        hr_s[pl.ds(0, 8), :] = zero8
        hi_s[pl.ds(0, 8), :] = zero8
        for ref in (dbr_ref, dbi_ref, dar_ref, dai_ref, dcre_ref, dcim_ref, dd_ref):
            ref[...] = jnp.zeros(ref.shape, F32)

        def p1(c, carry):
            t0 = pl.multiple_of(c * cr, cr)
            u = u_ref[pl.ds(t0, cr), :]
            hr_s[pl.ds(t0 + 8, cr), :] = _dot(u, br_ref[...], NN)
            hi_s[pl.ds(t0 + 8, cr), :] = _dot(u, bi_ref[...], NN)
            return carry

        lax.fori_loop(0, t // cr, p1, 0)
        _s5_scan_rows(t, ar_ref[...], ai_ref[...], hr_s, hi_s, 8, False)

        def p3(c, carry):
            t0 = pl.multiple_of(c * cr, cr)
            dyv = dy_ref[pl.ds(t0, cr), :]
            u = u_ref[pl.ds(t0, cr), :]
            gr_s[pl.ds(t0, cr), :] = _dot(dyv, cre_ref[...], NT)
            gi_s[pl.ds(t0, cr), :] = -_dot(dyv, cim_ref[...], NT)
            dcre_ref[...] += _dot(hr_s[pl.ds(t0 + 8, cr), :], dyv, TN)
            dcim_ref[...] -= _dot(hi_s[pl.ds(t0 + 8, cr), :], dyv, TN)
            dd_ref[...] += jnp.sum(dyv * u, axis=0, keepdims=True)
            du_ref[pl.ds(t0, cr), :] = dyv * d_ref[...]
            return carry

        lax.fori_loop(0, t // cr, p3, 0)
        _s5_scan_rows(t, ar_ref[...], ai_ref[...], gr_s, gi_s, 0, True)

        def p5(c, carry):
            t0 = pl.multiple_of(c * cr, cr)
            u = u_ref[pl.ds(t0, cr), :]
            gr = gr_s[pl.ds(t0, cr), :]
            gi = gi_s[pl.ds(t0, cr), :]
            dbr_ref[...] += _dot(u, gr, TN)
            dbi_ref[...] += _dot(u, gi, TN)
            du_ref[pl.ds(t0, cr), :] += _dot(gr, br_ref[...], NT) + _dot(gi, bi_ref[...], NT)
            hpr = pltpu.roll(hr_s[pl.ds(t0, cr + 8), :], 1, 0)[8:, :]
            hpi = pltpu.roll(hi_s[pl.ds(t0, cr + 8), :], 1, 0)[8:, :]
            dar_ref[...] += jnp.sum(gr * hpr + gi * hpi, axis=0, keepdims=True)
            dai_ref[...] += jnp.sum(gi * hpr - gr * hpi, axis=0, keepdims=True)
            return carry

        lax.fori_loop(0, t // cr, p5, 0)

    blk = lambda r, c: pl.BlockSpec((None, r, c), lambda b: (b, 0, 0))
    ucol = pl.BlockSpec((t, LANES), lambda b: (0, Z_U // LANES + b))
    ycol = pl.BlockSpec((t, LANES), lambda b: (0, b))
    dsp = pl.BlockSpec((1, LANES), lambda b: (0, b))
    sds = jax.ShapeDtypeStruct
    return pl.pallas_call(
        body, grid=(2,),
        in_specs=[ucol, ycol, blk(LANES, ns), blk(LANES, ns), blk(1, ns), blk(1, ns), blk(ns, LANES), blk(ns, LANES), dsp],
        out_specs=[ycol, blk(LANES, ns), blk(LANES, ns), blk(1, ns), blk(1, ns), blk(ns, LANES), blk(ns, LANES), dsp],
        out_shape=[sds((t, D_C), F32), sds((2, LANES, ns), F32), sds((2, LANES, ns), F32), sds((2, 1, ns), F32),
                   sds((2, 1, ns), F32), sds((2, ns, LANES), F32), sds((2, ns, LANES), F32), sds((1, D_C), F32)],
        scratch_shapes=[pltpu.VMEM((t + 8, ns), F32), pltpu.VMEM((t + 8, ns), F32), pltpu.VMEM((t, ns), F32),
                        pltpu.VMEM((t, ns), F32)],
        compiler_params=pltpu.CompilerParams(dimension_semantics=("arbitrary",)), name="s5_bwd")(
            z, dy, bd_re, bd_im, ab_re, ab_im, cd_re, cd_im, dvec)


def _mix_out(out_a, out_b, yc, x1, ga, gb, gc, wglu, wout, ln_g, ln_b):
    yg = _gelu(yc)
    out_c = yg * _sig(_bdot(yg, wglu))
    o = jnp.concatenate([_rms(out_a, ga), _rms(out_b, gb), _rms(out_c, gc)], axis=-1)
    return (_ln(ALPHA * x1 + _bdot(o, wout), ln_g, ln_b),)


def _ln_only(pre, g, b):
    return (_ln(pre, g, b),)


def _loss_head(y, target, tm):
    t, d = y.shape

    def body(y_ref, t_ref, dy_ref, l_ref):
        i = pl.program_id(0)
        e = y_ref[...] - t_ref[...]
        dy_ref[...] = e * (1.0 / d)
        part = 0.5 * jnp.sum(jnp.sum(e * e, axis=-1, keepdims=True) * (1.0 / d), axis=0, keepdims=True)
        row = jnp.where(lax.broadcasted_iota(jnp.int32, (1, LANES), 1) == 0, part, 0.0)

        @pl.when(i == 0)
        def _():
            l_ref[...] = row

        @pl.when(i > 0)
        def _():
            l_ref[...] += row

    spec = pl.BlockSpec((tm, d), lambda i: (i, 0))
    return pl.pallas_call(
        body, grid=(t // tm,), in_specs=[spec, spec], out_specs=[spec, pl.BlockSpec((1, LANES), lambda i: (0, 0))],
        out_shape=[jax.ShapeDtypeStruct((t, d), F32), jax.ShapeDtypeStruct((1, LANES), F32)],
        compiler_params=pltpu.CompilerParams(dimension_semantics=("arbitrary",)), name="loss_head")(y, target)


def _adamw(w, g, m, v, name):
    r, c = w.shape
    tr = r
    for cand in (512, 256, 352, 128):
        if r % cand == 0:
            tr = cand
            break

    def body(w_ref, g_ref, m_ref, v_ref, d_ref, nm_ref, nv_ref):
        gv = g_ref[...]
        mn = ADAM_B1 * m_ref[...] + (1.0 - ADAM_B1) * gv
        vn = ADAM_B2 * v_ref[...] + (1.0 - ADAM_B2) * (gv * gv)
        m_hat = mn / (1.0 - ADAM_B1 ** ADAM_STEP)
        v_hat = vn / (1.0 - ADAM_B2 ** ADAM_STEP)
        d_ref[...] = -ADAM_LR * (m_hat / (jnp.sqrt(v_hat) + ADAM_EPS) + ADAM_WD * w_ref[...])
        nm_ref[...] = mn
        nv_ref[...] = vn

    spec = pl.BlockSpec((tr, c), lambda i: (i, 0))
    return pl.pallas_call(
        body, grid=(r // tr,), in_specs=[spec] * 4, out_specs=[spec] * 3,
        out_shape=[jax.ShapeDtypeStruct((r, c), F32)] * 3,
        compiler_params=pltpu.CompilerParams(dimension_semantics=("parallel",)), name=name)(w, g, m, v)


def _exchange(name, ins, out_shapes, aliases, plan, n_remote, n_local):
    n_in = len(ins)
    n_out = len(out_shapes)

    def body(*refs):
        in_refs, out_refs = refs[:n_in], refs[n_in:n_in + n_out]
        send_sems, recv_sems, loc_sems = refs[n_in + n_out:]
        x, y, c = lax.axis_index("x"), lax.axis_index("y"), lax.axis_index("c")
        remote, local = plan(x, y, c, in_refs, out_refs)
        loc = [pltpu.make_async_copy(s, d, loc_sems.at[k]) for k, (s, d) in enumerate(local)]
        rem = [pltpu.make_async_remote_copy(src_ref=s, dst_ref=d, send_sem=send_sems.at[k], recv_sem=recv_sems.at[k],
                                            device_id=peer, device_id_type=MESH) for k, (s, d, peer) in enumerate(remote)]
        for cp in loc + rem:
            cp.start()
        for cp in loc + rem:
            cp.wait()

    hbm = pl.BlockSpec(memory_space=pl.ANY)
    return pl.pallas_call(
        body, in_specs=[hbm] * n_in, out_specs=[hbm] * n_out, out_shape=out_shapes, input_output_aliases=aliases,
        scratch_shapes=[pltpu.SemaphoreType.DMA((n_remote,)), pltpu.SemaphoreType.DMA((n_remote,)),
                        pltpu.SemaphoreType.DMA((max(n_local, 1),))],
        compiler_params=pltpu.CompilerParams(has_side_effects=True), name=name)(*ins)[0]


def _all_gather_rows(shard):
    r, w = shard.shape
    out = jax.ShapeDtypeStruct((8, r, w), shard.dtype)

    def plan1(x, y, c, ins, outs):
        me = 4 * x + 2 * y + c
        dst = outs[0].at[me]
        chips = [(1 - x, y), (x, 1 - y), (1 - x, 1 - y)]
        return [(ins[0], dst, (px, py, c)) for px, py in chips], [(ins[0], dst)]

    g = _exchange("ag_chips", [shard], [out], {}, plan1, 3, 1)

    def plan2(x, y, c, ins, outs):
        rem = []
        for q in range(4):
            blk = outs[0].at[2 * q + c]
            rem.append((blk, blk, (x, y, 1 - c)))
        return rem, []

    return _exchange("ag_sibling", [g], [out], {0: 0}, plan2, 4, 0)


def _pair_add(a, b, idx, name, out_dtype):
    _, _, r, w = a.shape
    tr = _tile(r, 512) if r % 512 == 0 else _tile(r, 256)

    def body(i_ref, a_ref, b_ref, o_ref):
        o_ref[...] = (a_ref[...].astype(F32) + b_ref[...].astype(F32)).astype(o_ref.dtype)

    grid_spec = pltpu.PrefetchScalarGridSpec(
        num_scalar_prefetch=1, grid=(4, r // tr),
        in_specs=[pl.BlockSpec((None, None, tr, w), lambda q, i, s: (q, s[0], i, 0)),
                  pl.BlockSpec((None, tr, w), lambda q, i, s: (q, i, 0))],
        out_specs=pl.BlockSpec((None, tr, w), lambda q, i, s: (q, i, 0)))
    return pl.pallas_call(body, grid_spec=grid_spec, out_shape=jax.ShapeDtypeStruct((4, r, w), out_dtype),
                          compiler_params=pltpu.CompilerParams(dimension_semantics=("parallel", "parallel")), name=name)(
                              idx, a, b)


def _quad_add(p, rb, idx, name):
    _, r, w = p.shape
    tr = _tile(r, 512) if r % 512 == 0 else _tile(r, 256)

    def body(i_ref, p_ref, r0, r1, r2, o_ref):
        o_ref[...] = ((p_ref[...].astype(F32) + r0[...].astype(F32)) + r1[...].astype(F32)) + r2[...].astype(F32)

    grid_spec = pltpu.PrefetchScalarGridSpec(
        num_scalar_prefetch=1, grid=(r // tr,),
        in_specs=[pl.BlockSpec((None, tr, w), lambda i, s: (s[0], i, 0))]
        + [pl.BlockSpec((None, tr, w), functools.partial(lambda i, s, k: (k, i, 0), k=k)) for k in range(3)],
        out_specs=pl.BlockSpec((tr, w), lambda i, s: (i, 0)))
    return pl.pallas_call(body, grid_spec=grid_spec, out_shape=jax.ShapeDtypeStruct((r, w), F32),
                          compiler_params=pltpu.CompilerParams(dimension_semantics=("parallel",)), name=name)(
                              idx, p, rb, rb, rb)


def _reduce_scatter_rows(full, x, y, c):
    _, r, w = full.shape

    def plan_a(px, py, pc, ins, outs):
        rem = [(ins[0].at[2 * q + (1 - pc)], outs[0].at[q], (px, py, 1 - pc)) for q in range(4)]
        return rem, []

    ra = _exchange("rs_sibling", [full], [jax.ShapeDtypeStruct((4, r, w), full.dtype)], {}, plan_a, 4, 0)
    part = _pair_add(full.reshape(4, 2, r, w), ra, jnp.reshape(c, (1,)).astype(jnp.int32), "rs_add_sibling", full.dtype)

    def plan_b(px, py, pc, ins, outs):
        rels = [(1, 0), (0, 1), (1, 1)]
        rem = []
        for k, (dx, dy) in enumerate(rels):
            tx, ty = px ^ dx, py ^ dy
            rem.append((ins[0].at[2 * tx + ty], outs[0].at[k], (tx, ty, pc)))
        return rem, []

    rb = _exchange("rs_chips", [part], [jax.ShapeDtypeStruct((3, r, w), full.dtype)], {}, plan_b, 3, 0)
    return _quad_add(part, rb, jnp.reshape(2 * x + y, (1,)).astype(jnp.int32), "rs_add_chips")


def _all_reduce_small(buf):
    r, w = buf.shape

    def plan(x, y, c, ins, outs):
        me = 4 * x + 2 * y + c
        dst = outs[0].at[me]
        rem = []
        for rel in range(1, 8):
            dx, dy, dc = rel >> 2, (rel >> 1) & 1, rel & 1
            rem.append((ins[0], dst, (x ^ dx, y ^ dy, c ^ dc)))
        return rem, [(ins[0], dst)]

    allb = _exchange("ar_small", [buf], [jax.ShapeDtypeStruct((8, r, w), buf.dtype)], {}, plan, 7, 1)

    def body(a_ref, o_ref):
        s = a_ref[0]
        for k in range(1, 8):
            s = s + a_ref[k]
        o_ref[...] = s

    return pl.pallas_call(body, out_shape=jax.ShapeDtypeStruct((r, w), F32), name="ar_small_sum")(allb)


def _pad_rows(a, rows):
    return jnp.pad(a, ((0, rows - a.shape[0]), (0, 0)))


def _pack_small(arrs):
    total = sum(int(a.size) for a in arrs)
    fill = (-total) % (8 * LANES)
    return jnp.concatenate([a.reshape(-1) for a in arrs] + [jnp.zeros((fill,), F32)]).reshape(-1, LANES)


def _unpack_small(buf, shapes):
    out, off = [], 0
    flat = buf.reshape(-1)
    for s in shapes:
        n = int(np.prod(s))
        out.append(flat[off:off + n].reshape(s))
        off += n
    return out


def _block_diag(blocks, nb):
    m, r, c = blocks.shape
    n = m // nb
    eye = jnp.eye(n, dtype=blocks.dtype)
    return (blocks.reshape(nb, n, r, 1, c) * eye[None, :, None, :, None]).reshape(nb, n * r, n * c)


def _diag_blocks(dense, n):
    nb = dense.shape[0]
    r, c = dense.shape[1] // n, dense.shape[2] // n
    eye = jnp.eye(n, dtype=dense.dtype)
    return jnp.sum(dense.reshape(nb, n, r, n, c) * eye[None, :, None, :, None], axis=3).reshape(nb * n, r, c)


def kernel(x, ffn1_w_gate, ffn1_w_up, ffn1_w_down, ln1_g, ln1_b, w_in, conv_w, conv_b, rg_w_a, rg_b_a, rg_w_x, rg_b_x, rg_lambda, fox_b_f, s5_a_re, s5_a_im, s5_log_dt, s5_b_re, s5_b_im, s5_c_re, s5_c_im, s5_d, s5_w_glu, mix_norm_g, w_out, ln2_g, ln2_b, ffn2_w_gate, ffn2_w_up, ffn2_w_down, ln3_g, ln3_b, loss_target, m_ffn1_w_gate, m_ffn1_w_up, m_ffn1_w_down, m_ln1_g, m_ln1_b, m_w_in, m_conv_w, m_conv_b, m_rg_w_a, m_rg_b_a, m_rg_w_x, m_rg_b_x, m_rg_lambda, m_fox_b_f, m_s5_a_re, m_s5_a_im, m_s5_log_dt, m_s5_b_re, m_s5_b_im, m_s5_c_re, m_s5_c_im, m_s5_d, m_s5_w_glu, m_mix_norm_g, m_w_out, m_ln2_g, m_ln2_b, m_ffn2_w_gate, m_ffn2_w_up, m_ffn2_w_down, m_ln3_g, m_ln3_b, v_ffn1_w_gate, v_ffn1_w_up, v_ffn1_w_down, v_ln1_g, v_ln1_b, v_w_in, v_conv_w, v_conv_b, v_rg_w_a, v_rg_b_a, v_rg_w_x, v_rg_b_x, v_rg_lambda, v_fox_b_f, v_s5_a_re, v_s5_a_im, v_s5_log_dt, v_s5_b_re, v_s5_b_im, v_s5_c_re, v_s5_c_im, v_s5_d, v_s5_w_glu, v_mix_norm_g, v_w_out, v_ln2_g, v_ln2_b, v_ffn2_w_gate, v_ffn2_w_up, v_ffn2_w_down, v_ln3_g, v_ln3_b):
    a = dict(locals())
    w = {n: a[n] for n in WEIGHTS}
    t, d = x.shape[1], x.shape[2]
    f = ffn1_w_down.shape[1] * 8
    fs, ds = f // 8, d // 8
    mx, my, mc = lax.axis_index("x"), lax.axis_index("y"), lax.axis_index("c")
    me = 4 * mx + 2 * my + mc
    tm = _tile(t, 512)
    tf = f // 2
    win_rows = ds * Z_W // d

    def shard_rows(l):
        wi = w['w_in'][l]
        win_p = jnp.concatenate([wi[:, :Z_F + N_HEADS], jnp.zeros((ds, Z_U - Z_F - N_HEADS), F32), wi[:, Z_F + N_HEADS:]], axis=1)
        conv_bits = lax.bitcast_convert_type(w['conv_w'][l], BF16).reshape(1, -1)
        segs = [w['ffn1_w_gate'][l].T, w['ffn1_w_up'][l].T, w['ffn1_w_down'][l],
                w['ffn2_w_gate'][l].T, w['ffn2_w_up'][l].T, w['ffn2_w_down'][l],
                win_p.reshape(win_rows, d), w['w_out'][l], _pad_rows(w['s5_w_glu'][l].reshape(-1, d), 16)]
        segs = [s.astype(BF16) for s in segs]
        segs.append(_pad_rows(jnp.pad(conv_bits, ((0, 0), (0, d - conv_bits.shape[1]))), 16))
        return jnp.concatenate(segs, axis=0)

    seg_rows = [fs] * 6 + [win_rows, ds, 16, 16]
    seg_off = [int(v) for v in np.cumsum([0] + seg_rows)]
    rl = seg_off[-1]
    gathered = _all_gather_rows(jnp.concatenate([shard_rows(l) for l in range(DEPTH)], axis=0))

    def seg(l, k):
        return gathered[:, l * rl + seg_off[k]: l * rl + seg_off[k + 1]]

    xs = x[0]
    saved = []
    cur = xs
    for l in range(DEPTH):
        g1, u1, d1, g2, u2, d2 = [seg(l, k).reshape(f, d) for k in range(6)]
        win = seg(l, 6).reshape(d, Z_W)
        wout = seg(l, 7).reshape(d, d).astype(F32)
        wglu = seg(l, 8)[:, :D_C * D_C // (8 * d)].reshape(D_C, D_C).astype(F32)
        conv_full = lax.bitcast_convert_type(seg(l, 9)[:, 0, :2 * CONV_WIDTH * D_A // 8].reshape(8, CONV_WIDTH, D_A // 8, 2), F32)
        conv_full = conv_full.transpose(1, 0, 2).reshape(CONV_WIDTH, D_A)
        row = lambda n: w[n][l].reshape(1, -1)
        wa = _block_diag(w['rg_w_a'][l], 3)
        wx = _block_diag(w['rg_w_x'][l], 3)
        bf = jnp.pad(row('fox_b_f'), ((0, 0), (0, LANES - N_HEADS)))
        s5p = (w['s5_a_re'][l], w['s5_a_im'][l], w['s5_log_dt'][l].reshape(-1, 1),
               w['s5_b_re'][l].transpose(0, 2, 1), w['s5_b_im'][l].transpose(0, 2, 1))
        ab_re, ab_im, bb_re, bb_im = _s5_prep(*s5p)
        bd_re, bd_im = _block_diag(bb_re, 2), _block_diag(bb_im, 2)
        cd_re = _block_diag(w['s5_c_re'][l].transpose(0, 2, 1), 2)
        cd_im = _block_diag(w['s5_c_im'][l].transpose(0, 2, 1), 2)
        abr, abi = ab_re.reshape(2, 1, N_STATE // 2), ab_im.reshape(2, 1, N_STATE // 2)
        gm = row('mix_norm_g')
        ga, gb, gc = gm[:, :D_A], gm[:, D_A:D_A + D_B], gm[:, D_A + D_B:]

        x0 = cur
        x1, pre1, gs1, us1 = _ffn_fwd(x0, g1, u1, d1, row('ln1_g'), row('ln1_b'), tm, tf)
        z = _mm(x1, win, 'nn', F32, tm, Z_W, d, "mix_in")
        out_a, h_a = _rg_fwd(z, conv_full, row('conv_b'), wa, row('rg_b_a'), wx, row('rg_b_x'), row('rg_lambda'))
        cs = _fgate_fwd(z, bf)
        crow = cs[:, :N_HEADS].T.reshape(N_HEADS, 1, t)
        out_b, lse = _attn_fwd(z, crow)
        yc = _s5_fwd(z, bd_re, bd_im, abr, abi, cd_re, cd_im, row('s5_d'))
        mix_params = [ga, gb, gc, wglu, wout, row('ln2_g'), row('ln2_b')]
        (x2,) = _rowwise(_mix_out, [out_a, out_b, yc, x1], mix_params, [(d, F32)], _tile(t, 256), "mix_out")
        x3, pre3, gs2, us2 = _ffn_fwd(x2, g2, u2, d2, row('ln3_g'), row('ln3_b'), tm, tf)
        saved.append(dict(x0=x0, x1=x1, pre1=pre1, gs1=gs1, us1=us1, z=z, out_a=out_a, h_a=h_a, crow=crow,
                          out_b=out_b, lse=lse, yc=yc, x2=x2, pre3=pre3, gs2=gs2, us2=us2, mix_params=mix_params,
                          ffn1=(g1, u1, d1), ffn2=(g2, u2, d2), win=win, conv_full=conv_full, wa=wa, wx=wx, bf=bf, s5p=s5p,
                          s5m=(bd_re, bd_im, abr, abi, cd_re, cd_im)))
        cur = x3

    dy, loss_row = _loss_head(cur, loss_target[0], tm)
    loss = lax.psum(loss_row[0, 0], ("x", "y", "c"))

    big_grads = [None] * DEPTH
    small_grads = {}
    grad_rows = [fs] * 6 + [win_rows, ds, 16, 16]
    for l in reversed(range(DEPTH)):
        s = saved[l]
        row = lambda n: w[n][l].reshape(1, -1)
        wg_tiles = dict(tm=tf, tn=d, tk=tm)

        def ffn_back(dyv, pre, gs, us, wts, xin, ln_g, ln_b):
            (dpre,), (dlg, dlb) = _rowwise_vjp(_ln_only, [pre], [ln_g, ln_b], [dyv], _tile(t, 256), "ln_bwd")
            dx, dg, du, hh = _ffn_bwd(dpre, gs, us, *wts, tm, tf)
            dwg = _mm(dg, xin, 'tn', BF16, name="ffn_dw_gate", **wg_tiles)
            dwu = _mm(du, xin, 'tn', BF16, name="ffn_dw_up", **wg_tiles)
            dwd = _mm(hh, dpre, 'tn', BF16, name="ffn_dw_down", **wg_tiles)
            return dx, dwg, dwu, dwd, dlg, dlb

        dx2, dwg2, dwu2, dwd2, dl3g, dl3b = ffn_back(dy, s['pre3'], s['gs2'], s['us2'], s['ffn2'], s['x2'], row('ln3_g'), row('ln3_b'))
        (d_oa, d_ob, d_yc, d_x1), (dga, dgb, dgc, dwglu, dwout, dl2g, dl2b) = _rowwise_vjp(
            _mix_out, [s['out_a'], s['out_b'], s['yc'], s['x1']], s['mix_params'], [dx2], _tile(t, 256), "mix_out_bwd")
        (d_ax, d_ag, dcw, dcb, dwa, dba, dwx, dbx, dlam) = _rg_bwd(
            s['z'], s['h_a'], d_oa, s['conv_full'], row('conv_b'), s['wa'], row('rg_b_a'), s['wx'], row('rg_b_x'), row('rg_lambda'))
        dq, dk, dv, dcrow = _attn_bwd(s['z'], s['crow'], s['lse'], d_ob)
        dc_pad = jnp.pad(dcrow.reshape(N_HEADS, t).T, ((0, 0), (0, LANES - N_HEADS)))
        dzf, dbf = _fgate_bwd(s['z'], s['bf'], dc_pad)
        du_c, dbd_re, dbd_im, dabr, dabi, dcd_re, dcd_im, dd = _s5_bwd(s['z'], d_yc, *s['s5m'], row('s5_d'))
        dz = jnp.concatenate([d_ax, d_ag, dq, dk, dv, dzf, du_c], axis=1)
        dx1 = _mm(dz, s['win'], 'nt', F32, tm, d, Z_W, "mix_in_dx", add=d_x1)
        dwin = _mm(s['x1'], dz, 'tn', BF16, d, Z_W, tm, "mix_in_dw")
        dx0, dwg1, dwu1, dwd1, dl1g, dl1b = ffn_back(dx1, s['pre1'], s['gs1'], s['us1'], s['ffn1'], s['x0'], row('ln1_g'), row('ln1_b'))
        dy = dx0

        dbb_re, dbb_im = _diag_blocks(dbd_re, N_GROUPS // 2), _diag_blocks(dbd_im, N_GROUPS // 2)
        dcm_re, dcm_im = _diag_blocks(dcd_re, N_GROUPS // 2), _diag_blocks(dcd_im, N_GROUPS // 2)
        da_re, da_im, dlog_dt, db_re, db_im = _s5_prep_bwd(*s['s5p'], dabr.reshape(N_GROUPS, C_STATE), dabi.reshape(N_GROUPS, C_STATE), dbb_re, dbb_im)

        sg = dict(ln1_g=dl1g, ln1_b=dl1b, conv_w=dcw, conv_b=dcb, rg_w_a=_diag_blocks(dwa, 2), rg_b_a=dba,
                  rg_w_x=_diag_blocks(dwx, 2), rg_b_x=dbx, rg_lambda=dlam, fox_b_f=dbf[:, :N_HEADS],
                  s5_a_re=da_re, s5_a_im=da_im, s5_log_dt=dlog_dt, s5_b_re=db_re.transpose(0, 2, 1), s5_b_im=db_im.transpose(0, 2, 1),
                  s5_c_re=dcm_re.transpose(0, 2, 1), s5_c_im=dcm_im.transpose(0, 2, 1), s5_d=dd,
                  mix_norm_g=jnp.concatenate([dga, dgb, dgc], axis=1), ln2_g=dl2g, ln2_b=dl2b, ln3_g=dl3g, ln3_b=dl3b)
        small_grads[l] = sg
        segs = [dwg1, dwu1, dwd1, dwg2, dwu2, dwd2, dwin.reshape(8 * win_rows, d), dwout.astype(BF16),
                jnp.pad(dwglu.astype(BF16).reshape(8, -1, d), ((0, 0), (0, 32 - D_C * D_C // (8 * d)), (0, 0))).reshape(-1, d)]
        big_grads[l] = jnp.concatenate([sgm.reshape(8, -1, d) for sgm in segs], axis=1)

    grad_x = dy.reshape(x.shape)
    goff = [int(v) for v in np.cumsum([0] + grad_rows)]
    rg = goff[-1]
    gsh = _reduce_scatter_rows(jnp.concatenate(big_grads, axis=1), mx, my, mc)

    def gseg(l, k):
        return gsh[l * rg + goff[k]: l * rg + goff[k + 1]]

    grads = {}
    for k, n in enumerate(['ffn1_w_gate', 'ffn1_w_up', 'ffn1_w_down', 'ffn2_w_gate', 'ffn2_w_up', 'ffn2_w_down']):
        grads[n] = jnp.stack([gseg(l, k).T if 'down' not in n else gseg(l, k) for l in range(DEPTH)])
    gwin = jnp.stack([gseg(l, 6).reshape(ds, Z_W) for l in range(DEPTH)])
    grads['w_in'] = jnp.concatenate([gwin[:, :, :Z_F + N_HEADS], gwin[:, :, Z_U:]], axis=2)
    grads['w_out'] = jnp.stack([gseg(l, 7) for l in range(DEPTH)])
    grads['s5_w_glu'] = jnp.stack([gseg(l, 8)[:D_C * D_C // (8 * d)].reshape(D_C // 8, D_C) for l in range(DEPTH)])

    small_names = SMALL + ['conv_w']
    small_shapes = [((DEPTH, CONV_WIDTH, D_A) if n == 'conv_w' else w[n].shape) for n in small_names]
    summed = _all_reduce_small(_pack_small([small_grads[l][n] for n in small_names for l in range(DEPTH)]))
    for n, g in zip(small_names, _unpack_small(summed, small_shapes)):
        grads[n] = g
    grads['conv_w'] = lax.dynamic_slice_in_dim(grads['conv_w'], me * (D_A // 8), D_A // 8, axis=2)

    delta, new_m, new_v = {}, {}, {}
    for n in BIG + ['conv_w']:
        sh = w[n].shape
        two = lambda v: v.reshape(-1, sh[-1])
        dl, nm, nv = _adamw(two(w[n]), two(grads[n]), two(a['m_' + n]), two(a['v_' + n]), "adamw_" + n)
        delta[n], new_m[n], new_v[n] = dl.reshape(sh), nm.reshape(sh), nv.reshape(sh)
    sshapes = [w[n].shape for n in SMALL]
    w_small = _pack_small([w[n] for n in SMALL])
    dl, nm, nv = _adamw(w_small, summed[:w_small.shape[0]], _pack_small([a['m_' + n] for n in SMALL]),
                        _pack_small([a['v_' + n] for n in SMALL]), "adamw_small")
    for n, v1, v2, v3 in zip(SMALL, _unpack_small(dl, sshapes), _unpack_small(nm, sshapes), _unpack_small(nv, sshapes)):
        delta[n], new_m[n], new_v[n] = v1, v2, v3

    return (loss, grad_x, *[grads[n] for n in WEIGHTS], *[delta[n] for n in WEIGHTS], *[new_m[n] for n in WEIGHTS],
            *[new_v[n] for n in WEIGHTS])
```

```python
import functools
import math

import jax
import jax.numpy as jnp
import numpy as np
from jax import lax
from jax.experimental import pallas as pl
from jax.experimental.pallas import tpu as pltpu

F32 = jnp.float32
BF16 = jnp.bfloat16
MESH = pl.DeviceIdType.MESH

DEPTH = 2
ALPHA = (2 * DEPTH) ** 0.25
LN_EPS = 1e-5
RMS_EPS = 1e-6
RG_C = 8.0
CONV_WIDTH = 4
HEAD_DIM = 64
C_GROUP = 16
C_STATE = 64
D_A = 384
D_B = 384
D_C = 256
N_HEADS = D_B // HEAD_DIM
N_GROUPS = D_C // C_GROUP
N_STATE = N_GROUPS * C_STATE
Z_F = 2 * D_A + 3 * D_B
Z_U = Z_F + 128
Z_W = Z_U + D_C
N_IN = Z_F + N_HEADS + D_C
ADAM_LR, ADAM_B1, ADAM_B2, ADAM_EPS, ADAM_WD, ADAM_STEP = 0.001, 0.9, 0.999, 1e-08, 0.01, 10
LANES = 128
NEG = -1e30

WEIGHTS = ['ffn1_w_gate', 'ffn1_w_up', 'ffn1_w_down', 'ln1_g', 'ln1_b', 'w_in', 'conv_w', 'conv_b', 'rg_w_a', 'rg_b_a',
           'rg_w_x', 'rg_b_x', 'rg_lambda', 'fox_b_f', 's5_a_re', 's5_a_im', 's5_log_dt', 's5_b_re', 's5_b_im', 's5_c_re',
           's5_c_im', 's5_d', 's5_w_glu', 'mix_norm_g', 'w_out', 'ln2_g', 'ln2_b', 'ffn2_w_gate', 'ffn2_w_up', 'ffn2_w_down',
           'ln3_g', 'ln3_b']
BIG = ['ffn1_w_gate', 'ffn1_w_up', 'ffn1_w_down', 'w_in', 's5_w_glu', 'w_out', 'ffn2_w_gate', 'ffn2_w_up', 'ffn2_w_down']
SMALL = [n for n in WEIGHTS if n not in BIG and n != 'conv_w']


def _sig(x):
    return 1.0 / (1.0 + jnp.exp(-x))


def _gelu(x):
    return 0.5 * x * (1.0 + jnp.tanh(math.sqrt(2.0 / math.pi) * (x + 0.044715 * (x * x * x))))


def _softplus(x):
    return jnp.maximum(x, 0.0) + jnp.log(1.0 + jnp.exp(jnp.minimum(x, -x)))


def _dot(a, b, dims):
    return lax.dot_general(a.astype(BF16), b.astype(BF16), (dims, ((), ())), preferred_element_type=F32)


NN = ((1,), (0,))
NT = ((1,), (1,))
TN = ((0,), (0,))


@jax.custom_vjp
def _bdot(a, w):
    return _dot(a, w, NN)


def _bdot_fwd(a, w):
    return _dot(a, w, NN), (a, w)


def _bdot_bwd(res, ct):
    a, w = res
    return _dot(ct, w, NT), _dot(a, ct, TN)


_bdot.defvjp(_bdot_fwd, _bdot_bwd)


def _ln(pre, g, b):
    mu = jnp.mean(pre, axis=-1, keepdims=True)
    xc = pre - mu
    var = jnp.mean(xc * xc, axis=-1, keepdims=True)
    return xc * lax.rsqrt(var + LN_EPS) * g + b


def _rms(x, g):
    return x * lax.rsqrt(jnp.mean(x * x, axis=-1, keepdims=True) + RMS_EPS) * g


def _tile(n, want):
    return want if n % want == 0 else n


class _Carry:
    def __init__(self, ins, outs, aliases, plan, n):
        self.ins, self.outs, self.aliases, self.plan, self.n = list(ins), list(outs), dict(aliases), plan, n


def _join(a, b):
    na, ma = len(a.ins), len(a.outs)

    def plan(x, y, c, ins, outs):
        return a.plan(x, y, c, ins[:na], outs[:ma]) + b.plan(x, y, c, ins[na:], outs[ma:])

    aliases = dict(a.aliases)
    aliases.update({na + i: ma + j for i, j in b.aliases.items()})
    return _Carry(a.ins + b.ins, a.outs + b.outs, aliases, plan, a.n + b.n)


def _copies(carry, cins, couts, send, recv):
    x, y, c = lax.axis_index("x"), lax.axis_index("y"), lax.axis_index("c")
    res = []
    for k, (s, d, peer) in enumerate(carry.plan(x, y, c, cins, couts)):
        if peer is None:
            res.append(pltpu.make_async_copy(s, d, send.at[k]))
        else:
            res.append(pltpu.make_async_remote_copy(src_ref=s, dst_ref=d, send_sem=send.at[k], recv_sem=recv.at[k],
                                                    device_id=peer, device_id_type=MESH))
    return res


def _call(body, grid, in_specs, out_specs, out_shape, scratch, semantics, name, args, carry=None):
    n_in, n_out, n_scr = len(in_specs), len(out_specs), len(scratch)
    if carry is None:
        res = pl.pallas_call(body, grid=grid, in_specs=in_specs, out_specs=out_specs, out_shape=out_shape,
                             scratch_shapes=scratch, compiler_params=pltpu.CompilerParams(dimension_semantics=semantics),
                             name=name)(*args)
        return list(res), []
    nci, nco = len(carry.ins), len(carry.outs)

    def wrapped(*refs):
        o0 = n_in + nci
        s0 = o0 + n_out + nco
        cins, couts = refs[n_in:o0], refs[o0 + n_out:s0]
        send, recv = refs[s0 + n_scr:]
        first = functools.reduce(jnp.logical_and, [pl.program_id(k) == 0 for k in range(len(grid))])
        last = functools.reduce(jnp.logical_and, [pl.program_id(k) == grid[k] - 1 for k in range(len(grid))])

        @pl.when(first)
        def _():
            for cp in _copies(carry, cins, couts, send, recv):
                cp.start()

        body(*refs[:n_in], *refs[o0:o0 + n_out], *refs[s0:s0 + n_scr])

        @pl.when(last)
        def _():
            for cp in _copies(carry, cins, couts, send, recv):
                cp.wait()

    hbm = pl.BlockSpec(memory_space=pl.ANY)
    res = pl.pallas_call(
        wrapped, grid=grid, in_specs=list(in_specs) + [hbm] * nci, out_specs=list(out_specs) + [hbm] * nco,
        out_shape=list(out_shape) + carry.outs, scratch_shapes=list(scratch) + [pltpu.SemaphoreType.DMA((carry.n,))] * 2,
        input_output_aliases={n_in + i: n_out + j for i, j in carry.aliases.items()},
        compiler_params=pltpu.CompilerParams(dimension_semantics=("arbitrary",) * len(grid), has_side_effects=True),
        name=name)(*args, *carry.ins)
    return list(res[:n_out]), list(res[n_out:])


def _run(carry, name):
    nci, nco = len(carry.ins), len(carry.outs)

    def body(*refs):
        cps = _copies(carry, refs[:nci], refs[nci:nci + nco], refs[-2], refs[-1])
        for cp in cps:
            cp.start()
        for cp in cps:
            cp.wait()

    hbm = pl.BlockSpec(memory_space=pl.ANY)
    return pl.pallas_call(
        body, in_specs=[hbm] * nci, out_specs=[hbm] * nco, out_shape=carry.outs, input_output_aliases=carry.aliases,
        scratch_shapes=[pltpu.SemaphoreType.DMA((carry.n,))] * 2, compiler_params=pltpu.CompilerParams(has_side_effects=True),
        name=name)(*carry.ins)


def _ag_chips(shard):
    def plan(x, y, c, ins, outs):
        dst = outs[0].at[4 * x + 2 * y + c]
        return [(ins[0], dst, None)] + [(ins[0], dst, (px, py, c)) for px, py in ((1 - x, y), (x, 1 - y), (1 - x, 1 - y))]

    return _Carry([shard], [jax.ShapeDtypeStruct((8,) + shard.shape, shard.dtype)], {}, plan, 4)


def _ag_sibling(g):
    def plan(x, y, c, ins, outs):
        return [(outs[0].at[2 * q + c], outs[0].at[2 * q + c], (x, y, 1 - c)) for q in range(4)]

    return _Carry([g], [jax.ShapeDtypeStruct(g.shape, g.dtype)], {0: 0}, plan, 4)


def _rs_sibling(full):
    def plan(x, y, c, ins, outs):
        return [(ins[0].at[2 * q + (1 - c)], outs[0].at[q], (x, y, 1 - c)) for q in range(4)]

    return _Carry([full], [jax.ShapeDtypeStruct((4,) + full.shape[1:], full.dtype)], {}, plan, 4)


def _rs_chips(part):
    def plan(x, y, c, ins, outs):
        res = []
        for k, (dx, dy) in enumerate(((1, 0), (0, 1), (1, 1))):
            tx, ty = x ^ dx, y ^ dy
            res.append((ins[0].at[2 * tx + ty], outs[0].at[k], (tx, ty, c)))
        return res

    return _Carry([part], [jax.ShapeDtypeStruct((3,) + part.shape[1:], part.dtype)], {}, plan, 3)


def _mm(a, b, dims, out_dtype, tm, tn, tk, name, add=None):
    if dims == 'nn':
        (m, k), n = a.shape, b.shape[1]
        a_spec = pl.BlockSpec((tm, tk), lambda i, j, q: (i, q))
        b_spec = pl.BlockSpec((tk, tn), lambda i, j, q: (q, j))
        dn = NN
    elif dims == 'nt':
        (m, k), n = a.shape, b.shape[0]
        a_spec = pl.BlockSpec((tm, tk), lambda i, j, q: (i, q))
        b_spec = pl.BlockSpec((tn, tk), lambda i, j, q: (j, q))
        dn = NT
    else:
        (k, m), n = a.shape, b.shape[1]
        a_spec = pl.BlockSpec((tk, tm), lambda i, j, q: (q, i))
        b_spec = pl.BlockSpec((tk, tn), lambda i, j, q: (q, j))
        dn = TN
    nk = k // tk
    o_spec = pl.BlockSpec((tm, tn), lambda i, j, q: (i, j))

    def body(*refs):
        if add is None:
            a_ref, b_ref, o_ref, acc_ref = refs
        else:
            a_ref, b_ref, add_ref, o_ref, acc_ref = refs
        q = pl.program_id(2)
        part = _dot(a_ref[...], b_ref[...], dn)

        @pl.when(q == 0)
        def _():
            acc_ref[...] = part

        @pl.when(q > 0)
        def _():
            acc_ref[...] += part

        @pl.when(q == nk - 1)
        def _():
            r = acc_ref[...]
            if add is not None:
                r = r + add_ref[...]
            o_ref[...] = r.astype(o_ref.dtype)

    ins = [a, b] + ([] if add is None else [add])
    specs = [a_spec, b_spec] + ([] if add is None else [o_spec])
    return pl.pallas_call(
        body, grid=(m // tm, n // tn, nk), in_specs=specs, out_specs=o_spec,
        out_shape=jax.ShapeDtypeStruct((m, n), out_dtype), scratch_shapes=[pltpu.VMEM((tm, tn), F32)],
        compiler_params=pltpu.CompilerParams(dimension_semantics=("parallel", "parallel", "arbitrary")), name=name)(*ins)


def _rowwise(fn, rows, params, outs, tm, name):
    t = rows[0].shape[0]
    nr, npar = len(rows), len(params)

    def body(*refs):
        r = [x[...] for x in refs[:nr]]
        p = [x[...] for x in refs[nr:nr + npar]]
        res = fn(*r, *p)
        for o_ref, o in zip(refs[nr + npar:], res):
            o_ref[...] = o.astype(o_ref.dtype)

    in_specs = ([pl.BlockSpec((tm, a.shape[1]), lambda i: (i, 0)) for a in rows]
                + [pl.BlockSpec(p.shape, lambda i: (0, 0)) for p in params])
    return pl.pallas_call(
        body, grid=(t // tm,), in_specs=in_specs,
        out_specs=[pl.BlockSpec((tm, c), lambda i: (i, 0)) for c, _ in outs],
        out_shape=[jax.ShapeDtypeStruct((t, c), d) for c, d in outs],
        compiler_params=pltpu.CompilerParams(dimension_semantics=("parallel",)), name=name)(*rows, *params)


def _rowwise_vjp(fn, rows, params, cots, tm, name, carry=None):
    t = rows[0].shape[0]
    nr, npar, nc = len(rows), len(params), len(cots)

    def body(*refs):
        r = [x[...] for x in refs[:nr]]
        p = [x[...] for x in refs[nr:nr + npar]]
        c = [x[...] for x in refs[nr + npar:nr + npar + nc]]
        o_refs = refs[nr + npar + nc:]
        _, pull = jax.vjp(fn, *r, *p)
        grads = pull(tuple(c))
        for o_ref, g in zip(o_refs[:nr], grads[:nr]):
            o_ref[...] = g
        i = pl.program_id(0)

        @pl.when(i == 0)
        def _():
            for o_ref, g in zip(o_refs[nr:], grads[nr:]):
                o_ref[...] = g

        @pl.when(i > 0)
        def _():
            for o_ref, g in zip(o_refs[nr:], grads[nr:]):
                o_ref[...] += g

    row_spec = lambda a: pl.BlockSpec((tm, a.shape[1]), lambda i: (i, 0))
    par_spec = lambda p: pl.BlockSpec(p.shape, lambda i: (0, 0))
    res, cres = _call(
        body, (t // tm,),
        [row_spec(a) for a in rows] + [par_spec(p) for p in params] + [row_spec(a) for a in cots],
        [row_spec(a) for a in rows] + [par_spec(p) for p in params],
        [jax.ShapeDtypeStruct(a.shape, F32) for a in rows] + [jax.ShapeDtypeStruct(p.shape, F32) for p in params],
        [], ("arbitrary",), name, [*rows, *params, *cots], carry)
    return res[:nr], res[nr:], cres


def _ffn_fwd(x, wgt, wut, wd, ln_g, ln_b, tm, tf, carry=None):
    t, d = x.shape
    f = wgt.shape[0]
    nj = f // tf

    def body(x_ref, wg_ref, wu_ref, wd_ref, g_ref, b_ref, y_ref, pre_ref, gs_ref, us_ref, acc_ref):
        j = pl.program_id(1)
        xv = x_ref[...]
        xb = xv.astype(BF16)
        g = _dot(xb, wg_ref[...], NT)
        u = _dot(xb, wu_ref[...], NT)
        gs_ref[...] = g.astype(BF16)
        us_ref[...] = u.astype(BF16)
        part = _dot(g * _sig(g) * u, wd_ref[...], NN)

        @pl.when(j == 0)
        def _():
            acc_ref[...] = part

        @pl.when(j > 0)
        def _():
            acc_ref[...] += part

        @pl.when(j == nj - 1)
        def _():
            pre = ALPHA * xv + 0.5 * acc_ref[...]
            pre_ref[...] = pre
            y_ref[...] = _ln(pre, g_ref[...], b_ref[...])

    w_spec = pl.BlockSpec((tf, d), lambda i, j: (j, 0))
    x_spec = pl.BlockSpec((tm, d), lambda i, j: (i, 0))
    v_spec = pl.BlockSpec((1, d), lambda i, j: (0, 0))
    h_spec = pl.BlockSpec((tm, tf), lambda i, j: (i, j))
    return _call(
        body, (t // tm, nj), [x_spec, w_spec, w_spec, w_spec, v_spec, v_spec], [x_spec, x_spec, h_spec, h_spec],
        [jax.ShapeDtypeStruct((t, d), F32), jax.ShapeDtypeStruct((t, d), F32),
         jax.ShapeDtypeStruct((t, f), BF16), jax.ShapeDtypeStruct((t, f), BF16)],
        [pltpu.VMEM((tm, d), F32)], ("parallel", "arbitrary"), "ffn_fwd", [x, wgt, wut, wd, ln_g, ln_b], carry)


def _ffn_bwd(dpre, gs, us, wgt, wut, wd, tm, tf, carry=None):
    t, d = dpre.shape
    f = wgt.shape[0]
    nj = f // tf

    def body(dp_ref, gs_ref, us_ref, wg_ref, wu_ref, wd_ref, dx_ref, dg_ref, du_ref, hh_ref, acc_ref):
        j = pl.program_id(1)
        dp = dp_ref[...]
        dh = _dot(0.5 * dp, wd_ref[...], NT)
        g = gs_ref[...].astype(F32)
        u = us_ref[...].astype(F32)
        s = _sig(g)
        sl = g * s
        dg = (dh * u * (s * (1.0 + g * (1.0 - s)))).astype(BF16)
        du = (dh * sl).astype(BF16)
        dg_ref[...] = dg
        du_ref[...] = du
        hh_ref[...] = (0.5 * sl * u).astype(BF16)
        part = _dot(dg, wg_ref[...], NN) + _dot(du, wu_ref[...], NN)

        @pl.when(j == 0)
        def _():
            acc_ref[...] = part

        @pl.when(j > 0)
        def _():
            acc_ref[...] += part

        @pl.when(j == nj - 1)
        def _():
            dx_ref[...] = ALPHA * dp + acc_ref[...]

    w_spec = pl.BlockSpec((tf, d), lambda i, j: (j, 0))
    x_spec = pl.BlockSpec((tm, d), lambda i, j: (i, 0))
    h_spec = pl.BlockSpec((tm, tf), lambda i, j: (i, j))
    return _call(
        body, (t // tm, nj), [x_spec, h_spec, h_spec, w_spec, w_spec, w_spec], [x_spec, h_spec, h_spec, h_spec],
        [jax.ShapeDtypeStruct((t, d), F32)] + [jax.ShapeDtypeStruct((t, f), BF16)] * 3,
        [pltpu.VMEM((tm, d), F32)], ("parallel", "arbitrary"), "ffn_bwd", [dpre, gs, us, wgt, wut, wd], carry)


def _rg_local(xa, wa, ba, wx, bx, lam):
    r = _sig(_bdot(xa, wa) + ba)
    i = _sig(_bdot(xa, wx) + bx)
    log_a = -RG_C * r * _softplus(-lam)
    a = jnp.exp(log_a)
    mult = jnp.sqrt(-jnp.tanh(log_a) * (a * a + 1.0))
    return a, mult * (i * xa)


def _conv_taps(ext, n):
    return [ext[8:, :]] + [pltpu.roll(ext, s, 0)[8:, :] for s in (1, 2, 3)]


def _rg_fwd(z, cw, cb, wa, ba, wx, bx, lam):
    t = z.shape[0]
    cr = _tile(t, 256)
    nb = D_A // LANES

    def body(ax_ref, ag_ref, cw_ref, cb_ref, wa_ref, ba_ref, wx_ref, bx_ref, lam_ref, out_ref, h_ref, axp, a_s, b_s):
        axp[pl.ds(0, 8), :] = jnp.zeros((8, LANES), F32)
        pltpu.sync_copy(ax_ref, axp.at[pl.ds(8, t)])
        w = [cw_ref[pl.ds(k, 1), :] for k in range(CONV_WIDTH)]

        def chunk(c, carry):
            t0 = pl.multiple_of(c * cr, cr)
            taps = _conv_taps(axp[pl.ds(t0, cr + 8), :], cr)
            xa = cb_ref[...] + w[3] * taps[0] + w[2] * taps[1] + w[1] * taps[2] + w[0] * taps[3]
            a, gated = _rg_local(xa, wa_ref[...], ba_ref[...], wx_ref[...], bx_ref[...], lam_ref[...])
            a_s[pl.ds(t0, cr), :] = a
            b_s[pl.ds(t0, cr), :] = gated
            return carry

        lax.fori_loop(0, t // cr, chunk, 0)

        def step(i, h):
            h = a_s[pl.ds(i, 1), :] * h + b_s[pl.ds(i, 1), :]
            h_ref[pl.ds(i, 1), :] = h
            return h

        lax.fori_loop(0, t, step, jnp.zeros((1, LANES), F32), unroll=8)

        def fin(c, carry):
            t0 = pl.multiple_of(c * cr, cr)
            out_ref[pl.ds(t0, cr), :] = _gelu(ag_ref[pl.ds(t0, cr), :]) * h_ref[pl.ds(t0, cr), :]
            return carry

        lax.fori_loop(0, t // cr, fin, 0)

    col = lambda off: pl.BlockSpec((t, LANES), lambda b: (0, off + b))
    vec = pl.BlockSpec((1, LANES), lambda b: (0, b))
    mat = pl.BlockSpec((None, LANES, LANES), lambda b: (b, 0, 0))
    return pl.pallas_call(
        body, grid=(nb,),
        in_specs=[col(0), col(nb), pl.BlockSpec((CONV_WIDTH, LANES), lambda b: (0, b)), vec, mat, vec, mat, vec, vec],
        out_specs=[col(0), col(0)],
        out_shape=[jax.ShapeDtypeStruct((t, D_A), F32), jax.ShapeDtypeStruct((t, D_A), F32)],
        scratch_shapes=[pltpu.VMEM((t + 8, LANES), F32), pltpu.VMEM((t, LANES), F32), pltpu.VMEM((t, LANES), F32)],
        compiler_params=pltpu.CompilerParams(dimension_semantics=("arbitrary",)), name="rglru_fwd")(
            z, z, cw, cb, wa, ba, wx, bx, lam)


def _rg_bwd(z, h, dout, cw, cb, wa, ba, wx, bx, lam):
    t = z.shape[0]
    cr = _tile(t, 256)
    nb = D_A // LANES

    def body(ax_ref, ag_ref, h_ref, do_ref, cw_ref, cb_ref, wa_ref, ba_ref, wx_ref, bx_ref, lam_ref,
             dax_ref, dag_ref, dcw_ref, dcb_ref, dwa_ref, dba_ref, dwx_ref, dbx_ref, dlam_ref,
             axp, hp, xa_s, a_s, g_s, dxa_s):
        zero8 = jnp.zeros((8, LANES), F32)
        axp[pl.ds(0, 8), :] = zero8
        hp[pl.ds(0, 8), :] = zero8
        dxa_s[pl.ds(t, 8), :] = zero8
        pltpu.sync_copy(ax_ref, axp.at[pl.ds(8, t)])
        pltpu.sync_copy(h_ref, hp.at[pl.ds(8, t)])
        w = [cw_ref[pl.ds(k, 1), :] for k in range(CONV_WIDTH)]
        for ref in (dcw_ref, dcb_ref, dwa_ref, dba_ref, dwx_ref, dbx_ref, dlam_ref):
            ref[...] = jnp.zeros(ref.shape, F32)

        def p1(c, carry):
            t0 = pl.multiple_of(c * cr, cr)
            taps = _conv_taps(axp[pl.ds(t0, cr + 8), :], cr)
            xa = cb_ref[...] + w[3] * taps[0] + w[2] * taps[1] + w[1] * taps[2] + w[0] * taps[3]
            a, _ = _rg_local(xa, wa_ref[...], ba_ref[...], wx_ref[...], bx_ref[...], lam_ref[...])
            xa_s[pl.ds(t0, cr), :] = xa
            a_s[pl.ds(t0, cr), :] = a
            ag = ag_ref[pl.ds(t0, cr), :]
            dov = do_ref[pl.ds(t0, cr), :]
            gel, pull = jax.vjp(_gelu, ag)
            g_s[pl.ds(t0, cr), :] = dov * gel
            dag_ref[pl.ds(t0, cr), :] = pull(dov * h_ref[pl.ds(t0, cr), :])[0]
            return carry

        lax.fori_loop(0, t // cr, p1, 0)

        def step(k, carry):
            i = t - 1 - k
            g = g_s[pl.ds(i, 1), :] + carry
            g_s[pl.ds(i, 1), :] = g
            return g * a_s[pl.ds(i, 1), :]

        lax.fori_loop(0, t, step, jnp.zeros((1, LANES), F32), unroll=8)

        def p3(c, carry):
            t0 = pl.multiple_of(c * cr, cr)
            g = g_s[pl.ds(t0, cr), :]
            h_prev = pltpu.roll(hp[pl.ds(t0, cr + 8), :], 1, 0)[8:, :]
            _, pull = jax.vjp(_rg_local, xa_s[pl.ds(t0, cr), :], wa_ref[...], ba_ref[...], wx_ref[...], bx_ref[...],
                              lam_ref[...])
            dxa, dwa, dba, dwx, dbx, dlam = pull((g * h_prev, g))
            dxa_s[pl.ds(t0, cr), :] = dxa
            dwa_ref[...] += dwa
            dba_ref[...] += dba
            dwx_ref[...] += dwx
            dbx_ref[...] += dbx
            dlam_ref[...] += dlam
            return carry

        lax.fori_loop(0, t // cr, p3, 0)

        def p4(c, carry):
            t0 = pl.multiple_of(c * cr, cr)
            ext = dxa_s[pl.ds(t0, cr + 8), :]
            n = cr + 8
            ahead = [ext[:cr, :]] + [pltpu.roll(ext, n - s, 0)[:cr, :] for s in (1, 2, 3)]
            dax_ref[pl.ds(t0, cr), :] = w[3] * ahead[0] + w[2] * ahead[1] + w[1] * ahead[2] + w[0] * ahead[3]
            taps = _conv_taps(axp[pl.ds(t0, cr + 8), :], cr)
            dxa = ahead[0]
            for k in range(CONV_WIDTH):
                dcw_ref[pl.ds(k, 1), :] += jnp.sum(dxa * taps[CONV_WIDTH - 1 - k], axis=0, keepdims=True)
            dcb_ref[...] += jnp.sum(dxa, axis=0, keepdims=True)
            return carry

        lax.fori_loop(0, t // cr, p4, 0)

    col = lambda off: pl.BlockSpec((t, LANES), lambda b: (0, off + b))
    vec = pl.BlockSpec((1, LANES), lambda b: (0, b))
    mat = pl.BlockSpec((None, LANES, LANES), lambda b: (b, 0, 0))
    cws = pl.BlockSpec((CONV_WIDTH, LANES), lambda b: (0, b))
    sds = jax.ShapeDtypeStruct
    return pl.pallas_call(
        body, grid=(nb,),
        in_specs=[col(0), col(nb), col(0), col(0), cws, vec, mat, vec, mat, vec, vec],
        out_specs=[col(0), col(0), cws, vec, mat, vec, mat, vec, vec],
        out_shape=[sds((t, D_A), F32), sds((t, D_A), F32), sds((CONV_WIDTH, D_A), F32), sds((1, D_A), F32),
                   sds((nb, LANES, LANES), F32), sds((1, D_A), F32), sds((nb, LANES, LANES), F32), sds((1, D_A), F32),
                   sds((1, D_A), F32)],
        scratch_shapes=[pltpu.VMEM((t + 8, LANES), F32), pltpu.VMEM((t + 8, LANES), F32), pltpu.VMEM((t, LANES), F32),
                        pltpu.VMEM((t, LANES), F32), pltpu.VMEM((t, LANES), F32), pltpu.VMEM((t + 8, LANES), F32)],
        compiler_params=pltpu.CompilerParams(dimension_semantics=("arbitrary",)), name="rglru_bwd")(
            z, z, h, dout, cw, cb, wa, ba, wx, bx, lam)


def _fgate_fwd(z, bf):
    t = z.shape[0]

    def body(zf_ref, bf_ref, c_ref):
        c_ref[...] = -_softplus(-(zf_ref[...] + bf_ref[...]))

        def step(i, c):
            c = c + c_ref[pl.ds(i, 1), :]
            c_ref[pl.ds(i, 1), :] = c
            return c

        lax.fori_loop(0, t, step, jnp.zeros((1, LANES), F32), unroll=8)

    return pl.pallas_call(
        body, grid=(1,), in_specs=[pl.BlockSpec((t, LANES), lambda i: (0, Z_F // LANES)), pl.BlockSpec((1, LANES), lambda i: (0, 0))],
        out_specs=pl.BlockSpec((t, LANES), lambda i: (0, 0)), out_shape=jax.ShapeDtypeStruct((t, LANES), F32),
        compiler_params=pltpu.CompilerParams(dimension_semantics=("arbitrary",)), name="fgate_fwd")(z, bf)


def _fgate_bwd(z, bf, dc):
    t = z.shape[0]

    def body(zf_ref, bf_ref, dc_ref, dz_ref, db_ref):
        def step(k, carry):
            i = t - 1 - k
            carry = carry + dc_ref[pl.ds(i, 1), :]
            dz_ref[pl.ds(i, 1), :] = carry
            return carry

        lax.fori_loop(0, t, step, jnp.zeros((1, LANES), F32), unroll=8)
        dz = dz_ref[...] * _sig(-(zf_ref[...] + bf_ref[...]))
        dz_ref[...] = dz
        db_ref[...] = jnp.sum(dz, axis=0, keepdims=True)

    return pl.pallas_call(
        body, grid=(1,),
        in_specs=[pl.BlockSpec((t, LANES), lambda i: (0, Z_F // LANES)), pl.BlockSpec((1, LANES), lambda i: (0, 0)),
                  pl.BlockSpec((t, LANES), lambda i: (0, 0))],
        out_specs=[pl.BlockSpec((t, LANES), lambda i: (0, 0)), pl.BlockSpec((1, LANES), lambda i: (0, 0))],
        out_shape=[jax.ShapeDtypeStruct((t, LANES), F32), jax.ShapeDtypeStruct((1, LANES), F32)],
        compiler_params=pltpu.CompilerParams(dimension_semantics=("arbitrary",)), name="fgate_bwd")(z, bf, dc)


def _cast_rows(src_ref, dst_ref, t, rows, fn):
    def cp(c, carry):
        r0 = pl.multiple_of(c * rows, rows)
        dst_ref[pl.ds(r0, rows), :] = fn(src_ref[pl.ds(r0, rows), :]).astype(dst_ref.dtype)
        return carry

    lax.fori_loop(0, t // rows, cp, 0)


def _attn_fwd(z, crow, carry=None):
    t = z.shape[0]
    tq = _tile(t, 256)
    nq = t // tq
    scale = HEAD_DIM ** -0.5

    def body(q_ref, k_ref, v_ref, cr_ref, o_ref, lse_ref, kb_s, vb_s):
        lane = lax.broadcasted_iota(jnp.int32, (1, LANES), 1)
        hmask = [(lane // HEAD_DIM) == hh for hh in range(2)]
        tri = lax.broadcasted_iota(jnp.int32, (tq, tq), 0) >= lax.broadcasted_iota(jnp.int32, (tq, tq), 1)
        _cast_rows(k_ref, kb_s, t, tq, lambda v: v)
        _cast_rows(v_ref, vb_s, t, tq, lambda v: v)

        def qblock(i, carry):
            q0 = pl.multiple_of(i * tq, tq)
            qv = q_ref[pl.ds(q0, tq), :] * scale
            qa = [jnp.where(hmask[hh], qv, 0.0).astype(BF16) for hh in range(2)]

            def update(st, k0, masked):
                kb = kb_s[pl.ds(k0, tq), :]
                vb = vb_s[pl.ds(k0, tq), :]
                new = []
                for hh in range(2):
                    m, l, acc = st[hh]
                    s = _dot(qa[hh], kb, NT) - cr_ref[hh, :, pl.ds(k0, tq)]
                    if masked:
                        s = jnp.where(tri, s, NEG)
                    m_new = jnp.maximum(m, jnp.max(s, axis=-1, keepdims=True))
                    p = jnp.exp(s - m_new)
                    corr = jnp.exp(m - m_new)
                    new.append((m_new, corr * l + jnp.sum(p, axis=-1, keepdims=True), corr * acc + _dot(p, vb, NN)))
                return tuple(new)

            one = (jnp.full((tq, 1), NEG, F32), jnp.zeros((tq, 1), F32), jnp.zeros((tq, LANES), F32))
            st = lax.fori_loop(0, i, lambda j, st: update(st, pl.multiple_of(j * tq, tq), False), (one, one))
            st = update(st, q0, True)
            o_ref[pl.ds(q0, tq), :] = jnp.where(hmask[0], st[0][2] / st[0][1], st[1][2] / st[1][1])
            for hh in range(2):
                lse_ref[hh, pl.ds(q0, tq), :] = st[hh][0] + jnp.log(st[hh][1])
            return carry

        lax.fori_loop(0, nq, qblock, 0)

    base = 2 * D_A // LANES
    nh = D_B // LANES
    col = lambda off: pl.BlockSpec((t, LANES), lambda p: (0, off + p))
    return _call(
        body, (nh,), [col(base), col(base + nh), col(base + 2 * nh), pl.BlockSpec((2, 1, t), lambda p: (p, 0, 0))],
        [col(0), pl.BlockSpec((2, t, 1), lambda p: (p, 0, 0))],
        [jax.ShapeDtypeStruct((t, D_B), F32), jax.ShapeDtypeStruct((N_HEADS, t, 1), F32)],
        [pltpu.VMEM((t, LANES), BF16), pltpu.VMEM((t, LANES), BF16)], ("parallel",), "attn_fwd", [z, z, z, crow], carry)


def _attn_bwd(z, crow, lse, do, carry=None):
    t = z.shape[0]
    tq = _tile(t, 256)
    nq = t // tq
    scale = HEAD_DIM ** -0.5

    def body(q_ref, k_ref, v_ref, cr_ref, lse_ref, do_ref, dq_ref, dk_ref, dv_ref, dc_ref, qa_s, da_s, kb_s, vb_s, dl_s):
        lane = lax.broadcasted_iota(jnp.int32, (1, LANES), 1)
        hmask = [(lane // HEAD_DIM) == hh for hh in range(2)]
        tri = lax.broadcasted_iota(jnp.int32, (tq, tq), 0) >= lax.broadcasted_iota(jnp.int32, (tq, tq), 1)
        _cast_rows(k_ref, kb_s, t, tq, lambda v: v)
        _cast_rows(v_ref, vb_s, t, tq, lambda v: v)
        for hh in range(2):
            _cast_rows(q_ref, qa_s.at[hh], t, tq, lambda v, hh=hh: jnp.where(hmask[hh], v * scale, 0.0))
            _cast_rows(do_ref, da_s.at[hh], t, tq, lambda v, hh=hh: jnp.where(hmask[hh], v, 0.0))
        _cast_rows(q_ref, dq_ref, t, tq, lambda v: jnp.zeros_like(v))

        def probs(hh, q0, k0, masked):
            s = _dot(qa_s[hh, pl.ds(q0, tq), :], kb_s[pl.ds(k0, tq), :], NT) - cr_ref[hh, :, pl.ds(k0, tq)]
            p = jnp.exp(s - lse_ref[hh, pl.ds(q0, tq), :])
            if masked:
                p = jnp.where(tri, p, 0.0)
            return p, _dot(da_s[hh, pl.ds(q0, tq), :], vb_s[pl.ds(k0, tq), :], NT)

        def pre(i, carry):
            q0 = pl.multiple_of(i * tq, tq)

            def add(k0, masked, acc):
                res = []
                for hh in range(2):
                    p, dp = probs(hh, q0, k0, masked)
                    res.append(acc[hh] + jnp.sum(p * dp, axis=-1, keepdims=True))
                return tuple(res)

            zcol = jnp.zeros((tq, 1), F32)
            acc = lax.fori_loop(0, i, lambda j, acc: add(pl.multiple_of(j * tq, tq), False, acc), (zcol, zcol))
            acc = add(q0, True, acc)
            for hh in range(2):
                dl_s[hh, pl.ds(q0, tq), :] = acc[hh]
            return carry

        lax.fori_loop(0, nq, pre, 0)

        def kblock(j, carry):
            k0 = pl.multiple_of(j * tq, tq)
            kb = kb_s[pl.ds(k0, tq), :]

            def upd(q0, masked, st):
                dk, dv, dc = st[0], st[1], [st[2], st[3]]
                dqs = []
                for hh in range(2):
                    p, dp = probs(hh, q0, k0, masked)
                    ds = p * (dp - dl_s[hh, pl.ds(q0, tq), :])
                    dv = dv + _dot(p, da_s[hh, pl.ds(q0, tq), :], TN)
                    dk = dk + _dot(ds, qa_s[hh, pl.ds(q0, tq), :], TN)
                    dqs.append(_dot(ds, kb, NN))
                    dc[hh] = dc[hh] - jnp.sum(ds, axis=0, keepdims=True)
                dq_ref[pl.ds(q0, tq), :] += jnp.where(hmask[0], dqs[0], dqs[1]) * scale
                return dk, dv, dc[0], dc[1]

            zero = jnp.zeros((tq, LANES), F32)
            zrow = jnp.zeros((1, tq), F32)
            st = upd(k0, True, (zero, zero, zrow, zrow))
            st = lax.fori_loop(j + 1, nq, lambda i, st: upd(pl.multiple_of(i * tq, tq), False, st), st)
            dk_ref[pl.ds(k0, tq), :] = st[0]
            dv_ref[pl.ds(k0, tq), :] = st[1]
            for hh in range(2):
                dc_ref[hh, :, pl.ds(k0, tq)] = st[2 + hh]
            return carry

        lax.fori_loop(0, nq, kblock, 0)

    base = 2 * D_A // LANES
    nh = D_B // LANES
    col = lambda off: pl.BlockSpec((t, LANES), lambda p: (0, off + p))
    ccs = pl.BlockSpec((2, t, 1), lambda p: (p, 0, 0))
    crs = pl.BlockSpec((2, 1, t), lambda p: (p, 0, 0))
    return _call(
        body, (nh,), [col(base), col(base + nh), col(base + 2 * nh), crs, ccs, col(0)], [col(0), col(0), col(0), crs],
        [jax.ShapeDtypeStruct((t, D_B), F32)] * 3 + [jax.ShapeDtypeStruct((N_HEADS, 1, t), F32)],
        [pltpu.VMEM((2, t, LANES), BF16), pltpu.VMEM((2, t, LANES), BF16), pltpu.VMEM((t, LANES), BF16),
         pltpu.VMEM((t, LANES), BF16), pltpu.VMEM((2, t, 1), F32)], ("parallel",), "attn_bwd", [z, z, z, crow, lse, do], carry)


def _s5_disc(a_re, a_im, log_dt, b_re, b_im):
    dt = jnp.exp(log_dt)
    mag = jnp.exp(a_re * dt)
    ar = mag * jnp.cos(a_im * dt)
    ai = mag * jnp.sin(a_im * dt)
    den = a_re * a_re + a_im * a_im
    kr = ((ar - 1.0) * a_re + ai * a_im) / den
    ki = (ai * a_re - (ar - 1.0) * a_im) / den
    kr3, ki3 = kr[:, None, :], ki[:, None, :]
    return ar, ai, kr3 * b_re - ki3 * b_im, kr3 * b_im + ki3 * b_re


def _s5_prep(a_re, a_im, log_dt, b_re, b_im):
    g, p = a_re.shape
    gc = b_re.shape[1]

    def body(*refs):
        res = _s5_disc(*[r[...] for r in refs[:5]])
        for o_ref, v in zip(refs[5:], res):
            o_ref[...] = v

    sds = jax.ShapeDtypeStruct
    return pl.pallas_call(body, out_shape=[sds((g, p), F32), sds((g, p), F32), sds((g, gc, p), F32), sds((g, gc, p), F32)],
                          name="s5_prep")(a_re, a_im, log_dt, b_re, b_im)


def _s5_prep_bwd(a_re, a_im, log_dt, b_re, b_im, d_ar, d_ai, d_br, d_bi):
    ins = (a_re, a_im, log_dt, b_re, b_im)

    def body(*refs):
        vals = [r[...] for r in refs[:5]]
        cts = tuple(r[...] for r in refs[5:9])
        _, pull = jax.vjp(_s5_disc, *vals)
        for o_ref, v in zip(refs[9:], pull(cts)):
            o_ref[...] = v

    return pl.pallas_call(body, out_shape=[jax.ShapeDtypeStruct(a.shape, F32) for a in ins], name="s5_prep_bwd")(
        *ins, d_ar, d_ai, d_br, d_bi)


def _s5_scan_rows(t, ar, ai, hr_s, hi_s, off, reverse):
    n = ar.shape[1]

    def step(k, carry):
        cr, ci = carry
        i = off + (t - 1 - k if reverse else k)
        if reverse:
            nr = ar * cr + ai * ci + hr_s[pl.ds(i, 1), :]
            ni = ar * ci - ai * cr + hi_s[pl.ds(i, 1), :]
        else:
            nr = ar * cr - ai * ci + hr_s[pl.ds(i, 1), :]
            ni = ar * ci + ai * cr + hi_s[pl.ds(i, 1), :]
        hr_s[pl.ds(i, 1), :] = nr
        hi_s[pl.ds(i, 1), :] = ni
        return nr, ni

    zero = jnp.zeros((1, n), F32)
    lax.fori_loop(0, t, step, (zero, zero), unroll=4)


def _s5_fwd(z, bd_re, bd_im, ab_re, ab_im, cd_re, cd_im, dvec, carry=None):
    t = z.shape[0]
    cr = _tile(t, 256)
    ns = N_STATE // 2

    def body(u_ref, br_ref, bi_ref, ar_ref, ai_ref, cre_ref, cim_ref, d_ref, y_ref, hr_s, hi_s):
        def p1(c, carry):
            t0 = pl.multiple_of(c * cr, cr)
            u = u_ref[pl.ds(t0, cr), :]
            hr_s[pl.ds(t0, cr), :] = _dot(u, br_ref[...], NN)
            hi_s[pl.ds(t0, cr), :] = _dot(u, bi_ref[...], NN)
            return carry

        lax.fori_loop(0, t // cr, p1, 0)
        _s5_scan_rows(t, ar_ref[...], ai_ref[...], hr_s, hi_s, 0, False)

        def p3(c, carry):
            t0 = pl.multiple_of(c * cr, cr)
            y_ref[pl.ds(t0, cr), :] = (_dot(hr_s[pl.ds(t0, cr), :], cre_ref[...], NN)
                                       - _dot(hi_s[pl.ds(t0, cr), :], cim_ref[...], NN)
                                       + d_ref[...] * u_ref[pl.ds(t0, cr), :])
            return carry

        lax.fori_loop(0, t // cr, p3, 0)

    blk = lambda r, c: pl.BlockSpec((None, r, c), lambda b: (b, 0, 0))
    return _call(
        body, (2,),
        [pl.BlockSpec((t, LANES), lambda b: (0, Z_U // LANES + b)), blk(LANES, ns), blk(LANES, ns), blk(1, ns),
         blk(1, ns), blk(ns, LANES), blk(ns, LANES), pl.BlockSpec((1, LANES), lambda b: (0, b))],
        [pl.BlockSpec((t, LANES), lambda b: (0, b))], [jax.ShapeDtypeStruct((t, D_C), F32)],
        [pltpu.VMEM((t, ns), F32), pltpu.VMEM((t, ns), F32)], ("arbitrary",), "s5_fwd",
        [z, bd_re, bd_im, ab_re, ab_im, cd_re, cd_im, dvec], carry)


def _s5_bwd(z, dy, bd_re, bd_im, ab_re, ab_im, cd_re, cd_im, dvec):
    t = z.shape[0]
    cr = _tile(t, 256)
    ns = N_STATE // 2

    def body(u_ref, dy_ref, br_ref, bi_ref, ar_ref, ai_ref, cre_ref, cim_ref, d_ref,
             du_ref, dbr_ref, dbi_ref, dar_ref, dai_ref, dcre_ref, dcim_ref, dd_ref, hr_s, hi_s, gr_s, gi_s):
        zero8 = jnp.zeros((8, ns), F32)
        hr_s[pl.ds(0, 8), :] = zero8
        hi_s[pl.ds(0, 8), :] = zero8
        for ref in (dbr_ref, dbi_ref, dar_ref, dai_ref, dcre_ref, dcim_ref, dd_ref):
            ref[...] = jnp.zeros(ref.shape, F32)

        def p1(c, carry):
            t0 = pl.multiple_of(c * cr, cr)
            u = u_ref[pl.ds(t0, cr), :]
            hr_s[pl.ds(t0 + 8, cr), :] = _dot(u, br_ref[...], NN)
            hi_s[pl.ds(t0 + 8, cr), :] = _dot(u, bi_ref[...], NN)
            return carry

        lax.fori_loop(0, t // cr, p1, 0)
        _s5_scan_rows(t, ar_ref[...], ai_ref[...], hr_s, hi_s, 8, False)

        def p3(c, carry):
            t0 = pl.multiple_of(c * cr, cr)
            dyv = dy_ref[pl.ds(t0, cr), :]
            u = u_ref[pl.ds(t0, cr), :]
            gr_s[pl.ds(t0, cr), :] = _dot(dyv, cre_ref[...], NT)
            gi_s[pl.ds(t0, cr), :] = -_dot(dyv, cim_ref[...], NT)
            dcre_ref[...] += _dot(hr_s[pl.ds(t0 + 8, cr), :], dyv, TN)
            dcim_ref[...] -= _dot(hi_s[pl.ds(t0 + 8, cr), :], dyv, TN)
            dd_ref[...] += jnp.sum(dyv * u, axis=0, keepdims=True)
            du_ref[pl.ds(t0, cr), :] = dyv * d_ref[...]
            return carry

        lax.fori_loop(0, t // cr, p3, 0)
        _s5_scan_rows(t, ar_ref[...], ai_ref[...], gr_s, gi_s, 0, True)

        def p5(c, carry):
            t0 = pl.multiple_of(c * cr, cr)
            u = u_ref[pl.ds(t0, cr), :]
            gr = gr_s[pl.ds(t0, cr), :]
            gi = gi_s[pl.ds(t0, cr), :]
            dbr_ref[...] += _dot(u, gr, TN)
            dbi_ref[...] += _dot(u, gi, TN)
            du_ref[pl.ds(t0, cr), :] += _dot(gr, br_ref[...], NT) + _dot(gi, bi_ref[...], NT)
            hpr = pltpu.roll(hr_s[pl.ds(t0, cr + 8), :], 1, 0)[8:, :]
            hpi = pltpu.roll(hi_s[pl.ds(t0, cr + 8), :], 1, 0)[8:, :]
            dar_ref[...] += jnp.sum(gr * hpr + gi * hpi, axis=0, keepdims=True)
            dai_ref[...] += jnp.sum(gi * hpr - gr * hpi, axis=0, keepdims=True)
            return carry

        lax.fori_loop(0, t // cr, p5, 0)

    blk = lambda r, c: pl.BlockSpec((None, r, c), lambda b: (b, 0, 0))
    ucol = pl.BlockSpec((t, LANES), lambda b: (0, Z_U // LANES + b))
    ycol = pl.BlockSpec((t, LANES), lambda b: (0, b))
    dsp = pl.BlockSpec((1, LANES), lambda b: (0, b))
    sds = jax.ShapeDtypeStruct
    return pl.pallas_call(
        body, grid=(2,),
        in_specs=[ucol, ycol, blk(LANES, ns), blk(LANES, ns), blk(1, ns), blk(1, ns), blk(ns, LANES), blk(ns, LANES), dsp],
        out_specs=[ycol, blk(LANES, ns), blk(LANES, ns), blk(1, ns), blk(1, ns), blk(ns, LANES), blk(ns, LANES), dsp],
        out_shape=[sds((t, D_C), F32), sds((2, LANES, ns), F32), sds((2, LANES, ns), F32), sds((2, 1, ns), F32),
                   sds((2, 1, ns), F32), sds((2, ns, LANES), F32), sds((2, ns, LANES), F32), sds((1, D_C), F32)],
        scratch_shapes=[pltpu.VMEM((t + 8, ns), F32), pltpu.VMEM((t + 8, ns), F32), pltpu.VMEM((t, ns), F32),
                        pltpu.VMEM((t, ns), F32)],
        compiler_params=pltpu.CompilerParams(dimension_semantics=("arbitrary",)), name="s5_bwd")(
            z, dy, bd_re, bd_im, ab_re, ab_im, cd_re, cd_im, dvec)


def _mix_out(out_a, out_b, yc, x1, ga, gb, gc, wglu, wout, ln_g, ln_b):
    yg = _gelu(yc)
    out_c = yg * _sig(_bdot(yg, wglu))
    o = jnp.concatenate([_rms(out_a, ga), _rms(out_b, gb), _rms(out_c, gc)], axis=-1)
    return (_ln(ALPHA * x1 + _bdot(o, wout), ln_g, ln_b),)


def _ln_only(pre, g, b):
    return (_ln(pre, g, b),)


def _loss_head(y, target, tm):
    t, d = y.shape

    def body(y_ref, t_ref, dy_ref, l_ref):
        i = pl.program_id(0)
        e = y_ref[...] - t_ref[...]
        dy_ref[...] = e * (1.0 / d)
        part = 0.5 * jnp.sum(jnp.sum(e * e, axis=-1, keepdims=True) * (1.0 / d), axis=0, keepdims=True)
        row = jnp.where(lax.broadcasted_iota(jnp.int32, (1, LANES), 1) == 0, part, 0.0)

        @pl.when(i == 0)
        def _():
            l_ref[...] = row

        @pl.when(i > 0)
        def _():
            l_ref[...] += row

    spec = pl.BlockSpec((tm, d), lambda i: (i, 0))
    return pl.pallas_call(
        body, grid=(t // tm,), in_specs=[spec, spec], out_specs=[spec, pl.BlockSpec((1, LANES), lambda i: (0, 0))],
        out_shape=[jax.ShapeDtypeStruct((t, d), F32), jax.ShapeDtypeStruct((1, LANES), F32)],
        compiler_params=pltpu.CompilerParams(dimension_semantics=("arbitrary",)), name="loss_head")(y, target)


def _adamw(w, g, m, v, name):
    r, c = w.shape
    tr = r
    for cand in (512, 256, 352, 128):
        if r % cand == 0:
            tr = cand
            break

    def body(w_ref, g_ref, m_ref, v_ref, d_ref, nm_ref, nv_ref):
        gv = g_ref[...]
        mn = ADAM_B1 * m_ref[...] + (1.0 - ADAM_B1) * gv
        vn = ADAM_B2 * v_ref[...] + (1.0 - ADAM_B2) * (gv * gv)
        m_hat = mn / (1.0 - ADAM_B1 ** ADAM_STEP)
        v_hat = vn / (1.0 - ADAM_B2 ** ADAM_STEP)
        d_ref[...] = -ADAM_LR * (m_hat / (jnp.sqrt(v_hat) + ADAM_EPS) + ADAM_WD * w_ref[...])
        nm_ref[...] = mn
        nv_ref[...] = vn

    spec = pl.BlockSpec((tr, c), lambda i: (i, 0))
    return pl.pallas_call(
        body, grid=(r // tr,), in_specs=[spec] * 4, out_specs=[spec] * 3,
        out_shape=[jax.ShapeDtypeStruct((r, c), F32)] * 3,
        compiler_params=pltpu.CompilerParams(dimension_semantics=("parallel",)), name=name)(w, g, m, v)


def _row_tile(r):
    for cand in (512, 448, 352, 256, 128):
        if r % cand == 0:
            return cand
    return r


def _pair_add(a, b, idx, name, out_dtype):
    _, _, r, w = a.shape
    tr = _row_tile(r)

    def body(i_ref, a_ref, b_ref, o_ref):
        o_ref[...] = (a_ref[...].astype(F32) + b_ref[...].astype(F32)).astype(o_ref.dtype)

    grid_spec = pltpu.PrefetchScalarGridSpec(
        num_scalar_prefetch=1, grid=(4, r // tr),
        in_specs=[pl.BlockSpec((None, None, tr, w), lambda q, i, s: (q, s[0], i, 0)),
                  pl.BlockSpec((None, tr, w), lambda q, i, s: (q, i, 0))],
        out_specs=pl.BlockSpec((None, tr, w), lambda q, i, s: (q, i, 0)))
    return pl.pallas_call(body, grid_spec=grid_spec, out_shape=jax.ShapeDtypeStruct((4, r, w), out_dtype),
                          compiler_params=pltpu.CompilerParams(dimension_semantics=("parallel", "parallel")), name=name)(
                              idx, a, b)


def _quad_add(p, rb, idx, name):
    _, r, w = p.shape
    tr = _row_tile(r)

    def body(i_ref, p_ref, r0, r1, r2, o_ref):
        o_ref[...] = ((p_ref[...].astype(F32) + r0[...].astype(F32)) + r1[...].astype(F32)) + r2[...].astype(F32)

    grid_spec = pltpu.PrefetchScalarGridSpec(
        num_scalar_prefetch=1, grid=(r // tr,),
        in_specs=[pl.BlockSpec((None, tr, w), lambda i, s: (s[0], i, 0))]
        + [pl.BlockSpec((None, tr, w), functools.partial(lambda i, s, k: (k, i, 0), k=k)) for k in range(3)],
        out_specs=pl.BlockSpec((tr, w), lambda i, s: (i, 0)))
    return pl.pallas_call(body, grid_spec=grid_spec, out_shape=jax.ShapeDtypeStruct((r, w), F32),
                          compiler_params=pltpu.CompilerParams(dimension_semantics=("parallel",)), name=name)(
                              idx, p, rb, rb, rb)


def _all_reduce_small(buf):
    r, w = buf.shape

    def plan(x, y, c, ins, outs):
        dst = outs[0].at[4 * x + 2 * y + c]
        res = [(ins[0], dst, None)]
        for rel in range(1, 8):
            res.append((ins[0], dst, (x ^ (rel >> 2), y ^ ((rel >> 1) & 1), c ^ (rel & 1))))
        return res

    (allb,) = _run(_Carry([buf], [jax.ShapeDtypeStruct((8, r, w), buf.dtype)], {}, plan, 8), "ar_small")

    def body(a_ref, o_ref):
        s = a_ref[0]
        for k in range(1, 8):
            s = s + a_ref[k]
        o_ref[...] = s

    return pl.pallas_call(body, out_shape=jax.ShapeDtypeStruct((r, w), F32), name="ar_small_sum")(allb)


def _pad_rows(a, rows):
    return jnp.pad(a, ((0, rows - a.shape[0]), (0, 0)))


def _pack_small(arrs):
    total = sum(int(a.size) for a in arrs)
    fill = (-total) % (8 * LANES)
    return jnp.concatenate([a.reshape(-1) for a in arrs] + [jnp.zeros((fill,), F32)]).reshape(-1, LANES)


def _unpack_small(buf, shapes):
    out, off = [], 0
    flat = buf.reshape(-1)
    for s in shapes:
        n = int(np.prod(s))
        out.append(flat[off:off + n].reshape(s))
        off += n
    return out


def _block_diag(blocks, nb):
    m, r, c = blocks.shape
    n = m // nb
    eye = jnp.eye(n, dtype=blocks.dtype)
    return (blocks.reshape(nb, n, r, 1, c) * eye[None, :, None, :, None]).reshape(nb, n * r, n * c)


def _diag_blocks(dense, n):
    nb = dense.shape[0]
    r, c = dense.shape[1] // n, dense.shape[2] // n
    eye = jnp.eye(n, dtype=dense.dtype)
    return jnp.sum(dense.reshape(nb, n, r, n, c) * eye[None, :, None, :, None], axis=3).reshape(nb * n, r, c)


def kernel(x, ffn1_w_gate, ffn1_w_up, ffn1_w_down, ln1_g, ln1_b, w_in, conv_w, conv_b, rg_w_a, rg_b_a, rg_w_x, rg_b_x, rg_lambda, fox_b_f, s5_a_re, s5_a_im, s5_log_dt, s5_b_re, s5_b_im, s5_c_re, s5_c_im, s5_d, s5_w_glu, mix_norm_g, w_out, ln2_g, ln2_b, ffn2_w_gate, ffn2_w_up, ffn2_w_down, ln3_g, ln3_b, loss_target, m_ffn1_w_gate, m_ffn1_w_up, m_ffn1_w_down, m_ln1_g, m_ln1_b, m_w_in, m_conv_w, m_conv_b, m_rg_w_a, m_rg_b_a, m_rg_w_x, m_rg_b_x, m_rg_lambda, m_fox_b_f, m_s5_a_re, m_s5_a_im, m_s5_log_dt, m_s5_b_re, m_s5_b_im, m_s5_c_re, m_s5_c_im, m_s5_d, m_s5_w_glu, m_mix_norm_g, m_w_out, m_ln2_g, m_ln2_b, m_ffn2_w_gate, m_ffn2_w_up, m_ffn2_w_down, m_ln3_g, m_ln3_b, v_ffn1_w_gate, v_ffn1_w_up, v_ffn1_w_down, v_ln1_g, v_ln1_b, v_w_in, v_conv_w, v_conv_b, v_rg_w_a, v_rg_b_a, v_rg_w_x, v_rg_b_x, v_rg_lambda, v_fox_b_f, v_s5_a_re, v_s5_a_im, v_s5_log_dt, v_s5_b_re, v_s5_b_im, v_s5_c_re, v_s5_c_im, v_s5_d, v_s5_w_glu, v_mix_norm_g, v_w_out, v_ln2_g, v_ln2_b, v_ffn2_w_gate, v_ffn2_w_up, v_ffn2_w_down, v_ln3_g, v_ln3_b):
    a = dict(locals())
    w = {n: a[n] for n in WEIGHTS}
    t, d = x.shape[1], x.shape[2]
    f = ffn1_w_down.shape[1] * 8
    fs, ds = f // 8, d // 8
    mx, my, mc = lax.axis_index("x"), lax.axis_index("y"), lax.axis_index("c")
    me = 4 * mx + 2 * my + mc
    tm = _tile(t, 512)
    tf = f // 2
    win_rows = ds * Z_W // d

    FFN1, MIXW, FFN2 = ['g1', 'u1', 'd1'], ['win', 'wout', 'glu', 'conv'], ['g2', 'u2', 'd2']
    glu_rows = D_C * D_C // (8 * d)

    def shard_segs(l):
        wi = w['w_in'][l]
        win_p = jnp.concatenate([wi[:, :Z_F + N_HEADS], jnp.zeros((ds, Z_U - Z_F - N_HEADS), F32), wi[:, Z_F + N_HEADS:]], axis=1)
        conv_bits = lax.bitcast_convert_type(w['conv_w'][l], BF16).reshape(1, -1)
        segs = dict(g1=w['ffn1_w_gate'][l].T, u1=w['ffn1_w_up'][l].T, d1=w['ffn1_w_down'][l],
                    g2=w['ffn2_w_gate'][l].T, u2=w['ffn2_w_up'][l].T, d2=w['ffn2_w_down'][l],
                    win=win_p.reshape(win_rows, d), wout=w['w_out'][l], glu=_pad_rows(w['s5_w_glu'][l].reshape(-1, d), 16))
        segs = {k: v.astype(BF16) for k, v in segs.items()}
        segs['conv'] = _pad_rows(jnp.pad(conv_bits, ((0, 0), (0, d - conv_bits.shape[1]))), 16)
        return segs

    shards = [shard_segs(l) for l in range(DEPTH)]
    wts = {}

    def cat(keys):
        return jnp.concatenate([shards[l][k] for l, k in keys], axis=0)

    def split(g, keys):
        off = 0
        for l, k in keys:
            r = shards[l][k].shape[0]
            wts[(l, k)] = g[:, off:off + r]
            off += r

    grp_a = [(0, k) for k in FFN1]
    grp_b = [(0, k) for k in MIXW]
    grp_c = [(0, k) for k in FFN2] + [(1, k) for k in FFN1 + MIXW + FFN2]
    (g_a,) = _run(_ag_chips(cat(grp_a)), "ag_chips")
    (g_a,) = _run(_ag_sibling(g_a), "ag_sibling")
    split(g_a, grp_a)

    xs = x[0]
    saved = []
    cur = xs
    for l in range(DEPTH):
        row = lambda n: w[n][l].reshape(1, -1)
        ffn = lambda keys: tuple(wts[(l, k)].reshape(f, d) for k in keys)
        wa = _block_diag(w['rg_w_a'][l], 3)
        wx = _block_diag(w['rg_w_x'][l], 3)
        bf = jnp.pad(row('fox_b_f'), ((0, 0), (0, LANES - N_HEADS)))
        s5p = (w['s5_a_re'][l], w['s5_a_im'][l], w['s5_log_dt'][l].reshape(-1, 1),
               w['s5_b_re'][l].transpose(0, 2, 1), w['s5_b_im'][l].transpose(0, 2, 1))
        ab_re, ab_im, bb_re, bb_im = _s5_prep(*s5p)
        bd_re, bd_im = _block_diag(bb_re, 2), _block_diag(bb_im, 2)
        cd_re = _block_diag(w['s5_c_re'][l].transpose(0, 2, 1), 2)
        cd_im = _block_diag(w['s5_c_im'][l].transpose(0, 2, 1), 2)
        abr, abi = ab_re.reshape(2, 1, N_STATE // 2), ab_im.reshape(2, 1, N_STATE // 2)
        gm = row('mix_norm_g')
        ga, gb, gc = gm[:, :D_A], gm[:, D_A:D_A + D_B], gm[:, D_A + D_B:]

        x0 = cur
        ffn1 = ffn(FFN1)
        (x1, pre1, gs1, us1), cres = _ffn_fwd(x0, *ffn1, row('ln1_g'), row('ln1_b'), tm, tf,
                                              carry=_ag_chips(cat(grp_b)) if l == 0 else None)
        if l == 0:
            (g_b,) = _run(_ag_sibling(cres[0]), "ag_sibling")
            split(g_b, grp_b)
        win = wts[(l, 'win')].reshape(d, Z_W)
        wout = wts[(l, 'wout')].reshape(d, d).astype(F32)
        wglu = wts[(l, 'glu')][:, :glu_rows].reshape(D_C, D_C).astype(F32)
        conv_full = lax.bitcast_convert_type(
            wts[(l, 'conv')][:, 0, :2 * CONV_WIDTH * D_A // 8].reshape(8, CONV_WIDTH, D_A // 8, 2), F32)
        conv_full = conv_full.transpose(1, 0, 2).reshape(CONV_WIDTH, D_A)
        z = _mm(x1, win, 'nn', F32, tm, Z_W, d, "mix_in")
        out_a, h_a = _rg_fwd(z, conv_full, row('conv_b'), wa, row('rg_b_a'), wx, row('rg_b_x'), row('rg_lambda'))
        cs = _fgate_fwd(z, bf)
        crow = cs[:, :N_HEADS].T.reshape(N_HEADS, 1, t)
        (out_b, lse), cres = _attn_fwd(z, crow, carry=_ag_chips(cat(grp_c)) if l == 0 else None)
        (yc,), cres = _s5_fwd(z, bd_re, bd_im, abr, abi, cd_re, cd_im, row('s5_d'),
                              carry=_ag_sibling(cres[0]) if l == 0 else None)
        if l == 0:
            split(cres[0], grp_c)
        mix_params = [ga, gb, gc, wglu, wout, row('ln2_g'), row('ln2_b')]
        (x2,) = _rowwise(_mix_out, [out_a, out_b, yc, x1], mix_params, [(d, F32)], _tile(t, 256), "mix_out")
        ffn2 = ffn(FFN2)
        (x3, pre3, gs2, us2), _ = _ffn_fwd(x2, *ffn2, row('ln3_g'), row('ln3_b'), tm, tf)
        saved.append(dict(x0=x0, x1=x1, pre1=pre1, gs1=gs1, us1=us1, z=z, out_a=out_a, h_a=h_a, crow=crow,
                          out_b=out_b, lse=lse, yc=yc, x2=x2, pre3=pre3, gs2=gs2, us2=us2, mix_params=mix_params,
                          ffn1=ffn1, ffn2=ffn2, win=win, conv_full=conv_full, wa=wa, wx=wx, bf=bf, s5p=s5p,
                          s5m=(bd_re, bd_im, abr, abi, cd_re, cd_im)))
        cur = x3

    dy, loss_row = _loss_head(cur, loss_target[0], tm)
    loss = lax.psum(loss_row[0, 0], ("x", "y", "c"))

    assert DEPTH == 2
    small_grads = {}
    c_idx = jnp.reshape(mc, (1,)).astype(jnp.int32)
    chip_idx = jnp.reshape(2 * mx + my, (1,)).astype(jnp.int32)

    def blocks(arrs):
        return jnp.concatenate([v.astype(BF16).reshape(8, -1, d) for v in arrs], axis=1)

    def mixer_blocks(dwin, dwout, dwglu):
        glu = jnp.pad(dwglu.astype(BF16).reshape(8, glu_rows, d), ((0, 0), (0, 32 - glu_rows), (0, 0)))
        return jnp.concatenate([dwin.reshape(8, win_rows, d), dwout.astype(BF16).reshape(8, ds, d), glu], axis=1)

    def pair(full, ra):
        return _pair_add(full.reshape(4, 2, full.shape[1], d), ra, c_idx, "rs_add_sibling", BF16)

    for l in reversed(range(DEPTH)):
        s = saved[l]
        row = lambda n: w[n][l].reshape(1, -1)
        wg_tiles = dict(tm=tf, tn=d, tk=tm)
        first = l == 0

        def ffn_back(dyv, pre, gs, us, wts3, xin, ln_g, ln_b, carry):
            (dpre,), (dlg, dlb), _ = _rowwise_vjp(_ln_only, [pre], [ln_g, ln_b], [dyv], _tile(t, 256), "ln_bwd")
            (dx, dg, du, hh), cres = _ffn_bwd(dpre, gs, us, *wts3, tm, tf, carry=carry)
            dwg = _mm(dg, xin, 'tn', BF16, name="ffn_dw_gate", **wg_tiles)
            dwu = _mm(du, xin, 'tn', BF16, name="ffn_dw_up", **wg_tiles)
            dwd = _mm(hh, dpre, 'tn', BF16, name="ffn_dw_down", **wg_tiles)
            return dx, dwg, dwu, dwd, dlg, dlb, cres

        dx2, dwg2, dwu2, dwd2, dl3g, dl3b, cres = ffn_back(dy, s['pre3'], s['gs2'], s['us2'], s['ffn2'], s['x2'], row('ln3_g'),
                                                           row('ln3_b'), _rs_sibling(full_l1) if first else None)
        if first:
            part_l1 = pair(full_l1, cres[0])
            full_c2 = blocks([dwg2, dwu2, dwd2])
        (d_oa, d_ob, d_yc, d_x1), (dga, dgb, dgc, dwglu, dwout, dl2g, dl2b), cres = _rowwise_vjp(
            _mix_out, [s['out_a'], s['out_b'], s['yc'], s['x1']], s['mix_params'], [dx2], _tile(t, 256), "mix_out_bwd",
            carry=_rs_sibling(full_c2) if first else None)
        if first:
            part_c2 = pair(full_c2, cres[0])
        (d_ax, d_ag, dcw, dcb, dwa, dba, dwx, dbx, dlam) = _rg_bwd(
            s['z'], s['h_a'], d_oa, s['conv_full'], row('conv_b'), s['wa'], row('rg_b_a'), s['wx'], row('rg_b_x'), row('rg_lambda'))
        (dq, dk, dv, dcrow), cres = _attn_bwd(s['z'], s['crow'], s['lse'], d_ob,
                                              carry=_join(_rs_chips(part_l1), _rs_chips(part_c2)) if first else None)
        if first:
            rb_l1, rb_c2 = cres
        dc_pad = jnp.pad(dcrow.reshape(N_HEADS, t).T, ((0, 0), (0, LANES - N_HEADS)))
        dzf, dbf = _fgate_bwd(s['z'], s['bf'], dc_pad)
        du_c, dbd_re, dbd_im, dabr, dabi, dcd_re, dcd_im, dd = _s5_bwd(s['z'], d_yc, *s['s5m'], row('s5_d'))
        dz = jnp.concatenate([d_ax, d_ag, dq, dk, dv, dzf, du_c], axis=1)
        dx1 = _mm(dz, s['win'], 'nt', F32, tm, d, Z_W, "mix_in_dx", add=d_x1)
        dwin = _mm(s['x1'], dz, 'tn', BF16, d, Z_W, tm, "mix_in_dw")
        if first:
            full_m = mixer_blocks(dwin, dwout, dwglu)
            (ra_m,) = _run(_rs_sibling(full_m), "rs_sibling")
            part_m = pair(full_m, ra_m)
        dx0, dwg1, dwu1, dwd1, dl1g, dl1b, cres = ffn_back(dx1, s['pre1'], s['gs1'], s['us1'], s['ffn1'], s['x0'], row('ln1_g'),
                                                           row('ln1_b'), _rs_chips(part_m) if first else None)
        if first:
            rb_m = cres[0]
            full_c1 = blocks([dwg1, dwu1, dwd1])
            (ra_c1,) = _run(_rs_sibling(full_c1), "rs_sibling")
            part_c1 = pair(full_c1, ra_c1)
            (rb_c1,) = _run(_rs_chips(part_c1), "rs_chips")
        else:
            full_l1 = jnp.concatenate([blocks([dwg1, dwu1, dwd1, dwg2, dwu2, dwd2]), mixer_blocks(dwin, dwout, dwglu)], axis=1)
        dy = dx0

        dbb_re, dbb_im = _diag_blocks(dbd_re, N_GROUPS // 2), _diag_blocks(dbd_im, N_GROUPS // 2)
        dcm_re, dcm_im = _diag_blocks(dcd_re, N_GROUPS // 2), _diag_blocks(dcd_im, N_GROUPS // 2)
        da_re, da_im, dlog_dt, db_re, db_im = _s5_prep_bwd(*s['s5p'], dabr.reshape(N_GROUPS, C_STATE), dabi.reshape(N_GROUPS, C_STATE), dbb_re, dbb_im)

        sg = dict(ln1_g=dl1g, ln1_b=dl1b, conv_w=dcw, conv_b=dcb, rg_w_a=_diag_blocks(dwa, 2), rg_b_a=dba,
                  rg_w_x=_diag_blocks(dwx, 2), rg_b_x=dbx, rg_lambda=dlam, fox_b_f=dbf[:, :N_HEADS],
                  s5_a_re=da_re, s5_a_im=da_im, s5_log_dt=dlog_dt, s5_b_re=db_re.transpose(0, 2, 1), s5_b_im=db_im.transpose(0, 2, 1),
                  s5_c_re=dcm_re.transpose(0, 2, 1), s5_c_im=dcm_im.transpose(0, 2, 1), s5_d=dd,
                  mix_norm_g=jnp.concatenate([dga, dgb, dgc], axis=1), ln2_g=dl2g, ln2_b=dl2b, ln3_g=dl3g, ln3_b=dl3b)
        small_grads[l] = sg

    grad_x = dy.reshape(x.shape)
    quad = lambda part, rb: _quad_add(part, rb, chip_idx, "rs_add_chips")
    own = {}

    def take(rows_f32, keys, l):
        off = 0
        for k, r in keys:
            own[(l, k)] = rows_f32[off:off + r]
            off += r

    ffn_keys = lambda names: [(k, fs) for k in names]
    mix_keys = [('win', win_rows), ('wout', ds), ('glu', glu_rows)]
    take(quad(part_l1, rb_l1), ffn_keys(FFN1 + FFN2) + mix_keys, 1)
    take(quad(part_c2, rb_c2), ffn_keys(FFN2), 0)
    take(quad(part_m, rb_m), mix_keys, 0)
    take(quad(part_c1, rb_c1), ffn_keys(FFN1), 0)

    grads = {}
    for k, n in zip(FFN1 + FFN2, ['ffn1_w_gate', 'ffn1_w_up', 'ffn1_w_down', 'ffn2_w_gate', 'ffn2_w_up', 'ffn2_w_down']):
        grads[n] = jnp.stack([own[(l, k)].T if 'down' not in n else own[(l, k)] for l in range(DEPTH)])
    gwin = jnp.stack([own[(l, 'win')].reshape(ds, Z_W) for l in range(DEPTH)])
    grads['w_in'] = jnp.concatenate([gwin[:, :, :Z_F + N_HEADS], gwin[:, :, Z_U:]], axis=2)
    grads['w_out'] = jnp.stack([own[(l, 'wout')] for l in range(DEPTH)])
    grads['s5_w_glu'] = jnp.stack([own[(l, 'glu')].reshape(D_C // 8, D_C) for l in range(DEPTH)])

    small_names = SMALL + ['conv_w']
    small_shapes = [((DEPTH, CONV_WIDTH, D_A) if n == 'conv_w' else w[n].shape) for n in small_names]
    summed = _all_reduce_small(_pack_small([small_grads[l][n] for n in small_names for l in range(DEPTH)]))
    for n, g in zip(small_names, _unpack_small(summed, small_shapes)):
        grads[n] = g
    grads['conv_w'] = lax.dynamic_slice_in_dim(grads['conv_w'], me * (D_A // 8), D_A // 8, axis=2)

    delta, new_m, new_v = {}, {}, {}
    for n in BIG + ['conv_w']:
        sh = w[n].shape
        two = lambda v: v.reshape(-1, sh[-1])
        dl, nm, nv = _adamw(two(w[n]), two(grads[n]), two(a['m_' + n]), two(a['v_' + n]), "adamw_" + n)
        delta[n], new_m[n], new_v[n] = dl.reshape(sh), nm.reshape(sh), nv.reshape(sh)
    sshapes = [w[n].shape for n in SMALL]
    w_small = _pack_small([w[n] for n in SMALL])
    dl, nm, nv = _adamw(w_small, summed[:w_small.shape[0]], _pack_small([a['m_' + n] for n in SMALL]),
                        _pack_small([a['v_' + n] for n in SMALL]), "adamw_small")
    for n, v1, v2, v3 in zip(SMALL, _unpack_small(dl, sshapes), _unpack_small(nm, sshapes), _unpack_small(nv, sshapes)):
        delta[n], new_m[n], new_v[n] = v1, v2, v3

    return (loss, grad_x, *[grads[n] for n in WEIGHTS], *[delta[n] for n in WEIGHTS], *[new_m[n] for n in WEIGHTS],
            *[new_v[n] for n in WEIGHTS])
```

```python
import functools
import math

import jax
import jax.numpy as jnp
import numpy as np
from jax import lax
from jax.experimental import pallas as pl
from jax.experimental.pallas import tpu as pltpu

F32 = jnp.float32
BF16 = jnp.bfloat16
MESH = pl.DeviceIdType.MESH

DEPTH = 2
ALPHA = (2 * DEPTH) ** 0.25
LN_EPS = 1e-5
RMS_EPS = 1e-6
RG_C = 8.0
CONV_WIDTH = 4
HEAD_DIM = 64
C_GROUP = 16
C_STATE = 64
D_A = 384
D_B = 384
D_C = 256
N_HEADS = D_B // HEAD_DIM
N_GROUPS = D_C // C_GROUP
N_STATE = N_GROUPS * C_STATE
Z_F = 2 * D_A + 3 * D_B
Z_U = Z_F + 128
Z_W = Z_U + D_C
N_IN = Z_F + N_HEADS + D_C
ADAM_LR, ADAM_B1, ADAM_B2, ADAM_EPS, ADAM_WD, ADAM_STEP = 0.001, 0.9, 0.999, 1e-08, 0.01, 10
LANES = 128
NEG = -1e30

WEIGHTS = ['ffn1_w_gate', 'ffn1_w_up', 'ffn1_w_down', 'ln1_g', 'ln1_b', 'w_in', 'conv_w', 'conv_b', 'rg_w_a', 'rg_b_a',
           'rg_w_x', 'rg_b_x', 'rg_lambda', 'fox_b_f', 's5_a_re', 's5_a_im', 's5_log_dt', 's5_b_re', 's5_b_im', 's5_c_re',
           's5_c_im', 's5_d', 's5_w_glu', 'mix_norm_g', 'w_out', 'ln2_g', 'ln2_b', 'ffn2_w_gate', 'ffn2_w_up', 'ffn2_w_down',
           'ln3_g', 'ln3_b']
BIG = ['ffn1_w_gate', 'ffn1_w_up', 'ffn1_w_down', 'w_in', 's5_w_glu', 'w_out', 'ffn2_w_gate', 'ffn2_w_up', 'ffn2_w_down']
SMALL_TAIL = ['fox_b_f', 's5_log_dt']
SMALL = [n for n in WEIGHTS if n not in BIG and n != 'conv_w' and n not in SMALL_TAIL] + SMALL_TAIL


def _sig(x):
    return 1.0 / (1.0 + jnp.exp(-x))


def _gelu(x):
    return 0.5 * x * (1.0 + jnp.tanh(math.sqrt(2.0 / math.pi) * (x + 0.044715 * (x * x * x))))


def _softplus(x):
    return jnp.maximum(x, 0.0) + jnp.log(1.0 + jnp.exp(jnp.minimum(x, -x)))


def _dot(a, b, dims):
    return lax.dot_general(a.astype(BF16), b.astype(BF16), (dims, ((), ())), preferred_element_type=F32)


NN = ((1,), (0,))
NT = ((1,), (1,))
TN = ((0,), (0,))


@jax.custom_vjp
def _bdot(a, w):
    return _dot(a, w, NN)


def _bdot_fwd(a, w):
    return _dot(a, w, NN), (a, w)


def _bdot_bwd(res, ct):
    a, w = res
    return _dot(ct, w, NT), _dot(a, ct, TN)


_bdot.defvjp(_bdot_fwd, _bdot_bwd)


def _ln(pre, g, b):
    mu = jnp.mean(pre, axis=-1, keepdims=True)
    xc = pre - mu
    var = jnp.mean(xc * xc, axis=-1, keepdims=True)
    return xc * lax.rsqrt(var + LN_EPS) * g + b


def _rms(x, g):
    return x * lax.rsqrt(jnp.mean(x * x, axis=-1, keepdims=True) + RMS_EPS) * g


def _tile(n, want):
    return want if n % want == 0 else n


class _Carry:
    def __init__(self, ins, outs, aliases, plan, n):
        self.ins, self.outs, self.aliases, self.plan, self.n = list(ins), list(outs), dict(aliases), plan, n


def _join(a, b):
    na, ma = len(a.ins), len(a.outs)

    def plan(x, y, c, ins, outs):
        return a.plan(x, y, c, ins[:na], outs[:ma]) + b.plan(x, y, c, ins[na:], outs[ma:])

    aliases = dict(a.aliases)
    aliases.update({na + i: ma + j for i, j in b.aliases.items()})
    return _Carry(a.ins + b.ins, a.outs + b.outs, aliases, plan, a.n + b.n)


def _copies(carry, cins, couts, send, recv):
    x, y, c = lax.axis_index("x"), lax.axis_index("y"), lax.axis_index("c")
    res = []
    for k, (s, d, peer) in enumerate(carry.plan(x, y, c, cins, couts)):
        if peer is None:
            res.append(pltpu.make_async_copy(s, d, send.at[k]))
        else:
            res.append(pltpu.make_async_remote_copy(src_ref=s, dst_ref=d, send_sem=send.at[k], recv_sem=recv.at[k],
                                                    device_id=peer, device_id_type=MESH))
    return res


def _call(body, grid, in_specs, out_specs, out_shape, scratch, semantics, name, args, carry=None):
    n_in, n_out, n_scr = len(in_specs), len(out_specs), len(scratch)
    if carry is None:
        res = pl.pallas_call(body, grid=grid, in_specs=in_specs, out_specs=out_specs, out_shape=out_shape,
                             scratch_shapes=scratch, compiler_params=pltpu.CompilerParams(dimension_semantics=semantics),
                             name=name)(*args)
        return list(res), []
    nci, nco = len(carry.ins), len(carry.outs)

    def wrapped(*refs):
        o0 = n_in + nci
        s0 = o0 + n_out + nco
        cins, couts = refs[n_in:o0], refs[o0 + n_out:s0]
        send, recv = refs[s0 + n_scr:]
        first = functools.reduce(jnp.logical_and, [pl.program_id(k) == 0 for k in range(len(grid))])
        last = functools.reduce(jnp.logical_and, [pl.program_id(k) == grid[k] - 1 for k in range(len(grid))])

        @pl.when(first)
        def _():
            for cp in _copies(carry, cins, couts, send, recv):
                cp.start()

        body(*refs[:n_in], *refs[o0:o0 + n_out], *refs[s0:s0 + n_scr])

        @pl.when(last)
        def _():
            for cp in _copies(carry, cins, couts, send, recv):
                cp.wait()

    hbm = pl.BlockSpec(memory_space=pl.ANY)
    res = pl.pallas_call(
        wrapped, grid=grid, in_specs=list(in_specs) + [hbm] * nci, out_specs=list(out_specs) + [hbm] * nco,
        out_shape=list(out_shape) + carry.outs, scratch_shapes=list(scratch) + [pltpu.SemaphoreType.DMA((carry.n,))] * 2,
        input_output_aliases={n_in + i: n_out + j for i, j in carry.aliases.items()},
        compiler_params=pltpu.CompilerParams(dimension_semantics=("arbitrary",) * len(grid), has_side_effects=True),
        name=name)(*args, *carry.ins)
    return list(res[:n_out]), list(res[n_out:])


def _run(carry, name):
    nci, nco = len(carry.ins), len(carry.outs)

    def body(*refs):
        cps = _copies(carry, refs[:nci], refs[nci:nci + nco], refs[-2], refs[-1])
        for cp in cps:
            cp.start()
        for cp in cps:
            cp.wait()

    hbm = pl.BlockSpec(memory_space=pl.ANY)
    return pl.pallas_call(
        body, in_specs=[hbm] * nci, out_specs=[hbm] * nco, out_shape=carry.outs, input_output_aliases=carry.aliases,
        scratch_shapes=[pltpu.SemaphoreType.DMA((carry.n,))] * 2, compiler_params=pltpu.CompilerParams(has_side_effects=True),
        name=name)(*carry.ins)


def _ag_chips(shard):
    def plan(x, y, c, ins, outs):
        dst = outs[0].at[4 * x + 2 * y + c]
        return [(ins[0], dst, None)] + [(ins[0], dst, (px, py, c)) for px, py in ((1 - x, y), (x, 1 - y), (1 - x, 1 - y))]

    return _Carry([shard], [jax.ShapeDtypeStruct((8,) + shard.shape, shard.dtype)], {}, plan, 4)


def _ag_sibling(g):
    def plan(x, y, c, ins, outs):
        return [(outs[0].at[2 * q + c], outs[0].at[2 * q + c], (x, y, 1 - c)) for q in range(4)]

    return _Carry([g], [jax.ShapeDtypeStruct(g.shape, g.dtype)], {0: 0}, plan, 4)


def _rs_sibling(full):
    def plan(x, y, c, ins, outs):
        return [(ins[0].at[2 * q + (1 - c)], outs[0].at[q], (x, y, 1 - c)) for q in range(4)]

    return _Carry([full], [jax.ShapeDtypeStruct((4,) + full.shape[1:], full.dtype)], {}, plan, 4)


def _rs_chips(part):
    def plan(x, y, c, ins, outs):
        res = []
        for k, (dx, dy) in enumerate(((1, 0), (0, 1), (1, 1))):
            tx, ty = x ^ dx, y ^ dy
            res.append((ins[0].at[2 * tx + ty], outs[0].at[k], (tx, ty, c)))
        return res

    return _Carry([part], [jax.ShapeDtypeStruct((3,) + part.shape[1:], part.dtype)], {}, plan, 3)


def _mm(a, b, dims, out_dtype, tm, tn, tk, name, add=None):
    if dims == 'nn':
        (m, k), n = a.shape, b.shape[1]
        a_spec = pl.BlockSpec((tm, tk), lambda i, j, q: (i, q))
        b_spec = pl.BlockSpec((tk, tn), lambda i, j, q: (q, j))
        dn = NN
    elif dims == 'nt':
        (m, k), n = a.shape, b.shape[0]
        a_spec = pl.BlockSpec((tm, tk), lambda i, j, q: (i, q))
        b_spec = pl.BlockSpec((tn, tk), lambda i, j, q: (j, q))
        dn = NT
    else:
        (k, m), n = a.shape, b.shape[1]
        a_spec = pl.BlockSpec((tk, tm), lambda i, j, q: (q, i))
        b_spec = pl.BlockSpec((tk, tn), lambda i, j, q: (q, j))
        dn = TN
    nk = k // tk
    o_spec = pl.BlockSpec((tm, tn), lambda i, j, q: (i, j))

    def body(*refs):
        if add is None:
            a_ref, b_ref, o_ref, acc_ref = refs
        else:
            a_ref, b_ref, add_ref, o_ref, acc_ref = refs
        q = pl.program_id(2)
        part = _dot(a_ref[...], b_ref[...], dn)

        @pl.when(q == 0)
        def _():
            acc_ref[...] = part

        @pl.when(q > 0)
        def _():
            acc_ref[...] += part

        @pl.when(q == nk - 1)
        def _():
            r = acc_ref[...]
            if add is not None:
                r = r + add_ref[...]
            o_ref[...] = r.astype(o_ref.dtype)

    ins = [a, b] + ([] if add is None else [add])
    specs = [a_spec, b_spec] + ([] if add is None else [o_spec])
    return pl.pallas_call(
        body, grid=(m // tm, n // tn, nk), in_specs=specs, out_specs=o_spec,
        out_shape=jax.ShapeDtypeStruct((m, n), out_dtype), scratch_shapes=[pltpu.VMEM((tm, tn), F32)],
        compiler_params=pltpu.CompilerParams(dimension_semantics=("parallel", "parallel", "arbitrary")), name=name)(*ins)


def _rowwise(fn, rows, params, outs, tm, name):
    t = rows[0].shape[0]
    nr, npar = len(rows), len(params)

    def body(*refs):
        r = [x[...] for x in refs[:nr]]
        p = [x[...] for x in refs[nr:nr + npar]]
        res = fn(*r, *p)
        for o_ref, o in zip(refs[nr + npar:], res):
            o_ref[...] = o.astype(o_ref.dtype)

    in_specs = ([pl.BlockSpec((tm, a.shape[1]), lambda i: (i, 0)) for a in rows]
                + [pl.BlockSpec(p.shape, lambda i: (0, 0)) for p in params])
    return pl.pallas_call(
        body, grid=(t // tm,), in_specs=in_specs,
        out_specs=[pl.BlockSpec((tm, c), lambda i: (i, 0)) for c, _ in outs],
        out_shape=[jax.ShapeDtypeStruct((t, c), d) for c, d in outs],
        compiler_params=pltpu.CompilerParams(dimension_semantics=("parallel",)), name=name)(*rows, *params)


def _rowwise_vjp(fn, rows, params, cots, tm, name, carry=None):
    t = rows[0].shape[0]
    nr, npar, nc = len(rows), len(params), len(cots)

    def body(*refs):
        r = [x[...] for x in refs[:nr]]
        p = [x[...] for x in refs[nr:nr + npar]]
        c = [x[...] for x in refs[nr + npar:nr + npar + nc]]
        o_refs = refs[nr + npar + nc:]
        _, pull = jax.vjp(fn, *r, *p)
        grads = pull(tuple(c))
        for o_ref, g in zip(o_refs[:nr], grads[:nr]):
            o_ref[...] = g
        i = pl.program_id(0)

        @pl.when(i == 0)
        def _():
            for o_ref, g in zip(o_refs[nr:], grads[nr:]):
                o_ref[...] = g

        @pl.when(i > 0)
        def _():
            for o_ref, g in zip(o_refs[nr:], grads[nr:]):
                o_ref[...] += g

    row_spec = lambda a: pl.BlockSpec((tm, a.shape[1]), lambda i: (i, 0))
    par_spec = lambda p: pl.BlockSpec(p.shape, lambda i: (0, 0))
    res, cres = _call(
        body, (t // tm,),
        [row_spec(a) for a in rows] + [par_spec(p) for p in params] + [row_spec(a) for a in cots],
        [row_spec(a) for a in rows] + [par_spec(p) for p in params],
        [jax.ShapeDtypeStruct(a.shape, F32) for a in rows] + [jax.ShapeDtypeStruct(p.shape, F32) for p in params],
        [], ("arbitrary",), name, [*rows, *params, *cots], carry)
    return res[:nr], res[nr:], cres


def _ffn_fwd(x, wgt, wut, wd, ln_g, ln_b, tm, tf, carry=None):
    t, d = x.shape
    f = wgt.shape[0]
    nj = f // tf

    def body(x_ref, wg_ref, wu_ref, wd_ref, g_ref, b_ref, y_ref, pre_ref, gs_ref, us_ref, acc_ref):
        j = pl.program_id(1)
        xv = x_ref[...]
        xb = xv.astype(BF16)
        g = _dot(xb, wg_ref[...], NT)
        u = _dot(xb, wu_ref[...], NT)
        gs_ref[...] = g.astype(BF16)
        us_ref[...] = u.astype(BF16)
        part = _dot(g * _sig(g) * u, wd_ref[...], NN)

        @pl.when(j == 0)
        def _():
            acc_ref[...] = part

        @pl.when(j > 0)
        def _():
            acc_ref[...] += part

        @pl.when(j == nj - 1)
        def _():
            pre = ALPHA * xv + 0.5 * acc_ref[...]
            pre_ref[...] = pre
            y_ref[...] = _ln(pre, g_ref[...], b_ref[...])

    w_spec = pl.BlockSpec((tf, d), lambda i, j: (j, 0))
    x_spec = pl.BlockSpec((tm, d), lambda i, j: (i, 0))
    v_spec = pl.BlockSpec((1, d), lambda i, j: (0, 0))
    h_spec = pl.BlockSpec((tm, tf), lambda i, j: (i, j))
    return _call(
        body, (t // tm, nj), [x_spec, w_spec, w_spec, w_spec, v_spec, v_spec], [x_spec, x_spec, h_spec, h_spec],
        [jax.ShapeDtypeStruct((t, d), F32), jax.ShapeDtypeStruct((t, d), F32),
         jax.ShapeDtypeStruct((t, f), BF16), jax.ShapeDtypeStruct((t, f), BF16)],
        [pltpu.VMEM((tm, d), F32)], ("parallel", "arbitrary"), "ffn_fwd", [x, wgt, wut, wd, ln_g, ln_b], carry)


def _ffn_bwd(dpre, gs, us, wgt, wut, wd, tm, tf, carry=None):
    t, d = dpre.shape
    f = wgt.shape[0]
    nj = f // tf

    def body(dp_ref, gs_ref, us_ref, wg_ref, wu_ref, wd_ref, dx_ref, dg_ref, du_ref, hh_ref, acc_ref):
        j = pl.program_id(1)
        dp = dp_ref[...]
        dh = _dot(0.5 * dp, wd_ref[...], NT)
        g = gs_ref[...].astype(F32)
        u = us_ref[...].astype(F32)
        s = _sig(g)
        sl = g * s
        dg = (dh * u * (s * (1.0 + g * (1.0 - s)))).astype(BF16)
        du = (dh * sl).astype(BF16)
        dg_ref[...] = dg
        du_ref[...] = du
        hh_ref[...] = (0.5 * sl * u).astype(BF16)
        part = _dot(dg, wg_ref[...], NN) + _dot(du, wu_ref[...], NN)

        @pl.when(j == 0)
        def _():
            acc_ref[...] = part

        @pl.when(j > 0)
        def _():
            acc_ref[...] += part

        @pl.when(j == nj - 1)
        def _():
            dx_ref[...] = ALPHA * dp + acc_ref[...]

    w_spec = pl.BlockSpec((tf, d), lambda i, j: (j, 0))
    x_spec = pl.BlockSpec((tm, d), lambda i, j: (i, 0))
    h_spec = pl.BlockSpec((tm, tf), lambda i, j: (i, j))
    return _call(
        body, (t // tm, nj), [x_spec, h_spec, h_spec, w_spec, w_spec, w_spec], [x_spec, h_spec, h_spec, h_spec],
        [jax.ShapeDtypeStruct((t, d), F32)] + [jax.ShapeDtypeStruct((t, f), BF16)] * 3,
        [pltpu.VMEM((tm, d), F32)], ("parallel", "arbitrary"), "ffn_bwd", [dpre, gs, us, wgt, wut, wd], carry)


def _scan8(a_ref, b_ref, out_ref, t, reverse=False):
    w = out_ref.shape[-1]
    sub = lax.broadcasted_iota(jnp.int32, (8, w), 0)

    def step(g, carry):
        r0 = pl.multiple_of((t // 8 - 1 - g if reverse else g) * 8, 8)
        bv = b_ref[pl.ds(r0, 8), :]
        av = None if a_ref is None else a_ref[pl.ds(r0, 8), :]
        for s in (1, 2, 4):
            ok = (sub < 8 - s) if reverse else (sub >= s)
            shift = 8 - s if reverse else s
            b_sh = jnp.where(ok, pltpu.roll(bv, shift, 0), 0.0)
            if av is None:
                bv = bv + b_sh
            else:
                bv = av * b_sh + bv
                av = av * jnp.where(ok, pltpu.roll(av, shift, 0), 1.0)
        h = bv + carry if av is None else bv + av * carry
        out_ref[pl.ds(r0, 8), :] = h
        return jnp.sum(jnp.where(sub == (0 if reverse else 7), h, 0.0), axis=0, keepdims=True)

    lax.fori_loop(0, t // 8, step, jnp.zeros((1, w), F32))


def _rg_local(xa, wa, ba, wx, bx, lam):
    r = _sig(_bdot(xa, wa) + ba)
    i = _sig(_bdot(xa, wx) + bx)
    log_a = -RG_C * r * _softplus(-lam)
    a = jnp.exp(log_a)
    mult = jnp.sqrt(-jnp.tanh(log_a) * (a * a + 1.0))
    return a, mult * (i * xa)


def _conv_taps(ext, n):
    return [ext[8:, :]] + [pltpu.roll(ext, s, 0)[8:, :] for s in (1, 2, 3)]


def _rg_fwd(z, cw, cb, wa, ba, wx, bx, lam):
    t = z.shape[0]
    cr = _tile(t, 256)
    nb = D_A // LANES

    def body(ax_ref, ag_ref, cw_ref, cb_ref, wa_ref, ba_ref, wx_ref, bx_ref, lam_ref, out_ref, h_ref, axp, a_s, b_s):
        axp[pl.ds(0, 8), :] = jnp.zeros((8, LANES), F32)
        pltpu.sync_copy(ax_ref, axp.at[pl.ds(8, t)])
        w = [cw_ref[pl.ds(k, 1), :] for k in range(CONV_WIDTH)]

        def chunk(c, carry):
            t0 = pl.multiple_of(c * cr, cr)
            taps = _conv_taps(axp[pl.ds(t0, cr + 8), :], cr)
            xa = cb_ref[...] + w[3] * taps[0] + w[2] * taps[1] + w[1] * taps[2] + w[0] * taps[3]
            a, gated = _rg_local(xa, wa_ref[...], ba_ref[...], wx_ref[...], bx_ref[...], lam_ref[...])
            a_s[pl.ds(t0, cr), :] = a
            b_s[pl.ds(t0, cr), :] = gated
            return carry

        lax.fori_loop(0, t // cr, chunk, 0)

        _scan8(a_s, b_s, h_ref, t)

        def fin(c, carry):
            t0 = pl.multiple_of(c * cr, cr)
            out_ref[pl.ds(t0, cr), :] = _gelu(ag_ref[pl.ds(t0, cr), :]) * h_ref[pl.ds(t0, cr), :]
            return carry

        lax.fori_loop(0, t // cr, fin, 0)

    col = lambda off: pl.BlockSpec((t, LANES), lambda b: (0, off + b))
    vec = pl.BlockSpec((1, LANES), lambda b: (0, b))
    mat = pl.BlockSpec((None, LANES, LANES), lambda b: (b, 0, 0))
    return pl.pallas_call(
        body, grid=(nb,),
        in_specs=[col(0), col(nb), pl.BlockSpec((CONV_WIDTH, LANES), lambda b: (0, b)), vec, mat, vec, mat, vec, vec],
        out_specs=[col(0), col(0)],
        out_shape=[jax.ShapeDtypeStruct((t, D_A), F32), jax.ShapeDtypeStruct((t, D_A), F32)],
        scratch_shapes=[pltpu.VMEM((t + 8, LANES), F32), pltpu.VMEM((t, LANES), F32), pltpu.VMEM((t, LANES), F32)],
        compiler_params=pltpu.CompilerParams(dimension_semantics=("arbitrary",)), name="rglru_fwd")(
            z, z, cw, cb, wa, ba, wx, bx, lam)


def _rg_bwd(z, h, dout, cw, cb, wa, ba, wx, bx, lam):
    t = z.shape[0]
    cr = _tile(t, 256)
    nb = D_A // LANES

    def body(ax_ref, ag_ref, h_ref, do_ref, cw_ref, cb_ref, wa_ref, ba_ref, wx_ref, bx_ref, lam_ref,
             dax_ref, dag_ref, dcw_ref, dcb_ref, dwa_ref, dba_ref, dwx_ref, dbx_ref, dlam_ref,
             axp, hp, xa_s, a_s, g_s, dxa_s, u_s):
        zero8 = jnp.zeros((8, LANES), F32)
        axp[pl.ds(0, 8), :] = zero8
        hp[pl.ds(0, 8), :] = zero8
        dxa_s[pl.ds(t, 8), :] = zero8
        u_s[pl.ds(t, 8), :] = zero8
        pltpu.sync_copy(ax_ref, axp.at[pl.ds(8, t)])
        pltpu.sync_copy(h_ref, hp.at[pl.ds(8, t)])
        w = [cw_ref[pl.ds(k, 1), :] for k in range(CONV_WIDTH)]
        for ref in (dcw_ref, dcb_ref, dwa_ref, dba_ref, dwx_ref, dbx_ref, dlam_ref):
            ref[...] = jnp.zeros(ref.shape, F32)

        def p1(c, carry):
            t0 = pl.multiple_of(c * cr, cr)
            taps = _conv_taps(axp[pl.ds(t0, cr + 8), :], cr)
            xa = cb_ref[...] + w[3] * taps[0] + w[2] * taps[1] + w[1] * taps[2] + w[0] * taps[3]
            a, _ = _rg_local(xa, wa_ref[...], ba_ref[...], wx_ref[...], bx_ref[...], lam_ref[...])
            xa_s[pl.ds(t0, cr), :] = xa
            a_s[pl.ds(t0, cr), :] = a
            ag = ag_ref[pl.ds(t0, cr), :]
            dov = do_ref[pl.ds(t0, cr), :]
            gel, pull = jax.vjp(_gelu, ag)
            g_s[pl.ds(t0, cr), :] = dov * gel
            u_s[pl.ds(t0, cr), :] = a * (dov * gel)
            dag_ref[pl.ds(t0, cr), :] = pull(dov * h_ref[pl.ds(t0, cr), :])[0]
            return carry

        lax.fori_loop(0, t // cr, p1, 0)
        _scan8(a_s, u_s, u_s, t, reverse=True)

        def p3(c, carry):
            t0 = pl.multiple_of(c * cr, cr)
            g = g_s[pl.ds(t0, cr), :] + pltpu.roll(u_s[pl.ds(t0, cr + 8), :], cr + 7, 0)[:cr, :]
            h_prev = pltpu.roll(hp[pl.ds(t0, cr + 8), :], 1, 0)[8:, :]
            _, pull = jax.vjp(_rg_local, xa_s[pl.ds(t0, cr), :], wa_ref[...], ba_ref[...], wx_ref[...], bx_ref[...],
                              lam_ref[...])
            dxa, dwa, dba, dwx, dbx, dlam = pull((g * h_prev, g))
            dxa_s[pl.ds(t0, cr), :] = dxa
            dwa_ref[...] += dwa
            dba_ref[...] += dba
            dwx_ref[...] += dwx
            dbx_ref[...] += dbx
            dlam_ref[...] += dlam
            return carry

        lax.fori_loop(0, t // cr, p3, 0)

        def p4(c, carry):
            t0 = pl.multiple_of(c * cr, cr)
            ext = dxa_s[pl.ds(t0, cr + 8), :]
            n = cr + 8
            ahead = [ext[:cr, :]] + [pltpu.roll(ext, n - s, 0)[:cr, :] for s in (1, 2, 3)]
            dax_ref[pl.ds(t0, cr), :] = w[3] * ahead[0] + w[2] * ahead[1] + w[1] * ahead[2] + w[0] * ahead[3]
            taps = _conv_taps(axp[pl.ds(t0, cr + 8), :], cr)
            dxa = ahead[0]
            for k in range(CONV_WIDTH):
                dcw_ref[pl.ds(k, 1), :] += jnp.sum(dxa * taps[CONV_WIDTH - 1 - k], axis=0, keepdims=True)
            dcb_ref[...] += jnp.sum(dxa, axis=0, keepdims=True)
            return carry

        lax.fori_loop(0, t // cr, p4, 0)

    col = lambda off: pl.BlockSpec((t, LANES), lambda b: (0, off + b))
    vec = pl.BlockSpec((1, LANES), lambda b: (0, b))
    mat = pl.BlockSpec((None, LANES, LANES), lambda b: (b, 0, 0))
    cws = pl.BlockSpec((CONV_WIDTH, LANES), lambda b: (0, b))
    sds = jax.ShapeDtypeStruct
    return pl.pallas_call(
        body, grid=(nb,),
        in_specs=[col(0), col(nb), col(0), col(0), cws, vec, mat, vec, mat, vec, vec],
        out_specs=[col(0), col(0), cws, vec, mat, vec, mat, vec, vec],
        out_shape=[sds((t, D_A), F32), sds((t, D_A), F32), sds((CONV_WIDTH, D_A), F32), sds((1, D_A), F32),
                   sds((nb, LANES, LANES), F32), sds((1, D_A), F32), sds((nb, LANES, LANES), F32), sds((1, D_A), F32),
                   sds((1, D_A), F32)],
        scratch_shapes=[pltpu.VMEM((t + 8, LANES), F32), pltpu.VMEM((t + 8, LANES), F32), pltpu.VMEM((t, LANES), F32),
                        pltpu.VMEM((t, LANES), F32), pltpu.VMEM((t, LANES), F32), pltpu.VMEM((t + 8, LANES), F32),
                        pltpu.VMEM((t + 8, LANES), F32)],
        compiler_params=pltpu.CompilerParams(dimension_semantics=("arbitrary",)), name="rglru_bwd")(
            z, z, h, dout, cw, cb, wa, ba, wx, bx, lam)


def _fgate_fwd(z, bf):
    t = z.shape[0]

    def body(zf_ref, bf_ref, c_ref):
        c_ref[...] = -_softplus(-(zf_ref[...] + bf_ref[...]))
        _scan8(None, c_ref, c_ref, t)

    return pl.pallas_call(
        body, grid=(1,), in_specs=[pl.BlockSpec((t, LANES), lambda i: (0, Z_F // LANES)), pl.BlockSpec((1, LANES), lambda i: (0, 0))],
        out_specs=pl.BlockSpec((t, LANES), lambda i: (0, 0)), out_shape=jax.ShapeDtypeStruct((t, LANES), F32),
        compiler_params=pltpu.CompilerParams(dimension_semantics=("arbitrary",)), name="fgate_fwd")(z, bf)


def _fgate_bwd(z, bf, dc):
    t = z.shape[0]

    def body(zf_ref, bf_ref, dc_ref, dz_ref, db_ref):
        _scan8(None, dc_ref, dz_ref, t, reverse=True)
        dz = dz_ref[...] * _sig(-(zf_ref[...] + bf_ref[...]))
        dz_ref[...] = dz
        db_ref[...] = jnp.sum(dz, axis=0, keepdims=True)

    return pl.pallas_call(
        body, grid=(1,),
        in_specs=[pl.BlockSpec((t, LANES), lambda i: (0, Z_F // LANES)), pl.BlockSpec((1, LANES), lambda i: (0, 0)),
                  pl.BlockSpec((t, LANES), lambda i: (0, 0))],
        out_specs=[pl.BlockSpec((t, LANES), lambda i: (0, 0)), pl.BlockSpec((1, LANES), lambda i: (0, 0))],
        out_shape=[jax.ShapeDtypeStruct((t, LANES), F32), jax.ShapeDtypeStruct((1, LANES), F32)],
        compiler_params=pltpu.CompilerParams(dimension_semantics=("arbitrary",)), name="fgate_bwd")(z, bf, dc)


def _cast_rows(src_ref, dst_ref, t, rows, fn):
    def cp(c, carry):
        r0 = pl.multiple_of(c * rows, rows)
        dst_ref[pl.ds(r0, rows), :] = fn(src_ref[pl.ds(r0, rows), :]).astype(dst_ref.dtype)
        return carry

    lax.fori_loop(0, t // rows, cp, 0)


def _attn_fwd(z, crow, carry=None):
    t = z.shape[0]
    tq = _tile(t, 256)
    nq = t // tq
    scale = HEAD_DIM ** -0.5

    def body(q_ref, k_ref, v_ref, cr_ref, o_ref, lse_ref, kb_s, vb_s):
        lane = lax.broadcasted_iota(jnp.int32, (1, LANES), 1)
        hmask = [(lane // HEAD_DIM) == hh for hh in range(2)]
        tri = lax.broadcasted_iota(jnp.int32, (tq, tq), 0) >= lax.broadcasted_iota(jnp.int32, (tq, tq), 1)
        _cast_rows(k_ref, kb_s, t, tq, lambda v: v)
        _cast_rows(v_ref, vb_s, t, tq, lambda v: v)

        def qblock(i, carry):
            q0 = pl.multiple_of(i * tq, tq)
            qv = q_ref[pl.ds(q0, tq), :] * scale
            qa = [jnp.where(hmask[hh], qv, 0.0).astype(BF16) for hh in range(2)]

            def update(st, k0, masked):
                kb = kb_s[pl.ds(k0, tq), :]
                vb = vb_s[pl.ds(k0, tq), :]
                new = []
                for hh in range(2):
                    m, l, acc = st[hh]
                    s = _dot(qa[hh], kb, NT) - cr_ref[hh, :, pl.ds(k0, tq)]
                    if masked:
                        s = jnp.where(tri, s, NEG)
                    m_new = jnp.maximum(m, jnp.max(s, axis=-1, keepdims=True))
                    p = jnp.exp(s - m_new)
                    corr = jnp.exp(m - m_new)
                    new.append((m_new, corr * l + jnp.sum(p, axis=-1, keepdims=True), corr * acc + _dot(p, vb, NN)))
                return tuple(new)

            one = (jnp.full((tq, 1), NEG, F32), jnp.zeros((tq, 1), F32), jnp.zeros((tq, LANES), F32))
            st = lax.fori_loop(0, i, lambda j, st: update(st, pl.multiple_of(j * tq, tq), False), (one, one))
            st = update(st, q0, True)
            o_ref[pl.ds(q0, tq), :] = jnp.where(hmask[0], st[0][2] / st[0][1], st[1][2] / st[1][1])
            for hh in range(2):
                lse_ref[hh, pl.ds(q0, tq), :] = st[hh][0] + jnp.log(st[hh][1])
            return carry

        lax.fori_loop(0, nq, qblock, 0)

    base = 2 * D_A // LANES
    nh = D_B // LANES
    col = lambda off: pl.BlockSpec((t, LANES), lambda p: (0, off + p))
    return _call(
        body, (nh,), [col(base), col(base + nh), col(base + 2 * nh), pl.BlockSpec((2, 1, t), lambda p: (p, 0, 0))],
        [col(0), pl.BlockSpec((2, t, 1), lambda p: (p, 0, 0))],
        [jax.ShapeDtypeStruct((t, D_B), F32), jax.ShapeDtypeStruct((N_HEADS, t, 1), F32)],
        [pltpu.VMEM((t, LANES), BF16), pltpu.VMEM((t, LANES), BF16)], ("parallel",), "attn_fwd", [z, z, z, crow], carry)


def _attn_bwd(z, crow, lse, do, carry=None):
    t = z.shape[0]
    tq = _tile(t, 256)
    nq = t // tq
    scale = HEAD_DIM ** -0.5

    def body(q_ref, k_ref, v_ref, cr_ref, lse_ref, do_ref, dq_ref, dk_ref, dv_ref, dc_ref, qa_s, da_s, kb_s, vb_s, dl_s):
        lane = lax.broadcasted_iota(jnp.int32, (1, LANES), 1)
        hmask = [(lane // HEAD_DIM) == hh for hh in range(2)]
        tri = lax.broadcasted_iota(jnp.int32, (tq, tq), 0) >= lax.broadcasted_iota(jnp.int32, (tq, tq), 1)
        _cast_rows(k_ref, kb_s, t, tq, lambda v: v)
        _cast_rows(v_ref, vb_s, t, tq, lambda v: v)
        for hh in range(2):
            _cast_rows(q_ref, qa_s.at[hh], t, tq, lambda v, hh=hh: jnp.where(hmask[hh], v * scale, 0.0))
            _cast_rows(do_ref, da_s.at[hh], t, tq, lambda v, hh=hh: jnp.where(hmask[hh], v, 0.0))
        _cast_rows(q_ref, dq_ref, t, tq, lambda v: jnp.zeros_like(v))

        def probs(hh, q0, k0, masked):
            s = _dot(qa_s[hh, pl.ds(q0, tq), :], kb_s[pl.ds(k0, tq), :], NT) - cr_ref[hh, :, pl.ds(k0, tq)]
            p = jnp.exp(s - lse_ref[hh, pl.ds(q0, tq), :])
            if masked:
                p = jnp.where(tri, p, 0.0)
            return p, _dot(da_s[hh, pl.ds(q0, tq), :], vb_s[pl.ds(k0, tq), :], NT)

        def pre(i, carry):
            q0 = pl.multiple_of(i * tq, tq)

            def add(k0, masked, acc):
                res = []
                for hh in range(2):
                    p, dp = probs(hh, q0, k0, masked)
                    res.append(acc[hh] + jnp.sum(p * dp, axis=-1, keepdims=True))
                return tuple(res)

            zcol = jnp.zeros((tq, 1), F32)
            acc = lax.fori_loop(0, i, lambda j, acc: add(pl.multiple_of(j * tq, tq), False, acc), (zcol, zcol))
            acc = add(q0, True, acc)
            for hh in range(2):
                dl_s[hh, pl.ds(q0, tq), :] = acc[hh]
            return carry

        lax.fori_loop(0, nq, pre, 0)

        def kblock(j, carry):
            k0 = pl.multiple_of(j * tq, tq)
            kb = kb_s[pl.ds(k0, tq), :]

            def upd(q0, masked, st):
                dk, dv, dc = st[0], st[1], [st[2], st[3]]
                dqs = []
                for hh in range(2):
                    p, dp = probs(hh, q0, k0, masked)
                    ds = p * (dp - dl_s[hh, pl.ds(q0, tq), :])
                    dv = dv + _dot(p, da_s[hh, pl.ds(q0, tq), :], TN)
                    dk = dk + _dot(ds, qa_s[hh, pl.ds(q0, tq), :], TN)
                    dqs.append(_dot(ds, kb, NN))
                    dc[hh] = dc[hh] - jnp.sum(ds, axis=0, keepdims=True)
                dq_ref[pl.ds(q0, tq), :] += jnp.where(hmask[0], dqs[0], dqs[1]) * scale
                return dk, dv, dc[0], dc[1]

            zero = jnp.zeros((tq, LANES), F32)
            zrow = jnp.zeros((1, tq), F32)
            st = upd(k0, True, (zero, zero, zrow, zrow))
            st = lax.fori_loop(j + 1, nq, lambda i, st: upd(pl.multiple_of(i * tq, tq), False, st), st)
            dk_ref[pl.ds(k0, tq), :] = st[0]
            dv_ref[pl.ds(k0, tq), :] = st[1]
            for hh in range(2):
                dc_ref[hh, :, pl.ds(k0, tq)] = st[2 + hh]
            return carry

        lax.fori_loop(0, nq, kblock, 0)

    base = 2 * D_A // LANES
    nh = D_B // LANES
    col = lambda off: pl.BlockSpec((t, LANES), lambda p: (0, off + p))
    ccs = pl.BlockSpec((2, t, 1), lambda p: (p, 0, 0))
    crs = pl.BlockSpec((2, 1, t), lambda p: (p, 0, 0))
    return _call(
        body, (nh,), [col(base), col(base + nh), col(base + 2 * nh), crs, ccs, col(0)], [col(0), col(0), col(0), crs],
        [jax.ShapeDtypeStruct((t, D_B), F32)] * 3 + [jax.ShapeDtypeStruct((N_HEADS, 1, t), F32)],
        [pltpu.VMEM((2, t, LANES), BF16), pltpu.VMEM((2, t, LANES), BF16), pltpu.VMEM((t, LANES), BF16),
         pltpu.VMEM((t, LANES), BF16), pltpu.VMEM((2, t, 1), F32)], ("parallel",), "attn_bwd", [z, z, z, crow, lse, do], carry)


def _s5_disc(a_re, a_im, log_dt, b_re, b_im):
    dt = jnp.exp(log_dt)
    mag = jnp.exp(a_re * dt)
    ar = mag * jnp.cos(a_im * dt)
    ai = mag * jnp.sin(a_im * dt)
    den = a_re * a_re + a_im * a_im
    kr = ((ar - 1.0) * a_re + ai * a_im) / den
    ki = (ai * a_re - (ar - 1.0) * a_im) / den
    kr3, ki3 = kr[:, None, :], ki[:, None, :]
    return ar, ai, kr3 * b_re - ki3 * b_im, kr3 * b_im + ki3 * b_re


def _s5_prep(a_re, a_im, log_dt, b_re, b_im):
    g, p = a_re.shape
    gc = b_re.shape[1]

    def body(*refs):
        res = _s5_disc(*[r[...] for r in refs[:5]])
        for o_ref, v in zip(refs[5:], res):
            o_ref[...] = v

    sds = jax.ShapeDtypeStruct
    return pl.pallas_call(body, out_shape=[sds((g, p), F32), sds((g, p), F32), sds((g, gc, p), F32), sds((g, gc, p), F32)],
                          name="s5_prep")(a_re, a_im, log_dt, b_re, b_im)


def _s5_prep_bwd(a_re, a_im, log_dt, b_re, b_im, d_ar, d_ai, d_br, d_bi):
    ins = (a_re, a_im, log_dt, b_re, b_im)

    def body(*refs):
        vals = [r[...] for r in refs[:5]]
        cts = tuple(r[...] for r in refs[5:9])
        _, pull = jax.vjp(_s5_disc, *vals)
        for o_ref, v in zip(refs[9:], pull(cts)):
            o_ref[...] = v

    return pl.pallas_call(body, out_shape=[jax.ShapeDtypeStruct(a.shape, F32) for a in ins], name="s5_prep_bwd")(
        *ins, d_ar, d_ai, d_br, d_bi)


def _s5_scan_rows(t, ar, ai, hr_s, hi_s, off, reverse):
    n = ar.shape[1]
    if reverse:
        ai = -ai
    sub = lax.broadcasted_iota(jnp.int32, (8, n), 0)
    cmul = lambda xr, xi, yr, yi: (xr * yr - xi * yi, xr * yi + xi * yr)
    pw = [(ar, ai)]
    for _ in range(7):
        pw.append(cmul(*pw[-1], ar, ai))
    pr = jnp.zeros((8, n), F32)
    pi = jnp.zeros((8, n), F32)
    for r in range(8):
        k = 7 - r if reverse else r
        pr = jnp.where(sub == r, pw[k][0], pr)
        pi = jnp.where(sub == r, pw[k][1], pi)

    def step(g, carry):
        cr, ci = carry
        r0 = pl.multiple_of(off + (t // 8 - 1 - g if reverse else g) * 8, 8)
        br = hr_s[pl.ds(r0, 8), :]
        bi = hi_s[pl.ds(r0, 8), :]
        for s in (1, 2, 4):
            ok = (sub < 8 - s) if reverse else (sub >= s)
            shift = 8 - s if reverse else s
            sr = jnp.where(ok, pltpu.roll(br, shift, 0), 0.0)
            si = jnp.where(ok, pltpu.roll(bi, shift, 0), 0.0)
            mr, mi = cmul(pw[s - 1][0], pw[s - 1][1], sr, si)
            br, bi = br + mr, bi + mi
        mr, mi = cmul(pr, pi, cr, ci)
        br, bi = br + mr, bi + mi
        hr_s[pl.ds(r0, 8), :] = br
        hi_s[pl.ds(r0, 8), :] = bi
        edge = sub == (0 if reverse else 7)
        return (jnp.sum(jnp.where(edge, br, 0.0), axis=0, keepdims=True),
                jnp.sum(jnp.where(edge, bi, 0.0), axis=0, keepdims=True))

    zero = jnp.zeros((1, n), F32)
    lax.fori_loop(0, t // 8, step, (zero, zero))


def _s5_fwd(z, bd_re, bd_im, ab_re, ab_im, cd_re, cd_im, dvec, carry=None):
    t = z.shape[0]
    cr = _tile(t, 256)
    ns = N_STATE // 2

    def body(u_ref, br_ref, bi_ref, ar_ref, ai_ref, cre_ref, cim_ref, d_ref, y_ref, hr_s, hi_s):
        def p1(c, carry):
            t0 = pl.multiple_of(c * cr, cr)
            u = u_ref[pl.ds(t0, cr), :]
            hr_s[pl.ds(t0, cr), :] = _dot(u, br_ref[...], NN)
            hi_s[pl.ds(t0, cr), :] = _dot(u, bi_ref[...], NN)
            return carry

        lax.fori_loop(0, t // cr, p1, 0)
        _s5_scan_rows(t, ar_ref[...], ai_ref[...], hr_s, hi_s, 0, False)

        def p3(c, carry):
            t0 = pl.multiple_of(c * cr, cr)
            y_ref[pl.ds(t0, cr), :] = (_dot(hr_s[pl.ds(t0, cr), :], cre_ref[...], NN)
                                       - _dot(hi_s[pl.ds(t0, cr), :], cim_ref[...], NN)
                                       + d_ref[...] * u_ref[pl.ds(t0, cr), :])
            return carry

        lax.fori_loop(0, t // cr, p3, 0)

    blk = lambda r, c: pl.BlockSpec((None, r, c), lambda b: (b, 0, 0))
    return _call(
        body, (2,),
        [pl.BlockSpec((t, LANES), lambda b: (0, Z_U // LANES + b)), blk(LANES, ns), blk(LANES, ns), blk(1, ns),
         blk(1, ns), blk(ns, LANES), blk(ns, LANES), pl.BlockSpec((1, LANES), lambda b: (0, b))],
        [pl.BlockSpec((t, LANES), lambda b: (0, b))], [jax.ShapeDtypeStruct((t, D_C), F32)],
        [pltpu.VMEM((t, ns), F32), pltpu.VMEM((t, ns), F32)], ("arbitrary",), "s5_fwd",
        [z, bd_re, bd_im, ab_re, ab_im, cd_re, cd_im, dvec], carry)


def _s5_bwd(z, dy, bd_re, bd_im, ab_re, ab_im, cd_re, cd_im, dvec):
    t = z.shape[0]
    cr = _tile(t, 256)
    ns = N_STATE // 2

    def body(u_ref, dy_ref, br_ref, bi_ref, ar_ref, ai_ref, cre_ref, cim_ref, d_ref,
             du_ref, dbr_ref, dbi_ref, dar_ref, dai_ref, dcre_ref, dcim_ref, dd_ref, hr_s, hi_s, gr_s, gi_s):
        zero8 = jnp.zeros((8, ns), F32)
        hr_s[pl.ds(0, 8), :] = zero8
        hi_s[pl.ds(0, 8), :] = zero8
        for ref in (dbr_ref, dbi_ref, dar_ref, dai_ref, dcre_ref, dcim_ref, dd_ref):
            ref[...] = jnp.zeros(ref.shape, F32)

        def p1(c, carry):
            t0 = pl.multiple_of(c * cr, cr)
            u = u_ref[pl.ds(t0, cr), :]
            hr_s[pl.ds(t0 + 8, cr), :] = _dot(u, br_ref[...], NN)
            hi_s[pl.ds(t0 + 8, cr), :] = _dot(u, bi_ref[...], NN)
            return carry

        lax.fori_loop(0, t // cr, p1, 0)
        _s5_scan_rows(t, ar_ref[...], ai_ref[...], hr_s, hi_s, 8, False)

        def p3(c, carry):
            t0 = pl.multiple_of(c * cr, cr)
            dyv = dy_ref[pl.ds(t0, cr), :]
            u = u_ref[pl.ds(t0, cr), :]
            gr_s[pl.ds(t0, cr), :] = _dot(dyv, cre_ref[...], NT)
            gi_s[pl.ds(t0, cr), :] = -_dot(dyv, cim_ref[...], NT)
            dcre_ref[...] += _dot(hr_s[pl.ds(t0 + 8, cr), :], dyv, TN)
            dcim_ref[...] -= _dot(hi_s[pl.ds(t0 + 8, cr), :], dyv, TN)
            dd_ref[...] += jnp.sum(dyv * u, axis=0, keepdims=True)
            du_ref[pl.ds(t0, cr), :] = dyv * d_ref[...]
            return carry

        lax.fori_loop(0, t // cr, p3, 0)
        _s5_scan_rows(t, ar_ref[...], ai_ref[...], gr_s, gi_s, 0, True)

        def p5(c, carry):
            t0 = pl.multiple_of(c * cr, cr)
            u = u_ref[pl.ds(t0, cr), :]
            gr = gr_s[pl.ds(t0, cr), :]
            gi = gi_s[pl.ds(t0, cr), :]
            dbr_ref[...] += _dot(u, gr, TN)
            dbi_ref[...] += _dot(u, gi, TN)
            du_ref[pl.ds(t0, cr), :] += _dot(gr, br_ref[...], NT) + _dot(gi, bi_ref[...], NT)
            hpr = pltpu.roll(hr_s[pl.ds(t0, cr + 8), :], 1, 0)[8:, :]
            hpi = pltpu.roll(hi_s[pl.ds(t0, cr + 8), :], 1, 0)[8:, :]
            dar_ref[...] += jnp.sum(gr * hpr + gi * hpi, axis=0, keepdims=True)
            dai_ref[...] += jnp.sum(gi * hpr - gr * hpi, axis=0, keepdims=True)
            return carry

        lax.fori_loop(0, t // cr, p5, 0)

    blk = lambda r, c: pl.BlockSpec((None, r, c), lambda b: (b, 0, 0))
    ucol = pl.BlockSpec((t, LANES), lambda b: (0, Z_U // LANES + b))
    ycol = pl.BlockSpec((t, LANES), lambda b: (0, b))
    dsp = pl.BlockSpec((1, LANES), lambda b: (0, b))
    sds = jax.ShapeDtypeStruct
    return pl.pallas_call(
        body, grid=(2,),
        in_specs=[ucol, ycol, blk(LANES, ns), blk(LANES, ns), blk(1, ns), blk(1, ns), blk(ns, LANES), blk(ns, LANES), dsp],
        out_specs=[ycol, blk(LANES, ns), blk(LANES, ns), blk(1, ns), blk(1, ns), blk(ns, LANES), blk(ns, LANES), dsp],
        out_shape=[sds((t, D_C), F32), sds((2, LANES, ns), F32), sds((2, LANES, ns), F32), sds((2, 1, ns), F32),
                   sds((2, 1, ns), F32), sds((2, ns, LANES), F32), sds((2, ns, LANES), F32), sds((1, D_C), F32)],
        scratch_shapes=[pltpu.VMEM((t + 8, ns), F32), pltpu.VMEM((t + 8, ns), F32), pltpu.VMEM((t, ns), F32),
                        pltpu.VMEM((t, ns), F32)],
        compiler_params=pltpu.CompilerParams(dimension_semantics=("arbitrary",)), name="s5_bwd")(
            z, dy, bd_re, bd_im, ab_re, ab_im, cd_re, cd_im, dvec)


def _mix_out(out_a, out_b, yc, x1, ga, gb, gc, wglu, wout, ln_g, ln_b):
    yg = _gelu(yc)
    out_c = yg * _sig(_bdot(yg, wglu))
    o = jnp.concatenate([_rms(out_a, ga), _rms(out_b, gb), _rms(out_c, gc)], axis=-1)
    return (_ln(ALPHA * x1 + _bdot(o, wout), ln_g, ln_b),)


def _ln_only(pre, g, b):
    return (_ln(pre, g, b),)


def _loss_head(y, target, tm):
    t, d = y.shape

    def body(y_ref, t_ref, dy_ref, l_ref):
        i = pl.program_id(0)
        e = y_ref[...] - t_ref[...]
        dy_ref[...] = e * (1.0 / d)
        part = 0.5 * jnp.sum(jnp.sum(e * e, axis=-1, keepdims=True) * (1.0 / d), axis=0, keepdims=True)
        row = jnp.where(lax.broadcasted_iota(jnp.int32, (1, LANES), 1) == 0, part, 0.0)

        @pl.when(i == 0)
        def _():
            l_ref[...] = row

        @pl.when(i > 0)
        def _():
            l_ref[...] += row

    spec = pl.BlockSpec((tm, d), lambda i: (i, 0))
    return pl.pallas_call(
        body, grid=(t // tm,), in_specs=[spec, spec], out_specs=[spec, pl.BlockSpec((1, LANES), lambda i: (0, 0))],
        out_shape=[jax.ShapeDtypeStruct((t, d), F32), jax.ShapeDtypeStruct((1, LANES), F32)],
        compiler_params=pltpu.CompilerParams(dimension_semantics=("arbitrary",)), name="loss_head")(y, target)


def _adamw(w, g, m, v, name):
    r, c = w.shape
    tr = r
    for cand in (512, 256, 352, 128):
        if r % cand == 0:
            tr = cand
            break

    def body(w_ref, g_ref, m_ref, v_ref, d_ref, nm_ref, nv_ref):
        gv = g_ref[...]
        mn = ADAM_B1 * m_ref[...] + (1.0 - ADAM_B1) * gv
        vn = ADAM_B2 * v_ref[...] + (1.0 - ADAM_B2) * (gv * gv)
        m_hat = mn / (1.0 - ADAM_B1 ** ADAM_STEP)
        v_hat = vn / (1.0 - ADAM_B2 ** ADAM_STEP)
        d_ref[...] = -ADAM_LR * (m_hat / (jnp.sqrt(v_hat) + ADAM_EPS) + ADAM_WD * w_ref[...])
        nm_ref[...] = mn
        nv_ref[...] = vn

    spec = pl.BlockSpec((tr, c), lambda i: (i, 0))
    return pl.pallas_call(
        body, grid=(r // tr,), in_specs=[spec] * 4, out_specs=[spec] * 3,
        out_shape=[jax.ShapeDtypeStruct((r, c), F32)] * 3,
        compiler_params=pltpu.CompilerParams(dimension_semantics=("parallel",)), name=name)(w, g, m, v)


def _row_tile(r):
    for cand in (512, 448, 352, 256, 128):
        if r % cand == 0:
            return cand
    return r


def _pair_add(a, b, idx, name, out_dtype):
    _, _, r, w = a.shape
    tr = _row_tile(r)

    def body(i_ref, a_ref, b_ref, o_ref):
        o_ref[...] = (a_ref[...].astype(F32) + b_ref[...].astype(F32)).astype(o_ref.dtype)

    grid_spec = pltpu.PrefetchScalarGridSpec(
        num_scalar_prefetch=1, grid=(4, r // tr),
        in_specs=[pl.BlockSpec((None, None, tr, w), lambda q, i, s: (q, s[0], i, 0)),
                  pl.BlockSpec((None, tr, w), lambda q, i, s: (q, i, 0))],
        out_specs=pl.BlockSpec((None, tr, w), lambda q, i, s: (q, i, 0)))
    return pl.pallas_call(body, grid_spec=grid_spec, out_shape=jax.ShapeDtypeStruct((4, r, w), out_dtype),
                          compiler_params=pltpu.CompilerParams(dimension_semantics=("parallel", "parallel")), name=name)(
                              idx, a, b)


def _quad_add(p, rb, idx, name):
    _, r, w = p.shape
    tr = _row_tile(r)

    def body(i_ref, p_ref, r0, r1, r2, o_ref):
        o_ref[...] = ((p_ref[...].astype(F32) + r0[...].astype(F32)) + r1[...].astype(F32)) + r2[...].astype(F32)

    grid_spec = pltpu.PrefetchScalarGridSpec(
        num_scalar_prefetch=1, grid=(r // tr,),
        in_specs=[pl.BlockSpec((None, tr, w), lambda i, s: (s[0], i, 0))]
        + [pl.BlockSpec((None, tr, w), functools.partial(lambda i, s, k: (k, i, 0), k=k)) for k in range(3)],
        out_specs=pl.BlockSpec((tr, w), lambda i, s: (i, 0)))
    return pl.pallas_call(body, grid_spec=grid_spec, out_shape=jax.ShapeDtypeStruct((r, w), F32),
                          compiler_params=pltpu.CompilerParams(dimension_semantics=("parallel",)), name=name)(
                              idx, p, rb, rb, rb)


def _all_reduce_small(buf):
    r, w = buf.shape

    def plan(x, y, c, ins, outs):
        dst = outs[0].at[4 * x + 2 * y + c]
        res = [(ins[0], dst, None)]
        for rel in range(1, 8):
            res.append((ins[0], dst, (x ^ (rel >> 2), y ^ ((rel >> 1) & 1), c ^ (rel & 1))))
        return res

    (allb,) = _run(_Carry([buf], [jax.ShapeDtypeStruct((8, r, w), buf.dtype)], {}, plan, 8), "ar_small")

    def body(a_ref, o_ref):
        s = a_ref[0]
        for k in range(1, 8):
            s = s + a_ref[k]
        o_ref[...] = s

    return pl.pallas_call(body, out_shape=jax.ShapeDtypeStruct((r, w), F32), name="ar_small_sum")(allb)


def _pad_rows(a, rows):
    return jnp.pad(a, ((0, rows - a.shape[0]), (0, 0)))


def _pack_small(arrs):
    rows, tail = [], []
    for a in arrs:
        if not tail and a.size % LANES == 0:
            rows.append(a.reshape(-1, LANES))
        else:
            tail.append(a.reshape(-1))
    n_rows = sum(r.shape[0] for r in rows)
    n_tail = sum(int(v.size) for v in tail)
    tail_rows = -(-n_tail // LANES)
    total_rows = n_rows + tail_rows + (-(n_rows + tail_rows)) % 8
    if tail:
        tail.append(jnp.zeros((tail_rows * LANES - n_tail,), F32))
        rows.append(jnp.concatenate(tail).reshape(tail_rows, LANES))
    if total_rows > n_rows + tail_rows:
        rows.append(jnp.zeros((total_rows - n_rows - tail_rows, LANES), F32))
    return jnp.concatenate(rows, axis=0)


def _unpack_small(buf, shapes):
    out, off = [], 0
    flat = buf.reshape(-1)
    for s in shapes:
        n = int(np.prod(s))
        out.append(flat[off:off + n].reshape(s))
        off += n
    return out


def _block_diag(blocks, nb):
    m, r, c = blocks.shape
    n = m // nb
    eye = jnp.eye(n, dtype=blocks.dtype)
    return (blocks.reshape(nb, n, r, 1, c) * eye[None, :, None, :, None]).reshape(nb, n * r, n * c)


def _diag_blocks(dense, n):
    nb = dense.shape[0]
    r, c = dense.shape[1] // n, dense.shape[2] // n
    eye = jnp.eye(n, dtype=dense.dtype)
    return jnp.sum(dense.reshape(nb, n, r, n, c) * eye[None, :, None, :, None], axis=3).reshape(nb * n, r, c)


def kernel(x, ffn1_w_gate, ffn1_w_up, ffn1_w_down, ln1_g, ln1_b, w_in, conv_w, conv_b, rg_w_a, rg_b_a, rg_w_x, rg_b_x, rg_lambda, fox_b_f, s5_a_re, s5_a_im, s5_log_dt, s5_b_re, s5_b_im, s5_c_re, s5_c_im, s5_d, s5_w_glu, mix_norm_g, w_out, ln2_g, ln2_b, ffn2_w_gate, ffn2_w_up, ffn2_w_down, ln3_g, ln3_b, loss_target, m_ffn1_w_gate, m_ffn1_w_up, m_ffn1_w_down, m_ln1_g, m_ln1_b, m_w_in, m_conv_w, m_conv_b, m_rg_w_a, m_rg_b_a, m_rg_w_x, m_rg_b_x, m_rg_lambda, m_fox_b_f, m_s5_a_re, m_s5_a_im, m_s5_log_dt, m_s5_b_re, m_s5_b_im, m_s5_c_re, m_s5_c_im, m_s5_d, m_s5_w_glu, m_mix_norm_g, m_w_out, m_ln2_g, m_ln2_b, m_ffn2_w_gate, m_ffn2_w_up, m_ffn2_w_down, m_ln3_g, m_ln3_b, v_ffn1_w_gate, v_ffn1_w_up, v_ffn1_w_down, v_ln1_g, v_ln1_b, v_w_in, v_conv_w, v_conv_b, v_rg_w_a, v_rg_b_a, v_rg_w_x, v_rg_b_x, v_rg_lambda, v_fox_b_f, v_s5_a_re, v_s5_a_im, v_s5_log_dt, v_s5_b_re, v_s5_b_im, v_s5_c_re, v_s5_c_im, v_s5_d, v_s5_w_glu, v_mix_norm_g, v_w_out, v_ln2_g, v_ln2_b, v_ffn2_w_gate, v_ffn2_w_up, v_ffn2_w_down, v_ln3_g, v_ln3_b):
    a = dict(locals())
    w = {n: a[n] for n in WEIGHTS}
    t, d = x.shape[1], x.shape[2]
    f = ffn1_w_down.shape[1] * 8
    fs, ds = f // 8, d // 8
    mx, my, mc = lax.axis_index("x"), lax.axis_index("y"), lax.axis_index("c")
    me = 4 * mx + 2 * my + mc
    tm = _tile(t, 512)
    tf = f // 2
    win_rows = ds * Z_W // d

    FFN1, MIXW, FFN2 = ['g1', 'u1', 'd1'], ['win', 'wout', 'glu', 'conv'], ['g2', 'u2', 'd2']
    glu_rows = D_C * D_C // (8 * d)

    def shard_segs(l):
        wi = w['w_in'][l]
        win_p = jnp.concatenate([wi[:, :Z_F + N_HEADS], jnp.zeros((ds, Z_U - Z_F - N_HEADS), F32), wi[:, Z_F + N_HEADS:]], axis=1)
        conv_bits = lax.bitcast_convert_type(w['conv_w'][l], BF16).reshape(1, -1)
        segs = dict(g1=w['ffn1_w_gate'][l].T, u1=w['ffn1_w_up'][l].T, d1=w['ffn1_w_down'][l],
                    g2=w['ffn2_w_gate'][l].T, u2=w['ffn2_w_up'][l].T, d2=w['ffn2_w_down'][l],
                    win=win_p.reshape(win_rows, d), wout=w['w_out'][l], glu=_pad_rows(w['s5_w_glu'][l].reshape(-1, d), 16))
        segs = {k: v.astype(BF16) for k, v in segs.items()}
        segs['conv'] = _pad_rows(jnp.pad(conv_bits, ((0, 0), (0, d - conv_bits.shape[1]))), 16)
        return segs

    shards = [shard_segs(l) for l in range(DEPTH)]
    wts = {}

    def cat(keys):
        return jnp.concatenate([shards[l][k] for l, k in keys], axis=0)

    def split(g, keys):
        off = 0
        for l, k in keys:
            r = shards[l][k].shape[0]
            wts[(l, k)] = g[:, off:off + r]
            off += r

    grp_a = [(0, k) for k in FFN1]
    grp_b = [(0, k) for k in MIXW]
    grp_c = [(0, k) for k in FFN2] + [(1, k) for k in FFN1 + MIXW + FFN2]
    (g_a,) = _run(_ag_chips(cat(grp_a)), "ag_chips")
    (g_a,) = _run(_ag_sibling(g_a), "ag_sibling")
    split(g_a, grp_a)

    xs = x[0]
    saved = []
    cur = xs
    for l in range(DEPTH):
        row = lambda n: w[n][l].reshape(1, -1)
        ffn = lambda keys: tuple(wts[(l, k)].reshape(f, d) for k in keys)
        wa = _block_diag(w['rg_w_a'][l], 3)
        wx = _block_diag(w['rg_w_x'][l], 3)
        bf = jnp.pad(row('fox_b_f'), ((0, 0), (0, LANES - N_HEADS)))
        s5p = (w['s5_a_re'][l], w['s5_a_im'][l], w['s5_log_dt'][l].reshape(-1, 1),
               w['s5_b_re'][l].transpose(0, 2, 1), w['s5_b_im'][l].transpose(0, 2, 1))
        ab_re, ab_im, bb_re, bb_im = _s5_prep(*s5p)
        bd_re, bd_im = _block_diag(bb_re, 2), _block_diag(bb_im, 2)
        cd_re = _block_diag(w['s5_c_re'][l].transpose(0, 2, 1), 2)
        cd_im = _block_diag(w['s5_c_im'][l].transpose(0, 2, 1), 2)
        abr, abi = ab_re.reshape(2, 1, N_STATE // 2), ab_im.reshape(2, 1, N_STATE // 2)
        gm = row('mix_norm_g')
        ga, gb, gc = gm[:, :D_A], gm[:, D_A:D_A + D_B], gm[:, D_A + D_B:]

        x0 = cur
        ffn1 = ffn(FFN1)
        (x1, pre1, gs1, us1), cres = _ffn_fwd(x0, *ffn1, row('ln1_g'), row('ln1_b'), tm, tf,
                                              carry=_ag_chips(cat(grp_b)) if l == 0 else None)
        if l == 0:
            (g_b,) = _run(_ag_sibling(cres[0]), "ag_sibling")
            split(g_b, grp_b)
        win = wts[(l, 'win')].reshape(d, Z_W)
        wout = wts[(l, 'wout')].reshape(d, d).astype(F32)
        wglu = wts[(l, 'glu')][:, :glu_rows].reshape(D_C, D_C).astype(F32)
        conv_full = lax.bitcast_convert_type(
            wts[(l, 'conv')][:, 0, :2 * CONV_WIDTH * D_A // 8].reshape(8, CONV_WIDTH, D_A // 8, 2), F32)
        conv_full = conv_full.transpose(1, 0, 2).reshape(CONV_WIDTH, D_A)
        z = _mm(x1, win, 'nn', F32, tm, Z_W, d, "mix_in")
        out_a, h_a = _rg_fwd(z, conv_full, row('conv_b'), wa, row('rg_b_a'), wx, row('rg_b_x'), row('rg_lambda'))
        cs = _fgate_fwd(z, bf)
        crow = cs[:, :N_HEADS].T.reshape(N_HEADS, 1, t)
        (out_b, lse), cres = _attn_fwd(z, crow, carry=_ag_chips(cat(grp_c)) if l == 0 else None)
        (yc,), cres = _s5_fwd(z, bd_re, bd_im, abr, abi, cd_re, cd_im, row('s5_d'),
                              carry=_ag_sibling(cres[0]) if l == 0 else None)
        if l == 0:
            split(cres[0], grp_c)
        mix_params = [ga, gb, gc, wglu, wout, row('ln2_g'), row('ln2_b')]
        (x2,) = _rowwise(_mix_out, [out_a, out_b, yc, x1], mix_params, [(d, F32)], _tile(t, 256), "mix_out")
        ffn2 = ffn(FFN2)
        (x3, pre3, gs2, us2), _ = _ffn_fwd(x2, *ffn2, row('ln3_g'), row('ln3_b'), tm, tf)
        saved.append(dict(x0=x0, x1=x1, pre1=pre1, gs1=gs1, us1=us1, z=z, out_a=out_a, h_a=h_a, crow=crow,
                          out_b=out_b, lse=lse, yc=yc, x2=x2, pre3=pre3, gs2=gs2, us2=us2, mix_params=mix_params,
                          ffn1=ffn1, ffn2=ffn2, win=win, conv_full=conv_full, wa=wa, wx=wx, bf=bf, s5p=s5p,
                          s5m=(bd_re, bd_im, abr, abi, cd_re, cd_im)))
        cur = x3

    dy, loss_row = _loss_head(cur, loss_target[0], tm)
    loss = lax.psum(loss_row[0, 0], ("x", "y", "c"))

    assert DEPTH == 2
    small_grads = {}
    c_idx = jnp.reshape(mc, (1,)).astype(jnp.int32)
    chip_idx = jnp.reshape(2 * mx + my, (1,)).astype(jnp.int32)

    def blocks(arrs):
        return jnp.concatenate([v.astype(BF16).reshape(8, -1, d) for v in arrs], axis=1)

    def mixer_blocks(dwin, dwout, dwglu):
        glu = jnp.pad(dwglu.astype(BF16).reshape(8, glu_rows, d), ((0, 0), (0, 32 - glu_rows), (0, 0)))
        return jnp.concatenate([dwin.reshape(8, win_rows, d), dwout.astype(BF16).reshape(8, ds, d), glu], axis=1)

    def pair(full, ra):
        return _pair_add(full.reshape(4, 2, full.shape[1], d), ra, c_idx, "rs_add_sibling", BF16)

    for l in reversed(range(DEPTH)):
        s = saved[l]
        row = lambda n: w[n][l].reshape(1, -1)
        wg_tiles = dict(tm=tf, tn=d, tk=_tile(t, 1024))
        first = l == 0

        def ffn_back(dyv, pre, gs, us, wts3, xin, ln_g, ln_b, carry):
            (dpre,), (dlg, dlb), _ = _rowwise_vjp(_ln_only, [pre], [ln_g, ln_b], [dyv], _tile(t, 256), "ln_bwd")
            (dx, dg, du, hh), cres = _ffn_bwd(dpre, gs, us, *wts3, tm, tf, carry=carry)
            dwg = _mm(dg, xin, 'tn', BF16, name="ffn_dw_gate", **wg_tiles)
            dwu = _mm(du, xin, 'tn', BF16, name="ffn_dw_up", **wg_tiles)
            dwd = _mm(hh, dpre, 'tn', BF16, name="ffn_dw_down", **wg_tiles)
            return dx, dwg, dwu, dwd, dlg, dlb, cres

        dx2, dwg2, dwu2, dwd2, dl3g, dl3b, cres = ffn_back(dy, s['pre3'], s['gs2'], s['us2'], s['ffn2'], s['x2'], row('ln3_g'),
                                                           row('ln3_b'), _rs_sibling(full_l1) if first else None)
        if first:
            part_l1 = pair(full_l1, cres[0])
            full_c2 = blocks([dwg2, dwu2, dwd2])
        (d_oa, d_ob, d_yc, d_x1), (dga, dgb, dgc, dwglu, dwout, dl2g, dl2b), cres = _rowwise_vjp(
            _mix_out, [s['out_a'], s['out_b'], s['yc'], s['x1']], s['mix_params'], [dx2], _tile(t, 256), "mix_out_bwd",
            carry=_rs_sibling(full_c2) if first else None)
        if first:
            part_c2 = pair(full_c2, cres[0])
        (d_ax, d_ag, dcw, dcb, dwa, dba, dwx, dbx, dlam) = _rg_bwd(
            s['z'], s['h_a'], d_oa, s['conv_full'], row('conv_b'), s['wa'], row('rg_b_a'), s['wx'], row('rg_b_x'), row('rg_lambda'))
        (dq, dk, dv, dcrow), cres = _attn_bwd(s['z'], s['crow'], s['lse'], d_ob,
                                              carry=_join(_rs_chips(part_l1), _rs_chips(part_c2)) if first else None)
        if first:
            rb_l1, rb_c2 = cres
        dc_pad = jnp.pad(dcrow.reshape(N_HEADS, t).T, ((0, 0), (0, LANES - N_HEADS)))
        dzf, dbf = _fgate_bwd(s['z'], s['bf'], dc_pad)
        du_c, dbd_re, dbd_im, dabr, dabi, dcd_re, dcd_im, dd = _s5_bwd(s['z'], d_yc, *s['s5m'], row('s5_d'))
        dz = jnp.concatenate([d_ax, d_ag, dq, dk, dv, dzf, du_c], axis=1)
        dx1 = _mm(dz, s['win'], 'nt', F32, tm, d, Z_W, "mix_in_dx", add=d_x1)
        dwin = _mm(s['x1'], dz, 'tn', BF16, d, Z_W, tm, "mix_in_dw")
        if first:
            full_m = mixer_blocks(dwin, dwout, dwglu)
            (ra_m,) = _run(_rs_sibling(full_m), "rs_sibling")
            part_m = pair(full_m, ra_m)
        dx0, dwg1, dwu1, dwd1, dl1g, dl1b, cres = ffn_back(dx1, s['pre1'], s['gs1'], s['us1'], s['ffn1'], s['x0'], row('ln1_g'),
                                                           row('ln1_b'), _rs_chips(part_m) if first else None)
        if first:
            rb_m = cres[0]
            full_c1 = blocks([dwg1, dwu1, dwd1])
            (ra_c1,) = _run(_rs_sibling(full_c1), "rs_sibling")
            part_c1 = pair(full_c1, ra_c1)
            (rb_c1,) = _run(_rs_chips(part_c1), "rs_chips")
        else:
            full_l1 = jnp.concatenate([blocks([dwg1, dwu1, dwd1, dwg2, dwu2, dwd2]), mixer_blocks(dwin, dwout, dwglu)], axis=1)
        dy = dx0

        dbb_re, dbb_im = _diag_blocks(dbd_re, N_GROUPS // 2), _diag_blocks(dbd_im, N_GROUPS // 2)
        dcm_re, dcm_im = _diag_blocks(dcd_re, N_GROUPS // 2), _diag_blocks(dcd_im, N_GROUPS // 2)
        da_re, da_im, dlog_dt, db_re, db_im = _s5_prep_bwd(*s['s5p'], dabr.reshape(N_GROUPS, C_STATE), dabi.reshape(N_GROUPS, C_STATE), dbb_re, dbb_im)

        sg = dict(ln1_g=dl1g, ln1_b=dl1b, conv_w=dcw, conv_b=dcb, rg_w_a=_diag_blocks(dwa, 2), rg_b_a=dba,
                  rg_w_x=_diag_blocks(dwx, 2), rg_b_x=dbx, rg_lambda=dlam, fox_b_f=dbf[:, :N_HEADS],
                  s5_a_re=da_re, s5_a_im=da_im, s5_log_dt=dlog_dt, s5_b_re=db_re.transpose(0, 2, 1), s5_b_im=db_im.transpose(0, 2, 1),
                  s5_c_re=dcm_re.transpose(0, 2, 1), s5_c_im=dcm_im.transpose(0, 2, 1), s5_d=dd,
                  mix_norm_g=jnp.concatenate([dga, dgb, dgc], axis=1), ln2_g=dl2g, ln2_b=dl2b, ln3_g=dl3g, ln3_b=dl3b)
        small_grads[l] = sg

    grad_x = dy.reshape(x.shape)
    quad = lambda part, rb: _quad_add(part, rb, chip_idx, "rs_add_chips")
    own = {}

    def take(rows_f32, keys, l):
        off = 0
        for k, r in keys:
            own[(l, k)] = rows_f32[off:off + r]
            off += r

    ffn_keys = lambda names: [(k, fs) for k in names]
    mix_keys = [('win', win_rows), ('wout', ds), ('glu', glu_rows)]
    take(quad(part_l1, rb_l1), ffn_keys(FFN1 + FFN2) + mix_keys, 1)
    take(quad(part_c2, rb_c2), ffn_keys(FFN2), 0)
    take(quad(part_m, rb_m), mix_keys, 0)
    take(quad(part_c1, rb_c1), ffn_keys(FFN1), 0)

    grads = {}
    for k, n in zip(FFN1 + FFN2, ['ffn1_w_gate', 'ffn1_w_up', 'ffn1_w_down', 'ffn2_w_gate', 'ffn2_w_up', 'ffn2_w_down']):
        grads[n] = jnp.stack([own[(l, k)].T if 'down' not in n else own[(l, k)] for l in range(DEPTH)])
    gwin = jnp.stack([own[(l, 'win')].reshape(ds, Z_W) for l in range(DEPTH)])
    grads['w_in'] = jnp.concatenate([gwin[:, :, :Z_F + N_HEADS], gwin[:, :, Z_U:]], axis=2)
    grads['w_out'] = jnp.stack([own[(l, 'wout')] for l in range(DEPTH)])
    grads['s5_w_glu'] = jnp.stack([own[(l, 'glu')].reshape(D_C // 8, D_C) for l in range(DEPTH)])

    small_names = ['conv_w'] + SMALL
    small_shapes = [((DEPTH, CONV_WIDTH, D_A) if n == 'conv_w' else w[n].shape) for n in small_names]
    conv_zero = jnp.zeros((DEPTH, CONV_WIDTH, D_A), F32)
    summed = _all_reduce_small(_pack_small([small_grads[l][n] for n in small_names for l in range(DEPTH)]))
    for n, g in zip(small_names, _unpack_small(summed, small_shapes)):
        grads[n] = g
    grads['conv_w'] = lax.dynamic_slice_in_dim(grads['conv_w'], me * (D_A // 8), D_A // 8, axis=2)

    delta, new_m, new_v = {}, {}, {}
    for n in BIG + ['conv_w']:
        sh = w[n].shape
        two = lambda v: v.reshape(-1, sh[-1])
        dl, nm, nv = _adamw(two(w[n]), two(grads[n]), two(a['m_' + n]), two(a['v_' + n]), "adamw_" + n)
        delta[n], new_m[n], new_v[n] = dl.reshape(sh), nm.reshape(sh), nv.reshape(sh)
    dl, nm, nv = _adamw(_pack_small([conv_zero] + [w[n] for n in SMALL]), summed,
                        _pack_small([conv_zero] + [a['m_' + n] for n in SMALL]),
                        _pack_small([conv_zero] + [a['v_' + n] for n in SMALL]), "adamw_small")
    for n, v1, v2, v3 in zip(small_names[1:], _unpack_small(dl, small_shapes)[1:], _unpack_small(nm, small_shapes)[1:],
                             _unpack_small(nv, small_shapes)[1:]):
        delta[n], new_m[n], new_v[n] = v1, v2, v3

    return (loss, grad_x, *[grads[n] for n in WEIGHTS], *[delta[n] for n in WEIGHTS], *[new_m[n] for n in WEIGHTS],
            *[new_v[n] for n in WEIGHTS])
```

```python
import functools
import math

import jax
import jax.numpy as jnp
import numpy as np
from jax import lax
from jax.experimental import pallas as pl
from jax.experimental.pallas import tpu as pltpu

F32 = jnp.float32
BF16 = jnp.bfloat16
MESH = pl.DeviceIdType.MESH

DEPTH = 2
ALPHA = (2 * DEPTH) ** 0.25
LN_EPS = 1e-5
RMS_EPS = 1e-6
RG_C = 8.0
CONV_WIDTH = 4
HEAD_DIM = 64
C_GROUP = 16
C_STATE = 64
D_A = 384
D_B = 384
D_C = 256
N_HEADS = D_B // HEAD_DIM
N_GROUPS = D_C // C_GROUP
N_STATE = N_GROUPS * C_STATE
Z_F = 2 * D_A + 3 * D_B
Z_U = Z_F + 128
Z_W = Z_U + D_C
N_IN = Z_F + N_HEADS + D_C
ADAM_LR, ADAM_B1, ADAM_B2, ADAM_EPS, ADAM_WD, ADAM_STEP = 0.001, 0.9, 0.999, 1e-08, 0.01, 10
LANES = 128
NEG = -1e30

WEIGHTS = ['ffn1_w_gate', 'ffn1_w_up', 'ffn1_w_down', 'ln1_g', 'ln1_b', 'w_in', 'conv_w', 'conv_b', 'rg_w_a', 'rg_b_a',
           'rg_w_x', 'rg_b_x', 'rg_lambda', 'fox_b_f', 's5_a_re', 's5_a_im', 's5_log_dt', 's5_b_re', 's5_b_im', 's5_c_re',
           's5_c_im', 's5_d', 's5_w_glu', 'mix_norm_g', 'w_out', 'ln2_g', 'ln2_b', 'ffn2_w_gate', 'ffn2_w_up', 'ffn2_w_down',
           'ln3_g', 'ln3_b']
BIG = ['ffn1_w_gate', 'ffn1_w_up', 'ffn1_w_down', 'w_in', 's5_w_glu', 'w_out', 'ffn2_w_gate', 'ffn2_w_up', 'ffn2_w_down']
SMALL_TAIL = ['fox_b_f', 's5_log_dt']
SMALL = [n for n in WEIGHTS if n not in BIG and n != 'conv_w' and n not in SMALL_TAIL] + SMALL_TAIL


def _sig(x):
    return 1.0 / (1.0 + jnp.exp(-x))


def _gelu(x):
    return 0.5 * x * (1.0 + jnp.tanh(math.sqrt(2.0 / math.pi) * (x + 0.044715 * (x * x * x))))


def _softplus(x):
    return jnp.maximum(x, 0.0) + jnp.log(1.0 + jnp.exp(jnp.minimum(x, -x)))


def _dot(a, b, dims):
    return lax.dot_general(a.astype(BF16), b.astype(BF16), (dims, ((), ())), preferred_element_type=F32)


NN = ((1,), (0,))
NT = ((1,), (1,))
TN = ((0,), (0,))


@jax.custom_vjp
def _bdot(a, w):
    return _dot(a, w, NN)


def _bdot_fwd(a, w):
    return _dot(a, w, NN), (a, w)


def _bdot_bwd(res, ct):
    a, w = res
    return _dot(ct, w, NT), _dot(a, ct, TN)


_bdot.defvjp(_bdot_fwd, _bdot_bwd)


def _ln(pre, g, b):
    mu = jnp.mean(pre, axis=-1, keepdims=True)
    xc = pre - mu
    var = jnp.mean(xc * xc, axis=-1, keepdims=True)
    return xc * lax.rsqrt(var + LN_EPS) * g + b


def _rms(x, g):
    return x * lax.rsqrt(jnp.mean(x * x, axis=-1, keepdims=True) + RMS_EPS) * g


def _tile(n, want):
    return want if n % want == 0 else n


class _Carry:
    def __init__(self, ins, outs, aliases, plan, n):
        self.ins, self.outs, self.aliases, self.plan, self.n = list(ins), list(outs), dict(aliases), plan, n


def _join(a, b):
    na, ma = len(a.ins), len(a.outs)

    def plan(x, y, c, ins, outs):
        return a.plan(x, y, c, ins[:na], outs[:ma]) + b.plan(x, y, c, ins[na:], outs[ma:])

    aliases = dict(a.aliases)
    aliases.update({na + i: ma + j for i, j in b.aliases.items()})
    return _Carry(a.ins + b.ins, a.outs + b.outs, aliases, plan, a.n + b.n)


def _copies(carry, cins, couts, send, recv):
    x, y, c = lax.axis_index("x"), lax.axis_index("y"), lax.axis_index("c")
    res = []
    for k, (s, d, peer) in enumerate(carry.plan(x, y, c, cins, couts)):
        if peer is None:
            res.append(pltpu.make_async_copy(s, d, send.at[k]))
        else:
            res.append(pltpu.make_async_remote_copy(src_ref=s, dst_ref=d, send_sem=send.at[k], recv_sem=recv.at[k],
                                                    device_id=peer, device_id_type=MESH))
    return res


def _call(body, grid, in_specs, out_specs, out_shape, scratch, semantics, name, args, carry=None):
    n_in, n_out, n_scr = len(in_specs), len(out_specs), len(scratch)
    if carry is None:
        res = pl.pallas_call(body, grid=grid, in_specs=in_specs, out_specs=out_specs, out_shape=out_shape,
                             scratch_shapes=scratch, compiler_params=pltpu.CompilerParams(dimension_semantics=semantics),
                             name=name)(*args)
        return list(res), []
    nci, nco = len(carry.ins), len(carry.outs)

    def wrapped(*refs):
        o0 = n_in + nci
        s0 = o0 + n_out + nco
        cins, couts = refs[n_in:o0], refs[o0 + n_out:s0]
        send, recv = refs[s0 + n_scr:]
        first = functools.reduce(jnp.logical_and, [pl.program_id(k) == 0 for k in range(len(grid))])
        last = functools.reduce(jnp.logical_and, [pl.program_id(k) == grid[k] - 1 for k in range(len(grid))])

        @pl.when(first)
        def _():
            for cp in _copies(carry, cins, couts, send, recv):
                cp.start()

        body(*refs[:n_in], *refs[o0:o0 + n_out], *refs[s0:s0 + n_scr])

        @pl.when(last)
        def _():
            for cp in _copies(carry, cins, couts, send, recv):
                cp.wait()

    hbm = pl.BlockSpec(memory_space=pl.ANY)
    res = pl.pallas_call(
        wrapped, grid=grid, in_specs=list(in_specs) + [hbm] * nci, out_specs=list(out_specs) + [hbm] * nco,
        out_shape=list(out_shape) + carry.outs, scratch_shapes=list(scratch) + [pltpu.SemaphoreType.DMA((carry.n,))] * 2,
        input_output_aliases={n_in + i: n_out + j for i, j in carry.aliases.items()},
        compiler_params=pltpu.CompilerParams(dimension_semantics=("arbitrary",) * len(grid), has_side_effects=True),
        name=name)(*args, *carry.ins)
    return list(res[:n_out]), list(res[n_out:])


def _run(carry, name):
    nci, nco = len(carry.ins), len(carry.outs)

    def body(*refs):
        cps = _copies(carry, refs[:nci], refs[nci:nci + nco], refs[-2], refs[-1])
        for cp in cps:
            cp.start()
        for cp in cps:
            cp.wait()

    hbm = pl.BlockSpec(memory_space=pl.ANY)
    return pl.pallas_call(
        body, in_specs=[hbm] * nci, out_specs=[hbm] * nco, out_shape=carry.outs, input_output_aliases=carry.aliases,
        scratch_shapes=[pltpu.SemaphoreType.DMA((carry.n,))] * 2, compiler_params=pltpu.CompilerParams(has_side_effects=True),
        name=name)(*carry.ins)


def _ag_chips(shard):
    def plan(x, y, c, ins, outs):
        dst = outs[0].at[4 * x + 2 * y + c]
        return [(ins[0], dst, None)] + [(ins[0], dst, (px, py, c)) for px, py in ((1 - x, y), (x, 1 - y), (1 - x, 1 - y))]

    return _Carry([shard], [jax.ShapeDtypeStruct((8,) + shard.shape, shard.dtype)], {}, plan, 4)


def _ag_sibling(g):
    def plan(x, y, c, ins, outs):
        return [(outs[0].at[2 * q + c], outs[0].at[2 * q + c], (x, y, 1 - c)) for q in range(4)]

    return _Carry([g], [jax.ShapeDtypeStruct(g.shape, g.dtype)], {0: 0}, plan, 4)


def _rs_sibling(full):
    def plan(x, y, c, ins, outs):
        return [(ins[0].at[2 * q + (1 - c)], outs[0].at[q], (x, y, 1 - c)) for q in range(4)]

    return _Carry([full], [jax.ShapeDtypeStruct((4,) + full.shape[1:], full.dtype)], {}, plan, 4)


def _rs_chips(part):
    def plan(x, y, c, ins, outs):
        res = []
        for k, (dx, dy) in enumerate(((1, 0), (0, 1), (1, 1))):
            tx, ty = x ^ dx, y ^ dy
            res.append((ins[0].at[2 * tx + ty], outs[0].at[k], (tx, ty, c)))
        return res

    return _Carry([part], [jax.ShapeDtypeStruct((3,) + part.shape[1:], part.dtype)], {}, plan, 3)


def _mm(a, b, dims, out_dtype, tm, tn, tk, name, add=None):
    if dims == 'nn':
        (m, k), n = a.shape, b.shape[1]
        a_spec = pl.BlockSpec((tm, tk), lambda i, j, q: (i, q))
        b_spec = pl.BlockSpec((tk, tn), lambda i, j, q: (q, j))
        dn = NN
    elif dims == 'nt':
        (m, k), n = a.shape, b.shape[0]
        a_spec = pl.BlockSpec((tm, tk), lambda i, j, q: (i, q))
        b_spec = pl.BlockSpec((tn, tk), lambda i, j, q: (j, q))
        dn = NT
    else:
        (k, m), n = a.shape, b.shape[1]
        a_spec = pl.BlockSpec((tk, tm), lambda i, j, q: (q, i))
        b_spec = pl.BlockSpec((tk, tn), lambda i, j, q: (q, j))
        dn = TN
    nk = k // tk
    o_spec = pl.BlockSpec((tm, tn), lambda i, j, q: (i, j))

    def body(*refs):
        if add is None:
            a_ref, b_ref, o_ref, acc_ref = refs
        else:
            a_ref, b_ref, add_ref, o_ref, acc_ref = refs
        q = pl.program_id(2)
        part = _dot(a_ref[...], b_ref[...], dn)

        @pl.when(q == 0)
        def _():
            acc_ref[...] = part

        @pl.when(q > 0)
        def _():
            acc_ref[...] += part

        @pl.when(q == nk - 1)
        def _():
            r = acc_ref[...]
            if add is not None:
                r = r + add_ref[...]
            o_ref[...] = r.astype(o_ref.dtype)

    ins = [a, b] + ([] if add is None else [add])
    specs = [a_spec, b_spec] + ([] if add is None else [o_spec])
    return pl.pallas_call(
        body, grid=(m // tm, n // tn, nk), in_specs=specs, out_specs=o_spec,
        out_shape=jax.ShapeDtypeStruct((m, n), out_dtype), scratch_shapes=[pltpu.VMEM((tm, tn), F32)],
        compiler_params=pltpu.CompilerParams(dimension_semantics=("parallel", "parallel", "arbitrary")), name=name)(*ins)


def _rowwise(fn, rows, params, outs, tm, name):
    t = rows[0].shape[0]
    nr, npar = len(rows), len(params)

    def body(*refs):
        r = [x[...] for x in refs[:nr]]
        p = [x[...] for x in refs[nr:nr + npar]]
        res = fn(*r, *p)
        for o_ref, o in zip(refs[nr + npar:], res):
            o_ref[...] = o.astype(o_ref.dtype)

    in_specs = ([pl.BlockSpec((tm, a.shape[1]), lambda i: (i, 0)) for a in rows]
                + [pl.BlockSpec(p.shape, lambda i: (0, 0)) for p in params])
    return pl.pallas_call(
        body, grid=(t // tm,), in_specs=in_specs,
        out_specs=[pl.BlockSpec((tm, c), lambda i: (i, 0)) for c, _ in outs],
        out_shape=[jax.ShapeDtypeStruct((t, c), d) for c, d in outs],
        compiler_params=pltpu.CompilerParams(dimension_semantics=("parallel",)), name=name)(*rows, *params)


def _rowwise_vjp(fn, rows, params, cots, tm, name, carry=None):
    t = rows[0].shape[0]
    nr, npar, nc = len(rows), len(params), len(cots)

    def body(*refs):
        r = [x[...] for x in refs[:nr]]
        p = [x[...] for x in refs[nr:nr + npar]]
        c = [x[...] for x in refs[nr + npar:nr + npar + nc]]
        o_refs = refs[nr + npar + nc:]
        _, pull = jax.vjp(fn, *r, *p)
        grads = pull(tuple(c))
        for o_ref, g in zip(o_refs[:nr], grads[:nr]):
            o_ref[...] = g
        i = pl.program_id(0)

        @pl.when(i == 0)
        def _():
            for o_ref, g in zip(o_refs[nr:], grads[nr:]):
                o_ref[...] = g

        @pl.when(i > 0)
        def _():
            for o_ref, g in zip(o_refs[nr:], grads[nr:]):
                o_ref[...] += g

    row_spec = lambda a: pl.BlockSpec((tm, a.shape[1]), lambda i: (i, 0))
    par_spec = lambda p: pl.BlockSpec(p.shape, lambda i: (0, 0))
    res, cres = _call(
        body, (t // tm,),
        [row_spec(a) for a in rows] + [par_spec(p) for p in params] + [row_spec(a) for a in cots],
        [row_spec(a) for a in rows] + [par_spec(p) for p in params],
        [jax.ShapeDtypeStruct(a.shape, F32) for a in rows] + [jax.ShapeDtypeStruct(p.shape, F32) for p in params],
        [], ("arbitrary",), name, [*rows, *params, *cots], carry)
    return res[:nr], res[nr:], cres


def _ffn_fwd(x, wgt, wut, wd, ln_g, ln_b, tm, tf, carry=None):
    t, d = x.shape
    f = wgt.shape[0]
    nj = f // tf

    def body(x_ref, wg_ref, wu_ref, wd_ref, g_ref, b_ref, y_ref, pre_ref, gs_ref, us_ref, acc_ref):
        j = pl.program_id(1)
        xv = x_ref[...]
        xb = xv.astype(BF16)
        g = _dot(xb, wg_ref[...], NT)
        u = _dot(xb, wu_ref[...], NT)
        gs_ref[...] = g.astype(BF16)
        us_ref[...] = u.astype(BF16)
        part = _dot(g * _sig(g) * u, wd_ref[...], NN)

        @pl.when(j == 0)
        def _():
            acc_ref[...] = part

        @pl.when(j > 0)
        def _():
            acc_ref[...] += part

        @pl.when(j == nj - 1)
        def _():
            pre = ALPHA * xv + 0.5 * acc_ref[...]
            pre_ref[...] = pre
            y_ref[...] = _ln(pre, g_ref[...], b_ref[...])

    w_spec = pl.BlockSpec((tf, d), lambda i, j: (j, 0))
    x_spec = pl.BlockSpec((tm, d), lambda i, j: (i, 0))
    v_spec = pl.BlockSpec((1, d), lambda i, j: (0, 0))
    h_spec = pl.BlockSpec((tm, tf), lambda i, j: (i, j))
    return _call(
        body, (t // tm, nj), [x_spec, w_spec, w_spec, w_spec, v_spec, v_spec], [x_spec, x_spec, h_spec, h_spec],
        [jax.ShapeDtypeStruct((t, d), F32), jax.ShapeDtypeStruct((t, d), F32),
         jax.ShapeDtypeStruct((t, f), BF16), jax.ShapeDtypeStruct((t, f), BF16)],
        [pltpu.VMEM((tm, d), F32)], ("parallel", "arbitrary"), "ffn_fwd", [x, wgt, wut, wd, ln_g, ln_b], carry)


def _ffn_bwd(dpre, gs, us, wgt, wut, wd, tm, tf, carry=None):
    t, d = dpre.shape
    f = wgt.shape[0]
    nj = f // tf

    def body(dp_ref, gs_ref, us_ref, wg_ref, wu_ref, wd_ref, dx_ref, dg_ref, du_ref, hh_ref, acc_ref):
        j = pl.program_id(1)
        dp = dp_ref[...]
        dh = _dot(0.5 * dp, wd_ref[...], NT)
        g = gs_ref[...].astype(F32)
        u = us_ref[...].astype(F32)
        s = _sig(g)
        sl = g * s
        dg = (dh * u * (s * (1.0 + g * (1.0 - s)))).astype(BF16)
        du = (dh * sl).astype(BF16)
        dg_ref[...] = dg
        du_ref[...] = du
        hh_ref[...] = (0.5 * sl * u).astype(BF16)
        part = _dot(dg, wg_ref[...], NN) + _dot(du, wu_ref[...], NN)

        @pl.when(j == 0)
        def _():
            acc_ref[...] = part

        @pl.when(j > 0)
        def _():
            acc_ref[...] += part

        @pl.when(j == nj - 1)
        def _():
            dx_ref[...] = ALPHA * dp + acc_ref[...]

    w_spec = pl.BlockSpec((tf, d), lambda i, j: (j, 0))
    x_spec = pl.BlockSpec((tm, d), lambda i, j: (i, 0))
    h_spec = pl.BlockSpec((tm, tf), lambda i, j: (i, j))
    return _call(
        body, (t // tm, nj), [x_spec, h_spec, h_spec, w_spec, w_spec, w_spec], [x_spec, h_spec, h_spec, h_spec],
        [jax.ShapeDtypeStruct((t, d), F32)] + [jax.ShapeDtypeStruct((t, f), BF16)] * 3,
        [pltpu.VMEM((tm, d), F32)], ("parallel", "arbitrary"), "ffn_bwd", [dpre, gs, us, wgt, wut, wd], carry)


def _scan8(a_ref, b_ref, out_ref, t, reverse=False):
    w = out_ref.shape[-1]
    sub = lax.broadcasted_iota(jnp.int32, (8, w), 0)

    def step(g, carry):
        r0 = pl.multiple_of((t // 8 - 1 - g if reverse else g) * 8, 8)
        bv = b_ref[pl.ds(r0, 8), :]
        av = None if a_ref is None else a_ref[pl.ds(r0, 8), :]
        for s in (1, 2, 4):
            ok = (sub < 8 - s) if reverse else (sub >= s)
            shift = 8 - s if reverse else s
            b_sh = jnp.where(ok, pltpu.roll(bv, shift, 0), 0.0)
            if av is None:
                bv = bv + b_sh
            else:
                bv = av * b_sh + bv
                av = av * jnp.where(ok, pltpu.roll(av, shift, 0), 1.0)
        h = bv + carry if av is None else bv + av * carry
        out_ref[pl.ds(r0, 8), :] = h
        return jnp.sum(jnp.where(sub == (0 if reverse else 7), h, 0.0), axis=0, keepdims=True)

    lax.fori_loop(0, t // 8, step, jnp.zeros((1, w), F32))


def _rg_local(xa, wa, ba, wx, bx, lam):
    r = _sig(_bdot(xa, wa) + ba)
    i = _sig(_bdot(xa, wx) + bx)
    log_a = -RG_C * r * _softplus(-lam)
    a = jnp.exp(log_a)
    mult = jnp.sqrt(-jnp.tanh(log_a) * (a * a + 1.0))
    return a, mult * (i * xa)


def _conv_taps(ext, n):
    return [ext[8:, :]] + [pltpu.roll(ext, s, 0)[8:, :] for s in (1, 2, 3)]


def _rg_fwd(z, cw, cb, wa, ba, wx, bx, lam):
    t = z.shape[0]
    cr = _tile(t, 256)
    nb = D_A // LANES

    def body(ax_ref, ag_ref, cw_ref, cb_ref, wa_ref, ba_ref, wx_ref, bx_ref, lam_ref, out_ref, h_ref, axp, a_s, b_s):
        axp[pl.ds(0, 8), :] = jnp.zeros((8, LANES), F32)
        pltpu.sync_copy(ax_ref, axp.at[pl.ds(8, t)])
        w = [cw_ref[pl.ds(k, 1), :] for k in range(CONV_WIDTH)]

        def chunk(c, carry):
            t0 = pl.multiple_of(c * cr, cr)
            taps = _conv_taps(axp[pl.ds(t0, cr + 8), :], cr)
            xa = cb_ref[...] + w[3] * taps[0] + w[2] * taps[1] + w[1] * taps[2] + w[0] * taps[3]
            a, gated = _rg_local(xa, wa_ref[...], ba_ref[...], wx_ref[...], bx_ref[...], lam_ref[...])
            a_s[pl.ds(t0, cr), :] = a
            b_s[pl.ds(t0, cr), :] = gated
            return carry

        lax.fori_loop(0, t // cr, chunk, 0)

        _scan8(a_s, b_s, h_ref, t)

        def fin(c, carry):
            t0 = pl.multiple_of(c * cr, cr)
            out_ref[pl.ds(t0, cr), :] = _gelu(ag_ref[pl.ds(t0, cr), :]) * h_ref[pl.ds(t0, cr), :]
            return carry

        lax.fori_loop(0, t // cr, fin, 0)

    col = lambda off: pl.BlockSpec((t, LANES), lambda b: (0, off + b))
    vec = pl.BlockSpec((1, LANES), lambda b: (0, b))
    mat = pl.BlockSpec((None, LANES, LANES), lambda b: (b, 0, 0))
    return pl.pallas_call(
        body, grid=(nb,),
        in_specs=[col(0), col(nb), pl.BlockSpec((CONV_WIDTH, LANES), lambda b: (0, b)), vec, mat, vec, mat, vec, vec],
        out_specs=[col(0), col(0)],
        out_shape=[jax.ShapeDtypeStruct((t, D_A), F32), jax.ShapeDtypeStruct((t, D_A), F32)],
        scratch_shapes=[pltpu.VMEM((t + 8, LANES), F32), pltpu.VMEM((t, LANES), F32), pltpu.VMEM((t, LANES), F32)],
        compiler_params=pltpu.CompilerParams(dimension_semantics=("arbitrary",)), name="rglru_fwd")(
            z, z, cw, cb, wa, ba, wx, bx, lam)


def _rg_bwd(z, h, dout, cw, cb, wa, ba, wx, bx, lam):
    t = z.shape[0]
    cr = _tile(t, 256)
    nb = D_A // LANES

    def body(ax_ref, ag_ref, h_ref, do_ref, cw_ref, cb_ref, wa_ref, ba_ref, wx_ref, bx_ref, lam_ref,
             dax_ref, dag_ref, dcw_ref, dcb_ref, dwa_ref, dba_ref, dwx_ref, dbx_ref, dlam_ref,
             axp, hp, xa_s, a_s, g_s, dxa_s, u_s):
        zero8 = jnp.zeros((8, LANES), F32)
        axp[pl.ds(0, 8), :] = zero8
        hp[pl.ds(0, 8), :] = zero8
        dxa_s[pl.ds(t, 8), :] = zero8
        u_s[pl.ds(t, 8), :] = zero8
        pltpu.sync_copy(ax_ref, axp.at[pl.ds(8, t)])
        pltpu.sync_copy(h_ref, hp.at[pl.ds(8, t)])
        w = [cw_ref[pl.ds(k, 1), :] for k in range(CONV_WIDTH)]
        for ref in (dcw_ref, dcb_ref, dwa_ref, dba_ref, dwx_ref, dbx_ref, dlam_ref):
            ref[...] = jnp.zeros(ref.shape, F32)

        def p1(c, carry):
            t0 = pl.multiple_of(c * cr, cr)
            taps = _conv_taps(axp[pl.ds(t0, cr + 8), :], cr)
            xa = cb_ref[...] + w[3] * taps[0] + w[2] * taps[1] + w[1] * taps[2] + w[0] * taps[3]
            a, _ = _rg_local(xa, wa_ref[...], ba_ref[...], wx_ref[...], bx_ref[...], lam_ref[...])
            xa_s[pl.ds(t0, cr), :] = xa
            a_s[pl.ds(t0, cr), :] = a
            ag = ag_ref[pl.ds(t0, cr), :]
            dov = do_ref[pl.ds(t0, cr), :]
            gel, pull = jax.vjp(_gelu, ag)
            g_s[pl.ds(t0, cr), :] = dov * gel
            u_s[pl.ds(t0, cr), :] = a * (dov * gel)
            dag_ref[pl.ds(t0, cr), :] = pull(dov * h_ref[pl.ds(t0, cr), :])[0]
            return carry

        lax.fori_loop(0, t // cr, p1, 0)
        _scan8(a_s, u_s, u_s, t, reverse=True)

        def p3(c, carry):
            t0 = pl.multiple_of(c * cr, cr)
            g = g_s[pl.ds(t0, cr), :] + pltpu.roll(u_s[pl.ds(t0, cr + 8), :], cr + 7, 0)[:cr, :]
            h_prev = pltpu.roll(hp[pl.ds(t0, cr + 8), :], 1, 0)[8:, :]
            _, pull = jax.vjp(_rg_local, xa_s[pl.ds(t0, cr), :], wa_ref[...], ba_ref[...], wx_ref[...], bx_ref[...],
                              lam_ref[...])
            dxa, dwa, dba, dwx, dbx, dlam = pull((g * h_prev, g))
            dxa_s[pl.ds(t0, cr), :] = dxa
            dwa_ref[...] += dwa
            dba_ref[...] += dba
            dwx_ref[...] += dwx
            dbx_ref[...] += dbx
            dlam_ref[...] += dlam
            return carry

        lax.fori_loop(0, t // cr, p3, 0)

        def p4(c, carry):
            t0 = pl.multiple_of(c * cr, cr)
            ext = dxa_s[pl.ds(t0, cr + 8), :]
            n = cr + 8
            ahead = [ext[:cr, :]] + [pltpu.roll(ext, n - s, 0)[:cr, :] for s in (1, 2, 3)]
            dax_ref[pl.ds(t0, cr), :] = w[3] * ahead[0] + w[2] * ahead[1] + w[1] * ahead[2] + w[0] * ahead[3]
            taps = _conv_taps(axp[pl.ds(t0, cr + 8), :], cr)
            dxa = ahead[0]
            for k in range(CONV_WIDTH):
                dcw_ref[pl.ds(k, 1), :] += jnp.sum(dxa * taps[CONV_WIDTH - 1 - k], axis=0, keepdims=True)
            dcb_ref[...] += jnp.sum(dxa, axis=0, keepdims=True)
            return carry

        lax.fori_loop(0, t // cr, p4, 0)

    col = lambda off: pl.BlockSpec((t, LANES), lambda b: (0, off + b))
    vec = pl.BlockSpec((1, LANES), lambda b: (0, b))
    mat = pl.BlockSpec((None, LANES, LANES), lambda b: (b, 0, 0))
    cws = pl.BlockSpec((CONV_WIDTH, LANES), lambda b: (0, b))
    sds = jax.ShapeDtypeStruct
    return pl.pallas_call(
        body, grid=(nb,),
        in_specs=[col(0), col(nb), col(0), col(0), cws, vec, mat, vec, mat, vec, vec],
        out_specs=[col(0), col(0), cws, vec, mat, vec, mat, vec, vec],
        out_shape=[sds((t, D_A), F32), sds((t, D_A), F32), sds((CONV_WIDTH, D_A), F32), sds((1, D_A), F32),
                   sds((nb, LANES, LANES), F32), sds((1, D_A), F32), sds((nb, LANES, LANES), F32), sds((1, D_A), F32),
                   sds((1, D_A), F32)],
        scratch_shapes=[pltpu.VMEM((t + 8, LANES), F32), pltpu.VMEM((t + 8, LANES), F32), pltpu.VMEM((t, LANES), F32),
                        pltpu.VMEM((t, LANES), F32), pltpu.VMEM((t, LANES), F32), pltpu.VMEM((t + 8, LANES), F32),
                        pltpu.VMEM((t + 8, LANES), F32)],
        compiler_params=pltpu.CompilerParams(dimension_semantics=("arbitrary",)), name="rglru_bwd")(
            z, z, h, dout, cw, cb, wa, ba, wx, bx, lam)


def _fgate_fwd(z, bf):
    t = z.shape[0]

    def body(zf_ref, bf_ref, c_ref):
        c_ref[...] = -_softplus(-(zf_ref[...] + bf_ref[...]))
        _scan8(None, c_ref, c_ref, t)

    return pl.pallas_call(
        body, grid=(1,), in_specs=[pl.BlockSpec((t, LANES), lambda i: (0, Z_F // LANES)), pl.BlockSpec((1, LANES), lambda i: (0, 0))],
        out_specs=pl.BlockSpec((t, LANES), lambda i: (0, 0)), out_shape=jax.ShapeDtypeStruct((t, LANES), F32),
        compiler_params=pltpu.CompilerParams(dimension_semantics=("arbitrary",)), name="fgate_fwd")(z, bf)


def _fgate_bwd(z, bf, dc):
    t = z.shape[0]

    def body(zf_ref, bf_ref, dc_ref, dz_ref, db_ref):
        _scan8(None, dc_ref, dz_ref, t, reverse=True)
        dz = dz_ref[...] * _sig(-(zf_ref[...] + bf_ref[...]))
        dz_ref[...] = dz
        db_ref[...] = jnp.sum(dz, axis=0, keepdims=True)

    return pl.pallas_call(
        body, grid=(1,),
        in_specs=[pl.BlockSpec((t, LANES), lambda i: (0, Z_F // LANES)), pl.BlockSpec((1, LANES), lambda i: (0, 0)),
                  pl.BlockSpec((t, LANES), lambda i: (0, 0))],
        out_specs=[pl.BlockSpec((t, LANES), lambda i: (0, 0)), pl.BlockSpec((1, LANES), lambda i: (0, 0))],
        out_shape=[jax.ShapeDtypeStruct((t, LANES), F32), jax.ShapeDtypeStruct((1, LANES), F32)],
        compiler_params=pltpu.CompilerParams(dimension_semantics=("arbitrary",)), name="fgate_bwd")(z, bf, dc)


def _cast_rows(src_ref, dst_ref, t, rows, fn):
    def cp(c, carry):
        r0 = pl.multiple_of(c * rows, rows)
        dst_ref[pl.ds(r0, rows), :] = fn(src_ref[pl.ds(r0, rows), :]).astype(dst_ref.dtype)
        return carry

    lax.fori_loop(0, t // rows, cp, 0)


def _attn_groups(t):
    tq = _tile(t, 256)
    nq = t // tq
    grp = 4 if nq % 4 == 0 else 1
    return tq, nq, grp


def _attn_fwd(z, crow, carry=None):
    t = z.shape[0]
    tq, nq, grp = _attn_groups(t)
    tk = grp * tq
    scale = HEAD_DIM ** -0.5

    def body(q_ref, k_ref, v_ref, cr_ref, o_ref, lse_ref, kb_s, vb_s):
        lane = lax.broadcasted_iota(jnp.int32, (1, LANES), 1)
        hmask = [(lane // HEAD_DIM) == hh for hh in range(2)]
        _cast_rows(k_ref, kb_s, t, tq, lambda v: v)
        _cast_rows(v_ref, vb_s, t, tq, lambda v: v)

        def qblock(g, r):
            q0 = pl.multiple_of((g * grp + r) * tq, tq)
            qv = q_ref[pl.ds(q0, tq), :] * scale
            qa = [jnp.where(hmask[hh], qv, 0.0).astype(BF16) for hh in range(2)]

            def update(st, k0, width, off):
                kb = kb_s[pl.ds(k0, width), :]
                vb = vb_s[pl.ds(k0, width), :]
                new = []
                for hh in range(2):
                    m, l, acc = st[hh]
                    s = _dot(qa[hh], kb, NT) - cr_ref[hh, :, pl.ds(k0, width)]
                    if off is not None:
                        keep = (lax.broadcasted_iota(jnp.int32, (tq, width), 0) + off
                                >= lax.broadcasted_iota(jnp.int32, (tq, width), 1))
                        s = jnp.where(keep, s, NEG)
                    m_new = jnp.maximum(m, jnp.max(s, axis=-1, keepdims=True))
                    p = jnp.exp(s - m_new)
                    corr = jnp.exp(m - m_new)
                    new.append((m_new, corr * l + jnp.sum(p, axis=-1, keepdims=True), corr * acc + _dot(p, vb, NN)))
                return tuple(new)

            one = (jnp.full((tq, 1), NEG, F32), jnp.zeros((tq, 1), F32), jnp.zeros((tq, LANES), F32))
            st = lax.fori_loop(0, g, lambda j, st: update(st, pl.multiple_of(j * tk, tk), tk, None), (one, one))
            st = update(st, pl.multiple_of(g * tk, tk), (r + 1) * tq, r * tq)
            o_ref[pl.ds(q0, tq), :] = jnp.where(hmask[0], st[0][2] / st[0][1], st[1][2] / st[1][1])
            for hh in range(2):
                lse_ref[hh, pl.ds(q0, tq), :] = st[hh][0] + jnp.log(st[hh][1])

        def group(g, carry):
            for r in range(grp):
                qblock(g, r)
            return carry

        lax.fori_loop(0, nq // grp, group, 0)

    base = 2 * D_A // LANES
    nh = D_B // LANES
    col = lambda off: pl.BlockSpec((t, LANES), lambda p: (0, off + p))
    return _call(
        body, (nh,), [col(base), col(base + nh), col(base + 2 * nh), pl.BlockSpec((2, 1, t), lambda p: (p, 0, 0))],
        [col(0), pl.BlockSpec((2, t, 1), lambda p: (p, 0, 0))],
        [jax.ShapeDtypeStruct((t, D_B), F32), jax.ShapeDtypeStruct((N_HEADS, t, 1), F32)],
        [pltpu.VMEM((t, LANES), BF16), pltpu.VMEM((t, LANES), BF16)], ("parallel",), "attn_fwd", [z, z, z, crow], carry)


def _attn_bwd(z, crow, lse, do, carry=None):
    t = z.shape[0]
    tq, nq, grp = _attn_groups(t)
    tw = grp * tq
    scale = HEAD_DIM ** -0.5

    def body(q_ref, k_ref, v_ref, cr_ref, lse_ref, do_ref, dq_ref, dk_ref, dv_ref, dc_ref, qa_s, da_s, kb_s, vb_s, dl_s):
        lane = lax.broadcasted_iota(jnp.int32, (1, LANES), 1)
        hmask = [(lane // HEAD_DIM) == hh for hh in range(2)]
        _cast_rows(k_ref, kb_s, t, tq, lambda v: v)
        _cast_rows(v_ref, vb_s, t, tq, lambda v: v)
        for hh in range(2):
            _cast_rows(q_ref, qa_s.at[hh], t, tq, lambda v, hh=hh: jnp.where(hmask[hh], v * scale, 0.0))
            _cast_rows(do_ref, da_s.at[hh], t, tq, lambda v, hh=hh: jnp.where(hmask[hh], v, 0.0))
        _cast_rows(q_ref, dq_ref, t, tq, lambda v: jnp.zeros_like(v))

        def probs(hh, q0, nq_rows, k0, nk_rows, off):
            s = _dot(qa_s[hh, pl.ds(q0, nq_rows), :], kb_s[pl.ds(k0, nk_rows), :], NT) - cr_ref[hh, :, pl.ds(k0, nk_rows)]
            p = jnp.exp(s - lse_ref[hh, pl.ds(q0, nq_rows), :])
            if off is not None:
                keep = (lax.broadcasted_iota(jnp.int32, (nq_rows, nk_rows), 0) + off
                        >= lax.broadcasted_iota(jnp.int32, (nq_rows, nk_rows), 1))
                p = jnp.where(keep, p, 0.0)
            return p, _dot(da_s[hh, pl.ds(q0, nq_rows), :], vb_s[pl.ds(k0, nk_rows), :], NT)

        def delta(g, r):
            q0 = pl.multiple_of((g * grp + r) * tq, tq)

            def add(k0, width, off, acc):
                res = []
                for hh in range(2):
                    p, dp = probs(hh, q0, tq, k0, width, off)
                    res.append(acc[hh] + jnp.sum(p * dp, axis=-1, keepdims=True))
                return tuple(res)

            zcol = jnp.zeros((tq, 1), F32)
            acc = lax.fori_loop(0, g, lambda j, acc: add(pl.multiple_of(j * tw, tw), tw, None, acc), (zcol, zcol))
            acc = add(pl.multiple_of(g * tw, tw), (r + 1) * tq, r * tq, acc)
            for hh in range(2):
                dl_s[hh, pl.ds(q0, tq), :] = acc[hh]

        def delta_group(g, carry):
            for r in range(grp):
                delta(g, r)
            return carry

        lax.fori_loop(0, nq // grp, delta_group, 0)

        def kblock(g, r):
            k0 = pl.multiple_of((g * grp + r) * tq, tq)
            kb = kb_s[pl.ds(k0, tq), :]

            def upd(q0, height, off, st):
                dk, dv, dc = st[0], st[1], [st[2], st[3]]
                dqs = []
                for hh in range(2):
                    p, dp = probs(hh, q0, height, k0, tq, off)
                    ds = p * (dp - dl_s[hh, pl.ds(q0, height), :])
                    dv = dv + _dot(p, da_s[hh, pl.ds(q0, height), :], TN)
                    dk = dk + _dot(ds, qa_s[hh, pl.ds(q0, height), :], TN)
                    dqs.append(_dot(ds, kb, NN))
                    dc[hh] = dc[hh] - jnp.sum(ds, axis=0, keepdims=True)
                dq_ref[pl.ds(q0, height), :] += jnp.where(hmask[0], dqs[0], dqs[1]) * scale
                return dk, dv, dc[0], dc[1]

            zero = jnp.zeros((tq, LANES), F32)
            zrow = jnp.zeros((1, tq), F32)
            st = upd(k0, (grp - r) * tq, 0, (zero, zero, zrow, zrow))
            st = lax.fori_loop(g + 1, nq // grp, lambda i, st: upd(pl.multiple_of(i * tw, tw), tw, None, st), st)
            dk_ref[pl.ds(k0, tq), :] = st[0]
            dv_ref[pl.ds(k0, tq), :] = st[1]
            for hh in range(2):
                dc_ref[hh, :, pl.ds(k0, tq)] = st[2 + hh]

        def kgroup(g, carry):
            for r in range(grp):
                kblock(g, r)
            return carry

        lax.fori_loop(0, nq // grp, kgroup, 0)

    base = 2 * D_A // LANES
    nh = D_B // LANES
    col = lambda off: pl.BlockSpec((t, LANES), lambda p: (0, off + p))
    ccs = pl.BlockSpec((2, t, 1), lambda p: (p, 0, 0))
    crs = pl.BlockSpec((2, 1, t), lambda p: (p, 0, 0))
    return _call(
        body, (nh,), [col(base), col(base + nh), col(base + 2 * nh), crs, ccs, col(0)], [col(0), col(0), col(0), crs],
        [jax.ShapeDtypeStruct((t, D_B), F32)] * 3 + [jax.ShapeDtypeStruct((N_HEADS, 1, t), F32)],
        [pltpu.VMEM((2, t, LANES), BF16), pltpu.VMEM((2, t, LANES), BF16), pltpu.VMEM((t, LANES), BF16),
         pltpu.VMEM((t, LANES), BF16), pltpu.VMEM((2, t, 1), F32)], ("parallel",), "attn_bwd", [z, z, z, crow, lse, do], carry)


def _s5_disc(a_re, a_im, log_dt, b_re, b_im):
    dt = jnp.exp(log_dt)
    mag = jnp.exp(a_re * dt)
    ar = mag * jnp.cos(a_im * dt)
    ai = mag * jnp.sin(a_im * dt)
    den = a_re * a_re + a_im * a_im
    kr = ((ar - 1.0) * a_re + ai * a_im) / den
    ki = (ai * a_re - (ar - 1.0) * a_im) / den
    kr3, ki3 = kr[:, None, :], ki[:, None, :]
    return ar, ai, kr3 * b_re - ki3 * b_im, kr3 * b_im + ki3 * b_re


def _s5_prep(a_re, a_im, log_dt, b_re, b_im):
    g, p = a_re.shape
    gc = b_re.shape[1]

    def body(*refs):
        res = _s5_disc(*[r[...] for r in refs[:5]])
        for o_ref, v in zip(refs[5:], res):
            o_ref[...] = v

    sds = jax.ShapeDtypeStruct
    return pl.pallas_call(body, out_shape=[sds((g, p), F32), sds((g, p), F32), sds((g, gc, p), F32), sds((g, gc, p), F32)],
                          name="s5_prep")(a_re, a_im, log_dt, b_re, b_im)


def _s5_prep_bwd(a_re, a_im, log_dt, b_re, b_im, d_ar, d_ai, d_br, d_bi):
    ins = (a_re, a_im, log_dt, b_re, b_im)

    def body(*refs):
        vals = [r[...] for r in refs[:5]]
        cts = tuple(r[...] for r in refs[5:9])
        _, pull = jax.vjp(_s5_disc, *vals)
        for o_ref, v in zip(refs[9:], pull(cts)):
            o_ref[...] = v

    return pl.pallas_call(body, out_shape=[jax.ShapeDtypeStruct(a.shape, F32) for a in ins], name="s5_prep_bwd")(
        *ins, d_ar, d_ai, d_br, d_bi)


def _s5_scan_rows(t, ar, ai, hr_s, hi_s, off, reverse):
    n = ar.shape[1]
    if reverse:
        ai = -ai
    sub = lax.broadcasted_iota(jnp.int32, (8, n), 0)
    cmul = lambda xr, xi, yr, yi: (xr * yr - xi * yi, xr * yi + xi * yr)
    pw = [(ar, ai)]
    for _ in range(7):
        pw.append(cmul(*pw[-1], ar, ai))
    pr = jnp.zeros((8, n), F32)
    pi = jnp.zeros((8, n), F32)
    for r in range(8):
        k = 7 - r if reverse else r
        pr = jnp.where(sub == r, pw[k][0], pr)
        pi = jnp.where(sub == r, pw[k][1], pi)

    def step(g, carry):
        cr, ci = carry
        r0 = pl.multiple_of(off + (t // 8 - 1 - g if reverse else g) * 8, 8)
        br = hr_s[pl.ds(r0, 8), :]
        bi = hi_s[pl.ds(r0, 8), :]
        for s in (1, 2, 4):
            ok = (sub < 8 - s) if reverse else (sub >= s)
            shift = 8 - s if reverse else s
            sr = jnp.where(ok, pltpu.roll(br, shift, 0), 0.0)
            si = jnp.where(ok, pltpu.roll(bi, shift, 0), 0.0)
            mr, mi = cmul(pw[s - 1][0], pw[s - 1][1], sr, si)
            br, bi = br + mr, bi + mi
        mr, mi = cmul(pr, pi, cr, ci)
        br, bi = br + mr, bi + mi
        hr_s[pl.ds(r0, 8), :] = br
        hi_s[pl.ds(r0, 8), :] = bi
        edge = sub == (0 if reverse else 7)
        return (jnp.sum(jnp.where(edge, br, 0.0), axis=0, keepdims=True),
                jnp.sum(jnp.where(edge, bi, 0.0), axis=0, keepdims=True))

    zero = jnp.zeros((1, n), F32)
    lax.fori_loop(0, t // 8, step, (zero, zero))


def _s5_fwd(z, bd_re, bd_im, ab_re, ab_im, cd_re, cd_im, dvec, carry=None):
    t = z.shape[0]
    cr = _tile(t, 256)
    ns = N_STATE // 2

    def body(u_ref, br_ref, bi_ref, ar_ref, ai_ref, cre_ref, cim_ref, d_ref, y_ref, hr_s, hi_s):
        def p1(c, carry):
            t0 = pl.multiple_of(c * cr, cr)
            u = u_ref[pl.ds(t0, cr), :]
            hr_s[pl.ds(t0, cr), :] = _dot(u, br_ref[...], NN)
            hi_s[pl.ds(t0, cr), :] = _dot(u, bi_ref[...], NN)
            return carry

        lax.fori_loop(0, t // cr, p1, 0)
        _s5_scan_rows(t, ar_ref[...], ai_ref[...], hr_s, hi_s, 0, False)

        def p3(c, carry):
            t0 = pl.multiple_of(c * cr, cr)
            y_ref[pl.ds(t0, cr), :] = (_dot(hr_s[pl.ds(t0, cr), :], cre_ref[...], NN)
                                       - _dot(hi_s[pl.ds(t0, cr), :], cim_ref[...], NN)
                                       + d_ref[...] * u_ref[pl.ds(t0, cr), :])
            return carry

        lax.fori_loop(0, t // cr, p3, 0)

    blk = lambda r, c: pl.BlockSpec((None, r, c), lambda b: (b, 0, 0))
    return _call(
        body, (2,),
        [pl.BlockSpec((t, LANES), lambda b: (0, Z_U // LANES + b)), blk(LANES, ns), blk(LANES, ns), blk(1, ns),
         blk(1, ns), blk(ns, LANES), blk(ns, LANES), pl.BlockSpec((1, LANES), lambda b: (0, b))],
        [pl.BlockSpec((t, LANES), lambda b: (0, b))], [jax.ShapeDtypeStruct((t, D_C), F32)],
        [pltpu.VMEM((t, ns), F32), pltpu.VMEM((t, ns), F32)], ("arbitrary",), "s5_fwd",
        [z, bd_re, bd_im, ab_re, ab_im, cd_re, cd_im, dvec], carry)


def _s5_bwd(z, dy, bd_re, bd_im, ab_re, ab_im, cd_re, cd_im, dvec):
    t = z.shape[0]
    cr = _tile(t, 256)
    ns = N_STATE // 2

    def body(u_ref, dy_ref, br_ref, bi_ref, ar_ref, ai_ref, cre_ref, cim_ref, d_ref,
             du_ref, dbr_ref, dbi_ref, dar_ref, dai_ref, dcre_ref, dcim_ref, dd_ref, hr_s, hi_s, gr_s, gi_s):
        zero8 = jnp.zeros((8, ns), F32)
        hr_s[pl.ds(0, 8), :] = zero8
        hi_s[pl.ds(0, 8), :] = zero8
        for ref in (dbr_ref, dbi_ref, dar_ref, dai_ref, dcre_ref, dcim_ref, dd_ref):
            ref[...] = jnp.zeros(ref.shape, F32)

        def p1(c, carry):
            t0 = pl.multiple_of(c * cr, cr)
            u = u_ref[pl.ds(t0, cr), :]
            hr_s[pl.ds(t0 + 8, cr), :] = _dot(u, br_ref[...], NN)
            hi_s[pl.ds(t0 + 8, cr), :] = _dot(u, bi_ref[...], NN)
            return carry

        lax.fori_loop(0, t // cr, p1, 0)
        _s5_scan_rows(t, ar_ref[...], ai_ref[...], hr_s, hi_s, 8, False)

        def p3(c, carry):
            t0 = pl.multiple_of(c * cr, cr)
            dyv = dy_ref[pl.ds(t0, cr), :]
            u = u_ref[pl.ds(t0, cr), :]
            gr_s[pl.ds(t0, cr), :] = _dot(dyv, cre_ref[...], NT)
            gi_s[pl.ds(t0, cr), :] = -_dot(dyv, cim_ref[...], NT)
            dcre_ref[...] += _dot(hr_s[pl.ds(t0 + 8, cr), :], dyv, TN)
            dcim_ref[...] -= _dot(hi_s[pl.ds(t0 + 8, cr), :], dyv, TN)
            dd_ref[...] += jnp.sum(dyv * u, axis=0, keepdims=True)
            du_ref[pl.ds(t0, cr), :] = dyv * d_ref[...]
            return carry

        lax.fori_loop(0, t // cr, p3, 0)
        _s5_scan_rows(t, ar_ref[...], ai_ref[...], gr_s, gi_s, 0, True)

        def p5(c, carry):
            t0 = pl.multiple_of(c * cr, cr)
            u = u_ref[pl.ds(t0, cr), :]
            gr = gr_s[pl.ds(t0, cr), :]
            gi = gi_s[pl.ds(t0, cr), :]
            dbr_ref[...] += _dot(u, gr, TN)
            dbi_ref[...] += _dot(u, gi, TN)
            du_ref[pl.ds(t0, cr), :] += _dot(gr, br_ref[...], NT) + _dot(gi, bi_ref[...], NT)
            hpr = pltpu.roll(hr_s[pl.ds(t0, cr + 8), :], 1, 0)[8:, :]
            hpi = pltpu.roll(hi_s[pl.ds(t0, cr + 8), :], 1, 0)[8:, :]
            dar_ref[...] += jnp.sum(gr * hpr + gi * hpi, axis=0, keepdims=True)
            dai_ref[...] += jnp.sum(gi * hpr - gr * hpi, axis=0, keepdims=True)
            return carry

        lax.fori_loop(0, t // cr, p5, 0)

    blk = lambda r, c: pl.BlockSpec((None, r, c), lambda b: (b, 0, 0))
    ucol = pl.BlockSpec((t, LANES), lambda b: (0, Z_U // LANES + b))
    ycol = pl.BlockSpec((t, LANES), lambda b: (0, b))
    dsp = pl.BlockSpec((1, LANES), lambda b: (0, b))
    sds = jax.ShapeDtypeStruct
    return pl.pallas_call(
        body, grid=(2,),
        in_specs=[ucol, ycol, blk(LANES, ns), blk(LANES, ns), blk(1, ns), blk(1, ns), blk(ns, LANES), blk(ns, LANES), dsp],
        out_specs=[ycol, blk(LANES, ns), blk(LANES, ns), blk(1, ns), blk(1, ns), blk(ns, LANES), blk(ns, LANES), dsp],
        out_shape=[sds((t, D_C), F32), sds((2, LANES, ns), F32), sds((2, LANES, ns), F32), sds((2, 1, ns), F32),
                   sds((2, 1, ns), F32), sds((2, ns, LANES), F32), sds((2, ns, LANES), F32), sds((1, D_C), F32)],
        scratch_shapes=[pltpu.VMEM((t + 8, ns), F32), pltpu.VMEM((t + 8, ns), F32), pltpu.VMEM((t, ns), F32),
                        pltpu.VMEM((t, ns), F32)],
        compiler_params=pltpu.CompilerParams(dimension_semantics=("arbitrary",)), name="s5_bwd")(
            z, dy, bd_re, bd_im, ab_re, ab_im, cd_re, cd_im, dvec)


def _mix_out(out_a, out_b, yc, x1, ga, gb, gc, wglu, wout, ln_g, ln_b):
    yg = _gelu(yc)
    out_c = yg * _sig(_bdot(yg, wglu))
    o = jnp.concatenate([_rms(out_a, ga), _rms(out_b, gb), _rms(out_c, gc)], axis=-1)
    return (_ln(ALPHA * x1 + _bdot(o, wout), ln_g, ln_b),)


def _ln_only(pre, g, b):
    return (_ln(pre, g, b),)


def _loss_head(y, target, tm):
    t, d = y.shape

    def body(y_ref, t_ref, dy_ref, l_ref):
        i = pl.program_id(0)
        e = y_ref[...] - t_ref[...]
        dy_ref[...] = e * (1.0 / d)
        part = 0.5 * jnp.sum(jnp.sum(e * e, axis=-1, keepdims=True) * (1.0 / d), axis=0, keepdims=True)
        row = jnp.where(lax.broadcasted_iota(jnp.int32, (1, LANES), 1) == 0, part, 0.0)

        @pl.when(i == 0)
        def _():
            l_ref[...] = row

        @pl.when(i > 0)
        def _():
            l_ref[...] += row

    spec = pl.BlockSpec((tm, d), lambda i: (i, 0))
    return pl.pallas_call(
        body, grid=(t // tm,), in_specs=[spec, spec], out_specs=[spec, pl.BlockSpec((1, LANES), lambda i: (0, 0))],
        out_shape=[jax.ShapeDtypeStruct((t, d), F32), jax.ShapeDtypeStruct((1, LANES), F32)],
        compiler_params=pltpu.CompilerParams(dimension_semantics=("arbitrary",)), name="loss_head")(y, target)


def _adamw(w, g, m, v, name):
    r, c = w.shape
    tr = r
    for cand in (512, 256, 352, 128):
        if r % cand == 0:
            tr = cand
            break

    def body(w_ref, g_ref, m_ref, v_ref, d_ref, nm_ref, nv_ref):
        gv = g_ref[...]
        mn = ADAM_B1 * m_ref[...] + (1.0 - ADAM_B1) * gv
        vn = ADAM_B2 * v_ref[...] + (1.0 - ADAM_B2) * (gv * gv)
        m_hat = mn / (1.0 - ADAM_B1 ** ADAM_STEP)
        v_hat = vn / (1.0 - ADAM_B2 ** ADAM_STEP)
        d_ref[...] = -ADAM_LR * (m_hat / (jnp.sqrt(v_hat) + ADAM_EPS) + ADAM_WD * w_ref[...])
        nm_ref[...] = mn
        nv_ref[...] = vn

    spec = pl.BlockSpec((tr, c), lambda i: (i, 0))
    return pl.pallas_call(
        body, grid=(r // tr,), in_specs=[spec] * 4, out_specs=[spec] * 3,
        out_shape=[jax.ShapeDtypeStruct((r, c), F32)] * 3,
        compiler_params=pltpu.CompilerParams(dimension_semantics=("parallel",)), name=name)(w, g, m, v)


def _row_tile(r):
    for cand in (512, 448, 352, 256, 128):
        if r % cand == 0:
            return cand
    return r


def _pair_add(a, b, idx, name, out_dtype):
    _, r, w = a.shape
    tr = _row_tile(r)

    def body(i_ref, a_ref, b_ref, o_ref):
        o_ref[...] = (a_ref[...].astype(F32) + b_ref[...].astype(F32)).astype(o_ref.dtype)

    grid_spec = pltpu.PrefetchScalarGridSpec(
        num_scalar_prefetch=1, grid=(4, r // tr),
        in_specs=[pl.BlockSpec((None, tr, w), lambda q, i, s: (2 * q + s[0], i, 0)),
                  pl.BlockSpec((None, tr, w), lambda q, i, s: (q, i, 0))],
        out_specs=pl.BlockSpec((None, tr, w), lambda q, i, s: (q, i, 0)))
    return pl.pallas_call(body, grid_spec=grid_spec, out_shape=jax.ShapeDtypeStruct((4, r, w), out_dtype),
                          compiler_params=pltpu.CompilerParams(dimension_semantics=("parallel", "parallel")), name=name)(
                              idx, a, b)


def _quad_add(p, rb, idx, name):
    _, r, w = p.shape
    tr = _row_tile(r)

    def body(i_ref, p_ref, r0, r1, r2, o_ref):
        o_ref[...] = ((p_ref[...].astype(F32) + r0[...].astype(F32)) + r1[...].astype(F32)) + r2[...].astype(F32)

    grid_spec = pltpu.PrefetchScalarGridSpec(
        num_scalar_prefetch=1, grid=(r // tr,),
        in_specs=[pl.BlockSpec((None, tr, w), lambda i, s: (s[0], i, 0))]
        + [pl.BlockSpec((None, tr, w), functools.partial(lambda i, s, k: (k, i, 0), k=k)) for k in range(3)],
        out_specs=pl.BlockSpec((tr, w), lambda i, s: (i, 0)))
    return pl.pallas_call(body, grid_spec=grid_spec, out_shape=jax.ShapeDtypeStruct((r, w), F32),
                          compiler_params=pltpu.CompilerParams(dimension_semantics=("parallel",)), name=name)(
                              idx, p, rb, rb, rb)


def _all_reduce_small(buf):
    r, w = buf.shape

    def plan(x, y, c, ins, outs):
        dst = outs[0].at[4 * x + 2 * y + c]
        res = [(ins[0], dst, None)]
        for rel in range(1, 8):
            res.append((ins[0], dst, (x ^ (rel >> 2), y ^ ((rel >> 1) & 1), c ^ (rel & 1))))
        return res

    (allb,) = _run(_Carry([buf], [jax.ShapeDtypeStruct((8, r, w), buf.dtype)], {}, plan, 8), "ar_small")

    def body(a_ref, o_ref):
        s = a_ref[0]
        for k in range(1, 8):
            s = s + a_ref[k]
        o_ref[...] = s

    return pl.pallas_call(body, out_shape=jax.ShapeDtypeStruct((r, w), F32), name="ar_small_sum")(allb)


def _pad_rows(a, rows):
    return jnp.pad(a, ((0, rows - a.shape[0]), (0, 0)))


def _pack_small(arrs):
    rows, tail = [], []
    for a in arrs:
        if not tail and a.size % LANES == 0:
            rows.append(a.reshape(-1, LANES))
        else:
            tail.append(a.reshape(-1))
    n_rows = sum(r.shape[0] for r in rows)
    n_tail = sum(int(v.size) for v in tail)
    tail_rows = -(-n_tail // LANES)
    total_rows = n_rows + tail_rows + (-(n_rows + tail_rows)) % 8
    if tail:
        tail.append(jnp.zeros((tail_rows * LANES - n_tail,), F32))
        rows.append(jnp.concatenate(tail).reshape(tail_rows, LANES))
    if total_rows > n_rows + tail_rows:
        rows.append(jnp.zeros((total_rows - n_rows - tail_rows, LANES), F32))
    return jnp.concatenate(rows, axis=0)


def _unpack_small(buf, shapes):
    out, off = [], 0
    flat = buf.reshape(-1)
    for s in shapes:
        n = int(np.prod(s))
        out.append(flat[off:off + n].reshape(s))
        off += n
    return out


def _block_diag(blocks, nb):
    m, r, c = blocks.shape
    n = m // nb
    eye = jnp.eye(n, dtype=blocks.dtype)
    return (blocks.reshape(nb, n, r, 1, c) * eye[None, :, None, :, None]).reshape(nb, n * r, n * c)


def _diag_blocks(dense, n):
    nb = dense.shape[0]
    r, c = dense.shape[1] // n, dense.shape[2] // n
    eye = jnp.eye(n, dtype=dense.dtype)
    return jnp.sum(dense.reshape(nb, n, r, n, c) * eye[None, :, None, :, None], axis=3).reshape(nb * n, r, c)


def kernel(x, ffn1_w_gate, ffn1_w_up, ffn1_w_down, ln1_g, ln1_b, w_in, conv_w, conv_b, rg_w_a, rg_b_a, rg_w_x, rg_b_x, rg_lambda, fox_b_f, s5_a_re, s5_a_im, s5_log_dt, s5_b_re, s5_b_im, s5_c_re, s5_c_im, s5_d, s5_w_glu, mix_norm_g, w_out, ln2_g, ln2_b, ffn2_w_gate, ffn2_w_up, ffn2_w_down, ln3_g, ln3_b, loss_target, m_ffn1_w_gate, m_ffn1_w_up, m_ffn1_w_down, m_ln1_g, m_ln1_b, m_w_in, m_conv_w, m_conv_b, m_rg_w_a, m_rg_b_a, m_rg_w_x, m_rg_b_x, m_rg_lambda, m_fox_b_f, m_s5_a_re, m_s5_a_im, m_s5_log_dt, m_s5_b_re, m_s5_b_im, m_s5_c_re, m_s5_c_im, m_s5_d, m_s5_w_glu, m_mix_norm_g, m_w_out, m_ln2_g, m_ln2_b, m_ffn2_w_gate, m_ffn2_w_up, m_ffn2_w_down, m_ln3_g, m_ln3_b, v_ffn1_w_gate, v_ffn1_w_up, v_ffn1_w_down, v_ln1_g, v_ln1_b, v_w_in, v_conv_w, v_conv_b, v_rg_w_a, v_rg_b_a, v_rg_w_x, v_rg_b_x, v_rg_lambda, v_fox_b_f, v_s5_a_re, v_s5_a_im, v_s5_log_dt, v_s5_b_re, v_s5_b_im, v_s5_c_re, v_s5_c_im, v_s5_d, v_s5_w_glu, v_mix_norm_g, v_w_out, v_ln2_g, v_ln2_b, v_ffn2_w_gate, v_ffn2_w_up, v_ffn2_w_down, v_ln3_g, v_ln3_b):
    a = dict(locals())
    w = {n: a[n] for n in WEIGHTS}
    t, d = x.shape[1], x.shape[2]
    f = ffn1_w_down.shape[1] * 8
    fs, ds = f // 8, d // 8
    mx, my, mc = lax.axis_index("x"), lax.axis_index("y"), lax.axis_index("c")
    me = 4 * mx + 2 * my + mc
    tm = _tile(t, 512)
    tf = f // 2
    win_rows = ds * Z_W // d

    FFN1, MIXW, FFN2 = ['g1', 'u1', 'd1'], ['win', 'wout', 'glu', 'conv'], ['g2', 'u2', 'd2']
    glu_rows = D_C * D_C // (8 * d)

    def shard_segs(l):
        wi = w['w_in'][l]
        win_p = jnp.concatenate([wi[:, :Z_F + N_HEADS], jnp.zeros((ds, Z_U - Z_F - N_HEADS), F32), wi[:, Z_F + N_HEADS:]], axis=1)
        conv_bits = lax.bitcast_convert_type(w['conv_w'][l], BF16).reshape(1, -1)
        segs = dict(g1=w['ffn1_w_gate'][l].T, u1=w['ffn1_w_up'][l].T, d1=w['ffn1_w_down'][l],
                    g2=w['ffn2_w_gate'][l].T, u2=w['ffn2_w_up'][l].T, d2=w['ffn2_w_down'][l],
                    win=win_p.reshape(win_rows, d), wout=w['w_out'][l], glu=_pad_rows(w['s5_w_glu'][l].reshape(-1, d), 16))
        segs = {k: v.astype(BF16) for k, v in segs.items()}
        segs['conv'] = _pad_rows(jnp.pad(conv_bits, ((0, 0), (0, d - conv_bits.shape[1]))), 16)
        return segs

    shards = [shard_segs(l) for l in range(DEPTH)]
    wts = {}

    def cat(keys):
        return jnp.concatenate([shards[l][k] for l, k in keys], axis=0)

    def split(g, keys):
        off = 0
        for l, k in keys:
            r = shards[l][k].shape[0]
            wts[(l, k)] = g[:, off:off + r]
            off += r

    grp_a = [(0, k) for k in FFN1]
    grp_b = [(0, k) for k in MIXW]
    grp_c = [(0, k) for k in FFN2] + [(1, k) for k in FFN1 + MIXW + FFN2]
    (g_a,) = _run(_ag_chips(cat(grp_a)), "ag_chips")
    (g_a,) = _run(_ag_sibling(g_a), "ag_sibling")
    split(g_a, grp_a)

    xs = x[0]
    saved = []
    cur = xs
    for l in range(DEPTH):
        row = lambda n: w[n][l].reshape(1, -1)
        ffn = lambda keys: tuple(wts[(l, k)].reshape(f, d) for k in keys)
        wa = _block_diag(w['rg_w_a'][l], 3)
        wx = _block_diag(w['rg_w_x'][l], 3)
        bf = jnp.pad(row('fox_b_f'), ((0, 0), (0, LANES - N_HEADS)))
        s5p = (w['s5_a_re'][l], w['s5_a_im'][l], w['s5_log_dt'][l].reshape(-1, 1),
               w['s5_b_re'][l].transpose(0, 2, 1), w['s5_b_im'][l].transpose(0, 2, 1))
        ab_re, ab_im, bb_re, bb_im = _s5_prep(*s5p)
        bd_re, bd_im = _block_diag(bb_re, 2), _block_diag(bb_im, 2)
        cd_re = _block_diag(w['s5_c_re'][l].transpose(0, 2, 1), 2)
        cd_im = _block_diag(w['s5_c_im'][l].transpose(0, 2, 1), 2)
        abr, abi = ab_re.reshape(2, 1, N_STATE // 2), ab_im.reshape(2, 1, N_STATE // 2)
        gm = row('mix_norm_g')
        ga, gb, gc = gm[:, :D_A], gm[:, D_A:D_A + D_B], gm[:, D_A + D_B:]

        x0 = cur
        ffn1 = ffn(FFN1)
        (x1, pre1, gs1, us1), cres = _ffn_fwd(x0, *ffn1, row('ln1_g'), row('ln1_b'), tm, tf,
                                              carry=_ag_chips(cat(grp_b)) if l == 0 else None)
        if l == 0:
            (g_b,) = _run(_ag_sibling(cres[0]), "ag_sibling")
            split(g_b, grp_b)
        win = wts[(l, 'win')].reshape(d, Z_W)
        wout = wts[(l, 'wout')].reshape(d, d).astype(F32)
        wglu = wts[(l, 'glu')][:, :glu_rows].reshape(D_C, D_C).astype(F32)
        conv_full = lax.bitcast_convert_type(
            wts[(l, 'conv')][:, 0, :2 * CONV_WIDTH * D_A // 8].reshape(8, CONV_WIDTH, D_A // 8, 2), F32)
        conv_full = conv_full.transpose(1, 0, 2).reshape(CONV_WIDTH, D_A)
        z = _mm(x1, win, 'nn', F32, tm, Z_W, d, "mix_in")
        out_a, h_a = _rg_fwd(z, conv_full, row('conv_b'), wa, row('rg_b_a'), wx, row('rg_b_x'), row('rg_lambda'))
        cs = _fgate_fwd(z, bf)
        crow = cs[:, :N_HEADS].T.reshape(N_HEADS, 1, t)
        (out_b, lse), cres = _attn_fwd(z, crow, carry=_ag_chips(cat(grp_c)) if l == 0 else None)
        (yc,), cres = _s5_fwd(z, bd_re, bd_im, abr, abi, cd_re, cd_im, row('s5_d'),
                              carry=_ag_sibling(cres[0]) if l == 0 else None)
        if l == 0:
            split(cres[0], grp_c)
        mix_params = [ga, gb, gc, wglu, wout, row('ln2_g'), row('ln2_b')]
        (x2,) = _rowwise(_mix_out, [out_a, out_b, yc, x1], mix_params, [(d, F32)], _tile(t, 256), "mix_out")
        ffn2 = ffn(FFN2)
        (x3, pre3, gs2, us2), _ = _ffn_fwd(x2, *ffn2, row('ln3_g'), row('ln3_b'), tm, tf)
        saved.append(dict(x0=x0, x1=x1, pre1=pre1, gs1=gs1, us1=us1, z=z, out_a=out_a, h_a=h_a, crow=crow,
                          out_b=out_b, lse=lse, yc=yc, x2=x2, pre3=pre3, gs2=gs2, us2=us2, mix_params=mix_params,
                          ffn1=ffn1, ffn2=ffn2, win=win, conv_full=conv_full, wa=wa, wx=wx, bf=bf, s5p=s5p,
                          s5m=(bd_re, bd_im, abr, abi, cd_re, cd_im)))
        cur = x3

    dy, loss_row = _loss_head(cur, loss_target[0], tm)
    loss = lax.psum(loss_row[0, 0], ("x", "y", "c"))

    assert DEPTH == 2
    small_grads = {}
    c_idx = jnp.reshape(mc, (1,)).astype(jnp.int32)
    chip_idx = jnp.reshape(2 * mx + my, (1,)).astype(jnp.int32)

    def blocks(arrs):
        return jnp.concatenate([v.astype(BF16).reshape(8, -1, d) for v in arrs], axis=1)

    def mixer_blocks(dwin, dwout, dwglu):
        glu = jnp.pad(dwglu.astype(BF16).reshape(8, glu_rows, d), ((0, 0), (0, 32 - glu_rows), (0, 0)))
        return jnp.concatenate([dwin.reshape(8, win_rows, d), dwout.astype(BF16).reshape(8, ds, d), glu], axis=1)

    def pair(full, ra):
        return _pair_add(full, ra, c_idx, "rs_add_sibling", BF16)

    for l in reversed(range(DEPTH)):
        s = saved[l]
        row = lambda n: w[n][l].reshape(1, -1)
        wg_tiles = dict(tm=tf, tn=d, tk=_tile(t, 1024))
        first = l == 0

        def ffn_back(dyv, pre, gs, us, wts3, xin, ln_g, ln_b, carry):
            (dpre,), (dlg, dlb), _ = _rowwise_vjp(_ln_only, [pre], [ln_g, ln_b], [dyv], _tile(t, 256), "ln_bwd")
            (dx, dg, du, hh), cres = _ffn_bwd(dpre, gs, us, *wts3, tm, tf, carry=carry)
            dwg = _mm(dg, xin, 'tn', BF16, name="ffn_dw_gate", **wg_tiles)
            dwu = _mm(du, xin, 'tn', BF16, name="ffn_dw_up", **wg_tiles)
            dwd = _mm(hh, dpre, 'tn', BF16, name="ffn_dw_down", **wg_tiles)
            return dx, dwg, dwu, dwd, dlg, dlb, cres

        dx2, dwg2, dwu2, dwd2, dl3g, dl3b, cres = ffn_back(dy, s['pre3'], s['gs2'], s['us2'], s['ffn2'], s['x2'], row('ln3_g'),
                                                           row('ln3_b'), _rs_sibling(full_l1) if first else None)
        if first:
            part_l1 = pair(full_l1, cres[0])
            full_c2 = blocks([dwg2, dwu2, dwd2])
        (d_oa, d_ob, d_yc, d_x1), (dga, dgb, dgc, dwglu, dwout, dl2g, dl2b), cres = _rowwise_vjp(
            _mix_out, [s['out_a'], s['out_b'], s['yc'], s['x1']], s['mix_params'], [dx2], _tile(t, 256), "mix_out_bwd",
            carry=_rs_sibling(full_c2) if first else None)
        if first:
            part_c2 = pair(full_c2, cres[0])
        (d_ax, d_ag, dcw, dcb, dwa, dba, dwx, dbx, dlam) = _rg_bwd(
            s['z'], s['h_a'], d_oa, s['conv_full'], row('conv_b'), s['wa'], row('rg_b_a'), s['wx'], row('rg_b_x'), row('rg_lambda'))
        (dq, dk, dv, dcrow), cres = _attn_bwd(s['z'], s['crow'], s['lse'], d_ob,
                                              carry=_join(_rs_chips(part_l1), _rs_chips(part_c2)) if first else None)
        if first:
            rb_l1, rb_c2 = cres
        dc_pad = jnp.pad(dcrow.reshape(N_HEADS, t).T, ((0, 0), (0, LANES - N_HEADS)))
        dzf, dbf = _fgate_bwd(s['z'], s['bf'], dc_pad)
        du_c, dbd_re, dbd_im, dabr, dabi, dcd_re, dcd_im, dd = _s5_bwd(s['z'], d_yc, *s['s5m'], row('s5_d'))
        dz = jnp.concatenate([d_ax, d_ag, dq, dk, dv, dzf, du_c], axis=1)
        dx1 = _mm(dz, s['win'], 'nt', F32, tm, d, Z_W, "mix_in_dx", add=d_x1)
        dwin = _mm(s['x1'], dz, 'tn', BF16, d, Z_W, tm, "mix_in_dw")
        if first:
            full_m = mixer_blocks(dwin, dwout, dwglu)
            (ra_m,) = _run(_rs_sibling(full_m), "rs_sibling")
            part_m = pair(full_m, ra_m)
        dx0, dwg1, dwu1, dwd1, dl1g, dl1b, cres = ffn_back(dx1, s['pre1'], s['gs1'], s['us1'], s['ffn1'], s['x0'], row('ln1_g'),
                                                           row('ln1_b'), _rs_chips(part_m) if first else None)
        if first:
            rb_m = cres[0]
            full_c1 = blocks([dwg1, dwu1, dwd1])
            (ra_c1,) = _run(_rs_sibling(full_c1), "rs_sibling")
            part_c1 = pair(full_c1, ra_c1)
            (rb_c1,) = _run(_rs_chips(part_c1), "rs_chips")
        else:
            full_l1 = jnp.concatenate([blocks([dwg1, dwu1, dwd1, dwg2, dwu2, dwd2]), mixer_blocks(dwin, dwout, dwglu)], axis=1)
        dy = dx0

        dbb_re, dbb_im = _diag_blocks(dbd_re, N_GROUPS // 2), _diag_blocks(dbd_im, N_GROUPS // 2)
        dcm_re, dcm_im = _diag_blocks(dcd_re, N_GROUPS // 2), _diag_blocks(dcd_im, N_GROUPS // 2)
        da_re, da_im, dlog_dt, db_re, db_im = _s5_prep_bwd(*s['s5p'], dabr.reshape(N_GROUPS, C_STATE), dabi.reshape(N_GROUPS, C_STATE), dbb_re, dbb_im)

        sg = dict(ln1_g=dl1g, ln1_b=dl1b, conv_w=dcw, conv_b=dcb, rg_w_a=_diag_blocks(dwa, 2), rg_b_a=dba,
                  rg_w_x=_diag_blocks(dwx, 2), rg_b_x=dbx, rg_lambda=dlam, fox_b_f=dbf[:, :N_HEADS],
                  s5_a_re=da_re, s5_a_im=da_im, s5_log_dt=dlog_dt, s5_b_re=db_re.transpose(0, 2, 1), s5_b_im=db_im.transpose(0, 2, 1),
                  s5_c_re=dcm_re.transpose(0, 2, 1), s5_c_im=dcm_im.transpose(0, 2, 1), s5_d=dd,
                  mix_norm_g=jnp.concatenate([dga, dgb, dgc], axis=1), ln2_g=dl2g, ln2_b=dl2b, ln3_g=dl3g, ln3_b=dl3b)
        small_grads[l] = sg

    grad_x = dy.reshape(x.shape)
    quad = lambda part, rb: _quad_add(part, rb, chip_idx, "rs_add_chips")
    own = {}

    def take(rows_f32, keys, l):
        off = 0
        for k, r in keys:
            own[(l, k)] = rows_f32[off:off + r]
            off += r

    ffn_keys = lambda names: [(k, fs) for k in names]
    mix_keys = [('win', win_rows), ('wout', ds), ('glu', glu_rows)]
    take(quad(part_l1, rb_l1), ffn_keys(FFN1 + FFN2) + mix_keys, 1)
    take(quad(part_c2, rb_c2), ffn_keys(FFN2), 0)
    take(quad(part_m, rb_m), mix_keys, 0)
    take(quad(part_c1, rb_c1), ffn_keys(FFN1), 0)

    grads = {}
    for k, n in zip(FFN1 + FFN2, ['ffn1_w_gate', 'ffn1_w_up', 'ffn1_w_down', 'ffn2_w_gate', 'ffn2_w_up', 'ffn2_w_down']):
        grads[n] = jnp.stack([own[(l, k)].T if 'down' not in n else own[(l, k)] for l in range(DEPTH)])
    gwin = jnp.stack([own[(l, 'win')].reshape(ds, Z_W) for l in range(DEPTH)])
    grads['w_in'] = jnp.concatenate([gwin[:, :, :Z_F + N_HEADS], gwin[:, :, Z_U:]], axis=2)
    grads['w_out'] = jnp.stack([own[(l, 'wout')] for l in range(DEPTH)])
    grads['s5_w_glu'] = jnp.stack([own[(l, 'glu')].reshape(D_C // 8, D_C) for l in range(DEPTH)])

    small_names = ['conv_w'] + SMALL
    small_shapes = [((DEPTH, CONV_WIDTH, D_A) if n == 'conv_w' else w[n].shape) for n in small_names]
    conv_zero = jnp.zeros((DEPTH, CONV_WIDTH, D_A), F32)
    summed = _all_reduce_small(_pack_small([small_grads[l][n] for n in small_names for l in range(DEPTH)]))
    for n, g in zip(small_names, _unpack_small(summed, small_shapes)):
        grads[n] = g
    grads['conv_w'] = lax.dynamic_slice_in_dim(grads['conv_w'], me * (D_A // 8), D_A // 8, axis=2)

    delta, new_m, new_v = {}, {}, {}
    for n in BIG + ['conv_w']:
        sh = w[n].shape
        two = lambda v: v.reshape(-1, sh[-1])
        dl, nm, nv = _adamw(two(w[n]), two(grads[n]), two(a['m_' + n]), two(a['v_' + n]), "adamw_" + n)
        delta[n], new_m[n], new_v[n] = dl.reshape(sh), nm.reshape(sh), nv.reshape(sh)
    dl, nm, nv = _adamw(_pack_small([conv_zero] + [w[n] for n in SMALL]), summed,
                        _pack_small([conv_zero] + [a['m_' + n] for n in SMALL]),
                        _pack_small([conv_zero] + [a['v_' + n] for n in SMALL]), "adamw_small")
    for n, v1, v2, v3 in zip(small_names[1:], _unpack_small(dl, small_shapes)[1:], _unpack_small(nm, small_shapes)[1:],
                             _unpack_small(nv, small_shapes)[1:]):
        delta[n], new_m[n], new_v[n] = v1, v2, v3

    return (loss, grad_x, *[grads[n] for n in WEIGHTS], *[delta[n] for n in WEIGHTS], *[new_m[n] for n in WEIGHTS],
            *[new_v[n] for n in WEIGHTS])
```

```python
import functools
import math

import jax
import jax.numpy as jnp
import numpy as np
from jax import lax
from jax.experimental import pallas as pl
from jax.experimental.pallas import tpu as pltpu

F32 = jnp.float32
BF16 = jnp.bfloat16
MESH = pl.DeviceIdType.MESH

DEPTH = 2
ALPHA = (2 * DEPTH) ** 0.25
LN_EPS = 1e-5
RMS_EPS = 1e-6
RG_C = 8.0
CONV_WIDTH = 4
HEAD_DIM = 64
C_GROUP = 16
C_STATE = 64
D_A = 384
D_B = 384
D_C = 256
N_HEADS = D_B // HEAD_DIM
N_GROUPS = D_C // C_GROUP
N_STATE = N_GROUPS * C_STATE
Z_F = 2 * D_A + 3 * D_B
Z_U = Z_F + 128
Z_W = Z_U + D_C
N_IN = Z_F + N_HEADS + D_C
ADAM_LR, ADAM_B1, ADAM_B2, ADAM_EPS, ADAM_WD, ADAM_STEP = 0.001, 0.9, 0.999, 1e-08, 0.01, 10
LANES = 128
NEG = -1e30

WEIGHTS = ['ffn1_w_gate', 'ffn1_w_up', 'ffn1_w_down', 'ln1_g', 'ln1_b', 'w_in', 'conv_w', 'conv_b', 'rg_w_a', 'rg_b_a',
           'rg_w_x', 'rg_b_x', 'rg_lambda', 'fox_b_f', 's5_a_re', 's5_a_im', 's5_log_dt', 's5_b_re', 's5_b_im', 's5_c_re',
           's5_c_im', 's5_d', 's5_w_glu', 'mix_norm_g', 'w_out', 'ln2_g', 'ln2_b', 'ffn2_w_gate', 'ffn2_w_up', 'ffn2_w_down',
           'ln3_g', 'ln3_b']
BIG = ['ffn1_w_gate', 'ffn1_w_up', 'ffn1_w_down', 'w_in', 's5_w_glu', 'w_out', 'ffn2_w_gate', 'ffn2_w_up', 'ffn2_w_down']
SMALL_TAIL = ['fox_b_f', 's5_log_dt']
SMALL = [n for n in WEIGHTS if n not in BIG and n != 'conv_w' and n not in SMALL_TAIL] + SMALL_TAIL


def _sig(x):
    return 1.0 / (1.0 + jnp.exp(-x))


def _gelu(x):
    return 0.5 * x * (1.0 + jnp.tanh(math.sqrt(2.0 / math.pi) * (x + 0.044715 * (x * x * x))))


def _softplus(x):
    return jnp.maximum(x, 0.0) + jnp.log(1.0 + jnp.exp(jnp.minimum(x, -x)))


def _dot(a, b, dims):
    return lax.dot_general(a.astype(BF16), b.astype(BF16), (dims, ((), ())), preferred_element_type=F32)


NN = ((1,), (0,))
NT = ((1,), (1,))
TN = ((0,), (0,))


@jax.custom_vjp
def _bdot(a, w):
    return _dot(a, w, NN)


def _bdot_fwd(a, w):
    return _dot(a, w, NN), (a, w)


def _bdot_bwd(res, ct):
    a, w = res
    return _dot(ct, w, NT), _dot(a, ct, TN)


_bdot.defvjp(_bdot_fwd, _bdot_bwd)


def _ln(pre, g, b):
    mu = jnp.mean(pre, axis=-1, keepdims=True)
    xc = pre - mu
    var = jnp.mean(xc * xc, axis=-1, keepdims=True)
    return xc * lax.rsqrt(var + LN_EPS) * g + b


def _rms(x, g):
    return x * lax.rsqrt(jnp.mean(x * x, axis=-1, keepdims=True) + RMS_EPS) * g


def _tile(n, want):
    return want if n % want == 0 else n


class _Carry:
    def __init__(self, ins, outs, aliases, plan, n):
        self.ins, self.outs, self.aliases, self.plan, self.n = list(ins), list(outs), dict(aliases), plan, n


def _join(a, b):
    na, ma = len(a.ins), len(a.outs)

    def plan(x, y, c, ins, outs):
        return a.plan(x, y, c, ins[:na], outs[:ma]) + b.plan(x, y, c, ins[na:], outs[ma:])

    aliases = dict(a.aliases)
    aliases.update({na + i: ma + j for i, j in b.aliases.items()})
    return _Carry(a.ins + b.ins, a.outs + b.outs, aliases, plan, a.n + b.n)


def _copies(carry, cins, couts, send, recv):
    x, y, c = lax.axis_index("x"), lax.axis_index("y"), lax.axis_index("c")
    res = []
    for k, (s, d, peer) in enumerate(carry.plan(x, y, c, cins, couts)):
        if peer is None:
            res.append(pltpu.make_async_copy(s, d, send.at[k]))
        else:
            res.append(pltpu.make_async_remote_copy(src_ref=s, dst_ref=d, send_sem=send.at[k], recv_sem=recv.at[k],
                                                    device_id=peer, device_id_type=MESH))
    return res


def _call(body, grid, in_specs, out_specs, out_shape, scratch, semantics, name, args, carry=None):
    n_in, n_out, n_scr = len(in_specs), len(out_specs), len(scratch)
    if carry is None:
        res = pl.pallas_call(body, grid=grid, in_specs=in_specs, out_specs=out_specs, out_shape=out_shape,
                             scratch_shapes=scratch, compiler_params=pltpu.CompilerParams(dimension_semantics=semantics),
                             name=name)(*args)
        return list(res), []
    nci, nco = len(carry.ins), len(carry.outs)

    def wrapped(*refs):
        o0 = n_in + nci
        s0 = o0 + n_out + nco
        cins, couts = refs[n_in:o0], refs[o0 + n_out:s0]
        send, recv = refs[s0 + n_scr:]
        first = functools.reduce(jnp.logical_and, [pl.program_id(k) == 0 for k in range(len(grid))])
        last = functools.reduce(jnp.logical_and, [pl.program_id(k) == grid[k] - 1 for k in range(len(grid))])

        @pl.when(first)
        def _():
            for cp in _copies(carry, cins, couts, send, recv):
                cp.start()

        body(*refs[:n_in], *refs[o0:o0 + n_out], *refs[s0:s0 + n_scr])

        @pl.when(last)
        def _():
            for cp in _copies(carry, cins, couts, send, recv):
                cp.wait()

    hbm = pl.BlockSpec(memory_space=pl.ANY)
    res = pl.pallas_call(
        wrapped, grid=grid, in_specs=list(in_specs) + [hbm] * nci, out_specs=list(out_specs) + [hbm] * nco,
        out_shape=list(out_shape) + carry.outs, scratch_shapes=list(scratch) + [pltpu.SemaphoreType.DMA((carry.n,))] * 2,
        input_output_aliases={n_in + i: n_out + j for i, j in carry.aliases.items()},
        compiler_params=pltpu.CompilerParams(dimension_semantics=("arbitrary",) * len(grid), has_side_effects=True),
        name=name)(*args, *carry.ins)
    return list(res[:n_out]), list(res[n_out:])


def _run(carry, name):
    nci, nco = len(carry.ins), len(carry.outs)

    def body(*refs):
        cps = _copies(carry, refs[:nci], refs[nci:nci + nco], refs[-2], refs[-1])
        for cp in cps:
            cp.start()
        for cp in cps:
            cp.wait()

    hbm = pl.BlockSpec(memory_space=pl.ANY)
    return pl.pallas_call(
        body, in_specs=[hbm] * nci, out_specs=[hbm] * nco, out_shape=carry.outs, input_output_aliases=carry.aliases,
        scratch_shapes=[pltpu.SemaphoreType.DMA((carry.n,))] * 2, compiler_params=pltpu.CompilerParams(has_side_effects=True),
        name=name)(*carry.ins)


def _ag_chips(shard):
    def plan(x, y, c, ins, outs):
        dst = outs[0].at[4 * x + 2 * y + c]
        return [(ins[0], dst, None)] + [(ins[0], dst, (px, py, c)) for px, py in ((1 - x, y), (x, 1 - y), (1 - x, 1 - y))]

    return _Carry([shard], [jax.ShapeDtypeStruct((8,) + shard.shape, shard.dtype)], {}, plan, 4)


def _ag_sibling(g):
    def plan(x, y, c, ins, outs):
        return [(outs[0].at[2 * q + c], outs[0].at[2 * q + c], (x, y, 1 - c)) for q in range(4)]

    return _Carry([g], [jax.ShapeDtypeStruct(g.shape, g.dtype)], {0: 0}, plan, 4)


def _rs_sibling(full):
    def plan(x, y, c, ins, outs):
        return [(ins[0].at[2 * q + (1 - c)], outs[0].at[q], (x, y, 1 - c)) for q in range(4)]

    return _Carry([full], [jax.ShapeDtypeStruct((4,) + full.shape[1:], full.dtype)], {}, plan, 4)


def _rs_chips(part):
    def plan(x, y, c, ins, outs):
        res = []
        for k, (dx, dy) in enumerate(((1, 0), (0, 1), (1, 1))):
            tx, ty = x ^ dx, y ^ dy
            res.append((ins[0].at[2 * tx + ty], outs[0].at[k], (tx, ty, c)))
        return res

    return _Carry([part], [jax.ShapeDtypeStruct((3,) + part.shape[1:], part.dtype)], {}, plan, 3)


def _mm(a, b, dims, out_dtype, tm, tn, tk, name, add=None):
    if dims == 'nn':
        (m, k), n = a.shape, b.shape[1]
        a_spec = pl.BlockSpec((tm, tk), lambda i, j, q: (i, q))
        b_spec = pl.BlockSpec((tk, tn), lambda i, j, q: (q, j))
        dn = NN
    elif dims == 'nt':
        (m, k), n = a.shape, b.shape[0]
        a_spec = pl.BlockSpec((tm, tk), lambda i, j, q: (i, q))
        b_spec = pl.BlockSpec((tn, tk), lambda i, j, q: (j, q))
        dn = NT
    else:
        (k, m), n = a.shape, b.shape[1]
        a_spec = pl.BlockSpec((tk, tm), lambda i, j, q: (q, i))
        b_spec = pl.BlockSpec((tk, tn), lambda i, j, q: (q, j))
        dn = TN
    nk = k // tk
    o_spec = pl.BlockSpec((tm, tn), lambda i, j, q: (i, j))

    def body(*refs):
        if add is None:
            a_ref, b_ref, o_ref, acc_ref = refs
        else:
            a_ref, b_ref, add_ref, o_ref, acc_ref = refs
        q = pl.program_id(2)
        part = _dot(a_ref[...], b_ref[...], dn)

        @pl.when(q == 0)
        def _():
            acc_ref[...] = part

        @pl.when(q > 0)
        def _():
            acc_ref[...] += part

        @pl.when(q == nk - 1)
        def _():
            r = acc_ref[...]
            if add is not None:
                r = r + add_ref[...]
            o_ref[...] = r.astype(o_ref.dtype)

    ins = [a, b] + ([] if add is None else [add])
    specs = [a_spec, b_spec] + ([] if add is None else [o_spec])
    return pl.pallas_call(
        body, grid=(m // tm, n // tn, nk), in_specs=specs, out_specs=o_spec,
        out_shape=jax.ShapeDtypeStruct((m, n), out_dtype), scratch_shapes=[pltpu.VMEM((tm, tn), F32)],
        compiler_params=pltpu.CompilerParams(dimension_semantics=("parallel", "parallel", "arbitrary")), name=name)(*ins)


def _rowwise(fn, rows, params, outs, tm, name):
    t = rows[0].shape[0]
    nr, npar = len(rows), len(params)

    def body(*refs):
        r = [x[...] for x in refs[:nr]]
        p = [x[...] for x in refs[nr:nr + npar]]
        res = fn(*r, *p)
        for o_ref, o in zip(refs[nr + npar:], res):
            o_ref[...] = o.astype(o_ref.dtype)

    in_specs = ([pl.BlockSpec((tm, a.shape[1]), lambda i: (i, 0)) for a in rows]
                + [pl.BlockSpec(p.shape, lambda i: (0, 0)) for p in params])
    return pl.pallas_call(
        body, grid=(t // tm,), in_specs=in_specs,
        out_specs=[pl.BlockSpec((tm, c), lambda i: (i, 0)) for c, _ in outs],
        out_shape=[jax.ShapeDtypeStruct((t, c), d) for c, d in outs],
        compiler_params=pltpu.CompilerParams(dimension_semantics=("parallel",)), name=name)(*rows, *params)


def _rowwise_vjp(fn, rows, params, cots, tm, name, carry=None):
    t = rows[0].shape[0]
    nr, npar, nc = len(rows), len(params), len(cots)

    def body(*refs):
        r = [x[...] for x in refs[:nr]]
        p = [x[...] for x in refs[nr:nr + npar]]
        c = [x[...] for x in refs[nr + npar:nr + npar + nc]]
        o_refs = refs[nr + npar + nc:]
        _, pull = jax.vjp(fn, *r, *p)
        grads = pull(tuple(c))
        for o_ref, g in zip(o_refs[:nr], grads[:nr]):
            o_ref[...] = g
        i = pl.program_id(0)

        @pl.when(i == 0)
        def _():
            for o_ref, g in zip(o_refs[nr:], grads[nr:]):
                o_ref[...] = g

        @pl.when(i > 0)
        def _():
            for o_ref, g in zip(o_refs[nr:], grads[nr:]):
                o_ref[...] += g

    row_spec = lambda a: pl.BlockSpec((tm, a.shape[1]), lambda i: (i, 0))
    par_spec = lambda p: pl.BlockSpec(p.shape, lambda i: (0, 0))
    res, cres = _call(
        body, (t // tm,),
        [row_spec(a) for a in rows] + [par_spec(p) for p in params] + [row_spec(a) for a in cots],
        [row_spec(a) for a in rows] + [par_spec(p) for p in params],
        [jax.ShapeDtypeStruct(a.shape, F32) for a in rows] + [jax.ShapeDtypeStruct(p.shape, F32) for p in params],
        [], ("arbitrary",), name, [*rows, *params, *cots], carry)
    return res[:nr], res[nr:], cres


def _ffn_fwd(x, wgt, wut, wd, ln_g, ln_b, tm, tf, carry=None):
    t, d = x.shape
    f = wgt.shape[0]
    nj = f // tf

    def body(x_ref, wg_ref, wu_ref, wd_ref, g_ref, b_ref, y_ref, pre_ref, gs_ref, us_ref, acc_ref):
        j = pl.program_id(1)
        xv = x_ref[...]
        xb = xv.astype(BF16)
        g = _dot(xb, wg_ref[...], NT)
        u = _dot(xb, wu_ref[...], NT)
        gs_ref[...] = g.astype(BF16)
        us_ref[...] = u.astype(BF16)
        part = _dot(g * _sig(g) * u, wd_ref[...], NN)

        @pl.when(j == 0)
        def _():
            acc_ref[...] = part

        @pl.when(j > 0)
        def _():
            acc_ref[...] += part

        @pl.when(j == nj - 1)
        def _():
            pre = ALPHA * xv + 0.5 * acc_ref[...]
            pre_ref[...] = pre
            y_ref[...] = _ln(pre, g_ref[...], b_ref[...])

    w_spec = pl.BlockSpec((tf, d), lambda i, j: (j, 0))
    x_spec = pl.BlockSpec((tm, d), lambda i, j: (i, 0))
    v_spec = pl.BlockSpec((1, d), lambda i, j: (0, 0))
    h_spec = pl.BlockSpec((tm, tf), lambda i, j: (i, j))
    return _call(
        body, (t // tm, nj), [x_spec, w_spec, w_spec, w_spec, v_spec, v_spec], [x_spec, x_spec, h_spec, h_spec],
        [jax.ShapeDtypeStruct((t, d), F32), jax.ShapeDtypeStruct((t, d), F32),
         jax.ShapeDtypeStruct((t, f), BF16), jax.ShapeDtypeStruct((t, f), BF16)],
        [pltpu.VMEM((tm, d), F32)], ("parallel", "arbitrary"), "ffn_fwd", [x, wgt, wut, wd, ln_g, ln_b], carry)


def _ffn_bwd(dpre, gs, us, wgt, wut, wd, tm, tf, carry=None):
    t, d = dpre.shape
    f = wgt.shape[0]
    nj = f // tf

    def body(dp_ref, gs_ref, us_ref, wg_ref, wu_ref, wd_ref, dx_ref, dg_ref, du_ref, hh_ref, acc_ref):
        j = pl.program_id(1)
        dp = dp_ref[...]
        dh = _dot(0.5 * dp, wd_ref[...], NT)
        g = gs_ref[...].astype(F32)
        u = us_ref[...].astype(F32)
        s = _sig(g)
        sl = g * s
        dg = (dh * u * (s * (1.0 + g * (1.0 - s)))).astype(BF16)
        du = (dh * sl).astype(BF16)
        dg_ref[...] = dg
        du_ref[...] = du
        hh_ref[...] = (0.5 * sl * u).astype(BF16)
        part = _dot(dg, wg_ref[...], NN) + _dot(du, wu_ref[...], NN)

        @pl.when(j == 0)
        def _():
            acc_ref[...] = part

        @pl.when(j > 0)
        def _():
            acc_ref[...] += part

        @pl.when(j == nj - 1)
        def _():
            dx_ref[...] = ALPHA * dp + acc_ref[...]

    w_spec = pl.BlockSpec((tf, d), lambda i, j: (j, 0))
    x_spec = pl.BlockSpec((tm, d), lambda i, j: (i, 0))
    h_spec = pl.BlockSpec((tm, tf), lambda i, j: (i, j))
    return _call(
        body, (t // tm, nj), [x_spec, h_spec, h_spec, w_spec, w_spec, w_spec], [x_spec, h_spec, h_spec, h_spec],
        [jax.ShapeDtypeStruct((t, d), F32)] + [jax.ShapeDtypeStruct((t, f), BF16)] * 3,
        [pltpu.VMEM((tm, d), F32)], ("parallel", "arbitrary"), "ffn_bwd", [dpre, gs, us, wgt, wut, wd], carry)


def _scan8(a_ref, b_ref, out_ref, t, reverse=False):
    w = out_ref.shape[-1]
    sub = lax.broadcasted_iota(jnp.int32, (8, w), 0)

    def step(g, carry):
        r0 = pl.multiple_of((t // 8 - 1 - g if reverse else g) * 8, 8)
        bv = b_ref[pl.ds(r0, 8), :]
        av = None if a_ref is None else a_ref[pl.ds(r0, 8), :]
        for s in (1, 2, 4):
            ok = (sub < 8 - s) if reverse else (sub >= s)
            shift = 8 - s if reverse else s
            b_sh = jnp.where(ok, pltpu.roll(bv, shift, 0), 0.0)
            if av is None:
                bv = bv + b_sh
            else:
                bv = av * b_sh + bv
                av = av * jnp.where(ok, pltpu.roll(av, shift, 0), 1.0)
        h = bv + carry if av is None else bv + av * carry
        out_ref[pl.ds(r0, 8), :] = h
        return jnp.sum(jnp.where(sub == (0 if reverse else 7), h, 0.0), axis=0, keepdims=True)

    lax.fori_loop(0, t // 8, step, jnp.zeros((1, w), F32))


def _rg_local(xa, wa, ba, wx, bx, lam):
    r = _sig(_bdot(xa, wa) + ba)
    i = _sig(_bdot(xa, wx) + bx)
    log_a = -RG_C * r * _softplus(-lam)
    a = jnp.exp(log_a)
    mult = jnp.sqrt(-jnp.tanh(log_a) * (a * a + 1.0))
    return a, mult * (i * xa)


def _conv_taps(ext, n):
    return [ext[8:, :]] + [pltpu.roll(ext, s, 0)[8:, :] for s in (1, 2, 3)]


def _rg_fwd(z, cw, cb, wa, ba, wx, bx, lam):
    t = z.shape[0]
    cr = _tile(t, 256)
    nb = D_A // LANES

    def body(ax_ref, ag_ref, cw_ref, cb_ref, wa_ref, ba_ref, wx_ref, bx_ref, lam_ref, out_ref, h_ref, axp, a_s, b_s):
        axp[pl.ds(0, 8), :] = jnp.zeros((8, LANES), F32)
        pltpu.sync_copy(ax_ref, axp.at[pl.ds(8, t)])
        w = [cw_ref[pl.ds(k, 1), :] for k in range(CONV_WIDTH)]

        def chunk(c, carry):
            t0 = pl.multiple_of(c * cr, cr)
            taps = _conv_taps(axp[pl.ds(t0, cr + 8), :], cr)
            xa = cb_ref[...] + w[3] * taps[0] + w[2] * taps[1] + w[1] * taps[2] + w[0] * taps[3]
            a, gated = _rg_local(xa, wa_ref[...], ba_ref[...], wx_ref[...], bx_ref[...], lam_ref[...])
            a_s[pl.ds(t0, cr), :] = a
            b_s[pl.ds(t0, cr), :] = gated
            return carry

        lax.fori_loop(0, t // cr, chunk, 0)

        _scan8(a_s, b_s, h_ref, t)

        def fin(c, carry):
            t0 = pl.multiple_of(c * cr, cr)
            out_ref[pl.ds(t0, cr), :] = _gelu(ag_ref[pl.ds(t0, cr), :]) * h_ref[pl.ds(t0, cr), :]
            return carry

        lax.fori_loop(0, t // cr, fin, 0)

    col = lambda off: pl.BlockSpec((t, LANES), lambda b: (0, off + b))
    vec = pl.BlockSpec((1, LANES), lambda b: (0, b))
    mat = pl.BlockSpec((None, LANES, LANES), lambda b: (b, 0, 0))
    return pl.pallas_call(
        body, grid=(nb,),
        in_specs=[col(0), col(nb), pl.BlockSpec((CONV_WIDTH, LANES), lambda b: (0, b)), vec, mat, vec, mat, vec, vec],
        out_specs=[col(0), col(0)],
        out_shape=[jax.ShapeDtypeStruct((t, D_A), F32), jax.ShapeDtypeStruct((t, D_A), F32)],
        scratch_shapes=[pltpu.VMEM((t + 8, LANES), F32), pltpu.VMEM((t, LANES), F32), pltpu.VMEM((t, LANES), F32)],
        compiler_params=pltpu.CompilerParams(dimension_semantics=("arbitrary",)), name="rglru_fwd")(
            z, z, cw, cb, wa, ba, wx, bx, lam)


def _rg_bwd(z, h, dout, cw, cb, wa, ba, wx, bx, lam):
    t = z.shape[0]
    cr = _tile(t, 256)
    nb = D_A // LANES

    def body(ax_ref, ag_ref, h_ref, do_ref, cw_ref, cb_ref, wa_ref, ba_ref, wx_ref, bx_ref, lam_ref,
             dax_ref, dag_ref, dcw_ref, dcb_ref, dwa_ref, dba_ref, dwx_ref, dbx_ref, dlam_ref,
             axp, hp, xa_s, a_s, g_s, dxa_s, u_s):
        zero8 = jnp.zeros((8, LANES), F32)
        axp[pl.ds(0, 8), :] = zero8
        hp[pl.ds(0, 8), :] = zero8
        dxa_s[pl.ds(t, 8), :] = zero8
        u_s[pl.ds(t, 8), :] = zero8
        pltpu.sync_copy(ax_ref, axp.at[pl.ds(8, t)])
        pltpu.sync_copy(h_ref, hp.at[pl.ds(8, t)])
        w = [cw_ref[pl.ds(k, 1), :] for k in range(CONV_WIDTH)]
        for ref in (dcw_ref, dcb_ref, dwa_ref, dba_ref, dwx_ref, dbx_ref, dlam_ref):
            ref[...] = jnp.zeros(ref.shape, F32)

        def p1(c, carry):
            t0 = pl.multiple_of(c * cr, cr)
            taps = _conv_taps(axp[pl.ds(t0, cr + 8), :], cr)
            xa = cb_ref[...] + w[3] * taps[0] + w[2] * taps[1] + w[1] * taps[2] + w[0] * taps[3]
            a, _ = _rg_local(xa, wa_ref[...], ba_ref[...], wx_ref[...], bx_ref[...], lam_ref[...])
            xa_s[pl.ds(t0, cr), :] = xa
            a_s[pl.ds(t0, cr), :] = a
            ag = ag_ref[pl.ds(t0, cr), :]
            dov = do_ref[pl.ds(t0, cr), :]
            gel, pull = jax.vjp(_gelu, ag)
            g_s[pl.ds(t0, cr), :] = dov * gel
            u_s[pl.ds(t0, cr), :] = a * (dov * gel)
            dag_ref[pl.ds(t0, cr), :] = pull(dov * h_ref[pl.ds(t0, cr), :])[0]
            return carry

        lax.fori_loop(0, t // cr, p1, 0)
        _scan8(a_s, u_s, u_s, t, reverse=True)

        def p3(c, carry):
            t0 = pl.multiple_of(c * cr, cr)
            g = g_s[pl.ds(t0, cr), :] + pltpu.roll(u_s[pl.ds(t0, cr + 8), :], cr + 7, 0)[:cr, :]
            h_prev = pltpu.roll(hp[pl.ds(t0, cr + 8), :], 1, 0)[8:, :]
            _, pull = jax.vjp(_rg_local, xa_s[pl.ds(t0, cr), :], wa_ref[...], ba_ref[...], wx_ref[...], bx_ref[...],
                              lam_ref[...])
            dxa, dwa, dba, dwx, dbx, dlam = pull((g * h_prev, g))
            dxa_s[pl.ds(t0, cr), :] = dxa
            dwa_ref[...] += dwa
            dba_ref[...] += dba
            dwx_ref[...] += dwx
            dbx_ref[...] += dbx
            dlam_ref[...] += dlam
            return carry

        lax.fori_loop(0, t // cr, p3, 0)

        def p4(c, carry):
            t0 = pl.multiple_of(c * cr, cr)
            ext = dxa_s[pl.ds(t0, cr + 8), :]
            n = cr + 8
            ahead = [ext[:cr, :]] + [pltpu.roll(ext, n - s, 0)[:cr, :] for s in (1, 2, 3)]
            dax_ref[pl.ds(t0, cr), :] = w[3] * ahead[0] + w[2] * ahead[1] + w[1] * ahead[2] + w[0] * ahead[3]
            taps = _conv_taps(axp[pl.ds(t0, cr + 8), :], cr)
            dxa = ahead[0]
            for k in range(CONV_WIDTH):
                dcw_ref[pl.ds(k, 1), :] += jnp.sum(dxa * taps[CONV_WIDTH - 1 - k], axis=0, keepdims=True)
            dcb_ref[...] += jnp.sum(dxa, axis=0, keepdims=True)
            return carry

        lax.fori_loop(0, t // cr, p4, 0)

    col = lambda off: pl.BlockSpec((t, LANES), lambda b: (0, off + b))
    vec = pl.BlockSpec((1, LANES), lambda b: (0, b))
    mat = pl.BlockSpec((None, LANES, LANES), lambda b: (b, 0, 0))
    cws = pl.BlockSpec((CONV_WIDTH, LANES), lambda b: (0, b))
    sds = jax.ShapeDtypeStruct
    return pl.pallas_call(
        body, grid=(nb,),
        in_specs=[col(0), col(nb), col(0), col(0), cws, vec, mat, vec, mat, vec, vec],
        out_specs=[col(0), col(0), cws, vec, mat, vec, mat, vec, vec],
        out_shape=[sds((t, D_A), F32), sds((t, D_A), F32), sds((CONV_WIDTH, D_A), F32), sds((1, D_A), F32),
                   sds((nb, LANES, LANES), F32), sds((1, D_A), F32), sds((nb, LANES, LANES), F32), sds((1, D_A), F32),
                   sds((1, D_A), F32)],
        scratch_shapes=[pltpu.VMEM((t + 8, LANES), F32), pltpu.VMEM((t + 8, LANES), F32), pltpu.VMEM((t, LANES), F32),
                        pltpu.VMEM((t, LANES), F32), pltpu.VMEM((t, LANES), F32), pltpu.VMEM((t + 8, LANES), F32),
                        pltpu.VMEM((t + 8, LANES), F32)],
        compiler_params=pltpu.CompilerParams(dimension_semantics=("arbitrary",)), name="rglru_bwd")(
            z, z, h, dout, cw, cb, wa, ba, wx, bx, lam)


def _fgate_fwd(z, bf):
    t = z.shape[0]

    def body(zf_ref, bf_ref, c_ref):
        c_ref[...] = -_softplus(-(zf_ref[...] + bf_ref[...]))
        _scan8(None, c_ref, c_ref, t)

    return pl.pallas_call(
        body, grid=(1,), in_specs=[pl.BlockSpec((t, LANES), lambda i: (0, Z_F // LANES)), pl.BlockSpec((1, LANES), lambda i: (0, 0))],
        out_specs=pl.BlockSpec((t, LANES), lambda i: (0, 0)), out_shape=jax.ShapeDtypeStruct((t, LANES), F32),
        compiler_params=pltpu.CompilerParams(dimension_semantics=("arbitrary",)), name="fgate_fwd")(z, bf)


def _fgate_bwd(z, bf, dc):
    t = z.shape[0]

    def body(zf_ref, bf_ref, dc_ref, dz_ref, db_ref):
        _scan8(None, dc_ref, dz_ref, t, reverse=True)
        dz = dz_ref[...] * _sig(-(zf_ref[...] + bf_ref[...]))
        dz_ref[...] = dz
        db_ref[...] = jnp.sum(dz, axis=0, keepdims=True)

    return pl.pallas_call(
        body, grid=(1,),
        in_specs=[pl.BlockSpec((t, LANES), lambda i: (0, Z_F // LANES)), pl.BlockSpec((1, LANES), lambda i: (0, 0)),
                  pl.BlockSpec((t, LANES), lambda i: (0, 0))],
        out_specs=[pl.BlockSpec((t, LANES), lambda i: (0, 0)), pl.BlockSpec((1, LANES), lambda i: (0, 0))],
        out_shape=[jax.ShapeDtypeStruct((t, LANES), F32), jax.ShapeDtypeStruct((1, LANES), F32)],
        compiler_params=pltpu.CompilerParams(dimension_semantics=("arbitrary",)), name="fgate_bwd")(z, bf, dc)


def _cast_rows(src_ref, dst_ref, t, rows, fn):
    def cp(c, carry):
        r0 = pl.multiple_of(c * rows, rows)
        dst_ref[pl.ds(r0, rows), :] = fn(src_ref[pl.ds(r0, rows), :]).astype(dst_ref.dtype)
        return carry

    lax.fori_loop(0, t // rows, cp, 0)


def _attn_groups(t):
    tq = _tile(t, 256)
    nq = t // tq
    grp = 4 if nq % 4 == 0 else 1
    return tq, nq, grp


def _attn_fwd(z, crow, carry=None):
    t = z.shape[0]
    tq, nq, grp = _attn_groups(t)
    tk = grp * tq
    scale = HEAD_DIM ** -0.5

    def body(q_ref, k_ref, v_ref, cr_ref, o_ref, lse_ref, kb_s, vb_s):
        lane = lax.broadcasted_iota(jnp.int32, (1, LANES), 1)
        hmask = [(lane // HEAD_DIM) == hh for hh in range(2)]
        _cast_rows(k_ref, kb_s, t, tq, lambda v: v)
        _cast_rows(v_ref, vb_s, t, tq, lambda v: v)

        def qblock(g, r):
            q0 = pl.multiple_of((g * grp + r) * tq, tq)
            qv = q_ref[pl.ds(q0, tq), :] * scale
            qa = [jnp.where(hmask[hh], qv, 0.0).astype(BF16) for hh in range(2)]

            def update(st, k0, width, off):
                kb = kb_s[pl.ds(k0, width), :]
                vb = vb_s[pl.ds(k0, width), :]
                new = []
                for hh in range(2):
                    m, l, acc = st[hh]
                    s = _dot(qa[hh], kb, NT) - cr_ref[hh, :, pl.ds(k0, width)]
                    if off is not None:
                        keep = (lax.broadcasted_iota(jnp.int32, (tq, width), 0) + off
                                >= lax.broadcasted_iota(jnp.int32, (tq, width), 1))
                        s = jnp.where(keep, s, NEG)
                    m_new = jnp.maximum(m, jnp.max(s, axis=-1, keepdims=True))
                    p = jnp.exp(s - m_new)
                    corr = jnp.exp(m - m_new)
                    new.append((m_new, corr * l + jnp.sum(p, axis=-1, keepdims=True), corr * acc + _dot(p, vb, NN)))
                return tuple(new)

            one = (jnp.full((tq, 1), NEG, F32), jnp.zeros((tq, 1), F32), jnp.zeros((tq, LANES), F32))
            st = lax.fori_loop(0, g, lambda j, st: update(st, pl.multiple_of(j * tk, tk), tk, None), (one, one))
            st = update(st, pl.multiple_of(g * tk, tk), (r + 1) * tq, r * tq)
            o_ref[pl.ds(q0, tq), :] = jnp.where(hmask[0], st[0][2] / st[0][1], st[1][2] / st[1][1])
            for hh in range(2):
                lse_ref[hh, pl.ds(q0, tq), :] = st[hh][0] + jnp.log(st[hh][1])

        def group(g, carry):
            for r in range(grp):
                qblock(g, r)
            return carry

        lax.fori_loop(0, nq // grp, group, 0)

    base = 2 * D_A // LANES
    nh = D_B // LANES
    col = lambda off: pl.BlockSpec((t, LANES), lambda p: (0, off + p))
    return _call(
        body, (nh,), [col(base), col(base + nh), col(base + 2 * nh), pl.BlockSpec((2, 1, t), lambda p: (p, 0, 0))],
        [col(0), pl.BlockSpec((2, t, 1), lambda p: (p, 0, 0))],
        [jax.ShapeDtypeStruct((t, D_B), F32), jax.ShapeDtypeStruct((N_HEADS, t, 1), F32)],
        [pltpu.VMEM((t, LANES), BF16), pltpu.VMEM((t, LANES), BF16)], ("parallel",), "attn_fwd", [z, z, z, crow], carry)


def _attn_bwd(z, crow, lse, do, carry=None):
    t = z.shape[0]
    tq, nq, grp = _attn_groups(t)
    tw = grp * tq
    scale = HEAD_DIM ** -0.5

    def body(q_ref, k_ref, v_ref, cr_ref, lse_ref, do_ref, dq_ref, dk_ref, dv_ref, dc_ref, qa_s, da_s, kb_s, vb_s, dl_s):
        lane = lax.broadcasted_iota(jnp.int32, (1, LANES), 1)
        hmask = [(lane // HEAD_DIM) == hh for hh in range(2)]
        _cast_rows(k_ref, kb_s, t, tq, lambda v: v)
        _cast_rows(v_ref, vb_s, t, tq, lambda v: v)
        for hh in range(2):
            _cast_rows(q_ref, qa_s.at[hh], t, tq, lambda v, hh=hh: jnp.where(hmask[hh], v * scale, 0.0))
            _cast_rows(do_ref, da_s.at[hh], t, tq, lambda v, hh=hh: jnp.where(hmask[hh], v, 0.0))
        _cast_rows(q_ref, dq_ref, t, tq, lambda v: jnp.zeros_like(v))

        def probs(hh, q0, nq_rows, k0, nk_rows, off):
            s = _dot(qa_s[hh, pl.ds(q0, nq_rows), :], kb_s[pl.ds(k0, nk_rows), :], NT) - cr_ref[hh, :, pl.ds(k0, nk_rows)]
            p = jnp.exp(s - lse_ref[hh, pl.ds(q0, nq_rows), :])
            if off is not None:
                keep = (lax.broadcasted_iota(jnp.int32, (nq_rows, nk_rows), 0) + off
                        >= lax.broadcasted_iota(jnp.int32, (nq_rows, nk_rows), 1))
                p = jnp.where(keep, p, 0.0)
            return p, _dot(da_s[hh, pl.ds(q0, nq_rows), :], vb_s[pl.ds(k0, nk_rows), :], NT)

        def delta(g, r):
            q0 = pl.multiple_of((g * grp + r) * tq, tq)

            def add(k0, width, off, acc):
                res = []
                for hh in range(2):
                    p, dp = probs(hh, q0, tq, k0, width, off)
                    res.append(acc[hh] + jnp.sum(p * dp, axis=-1, keepdims=True))
                return tuple(res)

            zcol = jnp.zeros((tq, 1), F32)
            acc = lax.fori_loop(0, g, lambda j, acc: add(pl.multiple_of(j * tw, tw), tw, None, acc), (zcol, zcol))
            acc = add(pl.multiple_of(g * tw, tw), (r + 1) * tq, r * tq, acc)
            for hh in range(2):
                dl_s[hh, pl.ds(q0, tq), :] = acc[hh]

        def delta_group(g, carry):
            for r in range(grp):
                delta(g, r)
            return carry

        lax.fori_loop(0, nq // grp, delta_group, 0)

        def kblock(g, r):
            k0 = pl.multiple_of((g * grp + r) * tq, tq)
            kb = kb_s[pl.ds(k0, tq), :]

            def upd(q0, height, off, st):
                dk, dv, dc = st[0], st[1], [st[2], st[3]]
                dqs = []
                for hh in range(2):
                    p, dp = probs(hh, q0, height, k0, tq, off)
                    ds = p * (dp - dl_s[hh, pl.ds(q0, height), :])
                    dv = dv + _dot(p, da_s[hh, pl.ds(q0, height), :], TN)
                    dk = dk + _dot(ds, qa_s[hh, pl.ds(q0, height), :], TN)
                    dqs.append(_dot(ds, kb, NN))
                    dc[hh] = dc[hh] - jnp.sum(ds, axis=0, keepdims=True)
                dq_ref[pl.ds(q0, height), :] += jnp.where(hmask[0], dqs[0], dqs[1]) * scale
                return dk, dv, dc[0], dc[1]

            zero = jnp.zeros((tq, LANES), F32)
            zrow = jnp.zeros((1, tq), F32)
            st = upd(k0, (grp - r) * tq, 0, (zero, zero, zrow, zrow))
            st = lax.fori_loop(g + 1, nq // grp, lambda i, st: upd(pl.multiple_of(i * tw, tw), tw, None, st), st)
            dk_ref[pl.ds(k0, tq), :] = st[0]
            dv_ref[pl.ds(k0, tq), :] = st[1]
            for hh in range(2):
                dc_ref[hh, :, pl.ds(k0, tq)] = st[2 + hh]

        def kgroup(g, carry):
            for r in range(grp):
                kblock(g, r)
            return carry

        lax.fori_loop(0, nq // grp, kgroup, 0)

    base = 2 * D_A // LANES
    nh = D_B // LANES
    col = lambda off: pl.BlockSpec((t, LANES), lambda p: (0, off + p))
    ccs = pl.BlockSpec((2, t, 1), lambda p: (p, 0, 0))
    crs = pl.BlockSpec((2, 1, t), lambda p: (p, 0, 0))
    return _call(
        body, (nh,), [col(base), col(base + nh), col(base + 2 * nh), crs, ccs, col(0)], [col(0), col(0), col(0), crs],
        [jax.ShapeDtypeStruct((t, D_B), F32)] * 3 + [jax.ShapeDtypeStruct((N_HEADS, 1, t), F32)],
        [pltpu.VMEM((2, t, LANES), BF16), pltpu.VMEM((2, t, LANES), BF16), pltpu.VMEM((t, LANES), BF16),
         pltpu.VMEM((t, LANES), BF16), pltpu.VMEM((2, t, 1), F32)], ("parallel",), "attn_bwd", [z, z, z, crow, lse, do], carry)


def _s5_disc(a_re, a_im, log_dt, b_re, b_im):
    dt = jnp.exp(log_dt)
    mag = jnp.exp(a_re * dt)
    ar = mag * jnp.cos(a_im * dt)
    ai = mag * jnp.sin(a_im * dt)
    den = a_re * a_re + a_im * a_im
    kr = ((ar - 1.0) * a_re + ai * a_im) / den
    ki = (ai * a_re - (ar - 1.0) * a_im) / den
    kr3, ki3 = kr[:, None, :], ki[:, None, :]
    return ar, ai, kr3 * b_re - ki3 * b_im, kr3 * b_im + ki3 * b_re


def _s5_prep(a_re, a_im, log_dt, b_re, b_im):
    g, p = a_re.shape
    gc = b_re.shape[1]

    def body(*refs):
        res = _s5_disc(*[r[...] for r in refs[:5]])
        for o_ref, v in zip(refs[5:], res):
            o_ref[...] = v

    sds = jax.ShapeDtypeStruct
    return pl.pallas_call(body, out_shape=[sds((g, p), F32), sds((g, p), F32), sds((g, gc, p), F32), sds((g, gc, p), F32)],
                          name="s5_prep")(a_re, a_im, log_dt, b_re, b_im)


def _s5_prep_bwd(a_re, a_im, log_dt, b_re, b_im, d_ar, d_ai, d_br, d_bi):
    ins = (a_re, a_im, log_dt, b_re, b_im)

    def body(*refs):
        vals = [r[...] for r in refs[:5]]
        cts = tuple(r[...] for r in refs[5:9])
        _, pull = jax.vjp(_s5_disc, *vals)
        for o_ref, v in zip(refs[9:], pull(cts)):
            o_ref[...] = v

    return pl.pallas_call(body, out_shape=[jax.ShapeDtypeStruct(a.shape, F32) for a in ins], name="s5_prep_bwd")(
        *ins, d_ar, d_ai, d_br, d_bi)


def _s5_scan_rows(t, ar, ai, hr_s, hi_s, off, reverse):
    n = ar.shape[1]
    if reverse:
        ai = -ai
    sub = lax.broadcasted_iota(jnp.int32, (8, n), 0)
    cmul = lambda xr, xi, yr, yi: (xr * yr - xi * yi, xr * yi + xi * yr)
    pw = [(ar, ai)]
    for _ in range(7):
        pw.append(cmul(*pw[-1], ar, ai))
    pr = jnp.zeros((8, n), F32)
    pi = jnp.zeros((8, n), F32)
    for r in range(8):
        k = 7 - r if reverse else r
        pr = jnp.where(sub == r, pw[k][0], pr)
        pi = jnp.where(sub == r, pw[k][1], pi)

    def step(g, carry):
        cr, ci = carry
        r0 = pl.multiple_of(off + (t // 8 - 1 - g if reverse else g) * 8, 8)
        br = hr_s[pl.ds(r0, 8), :]
        bi = hi_s[pl.ds(r0, 8), :]
        for s in (1, 2, 4):
            ok = (sub < 8 - s) if reverse else (sub >= s)
            shift = 8 - s if reverse else s
            sr = jnp.where(ok, pltpu.roll(br, shift, 0), 0.0)
            si = jnp.where(ok, pltpu.roll(bi, shift, 0), 0.0)
            mr, mi = cmul(pw[s - 1][0], pw[s - 1][1], sr, si)
            br, bi = br + mr, bi + mi
        mr, mi = cmul(pr, pi, cr, ci)
        br, bi = br + mr, bi + mi
        hr_s[pl.ds(r0, 8), :] = br
        hi_s[pl.ds(r0, 8), :] = bi
        edge = sub == (0 if reverse else 7)
        return (jnp.sum(jnp.where(edge, br, 0.0), axis=0, keepdims=True),
                jnp.sum(jnp.where(edge, bi, 0.0), axis=0, keepdims=True))

    zero = jnp.zeros((1, n), F32)
    lax.fori_loop(0, t // 8, step, (zero, zero))


def _s5_fwd(z, bd_re, bd_im, ab_re, ab_im, cd_re, cd_im, dvec, carry=None):
    t = z.shape[0]
    cr = _tile(t, 256)
    ns = N_STATE // 2

    def body(u_ref, br_ref, bi_ref, ar_ref, ai_ref, cre_ref, cim_ref, d_ref, y_ref, hr_s, hi_s):
        def p1(c, carry):
            t0 = pl.multiple_of(c * cr, cr)
            u = u_ref[pl.ds(t0, cr), :]
            hr_s[pl.ds(t0, cr), :] = _dot(u, br_ref[...], NN)
            hi_s[pl.ds(t0, cr), :] = _dot(u, bi_ref[...], NN)
            return carry

        lax.fori_loop(0, t // cr, p1, 0)
        _s5_scan_rows(t, ar_ref[...], ai_ref[...], hr_s, hi_s, 0, False)

        def p3(c, carry):
            t0 = pl.multiple_of(c * cr, cr)
            y_ref[pl.ds(t0, cr), :] = (_dot(hr_s[pl.ds(t0, cr), :], cre_ref[...], NN)
                                       - _dot(hi_s[pl.ds(t0, cr), :], cim_ref[...], NN)
                                       + d_ref[...] * u_ref[pl.ds(t0, cr), :])
            return carry

        lax.fori_loop(0, t // cr, p3, 0)

    blk = lambda r, c: pl.BlockSpec((None, r, c), lambda b: (b, 0, 0))
    return _call(
        body, (2,),
        [pl.BlockSpec((t, LANES), lambda b: (0, Z_U // LANES + b)), blk(LANES, ns), blk(LANES, ns), blk(1, ns),
         blk(1, ns), blk(ns, LANES), blk(ns, LANES), pl.BlockSpec((1, LANES), lambda b: (0, b))],
        [pl.BlockSpec((t, LANES), lambda b: (0, b))], [jax.ShapeDtypeStruct((t, D_C), F32)],
        [pltpu.VMEM((t, ns), F32), pltpu.VMEM((t, ns), F32)], ("arbitrary",), "s5_fwd",
        [z, bd_re, bd_im, ab_re, ab_im, cd_re, cd_im, dvec], carry)


def _s5_bwd(z, dy, bd_re, bd_im, ab_re, ab_im, cd_re, cd_im, dvec):
    t = z.shape[0]
    cr = _tile(t, 256)
    ns = N_STATE // 2

    def body(u_ref, dy_ref, br_ref, bi_ref, ar_ref, ai_ref, cre_ref, cim_ref, d_ref,
             du_ref, dbr_ref, dbi_ref, dar_ref, dai_ref, dcre_ref, dcim_ref, dd_ref, hr_s, hi_s, gr_s, gi_s):
        zero8 = jnp.zeros((8, ns), F32)
        hr_s[pl.ds(0, 8), :] = zero8
        hi_s[pl.ds(0, 8), :] = zero8
        for ref in (dbr_ref, dbi_ref, dar_ref, dai_ref, dcre_ref, dcim_ref, dd_ref):
            ref[...] = jnp.zeros(ref.shape, F32)

        def p1(c, carry):
            t0 = pl.multiple_of(c * cr, cr)
            u = u_ref[pl.ds(t0, cr), :]
            hr_s[pl.ds(t0 + 8, cr), :] = _dot(u, br_ref[...], NN)
            hi_s[pl.ds(t0 + 8, cr), :] = _dot(u, bi_ref[...], NN)
            return carry

        lax.fori_loop(0, t // cr, p1, 0)
        _s5_scan_rows(t, ar_ref[...], ai_ref[...], hr_s, hi_s, 8, False)

        def p3(c, carry):
            t0 = pl.multiple_of(c * cr, cr)
            dyv = dy_ref[pl.ds(t0, cr), :]
            u = u_ref[pl.ds(t0, cr), :]
            gr_s[pl.ds(t0, cr), :] = _dot(dyv, cre_ref[...], NT)
            gi_s[pl.ds(t0, cr), :] = -_dot(dyv, cim_ref[...], NT)
            dcre_ref[...] += _dot(hr_s[pl.ds(t0 + 8, cr), :], dyv, TN)
            dcim_ref[...] -= _dot(hi_s[pl.ds(t0 + 8, cr), :], dyv, TN)
            dd_ref[...] += jnp.sum(dyv * u, axis=0, keepdims=True)
            du_ref[pl.ds(t0, cr), :] = dyv * d_ref[...]
            return carry

        lax.fori_loop(0, t // cr, p3, 0)
        _s5_scan_rows(t, ar_ref[...], ai_ref[...], gr_s, gi_s, 0, True)

        def p5(c, carry):
            t0 = pl.multiple_of(c * cr, cr)
            u = u_ref[pl.ds(t0, cr), :]
            gr = gr_s[pl.ds(t0, cr), :]
            gi = gi_s[pl.ds(t0, cr), :]
            dbr_ref[...] += _dot(u, gr, TN)
            dbi_ref[...] += _dot(u, gi, TN)
            du_ref[pl.ds(t0, cr), :] += _dot(gr, br_ref[...], NT) + _dot(gi, bi_ref[...], NT)
            hpr = pltpu.roll(hr_s[pl.ds(t0, cr + 8), :], 1, 0)[8:, :]
            hpi = pltpu.roll(hi_s[pl.ds(t0, cr + 8), :], 1, 0)[8:, :]
            dar_ref[...] += jnp.sum(gr * hpr + gi * hpi, axis=0, keepdims=True)
            dai_ref[...] += jnp.sum(gi * hpr - gr * hpi, axis=0, keepdims=True)
            return carry

        lax.fori_loop(0, t // cr, p5, 0)

    blk = lambda r, c: pl.BlockSpec((None, r, c), lambda b: (b, 0, 0))
    ucol = pl.BlockSpec((t, LANES), lambda b: (0, Z_U // LANES + b))
    ycol = pl.BlockSpec((t, LANES), lambda b: (0, b))
    dsp = pl.BlockSpec((1, LANES), lambda b: (0, b))
    sds = jax.ShapeDtypeStruct
    return pl.pallas_call(
        body, grid=(2,),
        in_specs=[ucol, ycol, blk(LANES, ns), blk(LANES, ns), blk(1, ns), blk(1, ns), blk(ns, LANES), blk(ns, LANES), dsp],
        out_specs=[ycol, blk(LANES, ns), blk(LANES, ns), blk(1, ns), blk(1, ns), blk(ns, LANES), blk(ns, LANES), dsp],
        out_shape=[sds((t, D_C), F32), sds((2, LANES, ns), F32), sds((2, LANES, ns), F32), sds((2, 1, ns), F32),
                   sds((2, 1, ns), F32), sds((2, ns, LANES), F32), sds((2, ns, LANES), F32), sds((1, D_C), F32)],
        scratch_shapes=[pltpu.VMEM((t + 8, ns), F32), pltpu.VMEM((t + 8, ns), F32), pltpu.VMEM((t, ns), F32),
                        pltpu.VMEM((t, ns), F32)],
        compiler_params=pltpu.CompilerParams(dimension_semantics=("arbitrary",)), name="s5_bwd")(
            z, dy, bd_re, bd_im, ab_re, ab_im, cd_re, cd_im, dvec)


def _mix_out(out_a, out_b, yc, x1, ga, gb, gc, wglu, wout, ln_g, ln_b):
    yg = _gelu(yc)
    out_c = yg * _sig(_bdot(yg, wglu))
    o = jnp.concatenate([_rms(out_a, ga), _rms(out_b, gb), _rms(out_c, gc)], axis=-1)
    return (_ln(ALPHA * x1 + _bdot(o, wout), ln_g, ln_b),)


def _ln_only(pre, g, b):
    return (_ln(pre, g, b),)


def _loss_head(y, target, tm):
    t, d = y.shape

    def body(y_ref, t_ref, dy_ref, l_ref):
        i = pl.program_id(0)
        e = y_ref[...] - t_ref[...]
        dy_ref[...] = e * (1.0 / d)
        part = 0.5 * jnp.sum(jnp.sum(e * e, axis=-1, keepdims=True) * (1.0 / d), axis=0, keepdims=True)
        row = jnp.where(lax.broadcasted_iota(jnp.int32, (1, LANES), 1) == 0, part, 0.0)

        @pl.when(i == 0)
        def _():
            l_ref[...] = row

        @pl.when(i > 0)
        def _():
            l_ref[...] += row

    spec = pl.BlockSpec((tm, d), lambda i: (i, 0))
    return pl.pallas_call(
        body, grid=(t // tm,), in_specs=[spec, spec], out_specs=[spec, pl.BlockSpec((1, LANES), lambda i: (0, 0))],
        out_shape=[jax.ShapeDtypeStruct((t, d), F32), jax.ShapeDtypeStruct((1, LANES), F32)],
        compiler_params=pltpu.CompilerParams(dimension_semantics=("arbitrary",)), name="loss_head")(y, target)


def _adamw(w, g, m, v, name):
    r, c = w.shape
    tr = r
    for cand in (512, 256, 352, 128):
        if r % cand == 0:
            tr = cand
            break

    def body(w_ref, g_ref, m_ref, v_ref, d_ref, nm_ref, nv_ref):
        gv = g_ref[...]
        mn = ADAM_B1 * m_ref[...] + (1.0 - ADAM_B1) * gv
        vn = ADAM_B2 * v_ref[...] + (1.0 - ADAM_B2) * (gv * gv)
        m_hat = mn / (1.0 - ADAM_B1 ** ADAM_STEP)
        v_hat = vn / (1.0 - ADAM_B2 ** ADAM_STEP)
        d_ref[...] = -ADAM_LR * (m_hat / (jnp.sqrt(v_hat) + ADAM_EPS) + ADAM_WD * w_ref[...])
        nm_ref[...] = mn
        nv_ref[...] = vn

    spec = pl.BlockSpec((tr, c), lambda i: (i, 0))
    return pl.pallas_call(
        body, grid=(r // tr,), in_specs=[spec] * 4, out_specs=[spec] * 3,
        out_shape=[jax.ShapeDtypeStruct((r, c), F32)] * 3,
        compiler_params=pltpu.CompilerParams(dimension_semantics=("parallel",)), name=name)(w, g, m, v)


def _row_tile(r):
    for cand in (512, 448, 352, 256, 128):
        if r % cand == 0:
            return cand
    return r


def _pair_add(a, b, idx, name, out_dtype):
    _, r, w = a.shape
    tr = _row_tile(r)

    def body(i_ref, a_ref, b_ref, o_ref):
        o_ref[...] = (a_ref[...].astype(F32) + b_ref[...].astype(F32)).astype(o_ref.dtype)

    grid_spec = pltpu.PrefetchScalarGridSpec(
        num_scalar_prefetch=1, grid=(4, r // tr),
        in_specs=[pl.BlockSpec((None, tr, w), lambda q, i, s: (2 * q + s[0], i, 0)),
                  pl.BlockSpec((None, tr, w), lambda q, i, s: (q, i, 0))],
        out_specs=pl.BlockSpec((None, tr, w), lambda q, i, s: (q, i, 0)))
    return pl.pallas_call(body, grid_spec=grid_spec, out_shape=jax.ShapeDtypeStruct((4, r, w), out_dtype),
                          compiler_params=pltpu.CompilerParams(dimension_semantics=("parallel", "parallel")), name=name)(
                              idx, a, b)


def _quad_add(p, rb, idx, name):
    _, r, w = p.shape
    tr = _row_tile(r)

    def body(i_ref, p_ref, r0, r1, r2, o_ref):
        o_ref[...] = ((p_ref[...].astype(F32) + r0[...].astype(F32)) + r1[...].astype(F32)) + r2[...].astype(F32)

    grid_spec = pltpu.PrefetchScalarGridSpec(
        num_scalar_prefetch=1, grid=(r // tr,),
        in_specs=[pl.BlockSpec((None, tr, w), lambda i, s: (s[0], i, 0))]
        + [pl.BlockSpec((None, tr, w), functools.partial(lambda i, s, k: (k, i, 0), k=k)) for k in range(3)],
        out_specs=pl.BlockSpec((tr, w), lambda i, s: (i, 0)))
    return pl.pallas_call(body, grid_spec=grid_spec, out_shape=jax.ShapeDtypeStruct((r, w), F32),
                          compiler_params=pltpu.CompilerParams(dimension_semantics=("parallel",)), name=name)(
                              idx, p, rb, rb, rb)


def _all_reduce_small(buf):
    r, w = buf.shape

    def plan(x, y, c, ins, outs):
        dst = outs[0].at[4 * x + 2 * y + c]
        res = [(ins[0], dst, None)]
        for rel in range(1, 8):
            res.append((ins[0], dst, (x ^ (rel >> 2), y ^ ((rel >> 1) & 1), c ^ (rel & 1))))
        return res

    (allb,) = _run(_Carry([buf], [jax.ShapeDtypeStruct((8, r, w), buf.dtype)], {}, plan, 8), "ar_small")

    def body(a_ref, o_ref):
        s = a_ref[0]
        for k in range(1, 8):
            s = s + a_ref[k]
        o_ref[...] = s

    return pl.pallas_call(body, out_shape=jax.ShapeDtypeStruct((r, w), F32), name="ar_small_sum")(allb)


def _pad_rows(a, rows):
    return jnp.pad(a, ((0, rows - a.shape[0]), (0, 0)))


def _pack_small(arrs):
    rows, tail = [], []
    for a in arrs:
        if not tail and a.size % LANES == 0:
            rows.append(a.reshape(-1, LANES))
        else:
            tail.append(a.reshape(-1))
    n_rows = sum(r.shape[0] for r in rows)
    n_tail = sum(int(v.size) for v in tail)
    tail_rows = -(-n_tail // LANES)
    total_rows = n_rows + tail_rows + (-(n_rows + tail_rows)) % 8
    if tail:
        tail.append(jnp.zeros((tail_rows * LANES - n_tail,), F32))
        rows.append(jnp.concatenate(tail).reshape(tail_rows, LANES))
    if total_rows > n_rows + tail_rows:
        rows.append(jnp.zeros((total_rows - n_rows - tail_rows, LANES), F32))
    return jnp.concatenate(rows, axis=0)


def _unpack_small(buf, shapes):
    out, off = [], 0
    flat = buf.reshape(-1)
    for s in shapes:
        n = int(np.prod(s))
        out.append(flat[off:off + n].reshape(s))
        off += n
    return out


def _block_diag(blocks, nb):
    m, r, c = blocks.shape
    n = m // nb
    eye = jnp.eye(n, dtype=blocks.dtype)
    return (blocks.reshape(nb, n, r, 1, c) * eye[None, :, None, :, None]).reshape(nb, n * r, n * c)


def _diag_blocks(dense, n):
    nb = dense.shape[0]
    r, c = dense.shape[1] // n, dense.shape[2] // n
    eye = jnp.eye(n, dtype=dense.dtype)
    return jnp.sum(dense.reshape(nb, n, r, n, c) * eye[None, :, None, :, None], axis=3).reshape(nb * n, r, c)


def kernel(x, ffn1_w_gate, ffn1_w_up, ffn1_w_down, ln1_g, ln1_b, w_in, conv_w, conv_b, rg_w_a, rg_b_a, rg_w_x, rg_b_x, rg_lambda, fox_b_f, s5_a_re, s5_a_im, s5_log_dt, s5_b_re, s5_b_im, s5_c_re, s5_c_im, s5_d, s5_w_glu, mix_norm_g, w_out, ln2_g, ln2_b, ffn2_w_gate, ffn2_w_up, ffn2_w_down, ln3_g, ln3_b, loss_target, m_ffn1_w_gate, m_ffn1_w_up, m_ffn1_w_down, m_ln1_g, m_ln1_b, m_w_in, m_conv_w, m_conv_b, m_rg_w_a, m_rg_b_a, m_rg_w_x, m_rg_b_x, m_rg_lambda, m_fox_b_f, m_s5_a_re, m_s5_a_im, m_s5_log_dt, m_s5_b_re, m_s5_b_im, m_s5_c_re, m_s5_c_im, m_s5_d, m_s5_w_glu, m_mix_norm_g, m_w_out, m_ln2_g, m_ln2_b, m_ffn2_w_gate, m_ffn2_w_up, m_ffn2_w_down, m_ln3_g, m_ln3_b, v_ffn1_w_gate, v_ffn1_w_up, v_ffn1_w_down, v_ln1_g, v_ln1_b, v_w_in, v_conv_w, v_conv_b, v_rg_w_a, v_rg_b_a, v_rg_w_x, v_rg_b_x, v_rg_lambda, v_fox_b_f, v_s5_a_re, v_s5_a_im, v_s5_log_dt, v_s5_b_re, v_s5_b_im, v_s5_c_re, v_s5_c_im, v_s5_d, v_s5_w_glu, v_mix_norm_g, v_w_out, v_ln2_g, v_ln2_b, v_ffn2_w_gate, v_ffn2_w_up, v_ffn2_w_down, v_ln3_g, v_ln3_b):
    a = dict(locals())
    w = {n: a[n] for n in WEIGHTS}
    t, d = x.shape[1], x.shape[2]
    f = ffn1_w_down.shape[1] * 8
    fs, ds = f // 8, d // 8
    mx, my, mc = lax.axis_index("x"), lax.axis_index("y"), lax.axis_index("c")
    me = 4 * mx + 2 * my + mc
    tm = _tile(t, 512)
    tf = f // 2
    win_rows = ds * Z_W // d

    FFN1, MIXW, FFN2 = ['g1', 'u1', 'd1'], ['win', 'wout', 'glu', 'conv'], ['g2', 'u2', 'd2']
    glu_rows = D_C * D_C // (8 * d)

    def shard_segs(l):
        wi = w['w_in'][l]
        win_p = jnp.concatenate([wi[:, :Z_F + N_HEADS], jnp.zeros((ds, Z_U - Z_F - N_HEADS), F32), wi[:, Z_F + N_HEADS:]], axis=1)
        conv_bits = lax.bitcast_convert_type(w['conv_w'][l], BF16).reshape(1, -1)
        segs = dict(g1=w['ffn1_w_gate'][l].T, u1=w['ffn1_w_up'][l].T, d1=w['ffn1_w_down'][l],
                    g2=w['ffn2_w_gate'][l].T, u2=w['ffn2_w_up'][l].T, d2=w['ffn2_w_down'][l],
                    win=win_p.reshape(win_rows, d), wout=w['w_out'][l], glu=_pad_rows(w['s5_w_glu'][l].reshape(-1, d), 16))
        segs = {k: v.astype(BF16) for k, v in segs.items()}
        segs['conv'] = _pad_rows(jnp.pad(conv_bits, ((0, 0), (0, d - conv_bits.shape[1]))), 16)
        return segs

    shards = [shard_segs(l) for l in range(DEPTH)]
    wts = {}

    def cat(keys):
        return jnp.concatenate([shards[l][k] for l, k in keys], axis=0)

    def split(g, keys):
        off = 0
        for l, k in keys:
            r = shards[l][k].shape[0]
            wts[(l, k)] = g[:, off:off + r]
            off += r

    grp_a = [(0, k) for k in FFN1]
    grp_b = [(0, k) for k in MIXW]
    grp_c = [(0, k) for k in FFN2]
    grp_d = [(1, k) for k in FFN1 + MIXW]
    grp_e = [(1, k) for k in FFN2]
    (g_a,) = _run(_ag_chips(cat(grp_a)), "ag_chips")
    (g_a,) = _run(_ag_sibling(g_a), "ag_sibling")
    split(g_a, grp_a)

    xs = x[0]
    saved = []
    cur = xs
    for l in range(DEPTH):
        row = lambda n: w[n][l].reshape(1, -1)
        ffn = lambda keys: tuple(wts[(l, k)].reshape(f, d) for k in keys)
        wa = _block_diag(w['rg_w_a'][l], 3)
        wx = _block_diag(w['rg_w_x'][l], 3)
        bf = jnp.pad(row('fox_b_f'), ((0, 0), (0, LANES - N_HEADS)))
        s5p = (w['s5_a_re'][l], w['s5_a_im'][l], w['s5_log_dt'][l].reshape(-1, 1),
               w['s5_b_re'][l].transpose(0, 2, 1), w['s5_b_im'][l].transpose(0, 2, 1))
        ab_re, ab_im, bb_re, bb_im = _s5_prep(*s5p)
        bd_re, bd_im = _block_diag(bb_re, 2), _block_diag(bb_im, 2)
        cd_re = _block_diag(w['s5_c_re'][l].transpose(0, 2, 1), 2)
        cd_im = _block_diag(w['s5_c_im'][l].transpose(0, 2, 1), 2)
        abr, abi = ab_re.reshape(2, 1, N_STATE // 2), ab_im.reshape(2, 1, N_STATE // 2)
        gm = row('mix_norm_g')
        ga, gb, gc = gm[:, :D_A], gm[:, D_A:D_A + D_B], gm[:, D_A + D_B:]

        x0 = cur
        ffn1 = ffn(FFN1)
        (x1, pre1, gs1, us1), cres = _ffn_fwd(x0, *ffn1, row('ln1_g'), row('ln1_b'), tm, tf,
                                              carry=_ag_chips(cat(grp_b if l == 0 else grp_e)))
        if l == 0:
            (g_b,) = _run(_ag_sibling(cres[0]), "ag_sibling")
            split(g_b, grp_b)
        else:
            g_e = cres[0]
        win = wts[(l, 'win')].reshape(d, Z_W)
        wout = wts[(l, 'wout')].reshape(d, d).astype(F32)
        wglu = wts[(l, 'glu')][:, :glu_rows].reshape(D_C, D_C).astype(F32)
        conv_full = lax.bitcast_convert_type(
            wts[(l, 'conv')][:, 0, :2 * CONV_WIDTH * D_A // 8].reshape(8, CONV_WIDTH, D_A // 8, 2), F32)
        conv_full = conv_full.transpose(1, 0, 2).reshape(CONV_WIDTH, D_A)
        z = _mm(x1, win, 'nn', F32, tm, Z_W, d, "mix_in")
        out_a, h_a = _rg_fwd(z, conv_full, row('conv_b'), wa, row('rg_b_a'), wx, row('rg_b_x'), row('rg_lambda'))
        cs = _fgate_fwd(z, bf)
        crow = cs[:, :N_HEADS].T.reshape(N_HEADS, 1, t)
        (out_b, lse), cres = _attn_fwd(z, crow, carry=_ag_chips(cat(grp_c)) if l == 0 else _ag_sibling(g_e))
        if l == 0:
            (yc,), cres = _s5_fwd(z, bd_re, bd_im, abr, abi, cd_re, cd_im, row('s5_d'), carry=_ag_sibling(cres[0]))
            split(cres[0], grp_c)
        else:
            split(cres[0], grp_e)
            (yc,), _ = _s5_fwd(z, bd_re, bd_im, abr, abi, cd_re, cd_im, row('s5_d'))
        mix_params = [ga, gb, gc, wglu, wout, row('ln2_g'), row('ln2_b')]
        (x2,) = _rowwise(_mix_out, [out_a, out_b, yc, x1], mix_params, [(d, F32)], _tile(t, 256), "mix_out")
        ffn2 = ffn(FFN2)
        (x3, pre3, gs2, us2), cres = _ffn_fwd(x2, *ffn2, row('ln3_g'), row('ln3_b'), tm, tf,
                                              carry=_ag_chips(cat(grp_d)) if l == 0 else None)
        if l == 0:
            (g_d,) = _run(_ag_sibling(cres[0]), "ag_sibling")
            split(g_d, grp_d)
        saved.append(dict(x0=x0, x1=x1, pre1=pre1, gs1=gs1, us1=us1, z=z, out_a=out_a, h_a=h_a, crow=crow,
                          out_b=out_b, lse=lse, yc=yc, x2=x2, pre3=pre3, gs2=gs2, us2=us2, mix_params=mix_params,
                          ffn1=ffn1, ffn2=ffn2, win=win, conv_full=conv_full, wa=wa, wx=wx, bf=bf, s5p=s5p,
                          s5m=(bd_re, bd_im, abr, abi, cd_re, cd_im)))
        cur = x3

    dy, loss_row = _loss_head(cur, loss_target[0], tm)
    loss = lax.psum(loss_row[0, 0], ("x", "y", "c"))

    assert DEPTH == 2
    small_grads = {}
    c_idx = jnp.reshape(mc, (1,)).astype(jnp.int32)
    chip_idx = jnp.reshape(2 * mx + my, (1,)).astype(jnp.int32)

    def blocks(arrs):
        return jnp.concatenate([v.astype(BF16).reshape(8, -1, d) for v in arrs], axis=1)

    def mixer_blocks(dwin, dwout, dwglu):
        glu = jnp.pad(dwglu.astype(BF16).reshape(8, glu_rows, d), ((0, 0), (0, 32 - glu_rows), (0, 0)))
        return jnp.concatenate([dwin.reshape(8, win_rows, d), dwout.astype(BF16).reshape(8, ds, d), glu], axis=1)

    def pair(full, ra):
        return _pair_add(full, ra, c_idx, "rs_add_sibling", BF16)

    for l in reversed(range(DEPTH)):
        s = saved[l]
        row = lambda n: w[n][l].reshape(1, -1)
        wg_tiles = dict(tm=tf, tn=d, tk=_tile(t, 1024))
        first = l == 0

        def ffn_back(dyv, pre, gs, us, wts3, xin, ln_g, ln_b, carry):
            (dpre,), (dlg, dlb), _ = _rowwise_vjp(_ln_only, [pre], [ln_g, ln_b], [dyv], tm, "ln_bwd")
            (dx, dg, du, hh), cres = _ffn_bwd(dpre, gs, us, *wts3, tm, tf, carry=carry)
            dwg = _mm(dg, xin, 'tn', BF16, name="ffn_dw_gate", **wg_tiles)
            dwu = _mm(du, xin, 'tn', BF16, name="ffn_dw_up", **wg_tiles)
            dwd = _mm(hh, dpre, 'tn', BF16, name="ffn_dw_down", **wg_tiles)
            return dx, dwg, dwu, dwd, dlg, dlb, cres

        dx2, dwg2, dwu2, dwd2, dl3g, dl3b, cres = ffn_back(dy, s['pre3'], s['gs2'], s['us2'], s['ffn2'], s['x2'], row('ln3_g'),
                                                           row('ln3_b'), _rs_sibling(full_l1) if first else None)
        if first:
            part_l1 = pair(full_l1, cres[0])
            full_c2 = blocks([dwg2, dwu2, dwd2])
        (d_oa, d_ob, d_yc, d_x1), (dga, dgb, dgc, dwglu, dwout, dl2g, dl2b), cres = _rowwise_vjp(
            _mix_out, [s['out_a'], s['out_b'], s['yc'], s['x1']], s['mix_params'], [dx2], _tile(t, 256), "mix_out_bwd",
            carry=_rs_sibling(full_c2) if first else None)
        if first:
            part_c2 = pair(full_c2, cres[0])
        (d_ax, d_ag, dcw, dcb, dwa, dba, dwx, dbx, dlam) = _rg_bwd(
            s['z'], s['h_a'], d_oa, s['conv_full'], row('conv_b'), s['wa'], row('rg_b_a'), s['wx'], row('rg_b_x'), row('rg_lambda'))
        (dq, dk, dv, dcrow), cres = _attn_bwd(s['z'], s['crow'], s['lse'], d_ob,
                                              carry=_join(_rs_chips(part_l1), _rs_chips(part_c2)) if first else None)
        if first:
            rb_l1, rb_c2 = cres
        dc_pad = jnp.pad(dcrow.reshape(N_HEADS, t).T, ((0, 0), (0, LANES - N_HEADS)))
        dzf, dbf = _fgate_bwd(s['z'], s['bf'], dc_pad)
        du_c, dbd_re, dbd_im, dabr, dabi, dcd_re, dcd_im, dd = _s5_bwd(s['z'], d_yc, *s['s5m'], row('s5_d'))
        dz = jnp.concatenate([d_ax, d_ag, dq, dk, dv, dzf, du_c], axis=1)
        dx1 = _mm(dz, s['win'], 'nt', F32, tm, d, Z_W, "mix_in_dx", add=d_x1)
        dwin = _mm(s['x1'], dz, 'tn', BF16, d, Z_W, tm, "mix_in_dw")
        if first:
            full_m = mixer_blocks(dwin, dwout, dwglu)
            (ra_m,) = _run(_rs_sibling(full_m), "rs_sibling")
            part_m = pair(full_m, ra_m)
        dx0, dwg1, dwu1, dwd1, dl1g, dl1b, cres = ffn_back(dx1, s['pre1'], s['gs1'], s['us1'], s['ffn1'], s['x0'], row('ln1_g'),
                                                           row('ln1_b'), _rs_chips(part_m) if first else None)
        if first:
            rb_m = cres[0]
            full_c1 = blocks([dwg1, dwu1, dwd1])
            (ra_c1,) = _run(_rs_sibling(full_c1), "rs_sibling")
            part_c1 = pair(full_c1, ra_c1)
            (rb_c1,) = _run(_rs_chips(part_c1), "rs_chips")
        else:
            full_l1 = jnp.concatenate([blocks([dwg1, dwu1, dwd1, dwg2, dwu2, dwd2]), mixer_blocks(dwin, dwout, dwglu)], axis=1)
        dy = dx0

        dbb_re, dbb_im = _diag_blocks(dbd_re, N_GROUPS // 2), _diag_blocks(dbd_im, N_GROUPS // 2)
        dcm_re, dcm_im = _diag_blocks(dcd_re, N_GROUPS // 2), _diag_blocks(dcd_im, N_GROUPS // 2)
        da_re, da_im, dlog_dt, db_re, db_im = _s5_prep_bwd(*s['s5p'], dabr.reshape(N_GROUPS, C_STATE), dabi.reshape(N_GROUPS, C_STATE), dbb_re, dbb_im)

        sg = dict(ln1_g=dl1g, ln1_b=dl1b, conv_w=dcw, conv_b=dcb, rg_w_a=_diag_blocks(dwa, 2), rg_b_a=dba,
                  rg_w_x=_diag_blocks(dwx, 2), rg_b_x=dbx, rg_lambda=dlam, fox_b_f=dbf[:, :N_HEADS],
                  s5_a_re=da_re, s5_a_im=da_im, s5_log_dt=dlog_dt, s5_b_re=db_re.transpose(0, 2, 1), s5_b_im=db_im.transpose(0, 2, 1),
                  s5_c_re=dcm_re.transpose(0, 2, 1), s5_c_im=dcm_im.transpose(0, 2, 1), s5_d=dd,
                  mix_norm_g=jnp.concatenate([dga, dgb, dgc], axis=1), ln2_g=dl2g, ln2_b=dl2b, ln3_g=dl3g, ln3_b=dl3b)
        small_grads[l] = sg

    grad_x = dy.reshape(x.shape)
    quad = lambda part, rb: _quad_add(part, rb, chip_idx, "rs_add_chips")
    own = {}

    def take(rows_f32, keys, l):
        off = 0
        for k, r in keys:
            own[(l, k)] = rows_f32[off:off + r]
            off += r

    ffn_keys = lambda names: [(k, fs) for k in names]
    mix_keys = [('win', win_rows), ('wout', ds), ('glu', glu_rows)]
    take(quad(part_l1, rb_l1), ffn_keys(FFN1 + FFN2) + mix_keys, 1)
    take(quad(part_c2, rb_c2), ffn_keys(FFN2), 0)
    take(quad(part_m, rb_m), mix_keys, 0)
    take(quad(part_c1, rb_c1), ffn_keys(FFN1), 0)

    grads = {}
    for k, n in zip(FFN1 + FFN2, ['ffn1_w_gate', 'ffn1_w_up', 'ffn1_w_down', 'ffn2_w_gate', 'ffn2_w_up', 'ffn2_w_down']):
        grads[n] = jnp.stack([own[(l, k)].T if 'down' not in n else own[(l, k)] for l in range(DEPTH)])
    gwin = jnp.stack([own[(l, 'win')].reshape(ds, Z_W) for l in range(DEPTH)])
    grads['w_in'] = jnp.concatenate([gwin[:, :, :Z_F + N_HEADS], gwin[:, :, Z_U:]], axis=2)
    grads['w_out'] = jnp.stack([own[(l, 'wout')] for l in range(DEPTH)])
    grads['s5_w_glu'] = jnp.stack([own[(l, 'glu')].reshape(D_C // 8, D_C) for l in range(DEPTH)])

    small_names = ['conv_w'] + SMALL
    small_shapes = [((DEPTH, CONV_WIDTH, D_A) if n == 'conv_w' else w[n].shape) for n in small_names]
    conv_zero = jnp.zeros((DEPTH, CONV_WIDTH, D_A), F32)
    summed = _all_reduce_small(_pack_small([small_grads[l][n] for n in small_names for l in range(DEPTH)]))
    for n, g in zip(small_names, _unpack_small(summed, small_shapes)):
        grads[n] = g
    grads['conv_w'] = lax.dynamic_slice_in_dim(grads['conv_w'], me * (D_A // 8), D_A // 8, axis=2)

    delta, new_m, new_v = {}, {}, {}
    for n in BIG + ['conv_w']:
        sh = w[n].shape
        two = lambda v: v.reshape(-1, sh[-1])
        dl, nm, nv = _adamw(two(w[n]), two(grads[n]), two(a['m_' + n]), two(a['v_' + n]), "adamw_" + n)
        delta[n], new_m[n], new_v[n] = dl.reshape(sh), nm.reshape(sh), nv.reshape(sh)
    dl, nm, nv = _adamw(_pack_small([conv_zero] + [w[n] for n in SMALL]), summed,
                        _pack_small([conv_zero] + [a['m_' + n] for n in SMALL]),
                        _pack_small([conv_zero] + [a['v_' + n] for n in SMALL]), "adamw_small")
    for n, v1, v2, v3 in zip(small_names[1:], _unpack_small(dl, small_shapes)[1:], _unpack_small(nm, small_shapes)[1:],
                             _unpack_small(nv, small_shapes)[1:]):
        delta[n], new_m[n], new_v[n] = v1, v2, v3

    return (loss, grad_x, *[grads[n] for n in WEIGHTS], *[delta[n] for n in WEIGHTS], *[new_m[n] for n in WEIGHTS],
            *[new_v[n] for n in WEIGHTS])
```

```python
import functools
import math

import jax
import jax.numpy as jnp
import numpy as np
from jax import lax
from jax.experimental import pallas as pl
from jax.experimental.pallas import tpu as pltpu

F32 = jnp.float32
BF16 = jnp.bfloat16
MESH = pl.DeviceIdType.MESH

DEPTH = 2
ALPHA = (2 * DEPTH) ** 0.25
LN_EPS = 1e-5
RMS_EPS = 1e-6
RG_C = 8.0
CONV_WIDTH = 4
HEAD_DIM = 64
C_GROUP = 16
C_STATE = 64
D_A = 384
D_B = 384
D_C = 256
N_HEADS = D_B // HEAD_DIM
N_GROUPS = D_C // C_GROUP
N_STATE = N_GROUPS * C_STATE
Z_F = 2 * D_A + 3 * D_B
Z_U = Z_F + 128
Z_W = Z_U + D_C
N_IN = Z_F + N_HEADS + D_C
ADAM_LR, ADAM_B1, ADAM_B2, ADAM_EPS, ADAM_WD, ADAM_STEP = 0.001, 0.9, 0.999, 1e-08, 0.01, 10
LANES = 128
NEG = -1e30

WEIGHTS = ['ffn1_w_gate', 'ffn1_w_up', 'ffn1_w_down', 'ln1_g', 'ln1_b', 'w_in', 'conv_w', 'conv_b', 'rg_w_a', 'rg_b_a',
           'rg_w_x', 'rg_b_x', 'rg_lambda', 'fox_b_f', 's5_a_re', 's5_a_im', 's5_log_dt', 's5_b_re', 's5_b_im', 's5_c_re',
           's5_c_im', 's5_d', 's5_w_glu', 'mix_norm_g', 'w_out', 'ln2_g', 'ln2_b', 'ffn2_w_gate', 'ffn2_w_up', 'ffn2_w_down',
           'ln3_g', 'ln3_b']
BIG = ['ffn1_w_gate', 'ffn1_w_up', 'ffn1_w_down', 'w_in', 's5_w_glu', 'w_out', 'ffn2_w_gate', 'ffn2_w_up', 'ffn2_w_down']
SMALL_TAIL = ['fox_b_f', 's5_log_dt']
SMALL = [n for n in WEIGHTS if n not in BIG and n != 'conv_w' and n not in SMALL_TAIL] + SMALL_TAIL


def _sig(x):
    return 1.0 / (1.0 + jnp.exp(-x))


def _gelu(x):
    return 0.5 * x * (1.0 + jnp.tanh(math.sqrt(2.0 / math.pi) * (x + 0.044715 * (x * x * x))))


def _softplus(x):
    return jnp.maximum(x, 0.0) + jnp.log(1.0 + jnp.exp(jnp.minimum(x, -x)))


def _dot(a, b, dims):
    return lax.dot_general(a.astype(BF16), b.astype(BF16), (dims, ((), ())), preferred_element_type=F32)


NN = ((1,), (0,))
NT = ((1,), (1,))
TN = ((0,), (0,))


@jax.custom_vjp
def _bdot(a, w):
    return _dot(a, w, NN)


def _bdot_fwd(a, w):
    return _dot(a, w, NN), (a, w)


def _bdot_bwd(res, ct):
    a, w = res
    return _dot(ct, w, NT), _dot(a, ct, TN)


_bdot.defvjp(_bdot_fwd, _bdot_bwd)


def _ln(pre, g, b):
    mu = jnp.mean(pre, axis=-1, keepdims=True)
    xc = pre - mu
    var = jnp.mean(xc * xc, axis=-1, keepdims=True)
    return xc * lax.rsqrt(var + LN_EPS) * g + b


def _rms(x, g):
    return x * lax.rsqrt(jnp.mean(x * x, axis=-1, keepdims=True) + RMS_EPS) * g


def _tile(n, want):
    return want if n % want == 0 else n


class _Carry:
    def __init__(self, ins, outs, aliases, plan, n):
        self.ins, self.outs, self.aliases, self.plan, self.n = list(ins), list(outs), dict(aliases), plan, n


def _join(a, b):
    na, ma = len(a.ins), len(a.outs)

    def plan(x, y, c, ins, outs):
        return a.plan(x, y, c, ins[:na], outs[:ma]) + b.plan(x, y, c, ins[na:], outs[ma:])

    aliases = dict(a.aliases)
    aliases.update({na + i: ma + j for i, j in b.aliases.items()})
    return _Carry(a.ins + b.ins, a.outs + b.outs, aliases, plan, a.n + b.n)


def _copies(carry, cins, couts, send, recv):
    x, y, c = lax.axis_index("x"), lax.axis_index("y"), lax.axis_index("c")
    res = []
    for k, (s, d, peer) in enumerate(carry.plan(x, y, c, cins, couts)):
        if peer is None:
            res.append(pltpu.make_async_copy(s, d, send.at[k]))
        else:
            res.append(pltpu.make_async_remote_copy(src_ref=s, dst_ref=d, send_sem=send.at[k], recv_sem=recv.at[k],
                                                    device_id=peer, device_id_type=MESH))
    return res


def _call(body, grid, in_specs, out_specs, out_shape, scratch, semantics, name, args, carry=None):
    n_in, n_out, n_scr = len(in_specs), len(out_specs), len(scratch)
    if carry is None:
        res = pl.pallas_call(body, grid=grid, in_specs=in_specs, out_specs=out_specs, out_shape=out_shape,
                             scratch_shapes=scratch, compiler_params=pltpu.CompilerParams(dimension_semantics=semantics),
                             name=name)(*args)
        return list(res), []
    nci, nco = len(carry.ins), len(carry.outs)

    def wrapped(*refs):
        o0 = n_in + nci
        s0 = o0 + n_out + nco
        cins, couts = refs[n_in:o0], refs[o0 + n_out:s0]
        send, recv = refs[s0 + n_scr:]
        first = functools.reduce(jnp.logical_and, [pl.program_id(k) == 0 for k in range(len(grid))])
        last = functools.reduce(jnp.logical_and, [pl.program_id(k) == grid[k] - 1 for k in range(len(grid))])

        @pl.when(first)
        def _():
            for cp in _copies(carry, cins, couts, send, recv):
                cp.start()

        body(*refs[:n_in], *refs[o0:o0 + n_out], *refs[s0:s0 + n_scr])

        @pl.when(last)
        def _():
            for cp in _copies(carry, cins, couts, send, recv):
                cp.wait()

    hbm = pl.BlockSpec(memory_space=pl.ANY)
    res = pl.pallas_call(
        wrapped, grid=grid, in_specs=list(in_specs) + [hbm] * nci, out_specs=list(out_specs) + [hbm] * nco,
        out_shape=list(out_shape) + carry.outs, scratch_shapes=list(scratch) + [pltpu.SemaphoreType.DMA((carry.n,))] * 2,
        input_output_aliases={n_in + i: n_out + j for i, j in carry.aliases.items()},
        compiler_params=pltpu.CompilerParams(dimension_semantics=("arbitrary",) * len(grid), has_side_effects=True),
        name=name)(*args, *carry.ins)
    return list(res[:n_out]), list(res[n_out:])


def _run(carry, name):
    nci, nco = len(carry.ins), len(carry.outs)

    def body(*refs):
        cps = _copies(carry, refs[:nci], refs[nci:nci + nco], refs[-2], refs[-1])
        for cp in cps:
            cp.start()
        for cp in cps:
            cp.wait()

    hbm = pl.BlockSpec(memory_space=pl.ANY)
    return pl.pallas_call(
        body, in_specs=[hbm] * nci, out_specs=[hbm] * nco, out_shape=carry.outs, input_output_aliases=carry.aliases,
        scratch_shapes=[pltpu.SemaphoreType.DMA((carry.n,))] * 2, compiler_params=pltpu.CompilerParams(has_side_effects=True),
        name=name)(*carry.ins)


def _ag_chips(shard):
    def plan(x, y, c, ins, outs):
        dst = outs[0].at[4 * x + 2 * y + c]
        return [(ins[0], dst, None)] + [(ins[0], dst, (px, py, c)) for px, py in ((1 - x, y), (x, 1 - y), (1 - x, 1 - y))]

    return _Carry([shard], [jax.ShapeDtypeStruct((8,) + shard.shape, shard.dtype)], {}, plan, 4)


def _ag_sibling(g):
    def plan(x, y, c, ins, outs):
        return [(outs[0].at[2 * q + c], outs[0].at[2 * q + c], (x, y, 1 - c)) for q in range(4)]

    return _Carry([g], [jax.ShapeDtypeStruct(g.shape, g.dtype)], {0: 0}, plan, 4)


def _rs_sibling(full):
    def plan(x, y, c, ins, outs):
        return [(ins[0].at[2 * q + (1 - c)], outs[0].at[q], (x, y, 1 - c)) for q in range(4)]

    return _Carry([full], [jax.ShapeDtypeStruct((4,) + full.shape[1:], full.dtype)], {}, plan, 4)


def _rs_chips(part):
    def plan(x, y, c, ins, outs):
        res = []
        for k, (dx, dy) in enumerate(((1, 0), (0, 1), (1, 1))):
            tx, ty = x ^ dx, y ^ dy
            res.append((ins[0].at[2 * tx + ty], outs[0].at[k], (tx, ty, c)))
        return res

    return _Carry([part], [jax.ShapeDtypeStruct((3,) + part.shape[1:], part.dtype)], {}, plan, 3)


def _mm(a, b, dims, out_dtype, tm, tn, tk, name, add=None, carry=None):
    if dims == 'nn':
        (m, k), n = a.shape, b.shape[1]
        a_spec = pl.BlockSpec((tm, tk), lambda i, j, q: (i, q))
        b_spec = pl.BlockSpec((tk, tn), lambda i, j, q: (q, j))
        dn = NN
    elif dims == 'nt':
        (m, k), n = a.shape, b.shape[0]
        a_spec = pl.BlockSpec((tm, tk), lambda i, j, q: (i, q))
        b_spec = pl.BlockSpec((tn, tk), lambda i, j, q: (j, q))
        dn = NT
    else:
        (k, m), n = a.shape, b.shape[1]
        a_spec = pl.BlockSpec((tk, tm), lambda i, j, q: (q, i))
        b_spec = pl.BlockSpec((tk, tn), lambda i, j, q: (q, j))
        dn = TN
    nk = k // tk
    o_spec = pl.BlockSpec((tm, tn), lambda i, j, q: (i, j))

    def body(*refs):
        if add is None:
            a_ref, b_ref, o_ref, acc_ref = refs
        else:
            a_ref, b_ref, add_ref, o_ref, acc_ref = refs
        q = pl.program_id(2)
        part = _dot(a_ref[...], b_ref[...], dn)

        @pl.when(q == 0)
        def _():
            acc_ref[...] = part

        @pl.when(q > 0)
        def _():
            acc_ref[...] += part

        @pl.when(q == nk - 1)
        def _():
            r = acc_ref[...]
            if add is not None:
                r = r + add_ref[...]
            o_ref[...] = r.astype(o_ref.dtype)

    ins = [a, b] + ([] if add is None else [add])
    specs = [a_spec, b_spec] + ([] if add is None else [o_spec])
    (res,), cres = _call(body, (m // tm, n // tn, nk), specs, [o_spec], [jax.ShapeDtypeStruct((m, n), out_dtype)],
                         [pltpu.VMEM((tm, tn), F32)], ("parallel", "parallel", "arbitrary"), name, ins, carry)
    return res if carry is None else (res, cres)


def _rowwise(fn, rows, params, outs, tm, name):
    t = rows[0].shape[0]
    nr, npar = len(rows), len(params)

    def body(*refs):
        r = [x[...] for x in refs[:nr]]
        p = [x[...] for x in refs[nr:nr + npar]]
        res = fn(*r, *p)
        for o_ref, o in zip(refs[nr + npar:], res):
            o_ref[...] = o.astype(o_ref.dtype)

    in_specs = ([pl.BlockSpec((tm, a.shape[1]), lambda i: (i, 0)) for a in rows]
                + [pl.BlockSpec(p.shape, lambda i: (0, 0)) for p in params])
    return pl.pallas_call(
        body, grid=(t // tm,), in_specs=in_specs,
        out_specs=[pl.BlockSpec((tm, c), lambda i: (i, 0)) for c, _ in outs],
        out_shape=[jax.ShapeDtypeStruct((t, c), d) for c, d in outs],
        compiler_params=pltpu.CompilerParams(dimension_semantics=("parallel",)), name=name)(*rows, *params)


def _rowwise_vjp(fn, rows, params, cots, tm, name, carry=None):
    t = rows[0].shape[0]
    nr, npar, nc = len(rows), len(params), len(cots)

    def body(*refs):
        r = [x[...] for x in refs[:nr]]
        p = [x[...] for x in refs[nr:nr + npar]]
        c = [x[...] for x in refs[nr + npar:nr + npar + nc]]
        o_refs = refs[nr + npar + nc:]
        _, pull = jax.vjp(fn, *r, *p)
        grads = pull(tuple(c))
        for o_ref, g in zip(o_refs[:nr], grads[:nr]):
            o_ref[...] = g
        i = pl.program_id(0)

        @pl.when(i == 0)
        def _():
            for o_ref, g in zip(o_refs[nr:], grads[nr:]):
                o_ref[...] = g

        @pl.when(i > 0)
        def _():
            for o_ref, g in zip(o_refs[nr:], grads[nr:]):
                o_ref[...] += g

    row_spec = lambda a: pl.BlockSpec((tm, a.shape[1]), lambda i: (i, 0))
    par_spec = lambda p: pl.BlockSpec(p.shape, lambda i: (0, 0))
    res, cres = _call(
        body, (t // tm,),
        [row_spec(a) for a in rows] + [par_spec(p) for p in params] + [row_spec(a) for a in cots],
        [row_spec(a) for a in rows] + [par_spec(p) for p in params],
        [jax.ShapeDtypeStruct(a.shape, F32) for a in rows] + [jax.ShapeDtypeStruct(p.shape, F32) for p in params],
        [], ("arbitrary",), name, [*rows, *params, *cots], carry)
    return res[:nr], res[nr:], cres


def _ffn_fwd(x, wgt, wut, wd, ln_g, ln_b, tm, tf, carry=None):
    t, d = x.shape
    f = wgt.shape[0]
    nj = f // tf

    def body(x_ref, wg_ref, wu_ref, wd_ref, g_ref, b_ref, y_ref, pre_ref, gs_ref, us_ref, acc_ref):
        j = pl.program_id(1)
        xv = x_ref[...]
        xb = xv.astype(BF16)
        g = _dot(xb, wg_ref[...], NT)
        u = _dot(xb, wu_ref[...], NT)
        gs_ref[...] = g.astype(BF16)
        us_ref[...] = u.astype(BF16)
        part = _dot(g * _sig(g) * u, wd_ref[...], NN)

        @pl.when(j == 0)
        def _():
            acc_ref[...] = part

        @pl.when(j > 0)
        def _():
            acc_ref[...] += part

        @pl.when(j == nj - 1)
        def _():
            pre = ALPHA * xv + 0.5 * acc_ref[...]
            pre_ref[...] = pre
            y_ref[...] = _ln(pre, g_ref[...], b_ref[...])

    w_spec = pl.BlockSpec((tf, d), lambda i, j: (j, 0))
    x_spec = pl.BlockSpec((tm, d), lambda i, j: (i, 0))
    v_spec = pl.BlockSpec((1, d), lambda i, j: (0, 0))
    h_spec = pl.BlockSpec((tm, tf), lambda i, j: (i, j))
    return _call(
        body, (t // tm, nj), [x_spec, w_spec, w_spec, w_spec, v_spec, v_spec], [x_spec, x_spec, h_spec, h_spec],
        [jax.ShapeDtypeStruct((t, d), F32), jax.ShapeDtypeStruct((t, d), F32),
         jax.ShapeDtypeStruct((t, f), BF16), jax.ShapeDtypeStruct((t, f), BF16)],
        [pltpu.VMEM((tm, d), F32)], ("parallel", "arbitrary"), "ffn_fwd", [x, wgt, wut, wd, ln_g, ln_b], carry)


def _ffn_bwd(dpre, gs, us, wgt, wut, wd, tm, tf, carry=None):
    t, d = dpre.shape
    f = wgt.shape[0]
    nj = f // tf

    def body(dp_ref, gs_ref, us_ref, wg_ref, wu_ref, wd_ref, dx_ref, dg_ref, du_ref, hh_ref, acc_ref):
        j = pl.program_id(1)
        dp = dp_ref[...]
        dh = _dot(0.5 * dp, wd_ref[...], NT)
        g = gs_ref[...].astype(F32)
        u = us_ref[...].astype(F32)
        s = _sig(g)
        sl = g * s
        dg = (dh * u * (s * (1.0 + g * (1.0 - s)))).astype(BF16)
        du = (dh * sl).astype(BF16)
        dg_ref[...] = dg
        du_ref[...] = du
        hh_ref[...] = (0.5 * sl * u).astype(BF16)
        part = _dot(dg, wg_ref[...], NN) + _dot(du, wu_ref[...], NN)

        @pl.when(j == 0)
        def _():
            acc_ref[...] = part

        @pl.when(j > 0)
        def _():
            acc_ref[...] += part

        @pl.when(j == nj - 1)
        def _():
            dx_ref[...] = ALPHA * dp + acc_ref[...]

    w_spec = pl.BlockSpec((tf, d), lambda i, j: (j, 0))
    x_spec = pl.BlockSpec((tm, d), lambda i, j: (i, 0))
    h_spec = pl.BlockSpec((tm, tf), lambda i, j: (i, j))
    return _call(
        body, (t // tm, nj), [x_spec, h_spec, h_spec, w_spec, w_spec, w_spec], [x_spec, h_spec, h_spec, h_spec],
        [jax.ShapeDtypeStruct((t, d), F32)] + [jax.ShapeDtypeStruct((t, f), BF16)] * 3,
        [pltpu.VMEM((tm, d), F32)], ("parallel", "arbitrary"), "ffn_bwd", [dpre, gs, us, wgt, wut, wd], carry)


def _scan8(a_ref, b_ref, out_ref, t, reverse=False):
    w = out_ref.shape[-1]
    sub = lax.broadcasted_iota(jnp.int32, (8, w), 0)

    def step(g, carry):
        r0 = pl.multiple_of((t // 8 - 1 - g if reverse else g) * 8, 8)
        bv = b_ref[pl.ds(r0, 8), :]
        av = None if a_ref is None else a_ref[pl.ds(r0, 8), :]
        for s in (1, 2, 4):
            ok = (sub < 8 - s) if reverse else (sub >= s)
            shift = 8 - s if reverse else s
            b_sh = jnp.where(ok, pltpu.roll(bv, shift, 0), 0.0)
            if av is None:
                bv = bv + b_sh
            else:
                bv = av * b_sh + bv
                av = av * jnp.where(ok, pltpu.roll(av, shift, 0), 1.0)
        h = bv + carry if av is None else bv + av * carry
        out_ref[pl.ds(r0, 8), :] = h
        return jnp.sum(jnp.where(sub == (0 if reverse else 7), h, 0.0), axis=0, keepdims=True)

    lax.fori_loop(0, t // 8, step, jnp.zeros((1, w), F32))


def _rg_local(xa, wa, ba, wx, bx, lam):
    r = _sig(_bdot(xa, wa) + ba)
    i = _sig(_bdot(xa, wx) + bx)
    log_a = -RG_C * r * _softplus(-lam)
    a = jnp.exp(log_a)
    mult = jnp.sqrt(-jnp.tanh(log_a) * (a * a + 1.0))
    return a, mult * (i * xa)


def _conv_taps(ext, n):
    return [ext[8:, :]] + [pltpu.roll(ext, s, 0)[8:, :] for s in (1, 2, 3)]


def _rg_fwd(z, cw, cb, wa, ba, wx, bx, lam):
    t = z.shape[0]
    cr = _tile(t, 256)
    nb = D_A // LANES

    def body(ax_ref, ag_ref, cw_ref, cb_ref, wa_ref, ba_ref, wx_ref, bx_ref, lam_ref, out_ref, h_ref, axp, a_s, b_s):
        axp[pl.ds(0, 8), :] = jnp.zeros((8, LANES), F32)
        pltpu.sync_copy(ax_ref, axp.at[pl.ds(8, t)])
        w = [cw_ref[pl.ds(k, 1), :] for k in range(CONV_WIDTH)]

        def chunk(c, carry):
            t0 = pl.multiple_of(c * cr, cr)
            taps = _conv_taps(axp[pl.ds(t0, cr + 8), :], cr)
            xa = cb_ref[...] + w[3] * taps[0] + w[2] * taps[1] + w[1] * taps[2] + w[0] * taps[3]
            a, gated = _rg_local(xa, wa_ref[...], ba_ref[...], wx_ref[...], bx_ref[...], lam_ref[...])
            a_s[pl.ds(t0, cr), :] = a
            b_s[pl.ds(t0, cr), :] = gated
            return carry

        lax.fori_loop(0, t // cr, chunk, 0)

        _scan8(a_s, b_s, h_ref, t)

        def fin(c, carry):
            t0 = pl.multiple_of(c * cr, cr)
            out_ref[pl.ds(t0, cr), :] = _gelu(ag_ref[pl.ds(t0, cr), :]) * h_ref[pl.ds(t0, cr), :]
            return carry

        lax.fori_loop(0, t // cr, fin, 0)

    col = lambda off: pl.BlockSpec((t, LANES), lambda b: (0, off + b))
    vec = pl.BlockSpec((1, LANES), lambda b: (0, b))
    mat = pl.BlockSpec((None, LANES, LANES), lambda b: (b, 0, 0))
    return pl.pallas_call(
        body, grid=(nb,),
        in_specs=[col(0), col(nb), pl.BlockSpec((CONV_WIDTH, LANES), lambda b: (0, b)), vec, mat, vec, mat, vec, vec],
        out_specs=[col(0), col(0)],
        out_shape=[jax.ShapeDtypeStruct((t, D_A), F32), jax.ShapeDtypeStruct((t, D_A), F32)],
        scratch_shapes=[pltpu.VMEM((t + 8, LANES), F32), pltpu.VMEM((t, LANES), F32), pltpu.VMEM((t, LANES), F32)],
        compiler_params=pltpu.CompilerParams(dimension_semantics=("arbitrary",)), name="rglru_fwd")(
            z, z, cw, cb, wa, ba, wx, bx, lam)


def _rg_bwd(z, h, dout, cw, cb, wa, ba, wx, bx, lam):
    t = z.shape[0]
    cr = _tile(t, 256)
    nb = D_A // LANES

    def body(ax_ref, ag_ref, h_ref, do_ref, cw_ref, cb_ref, wa_ref, ba_ref, wx_ref, bx_ref, lam_ref,
             dax_ref, dag_ref, dcw_ref, dcb_ref, dwa_ref, dba_ref, dwx_ref, dbx_ref, dlam_ref,
             axp, hp, xa_s, a_s, g_s, dxa_s, u_s):
        zero8 = jnp.zeros((8, LANES), F32)
        axp[pl.ds(0, 8), :] = zero8
        hp[pl.ds(0, 8), :] = zero8
        dxa_s[pl.ds(t, 8), :] = zero8
        u_s[pl.ds(t, 8), :] = zero8
        pltpu.sync_copy(ax_ref, axp.at[pl.ds(8, t)])
        pltpu.sync_copy(h_ref, hp.at[pl.ds(8, t)])
        w = [cw_ref[pl.ds(k, 1), :] for k in range(CONV_WIDTH)]
        for ref in (dcw_ref, dcb_ref, dwa_ref, dba_ref, dwx_ref, dbx_ref, dlam_ref):
            ref[...] = jnp.zeros(ref.shape, F32)

        def p1(c, carry):
            t0 = pl.multiple_of(c * cr, cr)
            taps = _conv_taps(axp[pl.ds(t0, cr + 8), :], cr)
            xa = cb_ref[...] + w[3] * taps[0] + w[2] * taps[1] + w[1] * taps[2] + w[0] * taps[3]
            a, _ = _rg_local(xa, wa_ref[...], ba_ref[...], wx_ref[...], bx_ref[...], lam_ref[...])
            xa_s[pl.ds(t0, cr), :] = xa
            a_s[pl.ds(t0, cr), :] = a
            ag = ag_ref[pl.ds(t0, cr), :]
            dov = do_ref[pl.ds(t0, cr), :]
            gel, pull = jax.vjp(_gelu, ag)
            g_s[pl.ds(t0, cr), :] = dov * gel
            u_s[pl.ds(t0, cr), :] = a * (dov * gel)
            dag_ref[pl.ds(t0, cr), :] = pull(dov * h_ref[pl.ds(t0, cr), :])[0]
            return carry

        lax.fori_loop(0, t // cr, p1, 0)
        _scan8(a_s, u_s, u_s, t, reverse=True)

        def p3(c, carry):
            t0 = pl.multiple_of(c * cr, cr)
            g = g_s[pl.ds(t0, cr), :] + pltpu.roll(u_s[pl.ds(t0, cr + 8), :], cr + 7, 0)[:cr, :]
            h_prev = pltpu.roll(hp[pl.ds(t0, cr + 8), :], 1, 0)[8:, :]
            _, pull = jax.vjp(_rg_local, xa_s[pl.ds(t0, cr), :], wa_ref[...], ba_ref[...], wx_ref[...], bx_ref[...],
                              lam_ref[...])
            dxa, dwa, dba, dwx, dbx, dlam = pull((g * h_prev, g))
            dxa_s[pl.ds(t0, cr), :] = dxa
            dwa_ref[...] += dwa
            dba_ref[...] += dba
            dwx_ref[...] += dwx
            dbx_ref[...] += dbx
            dlam_ref[...] += dlam
            return carry

        lax.fori_loop(0, t // cr, p3, 0)

        def p4(c, carry):
            t0 = pl.multiple_of(c * cr, cr)
            ext = dxa_s[pl.ds(t0, cr + 8), :]
            n = cr + 8
            ahead = [ext[:cr, :]] + [pltpu.roll(ext, n - s, 0)[:cr, :] for s in (1, 2, 3)]
            dax_ref[pl.ds(t0, cr), :] = w[3] * ahead[0] + w[2] * ahead[1] + w[1] * ahead[2] + w[0] * ahead[3]
            taps = _conv_taps(axp[pl.ds(t0, cr + 8), :], cr)
            dxa = ahead[0]
            for k in range(CONV_WIDTH):
                dcw_ref[pl.ds(k, 1), :] += jnp.sum(dxa * taps[CONV_WIDTH - 1 - k], axis=0, keepdims=True)
            dcb_ref[...] += jnp.sum(dxa, axis=0, keepdims=True)
            return carry

        lax.fori_loop(0, t // cr, p4, 0)

    col = lambda off: pl.BlockSpec((t, LANES), lambda b: (0, off + b))
    vec = pl.BlockSpec((1, LANES), lambda b: (0, b))
    mat = pl.BlockSpec((None, LANES, LANES), lambda b: (b, 0, 0))
    cws = pl.BlockSpec((CONV_WIDTH, LANES), lambda b: (0, b))
    sds = jax.ShapeDtypeStruct
    return pl.pallas_call(
        body, grid=(nb,),
        in_specs=[col(0), col(nb), col(0), col(0), cws, vec, mat, vec, mat, vec, vec],
        out_specs=[col(0), col(0), cws, vec, mat, vec, mat, vec, vec],
        out_shape=[sds((t, D_A), F32), sds((t, D_A), F32), sds((CONV_WIDTH, D_A), F32), sds((1, D_A), F32),
                   sds((nb, LANES, LANES), F32), sds((1, D_A), F32), sds((nb, LANES, LANES), F32), sds((1, D_A), F32),
                   sds((1, D_A), F32)],
        scratch_shapes=[pltpu.VMEM((t + 8, LANES), F32), pltpu.VMEM((t + 8, LANES), F32), pltpu.VMEM((t, LANES), F32),
                        pltpu.VMEM((t, LANES), F32), pltpu.VMEM((t, LANES), F32), pltpu.VMEM((t + 8, LANES), F32),
                        pltpu.VMEM((t + 8, LANES), F32)],
        compiler_params=pltpu.CompilerParams(dimension_semantics=("arbitrary",)), name="rglru_bwd")(
            z, z, h, dout, cw, cb, wa, ba, wx, bx, lam)


def _fgate_fwd(z, bf):
    t = z.shape[0]

    def body(zf_ref, bf_ref, c_ref):
        c_ref[...] = -_softplus(-(zf_ref[...] + bf_ref[...]))
        _scan8(None, c_ref, c_ref, t)

    return pl.pallas_call(
        body, grid=(1,), in_specs=[pl.BlockSpec((t, LANES), lambda i: (0, Z_F // LANES)), pl.BlockSpec((1, LANES), lambda i: (0, 0))],
        out_specs=pl.BlockSpec((t, LANES), lambda i: (0, 0)), out_shape=jax.ShapeDtypeStruct((t, LANES), F32),
        compiler_params=pltpu.CompilerParams(dimension_semantics=("arbitrary",)), name="fgate_fwd")(z, bf)


def _fgate_bwd(z, bf, dc):
    t = z.shape[0]

    def body(zf_ref, bf_ref, dc_ref, dz_ref, db_ref):
        _scan8(None, dc_ref, dz_ref, t, reverse=True)
        dz = dz_ref[...] * _sig(-(zf_ref[...] + bf_ref[...]))
        dz_ref[...] = dz
        db_ref[...] = jnp.sum(dz, axis=0, keepdims=True)

    return pl.pallas_call(
        body, grid=(1,),
        in_specs=[pl.BlockSpec((t, LANES), lambda i: (0, Z_F // LANES)), pl.BlockSpec((1, LANES), lambda i: (0, 0)),
                  pl.BlockSpec((t, LANES), lambda i: (0, 0))],
        out_specs=[pl.BlockSpec((t, LANES), lambda i: (0, 0)), pl.BlockSpec((1, LANES), lambda i: (0, 0))],
        out_shape=[jax.ShapeDtypeStruct((t, LANES), F32), jax.ShapeDtypeStruct((1, LANES), F32)],
        compiler_params=pltpu.CompilerParams(dimension_semantics=("arbitrary",)), name="fgate_bwd")(z, bf, dc)


def _cast_rows(src_ref, dst_ref, t, rows, fn):
    def cp(c, carry):
        r0 = pl.multiple_of(c * rows, rows)
        dst_ref[pl.ds(r0, rows), :] = fn(src_ref[pl.ds(r0, rows), :]).astype(dst_ref.dtype)
        return carry

    lax.fori_loop(0, t // rows, cp, 0)


def _attn_groups(t):
    tq = _tile(t, 256)
    nq = t // tq
    grp = 4 if nq % 4 == 0 else 1
    return tq, nq, grp


def _attn_fwd(z, crow, carry=None):
    t = z.shape[0]
    tq, nq, grp = _attn_groups(t)
    tk = grp * tq
    scale = HEAD_DIM ** -0.5

    def body(q_ref, k_ref, v_ref, cr_ref, o_ref, lse_ref, kb_s, vb_s):
        lane = lax.broadcasted_iota(jnp.int32, (1, LANES), 1)
        hmask = [(lane // HEAD_DIM) == hh for hh in range(2)]
        _cast_rows(k_ref, kb_s, t, tq, lambda v: v)
        _cast_rows(v_ref, vb_s, t, tq, lambda v: v)

        def qblock(g, r):
            q0 = pl.multiple_of((g * grp + r) * tq, tq)
            qv = q_ref[pl.ds(q0, tq), :] * scale
            qa = [jnp.where(hmask[hh], qv, 0.0).astype(BF16) for hh in range(2)]

            def update(st, k0, width, off):
                kb = kb_s[pl.ds(k0, width), :]
                vb = vb_s[pl.ds(k0, width), :]
                new = []
                for hh in range(2):
                    m, l, acc = st[hh]
                    s = _dot(qa[hh], kb, NT) - cr_ref[hh, :, pl.ds(k0, width)]
                    if off is not None:
                        keep = (lax.broadcasted_iota(jnp.int32, (tq, width), 0) + off
                                >= lax.broadcasted_iota(jnp.int32, (tq, width), 1))
                        s = jnp.where(keep, s, NEG)
                    m_new = jnp.maximum(m, jnp.max(s, axis=-1, keepdims=True))
                    p = jnp.exp(s - m_new)
                    corr = jnp.exp(m - m_new)
                    new.append((m_new, corr * l + jnp.sum(p, axis=-1, keepdims=True), corr * acc + _dot(p, vb, NN)))
                return tuple(new)

            one = (jnp.full((tq, 1), NEG, F32), jnp.zeros((tq, 1), F32), jnp.zeros((tq, LANES), F32))
            st = lax.fori_loop(0, g, lambda j, st: update(st, pl.multiple_of(j * tk, tk), tk, None), (one, one))
            st = update(st, pl.multiple_of(g * tk, tk), (r + 1) * tq, r * tq)
            o_ref[pl.ds(q0, tq), :] = jnp.where(hmask[0], st[0][2] / st[0][1], st[1][2] / st[1][1])
            for hh in range(2):
                lse_ref[hh, pl.ds(q0, tq), :] = st[hh][0] + jnp.log(st[hh][1])

        def group(g, carry):
            for r in range(grp):
                qblock(g, r)
            return carry

        lax.fori_loop(0, nq // grp, group, 0)

    base = 2 * D_A // LANES
    nh = D_B // LANES
    col = lambda off: pl.BlockSpec((t, LANES), lambda p: (0, off + p))
    return _call(
        body, (nh,), [col(base), col(base + nh), col(base + 2 * nh), pl.BlockSpec((2, 1, t), lambda p: (p, 0, 0))],
        [col(0), pl.BlockSpec((2, t, 1), lambda p: (p, 0, 0))],
        [jax.ShapeDtypeStruct((t, D_B), F32), jax.ShapeDtypeStruct((N_HEADS, t, 1), F32)],
        [pltpu.VMEM((t, LANES), BF16), pltpu.VMEM((t, LANES), BF16)], ("parallel",), "attn_fwd", [z, z, z, crow], carry)


def _attn_bwd(z, crow, lse, do, carry=None):
    t = z.shape[0]
    tq, nq, grp = _attn_groups(t)
    tw = grp * tq
    scale = HEAD_DIM ** -0.5

    def body(q_ref, k_ref, v_ref, cr_ref, lse_ref, do_ref, dq_ref, dk_ref, dv_ref, dc_ref, qa_s, da_s, kb_s, vb_s, dl_s):
        lane = lax.broadcasted_iota(jnp.int32, (1, LANES), 1)
        hmask = [(lane // HEAD_DIM) == hh for hh in range(2)]
        _cast_rows(k_ref, kb_s, t, tq, lambda v: v)
        _cast_rows(v_ref, vb_s, t, tq, lambda v: v)
        for hh in range(2):
            _cast_rows(q_ref, qa_s.at[hh], t, tq, lambda v, hh=hh: jnp.where(hmask[hh], v * scale, 0.0))
            _cast_rows(do_ref, da_s.at[hh], t, tq, lambda v, hh=hh: jnp.where(hmask[hh], v, 0.0))
        _cast_rows(q_ref, dq_ref, t, tq, lambda v: jnp.zeros_like(v))

        def probs(hh, q0, nq_rows, k0, nk_rows, off):
            s = _dot(qa_s[hh, pl.ds(q0, nq_rows), :], kb_s[pl.ds(k0, nk_rows), :], NT) - cr_ref[hh, :, pl.ds(k0, nk_rows)]
            p = jnp.exp(s - lse_ref[hh, pl.ds(q0, nq_rows), :])
            if off is not None:
                keep = (lax.broadcasted_iota(jnp.int32, (nq_rows, nk_rows), 0) + off
                        >= lax.broadcasted_iota(jnp.int32, (nq_rows, nk_rows), 1))
                p = jnp.where(keep, p, 0.0)
            return p, _dot(da_s[hh, pl.ds(q0, nq_rows), :], vb_s[pl.ds(k0, nk_rows), :], NT)

        def delta(g, r):
            q0 = pl.multiple_of((g * grp + r) * tq, tq)

            def add(k0, width, off, acc):
                res = []
                for hh in range(2):
                    p, dp = probs(hh, q0, tq, k0, width, off)
                    res.append(acc[hh] + jnp.sum(p * dp, axis=-1, keepdims=True))
                return tuple(res)

            zcol = jnp.zeros((tq, 1), F32)
            acc = lax.fori_loop(0, g, lambda j, acc: add(pl.multiple_of(j * tw, tw), tw, None, acc), (zcol, zcol))
            acc = add(pl.multiple_of(g * tw, tw), (r + 1) * tq, r * tq, acc)
            for hh in range(2):
                dl_s[hh, pl.ds(q0, tq), :] = acc[hh]

        def delta_group(g, carry):
            for r in range(grp):
                delta(g, r)
            return carry

        lax.fori_loop(0, nq // grp, delta_group, 0)

        def kblock(g, r):
            k0 = pl.multiple_of((g * grp + r) * tq, tq)
            kb = kb_s[pl.ds(k0, tq), :]

            def upd(q0, height, off, st):
                dk, dv, dc = st[0], st[1], [st[2], st[3]]
                dqs = []
                for hh in range(2):
                    p, dp = probs(hh, q0, height, k0, tq, off)
                    ds = p * (dp - dl_s[hh, pl.ds(q0, height), :])
                    dv = dv + _dot(p, da_s[hh, pl.ds(q0, height), :], TN)
                    dk = dk + _dot(ds, qa_s[hh, pl.ds(q0, height), :], TN)
                    dqs.append(_dot(ds, kb, NN))
                    dc[hh] = dc[hh] - jnp.sum(ds, axis=0, keepdims=True)
                dq_ref[pl.ds(q0, height), :] += jnp.where(hmask[0], dqs[0], dqs[1]) * scale
                return dk, dv, dc[0], dc[1]

            zero = jnp.zeros((tq, LANES), F32)
            zrow = jnp.zeros((1, tq), F32)
            st = upd(k0, (grp - r) * tq, 0, (zero, zero, zrow, zrow))
            st = lax.fori_loop(g + 1, nq // grp, lambda i, st: upd(pl.multiple_of(i * tw, tw), tw, None, st), st)
            dk_ref[pl.ds(k0, tq), :] = st[0]
            dv_ref[pl.ds(k0, tq), :] = st[1]
            for hh in range(2):
                dc_ref[hh, :, pl.ds(k0, tq)] = st[2 + hh]

        def kgroup(g, carry):
            for r in range(grp):
                kblock(g, r)
            return carry

        lax.fori_loop(0, nq // grp, kgroup, 0)

    base = 2 * D_A // LANES
    nh = D_B // LANES
    col = lambda off: pl.BlockSpec((t, LANES), lambda p: (0, off + p))
    ccs = pl.BlockSpec((2, t, 1), lambda p: (p, 0, 0))
    crs = pl.BlockSpec((2, 1, t), lambda p: (p, 0, 0))
    return _call(
        body, (nh,), [col(base), col(base + nh), col(base + 2 * nh), crs, ccs, col(0)], [col(0), col(0), col(0), crs],
        [jax.ShapeDtypeStruct((t, D_B), F32)] * 3 + [jax.ShapeDtypeStruct((N_HEADS, 1, t), F32)],
        [pltpu.VMEM((2, t, LANES), BF16), pltpu.VMEM((2, t, LANES), BF16), pltpu.VMEM((t, LANES), BF16),
         pltpu.VMEM((t, LANES), BF16), pltpu.VMEM((2, t, 1), F32)], ("parallel",), "attn_bwd", [z, z, z, crow, lse, do], carry)


def _s5_disc(a_re, a_im, log_dt, b_re, b_im):
    dt = jnp.exp(log_dt)
    mag = jnp.exp(a_re * dt)
    ar = mag * jnp.cos(a_im * dt)
    ai = mag * jnp.sin(a_im * dt)
    den = a_re * a_re + a_im * a_im
    kr = ((ar - 1.0) * a_re + ai * a_im) / den
    ki = (ai * a_re - (ar - 1.0) * a_im) / den
    kr3, ki3 = kr[:, None, :], ki[:, None, :]
    return ar, ai, kr3 * b_re - ki3 * b_im, kr3 * b_im + ki3 * b_re


def _s5_prep(a_re, a_im, log_dt, b_re, b_im):
    g, p = a_re.shape
    gc = b_re.shape[1]

    def body(*refs):
        res = _s5_disc(*[r[...] for r in refs[:5]])
        for o_ref, v in zip(refs[5:], res):
            o_ref[...] = v

    sds = jax.ShapeDtypeStruct
    return pl.pallas_call(body, out_shape=[sds((g, p), F32), sds((g, p), F32), sds((g, gc, p), F32), sds((g, gc, p), F32)],
                          name="s5_prep")(a_re, a_im, log_dt, b_re, b_im)


def _s5_prep_bwd(a_re, a_im, log_dt, b_re, b_im, d_ar, d_ai, d_br, d_bi):
    ins = (a_re, a_im, log_dt, b_re, b_im)

    def body(*refs):
        vals = [r[...] for r in refs[:5]]
        cts = tuple(r[...] for r in refs[5:9])
        _, pull = jax.vjp(_s5_disc, *vals)
        for o_ref, v in zip(refs[9:], pull(cts)):
            o_ref[...] = v

    return pl.pallas_call(body, out_shape=[jax.ShapeDtypeStruct(a.shape, F32) for a in ins], name="s5_prep_bwd")(
        *ins, d_ar, d_ai, d_br, d_bi)


def _s5_scan_rows(t, ar, ai, hr_s, hi_s, off, reverse):
    n = ar.shape[1]
    if reverse:
        ai = -ai
    sub = lax.broadcasted_iota(jnp.int32, (8, n), 0)
    cmul = lambda xr, xi, yr, yi: (xr * yr - xi * yi, xr * yi + xi * yr)
    pw = [(ar, ai)]
    for _ in range(7):
        pw.append(cmul(*pw[-1], ar, ai))
    pr = jnp.zeros((8, n), F32)
    pi = jnp.zeros((8, n), F32)
    for r in range(8):
        k = 7 - r if reverse else r
        pr = jnp.where(sub == r, pw[k][0], pr)
        pi = jnp.where(sub == r, pw[k][1], pi)

    def step(g, carry):
        cr, ci = carry
        r0 = pl.multiple_of(off + (t // 8 - 1 - g if reverse else g) * 8, 8)
        br = hr_s[pl.ds(r0, 8), :]
        bi = hi_s[pl.ds(r0, 8), :]
        for s in (1, 2, 4):
            ok = (sub < 8 - s) if reverse else (sub >= s)
            shift = 8 - s if reverse else s
            sr = jnp.where(ok, pltpu.roll(br, shift, 0), 0.0)
            si = jnp.where(ok, pltpu.roll(bi, shift, 0), 0.0)
            mr, mi = cmul(pw[s - 1][0], pw[s - 1][1], sr, si)
            br, bi = br + mr, bi + mi
        mr, mi = cmul(pr, pi, cr, ci)
        br, bi = br + mr, bi + mi
        hr_s[pl.ds(r0, 8), :] = br
        hi_s[pl.ds(r0, 8), :] = bi
        edge = sub == (0 if reverse else 7)
        return (jnp.sum(jnp.where(edge, br, 0.0), axis=0, keepdims=True),
                jnp.sum(jnp.where(edge, bi, 0.0), axis=0, keepdims=True))

    zero = jnp.zeros((1, n), F32)
    lax.fori_loop(0, t // 8, step, (zero, zero))


def _s5_fwd(z, bd_re, bd_im, ab_re, ab_im, cd_re, cd_im, dvec, carry=None):
    t = z.shape[0]
    cr = _tile(t, 256)
    ns = N_STATE // 2

    def body(u_ref, br_ref, bi_ref, ar_ref, ai_ref, cre_ref, cim_ref, d_ref, y_ref, hr_s, hi_s):
        def p1(c, carry):
            t0 = pl.multiple_of(c * cr, cr)
            u = u_ref[pl.ds(t0, cr), :]
            hr_s[pl.ds(t0, cr), :] = _dot(u, br_ref[...], NN)
            hi_s[pl.ds(t0, cr), :] = _dot(u, bi_ref[...], NN)
            return carry

        lax.fori_loop(0, t // cr, p1, 0)
        _s5_scan_rows(t, ar_ref[...], ai_ref[...], hr_s, hi_s, 0, False)

        def p3(c, carry):
            t0 = pl.multiple_of(c * cr, cr)
            y_ref[pl.ds(t0, cr), :] = (_dot(hr_s[pl.ds(t0, cr), :], cre_ref[...], NN)
                                       - _dot(hi_s[pl.ds(t0, cr), :], cim_ref[...], NN)
                                       + d_ref[...] * u_ref[pl.ds(t0, cr), :])
            return carry

        lax.fori_loop(0, t // cr, p3, 0)

    blk = lambda r, c: pl.BlockSpec((None, r, c), lambda b: (b, 0, 0))
    return _call(
        body, (2,),
        [pl.BlockSpec((t, LANES), lambda b: (0, Z_U // LANES + b)), blk(LANES, ns), blk(LANES, ns), blk(1, ns),
         blk(1, ns), blk(ns, LANES), blk(ns, LANES), pl.BlockSpec((1, LANES), lambda b: (0, b))],
        [pl.BlockSpec((t, LANES), lambda b: (0, b))], [jax.ShapeDtypeStruct((t, D_C), F32)],
        [pltpu.VMEM((t, ns), F32), pltpu.VMEM((t, ns), F32)], ("arbitrary",), "s5_fwd",
        [z, bd_re, bd_im, ab_re, ab_im, cd_re, cd_im, dvec], carry)


def _s5_bwd(z, dy, bd_re, bd_im, ab_re, ab_im, cd_re, cd_im, dvec):
    t = z.shape[0]
    cr = _tile(t, 256)
    ns = N_STATE // 2

    def body(u_ref, dy_ref, br_ref, bi_ref, ar_ref, ai_ref, cre_ref, cim_ref, d_ref,
             du_ref, dbr_ref, dbi_ref, dar_ref, dai_ref, dcre_ref, dcim_ref, dd_ref, hr_s, hi_s, gr_s, gi_s):
        zero8 = jnp.zeros((8, ns), F32)
        hr_s[pl.ds(0, 8), :] = zero8
        hi_s[pl.ds(0, 8), :] = zero8
        for ref in (dbr_ref, dbi_ref, dar_ref, dai_ref, dcre_ref, dcim_ref, dd_ref):
            ref[...] = jnp.zeros(ref.shape, F32)

        def p1(c, carry):
            t0 = pl.multiple_of(c * cr, cr)
            u = u_ref[pl.ds(t0, cr), :]
            hr_s[pl.ds(t0 + 8, cr), :] = _dot(u, br_ref[...], NN)
            hi_s[pl.ds(t0 + 8, cr), :] = _dot(u, bi_ref[...], NN)
            return carry

        lax.fori_loop(0, t // cr, p1, 0)
        _s5_scan_rows(t, ar_ref[...], ai_ref[...], hr_s, hi_s, 8, False)

        def p3(c, carry):
            t0 = pl.multiple_of(c * cr, cr)
            dyv = dy_ref[pl.ds(t0, cr), :]
            u = u_ref[pl.ds(t0, cr), :]
            gr_s[pl.ds(t0, cr), :] = _dot(dyv, cre_ref[...], NT)
            gi_s[pl.ds(t0, cr), :] = -_dot(dyv, cim_ref[...], NT)
            dcre_ref[...] += _dot(hr_s[pl.ds(t0 + 8, cr), :], dyv, TN)
            dcim_ref[...] -= _dot(hi_s[pl.ds(t0 + 8, cr), :], dyv, TN)
            dd_ref[...] += jnp.sum(dyv * u, axis=0, keepdims=True)
            du_ref[pl.ds(t0, cr), :] = dyv * d_ref[...]
            return carry

        lax.fori_loop(0, t // cr, p3, 0)
        _s5_scan_rows(t, ar_ref[...], ai_ref[...], gr_s, gi_s, 0, True)

        def p5(c, carry):
            t0 = pl.multiple_of(c * cr, cr)
            u = u_ref[pl.ds(t0, cr), :]
            gr = gr_s[pl.ds(t0, cr), :]
            gi = gi_s[pl.ds(t0, cr), :]
            dbr_ref[...] += _dot(u, gr, TN)
            dbi_ref[...] += _dot(u, gi, TN)
            du_ref[pl.ds(t0, cr), :] += _dot(gr, br_ref[...], NT) + _dot(gi, bi_ref[...], NT)
            hpr = pltpu.roll(hr_s[pl.ds(t0, cr + 8), :], 1, 0)[8:, :]
            hpi = pltpu.roll(hi_s[pl.ds(t0, cr + 8), :], 1, 0)[8:, :]
            dar_ref[...] += jnp.sum(gr * hpr + gi * hpi, axis=0, keepdims=True)
            dai_ref[...] += jnp.sum(gi * hpr - gr * hpi, axis=0, keepdims=True)
            return carry

        lax.fori_loop(0, t // cr, p5, 0)

    blk = lambda r, c: pl.BlockSpec((None, r, c), lambda b: (b, 0, 0))
    ucol = pl.BlockSpec((t, LANES), lambda b: (0, Z_U // LANES + b))
    ycol = pl.BlockSpec((t, LANES), lambda b: (0, b))
    dsp = pl.BlockSpec((1, LANES), lambda b: (0, b))
    sds = jax.ShapeDtypeStruct
    return pl.pallas_call(
        body, grid=(2,),
        in_specs=[ucol, ycol, blk(LANES, ns), blk(LANES, ns), blk(1, ns), blk(1, ns), blk(ns, LANES), blk(ns, LANES), dsp],
        out_specs=[ycol, blk(LANES, ns), blk(LANES, ns), blk(1, ns), blk(1, ns), blk(ns, LANES), blk(ns, LANES), dsp],
        out_shape=[sds((t, D_C), F32), sds((2, LANES, ns), F32), sds((2, LANES, ns), F32), sds((2, 1, ns), F32),
                   sds((2, 1, ns), F32), sds((2, ns, LANES), F32), sds((2, ns, LANES), F32), sds((1, D_C), F32)],
        scratch_shapes=[pltpu.VMEM((t + 8, ns), F32), pltpu.VMEM((t + 8, ns), F32), pltpu.VMEM((t, ns), F32),
                        pltpu.VMEM((t, ns), F32)],
        compiler_params=pltpu.CompilerParams(dimension_semantics=("arbitrary",)), name="s5_bwd")(
            z, dy, bd_re, bd_im, ab_re, ab_im, cd_re, cd_im, dvec)


def _mix_out(out_a, out_b, yc, x1, ga, gb, gc, wglu, wout, ln_g, ln_b):
    yg = _gelu(yc)
    out_c = yg * _sig(_bdot(yg, wglu))
    o = jnp.concatenate([_rms(out_a, ga), _rms(out_b, gb), _rms(out_c, gc)], axis=-1)
    return (_ln(ALPHA * x1 + _bdot(o, wout), ln_g, ln_b),)


def _ln_only(pre, g, b):
    return (_ln(pre, g, b),)


def _loss_head(y, target, tm):
    t, d = y.shape

    def body(y_ref, t_ref, dy_ref, l_ref):
        i = pl.program_id(0)
        e = y_ref[...] - t_ref[...]
        dy_ref[...] = e * (1.0 / d)
        part = 0.5 * jnp.sum(jnp.sum(e * e, axis=-1, keepdims=True) * (1.0 / d), axis=0, keepdims=True)
        row = jnp.where(lax.broadcasted_iota(jnp.int32, (1, LANES), 1) == 0, part, 0.0)

        @pl.when(i == 0)
        def _():
            l_ref[...] = row

        @pl.when(i > 0)
        def _():
            l_ref[...] += row

    spec = pl.BlockSpec((tm, d), lambda i: (i, 0))
    return pl.pallas_call(
        body, grid=(t // tm,), in_specs=[spec, spec], out_specs=[spec, pl.BlockSpec((1, LANES), lambda i: (0, 0))],
        out_shape=[jax.ShapeDtypeStruct((t, d), F32), jax.ShapeDtypeStruct((1, LANES), F32)],
        compiler_params=pltpu.CompilerParams(dimension_semantics=("arbitrary",)), name="loss_head")(y, target)


def _adamw(w, g, m, v, name):
    r, c = w.shape
    tr = r
    for cand in (512, 256, 352, 128):
        if r % cand == 0:
            tr = cand
            break

    def body(w_ref, g_ref, m_ref, v_ref, d_ref, nm_ref, nv_ref):
        gv = g_ref[...]
        mn = ADAM_B1 * m_ref[...] + (1.0 - ADAM_B1) * gv
        vn = ADAM_B2 * v_ref[...] + (1.0 - ADAM_B2) * (gv * gv)
        m_hat = mn / (1.0 - ADAM_B1 ** ADAM_STEP)
        v_hat = vn / (1.0 - ADAM_B2 ** ADAM_STEP)
        d_ref[...] = -ADAM_LR * (m_hat / (jnp.sqrt(v_hat) + ADAM_EPS) + ADAM_WD * w_ref[...])
        nm_ref[...] = mn
        nv_ref[...] = vn

    spec = pl.BlockSpec((tr, c), lambda i: (i, 0))
    return pl.pallas_call(
        body, grid=(r // tr,), in_specs=[spec] * 4, out_specs=[spec] * 3,
        out_shape=[jax.ShapeDtypeStruct((r, c), F32)] * 3,
        compiler_params=pltpu.CompilerParams(dimension_semantics=("parallel",)), name=name)(w, g, m, v)


def _row_tile(r):
    for cand in (512, 448, 352, 256, 128):
        if r % cand == 0:
            return cand
    return r


def _pair_add(a, b, idx, name, out_dtype):
    _, r, w = a.shape
    tr = _row_tile(r)

    def body(i_ref, a_ref, b_ref, o_ref):
        o_ref[...] = (a_ref[...].astype(F32) + b_ref[...].astype(F32)).astype(o_ref.dtype)

    grid_spec = pltpu.PrefetchScalarGridSpec(
        num_scalar_prefetch=1, grid=(4, r // tr),
        in_specs=[pl.BlockSpec((None, tr, w), lambda q, i, s: (2 * q + s[0], i, 0)),
                  pl.BlockSpec((None, tr, w), lambda q, i, s: (q, i, 0))],
        out_specs=pl.BlockSpec((None, tr, w), lambda q, i, s: (q, i, 0)))
    return pl.pallas_call(body, grid_spec=grid_spec, out_shape=jax.ShapeDtypeStruct((4, r, w), out_dtype),
                          compiler_params=pltpu.CompilerParams(dimension_semantics=("parallel", "parallel")), name=name)(
                              idx, a, b)


def _quad_add(p, rb, idx, name):
    _, r, w = p.shape
    tr = _row_tile(r)

    def body(i_ref, p_ref, r0, r1, r2, o_ref):
        o_ref[...] = ((p_ref[...].astype(F32) + r0[...].astype(F32)) + r1[...].astype(F32)) + r2[...].astype(F32)

    grid_spec = pltpu.PrefetchScalarGridSpec(
        num_scalar_prefetch=1, grid=(r // tr,),
        in_specs=[pl.BlockSpec((None, tr, w), lambda i, s: (s[0], i, 0))]
        + [pl.BlockSpec((None, tr, w), functools.partial(lambda i, s, k: (k, i, 0), k=k)) for k in range(3)],
        out_specs=pl.BlockSpec((tr, w), lambda i, s: (i, 0)))
    return pl.pallas_call(body, grid_spec=grid_spec, out_shape=jax.ShapeDtypeStruct((r, w), F32),
                          compiler_params=pltpu.CompilerParams(dimension_semantics=("parallel",)), name=name)(
                              idx, p, rb, rb, rb)


def _gather_small(buf):
    def plan(x, y, c, ins, outs):
        dst = outs[0].at[4 * x + 2 * y + c]
        res = [(ins[0], dst, None)]
        for rel in range(1, 8):
            res.append((ins[0], dst, (x ^ (rel >> 2), y ^ ((rel >> 1) & 1), c ^ (rel & 1))))
        return res

    return _Carry([buf], [jax.ShapeDtypeStruct((8,) + buf.shape, buf.dtype)], {}, plan, 8)


def _sum_small(allb):
    _, r, w = allb.shape

    def body(a_ref, o_ref):
        s = a_ref[0]
        for k in range(1, 8):
            s = s + a_ref[k]
        o_ref[...] = s

    return pl.pallas_call(body, out_shape=jax.ShapeDtypeStruct((r, w), F32), name="ar_small_sum")(allb)


def _pad_rows(a, rows):
    return jnp.pad(a, ((0, rows - a.shape[0]), (0, 0)))


def _pack_small(arrs):
    rows, tail = [], []
    for a in arrs:
        if not tail and a.size % LANES == 0:
            rows.append(a.reshape(-1, LANES))
        else:
            tail.append(a.reshape(-1))
    n_rows = sum(r.shape[0] for r in rows)
    n_tail = sum(int(v.size) for v in tail)
    tail_rows = -(-n_tail // LANES)
    total_rows = n_rows + tail_rows + (-(n_rows + tail_rows)) % 8
    if tail:
        tail.append(jnp.zeros((tail_rows * LANES - n_tail,), F32))
        rows.append(jnp.concatenate(tail).reshape(tail_rows, LANES))
    if total_rows > n_rows + tail_rows:
        rows.append(jnp.zeros((total_rows - n_rows - tail_rows, LANES), F32))
    return jnp.concatenate(rows, axis=0)


def _unpack_small(buf, shapes):
    out, off = [], 0
    flat = buf.reshape(-1)
    for s in shapes:
        n = int(np.prod(s))
        out.append(flat[off:off + n].reshape(s))
        off += n
    return out


def _block_diag(blocks, nb):
    m, r, c = blocks.shape
    n = m // nb
    eye = jnp.eye(n, dtype=blocks.dtype)
    return (blocks.reshape(nb, n, r, 1, c) * eye[None, :, None, :, None]).reshape(nb, n * r, n * c)


def _diag_blocks(dense, n):
    nb = dense.shape[0]
    r, c = dense.shape[1] // n, dense.shape[2] // n
    eye = jnp.eye(n, dtype=dense.dtype)
    return jnp.sum(dense.reshape(nb, n, r, n, c) * eye[None, :, None, :, None], axis=3).reshape(nb * n, r, c)


def kernel(x, ffn1_w_gate, ffn1_w_up, ffn1_w_down, ln1_g, ln1_b, w_in, conv_w, conv_b, rg_w_a, rg_b_a, rg_w_x, rg_b_x, rg_lambda, fox_b_f, s5_a_re, s5_a_im, s5_log_dt, s5_b_re, s5_b_im, s5_c_re, s5_c_im, s5_d, s5_w_glu, mix_norm_g, w_out, ln2_g, ln2_b, ffn2_w_gate, ffn2_w_up, ffn2_w_down, ln3_g, ln3_b, loss_target, m_ffn1_w_gate, m_ffn1_w_up, m_ffn1_w_down, m_ln1_g, m_ln1_b, m_w_in, m_conv_w, m_conv_b, m_rg_w_a, m_rg_b_a, m_rg_w_x, m_rg_b_x, m_rg_lambda, m_fox_b_f, m_s5_a_re, m_s5_a_im, m_s5_log_dt, m_s5_b_re, m_s5_b_im, m_s5_c_re, m_s5_c_im, m_s5_d, m_s5_w_glu, m_mix_norm_g, m_w_out, m_ln2_g, m_ln2_b, m_ffn2_w_gate, m_ffn2_w_up, m_ffn2_w_down, m_ln3_g, m_ln3_b, v_ffn1_w_gate, v_ffn1_w_up, v_ffn1_w_down, v_ln1_g, v_ln1_b, v_w_in, v_conv_w, v_conv_b, v_rg_w_a, v_rg_b_a, v_rg_w_x, v_rg_b_x, v_rg_lambda, v_fox_b_f, v_s5_a_re, v_s5_a_im, v_s5_log_dt, v_s5_b_re, v_s5_b_im, v_s5_c_re, v_s5_c_im, v_s5_d, v_s5_w_glu, v_mix_norm_g, v_w_out, v_ln2_g, v_ln2_b, v_ffn2_w_gate, v_ffn2_w_up, v_ffn2_w_down, v_ln3_g, v_ln3_b):
    a = dict(locals())
    w = {n: a[n] for n in WEIGHTS}
    t, d = x.shape[1], x.shape[2]
    f = ffn1_w_down.shape[1] * 8
    fs, ds = f // 8, d // 8
    mx, my, mc = lax.axis_index("x"), lax.axis_index("y"), lax.axis_index("c")
    me = 4 * mx + 2 * my + mc
    tm = _tile(t, 512)
    tf = f // 2
    win_rows = ds * Z_W // d

    FFN1, MIXW, FFN2 = ['g1', 'u1', 'd1'], ['win', 'wout', 'glu', 'conv'], ['g2', 'u2', 'd2']
    glu_rows = D_C * D_C // (8 * d)

    def shard_segs(l):
        wi = w['w_in'][l]
        win_p = jnp.concatenate([wi[:, :Z_F + N_HEADS], jnp.zeros((ds, Z_U - Z_F - N_HEADS), F32), wi[:, Z_F + N_HEADS:]], axis=1)
        conv_bits = lax.bitcast_convert_type(w['conv_w'][l], BF16).reshape(1, -1)
        segs = dict(g1=w['ffn1_w_gate'][l].T, u1=w['ffn1_w_up'][l].T, d1=w['ffn1_w_down'][l],
                    g2=w['ffn2_w_gate'][l].T, u2=w['ffn2_w_up'][l].T, d2=w['ffn2_w_down'][l],
                    win=win_p.reshape(win_rows, d), wout=w['w_out'][l], glu=_pad_rows(w['s5_w_glu'][l].reshape(-1, d), 16))
        segs = {k: v.astype(BF16) for k, v in segs.items()}
        segs['conv'] = _pad_rows(jnp.pad(conv_bits, ((0, 0), (0, d - conv_bits.shape[1]))), 16)
        return segs

    shards = [shard_segs(l) for l in range(DEPTH)]
    wts = {}

    def cat(keys):
        return jnp.concatenate([shards[l][k] for l, k in keys], axis=0)

    def split(g, keys):
        off = 0
        for l, k in keys:
            r = shards[l][k].shape[0]
            wts[(l, k)] = g[:, off:off + r]
            off += r

    grp_a = [(0, k) for k in FFN1]
    grp_b = [(0, k) for k in MIXW]
    grp_c = [(0, k) for k in FFN2]
    grp_d = [(1, k) for k in FFN1 + MIXW]
    grp_e = [(1, k) for k in FFN2]
    (g_a,) = _run(_ag_chips(cat(grp_a)), "ag_chips")
    (g_a,) = _run(_ag_sibling(g_a), "ag_sibling")
    split(g_a, grp_a)

    xs = x[0]
    saved = []
    cur = xs
    for l in range(DEPTH):
        row = lambda n: w[n][l].reshape(1, -1)
        ffn = lambda keys: tuple(wts[(l, k)].reshape(f, d) for k in keys)
        wa = _block_diag(w['rg_w_a'][l], 3)
        wx = _block_diag(w['rg_w_x'][l], 3)
        bf = jnp.pad(row('fox_b_f'), ((0, 0), (0, LANES - N_HEADS)))
        s5p = (w['s5_a_re'][l], w['s5_a_im'][l], w['s5_log_dt'][l].reshape(-1, 1),
               w['s5_b_re'][l].transpose(0, 2, 1), w['s5_b_im'][l].transpose(0, 2, 1))
        ab_re, ab_im, bb_re, bb_im = _s5_prep(*s5p)
        bd_re, bd_im = _block_diag(bb_re, 2), _block_diag(bb_im, 2)
        cd_re = _block_diag(w['s5_c_re'][l].transpose(0, 2, 1), 2)
        cd_im = _block_diag(w['s5_c_im'][l].transpose(0, 2, 1), 2)
        abr, abi = ab_re.reshape(2, 1, N_STATE // 2), ab_im.reshape(2, 1, N_STATE // 2)
        gm = row('mix_norm_g')
        ga, gb, gc = gm[:, :D_A], gm[:, D_A:D_A + D_B], gm[:, D_A + D_B:]

        x0 = cur
        ffn1 = ffn(FFN1)
        (x1, pre1, gs1, us1), cres = _ffn_fwd(x0, *ffn1, row('ln1_g'), row('ln1_b'), tm, tf,
                                              carry=_ag_chips(cat(grp_b if l == 0 else grp_e)))
        if l == 0:
            (g_b,) = _run(_ag_sibling(cres[0]), "ag_sibling")
            split(g_b, grp_b)
        else:
            g_e = cres[0]
        win = wts[(l, 'win')].reshape(d, Z_W)
        wout = wts[(l, 'wout')].reshape(d, d).astype(F32)
        wglu = wts[(l, 'glu')][:, :glu_rows].reshape(D_C, D_C).astype(F32)
        conv_full = lax.bitcast_convert_type(
            wts[(l, 'conv')][:, 0, :2 * CONV_WIDTH * D_A // 8].reshape(8, CONV_WIDTH, D_A // 8, 2), F32)
        conv_full = conv_full.transpose(1, 0, 2).reshape(CONV_WIDTH, D_A)
        z = _mm(x1, win, 'nn', F32, tm, Z_W, d, "mix_in")
        out_a, h_a = _rg_fwd(z, conv_full, row('conv_b'), wa, row('rg_b_a'), wx, row('rg_b_x'), row('rg_lambda'))
        cs = _fgate_fwd(z, bf)
        crow = cs[:, :N_HEADS].T.reshape(N_HEADS, 1, t)
        (out_b, lse), cres = _attn_fwd(z, crow, carry=_ag_chips(cat(grp_c)) if l == 0 else _ag_sibling(g_e))
        if l == 0:
            (yc,), cres = _s5_fwd(z, bd_re, bd_im, abr, abi, cd_re, cd_im, row('s5_d'), carry=_ag_sibling(cres[0]))
            split(cres[0], grp_c)
        else:
            split(cres[0], grp_e)
            (yc,), _ = _s5_fwd(z, bd_re, bd_im, abr, abi, cd_re, cd_im, row('s5_d'))
        mix_params = [ga, gb, gc, wglu, wout, row('ln2_g'), row('ln2_b')]
        (x2,) = _rowwise(_mix_out, [out_a, out_b, yc, x1], mix_params, [(d, F32)], _tile(t, 256), "mix_out")
        ffn2 = ffn(FFN2)
        (x3, pre3, gs2, us2), cres = _ffn_fwd(x2, *ffn2, row('ln3_g'), row('ln3_b'), tm, tf,
                                              carry=_ag_chips(cat(grp_d)) if l == 0 else None)
        if l == 0:
            (g_d,) = _run(_ag_sibling(cres[0]), "ag_sibling")
            split(g_d, grp_d)
        saved.append(dict(x0=x0, x1=x1, pre1=pre1, gs1=gs1, us1=us1, z=z, out_a=out_a, h_a=h_a, crow=crow,
                          out_b=out_b, lse=lse, yc=yc, x2=x2, pre3=pre3, gs2=gs2, us2=us2, mix_params=mix_params,
                          ffn1=ffn1, ffn2=ffn2, win=win, conv_full=conv_full, wa=wa, wx=wx, bf=bf, s5p=s5p,
                          s5m=(bd_re, bd_im, abr, abi, cd_re, cd_im)))
        cur = x3

    dy, loss_row = _loss_head(cur, loss_target[0], tm)
    loss = lax.psum(loss_row[0, 0], ("x", "y", "c"))

    assert DEPTH == 2
    small_grads = {}
    small_names = ['conv_w'] + SMALL
    c_idx = jnp.reshape(mc, (1,)).astype(jnp.int32)
    chip_idx = jnp.reshape(2 * mx + my, (1,)).astype(jnp.int32)

    def blocks(arrs):
        return jnp.concatenate([v.astype(BF16).reshape(8, -1, d) for v in arrs], axis=1)

    def mixer_blocks(dwin, dwout, dwglu):
        glu = jnp.pad(dwglu.astype(BF16).reshape(8, glu_rows, d), ((0, 0), (0, 32 - glu_rows), (0, 0)))
        return jnp.concatenate([dwin.reshape(8, win_rows, d), dwout.astype(BF16).reshape(8, ds, d), glu], axis=1)

    def pair(full, ra):
        return _pair_add(full, ra, c_idx, "rs_add_sibling", BF16)

    for l in reversed(range(DEPTH)):
        s = saved[l]
        row = lambda n: w[n][l].reshape(1, -1)
        wg_tiles = dict(tm=tf, tn=d, tk=_tile(t, 1024))
        first = l == 0

        def ffn_back(dyv, pre, gs, us, wts3, xin, ln_g, ln_b, carry_fn=None, pipeline=False):
            (dpre,), (dlg, dlb), _ = _rowwise_vjp(_ln_only, [pre], [ln_g, ln_b], [dyv], tm, "ln_bwd")
            (dx, dg, du, hh), cres = _ffn_bwd(dpre, gs, us, *wts3, tm, tf, carry=carry_fn(dlg, dlb) if carry_fn else None)
            if not pipeline:
                dwg = _mm(dg, xin, 'tn', BF16, name="ffn_dw_gate", **wg_tiles)
                dwu = _mm(du, xin, 'tn', BF16, name="ffn_dw_up", **wg_tiles)
                dwd = _mm(hh, dpre, 'tn', BF16, name="ffn_dw_down", **wg_tiles)
                return dx, (dwg, dwu, dwd), dlg, dlb, cres

            def front(dw):
                full = dw.reshape(8, fs, d)
                (ra,) = _run(_rs_sibling(full), "rs_sibling")
                return pair(full, ra)

            p_g = front(_mm(dg, xin, 'tn', BF16, name="ffn_dw_gate", **wg_tiles))
            dwu, (rb_g,) = _mm(du, xin, 'tn', BF16, name="ffn_dw_up", carry=_rs_chips(p_g), **wg_tiles)
            p_u = front(dwu)
            dwd, (rb_u,) = _mm(hh, dpre, 'tn', BF16, name="ffn_dw_down", carry=_rs_chips(p_u), **wg_tiles)
            p_d = front(dwd)
            (rb_d,) = _run(_rs_chips(p_d), "rs_chips")
            return dx, ((p_g, rb_g), (p_u, rb_u), (p_d, rb_d)), dlg, dlb, cres

        dx2, (dwg2, dwu2, dwd2), dl3g, dl3b, cres = ffn_back(
            dy, s['pre3'], s['gs2'], s['us2'], s['ffn2'], s['x2'], row('ln3_g'), row('ln3_b'),
            (lambda *_: _rs_sibling(full_l1)) if first else None)
        if first:
            part_l1 = pair(full_l1, cres[0])
            full_c2 = blocks([dwg2, dwu2, dwd2])
        (d_oa, d_ob, d_yc, d_x1), (dga, dgb, dgc, dwglu, dwout, dl2g, dl2b), cres = _rowwise_vjp(
            _mix_out, [s['out_a'], s['out_b'], s['yc'], s['x1']], s['mix_params'], [dx2], _tile(t, 256), "mix_out_bwd",
            carry=_rs_sibling(full_c2) if first else None)
        if first:
            part_c2 = pair(full_c2, cres[0])
        (d_ax, d_ag, dcw, dcb, dwa, dba, dwx, dbx, dlam) = _rg_bwd(
            s['z'], s['h_a'], d_oa, s['conv_full'], row('conv_b'), s['wa'], row('rg_b_a'), s['wx'], row('rg_b_x'), row('rg_lambda'))
        (dq, dk, dv, dcrow), cres = _attn_bwd(s['z'], s['crow'], s['lse'], d_ob,
                                              carry=_join(_rs_chips(part_l1), _rs_chips(part_c2)) if first else None)
        if first:
            rb_l1, rb_c2 = cres
        dc_pad = jnp.pad(dcrow.reshape(N_HEADS, t).T, ((0, 0), (0, LANES - N_HEADS)))
        dzf, dbf = _fgate_bwd(s['z'], s['bf'], dc_pad)
        du_c, dbd_re, dbd_im, dabr, dabi, dcd_re, dcd_im, dd = _s5_bwd(s['z'], d_yc, *s['s5m'], row('s5_d'))
        dz = jnp.concatenate([d_ax, d_ag, dq, dk, dv, dzf, du_c], axis=1)
        dx1 = _mm(dz, s['win'], 'nt', F32, tm, d, Z_W, "mix_in_dx", add=d_x1)
        dwin = _mm(s['x1'], dz, 'tn', BF16, d, Z_W, tm, "mix_in_dw")
        dbb_re, dbb_im = _diag_blocks(dbd_re, N_GROUPS // 2), _diag_blocks(dbd_im, N_GROUPS // 2)
        dcm_re, dcm_im = _diag_blocks(dcd_re, N_GROUPS // 2), _diag_blocks(dcd_im, N_GROUPS // 2)
        da_re, da_im, dlog_dt, db_re, db_im = _s5_prep_bwd(*s['s5p'], dabr.reshape(N_GROUPS, C_STATE), dabi.reshape(N_GROUPS, C_STATE), dbb_re, dbb_im)
        sg = dict(conv_w=dcw, conv_b=dcb, rg_w_a=_diag_blocks(dwa, 2), rg_b_a=dba,
                  rg_w_x=_diag_blocks(dwx, 2), rg_b_x=dbx, rg_lambda=dlam, fox_b_f=dbf[:, :N_HEADS],
                  s5_a_re=da_re, s5_a_im=da_im, s5_log_dt=dlog_dt, s5_b_re=db_re.transpose(0, 2, 1), s5_b_im=db_im.transpose(0, 2, 1),
                  s5_c_re=dcm_re.transpose(0, 2, 1), s5_c_im=dcm_im.transpose(0, 2, 1), s5_d=dd,
                  mix_norm_g=jnp.concatenate([dga, dgb, dgc], axis=1), ln2_g=dl2g, ln2_b=dl2b, ln3_g=dl3g, ln3_b=dl3b)
        small_grads[l] = sg

        if first:
            full_m = mixer_blocks(dwin, dwout, dwglu)
            (ra_m,) = _run(_rs_sibling(full_m), "rs_sibling")
            part_m = pair(full_m, ra_m)

            def last_carry(dlg, dlb):
                sg.update(ln1_g=dlg, ln1_b=dlb)
                packed = _pack_small([small_grads[ll][n] for n in small_names for ll in range(DEPTH)])
                return _join(_rs_chips(part_m), _gather_small(packed))

            dx0, tail, _, _, (rb_m, all_small) = ffn_back(dx1, s['pre1'], s['gs1'], s['us1'], s['ffn1'], s['x0'], row('ln1_g'),
                                                          row('ln1_b'), last_carry, pipeline=True)
        else:
            dx0, (dwg1, dwu1, dwd1), dl1g, dl1b, _ = ffn_back(dx1, s['pre1'], s['gs1'], s['us1'], s['ffn1'], s['x0'],
                                                              row('ln1_g'), row('ln1_b'))
            sg.update(ln1_g=dl1g, ln1_b=dl1b)
            full_l1 = jnp.concatenate([blocks([dwg1, dwu1, dwd1, dwg2, dwu2, dwd2]), mixer_blocks(dwin, dwout, dwglu)], axis=1)
        dy = dx0

    grad_x = dy.reshape(x.shape)
    quad = lambda part, rb: _quad_add(part, rb, chip_idx, "rs_add_chips")
    own = {}

    def take(rows_f32, keys, l):
        off = 0
        for k, r in keys:
            own[(l, k)] = rows_f32[off:off + r]
            off += r

    ffn_keys = lambda names: [(k, fs) for k in names]
    mix_keys = [('win', win_rows), ('wout', ds), ('glu', glu_rows)]
    take(quad(part_l1, rb_l1), ffn_keys(FFN1 + FFN2) + mix_keys, 1)
    take(quad(part_c2, rb_c2), ffn_keys(FFN2), 0)
    take(quad(part_m, rb_m), mix_keys, 0)
    for k, (part, rb) in zip(FFN1, tail):
        own[(0, k)] = quad(part, rb)

    grads = {}
    for k, n in zip(FFN1 + FFN2, ['ffn1_w_gate', 'ffn1_w_up', 'ffn1_w_down', 'ffn2_w_gate', 'ffn2_w_up', 'ffn2_w_down']):
        grads[n] = jnp.stack([own[(l, k)].T if 'down' not in n else own[(l, k)] for l in range(DEPTH)])
    gwin = jnp.stack([own[(l, 'win')].reshape(ds, Z_W) for l in range(DEPTH)])
    grads['w_in'] = jnp.concatenate([gwin[:, :, :Z_F + N_HEADS], gwin[:, :, Z_U:]], axis=2)
    grads['w_out'] = jnp.stack([own[(l, 'wout')] for l in range(DEPTH)])
    grads['s5_w_glu'] = jnp.stack([own[(l, 'glu')].reshape(D_C // 8, D_C) for l in range(DEPTH)])

    small_shapes = [((DEPTH, CONV_WIDTH, D_A) if n == 'conv_w' else w[n].shape) for n in small_names]
    conv_zero = jnp.zeros((DEPTH, CONV_WIDTH, D_A), F32)
    summed = _sum_small(all_small)
    for n, g in zip(small_names, _unpack_small(summed, small_shapes)):
        grads[n] = g
    grads['conv_w'] = lax.dynamic_slice_in_dim(grads['conv_w'], me * (D_A // 8), D_A // 8, axis=2)

    delta, new_m, new_v = {}, {}, {}
    for n in BIG + ['conv_w']:
        sh = w[n].shape
        two = lambda v: v.reshape(-1, sh[-1])
        dl, nm, nv = _adamw(two(w[n]), two(grads[n]), two(a['m_' + n]), two(a['v_' + n]), "adamw_" + n)
        delta[n], new_m[n], new_v[n] = dl.reshape(sh), nm.reshape(sh), nv.reshape(sh)
    dl, nm, nv = _adamw(_pack_small([conv_zero] + [w[n] for n in SMALL]), summed,
                        _pack_small([conv_zero] + [a['m_' + n] for n in SMALL]),
                        _pack_small([conv_zero] + [a['v_' + n] for n in SMALL]), "adamw_small")
    for n, v1, v2, v3 in zip(small_names[1:], _unpack_small(dl, small_shapes)[1:], _unpack_small(nm, small_shapes)[1:],
                             _unpack_small(nv, small_shapes)[1:]):
        delta[n], new_m[n], new_v[n] = v1, v2, v3

    return (loss, grad_x, *[grads[n] for n in WEIGHTS], *[delta[n] for n in WEIGHTS], *[new_m[n] for n in WEIGHTS],
            *[new_v[n] for n in WEIGHTS])
```

```python
import functools
import math

import jax
import jax.numpy as jnp
import numpy as np
from jax import lax
from jax.experimental import pallas as pl
from jax.experimental.pallas import tpu as pltpu

F32 = jnp.float32
BF16 = jnp.bfloat16
MESH = pl.DeviceIdType.MESH

DEPTH = 2
ALPHA = (2 * DEPTH) ** 0.25
LN_EPS = 1e-5
RMS_EPS = 1e-6
RG_C = 8.0
CONV_WIDTH = 4
HEAD_DIM = 64
C_GROUP = 16
C_STATE = 64
D_A = 384
D_B = 384
D_C = 256
N_HEADS = D_B // HEAD_DIM
N_GROUPS = D_C // C_GROUP
N_STATE = N_GROUPS * C_STATE
Z_F = 2 * D_A + 3 * D_B
Z_U = Z_F + 128
Z_W = Z_U + D_C
N_IN = Z_F + N_HEADS + D_C
ADAM_LR, ADAM_B1, ADAM_B2, ADAM_EPS, ADAM_WD, ADAM_STEP = 0.001, 0.9, 0.999, 1e-08, 0.01, 10
LANES = 128
NEG = -1e30

WEIGHTS = ['ffn1_w_gate', 'ffn1_w_up', 'ffn1_w_down', 'ln1_g', 'ln1_b', 'w_in', 'conv_w', 'conv_b', 'rg_w_a', 'rg_b_a',
           'rg_w_x', 'rg_b_x', 'rg_lambda', 'fox_b_f', 's5_a_re', 's5_a_im', 's5_log_dt', 's5_b_re', 's5_b_im', 's5_c_re',
           's5_c_im', 's5_d', 's5_w_glu', 'mix_norm_g', 'w_out', 'ln2_g', 'ln2_b', 'ffn2_w_gate', 'ffn2_w_up', 'ffn2_w_down',
           'ln3_g', 'ln3_b']
BIG = ['ffn1_w_gate', 'ffn1_w_up', 'ffn1_w_down', 'w_in', 's5_w_glu', 'w_out', 'ffn2_w_gate', 'ffn2_w_up', 'ffn2_w_down']
SMALL_TAIL = ['fox_b_f', 's5_log_dt']
SMALL = [n for n in WEIGHTS if n not in BIG and n != 'conv_w' and n not in SMALL_TAIL] + SMALL_TAIL


def _sig(x):
    return 1.0 / (1.0 + jnp.exp(-x))


def _gelu(x):
    return 0.5 * x * (1.0 + jnp.tanh(math.sqrt(2.0 / math.pi) * (x + 0.044715 * (x * x * x))))


def _softplus(x):
    return jnp.maximum(x, 0.0) + jnp.log(1.0 + jnp.exp(jnp.minimum(x, -x)))


def _dot(a, b, dims):
    return lax.dot_general(a.astype(BF16), b.astype(BF16), (dims, ((), ())), preferred_element_type=F32)


NN = ((1,), (0,))
NT = ((1,), (1,))
TN = ((0,), (0,))


@jax.custom_vjp
def _bdot(a, w):
    return _dot(a, w, NN)


def _bdot_fwd(a, w):
    return _dot(a, w, NN), (a, w)


def _bdot_bwd(res, ct):
    a, w = res
    return _dot(ct, w, NT), _dot(a, ct, TN)


_bdot.defvjp(_bdot_fwd, _bdot_bwd)


def _ln(pre, g, b):
    mu = jnp.mean(pre, axis=-1, keepdims=True)
    xc = pre - mu
    var = jnp.mean(xc * xc, axis=-1, keepdims=True)
    return xc * lax.rsqrt(var + LN_EPS) * g + b


def _rms(x, g):
    return x * lax.rsqrt(jnp.mean(x * x, axis=-1, keepdims=True) + RMS_EPS) * g


def _tile(n, want):
    return want if n % want == 0 else n


class _Carry:
    def __init__(self, ins, outs, aliases, plan, n):
        self.ins, self.outs, self.aliases, self.plan, self.n = list(ins), list(outs), dict(aliases), plan, n


def _join(a, b):
    na, ma = len(a.ins), len(a.outs)

    def plan(x, y, c, ins, outs):
        return a.plan(x, y, c, ins[:na], outs[:ma]) + b.plan(x, y, c, ins[na:], outs[ma:])

    aliases = dict(a.aliases)
    aliases.update({na + i: ma + j for i, j in b.aliases.items()})
    return _Carry(a.ins + b.ins, a.outs + b.outs, aliases, plan, a.n + b.n)


def _copies(carry, cins, couts, send, recv):
    x, y, c = lax.axis_index("x"), lax.axis_index("y"), lax.axis_index("c")
    res = []
    for k, (s, d, peer) in enumerate(carry.plan(x, y, c, cins, couts)):
        if peer is None:
            res.append(pltpu.make_async_copy(s, d, send.at[k]))
        else:
            res.append(pltpu.make_async_remote_copy(src_ref=s, dst_ref=d, send_sem=send.at[k], recv_sem=recv.at[k],
                                                    device_id=peer, device_id_type=MESH))
    return res


def _call(body, grid, in_specs, out_specs, out_shape, scratch, semantics, name, args, carry=None):
    n_in, n_out, n_scr = len(in_specs), len(out_specs), len(scratch)
    if carry is None:
        res = pl.pallas_call(body, grid=grid, in_specs=in_specs, out_specs=out_specs, out_shape=out_shape,
                             scratch_shapes=scratch, compiler_params=pltpu.CompilerParams(dimension_semantics=semantics),
                             name=name)(*args)
        return list(res), []
    nci, nco = len(carry.ins), len(carry.outs)

    def wrapped(*refs):
        o0 = n_in + nci
        s0 = o0 + n_out + nco
        cins, couts = refs[n_in:o0], refs[o0 + n_out:s0]
        send, recv = refs[s0 + n_scr:]
        first = functools.reduce(jnp.logical_and, [pl.program_id(k) == 0 for k in range(len(grid))])
        last = functools.reduce(jnp.logical_and, [pl.program_id(k) == grid[k] - 1 for k in range(len(grid))])

        @pl.when(first)
        def _():
            for cp in _copies(carry, cins, couts, send, recv):
                cp.start()

        body(*refs[:n_in], *refs[o0:o0 + n_out], *refs[s0:s0 + n_scr])

        @pl.when(last)
        def _():
            for cp in _copies(carry, cins, couts, send, recv):
                cp.wait()

    hbm = pl.BlockSpec(memory_space=pl.ANY)
    res = pl.pallas_call(
        wrapped, grid=grid, in_specs=list(in_specs) + [hbm] * nci, out_specs=list(out_specs) + [hbm] * nco,
        out_shape=list(out_shape) + carry.outs, scratch_shapes=list(scratch) + [pltpu.SemaphoreType.DMA((carry.n,))] * 2,
        input_output_aliases={n_in + i: n_out + j for i, j in carry.aliases.items()},
        compiler_params=pltpu.CompilerParams(dimension_semantics=("arbitrary",) * len(grid), has_side_effects=True),
        name=name)(*args, *carry.ins)
    return list(res[:n_out]), list(res[n_out:])


def _run(carry, name):
    nci, nco = len(carry.ins), len(carry.outs)

    def body(*refs):
        cps = _copies(carry, refs[:nci], refs[nci:nci + nco], refs[-2], refs[-1])
        for cp in cps:
            cp.start()
        for cp in cps:
            cp.wait()

    hbm = pl.BlockSpec(memory_space=pl.ANY)
    return pl.pallas_call(
        body, in_specs=[hbm] * nci, out_specs=[hbm] * nco, out_shape=carry.outs, input_output_aliases=carry.aliases,
        scratch_shapes=[pltpu.SemaphoreType.DMA((carry.n,))] * 2, compiler_params=pltpu.CompilerParams(has_side_effects=True),
        name=name)(*carry.ins)


def _ag_chips(shard):
    def plan(x, y, c, ins, outs):
        dst = outs[0].at[4 * x + 2 * y + c]
        return [(ins[0], dst, None)] + [(ins[0], dst, (px, py, c)) for px, py in ((1 - x, y), (x, 1 - y), (1 - x, 1 - y))]

    return _Carry([shard], [jax.ShapeDtypeStruct((8,) + shard.shape, shard.dtype)], {}, plan, 4)


def _ag_sibling(g):
    def plan(x, y, c, ins, outs):
        return [(outs[0].at[2 * q + c], outs[0].at[2 * q + c], (x, y, 1 - c)) for q in range(4)]

    return _Carry([g], [jax.ShapeDtypeStruct(g.shape, g.dtype)], {0: 0}, plan, 4)


def _rs_sibling(full):
    def plan(x, y, c, ins, outs):
        return [(ins[0].at[2 * q + (1 - c)], outs[0].at[q], (x, y, 1 - c)) for q in range(4)]

    return _Carry([full], [jax.ShapeDtypeStruct((4,) + full.shape[1:], full.dtype)], {}, plan, 4)


def _rs_chips(part):
    def plan(x, y, c, ins, outs):
        res = []
        for k, (dx, dy) in enumerate(((1, 0), (0, 1), (1, 1))):
            tx, ty = x ^ dx, y ^ dy
            res.append((ins[0].at[2 * tx + ty], outs[0].at[k], (tx, ty, c)))
        return res

    return _Carry([part], [jax.ShapeDtypeStruct((3,) + part.shape[1:], part.dtype)], {}, plan, 3)


def _mm(a, b, dims, out_dtype, tm, tn, tk, name, add=None, carry=None):
    if dims == 'nn':
        (m, k), n = a.shape, b.shape[1]
        a_spec = pl.BlockSpec((tm, tk), lambda i, j, q: (i, q))
        b_spec = pl.BlockSpec((tk, tn), lambda i, j, q: (q, j))
        dn = NN
    elif dims == 'nt':
        (m, k), n = a.shape, b.shape[0]
        a_spec = pl.BlockSpec((tm, tk), lambda i, j, q: (i, q))
        b_spec = pl.BlockSpec((tn, tk), lambda i, j, q: (j, q))
        dn = NT
    else:
        (k, m), n = a.shape, b.shape[1]
        a_spec = pl.BlockSpec((tk, tm), lambda i, j, q: (q, i))
        b_spec = pl.BlockSpec((tk, tn), lambda i, j, q: (q, j))
        dn = TN
    nk = k // tk
    o_spec = pl.BlockSpec((tm, tn), lambda i, j, q: (i, j))

    def body(*refs):
        if add is None:
            a_ref, b_ref, o_ref, acc_ref = refs
        else:
            a_ref, b_ref, add_ref, o_ref, acc_ref = refs
        q = pl.program_id(2)
        part = _dot(a_ref[...], b_ref[...], dn)

        @pl.when(q == 0)
        def _():
            acc_ref[...] = part

        @pl.when(q > 0)
        def _():
            acc_ref[...] += part

        @pl.when(q == nk - 1)
        def _():
            r = acc_ref[...]
            if add is not None:
                r = r + add_ref[...]
            o_ref[...] = r.astype(o_ref.dtype)

    ins = [a, b] + ([] if add is None else [add])
    specs = [a_spec, b_spec] + ([] if add is None else [o_spec])
    (res,), cres = _call(body, (m // tm, n // tn, nk), specs, [o_spec], [jax.ShapeDtypeStruct((m, n), out_dtype)],
                         [pltpu.VMEM((tm, tn), F32)], ("parallel", "parallel", "arbitrary"), name, ins, carry)
    return res if carry is None else (res, cres)


def _rowwise(fn, rows, params, outs, tm, name):
    t = rows[0].shape[0]
    nr, npar = len(rows), len(params)

    def body(*refs):
        r = [x[...] for x in refs[:nr]]
        p = [x[...] for x in refs[nr:nr + npar]]
        res = fn(*r, *p)
        for o_ref, o in zip(refs[nr + npar:], res):
            o_ref[...] = o.astype(o_ref.dtype)

    in_specs = ([pl.BlockSpec((tm, a.shape[1]), lambda i: (i, 0)) for a in rows]
                + [pl.BlockSpec(p.shape, lambda i: (0, 0)) for p in params])
    return pl.pallas_call(
        body, grid=(t // tm,), in_specs=in_specs,
        out_specs=[pl.BlockSpec((tm, c), lambda i: (i, 0)) for c, _ in outs],
        out_shape=[jax.ShapeDtypeStruct((t, c), d) for c, d in outs],
        compiler_params=pltpu.CompilerParams(dimension_semantics=("parallel",)), name=name)(*rows, *params)


def _rowwise_vjp(fn, rows, params, cots, tm, name, carry=None):
    t = rows[0].shape[0]
    nr, npar, nc = len(rows), len(params), len(cots)

    def body(*refs):
        r = [x[...] for x in refs[:nr]]
        p = [x[...] for x in refs[nr:nr + npar]]
        c = [x[...] for x in refs[nr + npar:nr + npar + nc]]
        o_refs = refs[nr + npar + nc:]
        _, pull = jax.vjp(fn, *r, *p)
        grads = pull(tuple(c))
        for o_ref, g in zip(o_refs[:nr], grads[:nr]):
            o_ref[...] = g
        i = pl.program_id(0)

        @pl.when(i == 0)
        def _():
            for o_ref, g in zip(o_refs[nr:], grads[nr:]):
                o_ref[...] = g

        @pl.when(i > 0)
        def _():
            for o_ref, g in zip(o_refs[nr:], grads[nr:]):
                o_ref[...] += g

    row_spec = lambda a: pl.BlockSpec((tm, a.shape[1]), lambda i: (i, 0))
    par_spec = lambda p: pl.BlockSpec(p.shape, lambda i: (0, 0))
    res, cres = _call(
        body, (t // tm,),
        [row_spec(a) for a in rows] + [par_spec(p) for p in params] + [row_spec(a) for a in cots],
        [row_spec(a) for a in rows] + [par_spec(p) for p in params],
        [jax.ShapeDtypeStruct(a.shape, F32) for a in rows] + [jax.ShapeDtypeStruct(p.shape, F32) for p in params],
        [], ("arbitrary",), name, [*rows, *params, *cots], carry)
    return res[:nr], res[nr:], cres


def _ffn_fwd(x, wgt, wut, wd, ln_g, ln_b, tm, tf, carry=None):
    t, d = x.shape
    f = wgt.shape[0]
    nj = f // tf

    def body(x_ref, wg_ref, wu_ref, wd_ref, g_ref, b_ref, y_ref, pre_ref, gs_ref, us_ref, acc_ref):
        j = pl.program_id(1)
        xv = x_ref[...]
        xb = xv.astype(BF16)
        g = _dot(xb, wg_ref[...], NT)
        u = _dot(xb, wu_ref[...], NT)
        gs_ref[...] = g.astype(BF16)
        us_ref[...] = u.astype(BF16)
        part = _dot(g * _sig(g) * u, wd_ref[...], NN)

        @pl.when(j == 0)
        def _():
            acc_ref[...] = part

        @pl.when(j > 0)
        def _():
            acc_ref[...] += part

        @pl.when(j == nj - 1)
        def _():
            pre = ALPHA * xv + 0.5 * acc_ref[...]
            pre_ref[...] = pre
            y_ref[...] = _ln(pre, g_ref[...], b_ref[...])

    w_spec = pl.BlockSpec((tf, d), lambda i, j: (j, 0))
    x_spec = pl.BlockSpec((tm, d), lambda i, j: (i, 0))
    v_spec = pl.BlockSpec((1, d), lambda i, j: (0, 0))
    h_spec = pl.BlockSpec((tm, tf), lambda i, j: (i, j))
    return _call(
        body, (t // tm, nj), [x_spec, w_spec, w_spec, w_spec, v_spec, v_spec], [x_spec, x_spec, h_spec, h_spec],
        [jax.ShapeDtypeStruct((t, d), F32), jax.ShapeDtypeStruct((t, d), F32),
         jax.ShapeDtypeStruct((t, f), BF16), jax.ShapeDtypeStruct((t, f), BF16)],
        [pltpu.VMEM((tm, d), F32)], ("parallel", "arbitrary"), "ffn_fwd", [x, wgt, wut, wd, ln_g, ln_b], carry)


def _ffn_bwd(dpre, gs, us, wgt, wut, wd, tm, tf, carry=None):
    t, d = dpre.shape
    f = wgt.shape[0]
    nj = f // tf

    def body(dp_ref, gs_ref, us_ref, wg_ref, wu_ref, wd_ref, dx_ref, dg_ref, du_ref, hh_ref, acc_ref):
        j = pl.program_id(1)
        dp = dp_ref[...]
        dh = _dot(0.5 * dp, wd_ref[...], NT)
        g = gs_ref[...].astype(F32)
        u = us_ref[...].astype(F32)
        s = _sig(g)
        sl = g * s
        dg = (dh * u * (s * (1.0 + g * (1.0 - s)))).astype(BF16)
        du = (dh * sl).astype(BF16)
        dg_ref[...] = dg
        du_ref[...] = du
        hh_ref[...] = (0.5 * sl * u).astype(BF16)
        part = _dot(dg, wg_ref[...], NN) + _dot(du, wu_ref[...], NN)

        @pl.when(j == 0)
        def _():
            acc_ref[...] = part

        @pl.when(j > 0)
        def _():
            acc_ref[...] += part

        @pl.when(j == nj - 1)
        def _():
            dx_ref[...] = ALPHA * dp + acc_ref[...]

    w_spec = pl.BlockSpec((tf, d), lambda i, j: (j, 0))
    x_spec = pl.BlockSpec((tm, d), lambda i, j: (i, 0))
    h_spec = pl.BlockSpec((tm, tf), lambda i, j: (i, j))
    return _call(
        body, (t // tm, nj), [x_spec, h_spec, h_spec, w_spec, w_spec, w_spec], [x_spec, h_spec, h_spec, h_spec],
        [jax.ShapeDtypeStruct((t, d), F32)] + [jax.ShapeDtypeStruct((t, f), BF16)] * 3,
        [pltpu.VMEM((tm, d), F32)], ("parallel", "arbitrary"), "ffn_bwd", [dpre, gs, us, wgt, wut, wd], carry)


def _scan8(a_ref, b_ref, out_ref, t, reverse=False):
    w = out_ref.shape[-1]
    sub = lax.broadcasted_iota(jnp.int32, (8, w), 0)

    def step(g, carry):
        r0 = pl.multiple_of((t // 8 - 1 - g if reverse else g) * 8, 8)
        bv = b_ref[pl.ds(r0, 8), :]
        av = None if a_ref is None else a_ref[pl.ds(r0, 8), :]
        for s in (1, 2, 4):
            ok = (sub < 8 - s) if reverse else (sub >= s)
            shift = 8 - s if reverse else s
            b_sh = jnp.where(ok, pltpu.roll(bv, shift, 0), 0.0)
            if av is None:
                bv = bv + b_sh
            else:
                bv = av * b_sh + bv
                av = av * jnp.where(ok, pltpu.roll(av, shift, 0), 1.0)
        h = bv + carry if av is None else bv + av * carry
        out_ref[pl.ds(r0, 8), :] = h
        return jnp.sum(jnp.where(sub == (0 if reverse else 7), h, 0.0), axis=0, keepdims=True)

    lax.fori_loop(0, t // 8, step, jnp.zeros((1, w), F32))


def _rg_local(xa, wa, ba, wx, bx, lam):
    r = _sig(_bdot(xa, wa) + ba)
    i = _sig(_bdot(xa, wx) + bx)
    log_a = -RG_C * r * _softplus(-lam)
    a = jnp.exp(log_a)
    mult = jnp.sqrt(-jnp.tanh(log_a) * (a * a + 1.0))
    return a, mult * (i * xa)


def _conv_taps(ext, n):
    return [ext[8:, :]] + [pltpu.roll(ext, s, 0)[8:, :] for s in (1, 2, 3)]


def _rg_fwd(z, cw, cb, wa, ba, wx, bx, lam):
    t = z.shape[0]
    cr = _tile(t, 256)
    nb = D_A // LANES

    def body(ax_ref, ag_ref, cw_ref, cb_ref, wa_ref, ba_ref, wx_ref, bx_ref, lam_ref, out_ref, h_ref, axp, a_s, b_s):
        axp[pl.ds(0, 8), :] = jnp.zeros((8, LANES), F32)
        pltpu.sync_copy(ax_ref, axp.at[pl.ds(8, t)])
        w = [cw_ref[pl.ds(k, 1), :] for k in range(CONV_WIDTH)]

        def chunk(c, carry):
            t0 = pl.multiple_of(c * cr, cr)
            taps = _conv_taps(axp[pl.ds(t0, cr + 8), :], cr)
            xa = cb_ref[...] + w[3] * taps[0] + w[2] * taps[1] + w[1] * taps[2] + w[0] * taps[3]
            a, gated = _rg_local(xa, wa_ref[...], ba_ref[...], wx_ref[...], bx_ref[...], lam_ref[...])
            a_s[pl.ds(t0, cr), :] = a
            b_s[pl.ds(t0, cr), :] = gated
            return carry

        lax.fori_loop(0, t // cr, chunk, 0)

        _scan8(a_s, b_s, h_ref, t)

        def fin(c, carry):
            t0 = pl.multiple_of(c * cr, cr)
            out_ref[pl.ds(t0, cr), :] = _gelu(ag_ref[pl.ds(t0, cr), :]) * h_ref[pl.ds(t0, cr), :]
            return carry

        lax.fori_loop(0, t // cr, fin, 0)

    col = lambda off: pl.BlockSpec((t, LANES), lambda b: (0, off + b))
    vec = pl.BlockSpec((1, LANES), lambda b: (0, b))
    mat = pl.BlockSpec((None, LANES, LANES), lambda b: (b, 0, 0))
    return pl.pallas_call(
        body, grid=(nb,),
        in_specs=[col(0), col(nb), pl.BlockSpec((CONV_WIDTH, LANES), lambda b: (0, b)), vec, mat, vec, mat, vec, vec],
        out_specs=[col(0), col(0)],
        out_shape=[jax.ShapeDtypeStruct((t, D_A), F32), jax.ShapeDtypeStruct((t, D_A), F32)],
        scratch_shapes=[pltpu.VMEM((t + 8, LANES), F32), pltpu.VMEM((t, LANES), F32), pltpu.VMEM((t, LANES), F32)],
        compiler_params=pltpu.CompilerParams(dimension_semantics=("arbitrary",)), name="rglru_fwd")(
            z, z, cw, cb, wa, ba, wx, bx, lam)


def _rg_bwd(z, h, dout, cw, cb, wa, ba, wx, bx, lam):
    t = z.shape[0]
    cr = _tile(t, 256)
    nb = D_A // LANES

    def body(ax_ref, ag_ref, h_ref, do_ref, cw_ref, cb_ref, wa_ref, ba_ref, wx_ref, bx_ref, lam_ref,
             dax_ref, dag_ref, dcw_ref, dcb_ref, dwa_ref, dba_ref, dwx_ref, dbx_ref, dlam_ref,
             axp, hp, xa_s, a_s, g_s, dxa_s, u_s):
        zero8 = jnp.zeros((8, LANES), F32)
        axp[pl.ds(0, 8), :] = zero8
        hp[pl.ds(0, 8), :] = zero8
        dxa_s[pl.ds(t, 8), :] = zero8
        u_s[pl.ds(t, 8), :] = zero8
        pltpu.sync_copy(ax_ref, axp.at[pl.ds(8, t)])
        pltpu.sync_copy(h_ref, hp.at[pl.ds(8, t)])
        w = [cw_ref[pl.ds(k, 1), :] for k in range(CONV_WIDTH)]
        for ref in (dcw_ref, dcb_ref, dwa_ref, dba_ref, dwx_ref, dbx_ref, dlam_ref):
            ref[...] = jnp.zeros(ref.shape, F32)

        def p1(c, carry):
            t0 = pl.multiple_of(c * cr, cr)
            taps = _conv_taps(axp[pl.ds(t0, cr + 8), :], cr)
            xa = cb_ref[...] + w[3] * taps[0] + w[2] * taps[1] + w[1] * taps[2] + w[0] * taps[3]
            a, _ = _rg_local(xa, wa_ref[...], ba_ref[...], wx_ref[...], bx_ref[...], lam_ref[...])
            xa_s[pl.ds(t0, cr), :] = xa
            a_s[pl.ds(t0, cr), :] = a
            ag = ag_ref[pl.ds(t0, cr), :]
            dov = do_ref[pl.ds(t0, cr), :]
            gel, pull = jax.vjp(_gelu, ag)
            g_s[pl.ds(t0, cr), :] = dov * gel
            u_s[pl.ds(t0, cr), :] = a * (dov * gel)
            dag_ref[pl.ds(t0, cr), :] = pull(dov * h_ref[pl.ds(t0, cr), :])[0]
            return carry

        lax.fori_loop(0, t // cr, p1, 0)
        _scan8(a_s, u_s, u_s, t, reverse=True)

        def p3(c, carry):
            t0 = pl.multiple_of(c * cr, cr)
            g = g_s[pl.ds(t0, cr), :] + pltpu.roll(u_s[pl.ds(t0, cr + 8), :], cr + 7, 0)[:cr, :]
            h_prev = pltpu.roll(hp[pl.ds(t0, cr + 8), :], 1, 0)[8:, :]
            _, pull = jax.vjp(_rg_local, xa_s[pl.ds(t0, cr), :], wa_ref[...], ba_ref[...], wx_ref[...], bx_ref[...],
                              lam_ref[...])
            dxa, dwa, dba, dwx, dbx, dlam = pull((g * h_prev, g))
            dxa_s[pl.ds(t0, cr), :] = dxa
            dwa_ref[...] += dwa
            dba_ref[...] += dba
            dwx_ref[...] += dwx
            dbx_ref[...] += dbx
            dlam_ref[...] += dlam
            return carry

        lax.fori_loop(0, t // cr, p3, 0)

        def p4(c, carry):
            t0 = pl.multiple_of(c * cr, cr)
            ext = dxa_s[pl.ds(t0, cr + 8), :]
            n = cr + 8
            ahead = [ext[:cr, :]] + [pltpu.roll(ext, n - s, 0)[:cr, :] for s in (1, 2, 3)]
            dax_ref[pl.ds(t0, cr), :] = w[3] * ahead[0] + w[2] * ahead[1] + w[1] * ahead[2] + w[0] * ahead[3]
            taps = _conv_taps(axp[pl.ds(t0, cr + 8), :], cr)
            dxa = ahead[0]
            for k in range(CONV_WIDTH):
                dcw_ref[pl.ds(k, 1), :] += jnp.sum(dxa * taps[CONV_WIDTH - 1 - k], axis=0, keepdims=True)
            dcb_ref[...] += jnp.sum(dxa, axis=0, keepdims=True)
            return carry

        lax.fori_loop(0, t // cr, p4, 0)

    col = lambda off: pl.BlockSpec((t, LANES), lambda b: (0, off + b))
    vec = pl.BlockSpec((1, LANES), lambda b: (0, b))
    mat = pl.BlockSpec((None, LANES, LANES), lambda b: (b, 0, 0))
    cws = pl.BlockSpec((CONV_WIDTH, LANES), lambda b: (0, b))
    sds = jax.ShapeDtypeStruct
    return pl.pallas_call(
        body, grid=(nb,),
        in_specs=[col(0), col(nb), col(0), col(0), cws, vec, mat, vec, mat, vec, vec],
        out_specs=[col(0), col(0), cws, vec, mat, vec, mat, vec, vec],
        out_shape=[sds((t, D_A), F32), sds((t, D_A), F32), sds((CONV_WIDTH, D_A), F32), sds((1, D_A), F32),
                   sds((nb, LANES, LANES), F32), sds((1, D_A), F32), sds((nb, LANES, LANES), F32), sds((1, D_A), F32),
                   sds((1, D_A), F32)],
        scratch_shapes=[pltpu.VMEM((t + 8, LANES), F32), pltpu.VMEM((t + 8, LANES), F32), pltpu.VMEM((t, LANES), F32),
                        pltpu.VMEM((t, LANES), F32), pltpu.VMEM((t, LANES), F32), pltpu.VMEM((t + 8, LANES), F32),
                        pltpu.VMEM((t + 8, LANES), F32)],
        compiler_params=pltpu.CompilerParams(dimension_semantics=("arbitrary",)), name="rglru_bwd")(
            z, z, h, dout, cw, cb, wa, ba, wx, bx, lam)


def _fgate_fwd(z, bf):
    t = z.shape[0]

    def body(zf_ref, bf_ref, c_ref):
        c_ref[...] = -_softplus(-(zf_ref[...] + bf_ref[...]))
        _scan8(None, c_ref, c_ref, t)

    return pl.pallas_call(
        body, grid=(1,), in_specs=[pl.BlockSpec((t, LANES), lambda i: (0, Z_F // LANES)), pl.BlockSpec((1, LANES), lambda i: (0, 0))],
        out_specs=pl.BlockSpec((t, LANES), lambda i: (0, 0)), out_shape=jax.ShapeDtypeStruct((t, LANES), F32),
        compiler_params=pltpu.CompilerParams(dimension_semantics=("arbitrary",)), name="fgate_fwd")(z, bf)


def _fgate_bwd(z, bf, dc):
    t = z.shape[0]

    def body(zf_ref, bf_ref, dc_ref, dz_ref, db_ref):
        _scan8(None, dc_ref, dz_ref, t, reverse=True)
        dz = dz_ref[...] * _sig(-(zf_ref[...] + bf_ref[...]))
        dz_ref[...] = dz
        db_ref[...] = jnp.sum(dz, axis=0, keepdims=True)

    return pl.pallas_call(
        body, grid=(1,),
        in_specs=[pl.BlockSpec((t, LANES), lambda i: (0, Z_F // LANES)), pl.BlockSpec((1, LANES), lambda i: (0, 0)),
                  pl.BlockSpec((t, LANES), lambda i: (0, 0))],
        out_specs=[pl.BlockSpec((t, LANES), lambda i: (0, 0)), pl.BlockSpec((1, LANES), lambda i: (0, 0))],
        out_shape=[jax.ShapeDtypeStruct((t, LANES), F32), jax.ShapeDtypeStruct((1, LANES), F32)],
        compiler_params=pltpu.CompilerParams(dimension_semantics=("arbitrary",)), name="fgate_bwd")(z, bf, dc)


def _cast_rows(src_ref, dst_ref, t, rows, fn):
    def cp(c, carry):
        r0 = pl.multiple_of(c * rows, rows)
        dst_ref[pl.ds(r0, rows), :] = fn(src_ref[pl.ds(r0, rows), :]).astype(dst_ref.dtype)
        return carry

    lax.fori_loop(0, t // rows, cp, 0)


def _attn_groups(t):
    tq = _tile(t, 256)
    nq = t // tq
    grp = 4 if nq % 4 == 0 else 1
    return tq, nq, grp


def _attn_fwd(z, crow, carry=None):
    t = z.shape[0]
    tq, nq, grp = _attn_groups(t)
    tk = grp * tq
    scale = HEAD_DIM ** -0.5

    def body(q_ref, k_ref, v_ref, cr_ref, o_ref, lse_ref, kb_s, vb_s):
        lane = lax.broadcasted_iota(jnp.int32, (1, LANES), 1)
        hmask = [(lane // HEAD_DIM) == hh for hh in range(2)]
        _cast_rows(k_ref, kb_s, t, tq, lambda v: v)
        _cast_rows(v_ref, vb_s, t, tq, lambda v: v)

        def qblock(g, r):
            q0 = pl.multiple_of((g * grp + r) * tq, tq)
            qv = q_ref[pl.ds(q0, tq), :] * scale
            qa = [jnp.where(hmask[hh], qv, 0.0).astype(BF16) for hh in range(2)]

            def update(st, k0, width, off):
                kb = kb_s[pl.ds(k0, width), :]
                vb = vb_s[pl.ds(k0, width), :]
                new = []
                for hh in range(2):
                    m, l, acc = st[hh]
                    s = _dot(qa[hh], kb, NT) - cr_ref[hh, :, pl.ds(k0, width)]
                    if off is not None:
                        keep = (lax.broadcasted_iota(jnp.int32, (tq, width), 0) + off
                                >= lax.broadcasted_iota(jnp.int32, (tq, width), 1))
                        s = jnp.where(keep, s, NEG)
                    m_new = jnp.maximum(m, jnp.max(s, axis=-1, keepdims=True))
                    p = jnp.exp(s - m_new)
                    corr = jnp.exp(m - m_new)
                    new.append((m_new, corr * l + jnp.sum(p, axis=-1, keepdims=True), corr * acc + _dot(p, vb, NN)))
                return tuple(new)

            one = (jnp.full((tq, 1), NEG, F32), jnp.zeros((tq, 1), F32), jnp.zeros((tq, LANES), F32))
            st = lax.fori_loop(0, g, lambda j, st: update(st, pl.multiple_of(j * tk, tk), tk, None), (one, one))
            st = update(st, pl.multiple_of(g * tk, tk), (r + 1) * tq, r * tq)
            o_ref[pl.ds(q0, tq), :] = jnp.where(hmask[0], st[0][2] / st[0][1], st[1][2] / st[1][1])
            for hh in range(2):
                lse_ref[hh, pl.ds(q0, tq), :] = st[hh][0] + jnp.log(st[hh][1])

        def group(g, carry):
            for r in range(grp):
                qblock(g, r)
            return carry

        lax.fori_loop(0, nq // grp, group, 0)

    base = 2 * D_A // LANES
    nh = D_B // LANES
    col = lambda off: pl.BlockSpec((t, LANES), lambda p: (0, off + p))
    return _call(
        body, (nh,), [col(base), col(base + nh), col(base + 2 * nh), pl.BlockSpec((2, 1, t), lambda p: (p, 0, 0))],
        [col(0), pl.BlockSpec((2, t, 1), lambda p: (p, 0, 0))],
        [jax.ShapeDtypeStruct((t, D_B), F32), jax.ShapeDtypeStruct((N_HEADS, t, 1), F32)],
        [pltpu.VMEM((t, LANES), BF16), pltpu.VMEM((t, LANES), BF16)], ("parallel",), "attn_fwd", [z, z, z, crow], carry)


def _attn_bwd(z, crow, lse, do, carry=None):
    t = z.shape[0]
    tq, nq, grp = _attn_groups(t)
    tw = grp * tq
    scale = HEAD_DIM ** -0.5

    def body(q_ref, k_ref, v_ref, cr_ref, lse_ref, do_ref, dq_ref, dk_ref, dv_ref, dc_ref, qa_s, da_s, kb_s, vb_s, dl_s):
        lane = lax.broadcasted_iota(jnp.int32, (1, LANES), 1)
        hmask = [(lane // HEAD_DIM) == hh for hh in range(2)]
        _cast_rows(k_ref, kb_s, t, tq, lambda v: v)
        _cast_rows(v_ref, vb_s, t, tq, lambda v: v)
        for hh in range(2):
            _cast_rows(q_ref, qa_s.at[hh], t, tq, lambda v, hh=hh: jnp.where(hmask[hh], v * scale, 0.0))
            _cast_rows(do_ref, da_s.at[hh], t, tq, lambda v, hh=hh: jnp.where(hmask[hh], v, 0.0))
        _cast_rows(q_ref, dq_ref, t, tq, lambda v: jnp.zeros_like(v))

        def probs(hh, q0, nq_rows, k0, nk_rows, off):
            s = _dot(qa_s[hh, pl.ds(q0, nq_rows), :], kb_s[pl.ds(k0, nk_rows), :], NT) - cr_ref[hh, :, pl.ds(k0, nk_rows)]
            p = jnp.exp(s - lse_ref[hh, pl.ds(q0, nq_rows), :])
            if off is not None:
                keep = (lax.broadcasted_iota(jnp.int32, (nq_rows, nk_rows), 0) + off
                        >= lax.broadcasted_iota(jnp.int32, (nq_rows, nk_rows), 1))
                p = jnp.where(keep, p, 0.0)
            return p, _dot(da_s[hh, pl.ds(q0, nq_rows), :], vb_s[pl.ds(k0, nk_rows), :], NT)

        def delta(g, r):
            q0 = pl.multiple_of((g * grp + r) * tq, tq)

            def add(k0, width, off, acc):
                res = []
                for hh in range(2):
                    p, dp = probs(hh, q0, tq, k0, width, off)
                    res.append(acc[hh] + jnp.sum(p * dp, axis=-1, keepdims=True))
                return tuple(res)

            zcol = jnp.zeros((tq, 1), F32)
            acc = lax.fori_loop(0, g, lambda j, acc: add(pl.multiple_of(j * tw, tw), tw, None, acc), (zcol, zcol))
            acc = add(pl.multiple_of(g * tw, tw), (r + 1) * tq, r * tq, acc)
            for hh in range(2):
                dl_s[hh, pl.ds(q0, tq), :] = acc[hh]

        def delta_group(g, carry):
            for r in range(grp):
                delta(g, r)
            return carry

        lax.fori_loop(0, nq // grp, delta_group, 0)

        def kblock(g, r):
            k0 = pl.multiple_of((g * grp + r) * tq, tq)
            kb = kb_s[pl.ds(k0, tq), :]

            def upd(q0, height, off, st):
                dk, dv, dc = st[0], st[1], [st[2], st[3]]
                dqs = []
                for hh in range(2):
                    p, dp = probs(hh, q0, height, k0, tq, off)
                    ds = p * (dp - dl_s[hh, pl.ds(q0, height), :])
                    dv = dv + _dot(p, da_s[hh, pl.ds(q0, height), :], TN)
                    dk = dk + _dot(ds, qa_s[hh, pl.ds(q0, height), :], TN)
                    dqs.append(_dot(ds, kb, NN))
                    dc[hh] = dc[hh] - jnp.sum(ds, axis=0, keepdims=True)
                dq_ref[pl.ds(q0, height), :] += jnp.where(hmask[0], dqs[0], dqs[1]) * scale
                return dk, dv, dc[0], dc[1]

            zero = jnp.zeros((tq, LANES), F32)
            zrow = jnp.zeros((1, tq), F32)
            st = upd(k0, (grp - r) * tq, 0, (zero, zero, zrow, zrow))
            st = lax.fori_loop(g + 1, nq // grp, lambda i, st: upd(pl.multiple_of(i * tw, tw), tw, None, st), st)
            dk_ref[pl.ds(k0, tq), :] = st[0]
            dv_ref[pl.ds(k0, tq), :] = st[1]
            for hh in range(2):
                dc_ref[hh, :, pl.ds(k0, tq)] = st[2 + hh]

        def kgroup(g, carry):
            for r in range(grp):
                kblock(g, r)
            return carry

        lax.fori_loop(0, nq // grp, kgroup, 0)

    base = 2 * D_A // LANES
    nh = D_B // LANES
    col = lambda off: pl.BlockSpec((t, LANES), lambda p: (0, off + p))
    ccs = pl.BlockSpec((2, t, 1), lambda p: (p, 0, 0))
    crs = pl.BlockSpec((2, 1, t), lambda p: (p, 0, 0))
    return _call(
        body, (nh,), [col(base), col(base + nh), col(base + 2 * nh), crs, ccs, col(0)], [col(0), col(0), col(0), crs],
        [jax.ShapeDtypeStruct((t, D_B), F32)] * 3 + [jax.ShapeDtypeStruct((N_HEADS, 1, t), F32)],
        [pltpu.VMEM((2, t, LANES), BF16), pltpu.VMEM((2, t, LANES), BF16), pltpu.VMEM((t, LANES), BF16),
         pltpu.VMEM((t, LANES), BF16), pltpu.VMEM((2, t, 1), F32)], ("parallel",), "attn_bwd", [z, z, z, crow, lse, do], carry)


def _s5_disc(a_re, a_im, log_dt, b_re, b_im):
    dt = jnp.exp(log_dt)
    mag = jnp.exp(a_re * dt)
    ar = mag * jnp.cos(a_im * dt)
    ai = mag * jnp.sin(a_im * dt)
    den = a_re * a_re + a_im * a_im
    kr = ((ar - 1.0) * a_re + ai * a_im) / den
    ki = (ai * a_re - (ar - 1.0) * a_im) / den
    kr3, ki3 = kr[:, None, :], ki[:, None, :]
    return ar, ai, kr3 * b_re - ki3 * b_im, kr3 * b_im + ki3 * b_re


def _s5_prep(a_re, a_im, log_dt, b_re, b_im):
    g, p = a_re.shape
    gc = b_re.shape[1]

    def body(*refs):
        res = _s5_disc(*[r[...] for r in refs[:5]])
        for o_ref, v in zip(refs[5:], res):
            o_ref[...] = v

    sds = jax.ShapeDtypeStruct
    return pl.pallas_call(body, out_shape=[sds((g, p), F32), sds((g, p), F32), sds((g, gc, p), F32), sds((g, gc, p), F32)],
                          name="s5_prep")(a_re, a_im, log_dt, b_re, b_im)


def _s5_prep_bwd(a_re, a_im, log_dt, b_re, b_im, d_ar, d_ai, d_br, d_bi):
    ins = (a_re, a_im, log_dt, b_re, b_im)

    def body(*refs):
        vals = [r[...] for r in refs[:5]]
        cts = tuple(r[...] for r in refs[5:9])
        _, pull = jax.vjp(_s5_disc, *vals)
        for o_ref, v in zip(refs[9:], pull(cts)):
            o_ref[...] = v

    return pl.pallas_call(body, out_shape=[jax.ShapeDtypeStruct(a.shape, F32) for a in ins], name="s5_prep_bwd")(
        *ins, d_ar, d_ai, d_br, d_bi)


def _s5_scan_rows(t, ar, ai, hr_s, hi_s, off, reverse):
    n = ar.shape[1]
    if reverse:
        ai = -ai
    sub = lax.broadcasted_iota(jnp.int32, (8, n), 0)
    cmul = lambda xr, xi, yr, yi: (xr * yr - xi * yi, xr * yi + xi * yr)
    pw = [(ar, ai)]
    for _ in range(7):
        pw.append(cmul(*pw[-1], ar, ai))
    pr = jnp.zeros((8, n), F32)
    pi = jnp.zeros((8, n), F32)
    for r in range(8):
        k = 7 - r if reverse else r
        pr = jnp.where(sub == r, pw[k][0], pr)
        pi = jnp.where(sub == r, pw[k][1], pi)

    def step(g, carry):
        cr, ci = carry
        r0 = pl.multiple_of(off + (t // 8 - 1 - g if reverse else g) * 8, 8)
        br = hr_s[pl.ds(r0, 8), :]
        bi = hi_s[pl.ds(r0, 8), :]
        for s in (1, 2, 4):
            ok = (sub < 8 - s) if reverse else (sub >= s)
            shift = 8 - s if reverse else s
            sr = jnp.where(ok, pltpu.roll(br, shift, 0), 0.0)
            si = jnp.where(ok, pltpu.roll(bi, shift, 0), 0.0)
            mr, mi = cmul(pw[s - 1][0], pw[s - 1][1], sr, si)
            br, bi = br + mr, bi + mi
        mr, mi = cmul(pr, pi, cr, ci)
        br, bi = br + mr, bi + mi
        hr_s[pl.ds(r0, 8), :] = br
        hi_s[pl.ds(r0, 8), :] = bi
        edge = sub == (0 if reverse else 7)
        return (jnp.sum(jnp.where(edge, br, 0.0), axis=0, keepdims=True),
                jnp.sum(jnp.where(edge, bi, 0.0), axis=0, keepdims=True))

    zero = jnp.zeros((1, n), F32)
    lax.fori_loop(0, t // 8, step, (zero, zero))


def _s5_fwd(z, bd_re, bd_im, ab_re, ab_im, cd_re, cd_im, dvec, carry=None):
    t = z.shape[0]
    cr = _tile(t, 256)
    ns = N_STATE // 2

    def body(u_ref, br_ref, bi_ref, ar_ref, ai_ref, cre_ref, cim_ref, d_ref, y_ref, hr_s, hi_s):
        def p1(c, carry):
            t0 = pl.multiple_of(c * cr, cr)
            u = u_ref[pl.ds(t0, cr), :]
            hr_s[pl.ds(t0, cr), :] = _dot(u, br_ref[...], NN)
            hi_s[pl.ds(t0, cr), :] = _dot(u, bi_ref[...], NN)
            return carry

        lax.fori_loop(0, t // cr, p1, 0)
        _s5_scan_rows(t, ar_ref[...], ai_ref[...], hr_s, hi_s, 0, False)

        def p3(c, carry):
            t0 = pl.multiple_of(c * cr, cr)
            y_ref[pl.ds(t0, cr), :] = (_dot(hr_s[pl.ds(t0, cr), :], cre_ref[...], NN)
                                       - _dot(hi_s[pl.ds(t0, cr), :], cim_ref[...], NN)
                                       + d_ref[...] * u_ref[pl.ds(t0, cr), :])
            return carry

        lax.fori_loop(0, t // cr, p3, 0)

    blk = lambda r, c: pl.BlockSpec((None, r, c), lambda b: (b, 0, 0))
    return _call(
        body, (2,),
        [pl.BlockSpec((t, LANES), lambda b: (0, Z_U // LANES + b)), blk(LANES, ns), blk(LANES, ns), blk(1, ns),
         blk(1, ns), blk(ns, LANES), blk(ns, LANES), pl.BlockSpec((1, LANES), lambda b: (0, b))],
        [pl.BlockSpec((t, LANES), lambda b: (0, b))], [jax.ShapeDtypeStruct((t, D_C), F32)],
        [pltpu.VMEM((t, ns), F32), pltpu.VMEM((t, ns), F32)], ("arbitrary",), "s5_fwd",
        [z, bd_re, bd_im, ab_re, ab_im, cd_re, cd_im, dvec], carry)


def _s5_bwd(z, dy, bd_re, bd_im, ab_re, ab_im, cd_re, cd_im, dvec):
    t = z.shape[0]
    cr = _tile(t, 256)
    ns = N_STATE // 2

    def body(u_ref, dy_ref, br_ref, bi_ref, ar_ref, ai_ref, cre_ref, cim_ref, d_ref,
             du_ref, dbr_ref, dbi_ref, dar_ref, dai_ref, dcre_ref, dcim_ref, dd_ref, hr_s, hi_s, gr_s, gi_s):
        zero8 = jnp.zeros((8, ns), F32)
        hr_s[pl.ds(0, 8), :] = zero8
        hi_s[pl.ds(0, 8), :] = zero8
        for ref in (dbr_ref, dbi_ref, dar_ref, dai_ref, dcre_ref, dcim_ref, dd_ref):
            ref[...] = jnp.zeros(ref.shape, F32)

        def p1(c, carry):
            t0 = pl.multiple_of(c * cr, cr)
            u = u_ref[pl.ds(t0, cr), :]
            hr_s[pl.ds(t0 + 8, cr), :] = _dot(u, br_ref[...], NN)
            hi_s[pl.ds(t0 + 8, cr), :] = _dot(u, bi_ref[...], NN)
            return carry

        lax.fori_loop(0, t // cr, p1, 0)
        _s5_scan_rows(t, ar_ref[...], ai_ref[...], hr_s, hi_s, 8, False)

        def p3(c, carry):
            t0 = pl.multiple_of(c * cr, cr)
            dyv = dy_ref[pl.ds(t0, cr), :]
            u = u_ref[pl.ds(t0, cr), :]
            gr_s[pl.ds(t0, cr), :] = _dot(dyv, cre_ref[...], NT)
            gi_s[pl.ds(t0, cr), :] = -_dot(dyv, cim_ref[...], NT)
            dcre_ref[...] += _dot(hr_s[pl.ds(t0 + 8, cr), :], dyv, TN)
            dcim_ref[...] -= _dot(hi_s[pl.ds(t0 + 8, cr), :], dyv, TN)
            dd_ref[...] += jnp.sum(dyv * u, axis=0, keepdims=True)
            du_ref[pl.ds(t0, cr), :] = dyv * d_ref[...]
            return carry

        lax.fori_loop(0, t // cr, p3, 0)
        _s5_scan_rows(t, ar_ref[...], ai_ref[...], gr_s, gi_s, 0, True)

        def p5(c, carry):
            t0 = pl.multiple_of(c * cr, cr)
            u = u_ref[pl.ds(t0, cr), :]
            gr = gr_s[pl.ds(t0, cr), :]
            gi = gi_s[pl.ds(t0, cr), :]
            dbr_ref[...] += _dot(u, gr, TN)
            dbi_ref[...] += _dot(u, gi, TN)
            du_ref[pl.ds(t0, cr), :] += _dot(gr, br_ref[...], NT) + _dot(gi, bi_ref[...], NT)
            hpr = pltpu.roll(hr_s[pl.ds(t0, cr + 8), :], 1, 0)[8:, :]
            hpi = pltpu.roll(hi_s[pl.ds(t0, cr + 8), :], 1, 0)[8:, :]
            dar_ref[...] += jnp.sum(gr * hpr + gi * hpi, axis=0, keepdims=True)
            dai_ref[...] += jnp.sum(gi * hpr - gr * hpi, axis=0, keepdims=True)
            return carry

        lax.fori_loop(0, t // cr, p5, 0)

    blk = lambda r, c: pl.BlockSpec((None, r, c), lambda b: (b, 0, 0))
    ucol = pl.BlockSpec((t, LANES), lambda b: (0, Z_U // LANES + b))
    ycol = pl.BlockSpec((t, LANES), lambda b: (0, b))
    dsp = pl.BlockSpec((1, LANES), lambda b: (0, b))
    sds = jax.ShapeDtypeStruct
    return pl.pallas_call(
        body, grid=(2,),
        in_specs=[ucol, ycol, blk(LANES, ns), blk(LANES, ns), blk(1, ns), blk(1, ns), blk(ns, LANES), blk(ns, LANES), dsp],
        out_specs=[ycol, blk(LANES, ns), blk(LANES, ns), blk(1, ns), blk(1, ns), blk(ns, LANES), blk(ns, LANES), dsp],
        out_shape=[sds((t, D_C), F32), sds((2, LANES, ns), F32), sds((2, LANES, ns), F32), sds((2, 1, ns), F32),
                   sds((2, 1, ns), F32), sds((2, ns, LANES), F32), sds((2, ns, LANES), F32), sds((1, D_C), F32)],
        scratch_shapes=[pltpu.VMEM((t + 8, ns), F32), pltpu.VMEM((t + 8, ns), F32), pltpu.VMEM((t, ns), F32),
                        pltpu.VMEM((t, ns), F32)],
        compiler_params=pltpu.CompilerParams(dimension_semantics=("arbitrary",)), name="s5_bwd")(
            z, dy, bd_re, bd_im, ab_re, ab_im, cd_re, cd_im, dvec)


def _mix_out(out_a, out_b, yc, x1, ga, gb, gc, wglu, wout, ln_g, ln_b):
    yg = _gelu(yc)
    out_c = yg * _sig(_bdot(yg, wglu))
    o = jnp.concatenate([_rms(out_a, ga), _rms(out_b, gb), _rms(out_c, gc)], axis=-1)
    return (_ln(ALPHA * x1 + _bdot(o, wout), ln_g, ln_b),)


def _ln_only(pre, g, b):
    return (_ln(pre, g, b),)


def _loss_head(y, target, tm):
    t, d = y.shape

    def body(y_ref, t_ref, dy_ref, l_ref):
        i = pl.program_id(0)
        e = y_ref[...] - t_ref[...]
        dy_ref[...] = e * (1.0 / d)
        part = 0.5 * jnp.sum(jnp.sum(e * e, axis=-1, keepdims=True) * (1.0 / d), axis=0, keepdims=True)
        row = jnp.where(lax.broadcasted_iota(jnp.int32, (1, LANES), 1) == 0, part, 0.0)

        @pl.when(i == 0)
        def _():
            l_ref[...] = row

        @pl.when(i > 0)
        def _():
            l_ref[...] += row

    spec = pl.BlockSpec((tm, d), lambda i: (i, 0))
    return pl.pallas_call(
        body, grid=(t // tm,), in_specs=[spec, spec], out_specs=[spec, pl.BlockSpec((1, LANES), lambda i: (0, 0))],
        out_shape=[jax.ShapeDtypeStruct((t, d), F32), jax.ShapeDtypeStruct((1, LANES), F32)],
        compiler_params=pltpu.CompilerParams(dimension_semantics=("arbitrary",)), name="loss_head")(y, target)


def _adamw(w, g, m, v, name):
    r, c = w.shape
    tr = r
    for cand in (512, 256, 352, 128):
        if r % cand == 0:
            tr = cand
            break

    def body(w_ref, g_ref, m_ref, v_ref, d_ref, nm_ref, nv_ref):
        gv = g_ref[...]
        mn = ADAM_B1 * m_ref[...] + (1.0 - ADAM_B1) * gv
        vn = ADAM_B2 * v_ref[...] + (1.0 - ADAM_B2) * (gv * gv)
        m_hat = mn / (1.0 - ADAM_B1 ** ADAM_STEP)
        v_hat = vn / (1.0 - ADAM_B2 ** ADAM_STEP)
        d_ref[...] = -ADAM_LR * (m_hat / (jnp.sqrt(v_hat) + ADAM_EPS) + ADAM_WD * w_ref[...])
        nm_ref[...] = mn
        nv_ref[...] = vn

    spec = pl.BlockSpec((tr, c), lambda i: (i, 0))
    return pl.pallas_call(
        body, grid=(r // tr,), in_specs=[spec] * 4, out_specs=[spec] * 3,
        out_shape=[jax.ShapeDtypeStruct((r, c), F32)] * 3,
        compiler_params=pltpu.CompilerParams(dimension_semantics=("parallel",)), name=name)(w, g, m, v)


def _row_tile(r):
    for cand in (512, 448, 352, 256, 128):
        if r % cand == 0:
            return cand
    return r


def _pair_add(a, b, idx, name, out_dtype):
    _, r, w = a.shape
    tr = _row_tile(r)

    def body(i_ref, a_ref, b_ref, o_ref):
        o_ref[...] = (a_ref[...].astype(F32) + b_ref[...].astype(F32)).astype(o_ref.dtype)

    grid_spec = pltpu.PrefetchScalarGridSpec(
        num_scalar_prefetch=1, grid=(4, r // tr),
        in_specs=[pl.BlockSpec((None, tr, w), lambda q, i, s: (2 * q + s[0], i, 0)),
                  pl.BlockSpec((None, tr, w), lambda q, i, s: (q, i, 0))],
        out_specs=pl.BlockSpec((None, tr, w), lambda q, i, s: (q, i, 0)))
    return pl.pallas_call(body, grid_spec=grid_spec, out_shape=jax.ShapeDtypeStruct((4, r, w), out_dtype),
                          compiler_params=pltpu.CompilerParams(dimension_semantics=("parallel", "parallel")), name=name)(
                              idx, a, b)


def _quad_add(p, rb, idx, name):
    _, r, w = p.shape
    tr = _row_tile(r)

    def body(i_ref, p_ref, r0, r1, r2, o_ref):
        o_ref[...] = ((p_ref[...].astype(F32) + r0[...].astype(F32)) + r1[...].astype(F32)) + r2[...].astype(F32)

    grid_spec = pltpu.PrefetchScalarGridSpec(
        num_scalar_prefetch=1, grid=(r // tr,),
        in_specs=[pl.BlockSpec((None, tr, w), lambda i, s: (s[0], i, 0))]
        + [pl.BlockSpec((None, tr, w), functools.partial(lambda i, s, k: (k, i, 0), k=k)) for k in range(3)],
        out_specs=pl.BlockSpec((tr, w), lambda i, s: (i, 0)))
    return pl.pallas_call(body, grid_spec=grid_spec, out_shape=jax.ShapeDtypeStruct((r, w), F32),
                          compiler_params=pltpu.CompilerParams(dimension_semantics=("parallel",)), name=name)(
                              idx, p, rb, rb, rb)


def _gather_small(buf):
    def plan(x, y, c, ins, outs):
        dst = outs[0].at[4 * x + 2 * y + c]
        res = [(ins[0], dst, None)]
        for rel in range(1, 8):
            res.append((ins[0], dst, (x ^ (rel >> 2), y ^ ((rel >> 1) & 1), c ^ (rel & 1))))
        return res

    return _Carry([buf], [jax.ShapeDtypeStruct((8,) + buf.shape, buf.dtype)], {}, plan, 8)


def _sum_small(allb):
    _, r, w = allb.shape

    def body(a_ref, o_ref):
        s = a_ref[0]
        for k in range(1, 8):
            s = s + a_ref[k]
        o_ref[...] = s

    return pl.pallas_call(body, out_shape=jax.ShapeDtypeStruct((r, w), F32), name="ar_small_sum")(allb)


def _pad_rows(a, rows):
    return jnp.pad(a, ((0, rows - a.shape[0]), (0, 0)))


def _pack_small(arrs):
    rows, tail = [], []
    for a in arrs:
        if not tail and a.size % LANES == 0:
            rows.append(a.reshape(-1, LANES))
        else:
            tail.append(a.reshape(-1))
    n_rows = sum(r.shape[0] for r in rows)
    n_tail = sum(int(v.size) for v in tail)
    tail_rows = -(-n_tail // LANES)
    total_rows = n_rows + tail_rows + (-(n_rows + tail_rows)) % 8
    if tail:
        tail.append(jnp.zeros((tail_rows * LANES - n_tail,), F32))
        rows.append(jnp.concatenate(tail).reshape(tail_rows, LANES))
    if total_rows > n_rows + tail_rows:
        rows.append(jnp.zeros((total_rows - n_rows - tail_rows, LANES), F32))
    return jnp.concatenate(rows, axis=0)


def _unpack_small(buf, shapes):
    out, off = [], 0
    flat = buf.reshape(-1)
    for s in shapes:
        n = int(np.prod(s))
        out.append(flat[off:off + n].reshape(s))
        off += n
    return out


def _block_diag(blocks, nb):
    m, r, c = blocks.shape
    n = m // nb
    eye = jnp.eye(n, dtype=blocks.dtype)
    return (blocks.reshape(nb, n, r, 1, c) * eye[None, :, None, :, None]).reshape(nb, n * r, n * c)


def _diag_blocks(dense, n):
    nb = dense.shape[0]
    r, c = dense.shape[1] // n, dense.shape[2] // n
    eye = jnp.eye(n, dtype=dense.dtype)
    return jnp.sum(dense.reshape(nb, n, r, n, c) * eye[None, :, None, :, None], axis=3).reshape(nb * n, r, c)


def kernel(x, ffn1_w_gate, ffn1_w_up, ffn1_w_down, ln1_g, ln1_b, w_in, conv_w, conv_b, rg_w_a, rg_b_a, rg_w_x, rg_b_x, rg_lambda, fox_b_f, s5_a_re, s5_a_im, s5_log_dt, s5_b_re, s5_b_im, s5_c_re, s5_c_im, s5_d, s5_w_glu, mix_norm_g, w_out, ln2_g, ln2_b, ffn2_w_gate, ffn2_w_up, ffn2_w_down, ln3_g, ln3_b, loss_target, m_ffn1_w_gate, m_ffn1_w_up, m_ffn1_w_down, m_ln1_g, m_ln1_b, m_w_in, m_conv_w, m_conv_b, m_rg_w_a, m_rg_b_a, m_rg_w_x, m_rg_b_x, m_rg_lambda, m_fox_b_f, m_s5_a_re, m_s5_a_im, m_s5_log_dt, m_s5_b_re, m_s5_b_im, m_s5_c_re, m_s5_c_im, m_s5_d, m_s5_w_glu, m_mix_norm_g, m_w_out, m_ln2_g, m_ln2_b, m_ffn2_w_gate, m_ffn2_w_up, m_ffn2_w_down, m_ln3_g, m_ln3_b, v_ffn1_w_gate, v_ffn1_w_up, v_ffn1_w_down, v_ln1_g, v_ln1_b, v_w_in, v_conv_w, v_conv_b, v_rg_w_a, v_rg_b_a, v_rg_w_x, v_rg_b_x, v_rg_lambda, v_fox_b_f, v_s5_a_re, v_s5_a_im, v_s5_log_dt, v_s5_b_re, v_s5_b_im, v_s5_c_re, v_s5_c_im, v_s5_d, v_s5_w_glu, v_mix_norm_g, v_w_out, v_ln2_g, v_ln2_b, v_ffn2_w_gate, v_ffn2_w_up, v_ffn2_w_down, v_ln3_g, v_ln3_b):
    a = dict(locals())
    w = {n: a[n] for n in WEIGHTS}
    t, d = x.shape[1], x.shape[2]
    f = ffn1_w_down.shape[1] * 8
    fs, ds = f // 8, d // 8
    mx, my, mc = lax.axis_index("x"), lax.axis_index("y"), lax.axis_index("c")
    me = 4 * mx + 2 * my + mc
    tm = _tile(t, 512)
    tf = f // 2
    win_rows = ds * Z_W // d

    FFN1, MIXW, FFN2 = ['g1', 'u1', 'd1'], ['win', 'wout', 'glu', 'conv'], ['g2', 'u2', 'd2']
    glu_rows = D_C * D_C // (8 * d)

    def shard_segs(l):
        wi = w['w_in'][l]
        win_p = jnp.concatenate([wi[:, :Z_F + N_HEADS], jnp.zeros((ds, Z_U - Z_F - N_HEADS), F32), wi[:, Z_F + N_HEADS:]], axis=1)
        conv_bits = lax.bitcast_convert_type(w['conv_w'][l], BF16).reshape(1, -1)
        segs = dict(g1=w['ffn1_w_gate'][l].T, u1=w['ffn1_w_up'][l].T, d1=w['ffn1_w_down'][l],
                    g2=w['ffn2_w_gate'][l].T, u2=w['ffn2_w_up'][l].T, d2=w['ffn2_w_down'][l],
                    win=win_p.reshape(win_rows, d), wout=w['w_out'][l], glu=_pad_rows(w['s5_w_glu'][l].reshape(-1, d), 16))
        segs = {k: v.astype(BF16) for k, v in segs.items()}
        segs['conv'] = _pad_rows(jnp.pad(conv_bits, ((0, 0), (0, d - conv_bits.shape[1]))), 16)
        return segs

    shards = [shard_segs(l) for l in range(DEPTH)]
    wts = {}

    def cat(keys):
        return jnp.concatenate([shards[l][k] for l, k in keys], axis=0)

    def split(g, keys):
        off = 0
        for l, k in keys:
            r = shards[l][k].shape[0]
            wts[(l, k)] = g[:, off:off + r]
            off += r

    grp_a = [(0, k) for k in FFN1]
    grp_b = [(0, k) for k in MIXW]
    grp_c = [(0, k) for k in FFN2]
    grp_d = [(1, k) for k in FFN1]
    grp_e = [(1, k) for k in FFN2]
    grp_f = [(1, k) for k in MIXW]
    (g_a,) = _run(_ag_chips(cat(grp_a)), "ag_chips")
    (g_a,) = _run(_ag_sibling(g_a), "ag_sibling")
    split(g_a, grp_a)

    xs = x[0]
    saved = []
    cur = xs
    for l in range(DEPTH):
        row = lambda n: w[n][l].reshape(1, -1)
        ffn = lambda keys: tuple(wts[(l, k)].reshape(f, d) for k in keys)
        wa = _block_diag(w['rg_w_a'][l], 3)
        wx = _block_diag(w['rg_w_x'][l], 3)
        bf = jnp.pad(row('fox_b_f'), ((0, 0), (0, LANES - N_HEADS)))
        s5p = (w['s5_a_re'][l], w['s5_a_im'][l], w['s5_log_dt'][l].reshape(-1, 1),
               w['s5_b_re'][l].transpose(0, 2, 1), w['s5_b_im'][l].transpose(0, 2, 1))
        ab_re, ab_im, bb_re, bb_im = _s5_prep(*s5p)
        bd_re, bd_im = _block_diag(bb_re, 2), _block_diag(bb_im, 2)
        cd_re = _block_diag(w['s5_c_re'][l].transpose(0, 2, 1), 2)
        cd_im = _block_diag(w['s5_c_im'][l].transpose(0, 2, 1), 2)
        abr, abi = ab_re.reshape(2, 1, N_STATE // 2), ab_im.reshape(2, 1, N_STATE // 2)
        gm = row('mix_norm_g')
        ga, gb, gc = gm[:, :D_A], gm[:, D_A:D_A + D_B], gm[:, D_A + D_B:]

        x0 = cur
        ffn1 = ffn(FFN1)
        (x1, pre1, gs1, us1), cres = _ffn_fwd(x0, *ffn1, row('ln1_g'), row('ln1_b'), tm, tf,
                                              carry=_ag_chips(cat(grp_b if l == 0 else grp_e)))
        if l == 0:
            (g_b,) = _run(_ag_sibling(cres[0]), "ag_sibling")
            split(g_b, grp_b)
        else:
            g_e = cres[0]
        win = wts[(l, 'win')].reshape(d, Z_W)
        wout = wts[(l, 'wout')].reshape(d, d).astype(F32)
        wglu = wts[(l, 'glu')][:, :glu_rows].reshape(D_C, D_C).astype(F32)
        conv_full = lax.bitcast_convert_type(
            wts[(l, 'conv')][:, 0, :2 * CONV_WIDTH * D_A // 8].reshape(8, CONV_WIDTH, D_A // 8, 2), F32)
        conv_full = conv_full.transpose(1, 0, 2).reshape(CONV_WIDTH, D_A)
        z = _mm(x1, win, 'nn', F32, tm, Z_W, d, "mix_in")
        out_a, h_a = _rg_fwd(z, conv_full, row('conv_b'), wa, row('rg_b_a'), wx, row('rg_b_x'), row('rg_lambda'))
        cs = _fgate_fwd(z, bf)
        crow = cs[:, :N_HEADS].T.reshape(N_HEADS, 1, t)
        (out_b, lse), cres = _attn_fwd(z, crow, carry=_ag_chips(cat(grp_c)) if l == 0 else _ag_sibling(g_e))
        if l == 0:
            (yc,), cres = _s5_fwd(z, bd_re, bd_im, abr, abi, cd_re, cd_im, row('s5_d'),
                                  carry=_join(_ag_sibling(cres[0]), _ag_chips(cat(grp_f))))
            split(cres[0], grp_c)
            g_f = cres[1]
        else:
            split(cres[0], grp_e)
            (yc,), _ = _s5_fwd(z, bd_re, bd_im, abr, abi, cd_re, cd_im, row('s5_d'))
        mix_params = [ga, gb, gc, wglu, wout, row('ln2_g'), row('ln2_b')]
        (x2,) = _rowwise(_mix_out, [out_a, out_b, yc, x1], mix_params, [(d, F32)], _tile(t, 256), "mix_out")
        ffn2 = ffn(FFN2)
        (x3, pre3, gs2, us2), cres = _ffn_fwd(x2, *ffn2, row('ln3_g'), row('ln3_b'), tm, tf,
                                              carry=_join(_ag_chips(cat(grp_d)), _ag_sibling(g_f)) if l == 0 else None)
        if l == 0:
            split(cres[1], grp_f)
            (g_d,) = _run(_ag_sibling(cres[0]), "ag_sibling")
            split(g_d, grp_d)
        saved.append(dict(x0=x0, x1=x1, pre1=pre1, gs1=gs1, us1=us1, z=z, out_a=out_a, h_a=h_a, crow=crow,
                          out_b=out_b, lse=lse, yc=yc, x2=x2, pre3=pre3, gs2=gs2, us2=us2, mix_params=mix_params,
                          ffn1=ffn1, ffn2=ffn2, win=win, conv_full=conv_full, wa=wa, wx=wx, bf=bf, s5p=s5p,
                          s5m=(bd_re, bd_im, abr, abi, cd_re, cd_im)))
        cur = x3

    dy, loss_row = _loss_head(cur, loss_target[0], tm)
    loss = lax.psum(loss_row[0, 0], ("x", "y", "c"))

    assert DEPTH == 2
    small_grads = {}
    small_names = ['conv_w'] + SMALL
    c_idx = jnp.reshape(mc, (1,)).astype(jnp.int32)
    chip_idx = jnp.reshape(2 * mx + my, (1,)).astype(jnp.int32)

    def blocks(arrs):
        return jnp.concatenate([v.astype(BF16).reshape(8, -1, d) for v in arrs], axis=1)

    def mixer_blocks(dwin, dwout, dwglu):
        glu = jnp.pad(dwglu.astype(BF16).reshape(8, glu_rows, d), ((0, 0), (0, 32 - glu_rows), (0, 0)))
        return jnp.concatenate([dwin.reshape(8, win_rows, d), dwout.astype(BF16).reshape(8, ds, d), glu], axis=1)

    def pair(full, ra):
        return _pair_add(full, ra, c_idx, "rs_add_sibling", BF16)

    for l in reversed(range(DEPTH)):
        s = saved[l]
        row = lambda n: w[n][l].reshape(1, -1)
        wg_tiles = dict(tm=tf, tn=d, tk=_tile(t, 1024))
        first = l == 0

        def ffn_back(dyv, pre, gs, us, wts3, xin, ln_g, ln_b, carry_fn=None, pipeline=False):
            (dpre,), (dlg, dlb), _ = _rowwise_vjp(_ln_only, [pre], [ln_g, ln_b], [dyv], tm, "ln_bwd")
            (dx, dg, du, hh), cres = _ffn_bwd(dpre, gs, us, *wts3, tm, tf, carry=carry_fn(dlg, dlb) if carry_fn else None)
            if not pipeline:
                dwg = _mm(dg, xin, 'tn', BF16, name="ffn_dw_gate", **wg_tiles)
                dwu = _mm(du, xin, 'tn', BF16, name="ffn_dw_up", **wg_tiles)
                dwd = _mm(hh, dpre, 'tn', BF16, name="ffn_dw_down", **wg_tiles)
                return dx, (dwg, dwu, dwd), dlg, dlb, cres

            def front(dw):
                full = dw.reshape(8, fs, d)
                (ra,) = _run(_rs_sibling(full), "rs_sibling")
                return pair(full, ra)

            p_g = front(_mm(dg, xin, 'tn', BF16, name="ffn_dw_gate", **wg_tiles))
            dwu, (rb_g,) = _mm(du, xin, 'tn', BF16, name="ffn_dw_up", carry=_rs_chips(p_g), **wg_tiles)
            p_u = front(dwu)
            dwd, (rb_u,) = _mm(hh, dpre, 'tn', BF16, name="ffn_dw_down", carry=_rs_chips(p_u), **wg_tiles)
            p_d = front(dwd)
            (rb_d,) = _run(_rs_chips(p_d), "rs_chips")
            return dx, ((p_g, rb_g), (p_u, rb_u), (p_d, rb_d)), dlg, dlb, cres

        dx2, (dwg2, dwu2, dwd2), dl3g, dl3b, cres = ffn_back(
            dy, s['pre3'], s['gs2'], s['us2'], s['ffn2'], s['x2'], row('ln3_g'), row('ln3_b'),
            (lambda *_: _rs_sibling(full_l1)) if first else None)
        if first:
            part_l1 = pair(full_l1, cres[0])
            full_c2 = blocks([dwg2, dwu2, dwd2])
        (d_oa, d_ob, d_yc, d_x1), (dga, dgb, dgc, dwglu, dwout, dl2g, dl2b), cres = _rowwise_vjp(
            _mix_out, [s['out_a'], s['out_b'], s['yc'], s['x1']], s['mix_params'], [dx2], _tile(t, 256), "mix_out_bwd",
            carry=_rs_sibling(full_c2) if first else None)
        if first:
            part_c2 = pair(full_c2, cres[0])
        (d_ax, d_ag, dcw, dcb, dwa, dba, dwx, dbx, dlam) = _rg_bwd(
            s['z'], s['h_a'], d_oa, s['conv_full'], row('conv_b'), s['wa'], row('rg_b_a'), s['wx'], row('rg_b_x'), row('rg_lambda'))
        (dq, dk, dv, dcrow), cres = _attn_bwd(s['z'], s['crow'], s['lse'], d_ob,
                                              carry=_join(_rs_chips(part_l1), _rs_chips(part_c2)) if first else None)
        if first:
            rb_l1, rb_c2 = cres
        dc_pad = jnp.pad(dcrow.reshape(N_HEADS, t).T, ((0, 0), (0, LANES - N_HEADS)))
        dzf, dbf = _fgate_bwd(s['z'], s['bf'], dc_pad)
        du_c, dbd_re, dbd_im, dabr, dabi, dcd_re, dcd_im, dd = _s5_bwd(s['z'], d_yc, *s['s5m'], row('s5_d'))
        dz = jnp.concatenate([d_ax, d_ag, dq, dk, dv, dzf, du_c], axis=1)
        dx1 = _mm(dz, s['win'], 'nt', F32, tm, d, Z_W, "mix_in_dx", add=d_x1)
        dwin = _mm(s['x1'], dz, 'tn', BF16, d, Z_W, tm, "mix_in_dw")
        dbb_re, dbb_im = _diag_blocks(dbd_re, N_GROUPS // 2), _diag_blocks(dbd_im, N_GROUPS // 2)
        dcm_re, dcm_im = _diag_blocks(dcd_re, N_GROUPS // 2), _diag_blocks(dcd_im, N_GROUPS // 2)
        da_re, da_im, dlog_dt, db_re, db_im = _s5_prep_bwd(*s['s5p'], dabr.reshape(N_GROUPS, C_STATE), dabi.reshape(N_GROUPS, C_STATE), dbb_re, dbb_im)
        sg = dict(conv_w=dcw, conv_b=dcb, rg_w_a=_diag_blocks(dwa, 2), rg_b_a=dba,
                  rg_w_x=_diag_blocks(dwx, 2), rg_b_x=dbx, rg_lambda=dlam, fox_b_f=dbf[:, :N_HEADS],
                  s5_a_re=da_re, s5_a_im=da_im, s5_log_dt=dlog_dt, s5_b_re=db_re.transpose(0, 2, 1), s5_b_im=db_im.transpose(0, 2, 1),
                  s5_c_re=dcm_re.transpose(0, 2, 1), s5_c_im=dcm_im.transpose(0, 2, 1), s5_d=dd,
                  mix_norm_g=jnp.concatenate([dga, dgb, dgc], axis=1), ln2_g=dl2g, ln2_b=dl2b, ln3_g=dl3g, ln3_b=dl3b)
        small_grads[l] = sg

        if first:
            full_m = mixer_blocks(dwin, dwout, dwglu)
            (ra_m,) = _run(_rs_sibling(full_m), "rs_sibling")
            part_m = pair(full_m, ra_m)

            def last_carry(dlg, dlb):
                sg.update(ln1_g=dlg, ln1_b=dlb)
                packed = _pack_small([small_grads[ll][n] for n in small_names for ll in range(DEPTH)])
                return _join(_rs_chips(part_m), _gather_small(packed))

            dx0, tail, _, _, (rb_m, all_small) = ffn_back(dx1, s['pre1'], s['gs1'], s['us1'], s['ffn1'], s['x0'], row('ln1_g'),
                                                          row('ln1_b'), last_carry, pipeline=True)
        else:
            dx0, (dwg1, dwu1, dwd1), dl1g, dl1b, _ = ffn_back(dx1, s['pre1'], s['gs1'], s['us1'], s['ffn1'], s['x0'],
                                                              row('ln1_g'), row('ln1_b'))
            sg.update(ln1_g=dl1g, ln1_b=dl1b)
            full_l1 = jnp.concatenate([blocks([dwg1, dwu1, dwd1, dwg2, dwu2, dwd2]), mixer_blocks(dwin, dwout, dwglu)], axis=1)
        dy = dx0

    grad_x = dy.reshape(x.shape)
    quad = lambda part, rb: _quad_add(part, rb, chip_idx, "rs_add_chips")
    own = {}

    def take(rows_f32, keys, l):
        off = 0
        for k, r in keys:
            own[(l, k)] = rows_f32[off:off + r]
            off += r

    ffn_keys = lambda names: [(k, fs) for k in names]
    mix_keys = [('win', win_rows), ('wout', ds), ('glu', glu_rows)]
    take(quad(part_l1, rb_l1), ffn_keys(FFN1 + FFN2) + mix_keys, 1)
    take(quad(part_c2, rb_c2), ffn_keys(FFN2), 0)
    take(quad(part_m, rb_m), mix_keys, 0)
    for k, (part, rb) in zip(FFN1, tail):
        own[(0, k)] = quad(part, rb)

    grads = {}
    for k, n in zip(FFN1 + FFN2, ['ffn1_w_gate', 'ffn1_w_up', 'ffn1_w_down', 'ffn2_w_gate', 'ffn2_w_up', 'ffn2_w_down']):
        grads[n] = jnp.stack([own[(l, k)].T if 'down' not in n else own[(l, k)] for l in range(DEPTH)])
    gwin = jnp.stack([own[(l, 'win')].reshape(ds, Z_W) for l in range(DEPTH)])
    grads['w_in'] = jnp.concatenate([gwin[:, :, :Z_F + N_HEADS], gwin[:, :, Z_U:]], axis=2)
    grads['w_out'] = jnp.stack([own[(l, 'wout')] for l in range(DEPTH)])
    grads['s5_w_glu'] = jnp.stack([own[(l, 'glu')].reshape(D_C // 8, D_C) for l in range(DEPTH)])

    small_shapes = [((DEPTH, CONV_WIDTH, D_A) if n == 'conv_w' else w[n].shape) for n in small_names]
    conv_zero = jnp.zeros((DEPTH, CONV_WIDTH, D_A), F32)
    summed = _sum_small(all_small)
    for n, g in zip(small_names, _unpack_small(summed, small_shapes)):
        grads[n] = g
    grads['conv_w'] = lax.dynamic_slice_in_dim(grads['conv_w'], me * (D_A // 8), D_A // 8, axis=2)

    delta, new_m, new_v = {}, {}, {}
    for n in BIG + ['conv_w']:
        sh = w[n].shape
        two = lambda v: v.reshape(-1, sh[-1])
        dl, nm, nv = _adamw(two(w[n]), two(grads[n]), two(a['m_' + n]), two(a['v_' + n]), "adamw_" + n)
        delta[n], new_m[n], new_v[n] = dl.reshape(sh), nm.reshape(sh), nv.reshape(sh)
    dl, nm, nv = _adamw(_pack_small([conv_zero] + [w[n] for n in SMALL]), summed,
                        _pack_small([conv_zero] + [a['m_' + n] for n in SMALL]),
                        _pack_small([conv_zero] + [a['v_' + n] for n in SMALL]), "adamw_small")
    for n, v1, v2, v3 in zip(small_names[1:], _unpack_small(dl, small_shapes)[1:], _unpack_small(nm, small_shapes)[1:],
                             _unpack_small(nv, small_shapes)[1:]):
        delta[n], new_m[n], new_v[n] = v1, v2, v3

    return (loss, grad_x, *[grads[n] for n in WEIGHTS], *[delta[n] for n in WEIGHTS], *[new_m[n] for n in WEIGHTS],
            *[new_v[n] for n in WEIGHTS])
```

```python
import functools
import math

import jax
import jax.numpy as jnp
import numpy as np
from jax import lax
from jax.experimental import pallas as pl
from jax.experimental.pallas import tpu as pltpu

F32 = jnp.float32
BF16 = jnp.bfloat16
MESH = pl.DeviceIdType.MESH

DEPTH = 2
ALPHA = (2 * DEPTH) ** 0.25
LN_EPS = 1e-5
RMS_EPS = 1e-6
RG_C = 8.0
CONV_WIDTH = 4
HEAD_DIM = 64
C_GROUP = 16
C_STATE = 64
D_A = 384
D_B = 384
D_C = 256
N_HEADS = D_B // HEAD_DIM
N_GROUPS = D_C // C_GROUP
N_STATE = N_GROUPS * C_STATE
Z_F = 2 * D_A + 3 * D_B
Z_U = Z_F + 128
Z_W = Z_U + D_C
N_IN = Z_F + N_HEADS + D_C
ADAM_LR, ADAM_B1, ADAM_B2, ADAM_EPS, ADAM_WD, ADAM_STEP = 0.001, 0.9, 0.999, 1e-08, 0.01, 10
LANES = 128
NEG = -1e30

WEIGHTS = ['ffn1_w_gate', 'ffn1_w_up', 'ffn1_w_down', 'ln1_g', 'ln1_b', 'w_in', 'conv_w', 'conv_b', 'rg_w_a', 'rg_b_a',
           'rg_w_x', 'rg_b_x', 'rg_lambda', 'fox_b_f', 's5_a_re', 's5_a_im', 's5_log_dt', 's5_b_re', 's5_b_im', 's5_c_re',
           's5_c_im', 's5_d', 's5_w_glu', 'mix_norm_g', 'w_out', 'ln2_g', 'ln2_b', 'ffn2_w_gate', 'ffn2_w_up', 'ffn2_w_down',
           'ln3_g', 'ln3_b']
BIG = ['ffn1_w_gate', 'ffn1_w_up', 'ffn1_w_down', 'w_in', 's5_w_glu', 'w_out', 'ffn2_w_gate', 'ffn2_w_up', 'ffn2_w_down']
SMALL_TAIL = ['fox_b_f', 's5_log_dt']
SMALL = [n for n in WEIGHTS if n not in BIG and n != 'conv_w' and n not in SMALL_TAIL] + SMALL_TAIL


def _sig(x):
    return 1.0 / (1.0 + jnp.exp(-x))


def _gelu(x):
    return 0.5 * x * (1.0 + jnp.tanh(math.sqrt(2.0 / math.pi) * (x + 0.044715 * (x * x * x))))


def _softplus(x):
    return jnp.maximum(x, 0.0) + jnp.log(1.0 + jnp.exp(jnp.minimum(x, -x)))


def _dot(a, b, dims):
    return lax.dot_general(a.astype(BF16), b.astype(BF16), (dims, ((), ())), preferred_element_type=F32)


NN = ((1,), (0,))
NT = ((1,), (1,))
TN = ((0,), (0,))


@jax.custom_vjp
def _bdot(a, w):
    return _dot(a, w, NN)


def _bdot_fwd(a, w):
    return _dot(a, w, NN), (a, w)


def _bdot_bwd(res, ct):
    a, w = res
    return _dot(ct, w, NT), _dot(a, ct, TN)


_bdot.defvjp(_bdot_fwd, _bdot_bwd)


def _ln(pre, g, b):
    mu = jnp.mean(pre, axis=-1, keepdims=True)
    xc = pre - mu
    var = jnp.mean(xc * xc, axis=-1, keepdims=True)
    return xc * lax.rsqrt(var + LN_EPS) * g + b


def _rms(x, g):
    return x * lax.rsqrt(jnp.mean(x * x, axis=-1, keepdims=True) + RMS_EPS) * g


def _tile(n, want):
    return want if n % want == 0 else n


class _Carry:
    def __init__(self, ins, outs, aliases, plan, n):
        self.ins, self.outs, self.aliases, self.plan, self.n = list(ins), list(outs), dict(aliases), plan, n


def _join(a, b):
    na, ma = len(a.ins), len(a.outs)

    def plan(x, y, c, ins, outs):
        return a.plan(x, y, c, ins[:na], outs[:ma]) + b.plan(x, y, c, ins[na:], outs[ma:])

    aliases = dict(a.aliases)
    aliases.update({na + i: ma + j for i, j in b.aliases.items()})
    return _Carry(a.ins + b.ins, a.outs + b.outs, aliases, plan, a.n + b.n)


def _copies(carry, cins, couts, send, recv):
    x, y, c = lax.axis_index("x"), lax.axis_index("y"), lax.axis_index("c")
    res = []
    for k, (s, d, peer) in enumerate(carry.plan(x, y, c, cins, couts)):
        if peer is None:
            res.append(pltpu.make_async_copy(s, d, send.at[k]))
        else:
            res.append(pltpu.make_async_remote_copy(src_ref=s, dst_ref=d, send_sem=send.at[k], recv_sem=recv.at[k],
                                                    device_id=peer, device_id_type=MESH))
    return res


def _call(body, grid, in_specs, out_specs, out_shape, scratch, semantics, name, args, carry=None):
    n_in, n_out, n_scr = len(in_specs), len(out_specs), len(scratch)
    if carry is None:
        res = pl.pallas_call(body, grid=grid, in_specs=in_specs, out_specs=out_specs, out_shape=out_shape,
                             scratch_shapes=scratch, compiler_params=pltpu.CompilerParams(dimension_semantics=semantics),
                             name=name)(*args)
        return list(res), []
    nci, nco = len(carry.ins), len(carry.outs)

    def wrapped(*refs):
        o0 = n_in + nci
        s0 = o0 + n_out + nco
        cins, couts = refs[n_in:o0], refs[o0 + n_out:s0]
        send, recv = refs[s0 + n_scr:]
        first = functools.reduce(jnp.logical_and, [pl.program_id(k) == 0 for k in range(len(grid))])
        last = functools.reduce(jnp.logical_and, [pl.program_id(k) == grid[k] - 1 for k in range(len(grid))])

        @pl.when(first)
        def _():
            for cp in _copies(carry, cins, couts, send, recv):
                cp.start()

        body(*refs[:n_in], *refs[o0:o0 + n_out], *refs[s0:s0 + n_scr])

        @pl.when(last)
        def _():
            for cp in _copies(carry, cins, couts, send, recv):
                cp.wait()

    hbm = pl.BlockSpec(memory_space=pl.ANY)
    res = pl.pallas_call(
        wrapped, grid=grid, in_specs=list(in_specs) + [hbm] * nci, out_specs=list(out_specs) + [hbm] * nco,
        out_shape=list(out_shape) + carry.outs, scratch_shapes=list(scratch) + [pltpu.SemaphoreType.DMA((carry.n,))] * 2,
        input_output_aliases={n_in + i: n_out + j for i, j in carry.aliases.items()},
        compiler_params=pltpu.CompilerParams(dimension_semantics=("arbitrary",) * len(grid), has_side_effects=True),
        name=name)(*args, *carry.ins)
    return list(res[:n_out]), list(res[n_out:])


def _run(carry, name):
    nci, nco = len(carry.ins), len(carry.outs)

    def body(*refs):
        cps = _copies(carry, refs[:nci], refs[nci:nci + nco], refs[-2], refs[-1])
        for cp in cps:
            cp.start()
        for cp in cps:
            cp.wait()

    hbm = pl.BlockSpec(memory_space=pl.ANY)
    return pl.pallas_call(
        body, in_specs=[hbm] * nci, out_specs=[hbm] * nco, out_shape=carry.outs, input_output_aliases=carry.aliases,
        scratch_shapes=[pltpu.SemaphoreType.DMA((carry.n,))] * 2, compiler_params=pltpu.CompilerParams(has_side_effects=True),
        name=name)(*carry.ins)


def _ag_chips(shard):
    def plan(x, y, c, ins, outs):
        dst = outs[0].at[4 * x + 2 * y + c]
        return [(ins[0], dst, None)] + [(ins[0], dst, (px, py, c)) for px, py in ((1 - x, y), (x, 1 - y), (1 - x, 1 - y))]

    return _Carry([shard], [jax.ShapeDtypeStruct((8,) + shard.shape, shard.dtype)], {}, plan, 4)


def _ag_sibling(g):
    def plan(x, y, c, ins, outs):
        return [(outs[0].at[2 * q + c], outs[0].at[2 * q + c], (x, y, 1 - c)) for q in range(4)]

    return _Carry([g], [jax.ShapeDtypeStruct(g.shape, g.dtype)], {0: 0}, plan, 4)


def _rs_sibling(full):
    def plan(x, y, c, ins, outs):
        return [(ins[0].at[2 * q + (1 - c)], outs[0].at[q], (x, y, 1 - c)) for q in range(4)]

    return _Carry([full], [jax.ShapeDtypeStruct((4,) + full.shape[1:], full.dtype)], {}, plan, 4)


def _rs_chips(part):
    def plan(x, y, c, ins, outs):
        res = []
        for k, (dx, dy) in enumerate(((1, 0), (0, 1), (1, 1))):
            tx, ty = x ^ dx, y ^ dy
            res.append((ins[0].at[2 * tx + ty], outs[0].at[k], (tx, ty, c)))
        return res

    return _Carry([part], [jax.ShapeDtypeStruct((3,) + part.shape[1:], part.dtype)], {}, plan, 3)


def _mm(a, b, dims, out_dtype, tm, tn, tk, name, add=None, carry=None):
    if dims == 'nn':
        (m, k), n = a.shape, b.shape[1]
        a_spec = pl.BlockSpec((tm, tk), lambda i, j, q: (i, q))
        b_spec = pl.BlockSpec((tk, tn), lambda i, j, q: (q, j))
        dn = NN
    elif dims == 'nt':
        (m, k), n = a.shape, b.shape[0]
        a_spec = pl.BlockSpec((tm, tk), lambda i, j, q: (i, q))
        b_spec = pl.BlockSpec((tn, tk), lambda i, j, q: (j, q))
        dn = NT
    else:
        (k, m), n = a.shape, b.shape[1]
        a_spec = pl.BlockSpec((tk, tm), lambda i, j, q: (q, i))
        b_spec = pl.BlockSpec((tk, tn), lambda i, j, q: (q, j))
        dn = TN
    nk = k // tk
    o_spec = pl.BlockSpec((tm, tn), lambda i, j, q: (i, j))

    def body(*refs):
        if add is None:
            a_ref, b_ref, o_ref, acc_ref = refs
        else:
            a_ref, b_ref, add_ref, o_ref, acc_ref = refs
        q = pl.program_id(2)
        part = _dot(a_ref[...], b_ref[...], dn)

        @pl.when(q == 0)
        def _():
            acc_ref[...] = part

        @pl.when(q > 0)
        def _():
            acc_ref[...] += part

        @pl.when(q == nk - 1)
        def _():
            r = acc_ref[...]
            if add is not None:
                r = r + add_ref[...]
            o_ref[...] = r.astype(o_ref.dtype)

    ins = [a, b] + ([] if add is None else [add])
    specs = [a_spec, b_spec] + ([] if add is None else [o_spec])
    (res,), cres = _call(body, (m // tm, n // tn, nk), specs, [o_spec], [jax.ShapeDtypeStruct((m, n), out_dtype)],
                         [pltpu.VMEM((tm, tn), F32)], ("parallel", "parallel", "arbitrary"), name, ins, carry)
    return res if carry is None else (res, cres)


def _rowwise(fn, rows, params, outs, tm, name):
    t = rows[0].shape[0]
    nr, npar = len(rows), len(params)

    def body(*refs):
        r = [x[...] for x in refs[:nr]]
        p = [x[...] for x in refs[nr:nr + npar]]
        res = fn(*r, *p)
        for o_ref, o in zip(refs[nr + npar:], res):
            o_ref[...] = o.astype(o_ref.dtype)

    in_specs = ([pl.BlockSpec((tm, a.shape[1]), lambda i: (i, 0)) for a in rows]
                + [pl.BlockSpec(p.shape, lambda i: (0, 0)) for p in params])
    return pl.pallas_call(
        body, grid=(t // tm,), in_specs=in_specs,
        out_specs=[pl.BlockSpec((tm, c), lambda i: (i, 0)) for c, _ in outs],
        out_shape=[jax.ShapeDtypeStruct((t, c), d) for c, d in outs],
        compiler_params=pltpu.CompilerParams(dimension_semantics=("parallel",)), name=name)(*rows, *params)


def _rowwise_vjp(fn, rows, params, cots, tm, name, carry=None):
    t = rows[0].shape[0]
    nr, npar, nc = len(rows), len(params), len(cots)

    def body(*refs):
        r = [x[...] for x in refs[:nr]]
        p = [x[...] for x in refs[nr:nr + npar]]
        c = [x[...] for x in refs[nr + npar:nr + npar + nc]]
        o_refs = refs[nr + npar + nc:]
        _, pull = jax.vjp(fn, *r, *p)
        grads = pull(tuple(c))
        for o_ref, g in zip(o_refs[:nr], grads[:nr]):
            o_ref[...] = g
        i = pl.program_id(0)

        @pl.when(i == 0)
        def _():
            for o_ref, g in zip(o_refs[nr:], grads[nr:]):
                o_ref[...] = g

        @pl.when(i > 0)
        def _():
            for o_ref, g in zip(o_refs[nr:], grads[nr:]):
                o_ref[...] += g

    row_spec = lambda a: pl.BlockSpec((tm, a.shape[1]), lambda i: (i, 0))
    par_spec = lambda p: pl.BlockSpec(p.shape, lambda i: (0, 0))
    res, cres = _call(
        body, (t // tm,),
        [row_spec(a) for a in rows] + [par_spec(p) for p in params] + [row_spec(a) for a in cots],
        [row_spec(a) for a in rows] + [par_spec(p) for p in params],
        [jax.ShapeDtypeStruct(a.shape, F32) for a in rows] + [jax.ShapeDtypeStruct(p.shape, F32) for p in params],
        [], ("arbitrary",), name, [*rows, *params, *cots], carry)
    return res[:nr], res[nr:], cres


def _ffn_fwd(x, wgt, wut, wd, ln_g, ln_b, tm, tf, carry=None):
    t, d = x.shape
    f = wgt.shape[0]
    nj = f // tf

    def body(x_ref, wg_ref, wu_ref, wd_ref, g_ref, b_ref, y_ref, pre_ref, gs_ref, us_ref, acc_ref):
        j = pl.program_id(1)
        xv = x_ref[...]
        xb = xv.astype(BF16)
        g = _dot(xb, wg_ref[...], NT)
        u = _dot(xb, wu_ref[...], NT)
        gs_ref[...] = g.astype(BF16)
        us_ref[...] = u.astype(BF16)
        part = _dot(g * _sig(g) * u, wd_ref[...], NN)

        @pl.when(j == 0)
        def _():
            acc_ref[...] = part

        @pl.when(j > 0)
        def _():
            acc_ref[...] += part

        @pl.when(j == nj - 1)
        def _():
            pre = ALPHA * xv + 0.5 * acc_ref[...]
            pre_ref[...] = pre
            y_ref[...] = _ln(pre, g_ref[...], b_ref[...])

    w_spec = pl.BlockSpec((tf, d), lambda i, j: (j, 0))
    x_spec = pl.BlockSpec((tm, d), lambda i, j: (i, 0))
    v_spec = pl.BlockSpec((1, d), lambda i, j: (0, 0))
    h_spec = pl.BlockSpec((tm, tf), lambda i, j: (i, j))
    return _call(
        body, (t // tm, nj), [x_spec, w_spec, w_spec, w_spec, v_spec, v_spec], [x_spec, x_spec, h_spec, h_spec],
        [jax.ShapeDtypeStruct((t, d), F32), jax.ShapeDtypeStruct((t, d), F32),
         jax.ShapeDtypeStruct((t, f), BF16), jax.ShapeDtypeStruct((t, f), BF16)],
        [pltpu.VMEM((tm, d), F32)], ("parallel", "arbitrary"), "ffn_fwd", [x, wgt, wut, wd, ln_g, ln_b], carry)


def _ffn_bwd(dpre, gs, us, wgt, wut, wd, tm, tf, carry=None):
    t, d = dpre.shape
    f = wgt.shape[0]
    nj = f // tf

    def body(dp_ref, gs_ref, us_ref, wg_ref, wu_ref, wd_ref, dx_ref, dg_ref, du_ref, hh_ref, acc_ref):
        j = pl.program_id(1)
        dp = dp_ref[...]
        dh = _dot(0.5 * dp, wd_ref[...], NT)
        g = gs_ref[...].astype(F32)
        u = us_ref[...].astype(F32)
        s = _sig(g)
        sl = g * s
        dg = (dh * u * (s * (1.0 + g * (1.0 - s)))).astype(BF16)
        du = (dh * sl).astype(BF16)
        dg_ref[...] = dg
        du_ref[...] = du
        hh_ref[...] = (0.5 * sl * u).astype(BF16)
        part = _dot(dg, wg_ref[...], NN) + _dot(du, wu_ref[...], NN)

        @pl.when(j == 0)
        def _():
            acc_ref[...] = part

        @pl.when(j > 0)
        def _():
            acc_ref[...] += part

        @pl.when(j == nj - 1)
        def _():
            dx_ref[...] = ALPHA * dp + acc_ref[...]

    w_spec = pl.BlockSpec((tf, d), lambda i, j: (j, 0))
    x_spec = pl.BlockSpec((tm, d), lambda i, j: (i, 0))
    h_spec = pl.BlockSpec((tm, tf), lambda i, j: (i, j))
    return _call(
        body, (t // tm, nj), [x_spec, h_spec, h_spec, w_spec, w_spec, w_spec], [x_spec, h_spec, h_spec, h_spec],
        [jax.ShapeDtypeStruct((t, d), F32)] + [jax.ShapeDtypeStruct((t, f), BF16)] * 3,
        [pltpu.VMEM((tm, d), F32)], ("parallel", "arbitrary"), "ffn_bwd", [dpre, gs, us, wgt, wut, wd], carry)


def _scan8(a_ref, b_ref, out_ref, t, reverse=False):
    w = out_ref.shape[-1]
    sub = lax.broadcasted_iota(jnp.int32, (8, w), 0)

    def step(g, carry):
        r0 = pl.multiple_of((t // 8 - 1 - g if reverse else g) * 8, 8)
        bv = b_ref[pl.ds(r0, 8), :]
        av = None if a_ref is None else a_ref[pl.ds(r0, 8), :]
        for s in (1, 2, 4):
            ok = (sub < 8 - s) if reverse else (sub >= s)
            shift = 8 - s if reverse else s
            b_sh = jnp.where(ok, pltpu.roll(bv, shift, 0), 0.0)
            if av is None:
                bv = bv + b_sh
            else:
                bv = av * b_sh + bv
                av = av * jnp.where(ok, pltpu.roll(av, shift, 0), 1.0)
        h = bv + carry if av is None else bv + av * carry
        out_ref[pl.ds(r0, 8), :] = h
        return jnp.sum(jnp.where(sub == (0 if reverse else 7), h, 0.0), axis=0, keepdims=True)

    lax.fori_loop(0, t // 8, step, jnp.zeros((1, w), F32))


def _rg_local(xa, wa, ba, wx, bx, lam):
    r = _sig(_bdot(xa, wa) + ba)
    i = _sig(_bdot(xa, wx) + bx)
    log_a = -RG_C * r * _softplus(-lam)
    a = jnp.exp(log_a)
    mult = jnp.sqrt(-jnp.tanh(log_a) * (a * a + 1.0))
    return a, mult * (i * xa)


def _conv_taps(ext, n):
    return [ext[8:, :]] + [pltpu.roll(ext, s, 0)[8:, :] for s in (1, 2, 3)]


def _rg_fwd(z, cw, cb, wa, ba, wx, bx, lam):
    t = z.shape[0]
    cr = _tile(t, 256)
    nb = D_A // LANES

    def body(ax_ref, ag_ref, cw_ref, cb_ref, wa_ref, ba_ref, wx_ref, bx_ref, lam_ref, out_ref, h_ref, axp, a_s, b_s):
        axp[pl.ds(0, 8), :] = jnp.zeros((8, LANES), F32)
        pltpu.sync_copy(ax_ref, axp.at[pl.ds(8, t)])
        w = [cw_ref[pl.ds(k, 1), :] for k in range(CONV_WIDTH)]

        def chunk(c, carry):
            t0 = pl.multiple_of(c * cr, cr)
            taps = _conv_taps(axp[pl.ds(t0, cr + 8), :], cr)
            xa = cb_ref[...] + w[3] * taps[0] + w[2] * taps[1] + w[1] * taps[2] + w[0] * taps[3]
            a, gated = _rg_local(xa, wa_ref[...], ba_ref[...], wx_ref[...], bx_ref[...], lam_ref[...])
            a_s[pl.ds(t0, cr), :] = a
            b_s[pl.ds(t0, cr), :] = gated
            return carry

        lax.fori_loop(0, t // cr, chunk, 0)

        _scan8(a_s, b_s, h_ref, t)

        def fin(c, carry):
            t0 = pl.multiple_of(c * cr, cr)
            out_ref[pl.ds(t0, cr), :] = _gelu(ag_ref[pl.ds(t0, cr), :]) * h_ref[pl.ds(t0, cr), :]
            return carry

        lax.fori_loop(0, t // cr, fin, 0)

    col = lambda off: pl.BlockSpec((t, LANES), lambda b: (0, off + b))
    vec = pl.BlockSpec((1, LANES), lambda b: (0, b))
    mat = pl.BlockSpec((None, LANES, LANES), lambda b: (b, 0, 0))
    return pl.pallas_call(
        body, grid=(nb,),
        in_specs=[col(0), col(nb), pl.BlockSpec((CONV_WIDTH, LANES), lambda b: (0, b)), vec, mat, vec, mat, vec, vec],
        out_specs=[col(0), col(0)],
        out_shape=[jax.ShapeDtypeStruct((t, D_A), F32), jax.ShapeDtypeStruct((t, D_A), F32)],
        scratch_shapes=[pltpu.VMEM((t + 8, LANES), F32), pltpu.VMEM((t, LANES), F32), pltpu.VMEM((t, LANES), F32)],
        compiler_params=pltpu.CompilerParams(dimension_semantics=("arbitrary",)), name="rglru_fwd")(
            z, z, cw, cb, wa, ba, wx, bx, lam)


def _rg_bwd(z, h, dout, cw, cb, wa, ba, wx, bx, lam):
    t = z.shape[0]
    cr = _tile(t, 256)
    nb = D_A // LANES

    def body(ax_ref, ag_ref, h_ref, do_ref, cw_ref, cb_ref, wa_ref, ba_ref, wx_ref, bx_ref, lam_ref,
             dax_ref, dag_ref, dcw_ref, dcb_ref, dwa_ref, dba_ref, dwx_ref, dbx_ref, dlam_ref,
             axp, hp, xa_s, a_s, g_s, dxa_s, u_s):
        zero8 = jnp.zeros((8, LANES), F32)
        axp[pl.ds(0, 8), :] = zero8
        hp[pl.ds(0, 8), :] = zero8
        dxa_s[pl.ds(t, 8), :] = zero8
        u_s[pl.ds(t, 8), :] = zero8
        pltpu.sync_copy(ax_ref, axp.at[pl.ds(8, t)])
        pltpu.sync_copy(h_ref, hp.at[pl.ds(8, t)])
        w = [cw_ref[pl.ds(k, 1), :] for k in range(CONV_WIDTH)]
        for ref in (dcw_ref, dcb_ref, dwa_ref, dba_ref, dwx_ref, dbx_ref, dlam_ref):
            ref[...] = jnp.zeros(ref.shape, F32)

        def p1(c, carry):
            t0 = pl.multiple_of(c * cr, cr)
            taps = _conv_taps(axp[pl.ds(t0, cr + 8), :], cr)
            xa = cb_ref[...] + w[3] * taps[0] + w[2] * taps[1] + w[1] * taps[2] + w[0] * taps[3]
            a, _ = _rg_local(xa, wa_ref[...], ba_ref[...], wx_ref[...], bx_ref[...], lam_ref[...])
            xa_s[pl.ds(t0, cr), :] = xa
            a_s[pl.ds(t0, cr), :] = a
            ag = ag_ref[pl.ds(t0, cr), :]
            dov = do_ref[pl.ds(t0, cr), :]
            gel, pull = jax.vjp(_gelu, ag)
            g_s[pl.ds(t0, cr), :] = dov * gel
            u_s[pl.ds(t0, cr), :] = a * (dov * gel)
            dag_ref[pl.ds(t0, cr), :] = pull(dov * h_ref[pl.ds(t0, cr), :])[0]
            return carry

        lax.fori_loop(0, t // cr, p1, 0)
        _scan8(a_s, u_s, u_s, t, reverse=True)

        def p3(c, carry):
            t0 = pl.multiple_of(c * cr, cr)
            g = g_s[pl.ds(t0, cr), :] + pltpu.roll(u_s[pl.ds(t0, cr + 8), :], cr + 7, 0)[:cr, :]
            h_prev = pltpu.roll(hp[pl.ds(t0, cr + 8), :], 1, 0)[8:, :]
            _, pull = jax.vjp(_rg_local, xa_s[pl.ds(t0, cr), :], wa_ref[...], ba_ref[...], wx_ref[...], bx_ref[...],
                              lam_ref[...])
            dxa, dwa, dba, dwx, dbx, dlam = pull((g * h_prev, g))
            dxa_s[pl.ds(t0, cr), :] = dxa
            dwa_ref[...] += dwa
            dba_ref[...] += dba
            dwx_ref[...] += dwx
            dbx_ref[...] += dbx
            dlam_ref[...] += dlam
            return carry

        lax.fori_loop(0, t // cr, p3, 0)

        def p4(c, carry):
            t0 = pl.multiple_of(c * cr, cr)
            ext = dxa_s[pl.ds(t0, cr + 8), :]
            n = cr + 8
            ahead = [ext[:cr, :]] + [pltpu.roll(ext, n - s, 0)[:cr, :] for s in (1, 2, 3)]
            dax_ref[pl.ds(t0, cr), :] = w[3] * ahead[0] + w[2] * ahead[1] + w[1] * ahead[2] + w[0] * ahead[3]
            taps = _conv_taps(axp[pl.ds(t0, cr + 8), :], cr)
            dxa = ahead[0]
            for k in range(CONV_WIDTH):
                dcw_ref[pl.ds(k, 1), :] += jnp.sum(dxa * taps[CONV_WIDTH - 1 - k], axis=0, keepdims=True)
            dcb_ref[...] += jnp.sum(dxa, axis=0, keepdims=True)
            return carry

        lax.fori_loop(0, t // cr, p4, 0)

    col = lambda off: pl.BlockSpec((t, LANES), lambda b: (0, off + b))
    vec = pl.BlockSpec((1, LANES), lambda b: (0, b))
    mat = pl.BlockSpec((None, LANES, LANES), lambda b: (b, 0, 0))
    cws = pl.BlockSpec((CONV_WIDTH, LANES), lambda b: (0, b))
    sds = jax.ShapeDtypeStruct
    return pl.pallas_call(
        body, grid=(nb,),
        in_specs=[col(0), col(nb), col(0), col(0), cws, vec, mat, vec, mat, vec, vec],
        out_specs=[col(0), col(0), cws, vec, mat, vec, mat, vec, vec],
        out_shape=[sds((t, D_A), F32), sds((t, D_A), F32), sds((CONV_WIDTH, D_A), F32), sds((1, D_A), F32),
                   sds((nb, LANES, LANES), F32), sds((1, D_A), F32), sds((nb, LANES, LANES), F32), sds((1, D_A), F32),
                   sds((1, D_A), F32)],
        scratch_shapes=[pltpu.VMEM((t + 8, LANES), F32), pltpu.VMEM((t + 8, LANES), F32), pltpu.VMEM((t, LANES), F32),
                        pltpu.VMEM((t, LANES), F32), pltpu.VMEM((t, LANES), F32), pltpu.VMEM((t + 8, LANES), F32),
                        pltpu.VMEM((t + 8, LANES), F32)],
        compiler_params=pltpu.CompilerParams(dimension_semantics=("arbitrary",)), name="rglru_bwd")(
            z, z, h, dout, cw, cb, wa, ba, wx, bx, lam)


def _fgate_fwd(z, bf):
    t = z.shape[0]

    def body(zf_ref, bf_ref, c_ref):
        c_ref[...] = -_softplus(-(zf_ref[...] + bf_ref[...]))
        _scan8(None, c_ref, c_ref, t)

    return pl.pallas_call(
        body, grid=(1,), in_specs=[pl.BlockSpec((t, LANES), lambda i: (0, Z_F // LANES)), pl.BlockSpec((1, LANES), lambda i: (0, 0))],
        out_specs=pl.BlockSpec((t, LANES), lambda i: (0, 0)), out_shape=jax.ShapeDtypeStruct((t, LANES), F32),
        compiler_params=pltpu.CompilerParams(dimension_semantics=("arbitrary",)), name="fgate_fwd")(z, bf)


def _fgate_bwd(z, bf, dc):
    t = z.shape[0]

    def body(zf_ref, bf_ref, dc_ref, dz_ref, db_ref):
        _scan8(None, dc_ref, dz_ref, t, reverse=True)
        dz = dz_ref[...] * _sig(-(zf_ref[...] + bf_ref[...]))
        dz_ref[...] = dz
        db_ref[...] = jnp.sum(dz, axis=0, keepdims=True)

    return pl.pallas_call(
        body, grid=(1,),
        in_specs=[pl.BlockSpec((t, LANES), lambda i: (0, Z_F // LANES)), pl.BlockSpec((1, LANES), lambda i: (0, 0)),
                  pl.BlockSpec((t, LANES), lambda i: (0, 0))],
        out_specs=[pl.BlockSpec((t, LANES), lambda i: (0, 0)), pl.BlockSpec((1, LANES), lambda i: (0, 0))],
        out_shape=[jax.ShapeDtypeStruct((t, LANES), F32), jax.ShapeDtypeStruct((1, LANES), F32)],
        compiler_params=pltpu.CompilerParams(dimension_semantics=("arbitrary",)), name="fgate_bwd")(z, bf, dc)


def _cast_rows(src_ref, dst_ref, t, rows, fn):
    def cp(c, carry):
        r0 = pl.multiple_of(c * rows, rows)
        dst_ref[pl.ds(r0, rows), :] = fn(src_ref[pl.ds(r0, rows), :]).astype(dst_ref.dtype)
        return carry

    lax.fori_loop(0, t // rows, cp, 0)


def _attn_groups(t):
    tq = _tile(t, 256)
    nq = t // tq
    grp = 4 if nq % 4 == 0 else 1
    return tq, nq, grp


def _attn_fwd(z, crow, carry=None):
    t = z.shape[0]
    tq, nq, grp = _attn_groups(t)
    tk = grp * tq
    scale = HEAD_DIM ** -0.5

    def body(q_ref, k_ref, v_ref, cr_ref, o_ref, lse_ref, kb_s, vb_s):
        lane = lax.broadcasted_iota(jnp.int32, (1, LANES), 1)
        hmask = [(lane // HEAD_DIM) == hh for hh in range(2)]
        _cast_rows(k_ref, kb_s, t, tq, lambda v: v)
        _cast_rows(v_ref, vb_s, t, tq, lambda v: v)

        def qblock(g, r):
            q0 = pl.multiple_of((g * grp + r) * tq, tq)
            qv = q_ref[pl.ds(q0, tq), :] * scale
            qa = [jnp.where(hmask[hh], qv, 0.0).astype(BF16) for hh in range(2)]

            def update(st, k0, width, off):
                kb = kb_s[pl.ds(k0, width), :]
                vb = vb_s[pl.ds(k0, width), :]
                new = []
                for hh in range(2):
                    m, l, acc = st[hh]
                    s = _dot(qa[hh], kb, NT) - cr_ref[hh, :, pl.ds(k0, width)]
                    if off is not None:
                        keep = (lax.broadcasted_iota(jnp.int32, (tq, width), 0) + off
                                >= lax.broadcasted_iota(jnp.int32, (tq, width), 1))
                        s = jnp.where(keep, s, NEG)
                    m_new = jnp.maximum(m, jnp.max(s, axis=-1, keepdims=True))
                    p = jnp.exp(s - m_new)
                    corr = jnp.exp(m - m_new)
                    new.append((m_new, corr * l + jnp.sum(p, axis=-1, keepdims=True), corr * acc + _dot(p, vb, NN)))
                return tuple(new)

            one = (jnp.full((tq, 1), NEG, F32), jnp.zeros((tq, 1), F32), jnp.zeros((tq, LANES), F32))
            st = lax.fori_loop(0, g, lambda j, st: update(st, pl.multiple_of(j * tk, tk), tk, None), (one, one))
            st = update(st, pl.multiple_of(g * tk, tk), (r + 1) * tq, r * tq)
            o_ref[pl.ds(q0, tq), :] = jnp.where(hmask[0], st[0][2] / st[0][1], st[1][2] / st[1][1])
            for hh in range(2):
                lse_ref[hh, pl.ds(q0, tq), :] = st[hh][0] + jnp.log(st[hh][1])

        def group(g, carry):
            for r in range(grp):
                qblock(g, r)
            return carry

        lax.fori_loop(0, nq // grp, group, 0)

    base = 2 * D_A // LANES
    nh = D_B // LANES
    col = lambda off: pl.BlockSpec((t, LANES), lambda p: (0, off + p))
    return _call(
        body, (nh,), [col(base), col(base + nh), col(base + 2 * nh), pl.BlockSpec((2, 1, t), lambda p: (p, 0, 0))],
        [col(0), pl.BlockSpec((2, t, 1), lambda p: (p, 0, 0))],
        [jax.ShapeDtypeStruct((t, D_B), F32), jax.ShapeDtypeStruct((N_HEADS, t, 1), F32)],
        [pltpu.VMEM((t, LANES), BF16), pltpu.VMEM((t, LANES), BF16)], ("parallel",), "attn_fwd", [z, z, z, crow], carry)


def _attn_bwd(z, crow, lse, do, carry=None):
    t = z.shape[0]
    tq, nq, grp = _attn_groups(t)
    tw = grp * tq
    scale = HEAD_DIM ** -0.5

    def body(q_ref, k_ref, v_ref, cr_ref, lse_ref, do_ref, dq_ref, dk_ref, dv_ref, dc_ref, qa_s, da_s, kb_s, vb_s, dl_s):
        lane = lax.broadcasted_iota(jnp.int32, (1, LANES), 1)
        hmask = [(lane // HEAD_DIM) == hh for hh in range(2)]
        _cast_rows(k_ref, kb_s, t, tq, lambda v: v)
        _cast_rows(v_ref, vb_s, t, tq, lambda v: v)
        for hh in range(2):
            _cast_rows(q_ref, qa_s.at[hh], t, tq, lambda v, hh=hh: jnp.where(hmask[hh], v * scale, 0.0))
            _cast_rows(do_ref, da_s.at[hh], t, tq, lambda v, hh=hh: jnp.where(hmask[hh], v, 0.0))
        _cast_rows(q_ref, dq_ref, t, tq, lambda v: jnp.zeros_like(v))

        def probs(hh, q0, nq_rows, k0, nk_rows, off):
            s = _dot(qa_s[hh, pl.ds(q0, nq_rows), :], kb_s[pl.ds(k0, nk_rows), :], NT) - cr_ref[hh, :, pl.ds(k0, nk_rows)]
            p = jnp.exp(s - lse_ref[hh, pl.ds(q0, nq_rows), :])
            if off is not None:
                keep = (lax.broadcasted_iota(jnp.int32, (nq_rows, nk_rows), 0) + off
                        >= lax.broadcasted_iota(jnp.int32, (nq_rows, nk_rows), 1))
                p = jnp.where(keep, p, 0.0)
            return p, _dot(da_s[hh, pl.ds(q0, nq_rows), :], vb_s[pl.ds(k0, nk_rows), :], NT)

        def delta(g, r):
            q0 = pl.multiple_of((g * grp + r) * tq, tq)

            def add(k0, width, off, acc):
                res = []
                for hh in range(2):
                    p, dp = probs(hh, q0, tq, k0, width, off)
                    res.append(acc[hh] + jnp.sum(p * dp, axis=-1, keepdims=True))
                return tuple(res)

            zcol = jnp.zeros((tq, 1), F32)
            acc = lax.fori_loop(0, g, lambda j, acc: add(pl.multiple_of(j * tw, tw), tw, None, acc), (zcol, zcol))
            acc = add(pl.multiple_of(g * tw, tw), (r + 1) * tq, r * tq, acc)
            for hh in range(2):
                dl_s[hh, pl.ds(q0, tq), :] = acc[hh]

        def delta_group(g, carry):
            for r in range(grp):
                delta(g, r)
            return carry

        lax.fori_loop(0, nq // grp, delta_group, 0)

        def kblock(g, r):
            k0 = pl.multiple_of((g * grp + r) * tq, tq)
            kb = kb_s[pl.ds(k0, tq), :]

            def upd(q0, height, off, st):
                dk, dv, dc = st[0], st[1], [st[2], st[3]]
                dqs = []
                for hh in range(2):
                    p, dp = probs(hh, q0, height, k0, tq, off)
                    ds = p * (dp - dl_s[hh, pl.ds(q0, height), :])
                    dv = dv + _dot(p, da_s[hh, pl.ds(q0, height), :], TN)
                    dk = dk + _dot(ds, qa_s[hh, pl.ds(q0, height), :], TN)
                    dqs.append(_dot(ds, kb, NN))
                    dc[hh] = dc[hh] - jnp.sum(ds, axis=0, keepdims=True)
                dq_ref[pl.ds(q0, height), :] += jnp.where(hmask[0], dqs[0], dqs[1]) * scale
                return dk, dv, dc[0], dc[1]

            zero = jnp.zeros((tq, LANES), F32)
            zrow = jnp.zeros((1, tq), F32)
            st = upd(k0, (grp - r) * tq, 0, (zero, zero, zrow, zrow))
            st = lax.fori_loop(g + 1, nq // grp, lambda i, st: upd(pl.multiple_of(i * tw, tw), tw, None, st), st)
            dk_ref[pl.ds(k0, tq), :] = st[0]
            dv_ref[pl.ds(k0, tq), :] = st[1]
            for hh in range(2):
                dc_ref[hh, :, pl.ds(k0, tq)] = st[2 + hh]

        def kgroup(g, carry):
            for r in range(grp):
                kblock(g, r)
            return carry

        lax.fori_loop(0, nq // grp, kgroup, 0)

    base = 2 * D_A // LANES
    nh = D_B // LANES
    col = lambda off: pl.BlockSpec((t, LANES), lambda p: (0, off + p))
    ccs = pl.BlockSpec((2, t, 1), lambda p: (p, 0, 0))
    crs = pl.BlockSpec((2, 1, t), lambda p: (p, 0, 0))
    return _call(
        body, (nh,), [col(base), col(base + nh), col(base + 2 * nh), crs, ccs, col(0)], [col(0), col(0), col(0), crs],
        [jax.ShapeDtypeStruct((t, D_B), F32)] * 3 + [jax.ShapeDtypeStruct((N_HEADS, 1, t), F32)],
        [pltpu.VMEM((2, t, LANES), BF16), pltpu.VMEM((2, t, LANES), BF16), pltpu.VMEM((t, LANES), BF16),
         pltpu.VMEM((t, LANES), BF16), pltpu.VMEM((2, t, 1), F32)], ("parallel",), "attn_bwd", [z, z, z, crow, lse, do], carry)


def _s5_disc(a_re, a_im, log_dt, b_re, b_im):
    dt = jnp.exp(log_dt)
    mag = jnp.exp(a_re * dt)
    ar = mag * jnp.cos(a_im * dt)
    ai = mag * jnp.sin(a_im * dt)
    den = a_re * a_re + a_im * a_im
    kr = ((ar - 1.0) * a_re + ai * a_im) / den
    ki = (ai * a_re - (ar - 1.0) * a_im) / den
    kr3, ki3 = kr[:, None, :], ki[:, None, :]
    return ar, ai, kr3 * b_re - ki3 * b_im, kr3 * b_im + ki3 * b_re


def _s5_prep(a_re, a_im, log_dt, b_re, b_im):
    g, p = a_re.shape
    gc = b_re.shape[1]

    def body(*refs):
        res = _s5_disc(*[r[...] for r in refs[:5]])
        for o_ref, v in zip(refs[5:], res):
            o_ref[...] = v

    sds = jax.ShapeDtypeStruct
    return pl.pallas_call(body, out_shape=[sds((g, p), F32), sds((g, p), F32), sds((g, gc, p), F32), sds((g, gc, p), F32)],
                          name="s5_prep")(a_re, a_im, log_dt, b_re, b_im)


def _s5_prep_bwd(a_re, a_im, log_dt, b_re, b_im, d_ar, d_ai, d_br, d_bi):
    ins = (a_re, a_im, log_dt, b_re, b_im)

    def body(*refs):
        vals = [r[...] for r in refs[:5]]
        cts = tuple(r[...] for r in refs[5:9])
        _, pull = jax.vjp(_s5_disc, *vals)
        for o_ref, v in zip(refs[9:], pull(cts)):
            o_ref[...] = v

    return pl.pallas_call(body, out_shape=[jax.ShapeDtypeStruct(a.shape, F32) for a in ins], name="s5_prep_bwd")(
        *ins, d_ar, d_ai, d_br, d_bi)


def _s5_scan_rows(t, ar, ai, hr_s, hi_s, off, reverse):
    n = ar.shape[1]
    if reverse:
        ai = -ai
    sub = lax.broadcasted_iota(jnp.int32, (8, n), 0)
    cmul = lambda xr, xi, yr, yi: (xr * yr - xi * yi, xr * yi + xi * yr)
    pw = [(ar, ai)]
    for _ in range(7):
        pw.append(cmul(*pw[-1], ar, ai))
    pr = jnp.zeros((8, n), F32)
    pi = jnp.zeros((8, n), F32)
    for r in range(8):
        k = 7 - r if reverse else r
        pr = jnp.where(sub == r, pw[k][0], pr)
        pi = jnp.where(sub == r, pw[k][1], pi)

    def step(g, carry):
        cr, ci = carry
        r0 = pl.multiple_of(off + (t // 8 - 1 - g if reverse else g) * 8, 8)
        br = hr_s[pl.ds(r0, 8), :]
        bi = hi_s[pl.ds(r0, 8), :]
        for s in (1, 2, 4):
            ok = (sub < 8 - s) if reverse else (sub >= s)
            shift = 8 - s if reverse else s
            sr = jnp.where(ok, pltpu.roll(br, shift, 0), 0.0)
            si = jnp.where(ok, pltpu.roll(bi, shift, 0), 0.0)
            mr, mi = cmul(pw[s - 1][0], pw[s - 1][1], sr, si)
            br, bi = br + mr, bi + mi
        mr, mi = cmul(pr, pi, cr, ci)
        br, bi = br + mr, bi + mi
        hr_s[pl.ds(r0, 8), :] = br
        hi_s[pl.ds(r0, 8), :] = bi
        edge = sub == (0 if reverse else 7)
        return (jnp.sum(jnp.where(edge, br, 0.0), axis=0, keepdims=True),
                jnp.sum(jnp.where(edge, bi, 0.0), axis=0, keepdims=True))

    zero = jnp.zeros((1, n), F32)
    lax.fori_loop(0, t // 8, step, (zero, zero))


def _s5_fwd(z, bd_re, bd_im, ab_re, ab_im, cd_re, cd_im, dvec, carry=None):
    t = z.shape[0]
    cr = _tile(t, 256)
    ns = N_STATE // 2

    def body(u_ref, br_ref, bi_ref, ar_ref, ai_ref, cre_ref, cim_ref, d_ref, y_ref, hr_s, hi_s):
        def p1(c, carry):
            t0 = pl.multiple_of(c * cr, cr)
            u = u_ref[pl.ds(t0, cr), :]
            hr_s[pl.ds(t0, cr), :] = _dot(u, br_ref[...], NN)
            hi_s[pl.ds(t0, cr), :] = _dot(u, bi_ref[...], NN)
            return carry

        lax.fori_loop(0, t // cr, p1, 0)
        _s5_scan_rows(t, ar_ref[...], ai_ref[...], hr_s, hi_s, 0, False)

        def p3(c, carry):
            t0 = pl.multiple_of(c * cr, cr)
            y_ref[pl.ds(t0, cr), :] = (_dot(hr_s[pl.ds(t0, cr), :], cre_ref[...], NN)
                                       - _dot(hi_s[pl.ds(t0, cr), :], cim_ref[...], NN)
                                       + d_ref[...] * u_ref[pl.ds(t0, cr), :])
            return carry

        lax.fori_loop(0, t // cr, p3, 0)

    blk = lambda r, c: pl.BlockSpec((None, r, c), lambda b: (b, 0, 0))
    return _call(
        body, (2,),
        [pl.BlockSpec((t, LANES), lambda b: (0, Z_U // LANES + b)), blk(LANES, ns), blk(LANES, ns), blk(1, ns),
         blk(1, ns), blk(ns, LANES), blk(ns, LANES), pl.BlockSpec((1, LANES), lambda b: (0, b))],
        [pl.BlockSpec((t, LANES), lambda b: (0, b))], [jax.ShapeDtypeStruct((t, D_C), F32)],
        [pltpu.VMEM((t, ns), F32), pltpu.VMEM((t, ns), F32)], ("arbitrary",), "s5_fwd",
        [z, bd_re, bd_im, ab_re, ab_im, cd_re, cd_im, dvec], carry)


def _s5_bwd(z, dy, bd_re, bd_im, ab_re, ab_im, cd_re, cd_im, dvec):
    t = z.shape[0]
    cr = _tile(t, 256)
    ns = N_STATE // 2

    def body(u_ref, dy_ref, br_ref, bi_ref, ar_ref, ai_ref, cre_ref, cim_ref, d_ref,
             du_ref, dbr_ref, dbi_ref, dar_ref, dai_ref, dcre_ref, dcim_ref, dd_ref, hr_s, hi_s, gr_s, gi_s):
        zero8 = jnp.zeros((8, ns), F32)
        hr_s[pl.ds(0, 8), :] = zero8
        hi_s[pl.ds(0, 8), :] = zero8
        for ref in (dbr_ref, dbi_ref, dar_ref, dai_ref, dcre_ref, dcim_ref, dd_ref):
            ref[...] = jnp.zeros(ref.shape, F32)

        def p1(c, carry):
            t0 = pl.multiple_of(c * cr, cr)
            u = u_ref[pl.ds(t0, cr), :]
            hr_s[pl.ds(t0 + 8, cr), :] = _dot(u, br_ref[...], NN)
            hi_s[pl.ds(t0 + 8, cr), :] = _dot(u, bi_ref[...], NN)
            return carry

        lax.fori_loop(0, t // cr, p1, 0)
        _s5_scan_rows(t, ar_ref[...], ai_ref[...], hr_s, hi_s, 8, False)

        def p3(c, carry):
            t0 = pl.multiple_of(c * cr, cr)
            dyv = dy_ref[pl.ds(t0, cr), :]
            u = u_ref[pl.ds(t0, cr), :]
            gr_s[pl.ds(t0, cr), :] = _dot(dyv, cre_ref[...], NT)
            gi_s[pl.ds(t0, cr), :] = -_dot(dyv, cim_ref[...], NT)
            dcre_ref[...] += _dot(hr_s[pl.ds(t0 + 8, cr), :], dyv, TN)
            dcim_ref[...] -= _dot(hi_s[pl.ds(t0 + 8, cr), :], dyv, TN)
            dd_ref[...] += jnp.sum(dyv * u, axis=0, keepdims=True)
            du_ref[pl.ds(t0, cr), :] = dyv * d_ref[...]
            return carry

        lax.fori_loop(0, t // cr, p3, 0)
        _s5_scan_rows(t, ar_ref[...], ai_ref[...], gr_s, gi_s, 0, True)

        def p5(c, carry):
            t0 = pl.multiple_of(c * cr, cr)
            u = u_ref[pl.ds(t0, cr), :]
            gr = gr_s[pl.ds(t0, cr), :]
            gi = gi_s[pl.ds(t0, cr), :]
            dbr_ref[...] += _dot(u, gr, TN)
            dbi_ref[...] += _dot(u, gi, TN)
            du_ref[pl.ds(t0, cr), :] += _dot(gr, br_ref[...], NT) + _dot(gi, bi_ref[...], NT)
            hpr = pltpu.roll(hr_s[pl.ds(t0, cr + 8), :], 1, 0)[8:, :]
            hpi = pltpu.roll(hi_s[pl.ds(t0, cr + 8), :], 1, 0)[8:, :]
            dar_ref[...] += jnp.sum(gr * hpr + gi * hpi, axis=0, keepdims=True)
            dai_ref[...] += jnp.sum(gi * hpr - gr * hpi, axis=0, keepdims=True)
            return carry

        lax.fori_loop(0, t // cr, p5, 0)

    blk = lambda r, c: pl.BlockSpec((None, r, c), lambda b: (b, 0, 0))
    ucol = pl.BlockSpec((t, LANES), lambda b: (0, Z_U // LANES + b))
    ycol = pl.BlockSpec((t, LANES), lambda b: (0, b))
    dsp = pl.BlockSpec((1, LANES), lambda b: (0, b))
    sds = jax.ShapeDtypeStruct
    return pl.pallas_call(
        body, grid=(2,),
        in_specs=[ucol, ycol, blk(LANES, ns), blk(LANES, ns), blk(1, ns), blk(1, ns), blk(ns, LANES), blk(ns, LANES), dsp],
        out_specs=[ycol, blk(LANES, ns), blk(LANES, ns), blk(1, ns), blk(1, ns), blk(ns, LANES), blk(ns, LANES), dsp],
        out_shape=[sds((t, D_C), F32), sds((2, LANES, ns), F32), sds((2, LANES, ns), F32), sds((2, 1, ns), F32),
                   sds((2, 1, ns), F32), sds((2, ns, LANES), F32), sds((2, ns, LANES), F32), sds((1, D_C), F32)],
        scratch_shapes=[pltpu.VMEM((t + 8, ns), F32), pltpu.VMEM((t + 8, ns), F32), pltpu.VMEM((t, ns), F32),
                        pltpu.VMEM((t, ns), F32)],
        compiler_params=pltpu.CompilerParams(dimension_semantics=("arbitrary",)), name="s5_bwd")(
            z, dy, bd_re, bd_im, ab_re, ab_im, cd_re, cd_im, dvec)


def _mix_out(out_a, out_b, yc, x1, ga, gb, gc, wglu, wout, ln_g, ln_b):
    yg = _gelu(yc)
    out_c = yg * _sig(_bdot(yg, wglu))
    o = jnp.concatenate([_rms(out_a, ga), _rms(out_b, gb), _rms(out_c, gc)], axis=-1)
    return (_ln(ALPHA * x1 + _bdot(o, wout), ln_g, ln_b),)


def _ln_only(pre, g, b):
    return (_ln(pre, g, b),)


def _loss_head(y, target, tm):
    t, d = y.shape

    def body(y_ref, t_ref, dy_ref, l_ref):
        i = pl.program_id(0)
        e = y_ref[...] - t_ref[...]
        dy_ref[...] = e * (1.0 / d)
        part = 0.5 * jnp.sum(jnp.sum(e * e, axis=-1, keepdims=True) * (1.0 / d), axis=0, keepdims=True)
        row = jnp.where(lax.broadcasted_iota(jnp.int32, (1, LANES), 1) == 0, part, 0.0)

        @pl.when(i == 0)
        def _():
            l_ref[...] = row

        @pl.when(i > 0)
        def _():
            l_ref[...] += row

    spec = pl.BlockSpec((tm, d), lambda i: (i, 0))
    return pl.pallas_call(
        body, grid=(t // tm,), in_specs=[spec, spec], out_specs=[spec, pl.BlockSpec((1, LANES), lambda i: (0, 0))],
        out_shape=[jax.ShapeDtypeStruct((t, d), F32), jax.ShapeDtypeStruct((1, LANES), F32)],
        compiler_params=pltpu.CompilerParams(dimension_semantics=("arbitrary",)), name="loss_head")(y, target)


def _adamw(w, g, m, v, name):
    r, c = w.shape
    tr = r
    for cand in (512, 256, 352, 128):
        if r % cand == 0:
            tr = cand
            break

    def body(w_ref, g_ref, m_ref, v_ref, d_ref, nm_ref, nv_ref):
        gv = g_ref[...]
        mn = ADAM_B1 * m_ref[...] + (1.0 - ADAM_B1) * gv
        vn = ADAM_B2 * v_ref[...] + (1.0 - ADAM_B2) * (gv * gv)
        m_hat = mn / (1.0 - ADAM_B1 ** ADAM_STEP)
        v_hat = vn / (1.0 - ADAM_B2 ** ADAM_STEP)
        d_ref[...] = -ADAM_LR * (m_hat / (jnp.sqrt(v_hat) + ADAM_EPS) + ADAM_WD * w_ref[...])
        nm_ref[...] = mn
        nv_ref[...] = vn

    spec = pl.BlockSpec((tr, c), lambda i: (i, 0))
    return pl.pallas_call(
        body, grid=(r // tr,), in_specs=[spec] * 4, out_specs=[spec] * 3,
        out_shape=[jax.ShapeDtypeStruct((r, c), F32)] * 3,
        compiler_params=pltpu.CompilerParams(dimension_semantics=("parallel",)), name=name)(w, g, m, v)


def _row_tile(r):
    for cand in (512, 448, 352, 256, 128):
        if r % cand == 0:
            return cand
    return r


def _pair_add(a, b, idx, name, out_dtype):
    _, r, w = a.shape
    tr = _row_tile(r)

    def body(i_ref, a_ref, b_ref, o_ref):
        o_ref[...] = (a_ref[...].astype(F32) + b_ref[...].astype(F32)).astype(o_ref.dtype)

    grid_spec = pltpu.PrefetchScalarGridSpec(
        num_scalar_prefetch=1, grid=(4, r // tr),
        in_specs=[pl.BlockSpec((None, tr, w), lambda q, i, s: (2 * q + s[0], i, 0)),
                  pl.BlockSpec((None, tr, w), lambda q, i, s: (q, i, 0))],
        out_specs=pl.BlockSpec((None, tr, w), lambda q, i, s: (q, i, 0)))
    return pl.pallas_call(body, grid_spec=grid_spec, out_shape=jax.ShapeDtypeStruct((4, r, w), out_dtype),
                          compiler_params=pltpu.CompilerParams(dimension_semantics=("parallel", "parallel")), name=name)(
                              idx, a, b)


def _quad_add(p, rb, idx, name):
    _, r, w = p.shape
    tr = _row_tile(r)

    def body(i_ref, p_ref, r0, r1, r2, o_ref):
        o_ref[...] = ((p_ref[...].astype(F32) + r0[...].astype(F32)) + r1[...].astype(F32)) + r2[...].astype(F32)

    grid_spec = pltpu.PrefetchScalarGridSpec(
        num_scalar_prefetch=1, grid=(r // tr,),
        in_specs=[pl.BlockSpec((None, tr, w), lambda i, s: (s[0], i, 0))]
        + [pl.BlockSpec((None, tr, w), functools.partial(lambda i, s, k: (k, i, 0), k=k)) for k in range(3)],
        out_specs=pl.BlockSpec((tr, w), lambda i, s: (i, 0)))
    return pl.pallas_call(body, grid_spec=grid_spec, out_shape=jax.ShapeDtypeStruct((r, w), F32),
                          compiler_params=pltpu.CompilerParams(dimension_semantics=("parallel",)), name=name)(
                              idx, p, rb, rb, rb)


def _gather_small(buf):
    def plan(x, y, c, ins, outs):
        dst = outs[0].at[4 * x + 2 * y + c]
        res = [(ins[0], dst, None)]
        for rel in range(1, 8):
            res.append((ins[0], dst, (x ^ (rel >> 2), y ^ ((rel >> 1) & 1), c ^ (rel & 1))))
        return res

    return _Carry([buf], [jax.ShapeDtypeStruct((8,) + buf.shape, buf.dtype)], {}, plan, 8)


def _sum_small(allb):
    _, r, w = allb.shape

    def body(a_ref, o_ref):
        s = a_ref[0]
        for k in range(1, 8):
            s = s + a_ref[k]
        o_ref[...] = s

    return pl.pallas_call(body, out_shape=jax.ShapeDtypeStruct((r, w), F32), name="ar_small_sum")(allb)


def _pad_rows(a, rows):
    return jnp.pad(a, ((0, rows - a.shape[0]), (0, 0)))


def _pack_small(arrs):
    rows, tail = [], []
    for a in arrs:
        if not tail and a.size % LANES == 0:
            rows.append(a.reshape(-1, LANES))
        else:
            tail.append(a.reshape(-1))
    n_rows = sum(r.shape[0] for r in rows)
    n_tail = sum(int(v.size) for v in tail)
    tail_rows = -(-n_tail // LANES)
    total_rows = n_rows + tail_rows + (-(n_rows + tail_rows)) % 8
    if tail:
        tail.append(jnp.zeros((tail_rows * LANES - n_tail,), F32))
        rows.append(jnp.concatenate(tail).reshape(tail_rows, LANES))
    if total_rows > n_rows + tail_rows:
        rows.append(jnp.zeros((total_rows - n_rows - tail_rows, LANES), F32))
    return jnp.concatenate(rows, axis=0)


def _unpack_small(buf, shapes):
    out, off = [], 0
    flat = buf.reshape(-1)
    for s in shapes:
        n = int(np.prod(s))
        out.append(flat[off:off + n].reshape(s))
        off += n
    return out


def _block_diag(blocks, nb):
    m, r, c = blocks.shape
    n = m // nb
    eye = jnp.eye(n, dtype=blocks.dtype)
    return (blocks.reshape(nb, n, r, 1, c) * eye[None, :, None, :, None]).reshape(nb, n * r, n * c)


def _diag_blocks(dense, n):
    nb = dense.shape[0]
    r, c = dense.shape[1] // n, dense.shape[2] // n
    eye = jnp.eye(n, dtype=dense.dtype)
    return jnp.sum(dense.reshape(nb, n, r, n, c) * eye[None, :, None, :, None], axis=3).reshape(nb * n, r, c)


def kernel(x, ffn1_w_gate, ffn1_w_up, ffn1_w_down, ln1_g, ln1_b, w_in, conv_w, conv_b, rg_w_a, rg_b_a, rg_w_x, rg_b_x, rg_lambda, fox_b_f, s5_a_re, s5_a_im, s5_log_dt, s5_b_re, s5_b_im, s5_c_re, s5_c_im, s5_d, s5_w_glu, mix_norm_g, w_out, ln2_g, ln2_b, ffn2_w_gate, ffn2_w_up, ffn2_w_down, ln3_g, ln3_b, loss_target, m_ffn1_w_gate, m_ffn1_w_up, m_ffn1_w_down, m_ln1_g, m_ln1_b, m_w_in, m_conv_w, m_conv_b, m_rg_w_a, m_rg_b_a, m_rg_w_x, m_rg_b_x, m_rg_lambda, m_fox_b_f, m_s5_a_re, m_s5_a_im, m_s5_log_dt, m_s5_b_re, m_s5_b_im, m_s5_c_re, m_s5_c_im, m_s5_d, m_s5_w_glu, m_mix_norm_g, m_w_out, m_ln2_g, m_ln2_b, m_ffn2_w_gate, m_ffn2_w_up, m_ffn2_w_down, m_ln3_g, m_ln3_b, v_ffn1_w_gate, v_ffn1_w_up, v_ffn1_w_down, v_ln1_g, v_ln1_b, v_w_in, v_conv_w, v_conv_b, v_rg_w_a, v_rg_b_a, v_rg_w_x, v_rg_b_x, v_rg_lambda, v_fox_b_f, v_s5_a_re, v_s5_a_im, v_s5_log_dt, v_s5_b_re, v_s5_b_im, v_s5_c_re, v_s5_c_im, v_s5_d, v_s5_w_glu, v_mix_norm_g, v_w_out, v_ln2_g, v_ln2_b, v_ffn2_w_gate, v_ffn2_w_up, v_ffn2_w_down, v_ln3_g, v_ln3_b):
    a = dict(locals())
    w = {n: a[n] for n in WEIGHTS}
    t, d = x.shape[1], x.shape[2]
    f = ffn1_w_down.shape[1] * 8
    fs, ds = f // 8, d // 8
    mx, my, mc = lax.axis_index("x"), lax.axis_index("y"), lax.axis_index("c")
    me = 4 * mx + 2 * my + mc
    tm = _tile(t, 512)
    tf = f // 2
    win_rows = ds * Z_W // d

    FFN1, MIXW, FFN2 = ['g1', 'u1', 'd1'], ['win', 'wout', 'glu', 'conv'], ['g2', 'u2', 'd2']
    glu_rows = D_C * D_C // (8 * d)

    def shard_segs(l):
        wi = w['w_in'][l]
        win_p = jnp.concatenate([wi[:, :Z_F + N_HEADS], jnp.zeros((ds, Z_U - Z_F - N_HEADS), F32), wi[:, Z_F + N_HEADS:]], axis=1)
        conv_bits = lax.bitcast_convert_type(w['conv_w'][l], BF16).reshape(1, -1)
        segs = dict(g1=w['ffn1_w_gate'][l].T, u1=w['ffn1_w_up'][l].T, d1=w['ffn1_w_down'][l],
                    g2=w['ffn2_w_gate'][l].T, u2=w['ffn2_w_up'][l].T, d2=w['ffn2_w_down'][l],
                    win=win_p.reshape(win_rows, d), wout=w['w_out'][l], glu=_pad_rows(w['s5_w_glu'][l].reshape(-1, d), 16))
        segs = {k: v.astype(BF16) for k, v in segs.items()}
        segs['conv'] = _pad_rows(jnp.pad(conv_bits, ((0, 0), (0, d - conv_bits.shape[1]))), 16)
        return segs

    shards = [shard_segs(l) for l in range(DEPTH)]
    wts = {}

    def cat(keys):
        return jnp.concatenate([shards[l][k] for l, k in keys], axis=0)

    def split(g, keys):
        off = 0
        for l, k in keys:
            r = shards[l][k].shape[0]
            wts[(l, k)] = g[:, off:off + r]
            off += r

    grp_a = [(0, k) for k in FFN1]
    grp_b = [(0, k) for k in MIXW]
    grp_c = [(0, k) for k in FFN2]
    grp_d = [(1, k) for k in FFN1]
    grp_e = [(1, k) for k in FFN2]
    grp_f = [(1, k) for k in MIXW]
    (g_a,) = _run(_ag_chips(cat(grp_a)), "ag_chips")
    (g_a,) = _run(_ag_sibling(g_a), "ag_sibling")
    split(g_a, grp_a)

    xs = x[0]
    saved = []
    cur = xs
    for l in range(DEPTH):
        row = lambda n: w[n][l].reshape(1, -1)
        ffn = lambda keys: tuple(wts[(l, k)].reshape(f, d) for k in keys)
        wa = _block_diag(w['rg_w_a'][l], 3)
        wx = _block_diag(w['rg_w_x'][l], 3)
        bf = jnp.pad(row('fox_b_f'), ((0, 0), (0, LANES - N_HEADS)))
        s5p = (w['s5_a_re'][l], w['s5_a_im'][l], w['s5_log_dt'][l].reshape(-1, 1),
               w['s5_b_re'][l].transpose(0, 2, 1), w['s5_b_im'][l].transpose(0, 2, 1))
        ab_re, ab_im, bb_re, bb_im = _s5_prep(*s5p)
        bd_re, bd_im = _block_diag(bb_re, 2), _block_diag(bb_im, 2)
        cd_re = _block_diag(w['s5_c_re'][l].transpose(0, 2, 1), 2)
        cd_im = _block_diag(w['s5_c_im'][l].transpose(0, 2, 1), 2)
        abr, abi = ab_re.reshape(2, 1, N_STATE // 2), ab_im.reshape(2, 1, N_STATE // 2)
        gm = row('mix_norm_g')
        ga, gb, gc = gm[:, :D_A], gm[:, D_A:D_A + D_B], gm[:, D_A + D_B:]

        x0 = cur
        ffn1 = ffn(FFN1)
        (x1, pre1, gs1, us1), cres = _ffn_fwd(x0, *ffn1, row('ln1_g'), row('ln1_b'), tm, tf,
                                              carry=_ag_chips(cat(grp_b if l == 0 else grp_e)))
        if l == 0:
            (g_b,) = _run(_ag_sibling(cres[0]), "ag_sibling")
            split(g_b, grp_b)
        else:
            g_e = cres[0]
        win = wts[(l, 'win')].reshape(d, Z_W)
        wout = wts[(l, 'wout')].reshape(d, d).astype(F32)
        wglu = wts[(l, 'glu')][:, :glu_rows].reshape(D_C, D_C).astype(F32)
        conv_full = lax.bitcast_convert_type(
            wts[(l, 'conv')][:, 0, :2 * CONV_WIDTH * D_A // 8].reshape(8, CONV_WIDTH, D_A // 8, 2), F32)
        conv_full = conv_full.transpose(1, 0, 2).reshape(CONV_WIDTH, D_A)
        z = _mm(x1, win, 'nn', F32, tm, Z_W, d, "mix_in")
        out_a, h_a = _rg_fwd(z, conv_full, row('conv_b'), wa, row('rg_b_a'), wx, row('rg_b_x'), row('rg_lambda'))
        cs = _fgate_fwd(z, bf)
        crow = cs[:, :N_HEADS].T.reshape(N_HEADS, 1, t)
        (out_b, lse), cres = _attn_fwd(z, crow, carry=_ag_chips(cat(grp_c)) if l == 0 else _ag_sibling(g_e))
        if l == 0:
            (yc,), cres = _s5_fwd(z, bd_re, bd_im, abr, abi, cd_re, cd_im, row('s5_d'),
                                  carry=_join(_ag_sibling(cres[0]), _ag_chips(cat(grp_f))))
            split(cres[0], grp_c)
            g_f = cres[1]
        else:
            split(cres[0], grp_e)
            (yc,), _ = _s5_fwd(z, bd_re, bd_im, abr, abi, cd_re, cd_im, row('s5_d'))
        mix_params = [ga, gb, gc, wglu, wout, row('ln2_g'), row('ln2_b')]
        (x2,) = _rowwise(_mix_out, [out_a, out_b, yc, x1], mix_params, [(d, F32)], _tile(t, 256), "mix_out")
        ffn2 = ffn(FFN2)
        (x3, pre3, gs2, us2), cres = _ffn_fwd(x2, *ffn2, row('ln3_g'), row('ln3_b'), tm, tf,
                                              carry=_join(_ag_chips(cat(grp_d)), _ag_sibling(g_f)) if l == 0 else None)
        if l == 0:
            split(cres[1], grp_f)
            (g_d,) = _run(_ag_sibling(cres[0]), "ag_sibling")
            split(g_d, grp_d)
        saved.append(dict(x0=x0, x1=x1, pre1=pre1, gs1=gs1, us1=us1, z=z, out_a=out_a, h_a=h_a, crow=crow,
                          out_b=out_b, lse=lse, yc=yc, x2=x2, pre3=pre3, gs2=gs2, us2=us2, mix_params=mix_params,
                          ffn1=ffn1, ffn2=ffn2, win=win, conv_full=conv_full, wa=wa, wx=wx, bf=bf, s5p=s5p,
                          s5m=(bd_re, bd_im, abr, abi, cd_re, cd_im)))
        cur = x3

    dy, loss_row = _loss_head(cur, loss_target[0], tm)
    loss = lax.psum(loss_row[0, 0], ("x", "y", "c"))

    assert DEPTH == 2
    small_grads = {}
    small_names = ['conv_w'] + SMALL
    c_idx = jnp.reshape(mc, (1,)).astype(jnp.int32)
    chip_idx = jnp.reshape(2 * mx + my, (1,)).astype(jnp.int32)

    def blocks(arrs):
        return jnp.concatenate([v.astype(BF16).reshape(8, -1, d) for v in arrs], axis=1)

    def mixer_blocks(dwin, dwout, dwglu):
        glu = jnp.pad(dwglu.astype(BF16).reshape(8, glu_rows, d), ((0, 0), (0, 32 - glu_rows), (0, 0)))
        return jnp.concatenate([dwin.reshape(8, win_rows, d), dwout.astype(BF16).reshape(8, ds, d), glu], axis=1)

    def pair(full, ra):
        return _pair_add(full, ra, c_idx, "rs_add_sibling", BF16)

    for l in reversed(range(DEPTH)):
        s = saved[l]
        row = lambda n: w[n][l].reshape(1, -1)
        wg_tiles = dict(tm=tf, tn=d, tk=_tile(t, 1024))
        first = l == 0

        def ffn_back(dyv, pre, gs, us, wts3, xin, ln_g, ln_b, carry_fn=None, pipeline=False):
            (dpre,), (dlg, dlb), _ = _rowwise_vjp(_ln_only, [pre], [ln_g, ln_b], [dyv], tm, "ln_bwd")
            (dx, dg, du, hh), cres = _ffn_bwd(dpre, gs, us, *wts3, tm, tf, carry=carry_fn(dlg, dlb) if carry_fn else None)
            if not pipeline:
                dwg = _mm(dg, xin, 'tn', BF16, name="ffn_dw_gate", **wg_tiles)
                dwu = _mm(du, xin, 'tn', BF16, name="ffn_dw_up", **wg_tiles)
                dwd = _mm(hh, dpre, 'tn', BF16, name="ffn_dw_down", **wg_tiles)
                return dx, (dwg, dwu, dwd), dlg, dlb, cres

            def front(dw):
                full = dw.reshape(8, fs, d)
                (ra,) = _run(_rs_sibling(full), "rs_sibling")
                return pair(full, ra)

            p_g = front(_mm(dg, xin, 'tn', BF16, name="ffn_dw_gate", **wg_tiles))
            dwu, (rb_g,) = _mm(du, xin, 'tn', BF16, name="ffn_dw_up", carry=_rs_chips(p_g), **wg_tiles)
            p_u = front(dwu)
            dwd, (rb_u,) = _mm(hh, dpre, 'tn', BF16, name="ffn_dw_down", carry=_rs_chips(p_u), **wg_tiles)
            p_d = front(dwd)
            (rb_d,) = _run(_rs_chips(p_d), "rs_chips")
            return dx, ((p_g, rb_g), (p_u, rb_u), (p_d, rb_d)), dlg, dlb, cres

        dx2, (dwg2, dwu2, dwd2), dl3g, dl3b, cres = ffn_back(
            dy, s['pre3'], s['gs2'], s['us2'], s['ffn2'], s['x2'], row('ln3_g'), row('ln3_b'),
            (lambda *_: _rs_sibling(full_l1)) if first else None)
        if first:
            part_l1 = pair(full_l1, cres[0])
            full_c2 = blocks([dwg2, dwu2, dwd2])
        (d_oa, d_ob, d_yc, d_x1), (dga, dgb, dgc, dwglu, dwout, dl2g, dl2b), cres = _rowwise_vjp(
            _mix_out, [s['out_a'], s['out_b'], s['yc'], s['x1']], s['mix_params'], [dx2], _tile(t, 256), "mix_out_bwd",
            carry=_rs_sibling(full_c2) if first else None)
        if first:
            part_c2 = pair(full_c2, cres[0])
        (d_ax, d_ag, dcw, dcb, dwa, dba, dwx, dbx, dlam) = _rg_bwd(
            s['z'], s['h_a'], d_oa, s['conv_full'], row('conv_b'), s['wa'], row('rg_b_a'), s['wx'], row('rg_b_x'), row('rg_lambda'))
        (dq, dk, dv, dcrow), cres = _attn_bwd(s['z'], s['crow'], s['lse'], d_ob,
                                              carry=_join(_rs_chips(part_l1), _rs_chips(part_c2)) if first else None)
        if first:
            rb_l1, rb_c2 = cres
        dc_pad = jnp.pad(dcrow.reshape(N_HEADS, t).T, ((0, 0), (0, LANES - N_HEADS)))
        dzf, dbf = _fgate_bwd(s['z'], s['bf'], dc_pad)
        du_c, dbd_re, dbd_im, dabr, dabi, dcd_re, dcd_im, dd = _s5_bwd(s['z'], d_yc, *s['s5m'], row('s5_d'))
        dz = jnp.concatenate([d_ax, d_ag, dq, dk, dv, dzf, du_c], axis=1)
        dx1 = _mm(dz, s['win'], 'nt', F32, tm, d, Z_W, "mix_in_dx", add=d_x1)
        dwin = _mm(s['x1'], dz, 'tn', BF16, d, Z_W, tm, "mix_in_dw")
        dbb_re, dbb_im = _diag_blocks(dbd_re, N_GROUPS // 2), _diag_blocks(dbd_im, N_GROUPS // 2)
        dcm_re, dcm_im = _diag_blocks(dcd_re, N_GROUPS // 2), _diag_blocks(dcd_im, N_GROUPS // 2)
        da_re, da_im, dlog_dt, db_re, db_im = _s5_prep_bwd(*s['s5p'], dabr.reshape(N_GROUPS, C_STATE), dabi.reshape(N_GROUPS, C_STATE), dbb_re, dbb_im)
        sg = dict(conv_w=dcw, conv_b=dcb, rg_w_a=_diag_blocks(dwa, 2), rg_b_a=dba,
                  rg_w_x=_diag_blocks(dwx, 2), rg_b_x=dbx, rg_lambda=dlam, fox_b_f=dbf[:, :N_HEADS],
                  s5_a_re=da_re, s5_a_im=da_im, s5_log_dt=dlog_dt, s5_b_re=db_re.transpose(0, 2, 1), s5_b_im=db_im.transpose(0, 2, 1),
                  s5_c_re=dcm_re.transpose(0, 2, 1), s5_c_im=dcm_im.transpose(0, 2, 1), s5_d=dd,
                  mix_norm_g=jnp.concatenate([dga, dgb, dgc], axis=1), ln2_g=dl2g, ln2_b=dl2b, ln3_g=dl3g, ln3_b=dl3b)
        small_grads[l] = sg

        if first:
            full_m = mixer_blocks(dwin, dwout, dwglu)
            (ra_m,) = _run(_rs_sibling(full_m), "rs_sibling")
            part_m = pair(full_m, ra_m)

            def last_carry(dlg, dlb):
                sg.update(ln1_g=dlg, ln1_b=dlb)
                packed = _pack_small([small_grads[ll][n] for n in small_names for ll in range(DEPTH)])
                return _join(_rs_chips(part_m), _gather_small(packed))

            dx0, tail, _, _, (rb_m, all_small) = ffn_back(dx1, s['pre1'], s['gs1'], s['us1'], s['ffn1'], s['x0'], row('ln1_g'),
                                                          row('ln1_b'), last_carry, pipeline=True)
        else:
            dx0, (dwg1, dwu1, dwd1), dl1g, dl1b, _ = ffn_back(dx1, s['pre1'], s['gs1'], s['us1'], s['ffn1'], s['x0'],
                                                              row('ln1_g'), row('ln1_b'))
            sg.update(ln1_g=dl1g, ln1_b=dl1b)
            full_l1 = jnp.concatenate([blocks([dwg1, dwu1, dwd1, dwg2, dwu2, dwd2]), mixer_blocks(dwin, dwout, dwglu)], axis=1)
        dy = dx0

    grad_x = dy.reshape(x.shape)
    quad = lambda part, rb: _quad_add(part, rb, chip_idx, "rs_add_chips")
    own = {}

    def take(rows_f32, keys, l):
        off = 0
        for k, r in keys:
            own[(l, k)] = rows_f32[off:off + r]
            off += r

    ffn_keys = lambda names: [(k, fs) for k in names]
    mix_keys = [('win', win_rows), ('wout', ds), ('glu', glu_rows)]
    take(quad(part_l1, rb_l1), ffn_keys(FFN1 + FFN2) + mix_keys, 1)
    take(quad(part_c2, rb_c2), ffn_keys(FFN2), 0)
    take(quad(part_m, rb_m), mix_keys, 0)
    for k, (part, rb) in zip(FFN1, tail):
        own[(0, k)] = quad(part, rb)

    grads = {}
    grads_t = {}
    for k, n in zip(FFN1 + FFN2, ['ffn1_w_gate', 'ffn1_w_up', 'ffn1_w_down', 'ffn2_w_gate', 'ffn2_w_up', 'ffn2_w_down']):
        stacked = jnp.stack([own[(l, k)] for l in range(DEPTH)])
        if 'down' in n:
            grads[n] = stacked
        else:
            grads_t[n] = stacked
            grads[n] = jnp.swapaxes(stacked, 1, 2)
    gwin = jnp.stack([own[(l, 'win')].reshape(ds, Z_W) for l in range(DEPTH)])
    grads['w_in'] = jnp.concatenate([gwin[:, :, :Z_F + N_HEADS], gwin[:, :, Z_U:]], axis=2)
    grads['w_out'] = jnp.stack([own[(l, 'wout')] for l in range(DEPTH)])
    grads['s5_w_glu'] = jnp.stack([own[(l, 'glu')].reshape(D_C // 8, D_C) for l in range(DEPTH)])

    small_shapes = [((DEPTH, CONV_WIDTH, D_A) if n == 'conv_w' else w[n].shape) for n in small_names]
    conv_zero = jnp.zeros((DEPTH, CONV_WIDTH, D_A), F32)
    summed = _sum_small(all_small)
    for n, g in zip(small_names, _unpack_small(summed, small_shapes)):
        grads[n] = g
    grads['conv_w'] = lax.dynamic_slice_in_dim(grads['conv_w'], me * (D_A // 8), D_A // 8, axis=2)

    delta, new_m, new_v = {}, {}, {}
    for n in BIG + ['conv_w']:
        if n in grads_t:
            sh = grads_t[n].shape
            two = lambda v: jnp.swapaxes(v, 1, 2).reshape(-1, sh[-1])
            back = lambda v: jnp.swapaxes(v.reshape(sh), 1, 2)
            g2 = grads_t[n].reshape(-1, sh[-1])
        else:
            sh = w[n].shape
            two = lambda v: v.reshape(-1, sh[-1])
            back = lambda v: v.reshape(sh)
            g2 = two(grads[n])
        dl, nm, nv = _adamw(two(w[n]), g2, two(a['m_' + n]), two(a['v_' + n]), "adamw_" + n)
        delta[n], new_m[n], new_v[n] = back(dl), back(nm), back(nv)
    dl, nm, nv = _adamw(_pack_small([conv_zero] + [w[n] for n in SMALL]), summed,
                        _pack_small([conv_zero] + [a['m_' + n] for n in SMALL]),
                        _pack_small([conv_zero] + [a['v_' + n] for n in SMALL]), "adamw_small")
    for n, v1, v2, v3 in zip(small_names[1:], _unpack_small(dl, small_shapes)[1:], _unpack_small(nm, small_shapes)[1:],
                             _unpack_small(nv, small_shapes)[1:]):
        delta[n], new_m[n], new_v[n] = v1, v2, v3

    return (loss, grad_x, *[grads[n] for n in WEIGHTS], *[delta[n] for n in WEIGHTS], *[new_m[n] for n in WEIGHTS],
            *[new_v[n] for n in WEIGHTS])
```

```python
import functools
import math

import jax
import jax.numpy as jnp
import numpy as np
from jax import lax
from jax.experimental import pallas as pl
from jax.experimental.pallas import tpu as pltpu

F32 = jnp.float32
BF16 = jnp.bfloat16
MESH = pl.DeviceIdType.MESH

DEPTH = 2
ALPHA = (2 * DEPTH) ** 0.25
LN_EPS = 1e-5
RMS_EPS = 1e-6
RG_C = 8.0
CONV_WIDTH = 4
HEAD_DIM = 64
C_GROUP = 16
C_STATE = 64
D_A = 384
D_B = 384
D_C = 256
N_HEADS = D_B // HEAD_DIM
N_GROUPS = D_C // C_GROUP
N_STATE = N_GROUPS * C_STATE
Z_F = 2 * D_A + 3 * D_B
Z_U = Z_F + 128
Z_W = Z_U + D_C
N_IN = Z_F + N_HEADS + D_C
ADAM_LR, ADAM_B1, ADAM_B2, ADAM_EPS, ADAM_WD, ADAM_STEP = 0.001, 0.9, 0.999, 1e-08, 0.01, 10
LANES = 128
NEG = -1e30

WEIGHTS = ['ffn1_w_gate', 'ffn1_w_up', 'ffn1_w_down', 'ln1_g', 'ln1_b', 'w_in', 'conv_w', 'conv_b', 'rg_w_a', 'rg_b_a',
           'rg_w_x', 'rg_b_x', 'rg_lambda', 'fox_b_f', 's5_a_re', 's5_a_im', 's5_log_dt', 's5_b_re', 's5_b_im', 's5_c_re',
           's5_c_im', 's5_d', 's5_w_glu', 'mix_norm_g', 'w_out', 'ln2_g', 'ln2_b', 'ffn2_w_gate', 'ffn2_w_up', 'ffn2_w_down',
           'ln3_g', 'ln3_b']
BIG = ['ffn1_w_gate', 'ffn1_w_up', 'ffn1_w_down', 'w_in', 's5_w_glu', 'w_out', 'ffn2_w_gate', 'ffn2_w_up', 'ffn2_w_down']
SMALL_TAIL = ['fox_b_f', 's5_log_dt']
SMALL = [n for n in WEIGHTS if n not in BIG and n != 'conv_w' and n not in SMALL_TAIL] + SMALL_TAIL


def _sig(x):
    return 1.0 / (1.0 + jnp.exp(-x))


def _gelu(x):
    return 0.5 * x * (1.0 + jnp.tanh(math.sqrt(2.0 / math.pi) * (x + 0.044715 * (x * x * x))))


def _softplus(x):
    return jnp.maximum(x, 0.0) + jnp.log(1.0 + jnp.exp(jnp.minimum(x, -x)))


def _dot(a, b, dims):
    return lax.dot_general(a.astype(BF16), b.astype(BF16), (dims, ((), ())), preferred_element_type=F32)


NN = ((1,), (0,))
NT = ((1,), (1,))
TN = ((0,), (0,))


@jax.custom_vjp
def _bdot(a, w):
    return _dot(a, w, NN)


def _bdot_fwd(a, w):
    return _dot(a, w, NN), (a, w)


def _bdot_bwd(res, ct):
    a, w = res
    return _dot(ct, w, NT), _dot(a, ct, TN)


_bdot.defvjp(_bdot_fwd, _bdot_bwd)


def _ln(pre, g, b):
    mu = jnp.mean(pre, axis=-1, keepdims=True)
    xc = pre - mu
    var = jnp.mean(xc * xc, axis=-1, keepdims=True)
    return xc * lax.rsqrt(var + LN_EPS) * g + b


def _rms(x, g):
    return x * lax.rsqrt(jnp.mean(x * x, axis=-1, keepdims=True) + RMS_EPS) * g


def _tile(n, want):
    return want if n % want == 0 else n


class _Carry:
    def __init__(self, ins, outs, aliases, plan, n):
        self.ins, self.outs, self.aliases, self.plan, self.n = list(ins), list(outs), dict(aliases), plan, n


def _join(a, b):
    na, ma = len(a.ins), len(a.outs)

    def plan(x, y, c, ins, outs):
        return a.plan(x, y, c, ins[:na], outs[:ma]) + b.plan(x, y, c, ins[na:], outs[ma:])

    aliases = dict(a.aliases)
    aliases.update({na + i: ma + j for i, j in b.aliases.items()})
    return _Carry(a.ins + b.ins, a.outs + b.outs, aliases, plan, a.n + b.n)


def _copies(carry, cins, couts, send, recv):
    x, y, c = lax.axis_index("x"), lax.axis_index("y"), lax.axis_index("c")
    res = []
    for k, (s, d, peer) in enumerate(carry.plan(x, y, c, cins, couts)):
        if peer is None:
            res.append(pltpu.make_async_copy(s, d, send.at[k]))
        else:
            res.append(pltpu.make_async_remote_copy(src_ref=s, dst_ref=d, send_sem=send.at[k], recv_sem=recv.at[k],
                                                    device_id=peer, device_id_type=MESH))
    return res


def _call(body, grid, in_specs, out_specs, out_shape, scratch, semantics, name, args, carry=None):
    n_in, n_out, n_scr = len(in_specs), len(out_specs), len(scratch)
    if carry is None:
        res = pl.pallas_call(body, grid=grid, in_specs=in_specs, out_specs=out_specs, out_shape=out_shape,
                             scratch_shapes=scratch, compiler_params=pltpu.CompilerParams(dimension_semantics=semantics),
                             name=name)(*args)
        return list(res), []
    nci, nco = len(carry.ins), len(carry.outs)

    def wrapped(*refs):
        o0 = n_in + nci
        s0 = o0 + n_out + nco
        cins, couts = refs[n_in:o0], refs[o0 + n_out:s0]
        send, recv = refs[s0 + n_scr:]
        first = functools.reduce(jnp.logical_and, [pl.program_id(k) == 0 for k in range(len(grid))])
        last = functools.reduce(jnp.logical_and, [pl.program_id(k) == grid[k] - 1 for k in range(len(grid))])

        @pl.when(first)
        def _():
            for cp in _copies(carry, cins, couts, send, recv):
                cp.start()

        body(*refs[:n_in], *refs[o0:o0 + n_out], *refs[s0:s0 + n_scr])

        @pl.when(last)
        def _():
            for cp in _copies(carry, cins, couts, send, recv):
                cp.wait()

    hbm = pl.BlockSpec(memory_space=pl.ANY)
    res = pl.pallas_call(
        wrapped, grid=grid, in_specs=list(in_specs) + [hbm] * nci, out_specs=list(out_specs) + [hbm] * nco,
        out_shape=list(out_shape) + carry.outs, scratch_shapes=list(scratch) + [pltpu.SemaphoreType.DMA((carry.n,))] * 2,
        input_output_aliases={n_in + i: n_out + j for i, j in carry.aliases.items()},
        compiler_params=pltpu.CompilerParams(dimension_semantics=("arbitrary",) * len(grid), has_side_effects=True),
        name=name)(*args, *carry.ins)
    return list(res[:n_out]), list(res[n_out:])


def _run(carry, name):
    nci, nco = len(carry.ins), len(carry.outs)

    def body(*refs):
        cps = _copies(carry, refs[:nci], refs[nci:nci + nco], refs[-2], refs[-1])
        for cp in cps:
            cp.start()
        for cp in cps:
            cp.wait()

    hbm = pl.BlockSpec(memory_space=pl.ANY)
    return pl.pallas_call(
        body, in_specs=[hbm] * nci, out_specs=[hbm] * nco, out_shape=carry.outs, input_output_aliases=carry.aliases,
        scratch_shapes=[pltpu.SemaphoreType.DMA((carry.n,))] * 2, compiler_params=pltpu.CompilerParams(has_side_effects=True),
        name=name)(*carry.ins)


def _ag_chips(shards):
    def plan(x, y, c, ins, outs):
        res = []
        for src, out in zip(ins, outs):
            dst = out.at[4 * x + 2 * y + c]
            res += [(src, dst, None)] + [(src, dst, (px, py, c)) for px, py in ((1 - x, y), (x, 1 - y), (1 - x, 1 - y))]
        return res

    return _Carry(shards, [jax.ShapeDtypeStruct((8,) + s.shape, s.dtype) for s in shards], {}, plan, 4 * len(shards))


def _ag_sibling(gs):
    def plan(x, y, c, ins, outs):
        return [(out.at[2 * q + c], out.at[2 * q + c], (x, y, 1 - c)) for out in outs for q in range(4)]

    return _Carry(gs, [jax.ShapeDtypeStruct(g.shape, g.dtype) for g in gs], {i: i for i in range(len(gs))}, plan, 4 * len(gs))


def _rs_sibling(full):
    def plan(x, y, c, ins, outs):
        return [(ins[0].at[2 * q + (1 - c)], outs[0].at[q], (x, y, 1 - c)) for q in range(4)]

    return _Carry([full], [jax.ShapeDtypeStruct((4,) + full.shape[1:], full.dtype)], {}, plan, 4)


def _rs_chips(part):
    def plan(x, y, c, ins, outs):
        res = []
        for k, (dx, dy) in enumerate(((1, 0), (0, 1), (1, 1))):
            tx, ty = x ^ dx, y ^ dy
            res.append((ins[0].at[2 * tx + ty], outs[0].at[k], (tx, ty, c)))
        return res

    return _Carry([part], [jax.ShapeDtypeStruct((3,) + part.shape[1:], part.dtype)], {}, plan, 3)


def _mm(a, b, dims, out_dtype, tm, tn, tk, name, add=None, carry=None):
    if dims == 'nn':
        (m, k), n = a.shape, b.shape[1]
        a_spec = pl.BlockSpec((tm, tk), lambda i, j, q: (i, q))
        b_spec = pl.BlockSpec((tk, tn), lambda i, j, q: (q, j))
        dn = NN
    elif dims == 'nt':
        (m, k), n = a.shape, b.shape[0]
        a_spec = pl.BlockSpec((tm, tk), lambda i, j, q: (i, q))
        b_spec = pl.BlockSpec((tn, tk), lambda i, j, q: (j, q))
        dn = NT
    else:
        (k, m), n = a.shape, b.shape[1]
        a_spec = pl.BlockSpec((tk, tm), lambda i, j, q: (q, i))
        b_spec = pl.BlockSpec((tk, tn), lambda i, j, q: (q, j))
        dn = TN
    nk = k // tk
    o_spec = pl.BlockSpec((tm, tn), lambda i, j, q: (i, j))

    def body(*refs):
        if add is None:
            a_ref, b_ref, o_ref, acc_ref = refs
        else:
            a_ref, b_ref, add_ref, o_ref, acc_ref = refs
        q = pl.program_id(2)
        part = _dot(a_ref[...], b_ref[...], dn)

        @pl.when(q == 0)
        def _():
            acc_ref[...] = part

        @pl.when(q > 0)
        def _():
            acc_ref[...] += part

        @pl.when(q == nk - 1)
        def _():
            r = acc_ref[...]
            if add is not None:
                r = r + add_ref[...]
            o_ref[...] = r.astype(o_ref.dtype)

    ins = [a, b] + ([] if add is None else [add])
    specs = [a_spec, b_spec] + ([] if add is None else [o_spec])
    (res,), cres = _call(body, (m // tm, n // tn, nk), specs, [o_spec], [jax.ShapeDtypeStruct((m, n), out_dtype)],
                         [pltpu.VMEM((tm, tn), F32)], ("parallel", "parallel", "arbitrary"), name, ins, carry)
    return res if carry is None else (res, cres)


def _rowwise(fn, rows, params, outs, tm, name):
    t = rows[0].shape[0]
    nr, npar = len(rows), len(params)

    def body(*refs):
        r = [x[...] for x in refs[:nr]]
        p = [x[...] for x in refs[nr:nr + npar]]
        res = fn(*r, *p)
        for o_ref, o in zip(refs[nr + npar:], res):
            o_ref[...] = o.astype(o_ref.dtype)

    in_specs = ([pl.BlockSpec((tm, a.shape[1]), lambda i: (i, 0)) for a in rows]
                + [pl.BlockSpec(p.shape, lambda i: (0, 0)) for p in params])
    return pl.pallas_call(
        body, grid=(t // tm,), in_specs=in_specs,
        out_specs=[pl.BlockSpec((tm, c), lambda i: (i, 0)) for c, _ in outs],
        out_shape=[jax.ShapeDtypeStruct((t, c), d) for c, d in outs],
        compiler_params=pltpu.CompilerParams(dimension_semantics=("parallel",)), name=name)(*rows, *params)


def _rowwise_vjp(fn, rows, params, cots, tm, name, carry=None):
    t = rows[0].shape[0]
    nr, npar, nc = len(rows), len(params), len(cots)

    def body(*refs):
        r = [x[...] for x in refs[:nr]]
        p = [x[...] for x in refs[nr:nr + npar]]
        c = [x[...] for x in refs[nr + npar:nr + npar + nc]]
        o_refs = refs[nr + npar + nc:]
        _, pull = jax.vjp(fn, *r, *p)
        grads = pull(tuple(c))
        for o_ref, g in zip(o_refs[:nr], grads[:nr]):
            o_ref[...] = g
        i = pl.program_id(0)

        @pl.when(i == 0)
        def _():
            for o_ref, g in zip(o_refs[nr:], grads[nr:]):
                o_ref[...] = g

        @pl.when(i > 0)
        def _():
            for o_ref, g in zip(o_refs[nr:], grads[nr:]):
                o_ref[...] += g

    row_spec = lambda a: pl.BlockSpec((tm, a.shape[1]), lambda i: (i, 0))
    par_spec = lambda p: pl.BlockSpec(p.shape, lambda i: (0, 0))
    res, cres = _call(
        body, (t // tm,),
        [row_spec(a) for a in rows] + [par_spec(p) for p in params] + [row_spec(a) for a in cots],
        [row_spec(a) for a in rows] + [par_spec(p) for p in params],
        [jax.ShapeDtypeStruct(a.shape, F32) for a in rows] + [jax.ShapeDtypeStruct(p.shape, F32) for p in params],
        [], ("arbitrary",), name, [*rows, *params, *cots], carry)
    return res[:nr], res[nr:], cres


def _ffn_fwd(x, wgt, wut, wd, ln_g, ln_b, tm, tf, carry=None):
    t, d = x.shape
    f = wgt.shape[0]
    nj = f // tf

    def body(x_ref, wg_ref, wu_ref, wd_ref, g_ref, b_ref, y_ref, pre_ref, gs_ref, us_ref, acc_ref):
        j = pl.program_id(1)
        xv = x_ref[...]
        xb = xv.astype(BF16)
        g = _dot(xb, wg_ref[...], NT)
        u = _dot(xb, wu_ref[...], NT)
        gs_ref[...] = g.astype(BF16)
        us_ref[...] = u.astype(BF16)
        part = _dot(g * _sig(g) * u, wd_ref[...], NN)

        @pl.when(j == 0)
        def _():
            acc_ref[...] = part

        @pl.when(j > 0)
        def _():
            acc_ref[...] += part

        @pl.when(j == nj - 1)
        def _():
            pre = ALPHA * xv + 0.5 * acc_ref[...]
            pre_ref[...] = pre
            y_ref[...] = _ln(pre, g_ref[...], b_ref[...])

    w_spec = pl.BlockSpec((tf, d), lambda i, j: (j, 0))
    x_spec = pl.BlockSpec((tm, d), lambda i, j: (i, 0))
    v_spec = pl.BlockSpec((1, d), lambda i, j: (0, 0))
    h_spec = pl.BlockSpec((tm, tf), lambda i, j: (i, j))
    return _call(
        body, (t // tm, nj), [x_spec, w_spec, w_spec, w_spec, v_spec, v_spec], [x_spec, x_spec, h_spec, h_spec],
        [jax.ShapeDtypeStruct((t, d), F32), jax.ShapeDtypeStruct((t, d), F32),
         jax.ShapeDtypeStruct((t, f), BF16), jax.ShapeDtypeStruct((t, f), BF16)],
        [pltpu.VMEM((tm, d), F32)], ("parallel", "arbitrary"), "ffn_fwd", [x, wgt, wut, wd, ln_g, ln_b], carry)


def _ffn_bwd(dpre, gs, us, wgt, wut, wd, tm, tf, carry=None):
    t, d = dpre.shape
    f = wgt.shape[0]
    nj = f // tf

    def body(dp_ref, gs_ref, us_ref, wg_ref, wu_ref, wd_ref, dx_ref, dg_ref, du_ref, hh_ref, acc_ref):
        j = pl.program_id(1)
        dp = dp_ref[...]
        dh = _dot(0.5 * dp, wd_ref[...], NT)
        g = gs_ref[...].astype(F32)
        u = us_ref[...].astype(F32)
        s = _sig(g)
        sl = g * s
        dg = (dh * u * (s * (1.0 + g * (1.0 - s)))).astype(BF16)
        du = (dh * sl).astype(BF16)
        dg_ref[...] = dg
        du_ref[...] = du
        hh_ref[...] = (0.5 * sl * u).astype(BF16)
        part = _dot(dg, wg_ref[...], NN) + _dot(du, wu_ref[...], NN)

        @pl.when(j == 0)
        def _():
            acc_ref[...] = part

        @pl.when(j > 0)
        def _():
            acc_ref[...] += part

        @pl.when(j == nj - 1)
        def _():
            dx_ref[...] = ALPHA * dp + acc_ref[...]

    w_spec = pl.BlockSpec((tf, d), lambda i, j: (j, 0))
    x_spec = pl.BlockSpec((tm, d), lambda i, j: (i, 0))
    h_spec = pl.BlockSpec((tm, tf), lambda i, j: (i, j))
    return _call(
        body, (t // tm, nj), [x_spec, h_spec, h_spec, w_spec, w_spec, w_spec], [x_spec, h_spec, h_spec, h_spec],
        [jax.ShapeDtypeStruct((t, d), F32)] + [jax.ShapeDtypeStruct((t, f), BF16)] * 3,
        [pltpu.VMEM((tm, d), F32)], ("parallel", "arbitrary"), "ffn_bwd", [dpre, gs, us, wgt, wut, wd], carry)


def _scan8(a_ref, b_ref, out_ref, t, reverse=False):
    w = out_ref.shape[-1]
    sub = lax.broadcasted_iota(jnp.int32, (8, w), 0)

    def step(g, carry):
        r0 = pl.multiple_of((t // 8 - 1 - g if reverse else g) * 8, 8)
        bv = b_ref[pl.ds(r0, 8), :]
        av = None if a_ref is None else a_ref[pl.ds(r0, 8), :]
        for s in (1, 2, 4):
            ok = (sub < 8 - s) if reverse else (sub >= s)
            shift = 8 - s if reverse else s
            b_sh = jnp.where(ok, pltpu.roll(bv, shift, 0), 0.0)
            if av is None:
                bv = bv + b_sh
            else:
                bv = av * b_sh + bv
                av = av * jnp.where(ok, pltpu.roll(av, shift, 0), 1.0)
        h = bv + carry if av is None else bv + av * carry
        out_ref[pl.ds(r0, 8), :] = h
        return jnp.sum(jnp.where(sub == (0 if reverse else 7), h, 0.0), axis=0, keepdims=True)

    lax.fori_loop(0, t // 8, step, jnp.zeros((1, w), F32))


def _rg_local(xa, wa, ba, wx, bx, lam):
    r = _sig(_bdot(xa, wa) + ba)
    i = _sig(_bdot(xa, wx) + bx)
    log_a = -RG_C * r * _softplus(-lam)
    a = jnp.exp(log_a)
    mult = jnp.sqrt(-jnp.tanh(log_a) * (a * a + 1.0))
    return a, mult * (i * xa)


def _conv_taps(ext, n):
    return [ext[8:, :]] + [pltpu.roll(ext, s, 0)[8:, :] for s in (1, 2, 3)]


def _rg_fwd(z, cw, cb, wa, ba, wx, bx, lam):
    t = z.shape[0]
    cr = _tile(t, 256)
    nb = D_A // LANES

    def body(ax_ref, ag_ref, cw_ref, cb_ref, wa_ref, ba_ref, wx_ref, bx_ref, lam_ref, out_ref, h_ref, axp, a_s, b_s):
        axp[pl.ds(0, 8), :] = jnp.zeros((8, LANES), F32)
        pltpu.sync_copy(ax_ref, axp.at[pl.ds(8, t)])
        w = [cw_ref[pl.ds(k, 1), :] for k in range(CONV_WIDTH)]

        def chunk(c, carry):
            t0 = pl.multiple_of(c * cr, cr)
            taps = _conv_taps(axp[pl.ds(t0, cr + 8), :], cr)
            xa = cb_ref[...] + w[3] * taps[0] + w[2] * taps[1] + w[1] * taps[2] + w[0] * taps[3]
            a, gated = _rg_local(xa, wa_ref[...], ba_ref[...], wx_ref[...], bx_ref[...], lam_ref[...])
            a_s[pl.ds(t0, cr), :] = a
            b_s[pl.ds(t0, cr), :] = gated
            return carry

        lax.fori_loop(0, t // cr, chunk, 0)

        _scan8(a_s, b_s, h_ref, t)

        def fin(c, carry):
            t0 = pl.multiple_of(c * cr, cr)
            out_ref[pl.ds(t0, cr), :] = _gelu(ag_ref[pl.ds(t0, cr), :]) * h_ref[pl.ds(t0, cr), :]
            return carry

        lax.fori_loop(0, t // cr, fin, 0)

    col = lambda off: pl.BlockSpec((t, LANES), lambda b: (0, off + b))
    vec = pl.BlockSpec((1, LANES), lambda b: (0, b))
    mat = pl.BlockSpec((None, LANES, LANES), lambda b: (b, 0, 0))
    return pl.pallas_call(
        body, grid=(nb,),
        in_specs=[col(0), col(nb), pl.BlockSpec((CONV_WIDTH, LANES), lambda b: (0, b)), vec, mat, vec, mat, vec, vec],
        out_specs=[col(0), col(0)],
        out_shape=[jax.ShapeDtypeStruct((t, D_A), F32), jax.ShapeDtypeStruct((t, D_A), F32)],
        scratch_shapes=[pltpu.VMEM((t + 8, LANES), F32), pltpu.VMEM((t, LANES), F32), pltpu.VMEM((t, LANES), F32)],
        compiler_params=pltpu.CompilerParams(dimension_semantics=("arbitrary",)), name="rglru_fwd")(
            z, z, cw, cb, wa, ba, wx, bx, lam)


def _rg_bwd(z, h, dout, cw, cb, wa, ba, wx, bx, lam):
    t = z.shape[0]
    cr = _tile(t, 256)
    nb = D_A // LANES

    def body(ax_ref, ag_ref, h_ref, do_ref, cw_ref, cb_ref, wa_ref, ba_ref, wx_ref, bx_ref, lam_ref,
             dax_ref, dag_ref, dcw_ref, dcb_ref, dwa_ref, dba_ref, dwx_ref, dbx_ref, dlam_ref,
             axp, hp, xa_s, a_s, g_s, dxa_s, u_s):
        zero8 = jnp.zeros((8, LANES), F32)
        axp[pl.ds(0, 8), :] = zero8
        hp[pl.ds(0, 8), :] = zero8
        dxa_s[pl.ds(t, 8), :] = zero8
        u_s[pl.ds(t, 8), :] = zero8
        pltpu.sync_copy(ax_ref, axp.at[pl.ds(8, t)])
        pltpu.sync_copy(h_ref, hp.at[pl.ds(8, t)])
        w = [cw_ref[pl.ds(k, 1), :] for k in range(CONV_WIDTH)]
        for ref in (dcw_ref, dcb_ref, dwa_ref, dba_ref, dwx_ref, dbx_ref, dlam_ref):
            ref[...] = jnp.zeros(ref.shape, F32)

        def p1(c, carry):
            t0 = pl.multiple_of(c * cr, cr)
            taps = _conv_taps(axp[pl.ds(t0, cr + 8), :], cr)
            xa = cb_ref[...] + w[3] * taps[0] + w[2] * taps[1] + w[1] * taps[2] + w[0] * taps[3]
            a, _ = _rg_local(xa, wa_ref[...], ba_ref[...], wx_ref[...], bx_ref[...], lam_ref[...])
            xa_s[pl.ds(t0, cr), :] = xa
            a_s[pl.ds(t0, cr), :] = a
            ag = ag_ref[pl.ds(t0, cr), :]
            dov = do_ref[pl.ds(t0, cr), :]
            gel, pull = jax.vjp(_gelu, ag)
            g_s[pl.ds(t0, cr), :] = dov * gel
            u_s[pl.ds(t0, cr), :] = a * (dov * gel)
            dag_ref[pl.ds(t0, cr), :] = pull(dov * h_ref[pl.ds(t0, cr), :])[0]
            return carry

        lax.fori_loop(0, t // cr, p1, 0)
        _scan8(a_s, u_s, u_s, t, reverse=True)

        def p3(c, carry):
            t0 = pl.multiple_of(c * cr, cr)
            g = g_s[pl.ds(t0, cr), :] + pltpu.roll(u_s[pl.ds(t0, cr + 8), :], cr + 7, 0)[:cr, :]
            h_prev = pltpu.roll(hp[pl.ds(t0, cr + 8), :], 1, 0)[8:, :]
            _, pull = jax.vjp(_rg_local, xa_s[pl.ds(t0, cr), :], wa_ref[...], ba_ref[...], wx_ref[...], bx_ref[...],
                              lam_ref[...])
            dxa, dwa, dba, dwx, dbx, dlam = pull((g * h_prev, g))
            dxa_s[pl.ds(t0, cr), :] = dxa
            dwa_ref[...] += dwa
            dba_ref[...] += dba
            dwx_ref[...] += dwx
            dbx_ref[...] += dbx
            dlam_ref[...] += dlam
            return carry

        lax.fori_loop(0, t // cr, p3, 0)

        def p4(c, carry):
            t0 = pl.multiple_of(c * cr, cr)
            ext = dxa_s[pl.ds(t0, cr + 8), :]
            n = cr + 8
            ahead = [ext[:cr, :]] + [pltpu.roll(ext, n - s, 0)[:cr, :] for s in (1, 2, 3)]
            dax_ref[pl.ds(t0, cr), :] = w[3] * ahead[0] + w[2] * ahead[1] + w[1] * ahead[2] + w[0] * ahead[3]
            taps = _conv_taps(axp[pl.ds(t0, cr + 8), :], cr)
            dxa = ahead[0]
            for k in range(CONV_WIDTH):
                dcw_ref[pl.ds(k, 1), :] += jnp.sum(dxa * taps[CONV_WIDTH - 1 - k], axis=0, keepdims=True)
            dcb_ref[...] += jnp.sum(dxa, axis=0, keepdims=True)
            return carry

        lax.fori_loop(0, t // cr, p4, 0)

    col = lambda off: pl.BlockSpec((t, LANES), lambda b: (0, off + b))
    vec = pl.BlockSpec((1, LANES), lambda b: (0, b))
    mat = pl.BlockSpec((None, LANES, LANES), lambda b: (b, 0, 0))
    cws = pl.BlockSpec((CONV_WIDTH, LANES), lambda b: (0, b))
    sds = jax.ShapeDtypeStruct
    return pl.pallas_call(
        body, grid=(nb,),
        in_specs=[col(0), col(nb), col(0), col(0), cws, vec, mat, vec, mat, vec, vec],
        out_specs=[col(0), col(0), cws, vec, mat, vec, mat, vec, vec],
        out_shape=[sds((t, D_A), F32), sds((t, D_A), F32), sds((CONV_WIDTH, D_A), F32), sds((1, D_A), F32),
                   sds((nb, LANES, LANES), F32), sds((1, D_A), F32), sds((nb, LANES, LANES), F32), sds((1, D_A), F32),
                   sds((1, D_A), F32)],
        scratch_shapes=[pltpu.VMEM((t + 8, LANES), F32), pltpu.VMEM((t + 8, LANES), F32), pltpu.VMEM((t, LANES), F32),
                        pltpu.VMEM((t, LANES), F32), pltpu.VMEM((t, LANES), F32), pltpu.VMEM((t + 8, LANES), F32),
                        pltpu.VMEM((t + 8, LANES), F32)],
        compiler_params=pltpu.CompilerParams(dimension_semantics=("arbitrary",)), name="rglru_bwd")(
            z, z, h, dout, cw, cb, wa, ba, wx, bx, lam)


def _fgate_fwd(z, bf):
    t = z.shape[0]

    def body(zf_ref, bf_ref, c_ref):
        c_ref[...] = -_softplus(-(zf_ref[...] + bf_ref[...]))
        _scan8(None, c_ref, c_ref, t)

    return pl.pallas_call(
        body, grid=(1,), in_specs=[pl.BlockSpec((t, LANES), lambda i: (0, Z_F // LANES)), pl.BlockSpec((1, LANES), lambda i: (0, 0))],
        out_specs=pl.BlockSpec((t, LANES), lambda i: (0, 0)), out_shape=jax.ShapeDtypeStruct((t, LANES), F32),
        compiler_params=pltpu.CompilerParams(dimension_semantics=("arbitrary",)), name="fgate_fwd")(z, bf)


def _fgate_bwd(z, bf, dc):
    t = z.shape[0]

    def body(zf_ref, bf_ref, dc_ref, dz_ref, db_ref):
        _scan8(None, dc_ref, dz_ref, t, reverse=True)
        dz = dz_ref[...] * _sig(-(zf_ref[...] + bf_ref[...]))
        dz_ref[...] = dz
        db_ref[...] = jnp.sum(dz, axis=0, keepdims=True)

    return pl.pallas_call(
        body, grid=(1,),
        in_specs=[pl.BlockSpec((t, LANES), lambda i: (0, Z_F // LANES)), pl.BlockSpec((1, LANES), lambda i: (0, 0)),
                  pl.BlockSpec((t, LANES), lambda i: (0, 0))],
        out_specs=[pl.BlockSpec((t, LANES), lambda i: (0, 0)), pl.BlockSpec((1, LANES), lambda i: (0, 0))],
        out_shape=[jax.ShapeDtypeStruct((t, LANES), F32), jax.ShapeDtypeStruct((1, LANES), F32)],
        compiler_params=pltpu.CompilerParams(dimension_semantics=("arbitrary",)), name="fgate_bwd")(z, bf, dc)


def _cast_rows(src_ref, dst_ref, t, rows, fn):
    def cp(c, carry):
        r0 = pl.multiple_of(c * rows, rows)
        dst_ref[pl.ds(r0, rows), :] = fn(src_ref[pl.ds(r0, rows), :]).astype(dst_ref.dtype)
        return carry

    lax.fori_loop(0, t // rows, cp, 0)


def _attn_groups(t):
    tq = _tile(t, 256)
    nq = t // tq
    grp = 4 if nq % 4 == 0 else 1
    return tq, nq, grp


def _attn_fwd(z, crow, carry=None):
    t = z.shape[0]
    tq, nq, grp = _attn_groups(t)
    tk = grp * tq
    scale = HEAD_DIM ** -0.5

    def body(q_ref, k_ref, v_ref, cr_ref, o_ref, lse_ref, kb_s, vb_s):
        lane = lax.broadcasted_iota(jnp.int32, (1, LANES), 1)
        hmask = [(lane // HEAD_DIM) == hh for hh in range(2)]
        _cast_rows(k_ref, kb_s, t, tq, lambda v: v)
        _cast_rows(v_ref, vb_s, t, tq, lambda v: v)

        def qblock(g, r):
            q0 = pl.multiple_of((g * grp + r) * tq, tq)
            qv = q_ref[pl.ds(q0, tq), :] * scale
            qa = [jnp.where(hmask[hh], qv, 0.0).astype(BF16) for hh in range(2)]

            def update(st, k0, width, off):
                kb = kb_s[pl.ds(k0, width), :]
                vb = vb_s[pl.ds(k0, width), :]
                new = []
                for hh in range(2):
                    m, l, acc = st[hh]
                    s = _dot(qa[hh], kb, NT) - cr_ref[hh, :, pl.ds(k0, width)]
                    if off is not None:
                        keep = (lax.broadcasted_iota(jnp.int32, (tq, width), 0) + off
                                >= lax.broadcasted_iota(jnp.int32, (tq, width), 1))
                        s = jnp.where(keep, s, NEG)
                    m_new = jnp.maximum(m, jnp.max(s, axis=-1, keepdims=True))
                    p = jnp.exp(s - m_new)
                    corr = jnp.exp(m - m_new)
                    new.append((m_new, corr * l + jnp.sum(p, axis=-1, keepdims=True), corr * acc + _dot(p, vb, NN)))
                return tuple(new)

            one = (jnp.full((tq, 1), NEG, F32), jnp.zeros((tq, 1), F32), jnp.zeros((tq, LANES), F32))
            st = lax.fori_loop(0, g, lambda j, st: update(st, pl.multiple_of(j * tk, tk), tk, None), (one, one))
            st = update(st, pl.multiple_of(g * tk, tk), (r + 1) * tq, r * tq)
            o_ref[pl.ds(q0, tq), :] = jnp.where(hmask[0], st[0][2] / st[0][1], st[1][2] / st[1][1])
            for hh in range(2):
                lse_ref[hh, pl.ds(q0, tq), :] = st[hh][0] + jnp.log(st[hh][1])

        def group(g, carry):
            for r in range(grp):
                qblock(g, r)
            return carry

        lax.fori_loop(0, nq // grp, group, 0)

    base = 2 * D_A // LANES
    nh = D_B // LANES
    col = lambda off: pl.BlockSpec((t, LANES), lambda p: (0, off + p))
    return _call(
        body, (nh,), [col(base), col(base + nh), col(base + 2 * nh), pl.BlockSpec((2, 1, t), lambda p: (p, 0, 0))],
        [col(0), pl.BlockSpec((2, t, 1), lambda p: (p, 0, 0))],
        [jax.ShapeDtypeStruct((t, D_B), F32), jax.ShapeDtypeStruct((N_HEADS, t, 1), F32)],
        [pltpu.VMEM((t, LANES), BF16), pltpu.VMEM((t, LANES), BF16)], ("parallel",), "attn_fwd", [z, z, z, crow], carry)


def _attn_bwd(z, crow, lse, do, carry=None):
    t = z.shape[0]
    tq, nq, grp = _attn_groups(t)
    tw = grp * tq
    scale = HEAD_DIM ** -0.5

    def body(q_ref, k_ref, v_ref, cr_ref, lse_ref, do_ref, dq_ref, dk_ref, dv_ref, dc_ref, qa_s, da_s, kb_s, vb_s, dl_s):
        lane = lax.broadcasted_iota(jnp.int32, (1, LANES), 1)
        hmask = [(lane // HEAD_DIM) == hh for hh in range(2)]
        _cast_rows(k_ref, kb_s, t, tq, lambda v: v)
        _cast_rows(v_ref, vb_s, t, tq, lambda v: v)
        for hh in range(2):
            _cast_rows(q_ref, qa_s.at[hh], t, tq, lambda v, hh=hh: jnp.where(hmask[hh], v * scale, 0.0))
            _cast_rows(do_ref, da_s.at[hh], t, tq, lambda v, hh=hh: jnp.where(hmask[hh], v, 0.0))
        _cast_rows(q_ref, dq_ref, t, tq, lambda v: jnp.zeros_like(v))

        def probs(hh, q0, nq_rows, k0, nk_rows, off):
            s = _dot(qa_s[hh, pl.ds(q0, nq_rows), :], kb_s[pl.ds(k0, nk_rows), :], NT) - cr_ref[hh, :, pl.ds(k0, nk_rows)]
            p = jnp.exp(s - lse_ref[hh, pl.ds(q0, nq_rows), :])
            if off is not None:
                keep = (lax.broadcasted_iota(jnp.int32, (nq_rows, nk_rows), 0) + off
                        >= lax.broadcasted_iota(jnp.int32, (nq_rows, nk_rows), 1))
                p = jnp.where(keep, p, 0.0)
            return p, _dot(da_s[hh, pl.ds(q0, nq_rows), :], vb_s[pl.ds(k0, nk_rows), :], NT)

        def delta(g, r):
            q0 = pl.multiple_of((g * grp + r) * tq, tq)

            def add(k0, width, off, acc):
                res = []
                for hh in range(2):
                    p, dp = probs(hh, q0, tq, k0, width, off)
                    res.append(acc[hh] + jnp.sum(p * dp, axis=-1, keepdims=True))
                return tuple(res)

            zcol = jnp.zeros((tq, 1), F32)
            acc = lax.fori_loop(0, g, lambda j, acc: add(pl.multiple_of(j * tw, tw), tw, None, acc), (zcol, zcol))
            acc = add(pl.multiple_of(g * tw, tw), (r + 1) * tq, r * tq, acc)
            for hh in range(2):
                dl_s[hh, pl.ds(q0, tq), :] = acc[hh]

        def delta_group(g, carry):
            for r in range(grp):
                delta(g, r)
            return carry

        lax.fori_loop(0, nq // grp, delta_group, 0)

        def kblock(g, r):
            k0 = pl.multiple_of((g * grp + r) * tq, tq)
            kb = kb_s[pl.ds(k0, tq), :]

            def upd(q0, height, off, st):
                dk, dv, dc = st[0], st[1], [st[2], st[3]]
                dqs = []
                for hh in range(2):
                    p, dp = probs(hh, q0, height, k0, tq, off)
                    ds = p * (dp - dl_s[hh, pl.ds(q0, height), :])
                    dv = dv + _dot(p, da_s[hh, pl.ds(q0, height), :], TN)
                    dk = dk + _dot(ds, qa_s[hh, pl.ds(q0, height), :], TN)
                    dqs.append(_dot(ds, kb, NN))
                    dc[hh] = dc[hh] - jnp.sum(ds, axis=0, keepdims=True)
                dq_ref[pl.ds(q0, height), :] += jnp.where(hmask[0], dqs[0], dqs[1]) * scale
                return dk, dv, dc[0], dc[1]

            zero = jnp.zeros((tq, LANES), F32)
            zrow = jnp.zeros((1, tq), F32)
            st = upd(k0, (grp - r) * tq, 0, (zero, zero, zrow, zrow))
            st = lax.fori_loop(g + 1, nq // grp, lambda i, st: upd(pl.multiple_of(i * tw, tw), tw, None, st), st)
            dk_ref[pl.ds(k0, tq), :] = st[0]
            dv_ref[pl.ds(k0, tq), :] = st[1]
            for hh in range(2):
                dc_ref[hh, :, pl.ds(k0, tq)] = st[2 + hh]

        def kgroup(g, carry):
            for r in range(grp):
                kblock(g, r)
            return carry

        lax.fori_loop(0, nq // grp, kgroup, 0)

    base = 2 * D_A // LANES
    nh = D_B // LANES
    col = lambda off: pl.BlockSpec((t, LANES), lambda p: (0, off + p))
    ccs = pl.BlockSpec((2, t, 1), lambda p: (p, 0, 0))
    crs = pl.BlockSpec((2, 1, t), lambda p: (p, 0, 0))
    return _call(
        body, (nh,), [col(base), col(base + nh), col(base + 2 * nh), crs, ccs, col(0)], [col(0), col(0), col(0), crs],
        [jax.ShapeDtypeStruct((t, D_B), F32)] * 3 + [jax.ShapeDtypeStruct((N_HEADS, 1, t), F32)],
        [pltpu.VMEM((2, t, LANES), BF16), pltpu.VMEM((2, t, LANES), BF16), pltpu.VMEM((t, LANES), BF16),
         pltpu.VMEM((t, LANES), BF16), pltpu.VMEM((2, t, 1), F32)], ("parallel",), "attn_bwd", [z, z, z, crow, lse, do], carry)


def _s5_disc(a_re, a_im, log_dt, b_re, b_im):
    dt = jnp.exp(log_dt)
    mag = jnp.exp(a_re * dt)
    ar = mag * jnp.cos(a_im * dt)
    ai = mag * jnp.sin(a_im * dt)
    den = a_re * a_re + a_im * a_im
    kr = ((ar - 1.0) * a_re + ai * a_im) / den
    ki = (ai * a_re - (ar - 1.0) * a_im) / den
    kr3, ki3 = kr[:, None, :], ki[:, None, :]
    return ar, ai, kr3 * b_re - ki3 * b_im, kr3 * b_im + ki3 * b_re


def _s5_prep(a_re, a_im, log_dt, b_re, b_im):
    g, p = a_re.shape
    gc = b_re.shape[1]

    def body(*refs):
        res = _s5_disc(*[r[...] for r in refs[:5]])
        for o_ref, v in zip(refs[5:], res):
            o_ref[...] = v

    sds = jax.ShapeDtypeStruct
    return pl.pallas_call(body, out_shape=[sds((g, p), F32), sds((g, p), F32), sds((g, gc, p), F32), sds((g, gc, p), F32)],
                          name="s5_prep")(a_re, a_im, log_dt, b_re, b_im)


def _s5_prep_bwd(a_re, a_im, log_dt, b_re, b_im, d_ar, d_ai, d_br, d_bi):
    ins = (a_re, a_im, log_dt, b_re, b_im)

    def body(*refs):
        vals = [r[...] for r in refs[:5]]
        cts = tuple(r[...] for r in refs[5:9])
        _, pull = jax.vjp(_s5_disc, *vals)
        for o_ref, v in zip(refs[9:], pull(cts)):
            o_ref[...] = v

    return pl.pallas_call(body, out_shape=[jax.ShapeDtypeStruct(a.shape, F32) for a in ins], name="s5_prep_bwd")(
        *ins, d_ar, d_ai, d_br, d_bi)


def _s5_scan_rows(t, ar, ai, hr_s, hi_s, off, reverse):
    n = ar.shape[1]
    if reverse:
        ai = -ai
    sub = lax.broadcasted_iota(jnp.int32, (8, n), 0)
    cmul = lambda xr, xi, yr, yi: (xr * yr - xi * yi, xr * yi + xi * yr)
    pw = [(ar, ai)]
    for _ in range(7):
        pw.append(cmul(*pw[-1], ar, ai))
    pr = jnp.zeros((8, n), F32)
    pi = jnp.zeros((8, n), F32)
    for r in range(8):
        k = 7 - r if reverse else r
        pr = jnp.where(sub == r, pw[k][0], pr)
        pi = jnp.where(sub == r, pw[k][1], pi)

    def step(g, carry):
        cr, ci = carry
        r0 = pl.multiple_of(off + (t // 8 - 1 - g if reverse else g) * 8, 8)
        br = hr_s[pl.ds(r0, 8), :]
        bi = hi_s[pl.ds(r0, 8), :]
        for s in (1, 2, 4):
            ok = (sub < 8 - s) if reverse else (sub >= s)
            shift = 8 - s if reverse else s
            sr = jnp.where(ok, pltpu.roll(br, shift, 0), 0.0)
            si = jnp.where(ok, pltpu.roll(bi, shift, 0), 0.0)
            mr, mi = cmul(pw[s - 1][0], pw[s - 1][1], sr, si)
            br, bi = br + mr, bi + mi
        mr, mi = cmul(pr, pi, cr, ci)
        br, bi = br + mr, bi + mi
        hr_s[pl.ds(r0, 8), :] = br
        hi_s[pl.ds(r0, 8), :] = bi
        edge = sub == (0 if reverse else 7)
        return (jnp.sum(jnp.where(edge, br, 0.0), axis=0, keepdims=True),
                jnp.sum(jnp.where(edge, bi, 0.0), axis=0, keepdims=True))

    zero = jnp.zeros((1, n), F32)
    lax.fori_loop(0, t // 8, step, (zero, zero))


def _s5_fwd(z, bd_re, bd_im, ab_re, ab_im, cd_re, cd_im, dvec, carry=None):
    t = z.shape[0]
    cr = _tile(t, 256)
    ns = N_STATE // 2

    def body(u_ref, br_ref, bi_ref, ar_ref, ai_ref, cre_ref, cim_ref, d_ref, y_ref, hr_s, hi_s):
        def p1(c, carry):
            t0 = pl.multiple_of(c * cr, cr)
            u = u_ref[pl.ds(t0, cr), :]
            hr_s[pl.ds(t0, cr), :] = _dot(u, br_ref[...], NN)
            hi_s[pl.ds(t0, cr), :] = _dot(u, bi_ref[...], NN)
            return carry

        lax.fori_loop(0, t // cr, p1, 0)
        _s5_scan_rows(t, ar_ref[...], ai_ref[...], hr_s, hi_s, 0, False)

        def p3(c, carry):
            t0 = pl.multiple_of(c * cr, cr)
            y_ref[pl.ds(t0, cr), :] = (_dot(hr_s[pl.ds(t0, cr), :], cre_ref[...], NN)
                                       - _dot(hi_s[pl.ds(t0, cr), :], cim_ref[...], NN)
                                       + d_ref[...] * u_ref[pl.ds(t0, cr), :])
            return carry

        lax.fori_loop(0, t // cr, p3, 0)

    blk = lambda r, c: pl.BlockSpec((None, r, c), lambda b: (b, 0, 0))
    return _call(
        body, (2,),
        [pl.BlockSpec((t, LANES), lambda b: (0, Z_U // LANES + b)), blk(LANES, ns), blk(LANES, ns), blk(1, ns),
         blk(1, ns), blk(ns, LANES), blk(ns, LANES), pl.BlockSpec((1, LANES), lambda b: (0, b))],
        [pl.BlockSpec((t, LANES), lambda b: (0, b))], [jax.ShapeDtypeStruct((t, D_C), F32)],
        [pltpu.VMEM((t, ns), F32), pltpu.VMEM((t, ns), F32)], ("arbitrary",), "s5_fwd",
        [z, bd_re, bd_im, ab_re, ab_im, cd_re, cd_im, dvec], carry)


def _s5_bwd(z, dy, bd_re, bd_im, ab_re, ab_im, cd_re, cd_im, dvec):
    t = z.shape[0]
    cr = _tile(t, 256)
    ns = N_STATE // 2

    def body(u_ref, dy_ref, br_ref, bi_ref, ar_ref, ai_ref, cre_ref, cim_ref, d_ref,
             du_ref, dbr_ref, dbi_ref, dar_ref, dai_ref, dcre_ref, dcim_ref, dd_ref, hr_s, hi_s, gr_s, gi_s):
        zero8 = jnp.zeros((8, ns), F32)
        hr_s[pl.ds(0, 8), :] = zero8
        hi_s[pl.ds(0, 8), :] = zero8
        for ref in (dbr_ref, dbi_ref, dar_ref, dai_ref, dcre_ref, dcim_ref, dd_ref):
            ref[...] = jnp.zeros(ref.shape, F32)

        def p1(c, carry):
            t0 = pl.multiple_of(c * cr, cr)
            u = u_ref[pl.ds(t0, cr), :]
            hr_s[pl.ds(t0 + 8, cr), :] = _dot(u, br_ref[...], NN)
            hi_s[pl.ds(t0 + 8, cr), :] = _dot(u, bi_ref[...], NN)
            return carry

        lax.fori_loop(0, t // cr, p1, 0)
        _s5_scan_rows(t, ar_ref[...], ai_ref[...], hr_s, hi_s, 8, False)

        def p3(c, carry):
            t0 = pl.multiple_of(c * cr, cr)
            dyv = dy_ref[pl.ds(t0, cr), :]
            u = u_ref[pl.ds(t0, cr), :]
            gr_s[pl.ds(t0, cr), :] = _dot(dyv, cre_ref[...], NT)
            gi_s[pl.ds(t0, cr), :] = -_dot(dyv, cim_ref[...], NT)
            dcre_ref[...] += _dot(hr_s[pl.ds(t0 + 8, cr), :], dyv, TN)
            dcim_ref[...] -= _dot(hi_s[pl.ds(t0 + 8, cr), :], dyv, TN)
            dd_ref[...] += jnp.sum(dyv * u, axis=0, keepdims=True)
            du_ref[pl.ds(t0, cr), :] = dyv * d_ref[...]
            return carry

        lax.fori_loop(0, t // cr, p3, 0)
        _s5_scan_rows(t, ar_ref[...], ai_ref[...], gr_s, gi_s, 0, True)

        def p5(c, carry):
            t0 = pl.multiple_of(c * cr, cr)
            u = u_ref[pl.ds(t0, cr), :]
            gr = gr_s[pl.ds(t0, cr), :]
            gi = gi_s[pl.ds(t0, cr), :]
            dbr_ref[...] += _dot(u, gr, TN)
            dbi_ref[...] += _dot(u, gi, TN)
            du_ref[pl.ds(t0, cr), :] += _dot(gr, br_ref[...], NT) + _dot(gi, bi_ref[...], NT)
            hpr = pltpu.roll(hr_s[pl.ds(t0, cr + 8), :], 1, 0)[8:, :]
            hpi = pltpu.roll(hi_s[pl.ds(t0, cr + 8), :], 1, 0)[8:, :]
            dar_ref[...] += jnp.sum(gr * hpr + gi * hpi, axis=0, keepdims=True)
            dai_ref[...] += jnp.sum(gi * hpr - gr * hpi, axis=0, keepdims=True)
            return carry

        lax.fori_loop(0, t // cr, p5, 0)

    blk = lambda r, c: pl.BlockSpec((None, r, c), lambda b: (b, 0, 0))
    ucol = pl.BlockSpec((t, LANES), lambda b: (0, Z_U // LANES + b))
    ycol = pl.BlockSpec((t, LANES), lambda b: (0, b))
    dsp = pl.BlockSpec((1, LANES), lambda b: (0, b))
    sds = jax.ShapeDtypeStruct
    return pl.pallas_call(
        body, grid=(2,),
        in_specs=[ucol, ycol, blk(LANES, ns), blk(LANES, ns), blk(1, ns), blk(1, ns), blk(ns, LANES), blk(ns, LANES), dsp],
        out_specs=[ycol, blk(LANES, ns), blk(LANES, ns), blk(1, ns), blk(1, ns), blk(ns, LANES), blk(ns, LANES), dsp],
        out_shape=[sds((t, D_C), F32), sds((2, LANES, ns), F32), sds((2, LANES, ns), F32), sds((2, 1, ns), F32),
                   sds((2, 1, ns), F32), sds((2, ns, LANES), F32), sds((2, ns, LANES), F32), sds((1, D_C), F32)],
        scratch_shapes=[pltpu.VMEM((t + 8, ns), F32), pltpu.VMEM((t + 8, ns), F32), pltpu.VMEM((t, ns), F32),
                        pltpu.VMEM((t, ns), F32)],
        compiler_params=pltpu.CompilerParams(dimension_semantics=("arbitrary",)), name="s5_bwd")(
            z, dy, bd_re, bd_im, ab_re, ab_im, cd_re, cd_im, dvec)


def _mix_out(out_a, out_b, yc, x1, ga, gb, gc, wglu, wout, ln_g, ln_b):
    yg = _gelu(yc)
    out_c = yg * _sig(_bdot(yg, wglu))
    o = jnp.concatenate([_rms(out_a, ga), _rms(out_b, gb), _rms(out_c, gc)], axis=-1)
    return (_ln(ALPHA * x1 + _bdot(o, wout), ln_g, ln_b),)


def _ln_only(pre, g, b):
    return (_ln(pre, g, b),)


def _loss_head(y, target, tm):
    t, d = y.shape

    def body(y_ref, t_ref, dy_ref, l_ref):
        i = pl.program_id(0)
        e = y_ref[...] - t_ref[...]
        dy_ref[...] = e * (1.0 / d)
        part = 0.5 * jnp.sum(jnp.sum(e * e, axis=-1, keepdims=True) * (1.0 / d), axis=0, keepdims=True)
        row = jnp.where(lax.broadcasted_iota(jnp.int32, (1, LANES), 1) == 0, part, 0.0)

        @pl.when(i == 0)
        def _():
            l_ref[...] = row

        @pl.when(i > 0)
        def _():
            l_ref[...] += row

    spec = pl.BlockSpec((tm, d), lambda i: (i, 0))
    return pl.pallas_call(
        body, grid=(t // tm,), in_specs=[spec, spec], out_specs=[spec, pl.BlockSpec((1, LANES), lambda i: (0, 0))],
        out_shape=[jax.ShapeDtypeStruct((t, d), F32), jax.ShapeDtypeStruct((1, LANES), F32)],
        compiler_params=pltpu.CompilerParams(dimension_semantics=("arbitrary",)), name="loss_head")(y, target)


def _adamw(w, g, m, v, name):
    r, c = w.shape
    tr = r
    for cand in (512, 256, 352, 128):
        if r % cand == 0:
            tr = cand
            break

    def body(w_ref, g_ref, m_ref, v_ref, d_ref, nm_ref, nv_ref):
        gv = g_ref[...]
        mn = ADAM_B1 * m_ref[...] + (1.0 - ADAM_B1) * gv
        vn = ADAM_B2 * v_ref[...] + (1.0 - ADAM_B2) * (gv * gv)
        m_hat = mn / (1.0 - ADAM_B1 ** ADAM_STEP)
        v_hat = vn / (1.0 - ADAM_B2 ** ADAM_STEP)
        d_ref[...] = -ADAM_LR * (m_hat / (jnp.sqrt(v_hat) + ADAM_EPS) + ADAM_WD * w_ref[...])
        nm_ref[...] = mn
        nv_ref[...] = vn

    spec = pl.BlockSpec((tr, c), lambda i: (i, 0))
    return pl.pallas_call(
        body, grid=(r // tr,), in_specs=[spec] * 4, out_specs=[spec] * 3,
        out_shape=[jax.ShapeDtypeStruct((r, c), F32)] * 3,
        compiler_params=pltpu.CompilerParams(dimension_semantics=("parallel",)), name=name)(w, g, m, v)


def _row_tile(r):
    for cand in (512, 448, 352, 256, 128):
        if r % cand == 0:
            return cand
    return r


def _pair_add(a, b, idx, name, out_dtype):
    _, r, w = a.shape
    tr = _row_tile(r)

    def body(i_ref, a_ref, b_ref, o_ref):
        o_ref[...] = (a_ref[...].astype(F32) + b_ref[...].astype(F32)).astype(o_ref.dtype)

    grid_spec = pltpu.PrefetchScalarGridSpec(
        num_scalar_prefetch=1, grid=(4, r // tr),
        in_specs=[pl.BlockSpec((None, tr, w), lambda q, i, s: (2 * q + s[0], i, 0)),
                  pl.BlockSpec((None, tr, w), lambda q, i, s: (q, i, 0))],
        out_specs=pl.BlockSpec((None, tr, w), lambda q, i, s: (q, i, 0)))
    return pl.pallas_call(body, grid_spec=grid_spec, out_shape=jax.ShapeDtypeStruct((4, r, w), out_dtype),
                          compiler_params=pltpu.CompilerParams(dimension_semantics=("parallel", "parallel")), name=name)(
                              idx, a, b)


def _quad_add(p, rb, idx, name):
    _, r, w = p.shape
    tr = _row_tile(r)

    def body(i_ref, p_ref, r0, r1, r2, o_ref):
        o_ref[...] = ((p_ref[...].astype(F32) + r0[...].astype(F32)) + r1[...].astype(F32)) + r2[...].astype(F32)

    grid_spec = pltpu.PrefetchScalarGridSpec(
        num_scalar_prefetch=1, grid=(r // tr,),
        in_specs=[pl.BlockSpec((None, tr, w), lambda i, s: (s[0], i, 0))]
        + [pl.BlockSpec((None, tr, w), functools.partial(lambda i, s, k: (k, i, 0), k=k)) for k in range(3)],
        out_specs=pl.BlockSpec((tr, w), lambda i, s: (i, 0)))
    return pl.pallas_call(body, grid_spec=grid_spec, out_shape=jax.ShapeDtypeStruct((r, w), F32),
                          compiler_params=pltpu.CompilerParams(dimension_semantics=("parallel",)), name=name)(
                              idx, p, rb, rb, rb)


def _gather_small(buf):
    def plan(x, y, c, ins, outs):
        dst = outs[0].at[4 * x + 2 * y + c]
        res = [(ins[0], dst, None)]
        for rel in range(1, 8):
            res.append((ins[0], dst, (x ^ (rel >> 2), y ^ ((rel >> 1) & 1), c ^ (rel & 1))))
        return res

    return _Carry([buf], [jax.ShapeDtypeStruct((8,) + buf.shape, buf.dtype)], {}, plan, 8)


def _sum_small(allb):
    _, r, w = allb.shape

    def body(a_ref, o_ref):
        s = a_ref[0]
        for k in range(1, 8):
            s = s + a_ref[k]
        o_ref[...] = s

    return pl.pallas_call(body, out_shape=jax.ShapeDtypeStruct((r, w), F32), name="ar_small_sum")(allb)


def _pad_rows(a, rows):
    return jnp.pad(a, ((0, rows - a.shape[0]), (0, 0)))


def _pack_small(arrs):
    rows, tail = [], []
    for a in arrs:
        if not tail and a.size % LANES == 0:
            rows.append(a.reshape(-1, LANES))
        else:
            tail.append(a.reshape(-1))
    n_rows = sum(r.shape[0] for r in rows)
    n_tail = sum(int(v.size) for v in tail)
    tail_rows = -(-n_tail // LANES)
    total_rows = n_rows + tail_rows + (-(n_rows + tail_rows)) % 8
    if tail:
        tail.append(jnp.zeros((tail_rows * LANES - n_tail,), F32))
        rows.append(jnp.concatenate(tail).reshape(tail_rows, LANES))
    if total_rows > n_rows + tail_rows:
        rows.append(jnp.zeros((total_rows - n_rows - tail_rows, LANES), F32))
    return jnp.concatenate(rows, axis=0)


def _unpack_small(buf, shapes):
    out, off = [], 0
    flat = buf.reshape(-1)
    for s in shapes:
        n = int(np.prod(s))
        out.append(flat[off:off + n].reshape(s))
        off += n
    return out


def _block_diag(blocks, nb):
    m, r, c = blocks.shape
    n = m // nb
    eye = jnp.eye(n, dtype=blocks.dtype)
    return (blocks.reshape(nb, n, r, 1, c) * eye[None, :, None, :, None]).reshape(nb, n * r, n * c)


def _diag_blocks(dense, n):
    nb = dense.shape[0]
    r, c = dense.shape[1] // n, dense.shape[2] // n
    eye = jnp.eye(n, dtype=dense.dtype)
    return jnp.sum(dense.reshape(nb, n, r, n, c) * eye[None, :, None, :, None], axis=3).reshape(nb * n, r, c)


def kernel(x, ffn1_w_gate, ffn1_w_up, ffn1_w_down, ln1_g, ln1_b, w_in, conv_w, conv_b, rg_w_a, rg_b_a, rg_w_x, rg_b_x, rg_lambda, fox_b_f, s5_a_re, s5_a_im, s5_log_dt, s5_b_re, s5_b_im, s5_c_re, s5_c_im, s5_d, s5_w_glu, mix_norm_g, w_out, ln2_g, ln2_b, ffn2_w_gate, ffn2_w_up, ffn2_w_down, ln3_g, ln3_b, loss_target, m_ffn1_w_gate, m_ffn1_w_up, m_ffn1_w_down, m_ln1_g, m_ln1_b, m_w_in, m_conv_w, m_conv_b, m_rg_w_a, m_rg_b_a, m_rg_w_x, m_rg_b_x, m_rg_lambda, m_fox_b_f, m_s5_a_re, m_s5_a_im, m_s5_log_dt, m_s5_b_re, m_s5_b_im, m_s5_c_re, m_s5_c_im, m_s5_d, m_s5_w_glu, m_mix_norm_g, m_w_out, m_ln2_g, m_ln2_b, m_ffn2_w_gate, m_ffn2_w_up, m_ffn2_w_down, m_ln3_g, m_ln3_b, v_ffn1_w_gate, v_ffn1_w_up, v_ffn1_w_down, v_ln1_g, v_ln1_b, v_w_in, v_conv_w, v_conv_b, v_rg_w_a, v_rg_b_a, v_rg_w_x, v_rg_b_x, v_rg_lambda, v_fox_b_f, v_s5_a_re, v_s5_a_im, v_s5_log_dt, v_s5_b_re, v_s5_b_im, v_s5_c_re, v_s5_c_im, v_s5_d, v_s5_w_glu, v_mix_norm_g, v_w_out, v_ln2_g, v_ln2_b, v_ffn2_w_gate, v_ffn2_w_up, v_ffn2_w_down, v_ln3_g, v_ln3_b):
    a = dict(locals())
    w = {n: a[n] for n in WEIGHTS}
    t, d = x.shape[1], x.shape[2]
    f = ffn1_w_down.shape[1] * 8
    fs, ds = f // 8, d // 8
    mx, my, mc = lax.axis_index("x"), lax.axis_index("y"), lax.axis_index("c")
    me = 4 * mx + 2 * my + mc
    tm = _tile(t, 512)
    tf = f // 2
    win_rows = ds * Z_W // d

    FFN1, MIXW, FFN2 = ['g1', 'u1', 'd1'], ['win', 'wout', 'glu', 'conv'], ['g2', 'u2', 'd2']
    glu_rows = D_C * D_C // (8 * d)

    def shard_segs(l):
        wi = w['w_in'][l]
        win_p = jnp.concatenate([wi[:, :Z_F + N_HEADS], jnp.zeros((ds, Z_U - Z_F - N_HEADS), F32), wi[:, Z_F + N_HEADS:]], axis=1)
        conv_bits = lax.bitcast_convert_type(w['conv_w'][l], BF16).reshape(1, -1)
        segs = dict(g1=w['ffn1_w_gate'][l].T, u1=w['ffn1_w_up'][l].T, d1=w['ffn1_w_down'][l],
                    g2=w['ffn2_w_gate'][l].T, u2=w['ffn2_w_up'][l].T, d2=w['ffn2_w_down'][l],
                    win=win_p.reshape(win_rows, d), wout=w['w_out'][l], glu=_pad_rows(w['s5_w_glu'][l].reshape(-1, d), 16))
        segs = {k: v.astype(BF16) for k, v in segs.items()}
        segs['conv'] = _pad_rows(jnp.pad(conv_bits, ((0, 0), (0, d - conv_bits.shape[1]))), 16)
        return segs

    shards = [shard_segs(l) for l in range(DEPTH)]
    wts = {}

    def cat(keys):
        return [shards[l][k] for l, k in keys]

    def split(gs, keys):
        for (l, k), g in zip(keys, gs):
            wts[(l, k)] = g

    grp_a = [(0, k) for k in FFN1]
    grp_b = [(0, k) for k in MIXW]
    grp_c = [(0, k) for k in FFN2]
    grp_d = [(1, k) for k in FFN1]
    grp_e = [(1, k) for k in FFN2]
    grp_f = [(1, k) for k in MIXW]
    split(_run(_ag_sibling(_run(_ag_chips(cat(grp_a)), "ag_chips")), "ag_sibling"), grp_a)

    xs = x[0]
    saved = []
    cur = xs
    for l in range(DEPTH):
        row = lambda n: w[n][l].reshape(1, -1)
        ffn = lambda keys: tuple(wts[(l, k)].reshape(f, d) for k in keys)
        wa = _block_diag(w['rg_w_a'][l], 3)
        wx = _block_diag(w['rg_w_x'][l], 3)
        bf = jnp.pad(row('fox_b_f'), ((0, 0), (0, LANES - N_HEADS)))
        s5p = (w['s5_a_re'][l], w['s5_a_im'][l], w['s5_log_dt'][l].reshape(-1, 1),
               w['s5_b_re'][l].transpose(0, 2, 1), w['s5_b_im'][l].transpose(0, 2, 1))
        ab_re, ab_im, bb_re, bb_im = _s5_prep(*s5p)
        bd_re, bd_im = _block_diag(bb_re, 2), _block_diag(bb_im, 2)
        cd_re = _block_diag(w['s5_c_re'][l].transpose(0, 2, 1), 2)
        cd_im = _block_diag(w['s5_c_im'][l].transpose(0, 2, 1), 2)
        abr, abi = ab_re.reshape(2, 1, N_STATE // 2), ab_im.reshape(2, 1, N_STATE // 2)
        gm = row('mix_norm_g')
        ga, gb, gc = gm[:, :D_A], gm[:, D_A:D_A + D_B], gm[:, D_A + D_B:]

        x0 = cur
        ffn1 = ffn(FFN1)
        (x1, pre1, gs1, us1), cres = _ffn_fwd(x0, *ffn1, row('ln1_g'), row('ln1_b'), tm, tf,
                                              carry=_ag_chips(cat(grp_b if l == 0 else grp_e)))
        if l == 0:
            split(_run(_ag_sibling(cres), "ag_sibling"), grp_b)
        else:
            g_e = cres
        win = wts[(l, 'win')].reshape(d, Z_W)
        wout = wts[(l, 'wout')].reshape(d, d).astype(F32)
        wglu = wts[(l, 'glu')][:, :glu_rows].reshape(D_C, D_C).astype(F32)
        conv_full = lax.bitcast_convert_type(
            wts[(l, 'conv')][:, 0, :2 * CONV_WIDTH * D_A // 8].reshape(8, CONV_WIDTH, D_A // 8, 2), F32)
        conv_full = conv_full.transpose(1, 0, 2).reshape(CONV_WIDTH, D_A)
        z = _mm(x1, win, 'nn', F32, tm, Z_W, d, "mix_in")
        out_a, h_a = _rg_fwd(z, conv_full, row('conv_b'), wa, row('rg_b_a'), wx, row('rg_b_x'), row('rg_lambda'))
        cs = _fgate_fwd(z, bf)
        crow = cs[:, :N_HEADS].T.reshape(N_HEADS, 1, t)
        (out_b, lse), cres = _attn_fwd(z, crow, carry=_ag_chips(cat(grp_c)) if l == 0 else _ag_sibling(g_e))
        if l == 0:
            (yc,), cres = _s5_fwd(z, bd_re, bd_im, abr, abi, cd_re, cd_im, row('s5_d'),
                                  carry=_join(_ag_sibling(cres), _ag_chips(cat(grp_f))))
            split(cres[:len(grp_c)], grp_c)
            g_f = cres[len(grp_c):]
        else:
            split(cres, grp_e)
            (yc,), _ = _s5_fwd(z, bd_re, bd_im, abr, abi, cd_re, cd_im, row('s5_d'))
        mix_params = [ga, gb, gc, wglu, wout, row('ln2_g'), row('ln2_b')]
        (x2,) = _rowwise(_mix_out, [out_a, out_b, yc, x1], mix_params, [(d, F32)], _tile(t, 256), "mix_out")
        ffn2 = ffn(FFN2)
        (x3, pre3, gs2, us2), cres = _ffn_fwd(x2, *ffn2, row('ln3_g'), row('ln3_b'), tm, tf,
                                              carry=_join(_ag_chips(cat(grp_d)), _ag_sibling(g_f)) if l == 0 else None)
        if l == 0:
            split(cres[len(grp_d):], grp_f)
            split(_run(_ag_sibling(cres[:len(grp_d)]), "ag_sibling"), grp_d)
        saved.append(dict(x0=x0, x1=x1, pre1=pre1, gs1=gs1, us1=us1, z=z, out_a=out_a, h_a=h_a, crow=crow,
                          out_b=out_b, lse=lse, yc=yc, x2=x2, pre3=pre3, gs2=gs2, us2=us2, mix_params=mix_params,
                          ffn1=ffn1, ffn2=ffn2, win=win, conv_full=conv_full, wa=wa, wx=wx, bf=bf, s5p=s5p,
                          s5m=(bd_re, bd_im, abr, abi, cd_re, cd_im)))
        cur = x3

    dy, loss_row = _loss_head(cur, loss_target[0], tm)
    loss = lax.psum(loss_row[0, 0], ("x", "y", "c"))

    assert DEPTH == 2
    small_grads = {}
    small_names = ['conv_w'] + SMALL
    c_idx = jnp.reshape(mc, (1,)).astype(jnp.int32)
    chip_idx = jnp.reshape(2 * mx + my, (1,)).astype(jnp.int32)

    def blocks(arrs):
        return jnp.concatenate([v.astype(BF16).reshape(8, -1, d) for v in arrs], axis=1)

    def mixer_blocks(dwin, dwout, dwglu):
        glu = jnp.pad(dwglu.astype(BF16).reshape(8, glu_rows, d), ((0, 0), (0, 32 - glu_rows), (0, 0)))
        return jnp.concatenate([dwin.reshape(8, win_rows, d), dwout.astype(BF16).reshape(8, ds, d), glu], axis=1)

    def pair(full, ra):
        return _pair_add(full, ra, c_idx, "rs_add_sibling", BF16)

    for l in reversed(range(DEPTH)):
        s = saved[l]
        row = lambda n: w[n][l].reshape(1, -1)
        wg_tiles = dict(tm=tf, tn=d, tk=_tile(t, 1024))
        first = l == 0

        def ffn_back(dyv, pre, gs, us, wts3, xin, ln_g, ln_b, carry_fn=None, pipeline=False):
            (dpre,), (dlg, dlb), _ = _rowwise_vjp(_ln_only, [pre], [ln_g, ln_b], [dyv], tm, "ln_bwd")
            (dx, dg, du, hh), cres = _ffn_bwd(dpre, gs, us, *wts3, tm, tf, carry=carry_fn(dlg, dlb) if carry_fn else None)
            if not pipeline:
                dwg = _mm(dg, xin, 'tn', BF16, name="ffn_dw_gate", **wg_tiles)
                dwu = _mm(du, xin, 'tn', BF16, name="ffn_dw_up", **wg_tiles)
                dwd = _mm(hh, dpre, 'tn', BF16, name="ffn_dw_down", **wg_tiles)
                return dx, (dwg, dwu, dwd), dlg, dlb, cres

            def front(dw):
                full = dw.reshape(8, fs, d)
                (ra,) = _run(_rs_sibling(full), "rs_sibling")
                return pair(full, ra)

            p_g = front(_mm(dg, xin, 'tn', BF16, name="ffn_dw_gate", **wg_tiles))
            dwu, (rb_g,) = _mm(du, xin, 'tn', BF16, name="ffn_dw_up", carry=_rs_chips(p_g), **wg_tiles)
            p_u = front(dwu)
            dwd, (rb_u,) = _mm(hh, dpre, 'tn', BF16, name="ffn_dw_down", carry=_rs_chips(p_u), **wg_tiles)
            p_d = front(dwd)
            (rb_d,) = _run(_rs_chips(p_d), "rs_chips")
            return dx, ((p_g, rb_g), (p_u, rb_u), (p_d, rb_d)), dlg, dlb, cres

        dx2, (dwg2, dwu2, dwd2), dl3g, dl3b, cres = ffn_back(
            dy, s['pre3'], s['gs2'], s['us2'], s['ffn2'], s['x2'], row('ln3_g'), row('ln3_b'),
            (lambda *_: _rs_sibling(full_l1)) if first else None)
        if first:
            part_l1 = pair(full_l1, cres[0])
            full_c2 = blocks([dwg2, dwu2, dwd2])
        (d_oa, d_ob, d_yc, d_x1), (dga, dgb, dgc, dwglu, dwout, dl2g, dl2b), cres = _rowwise_vjp(
            _mix_out, [s['out_a'], s['out_b'], s['yc'], s['x1']], s['mix_params'], [dx2], _tile(t, 256), "mix_out_bwd",
            carry=_rs_sibling(full_c2) if first else None)
        if first:
            part_c2 = pair(full_c2, cres[0])
        (d_ax, d_ag, dcw, dcb, dwa, dba, dwx, dbx, dlam) = _rg_bwd(
            s['z'], s['h_a'], d_oa, s['conv_full'], row('conv_b'), s['wa'], row('rg_b_a'), s['wx'], row('rg_b_x'), row('rg_lambda'))
        (dq, dk, dv, dcrow), cres = _attn_bwd(s['z'], s['crow'], s['lse'], d_ob,
                                              carry=_join(_rs_chips(part_l1), _rs_chips(part_c2)) if first else None)
        if first:
            rb_l1, rb_c2 = cres
        dc_pad = jnp.pad(dcrow.reshape(N_HEADS, t).T, ((0, 0), (0, LANES - N_HEADS)))
        dzf, dbf = _fgate_bwd(s['z'], s['bf'], dc_pad)
        du_c, dbd_re, dbd_im, dabr, dabi, dcd_re, dcd_im, dd = _s5_bwd(s['z'], d_yc, *s['s5m'], row('s5_d'))
        dz = jnp.concatenate([d_ax, d_ag, dq, dk, dv, dzf, du_c], axis=1)
        dx1 = _mm(dz, s['win'], 'nt', F32, tm, d, Z_W, "mix_in_dx", add=d_x1)
        dwin = _mm(s['x1'], dz, 'tn', BF16, d, Z_W, tm, "mix_in_dw")
        dbb_re, dbb_im = _diag_blocks(dbd_re, N_GROUPS // 2), _diag_blocks(dbd_im, N_GROUPS // 2)
        dcm_re, dcm_im = _diag_blocks(dcd_re, N_GROUPS // 2), _diag_blocks(dcd_im, N_GROUPS // 2)
        da_re, da_im, dlog_dt, db_re, db_im = _s5_prep_bwd(*s['s5p'], dabr.reshape(N_GROUPS, C_STATE), dabi.reshape(N_GROUPS, C_STATE), dbb_re, dbb_im)
        sg = dict(conv_w=dcw, conv_b=dcb, rg_w_a=_diag_blocks(dwa, 2), rg_b_a=dba,
                  rg_w_x=_diag_blocks(dwx, 2), rg_b_x=dbx, rg_lambda=dlam, fox_b_f=dbf[:, :N_HEADS],
                  s5_a_re=da_re, s5_a_im=da_im, s5_log_dt=dlog_dt, s5_b_re=db_re.transpose(0, 2, 1), s5_b_im=db_im.transpose(0, 2, 1),
                  s5_c_re=dcm_re.transpose(0, 2, 1), s5_c_im=dcm_im.transpose(0, 2, 1), s5_d=dd,
                  mix_norm_g=jnp.concatenate([dga, dgb, dgc], axis=1), ln2_g=dl2g, ln2_b=dl2b, ln3_g=dl3g, ln3_b=dl3b)
        small_grads[l] = sg

        if first:
            full_m = mixer_blocks(dwin, dwout, dwglu)
            (ra_m,) = _run(_rs_sibling(full_m), "rs_sibling")
            part_m = pair(full_m, ra_m)

            def last_carry(dlg, dlb):
                sg.update(ln1_g=dlg, ln1_b=dlb)
                packed = _pack_small([small_grads[ll][n] for n in small_names for ll in range(DEPTH)])
                return _join(_rs_chips(part_m), _gather_small(packed))

            dx0, tail, _, _, (rb_m, all_small) = ffn_back(dx1, s['pre1'], s['gs1'], s['us1'], s['ffn1'], s['x0'], row('ln1_g'),
                                                          row('ln1_b'), last_carry, pipeline=True)
        else:
            dx0, (dwg1, dwu1, dwd1), dl1g, dl1b, _ = ffn_back(dx1, s['pre1'], s['gs1'], s['us1'], s['ffn1'], s['x0'],
                                                              row('ln1_g'), row('ln1_b'))
            sg.update(ln1_g=dl1g, ln1_b=dl1b)
            full_l1 = jnp.concatenate([blocks([dwg1, dwu1, dwd1, dwg2, dwu2, dwd2]), mixer_blocks(dwin, dwout, dwglu)], axis=1)
        dy = dx0

    grad_x = dy.reshape(x.shape)
    quad = lambda part, rb: _quad_add(part, rb, chip_idx, "rs_add_chips")
    own = {}

    def take(rows_f32, keys, l):
        off = 0
        for k, r in keys:
            own[(l, k)] = rows_f32[off:off + r]
            off += r

    ffn_keys = lambda names: [(k, fs) for k in names]
    mix_keys = [('win', win_rows), ('wout', ds), ('glu', glu_rows)]
    take(quad(part_l1, rb_l1), ffn_keys(FFN1 + FFN2) + mix_keys, 1)
    take(quad(part_c2, rb_c2), ffn_keys(FFN2), 0)
    take(quad(part_m, rb_m), mix_keys, 0)
    for k, (part, rb) in zip(FFN1, tail):
        own[(0, k)] = quad(part, rb)

    grads = {}
    grads_t = {}
    for k, n in zip(FFN1 + FFN2, ['ffn1_w_gate', 'ffn1_w_up', 'ffn1_w_down', 'ffn2_w_gate', 'ffn2_w_up', 'ffn2_w_down']):
        stacked = jnp.stack([own[(l, k)] for l in range(DEPTH)])
        if 'down' in n:
            grads[n] = stacked
        else:
            grads_t[n] = stacked
            grads[n] = jnp.swapaxes(stacked, 1, 2)
    gwin = jnp.stack([own[(l, 'win')].reshape(ds, Z_W) for l in range(DEPTH)])
    grads['w_in'] = jnp.concatenate([gwin[:, :, :Z_F + N_HEADS], gwin[:, :, Z_U:]], axis=2)
    grads['w_out'] = jnp.stack([own[(l, 'wout')] for l in range(DEPTH)])
    grads['s5_w_glu'] = jnp.stack([own[(l, 'glu')].reshape(D_C // 8, D_C) for l in range(DEPTH)])

    small_shapes = [((DEPTH, CONV_WIDTH, D_A) if n == 'conv_w' else w[n].shape) for n in small_names]
    conv_zero = jnp.zeros((DEPTH, CONV_WIDTH, D_A), F32)
    summed = _sum_small(all_small)
    for n, g in zip(small_names, _unpack_small(summed, small_shapes)):
        grads[n] = g
    grads['conv_w'] = lax.dynamic_slice_in_dim(grads['conv_w'], me * (D_A // 8), D_A // 8, axis=2)

    delta, new_m, new_v = {}, {}, {}
    for n in BIG + ['conv_w']:
        if n in grads_t:
            sh = grads_t[n].shape
            two = lambda v: jnp.swapaxes(v, 1, 2).reshape(-1, sh[-1])
            back = lambda v: jnp.swapaxes(v.reshape(sh), 1, 2)
            g2 = grads_t[n].reshape(-1, sh[-1])
        else:
            sh = w[n].shape
            two = lambda v: v.reshape(-1, sh[-1])
            back = lambda v: v.reshape(sh)
            g2 = two(grads[n])
        dl, nm, nv = _adamw(two(w[n]), g2, two(a['m_' + n]), two(a['v_' + n]), "adamw_" + n)
        delta[n], new_m[n], new_v[n] = back(dl), back(nm), back(nv)
    dl, nm, nv = _adamw(_pack_small([conv_zero] + [w[n] for n in SMALL]), summed,
                        _pack_small([conv_zero] + [a['m_' + n] for n in SMALL]),
                        _pack_small([conv_zero] + [a['v_' + n] for n in SMALL]), "adamw_small")
    for n, v1, v2, v3 in zip(small_names[1:], _unpack_small(dl, small_shapes)[1:], _unpack_small(nm, small_shapes)[1:],
                             _unpack_small(nv, small_shapes)[1:]):
        delta[n], new_m[n], new_v[n] = v1, v2, v3

    return (loss, grad_x, *[grads[n] for n in WEIGHTS], *[delta[n] for n in WEIGHTS], *[new_m[n] for n in WEIGHTS],
            *[new_v[n] for n in WEIGHTS])
```

```python
import functools
import math

import jax
import jax.numpy as jnp
import numpy as np
from jax import lax
from jax.experimental import pallas as pl
from jax.experimental.pallas import tpu as pltpu

F32 = jnp.float32
BF16 = jnp.bfloat16
MESH = pl.DeviceIdType.MESH

DEPTH = 2
ALPHA = (2 * DEPTH) ** 0.25
LN_EPS = 1e-5
RMS_EPS = 1e-6
RG_C = 8.0
CONV_WIDTH = 4
HEAD_DIM = 64
C_GROUP = 16
C_STATE = 64
D_A = 384
D_B = 384
D_C = 256
N_HEADS = D_B // HEAD_DIM
N_GROUPS = D_C // C_GROUP
N_STATE = N_GROUPS * C_STATE
Z_F = 2 * D_A + 3 * D_B
Z_U = Z_F + 128
Z_W = Z_U + D_C
N_IN = Z_F + N_HEADS + D_C
ADAM_LR, ADAM_B1, ADAM_B2, ADAM_EPS, ADAM_WD, ADAM_STEP = 0.001, 0.9, 0.999, 1e-08, 0.01, 10
LANES = 128
NEG = -1e30

WEIGHTS = ['ffn1_w_gate', 'ffn1_w_up', 'ffn1_w_down', 'ln1_g', 'ln1_b', 'w_in', 'conv_w', 'conv_b', 'rg_w_a', 'rg_b_a',
           'rg_w_x', 'rg_b_x', 'rg_lambda', 'fox_b_f', 's5_a_re', 's5_a_im', 's5_log_dt', 's5_b_re', 's5_b_im', 's5_c_re',
           's5_c_im', 's5_d', 's5_w_glu', 'mix_norm_g', 'w_out', 'ln2_g', 'ln2_b', 'ffn2_w_gate', 'ffn2_w_up', 'ffn2_w_down',
           'ln3_g', 'ln3_b']
BIG = ['ffn1_w_gate', 'ffn1_w_up', 'ffn1_w_down', 'w_in', 's5_w_glu', 'w_out', 'ffn2_w_gate', 'ffn2_w_up', 'ffn2_w_down']
SMALL_TAIL = ['fox_b_f', 's5_log_dt']
SMALL = [n for n in WEIGHTS if n not in BIG and n != 'conv_w' and n not in SMALL_TAIL] + SMALL_TAIL


def _sig(x):
    return 1.0 / (1.0 + jnp.exp(-x))


def _gelu(x):
    return 0.5 * x * (1.0 + jnp.tanh(math.sqrt(2.0 / math.pi) * (x + 0.044715 * (x * x * x))))


def _softplus(x):
    return jnp.maximum(x, 0.0) + jnp.log(1.0 + jnp.exp(jnp.minimum(x, -x)))


def _dot(a, b, dims):
    return lax.dot_general(a.astype(BF16), b.astype(BF16), (dims, ((), ())), preferred_element_type=F32)


NN = ((1,), (0,))
NT = ((1,), (1,))
TN = ((0,), (0,))


@jax.custom_vjp
def _bdot(a, w):
    return _dot(a, w, NN)


def _bdot_fwd(a, w):
    return _dot(a, w, NN), (a, w)


def _bdot_bwd(res, ct):
    a, w = res
    return _dot(ct, w, NT), _dot(a, ct, TN)


_bdot.defvjp(_bdot_fwd, _bdot_bwd)


def _ln(pre, g, b):
    mu = jnp.mean(pre, axis=-1, keepdims=True)
    xc = pre - mu
    var = jnp.mean(xc * xc, axis=-1, keepdims=True)
    return xc * lax.rsqrt(var + LN_EPS) * g + b


def _rms(x, g):
    return x * lax.rsqrt(jnp.mean(x * x, axis=-1, keepdims=True) + RMS_EPS) * g


def _tile(n, want):
    return want if n % want == 0 else n


class _Carry:
    def __init__(self, ins, outs, aliases, plan, n):
        self.ins, self.outs, self.aliases, self.plan, self.n = list(ins), list(outs), dict(aliases), plan, n


def _join(a, b):
    na, ma = len(a.ins), len(a.outs)

    def plan(x, y, c, ins, outs):
        return a.plan(x, y, c, ins[:na], outs[:ma]) + b.plan(x, y, c, ins[na:], outs[ma:])

    aliases = dict(a.aliases)
    aliases.update({na + i: ma + j for i, j in b.aliases.items()})
    return _Carry(a.ins + b.ins, a.outs + b.outs, aliases, plan, a.n + b.n)


def _copies(carry, cins, couts, send, recv):
    x, y, c = lax.axis_index("x"), lax.axis_index("y"), lax.axis_index("c")
    res = []
    for k, (s, d, peer) in enumerate(carry.plan(x, y, c, cins, couts)):
        if peer is None:
            res.append(pltpu.make_async_copy(s, d, send.at[k]))
        else:
            res.append(pltpu.make_async_remote_copy(src_ref=s, dst_ref=d, send_sem=send.at[k], recv_sem=recv.at[k],
                                                    device_id=peer, device_id_type=MESH))
    return res


def _call(body, grid, in_specs, out_specs, out_shape, scratch, semantics, name, args, carry=None):
    n_in, n_out, n_scr = len(in_specs), len(out_specs), len(scratch)
    if carry is None:
        res = pl.pallas_call(body, grid=grid, in_specs=in_specs, out_specs=out_specs, out_shape=out_shape,
                             scratch_shapes=scratch, compiler_params=pltpu.CompilerParams(dimension_semantics=semantics),
                             name=name)(*args)
        return list(res), []
    nci, nco = len(carry.ins), len(carry.outs)

    def wrapped(*refs):
        o0 = n_in + nci
        s0 = o0 + n_out + nco
        cins, couts = refs[n_in:o0], refs[o0 + n_out:s0]
        send, recv = refs[s0 + n_scr:]
        first = functools.reduce(jnp.logical_and, [pl.program_id(k) == 0 for k in range(len(grid))])
        last = functools.reduce(jnp.logical_and, [pl.program_id(k) == grid[k] - 1 for k in range(len(grid))])

        @pl.when(first)
        def _():
            for cp in _copies(carry, cins, couts, send, recv):
                cp.start()

        body(*refs[:n_in], *refs[o0:o0 + n_out], *refs[s0:s0 + n_scr])

        @pl.when(last)
        def _():
            for cp in _copies(carry, cins, couts, send, recv):
                cp.wait()

    hbm = pl.BlockSpec(memory_space=pl.ANY)
    res = pl.pallas_call(
        wrapped, grid=grid, in_specs=list(in_specs) + [hbm] * nci, out_specs=list(out_specs) + [hbm] * nco,
        out_shape=list(out_shape) + carry.outs, scratch_shapes=list(scratch) + [pltpu.SemaphoreType.DMA((carry.n,))] * 2,
        input_output_aliases={n_in + i: n_out + j for i, j in carry.aliases.items()},
        compiler_params=pltpu.CompilerParams(dimension_semantics=("arbitrary",) * len(grid), has_side_effects=True),
        name=name)(*args, *carry.ins)
    return list(res[:n_out]), list(res[n_out:])


def _run(carry, name):
    nci, nco = len(carry.ins), len(carry.outs)

    def body(*refs):
        cps = _copies(carry, refs[:nci], refs[nci:nci + nco], refs[-2], refs[-1])
        for cp in cps:
            cp.start()
        for cp in cps:
            cp.wait()

    hbm = pl.BlockSpec(memory_space=pl.ANY)
    return pl.pallas_call(
        body, in_specs=[hbm] * nci, out_specs=[hbm] * nco, out_shape=carry.outs, input_output_aliases=carry.aliases,
        scratch_shapes=[pltpu.SemaphoreType.DMA((carry.n,))] * 2, compiler_params=pltpu.CompilerParams(has_side_effects=True),
        name=name)(*carry.ins)


def _ag_chips(shards):
    def plan(x, y, c, ins, outs):
        res = []
        for src, out in zip(ins, outs):
            dst = out.at[4 * x + 2 * y + c]
            res += [(src, dst, None)] + [(src, dst, (px, py, c)) for px, py in ((1 - x, y), (x, 1 - y), (1 - x, 1 - y))]
        return res

    return _Carry(shards, [jax.ShapeDtypeStruct((8,) + s.shape, s.dtype) for s in shards], {}, plan, 4 * len(shards))


def _ag_sibling(gs):
    def plan(x, y, c, ins, outs):
        return [(out.at[2 * q + c], out.at[2 * q + c], (x, y, 1 - c)) for out in outs for q in range(4)]

    return _Carry(gs, [jax.ShapeDtypeStruct(g.shape, g.dtype) for g in gs], {i: i for i in range(len(gs))}, plan, 4 * len(gs))


def _rs_sibling(full):
    def plan(x, y, c, ins, outs):
        return [(ins[0].at[2 * q + (1 - c)], outs[0].at[q], (x, y, 1 - c)) for q in range(4)]

    return _Carry([full], [jax.ShapeDtypeStruct((4,) + full.shape[1:], full.dtype)], {}, plan, 4)


def _rs_chips(part):
    def plan(x, y, c, ins, outs):
        res = []
        for k, (dx, dy) in enumerate(((1, 0), (0, 1), (1, 1))):
            tx, ty = x ^ dx, y ^ dy
            res.append((ins[0].at[2 * tx + ty], outs[0].at[k], (tx, ty, c)))
        return res

    return _Carry([part], [jax.ShapeDtypeStruct((3,) + part.shape[1:], part.dtype)], {}, plan, 3)


def _mm(a, b, dims, out_dtype, tm, tn, tk, name, add=None, carry=None):
    if dims == 'nn':
        (m, k), n = a.shape, b.shape[1]
        a_spec = pl.BlockSpec((tm, tk), lambda i, j, q: (i, q))
        b_spec = pl.BlockSpec((tk, tn), lambda i, j, q: (q, j))
        dn = NN
    elif dims == 'nt':
        (m, k), n = a.shape, b.shape[0]
        a_spec = pl.BlockSpec((tm, tk), lambda i, j, q: (i, q))
        b_spec = pl.BlockSpec((tn, tk), lambda i, j, q: (j, q))
        dn = NT
    else:
        (k, m), n = a.shape, b.shape[1]
        a_spec = pl.BlockSpec((tk, tm), lambda i, j, q: (q, i))
        b_spec = pl.BlockSpec((tk, tn), lambda i, j, q: (q, j))
        dn = TN
    nk = k // tk
    o_spec = pl.BlockSpec((tm, tn), lambda i, j, q: (i, j))

    def body(*refs):
        if add is None:
            a_ref, b_ref, o_ref, acc_ref = refs
        else:
            a_ref, b_ref, add_ref, o_ref, acc_ref = refs
        q = pl.program_id(2)
        part = _dot(a_ref[...], b_ref[...], dn)

        @pl.when(q == 0)
        def _():
            acc_ref[...] = part

        @pl.when(q > 0)
        def _():
            acc_ref[...] += part

        @pl.when(q == nk - 1)
        def _():
            r = acc_ref[...]
            if add is not None:
                r = r + add_ref[...]
            o_ref[...] = r.astype(o_ref.dtype)

    ins = [a, b] + ([] if add is None else [add])
    specs = [a_spec, b_spec] + ([] if add is None else [o_spec])
    (res,), cres = _call(body, (m // tm, n // tn, nk), specs, [o_spec], [jax.ShapeDtypeStruct((m, n), out_dtype)],
                         [pltpu.VMEM((tm, tn), F32)], ("parallel", "parallel", "arbitrary"), name, ins, carry)
    return res if carry is None else (res, cres)


def _rowwise(fn, rows, params, outs, tm, name):
    t = rows[0].shape[0]
    nr, npar = len(rows), len(params)

    def body(*refs):
        r = [x[...] for x in refs[:nr]]
        p = [x[...] for x in refs[nr:nr + npar]]
        res = fn(*r, *p)
        for o_ref, o in zip(refs[nr + npar:], res):
            o_ref[...] = o.astype(o_ref.dtype)

    in_specs = ([pl.BlockSpec((tm, a.shape[1]), lambda i: (i, 0)) for a in rows]
                + [pl.BlockSpec(p.shape, lambda i: (0, 0)) for p in params])
    return pl.pallas_call(
        body, grid=(t // tm,), in_specs=in_specs,
        out_specs=[pl.BlockSpec((tm, c), lambda i: (i, 0)) for c, _ in outs],
        out_shape=[jax.ShapeDtypeStruct((t, c), d) for c, d in outs],
        compiler_params=pltpu.CompilerParams(dimension_semantics=("parallel",)), name=name)(*rows, *params)


def _rowwise_vjp(fn, rows, params, cots, tm, name, carry=None):
    t = rows[0].shape[0]
    nr, npar, nc = len(rows), len(params), len(cots)

    def body(*refs):
        r = [x[...] for x in refs[:nr]]
        p = [x[...] for x in refs[nr:nr + npar]]
        c = [x[...] for x in refs[nr + npar:nr + npar + nc]]
        o_refs = refs[nr + npar + nc:]
        _, pull = jax.vjp(fn, *r, *p)
        grads = pull(tuple(c))
        for o_ref, g in zip(o_refs[:nr], grads[:nr]):
            o_ref[...] = g
        i = pl.program_id(0)

        @pl.when(i == 0)
        def _():
            for o_ref, g in zip(o_refs[nr:], grads[nr:]):
                o_ref[...] = g

        @pl.when(i > 0)
        def _():
            for o_ref, g in zip(o_refs[nr:], grads[nr:]):
                o_ref[...] += g

    row_spec = lambda a: pl.BlockSpec((tm, a.shape[1]), lambda i: (i, 0))
    par_spec = lambda p: pl.BlockSpec(p.shape, lambda i: (0, 0))
    res, cres = _call(
        body, (t // tm,),
        [row_spec(a) for a in rows] + [par_spec(p) for p in params] + [row_spec(a) for a in cots],
        [row_spec(a) for a in rows] + [par_spec(p) for p in params],
        [jax.ShapeDtypeStruct(a.shape, F32) for a in rows] + [jax.ShapeDtypeStruct(p.shape, F32) for p in params],
        [], ("arbitrary",), name, [*rows, *params, *cots], carry)
    return res[:nr], res[nr:], cres


def _ffn_fwd(x, wgt, wut, wd, ln_g, ln_b, tm, tf, carry=None):
    t, d = x.shape
    f = wgt.shape[0]
    nj = f // tf

    def body(x_ref, wg_ref, wu_ref, wd_ref, g_ref, b_ref, y_ref, pre_ref, gs_ref, us_ref, acc_ref):
        j = pl.program_id(1)
        xv = x_ref[...]
        xb = xv.astype(BF16)
        g = _dot(xb, wg_ref[...], NT)
        u = _dot(xb, wu_ref[...], NT)
        gs_ref[...] = g.astype(BF16)
        us_ref[...] = u.astype(BF16)
        part = _dot(g * _sig(g) * u, wd_ref[...], NN)

        @pl.when(j == 0)
        def _():
            acc_ref[...] = part

        @pl.when(j > 0)
        def _():
            acc_ref[...] += part

        @pl.when(j == nj - 1)
        def _():
            pre = ALPHA * xv + 0.5 * acc_ref[...]
            pre_ref[...] = pre
            y_ref[...] = _ln(pre, g_ref[...], b_ref[...])

    w_spec = pl.BlockSpec((tf, d), lambda i, j: (j, 0))
    x_spec = pl.BlockSpec((tm, d), lambda i, j: (i, 0))
    v_spec = pl.BlockSpec((1, d), lambda i, j: (0, 0))
    h_spec = pl.BlockSpec((tm, tf), lambda i, j: (i, j))
    return _call(
        body, (t // tm, nj), [x_spec, w_spec, w_spec, w_spec, v_spec, v_spec], [x_spec, x_spec, h_spec, h_spec],
        [jax.ShapeDtypeStruct((t, d), F32), jax.ShapeDtypeStruct((t, d), F32),
         jax.ShapeDtypeStruct((t, f), BF16), jax.ShapeDtypeStruct((t, f), BF16)],
        [pltpu.VMEM((tm, d), F32)], ("parallel", "arbitrary"), "ffn_fwd", [x, wgt, wut, wd, ln_g, ln_b], carry)


def _ffn_bwd(dpre, gs, us, wgt, wut, wd, tm, tf, carry=None):
    t, d = dpre.shape
    f = wgt.shape[0]
    nj = f // tf

    def body(dp_ref, gs_ref, us_ref, wg_ref, wu_ref, wd_ref, dx_ref, dg_ref, du_ref, hh_ref, acc_ref):
        j = pl.program_id(1)
        dp = dp_ref[...]
        dh = _dot(0.5 * dp, wd_ref[...], NT)
        g = gs_ref[...].astype(F32)
        u = us_ref[...].astype(F32)
        s = _sig(g)
        sl = g * s
        dg = (dh * u * (s * (1.0 + g * (1.0 - s)))).astype(BF16)
        du = (dh * sl).astype(BF16)
        dg_ref[...] = dg
        du_ref[...] = du
        hh_ref[...] = (0.5 * sl * u).astype(BF16)
        part = _dot(dg, wg_ref[...], NN) + _dot(du, wu_ref[...], NN)

        @pl.when(j == 0)
        def _():
            acc_ref[...] = part

        @pl.when(j > 0)
        def _():
            acc_ref[...] += part

        @pl.when(j == nj - 1)
        def _():
            dx_ref[...] = ALPHA * dp + acc_ref[...]

    w_spec = pl.BlockSpec((tf, d), lambda i, j: (j, 0))
    x_spec = pl.BlockSpec((tm, d), lambda i, j: (i, 0))
    h_spec = pl.BlockSpec((tm, tf), lambda i, j: (i, j))
    return _call(
        body, (t // tm, nj), [x_spec, h_spec, h_spec, w_spec, w_spec, w_spec], [x_spec, h_spec, h_spec, h_spec],
        [jax.ShapeDtypeStruct((t, d), F32)] + [jax.ShapeDtypeStruct((t, f), BF16)] * 3,
        [pltpu.VMEM((tm, d), F32)], ("parallel", "arbitrary"), "ffn_bwd", [dpre, gs, us, wgt, wut, wd], carry)


def _scan8(a_ref, b_ref, out_ref, t, reverse=False):
    w = out_ref.shape[-1]
    sub = lax.broadcasted_iota(jnp.int32, (8, w), 0)

    def step(g, carry):
        r0 = pl.multiple_of((t // 8 - 1 - g if reverse else g) * 8, 8)
        bv = b_ref[pl.ds(r0, 8), :]
        av = None if a_ref is None else a_ref[pl.ds(r0, 8), :]
        for s in (1, 2, 4):
            ok = (sub < 8 - s) if reverse else (sub >= s)
            shift = 8 - s if reverse else s
            b_sh = jnp.where(ok, pltpu.roll(bv, shift, 0), 0.0)
            if av is None:
                bv = bv + b_sh
            else:
                bv = av * b_sh + bv
                av = av * jnp.where(ok, pltpu.roll(av, shift, 0), 1.0)
        h = bv + carry if av is None else bv + av * carry
        out_ref[pl.ds(r0, 8), :] = h
        return jnp.sum(jnp.where(sub == (0 if reverse else 7), h, 0.0), axis=0, keepdims=True)

    lax.fori_loop(0, t // 8, step, jnp.zeros((1, w), F32))


def _rg_local(xa, wa, ba, wx, bx, lam):
    r = _sig(_bdot(xa, wa) + ba)
    i = _sig(_bdot(xa, wx) + bx)
    log_a = -RG_C * r * _softplus(-lam)
    a = jnp.exp(log_a)
    mult = jnp.sqrt(-jnp.tanh(log_a) * (a * a + 1.0))
    return a, mult * (i * xa)


def _conv_taps(ext, n):
    return [ext[8:, :]] + [pltpu.roll(ext, s, 0)[8:, :] for s in (1, 2, 3)]


def _rg_fwd(z, cw, cb, wa, ba, wx, bx, lam):
    t = z.shape[0]
    cr = _tile(t, 256)
    nb = D_A // LANES

    def body(ax_ref, ag_ref, cw_ref, cb_ref, wa_ref, ba_ref, wx_ref, bx_ref, lam_ref, out_ref, h_ref, axp, a_s, b_s):
        axp[pl.ds(0, 8), :] = jnp.zeros((8, LANES), F32)
        pltpu.sync_copy(ax_ref, axp.at[pl.ds(8, t)])
        w = [cw_ref[pl.ds(k, 1), :] for k in range(CONV_WIDTH)]

        def chunk(c, carry):
            t0 = pl.multiple_of(c * cr, cr)
            taps = _conv_taps(axp[pl.ds(t0, cr + 8), :], cr)
            xa = cb_ref[...] + w[3] * taps[0] + w[2] * taps[1] + w[1] * taps[2] + w[0] * taps[3]
            a, gated = _rg_local(xa, wa_ref[...], ba_ref[...], wx_ref[...], bx_ref[...], lam_ref[...])
            a_s[pl.ds(t0, cr), :] = a
            b_s[pl.ds(t0, cr), :] = gated
            return carry

        lax.fori_loop(0, t // cr, chunk, 0)

        _scan8(a_s, b_s, h_ref, t)

        def fin(c, carry):
            t0 = pl.multiple_of(c * cr, cr)
            out_ref[pl.ds(t0, cr), :] = _gelu(ag_ref[pl.ds(t0, cr), :]) * h_ref[pl.ds(t0, cr), :]
            return carry

        lax.fori_loop(0, t // cr, fin, 0)

    col = lambda off: pl.BlockSpec((t, LANES), lambda b: (0, off + b))
    vec = pl.BlockSpec((1, LANES), lambda b: (0, b))
    mat = pl.BlockSpec((None, LANES, LANES), lambda b: (b, 0, 0))
    return pl.pallas_call(
        body, grid=(nb,),
        in_specs=[col(0), col(nb), pl.BlockSpec((CONV_WIDTH, LANES), lambda b: (0, b)), vec, mat, vec, mat, vec, vec],
        out_specs=[col(0), col(0)],
        out_shape=[jax.ShapeDtypeStruct((t, D_A), F32), jax.ShapeDtypeStruct((t, D_A), F32)],
        scratch_shapes=[pltpu.VMEM((t + 8, LANES), F32), pltpu.VMEM((t, LANES), F32), pltpu.VMEM((t, LANES), F32)],
        compiler_params=pltpu.CompilerParams(dimension_semantics=("arbitrary",)), name="rglru_fwd")(
            z, z, cw, cb, wa, ba, wx, bx, lam)


def _rg_bwd(z, h, dout, cw, cb, wa, ba, wx, bx, lam):
    t = z.shape[0]
    cr = _tile(t, 256)
    nb = D_A // LANES

    def body(ax_ref, ag_ref, h_ref, do_ref, cw_ref, cb_ref, wa_ref, ba_ref, wx_ref, bx_ref, lam_ref,
             dax_ref, dag_ref, dcw_ref, dcb_ref, dwa_ref, dba_ref, dwx_ref, dbx_ref, dlam_ref,
             axp, hp, xa_s, a_s, g_s, dxa_s, u_s):
        zero8 = jnp.zeros((8, LANES), F32)
        axp[pl.ds(0, 8), :] = zero8
        hp[pl.ds(0, 8), :] = zero8
        dxa_s[pl.ds(t, 8), :] = zero8
        u_s[pl.ds(t, 8), :] = zero8
        pltpu.sync_copy(ax_ref, axp.at[pl.ds(8, t)])
        pltpu.sync_copy(h_ref, hp.at[pl.ds(8, t)])
        w = [cw_ref[pl.ds(k, 1), :] for k in range(CONV_WIDTH)]
        for ref in (dcw_ref, dcb_ref, dwa_ref, dba_ref, dwx_ref, dbx_ref, dlam_ref):
            ref[...] = jnp.zeros(ref.shape, F32)

        def p1(c, carry):
            t0 = pl.multiple_of(c * cr, cr)
            taps = _conv_taps(axp[pl.ds(t0, cr + 8), :], cr)
            xa = cb_ref[...] + w[3] * taps[0] + w[2] * taps[1] + w[1] * taps[2] + w[0] * taps[3]
            a, _ = _rg_local(xa, wa_ref[...], ba_ref[...], wx_ref[...], bx_ref[...], lam_ref[...])
            xa_s[pl.ds(t0, cr), :] = xa
            a_s[pl.ds(t0, cr), :] = a
            ag = ag_ref[pl.ds(t0, cr), :]
            dov = do_ref[pl.ds(t0, cr), :]
            gel, pull = jax.vjp(_gelu, ag)
            g_s[pl.ds(t0, cr), :] = dov * gel
            u_s[pl.ds(t0, cr), :] = a * (dov * gel)
            dag_ref[pl.ds(t0, cr), :] = pull(dov * h_ref[pl.ds(t0, cr), :])[0]
            return carry

        lax.fori_loop(0, t // cr, p1, 0)
        _scan8(a_s, u_s, u_s, t, reverse=True)

        def p3(c, carry):
            t0 = pl.multiple_of(c * cr, cr)
            g = g_s[pl.ds(t0, cr), :] + pltpu.roll(u_s[pl.ds(t0, cr + 8), :], cr + 7, 0)[:cr, :]
            h_prev = pltpu.roll(hp[pl.ds(t0, cr + 8), :], 1, 0)[8:, :]
            _, pull = jax.vjp(_rg_local, xa_s[pl.ds(t0, cr), :], wa_ref[...], ba_ref[...], wx_ref[...], bx_ref[...],
                              lam_ref[...])
            dxa, dwa, dba, dwx, dbx, dlam = pull((g * h_prev, g))
            dxa_s[pl.ds(t0, cr), :] = dxa
            dwa_ref[...] += dwa
            dba_ref[...] += dba
            dwx_ref[...] += dwx
            dbx_ref[...] += dbx
            dlam_ref[...] += dlam
            return carry

        lax.fori_loop(0, t // cr, p3, 0)

        def p4(c, carry):
            t0 = pl.multiple_of(c * cr, cr)
            ext = dxa_s[pl.ds(t0, cr + 8), :]
            n = cr + 8
            ahead = [ext[:cr, :]] + [pltpu.roll(ext, n - s, 0)[:cr, :] for s in (1, 2, 3)]
            dax_ref[pl.ds(t0, cr), :] = w[3] * ahead[0] + w[2] * ahead[1] + w[1] * ahead[2] + w[0] * ahead[3]
            taps = _conv_taps(axp[pl.ds(t0, cr + 8), :], cr)
            dxa = ahead[0]
            for k in range(CONV_WIDTH):
                dcw_ref[pl.ds(k, 1), :] += jnp.sum(dxa * taps[CONV_WIDTH - 1 - k], axis=0, keepdims=True)
            dcb_ref[...] += jnp.sum(dxa, axis=0, keepdims=True)
            return carry

        lax.fori_loop(0, t // cr, p4, 0)

    col = lambda off: pl.BlockSpec((t, LANES), lambda b: (0, off + b))
    vec = pl.BlockSpec((1, LANES), lambda b: (0, b))
    mat = pl.BlockSpec((None, LANES, LANES), lambda b: (b, 0, 0))
    cws = pl.BlockSpec((CONV_WIDTH, LANES), lambda b: (0, b))
    sds = jax.ShapeDtypeStruct
    return pl.pallas_call(
        body, grid=(nb,),
        in_specs=[col(0), col(nb), col(0), col(0), cws, vec, mat, vec, mat, vec, vec],
        out_specs=[col(0), col(0), cws, vec, mat, vec, mat, vec, vec],
        out_shape=[sds((t, D_A), F32), sds((t, D_A), F32), sds((CONV_WIDTH, D_A), F32), sds((1, D_A), F32),
                   sds((nb, LANES, LANES), F32), sds((1, D_A), F32), sds((nb, LANES, LANES), F32), sds((1, D_A), F32),
                   sds((1, D_A), F32)],
        scratch_shapes=[pltpu.VMEM((t + 8, LANES), F32), pltpu.VMEM((t + 8, LANES), F32), pltpu.VMEM((t, LANES), F32),
                        pltpu.VMEM((t, LANES), F32), pltpu.VMEM((t, LANES), F32), pltpu.VMEM((t + 8, LANES), F32),
                        pltpu.VMEM((t + 8, LANES), F32)],
        compiler_params=pltpu.CompilerParams(dimension_semantics=("arbitrary",)), name="rglru_bwd")(
            z, z, h, dout, cw, cb, wa, ba, wx, bx, lam)


def _fgate_fwd(z, bf):
    t = z.shape[0]

    def body(zf_ref, bf_ref, c_ref):
        c_ref[...] = -_softplus(-(zf_ref[...] + bf_ref[...]))
        _scan8(None, c_ref, c_ref, t)

    return pl.pallas_call(
        body, grid=(1,), in_specs=[pl.BlockSpec((t, LANES), lambda i: (0, Z_F // LANES)), pl.BlockSpec((1, LANES), lambda i: (0, 0))],
        out_specs=pl.BlockSpec((t, LANES), lambda i: (0, 0)), out_shape=jax.ShapeDtypeStruct((t, LANES), F32),
        compiler_params=pltpu.CompilerParams(dimension_semantics=("arbitrary",)), name="fgate_fwd")(z, bf)


def _fgate_bwd(z, bf, dc):
    t = z.shape[0]

    def body(zf_ref, bf_ref, dc_ref, dz_ref, db_ref):
        _scan8(None, dc_ref, dz_ref, t, reverse=True)
        dz = dz_ref[...] * _sig(-(zf_ref[...] + bf_ref[...]))
        dz_ref[...] = dz
        db_ref[...] = jnp.sum(dz, axis=0, keepdims=True)

    return pl.pallas_call(
        body, grid=(1,),
        in_specs=[pl.BlockSpec((t, LANES), lambda i: (0, Z_F // LANES)), pl.BlockSpec((1, LANES), lambda i: (0, 0)),
                  pl.BlockSpec((t, LANES), lambda i: (0, 0))],
        out_specs=[pl.BlockSpec((t, LANES), lambda i: (0, 0)), pl.BlockSpec((1, LANES), lambda i: (0, 0))],
        out_shape=[jax.ShapeDtypeStruct((t, LANES), F32), jax.ShapeDtypeStruct((1, LANES), F32)],
        compiler_params=pltpu.CompilerParams(dimension_semantics=("arbitrary",)), name="fgate_bwd")(z, bf, dc)


def _cast_rows(src_ref, dst_ref, t, rows, fn):
    def cp(c, carry):
        r0 = pl.multiple_of(c * rows, rows)
        dst_ref[pl.ds(r0, rows), :] = fn(src_ref[pl.ds(r0, rows), :]).astype(dst_ref.dtype)
        return carry

    lax.fori_loop(0, t // rows, cp, 0)


def _attn_groups(t):
    tq = _tile(t, 256)
    nq = t // tq
    grp = 4 if nq % 4 == 0 else 1
    return tq, nq, grp


def _attn_fwd(z, crow, carry=None):
    t = z.shape[0]
    tq, nq, grp = _attn_groups(t)
    tk = grp * tq
    scale = HEAD_DIM ** -0.5

    def body(q_ref, k_ref, v_ref, cr_ref, o_ref, lse_ref, kb_s, vb_s):
        lane = lax.broadcasted_iota(jnp.int32, (1, LANES), 1)
        hmask = [(lane // HEAD_DIM) == hh for hh in range(2)]
        _cast_rows(k_ref, kb_s, t, tq, lambda v: v)
        _cast_rows(v_ref, vb_s, t, tq, lambda v: v)

        def qblock(g, r):
            q0 = pl.multiple_of((g * grp + r) * tq, tq)
            qv = q_ref[pl.ds(q0, tq), :] * scale
            qa = [jnp.where(hmask[hh], qv, 0.0).astype(BF16) for hh in range(2)]

            def update(st, k0, width, off):
                kb = kb_s[pl.ds(k0, width), :]
                vb = vb_s[pl.ds(k0, width), :]
                new = []
                for hh in range(2):
                    m, l, acc = st[hh]
                    s = _dot(qa[hh], kb, NT) - cr_ref[hh, :, pl.ds(k0, width)]
                    if off is not None:
                        keep = (lax.broadcasted_iota(jnp.int32, (tq, width), 0) + off
                                >= lax.broadcasted_iota(jnp.int32, (tq, width), 1))
                        s = jnp.where(keep, s, NEG)
                    m_new = jnp.maximum(m, jnp.max(s, axis=-1, keepdims=True))
                    p = jnp.exp(s - m_new)
                    corr = jnp.exp(m - m_new)
                    new.append((m_new, corr * l + jnp.sum(p, axis=-1, keepdims=True), corr * acc + _dot(p, vb, NN)))
                return tuple(new)

            one = (jnp.full((tq, 1), NEG, F32), jnp.zeros((tq, 1), F32), jnp.zeros((tq, LANES), F32))
            st = lax.fori_loop(0, g, lambda j, st: update(st, pl.multiple_of(j * tk, tk), tk, None), (one, one))
            st = update(st, pl.multiple_of(g * tk, tk), (r + 1) * tq, r * tq)
            o_ref[pl.ds(q0, tq), :] = jnp.where(hmask[0], st[0][2] / st[0][1], st[1][2] / st[1][1])
            for hh in range(2):
                lse_ref[hh, pl.ds(q0, tq), :] = st[hh][0] + jnp.log(st[hh][1])

        def group(g, carry):
            for r in range(grp):
                qblock(g, r)
            return carry

        lax.fori_loop(0, nq // grp, group, 0)

    base = 2 * D_A // LANES
    nh = D_B // LANES
    col = lambda off: pl.BlockSpec((t, LANES), lambda p: (0, off + p))
    return _call(
        body, (nh,), [col(base), col(base + nh), col(base + 2 * nh), pl.BlockSpec((2, 1, t), lambda p: (p, 0, 0))],
        [col(0), pl.BlockSpec((2, t, 1), lambda p: (p, 0, 0))],
        [jax.ShapeDtypeStruct((t, D_B), F32), jax.ShapeDtypeStruct((N_HEADS, t, 1), F32)],
        [pltpu.VMEM((t, LANES), BF16), pltpu.VMEM((t, LANES), BF16)], ("parallel",), "attn_fwd", [z, z, z, crow], carry)


def _attn_bwd(z, crow, lse, do, carry=None):
    t = z.shape[0]
    tq, nq, grp = _attn_groups(t)
    tw = grp * tq
    scale = HEAD_DIM ** -0.5

    def body(q_ref, k_ref, v_ref, cr_ref, lse_ref, do_ref, dq_ref, dk_ref, dv_ref, dc_ref, qa_s, da_s, kb_s, vb_s, dl_s):
        lane = lax.broadcasted_iota(jnp.int32, (1, LANES), 1)
        hmask = [(lane // HEAD_DIM) == hh for hh in range(2)]
        _cast_rows(k_ref, kb_s, t, tq, lambda v: v)
        _cast_rows(v_ref, vb_s, t, tq, lambda v: v)
        for hh in range(2):
            _cast_rows(q_ref, qa_s.at[hh], t, tq, lambda v, hh=hh: jnp.where(hmask[hh], v * scale, 0.0))
            _cast_rows(do_ref, da_s.at[hh], t, tq, lambda v, hh=hh: jnp.where(hmask[hh], v, 0.0))
        _cast_rows(q_ref, dq_ref, t, tq, lambda v: jnp.zeros_like(v))

        def probs(hh, q0, nq_rows, k0, nk_rows, off):
            s = _dot(qa_s[hh, pl.ds(q0, nq_rows), :], kb_s[pl.ds(k0, nk_rows), :], NT) - cr_ref[hh, :, pl.ds(k0, nk_rows)]
            p = jnp.exp(s - lse_ref[hh, pl.ds(q0, nq_rows), :])
            if off is not None:
                keep = (lax.broadcasted_iota(jnp.int32, (nq_rows, nk_rows), 0) + off
                        >= lax.broadcasted_iota(jnp.int32, (nq_rows, nk_rows), 1))
                p = jnp.where(keep, p, 0.0)
            return p, _dot(da_s[hh, pl.ds(q0, nq_rows), :], vb_s[pl.ds(k0, nk_rows), :], NT)

        def delta(g, r):
            q0 = pl.multiple_of((g * grp + r) * tq, tq)

            def add(k0, width, off, acc):
                res = []
                for hh in range(2):
                    p, dp = probs(hh, q0, tq, k0, width, off)
                    res.append(acc[hh] + jnp.sum(p * dp, axis=-1, keepdims=True))
                return tuple(res)

            zcol = jnp.zeros((tq, 1), F32)
            acc = lax.fori_loop(0, g, lambda j, acc: add(pl.multiple_of(j * tw, tw), tw, None, acc), (zcol, zcol))
            acc = add(pl.multiple_of(g * tw, tw), (r + 1) * tq, r * tq, acc)
            for hh in range(2):
                dl_s[hh, pl.ds(q0, tq), :] = acc[hh]

        def delta_group(g, carry):
            for r in range(grp):
                delta(g, r)
            return carry

        lax.fori_loop(0, nq // grp, delta_group, 0)

        def kblock(g, r):
            k0 = pl.multiple_of((g * grp + r) * tq, tq)
            kb = kb_s[pl.ds(k0, tq), :]

            def upd(q0, height, off, st):
                dk, dv, dc = st[0], st[1], [st[2], st[3]]
                dqs = []
                for hh in range(2):
                    p, dp = probs(hh, q0, height, k0, tq, off)
                    ds = p * (dp - dl_s[hh, pl.ds(q0, height), :])
                    dv = dv + _dot(p, da_s[hh, pl.ds(q0, height), :], TN)
                    dk = dk + _dot(ds, qa_s[hh, pl.ds(q0, height), :], TN)
                    dqs.append(_dot(ds, kb, NN))
                    dc[hh] = dc[hh] - jnp.sum(ds, axis=0, keepdims=True)
                dq_ref[pl.ds(q0, height), :] += jnp.where(hmask[0], dqs[0], dqs[1]) * scale
                return dk, dv, dc[0], dc[1]

            zero = jnp.zeros((tq, LANES), F32)
            zrow = jnp.zeros((1, tq), F32)
            st = upd(k0, (grp - r) * tq, 0, (zero, zero, zrow, zrow))
            st = lax.fori_loop(g + 1, nq // grp, lambda i, st: upd(pl.multiple_of(i * tw, tw), tw, None, st), st)
            dk_ref[pl.ds(k0, tq), :] = st[0]
            dv_ref[pl.ds(k0, tq), :] = st[1]
            for hh in range(2):
                dc_ref[hh, :, pl.ds(k0, tq)] = st[2 + hh]

        def kgroup(g, carry):
            for r in range(grp):
                kblock(g, r)
            return carry

        lax.fori_loop(0, nq // grp, kgroup, 0)

    base = 2 * D_A // LANES
    nh = D_B // LANES
    col = lambda off: pl.BlockSpec((t, LANES), lambda p: (0, off + p))
    ccs = pl.BlockSpec((2, t, 1), lambda p: (p, 0, 0))
    crs = pl.BlockSpec((2, 1, t), lambda p: (p, 0, 0))
    return _call(
        body, (nh,), [col(base), col(base + nh), col(base + 2 * nh), crs, ccs, col(0)], [col(0), col(0), col(0), crs],
        [jax.ShapeDtypeStruct((t, D_B), F32)] * 3 + [jax.ShapeDtypeStruct((N_HEADS, 1, t), F32)],
        [pltpu.VMEM((2, t, LANES), BF16), pltpu.VMEM((2, t, LANES), BF16), pltpu.VMEM((t, LANES), BF16),
         pltpu.VMEM((t, LANES), BF16), pltpu.VMEM((2, t, 1), F32)], ("parallel",), "attn_bwd", [z, z, z, crow, lse, do], carry)


def _s5_disc(a_re, a_im, log_dt, b_re, b_im):
    dt = jnp.exp(log_dt)
    mag = jnp.exp(a_re * dt)
    ar = mag * jnp.cos(a_im * dt)
    ai = mag * jnp.sin(a_im * dt)
    den = a_re * a_re + a_im * a_im
    kr = ((ar - 1.0) * a_re + ai * a_im) / den
    ki = (ai * a_re - (ar - 1.0) * a_im) / den
    kr3, ki3 = kr[:, None, :], ki[:, None, :]
    return ar, ai, kr3 * b_re - ki3 * b_im, kr3 * b_im + ki3 * b_re


def _s5_prep(a_re, a_im, log_dt, b_re, b_im):
    g, p = a_re.shape
    gc = b_re.shape[1]

    def body(*refs):
        res = _s5_disc(*[r[...] for r in refs[:5]])
        for o_ref, v in zip(refs[5:], res):
            o_ref[...] = v

    sds = jax.ShapeDtypeStruct
    return pl.pallas_call(body, out_shape=[sds((g, p), F32), sds((g, p), F32), sds((g, gc, p), F32), sds((g, gc, p), F32)],
                          name="s5_prep")(a_re, a_im, log_dt, b_re, b_im)


def _s5_prep_bwd(a_re, a_im, log_dt, b_re, b_im, d_ar, d_ai, d_br, d_bi):
    ins = (a_re, a_im, log_dt, b_re, b_im)

    def body(*refs):
        vals = [r[...] for r in refs[:5]]
        cts = tuple(r[...] for r in refs[5:9])
        _, pull = jax.vjp(_s5_disc, *vals)
        for o_ref, v in zip(refs[9:], pull(cts)):
            o_ref[...] = v

    return pl.pallas_call(body, out_shape=[jax.ShapeDtypeStruct(a.shape, F32) for a in ins], name="s5_prep_bwd")(
        *ins, d_ar, d_ai, d_br, d_bi)


def _s5_scan_rows(t, ar, ai, hr_s, hi_s, off, reverse):
    n = ar.shape[1]
    if reverse:
        ai = -ai
    sub = lax.broadcasted_iota(jnp.int32, (8, n), 0)
    cmul = lambda xr, xi, yr, yi: (xr * yr - xi * yi, xr * yi + xi * yr)
    pw = [(ar, ai)]
    for _ in range(7):
        pw.append(cmul(*pw[-1], ar, ai))
    pr = jnp.zeros((8, n), F32)
    pi = jnp.zeros((8, n), F32)
    for r in range(8):
        k = 7 - r if reverse else r
        pr = jnp.where(sub == r, pw[k][0], pr)
        pi = jnp.where(sub == r, pw[k][1], pi)

    def step(g, carry):
        cr, ci = carry
        r0 = pl.multiple_of(off + (t // 8 - 1 - g if reverse else g) * 8, 8)
        br = hr_s[pl.ds(r0, 8), :]
        bi = hi_s[pl.ds(r0, 8), :]
        for s in (1, 2, 4):
            ok = (sub < 8 - s) if reverse else (sub >= s)
            shift = 8 - s if reverse else s
            sr = jnp.where(ok, pltpu.roll(br, shift, 0), 0.0)
            si = jnp.where(ok, pltpu.roll(bi, shift, 0), 0.0)
            mr, mi = cmul(pw[s - 1][0], pw[s - 1][1], sr, si)
            br, bi = br + mr, bi + mi
        mr, mi = cmul(pr, pi, cr, ci)
        br, bi = br + mr, bi + mi
        hr_s[pl.ds(r0, 8), :] = br
        hi_s[pl.ds(r0, 8), :] = bi
        edge = sub == (0 if reverse else 7)
        return (jnp.sum(jnp.where(edge, br, 0.0), axis=0, keepdims=True),
                jnp.sum(jnp.where(edge, bi, 0.0), axis=0, keepdims=True))

    zero = jnp.zeros((1, n), F32)
    lax.fori_loop(0, t // 8, step, (zero, zero))


def _s5_fwd(z, bd_re, bd_im, ab_re, ab_im, cd_re, cd_im, dvec, carry=None):
    t = z.shape[0]
    cr = _tile(t, 256)
    ns = N_STATE // 2

    def body(u_ref, br_ref, bi_ref, ar_ref, ai_ref, cre_ref, cim_ref, d_ref, y_ref, hr_s, hi_s):
        def p1(c, carry):
            t0 = pl.multiple_of(c * cr, cr)
            u = u_ref[pl.ds(t0, cr), :]
            hr_s[pl.ds(t0, cr), :] = _dot(u, br_ref[...], NN)
            hi_s[pl.ds(t0, cr), :] = _dot(u, bi_ref[...], NN)
            return carry

        lax.fori_loop(0, t // cr, p1, 0)
        _s5_scan_rows(t, ar_ref[...], ai_ref[...], hr_s, hi_s, 0, False)

        def p3(c, carry):
            t0 = pl.multiple_of(c * cr, cr)
            y_ref[pl.ds(t0, cr), :] = (_dot(hr_s[pl.ds(t0, cr), :], cre_ref[...], NN)
                                       - _dot(hi_s[pl.ds(t0, cr), :], cim_ref[...], NN)
                                       + d_ref[...] * u_ref[pl.ds(t0, cr), :])
            return carry

        lax.fori_loop(0, t // cr, p3, 0)

    blk = lambda r, c: pl.BlockSpec((None, r, c), lambda b: (b, 0, 0))
    return _call(
        body, (2,),
        [pl.BlockSpec((t, LANES), lambda b: (0, Z_U // LANES + b)), blk(LANES, ns), blk(LANES, ns), blk(1, ns),
         blk(1, ns), blk(ns, LANES), blk(ns, LANES), pl.BlockSpec((1, LANES), lambda b: (0, b))],
        [pl.BlockSpec((t, LANES), lambda b: (0, b))], [jax.ShapeDtypeStruct((t, D_C), F32)],
        [pltpu.VMEM((t, ns), F32), pltpu.VMEM((t, ns), F32)], ("arbitrary",), "s5_fwd",
        [z, bd_re, bd_im, ab_re, ab_im, cd_re, cd_im, dvec], carry)


def _s5_bwd(z, dy, bd_re, bd_im, ab_re, ab_im, cd_re, cd_im, dvec):
    t = z.shape[0]
    cr = _tile(t, 256)
    ns = N_STATE // 2

    def body(u_ref, dy_ref, br_ref, bi_ref, ar_ref, ai_ref, cre_ref, cim_ref, d_ref,
             du_ref, dbr_ref, dbi_ref, dar_ref, dai_ref, dcre_ref, dcim_ref, dd_ref, hr_s, hi_s, gr_s, gi_s):
        zero8 = jnp.zeros((8, ns), F32)
        hr_s[pl.ds(0, 8), :] = zero8
        hi_s[pl.ds(0, 8), :] = zero8
        for ref in (dbr_ref, dbi_ref, dar_ref, dai_ref, dcre_ref, dcim_ref, dd_ref):
            ref[...] = jnp.zeros(ref.shape, F32)

        def p1(c, carry):
            t0 = pl.multiple_of(c * cr, cr)
            u = u_ref[pl.ds(t0, cr), :]
            hr_s[pl.ds(t0 + 8, cr), :] = _dot(u, br_ref[...], NN)
            hi_s[pl.ds(t0 + 8, cr), :] = _dot(u, bi_ref[...], NN)
            return carry

        lax.fori_loop(0, t // cr, p1, 0)
        _s5_scan_rows(t, ar_ref[...], ai_ref[...], hr_s, hi_s, 8, False)

        def p3(c, carry):
            t0 = pl.multiple_of(c * cr, cr)
            dyv = dy_ref[pl.ds(t0, cr), :]
            u = u_ref[pl.ds(t0, cr), :]
            gr_s[pl.ds(t0, cr), :] = _dot(dyv, cre_ref[...], NT)
            gi_s[pl.ds(t0, cr), :] = -_dot(dyv, cim_ref[...], NT)
            dcre_ref[...] += _dot(hr_s[pl.ds(t0 + 8, cr), :], dyv, TN)
            dcim_ref[...] -= _dot(hi_s[pl.ds(t0 + 8, cr), :], dyv, TN)
            dd_ref[...] += jnp.sum(dyv * u, axis=0, keepdims=True)
            du_ref[pl.ds(t0, cr), :] = dyv * d_ref[...]
            return carry

        lax.fori_loop(0, t // cr, p3, 0)
        _s5_scan_rows(t, ar_ref[...], ai_ref[...], gr_s, gi_s, 0, True)

        def p5(c, carry):
            t0 = pl.multiple_of(c * cr, cr)
            u = u_ref[pl.ds(t0, cr), :]
            gr = gr_s[pl.ds(t0, cr), :]
            gi = gi_s[pl.ds(t0, cr), :]
            dbr_ref[...] += _dot(u, gr, TN)
            dbi_ref[...] += _dot(u, gi, TN)
            du_ref[pl.ds(t0, cr), :] += _dot(gr, br_ref[...], NT) + _dot(gi, bi_ref[...], NT)
            hpr = pltpu.roll(hr_s[pl.ds(t0, cr + 8), :], 1, 0)[8:, :]
            hpi = pltpu.roll(hi_s[pl.ds(t0, cr + 8), :], 1, 0)[8:, :]
            dar_ref[...] += jnp.sum(gr * hpr + gi * hpi, axis=0, keepdims=True)
            dai_ref[...] += jnp.sum(gi * hpr - gr * hpi, axis=0, keepdims=True)
            return carry

        lax.fori_loop(0, t // cr, p5, 0)

    blk = lambda r, c: pl.BlockSpec((None, r, c), lambda b: (b, 0, 0))
    ucol = pl.BlockSpec((t, LANES), lambda b: (0, Z_U // LANES + b))
    ycol = pl.BlockSpec((t, LANES), lambda b: (0, b))
    dsp = pl.BlockSpec((1, LANES), lambda b: (0, b))
    sds = jax.ShapeDtypeStruct
    return pl.pallas_call(
        body, grid=(2,),
        in_specs=[ucol, ycol, blk(LANES, ns), blk(LANES, ns), blk(1, ns), blk(1, ns), blk(ns, LANES), blk(ns, LANES), dsp],
        out_specs=[ycol, blk(LANES, ns), blk(LANES, ns), blk(1, ns), blk(1, ns), blk(ns, LANES), blk(ns, LANES), dsp],
        out_shape=[sds((t, D_C), F32), sds((2, LANES, ns), F32), sds((2, LANES, ns), F32), sds((2, 1, ns), F32),
                   sds((2, 1, ns), F32), sds((2, ns, LANES), F32), sds((2, ns, LANES), F32), sds((1, D_C), F32)],
        scratch_shapes=[pltpu.VMEM((t + 8, ns), F32), pltpu.VMEM((t + 8, ns), F32), pltpu.VMEM((t, ns), F32),
                        pltpu.VMEM((t, ns), F32)],
        compiler_params=pltpu.CompilerParams(dimension_semantics=("arbitrary",)), name="s5_bwd")(
            z, dy, bd_re, bd_im, ab_re, ab_im, cd_re, cd_im, dvec)


def _mix_out(out_a, out_b, yc, x1, ga, gb, gc, wglu, wout, ln_g, ln_b):
    yg = _gelu(yc)
    out_c = yg * _sig(_bdot(yg, wglu))
    o = jnp.concatenate([_rms(out_a, ga), _rms(out_b, gb), _rms(out_c, gc)], axis=-1)
    return (_ln(ALPHA * x1 + _bdot(o, wout), ln_g, ln_b),)


def _ln_only(pre, g, b):
    return (_ln(pre, g, b),)


def _loss_head(y, target, tm):
    t, d = y.shape

    def body(y_ref, t_ref, dy_ref, l_ref):
        i = pl.program_id(0)
        e = y_ref[...] - t_ref[...]
        dy_ref[...] = e * (1.0 / d)
        part = 0.5 * jnp.sum(jnp.sum(e * e, axis=-1, keepdims=True) * (1.0 / d), axis=0, keepdims=True)
        row = jnp.where(lax.broadcasted_iota(jnp.int32, (1, LANES), 1) == 0, part, 0.0)

        @pl.when(i == 0)
        def _():
            l_ref[...] = row

        @pl.when(i > 0)
        def _():
            l_ref[...] += row

    spec = pl.BlockSpec((tm, d), lambda i: (i, 0))
    return pl.pallas_call(
        body, grid=(t // tm,), in_specs=[spec, spec], out_specs=[spec, pl.BlockSpec((1, LANES), lambda i: (0, 0))],
        out_shape=[jax.ShapeDtypeStruct((t, d), F32), jax.ShapeDtypeStruct((1, LANES), F32)],
        compiler_params=pltpu.CompilerParams(dimension_semantics=("arbitrary",)), name="loss_head")(y, target)


def _adamw(w, g, m, v, name):
    r, c = w.shape
    tr = r
    for cand in (512, 256, 352, 128):
        if r % cand == 0:
            tr = cand
            break

    def body(w_ref, g_ref, m_ref, v_ref, d_ref, nm_ref, nv_ref):
        gv = g_ref[...]
        mn = ADAM_B1 * m_ref[...] + (1.0 - ADAM_B1) * gv
        vn = ADAM_B2 * v_ref[...] + (1.0 - ADAM_B2) * (gv * gv)
        m_hat = mn / (1.0 - ADAM_B1 ** ADAM_STEP)
        v_hat = vn / (1.0 - ADAM_B2 ** ADAM_STEP)
        d_ref[...] = -ADAM_LR * (m_hat / (jnp.sqrt(v_hat) + ADAM_EPS) + ADAM_WD * w_ref[...])
        nm_ref[...] = mn
        nv_ref[...] = vn

    spec = pl.BlockSpec((tr, c), lambda i: (i, 0))
    return pl.pallas_call(
        body, grid=(r // tr,), in_specs=[spec] * 4, out_specs=[spec] * 3,
        out_shape=[jax.ShapeDtypeStruct((r, c), F32)] * 3,
        compiler_params=pltpu.CompilerParams(dimension_semantics=("parallel",)), name=name)(w, g, m, v)


def _row_tile(r):
    for cand in (512, 448, 352, 256, 128):
        if r % cand == 0:
            return cand
    return r


def _pair_add(a, b, idx, name, out_dtype):
    _, r, w = a.shape
    tr = _row_tile(r)

    def body(i_ref, a_ref, b_ref, o_ref):
        o_ref[...] = (a_ref[...].astype(F32) + b_ref[...].astype(F32)).astype(o_ref.dtype)

    grid_spec = pltpu.PrefetchScalarGridSpec(
        num_scalar_prefetch=1, grid=(4, r // tr),
        in_specs=[pl.BlockSpec((None, tr, w), lambda q, i, s: (2 * q + s[0], i, 0)),
                  pl.BlockSpec((None, tr, w), lambda q, i, s: (q, i, 0))],
        out_specs=pl.BlockSpec((None, tr, w), lambda q, i, s: (q, i, 0)))
    return pl.pallas_call(body, grid_spec=grid_spec, out_shape=jax.ShapeDtypeStruct((4, r, w), out_dtype),
                          compiler_params=pltpu.CompilerParams(dimension_semantics=("parallel", "parallel")), name=name)(
                              idx, a, b)


def _quad_add(p, rb, idx, name):
    _, r, w = p.shape
    tr = _row_tile(r)

    def body(i_ref, p_ref, r0, r1, r2, o_ref):
        o_ref[...] = ((p_ref[...].astype(F32) + r0[...].astype(F32)) + r1[...].astype(F32)) + r2[...].astype(F32)

    grid_spec = pltpu.PrefetchScalarGridSpec(
        num_scalar_prefetch=1, grid=(r // tr,),
        in_specs=[pl.BlockSpec((None, tr, w), lambda i, s: (s[0], i, 0))]
        + [pl.BlockSpec((None, tr, w), functools.partial(lambda i, s, k: (k, i, 0), k=k)) for k in range(3)],
        out_specs=pl.BlockSpec((tr, w), lambda i, s: (i, 0)))
    return pl.pallas_call(body, grid_spec=grid_spec, out_shape=jax.ShapeDtypeStruct((r, w), F32),
                          compiler_params=pltpu.CompilerParams(dimension_semantics=("parallel",)), name=name)(
                              idx, p, rb, rb, rb)


def _gather_small(buf):
    def plan(x, y, c, ins, outs):
        dst = outs[0].at[4 * x + 2 * y + c]
        res = [(ins[0], dst, None)]
        for rel in range(1, 8):
            res.append((ins[0], dst, (x ^ (rel >> 2), y ^ ((rel >> 1) & 1), c ^ (rel & 1))))
        return res

    return _Carry([buf], [jax.ShapeDtypeStruct((8,) + buf.shape, buf.dtype)], {}, plan, 8)


def _sum_small(allb):
    _, r, w = allb.shape

    def body(a_ref, o_ref):
        s = a_ref[0]
        for k in range(1, 8):
            s = s + a_ref[k]
        o_ref[...] = s

    return pl.pallas_call(body, out_shape=jax.ShapeDtypeStruct((r, w), F32), name="ar_small_sum")(allb)


def _pad_rows(a, rows):
    return jnp.pad(a, ((0, rows - a.shape[0]), (0, 0)))


def _pack_small(arrs):
    rows, tail = [], []
    for a in arrs:
        if not tail and a.size % LANES == 0:
            rows.append(a.reshape(-1, LANES))
        else:
            tail.append(a.reshape(-1))
    n_rows = sum(r.shape[0] for r in rows)
    n_tail = sum(int(v.size) for v in tail)
    tail_rows = -(-n_tail // LANES)
    total_rows = n_rows + tail_rows + (-(n_rows + tail_rows)) % 8
    if tail:
        tail.append(jnp.zeros((tail_rows * LANES - n_tail,), F32))
        rows.append(jnp.concatenate(tail).reshape(tail_rows, LANES))
    if total_rows > n_rows + tail_rows:
        rows.append(jnp.zeros((total_rows - n_rows - tail_rows, LANES), F32))
    return jnp.concatenate(rows, axis=0)


def _unpack_small(buf, shapes):
    out, off = [], 0
    flat = None
    for s in shapes:
        n = int(np.prod(s))
        if off % LANES == 0 and n % LANES == 0:
            out.append(buf[off // LANES:(off + n) // LANES].reshape(s))
        else:
            flat = buf.reshape(-1) if flat is None else flat
            out.append(flat[off:off + n].reshape(s))
        off += n
    return out


def _block_diag(blocks, nb):
    m, r, c = blocks.shape
    n = m // nb
    eye = jnp.eye(n, dtype=blocks.dtype)
    return (blocks.reshape(nb, n, r, 1, c) * eye[None, :, None, :, None]).reshape(nb, n * r, n * c)


def _diag_blocks(dense, n):
    nb = dense.shape[0]
    r, c = dense.shape[1] // n, dense.shape[2] // n
    eye = jnp.eye(n, dtype=dense.dtype)
    return jnp.sum(dense.reshape(nb, n, r, n, c) * eye[None, :, None, :, None], axis=3).reshape(nb * n, r, c)


def kernel(x, ffn1_w_gate, ffn1_w_up, ffn1_w_down, ln1_g, ln1_b, w_in, conv_w, conv_b, rg_w_a, rg_b_a, rg_w_x, rg_b_x, rg_lambda, fox_b_f, s5_a_re, s5_a_im, s5_log_dt, s5_b_re, s5_b_im, s5_c_re, s5_c_im, s5_d, s5_w_glu, mix_norm_g, w_out, ln2_g, ln2_b, ffn2_w_gate, ffn2_w_up, ffn2_w_down, ln3_g, ln3_b, loss_target, m_ffn1_w_gate, m_ffn1_w_up, m_ffn1_w_down, m_ln1_g, m_ln1_b, m_w_in, m_conv_w, m_conv_b, m_rg_w_a, m_rg_b_a, m_rg_w_x, m_rg_b_x, m_rg_lambda, m_fox_b_f, m_s5_a_re, m_s5_a_im, m_s5_log_dt, m_s5_b_re, m_s5_b_im, m_s5_c_re, m_s5_c_im, m_s5_d, m_s5_w_glu, m_mix_norm_g, m_w_out, m_ln2_g, m_ln2_b, m_ffn2_w_gate, m_ffn2_w_up, m_ffn2_w_down, m_ln3_g, m_ln3_b, v_ffn1_w_gate, v_ffn1_w_up, v_ffn1_w_down, v_ln1_g, v_ln1_b, v_w_in, v_conv_w, v_conv_b, v_rg_w_a, v_rg_b_a, v_rg_w_x, v_rg_b_x, v_rg_lambda, v_fox_b_f, v_s5_a_re, v_s5_a_im, v_s5_log_dt, v_s5_b_re, v_s5_b_im, v_s5_c_re, v_s5_c_im, v_s5_d, v_s5_w_glu, v_mix_norm_g, v_w_out, v_ln2_g, v_ln2_b, v_ffn2_w_gate, v_ffn2_w_up, v_ffn2_w_down, v_ln3_g, v_ln3_b):
    a = dict(locals())
    w = {n: a[n] for n in WEIGHTS}
    t, d = x.shape[1], x.shape[2]
    f = ffn1_w_down.shape[1] * 8
    fs, ds = f // 8, d // 8
    mx, my, mc = lax.axis_index("x"), lax.axis_index("y"), lax.axis_index("c")
    me = 4 * mx + 2 * my + mc
    tm = _tile(t, 512)
    tf = f // 2
    win_rows = ds * Z_W // d

    FFN1, MIXW, FFN2 = ['g1', 'u1', 'd1'], ['win', 'wout', 'glu', 'conv'], ['g2', 'u2', 'd2']
    glu_rows = D_C * D_C // (8 * d)

    def shard_segs(l):
        wi = w['w_in'][l]
        win_p = jnp.concatenate([wi[:, :Z_F + N_HEADS], jnp.zeros((ds, Z_U - Z_F - N_HEADS), F32), wi[:, Z_F + N_HEADS:]], axis=1)
        conv_bits = lax.bitcast_convert_type(w['conv_w'][l], BF16).reshape(1, -1)
        segs = dict(g1=w['ffn1_w_gate'][l].T, u1=w['ffn1_w_up'][l].T, d1=w['ffn1_w_down'][l],
                    g2=w['ffn2_w_gate'][l].T, u2=w['ffn2_w_up'][l].T, d2=w['ffn2_w_down'][l],
                    win=win_p.reshape(win_rows, d), wout=w['w_out'][l], glu=_pad_rows(w['s5_w_glu'][l].reshape(-1, d), 16))
        segs = {k: v.astype(BF16) for k, v in segs.items()}
        segs['conv'] = _pad_rows(jnp.pad(conv_bits, ((0, 0), (0, d - conv_bits.shape[1]))), 16)
        return segs

    shards = [shard_segs(l) for l in range(DEPTH)]
    wts = {}

    def cat(keys):
        return [shards[l][k] for l, k in keys]

    def split(gs, keys):
        for (l, k), g in zip(keys, gs):
            wts[(l, k)] = g

    grp_a = [(0, k) for k in FFN1]
    grp_b = [(0, k) for k in MIXW]
    grp_c = [(0, k) for k in FFN2]
    grp_d = [(1, k) for k in FFN1]
    grp_e = [(1, k) for k in FFN2]
    grp_f = [(1, k) for k in MIXW]
    split(_run(_ag_sibling(_run(_ag_chips(cat(grp_a)), "ag_chips")), "ag_sibling"), grp_a)

    xs = x[0]
    saved = []
    cur = xs
    for l in range(DEPTH):
        row = lambda n: w[n][l].reshape(1, -1)
        ffn = lambda keys: tuple(wts[(l, k)].reshape(f, d) for k in keys)
        wa = _block_diag(w['rg_w_a'][l], 3)
        wx = _block_diag(w['rg_w_x'][l], 3)
        bf = jnp.pad(row('fox_b_f'), ((0, 0), (0, LANES - N_HEADS)))
        s5p = (w['s5_a_re'][l], w['s5_a_im'][l], w['s5_log_dt'][l].reshape(-1, 1),
               w['s5_b_re'][l].transpose(0, 2, 1), w['s5_b_im'][l].transpose(0, 2, 1))
        ab_re, ab_im, bb_re, bb_im = _s5_prep(*s5p)
        bd_re, bd_im = _block_diag(bb_re, 2), _block_diag(bb_im, 2)
        cd_re = _block_diag(w['s5_c_re'][l].transpose(0, 2, 1), 2)
        cd_im = _block_diag(w['s5_c_im'][l].transpose(0, 2, 1), 2)
        abr, abi = ab_re.reshape(2, 1, N_STATE // 2), ab_im.reshape(2, 1, N_STATE // 2)
        gm = row('mix_norm_g')
        ga, gb, gc = gm[:, :D_A], gm[:, D_A:D_A + D_B], gm[:, D_A + D_B:]

        x0 = cur
        ffn1 = ffn(FFN1)
        (x1, pre1, gs1, us1), cres = _ffn_fwd(x0, *ffn1, row('ln1_g'), row('ln1_b'), tm, tf,
                                              carry=_ag_chips(cat(grp_b if l == 0 else grp_e)))
        if l == 0:
            split(_run(_ag_sibling(cres), "ag_sibling"), grp_b)
        else:
            g_e = cres
        win = wts[(l, 'win')].reshape(d, Z_W)
        wout = wts[(l, 'wout')].reshape(d, d).astype(F32)
        wglu = wts[(l, 'glu')][:, :glu_rows].reshape(D_C, D_C).astype(F32)
        conv_full = lax.bitcast_convert_type(
            wts[(l, 'conv')][:, 0, :2 * CONV_WIDTH * D_A // 8].reshape(8, CONV_WIDTH, D_A // 8, 2), F32)
        conv_full = conv_full.transpose(1, 0, 2).reshape(CONV_WIDTH, D_A)
        z = _mm(x1, win, 'nn', F32, tm, Z_W, d, "mix_in")
        out_a, h_a = _rg_fwd(z, conv_full, row('conv_b'), wa, row('rg_b_a'), wx, row('rg_b_x'), row('rg_lambda'))
        cs = _fgate_fwd(z, bf)
        crow = cs[:, :N_HEADS].T.reshape(N_HEADS, 1, t)
        (out_b, lse), cres = _attn_fwd(z, crow, carry=_ag_chips(cat(grp_c)) if l == 0 else _ag_sibling(g_e))
        if l == 0:
            (yc,), cres = _s5_fwd(z, bd_re, bd_im, abr, abi, cd_re, cd_im, row('s5_d'),
                                  carry=_join(_ag_sibling(cres), _ag_chips(cat(grp_f))))
            split(cres[:len(grp_c)], grp_c)
            g_f = cres[len(grp_c):]
        else:
            split(cres, grp_e)
            (yc,), _ = _s5_fwd(z, bd_re, bd_im, abr, abi, cd_re, cd_im, row('s5_d'))
        mix_params = [ga, gb, gc, wglu, wout, row('ln2_g'), row('ln2_b')]
        (x2,) = _rowwise(_mix_out, [out_a, out_b, yc, x1], mix_params, [(d, F32)], _tile(t, 256), "mix_out")
        ffn2 = ffn(FFN2)
        (x3, pre3, gs2, us2), cres = _ffn_fwd(x2, *ffn2, row('ln3_g'), row('ln3_b'), tm, tf,
                                              carry=_join(_ag_chips(cat(grp_d)), _ag_sibling(g_f)) if l == 0 else None)
        if l == 0:
            split(cres[len(grp_d):], grp_f)
            split(_run(_ag_sibling(cres[:len(grp_d)]), "ag_sibling"), grp_d)
        saved.append(dict(x0=x0, x1=x1, pre1=pre1, gs1=gs1, us1=us1, z=z, out_a=out_a, h_a=h_a, crow=crow,
                          out_b=out_b, lse=lse, yc=yc, x2=x2, pre3=pre3, gs2=gs2, us2=us2, mix_params=mix_params,
                          ffn1=ffn1, ffn2=ffn2, win=win, conv_full=conv_full, wa=wa, wx=wx, bf=bf, s5p=s5p,
                          s5m=(bd_re, bd_im, abr, abi, cd_re, cd_im)))
        cur = x3

    dy, loss_row = _loss_head(cur, loss_target[0], tm)
    loss = lax.psum(loss_row[0, 0], ("x", "y", "c"))

    assert DEPTH == 2
    small_grads = {}
    small_names = ['conv_w'] + SMALL
    c_idx = jnp.reshape(mc, (1,)).astype(jnp.int32)
    chip_idx = jnp.reshape(2 * mx + my, (1,)).astype(jnp.int32)

    def blocks(arrs):
        return jnp.concatenate([v.astype(BF16).reshape(8, -1, d) for v in arrs], axis=1)

    def mixer_blocks(dwin, dwout, dwglu):
        glu = jnp.pad(dwglu.astype(BF16).reshape(8, glu_rows, d), ((0, 0), (0, 32 - glu_rows), (0, 0)))
        return jnp.concatenate([dwin.reshape(8, win_rows, d), dwout.astype(BF16).reshape(8, ds, d), glu], axis=1)

    def pair(full, ra):
        return _pair_add(full, ra, c_idx, "rs_add_sibling", BF16)

    for l in reversed(range(DEPTH)):
        s = saved[l]
        row = lambda n: w[n][l].reshape(1, -1)
        wg_tiles = dict(tm=tf, tn=d, tk=_tile(t, 1024))
        first = l == 0

        def ffn_back(dyv, pre, gs, us, wts3, xin, ln_g, ln_b, carry_fn=None, pipeline=None):
            (dpre,), (dlg, dlb), _ = _rowwise_vjp(_ln_only, [pre], [ln_g, ln_b], [dyv], tm, "ln_bwd")
            (dx, dg, du, hh), cres = _ffn_bwd(dpre, gs, us, *wts3, tm, tf, carry=carry_fn(dlg, dlb) if carry_fn else None)
            if pipeline is None:
                dwg = _mm(dg, xin, 'tn', BF16, name="ffn_dw_gate", **wg_tiles)
                dwu = _mm(du, xin, 'tn', BF16, name="ffn_dw_up", **wg_tiles)
                dwd = _mm(hh, dpre, 'tn', BF16, name="ffn_dw_down", **wg_tiles)
                return dx, (dwg, dwu, dwd), dlg, dlb, cres

            def front(dw):
                full = dw.reshape(8, fs, d)
                (ra,) = _run(_rs_sibling(full), "rs_sibling")
                return pair(full, ra)

            dwg, (rb_first,) = _mm(dg, xin, 'tn', BF16, name="ffn_dw_gate", carry=_rs_chips(pipeline), **wg_tiles)
            cres = list(cres) + [rb_first]
            p_g = front(dwg)
            dwu, (rb_g,) = _mm(du, xin, 'tn', BF16, name="ffn_dw_up", carry=_rs_chips(p_g), **wg_tiles)
            p_u = front(dwu)
            dwd, (rb_u,) = _mm(hh, dpre, 'tn', BF16, name="ffn_dw_down", carry=_rs_chips(p_u), **wg_tiles)
            p_d = front(dwd)
            (rb_d,) = _run(_rs_chips(p_d), "rs_chips")
            return dx, ((p_g, rb_g), (p_u, rb_u), (p_d, rb_d)), dlg, dlb, cres

        dx2, (dwg2, dwu2, dwd2), dl3g, dl3b, cres = ffn_back(
            dy, s['pre3'], s['gs2'], s['us2'], s['ffn2'], s['x2'], row('ln3_g'), row('ln3_b'),
            (lambda *_: _rs_sibling(full_l1)) if first else None)
        if first:
            part_l1 = pair(full_l1, cres[0])
            full_c2 = blocks([dwg2, dwu2, dwd2])
        (d_oa, d_ob, d_yc, d_x1), (dga, dgb, dgc, dwglu, dwout, dl2g, dl2b), cres = _rowwise_vjp(
            _mix_out, [s['out_a'], s['out_b'], s['yc'], s['x1']], s['mix_params'], [dx2], tm, "mix_out_bwd",
            carry=_rs_sibling(full_c2) if first else None)
        if first:
            part_c2 = pair(full_c2, cres[0])
        (d_ax, d_ag, dcw, dcb, dwa, dba, dwx, dbx, dlam) = _rg_bwd(
            s['z'], s['h_a'], d_oa, s['conv_full'], row('conv_b'), s['wa'], row('rg_b_a'), s['wx'], row('rg_b_x'), row('rg_lambda'))
        (dq, dk, dv, dcrow), cres = _attn_bwd(s['z'], s['crow'], s['lse'], d_ob,
                                              carry=_join(_rs_chips(part_l1), _rs_chips(part_c2)) if first else None)
        if first:
            rb_l1, rb_c2 = cres
        dc_pad = jnp.pad(dcrow.reshape(N_HEADS, t).T, ((0, 0), (0, LANES - N_HEADS)))
        dzf, dbf = _fgate_bwd(s['z'], s['bf'], dc_pad)
        du_c, dbd_re, dbd_im, dabr, dabi, dcd_re, dcd_im, dd = _s5_bwd(s['z'], d_yc, *s['s5m'], row('s5_d'))
        dz = jnp.concatenate([d_ax, d_ag, dq, dk, dv, dzf, du_c], axis=1)
        dx1 = _mm(dz, s['win'], 'nt', F32, tm, d, Z_W, "mix_in_dx", add=d_x1)
        dwin = _mm(s['x1'], dz, 'tn', BF16, d, Z_W, tm, "mix_in_dw")
        dbb_re, dbb_im = _diag_blocks(dbd_re, N_GROUPS // 2), _diag_blocks(dbd_im, N_GROUPS // 2)
        dcm_re, dcm_im = _diag_blocks(dcd_re, N_GROUPS // 2), _diag_blocks(dcd_im, N_GROUPS // 2)
        da_re, da_im, dlog_dt, db_re, db_im = _s5_prep_bwd(*s['s5p'], dabr.reshape(N_GROUPS, C_STATE), dabi.reshape(N_GROUPS, C_STATE), dbb_re, dbb_im)
        sg = dict(conv_w=dcw, conv_b=dcb, rg_w_a=_diag_blocks(dwa, 2), rg_b_a=dba,
                  rg_w_x=_diag_blocks(dwx, 2), rg_b_x=dbx, rg_lambda=dlam, fox_b_f=dbf[:, :N_HEADS],
                  s5_a_re=da_re, s5_a_im=da_im, s5_log_dt=dlog_dt, s5_b_re=db_re.transpose(0, 2, 1), s5_b_im=db_im.transpose(0, 2, 1),
                  s5_c_re=dcm_re.transpose(0, 2, 1), s5_c_im=dcm_im.transpose(0, 2, 1), s5_d=dd,
                  mix_norm_g=jnp.concatenate([dga, dgb, dgc], axis=1), ln2_g=dl2g, ln2_b=dl2b, ln3_g=dl3g, ln3_b=dl3b)
        small_grads[l] = sg

        if first:
            full_m = mixer_blocks(dwin, dwout, dwglu)
            (ra_m,) = _run(_rs_sibling(full_m), "rs_sibling")
            part_m = pair(full_m, ra_m)

            def last_carry(dlg, dlb):
                sg.update(ln1_g=dlg, ln1_b=dlb)
                packed = _pack_small([small_grads[ll][n] for n in small_names for ll in range(DEPTH)])
                return _gather_small(packed)

            dx0, tail, _, _, (all_small, rb_m) = ffn_back(dx1, s['pre1'], s['gs1'], s['us1'], s['ffn1'], s['x0'], row('ln1_g'),
                                                          row('ln1_b'), last_carry, pipeline=part_m)
        else:
            dx0, (dwg1, dwu1, dwd1), dl1g, dl1b, _ = ffn_back(dx1, s['pre1'], s['gs1'], s['us1'], s['ffn1'], s['x0'],
                                                              row('ln1_g'), row('ln1_b'))
            sg.update(ln1_g=dl1g, ln1_b=dl1b)
            full_l1 = jnp.concatenate([blocks([dwg1, dwu1, dwd1, dwg2, dwu2, dwd2]), mixer_blocks(dwin, dwout, dwglu)], axis=1)
        dy = dx0

    grad_x = dy.reshape(x.shape)
    quad = lambda part, rb: _quad_add(part, rb, chip_idx, "rs_add_chips")
    own = {}

    def take(rows_f32, keys, l):
        off = 0
        for k, r in keys:
            own[(l, k)] = rows_f32[off:off + r]
            off += r

    ffn_keys = lambda names: [(k, fs) for k in names]
    mix_keys = [('win', win_rows), ('wout', ds), ('glu', glu_rows)]
    take(quad(part_l1, rb_l1), ffn_keys(FFN1 + FFN2) + mix_keys, 1)
    take(quad(part_c2, rb_c2), ffn_keys(FFN2), 0)
    take(quad(part_m, rb_m), mix_keys, 0)
    for k, (part, rb) in zip(FFN1, tail):
        own[(0, k)] = quad(part, rb)

    grads = {}
    grads_t = {}
    for k, n in zip(FFN1 + FFN2, ['ffn1_w_gate', 'ffn1_w_up', 'ffn1_w_down', 'ffn2_w_gate', 'ffn2_w_up', 'ffn2_w_down']):
        stacked = jnp.stack([own[(l, k)] for l in range(DEPTH)])
        if 'down' in n:
            grads[n] = stacked
        else:
            grads_t[n] = stacked
            grads[n] = jnp.swapaxes(stacked, 1, 2)
    gwin = jnp.stack([own[(l, 'win')].reshape(ds, Z_W) for l in range(DEPTH)])
    grads['w_in'] = jnp.concatenate([gwin[:, :, :Z_F + N_HEADS], gwin[:, :, Z_U:]], axis=2)
    grads['w_out'] = jnp.stack([own[(l, 'wout')] for l in range(DEPTH)])
    grads['s5_w_glu'] = jnp.stack([own[(l, 'glu')].reshape(D_C // 8, D_C) for l in range(DEPTH)])

    small_shapes = [((DEPTH, CONV_WIDTH, D_A) if n == 'conv_w' else w[n].shape) for n in small_names]
    conv_zero = jnp.zeros((DEPTH, CONV_WIDTH, D_A), F32)
    summed = _sum_small(all_small)
    for n, g in zip(small_names, _unpack_small(summed, small_shapes)):
        grads[n] = g
    grads['conv_w'] = lax.dynamic_slice_in_dim(grads['conv_w'], me * (D_A // 8), D_A // 8, axis=2)

    delta, new_m, new_v = {}, {}, {}
    for n in BIG + ['conv_w']:
        if n in grads_t:
            sh = grads_t[n].shape
            two = lambda v: jnp.swapaxes(v, 1, 2).reshape(-1, sh[-1])
            back = lambda v: jnp.swapaxes(v.reshape(sh), 1, 2)
            g2 = grads_t[n].reshape(-1, sh[-1])
        else:
            sh = w[n].shape
            two = lambda v: v.reshape(-1, sh[-1])
            back = lambda v: v.reshape(sh)
            g2 = two(grads[n])
        dl, nm, nv = _adamw(two(w[n]), g2, two(a['m_' + n]), two(a['v_' + n]), "adamw_" + n)
        delta[n], new_m[n], new_v[n] = back(dl), back(nm), back(nv)
    dl, nm, nv = _adamw(_pack_small([conv_zero] + [w[n] for n in SMALL]), summed,
                        _pack_small([conv_zero] + [a['m_' + n] for n in SMALL]),
                        _pack_small([conv_zero] + [a['v_' + n] for n in SMALL]), "adamw_small")
    for n, v1, v2, v3 in zip(small_names[1:], _unpack_small(dl, small_shapes)[1:], _unpack_small(nm, small_shapes)[1:],
                             _unpack_small(nv, small_shapes)[1:]):
        delta[n], new_m[n], new_v[n] = v1, v2, v3

    return (loss, grad_x, *[grads[n] for n in WEIGHTS], *[delta[n] for n in WEIGHTS], *[new_m[n] for n in WEIGHTS],
            *[new_v[n] for n in WEIGHTS])
```

```python
import functools
import math

import jax
import jax.numpy as jnp
import numpy as np
from jax import lax
from jax.experimental import pallas as pl
from jax.experimental.pallas import tpu as pltpu

F32 = jnp.float32
BF16 = jnp.bfloat16
MESH = pl.DeviceIdType.MESH

DEPTH = 2
ALPHA = (2 * DEPTH) ** 0.25
LN_EPS = 1e-5
RMS_EPS = 1e-6
RG_C = 8.0
CONV_WIDTH = 4
HEAD_DIM = 64
C_GROUP = 16
C_STATE = 64
D_A = 384
D_B = 384
D_C = 256
N_HEADS = D_B // HEAD_DIM
N_GROUPS = D_C // C_GROUP
N_STATE = N_GROUPS * C_STATE
Z_F = 2 * D_A + 3 * D_B
Z_U = Z_F + 128
Z_W = Z_U + D_C
N_IN = Z_F + N_HEADS + D_C
ADAM_LR, ADAM_B1, ADAM_B2, ADAM_EPS, ADAM_WD, ADAM_STEP = 0.001, 0.9, 0.999, 1e-08, 0.01, 10
LANES = 128
NEG = -1e30

WEIGHTS = ['ffn1_w_gate', 'ffn1_w_up', 'ffn1_w_down', 'ln1_g', 'ln1_b', 'w_in', 'conv_w', 'conv_b', 'rg_w_a', 'rg_b_a',
           'rg_w_x', 'rg_b_x', 'rg_lambda', 'fox_b_f', 's5_a_re', 's5_a_im', 's5_log_dt', 's5_b_re', 's5_b_im', 's5_c_re',
           's5_c_im', 's5_d', 's5_w_glu', 'mix_norm_g', 'w_out', 'ln2_g', 'ln2_b', 'ffn2_w_gate', 'ffn2_w_up', 'ffn2_w_down',
           'ln3_g', 'ln3_b']
BIG = ['ffn1_w_gate', 'ffn1_w_up', 'ffn1_w_down', 'w_in', 's5_w_glu', 'w_out', 'ffn2_w_gate', 'ffn2_w_up', 'ffn2_w_down']
SMALL_TAIL = ['fox_b_f', 's5_log_dt']
SMALL = [n for n in WEIGHTS if n not in BIG and n != 'conv_w' and n not in SMALL_TAIL] + SMALL_TAIL


def _sig(x):
    return 1.0 / (1.0 + jnp.exp(-x))


def _gelu(x):
    return 0.5 * x * (1.0 + jnp.tanh(math.sqrt(2.0 / math.pi) * (x + 0.044715 * (x * x * x))))


def _softplus(x):
    return jnp.maximum(x, 0.0) + jnp.log(1.0 + jnp.exp(jnp.minimum(x, -x)))


def _dot(a, b, dims):
    return lax.dot_general(a.astype(BF16), b.astype(BF16), (dims, ((), ())), preferred_element_type=F32)


NN = ((1,), (0,))
NT = ((1,), (1,))
TN = ((0,), (0,))


@jax.custom_vjp
def _bdot(a, w):
    return _dot(a, w, NN)


def _bdot_fwd(a, w):
    return _dot(a, w, NN), (a, w)


def _bdot_bwd(res, ct):
    a, w = res
    return _dot(ct, w, NT), _dot(a, ct, TN)


_bdot.defvjp(_bdot_fwd, _bdot_bwd)


def _ln(pre, g, b):
    mu = jnp.mean(pre, axis=-1, keepdims=True)
    xc = pre - mu
    var = jnp.mean(xc * xc, axis=-1, keepdims=True)
    return xc * lax.rsqrt(var + LN_EPS) * g + b


def _rms(x, g):
    return x * lax.rsqrt(jnp.mean(x * x, axis=-1, keepdims=True) + RMS_EPS) * g


def _tile(n, want):
    return want if n % want == 0 else n


class _Carry:
    def __init__(self, ins, outs, aliases, plan, n):
        self.ins, self.outs, self.aliases, self.plan, self.n = list(ins), list(outs), dict(aliases), plan, n


def _join(a, b):
    na, ma = len(a.ins), len(a.outs)

    def plan(x, y, c, ins, outs):
        return a.plan(x, y, c, ins[:na], outs[:ma]) + b.plan(x, y, c, ins[na:], outs[ma:])

    aliases = dict(a.aliases)
    aliases.update({na + i: ma + j for i, j in b.aliases.items()})
    return _Carry(a.ins + b.ins, a.outs + b.outs, aliases, plan, a.n + b.n)


def _copies(carry, cins, couts, send, recv):
    x, y, c = lax.axis_index("x"), lax.axis_index("y"), lax.axis_index("c")
    res = []
    for k, (s, d, peer) in enumerate(carry.plan(x, y, c, cins, couts)):
        if peer is None:
            res.append(pltpu.make_async_copy(s, d, send.at[k]))
        else:
            res.append(pltpu.make_async_remote_copy(src_ref=s, dst_ref=d, send_sem=send.at[k], recv_sem=recv.at[k],
                                                    device_id=peer, device_id_type=MESH))
    return res


def _call(body, grid, in_specs, out_specs, out_shape, scratch, semantics, name, args, carry=None):
    n_in, n_out, n_scr = len(in_specs), len(out_specs), len(scratch)
    if carry is None:
        res = pl.pallas_call(body, grid=grid, in_specs=in_specs, out_specs=out_specs, out_shape=out_shape,
                             scratch_shapes=scratch, compiler_params=pltpu.CompilerParams(dimension_semantics=semantics),
                             name=name)(*args)
        return list(res), []
    nci, nco = len(carry.ins), len(carry.outs)

    def wrapped(*refs):
        o0 = n_in + nci
        s0 = o0 + n_out + nco
        cins, couts = refs[n_in:o0], refs[o0 + n_out:s0]
        send, recv = refs[s0 + n_scr:]
        first = functools.reduce(jnp.logical_and, [pl.program_id(k) == 0 for k in range(len(grid))])
        last = functools.reduce(jnp.logical_and, [pl.program_id(k) == grid[k] - 1 for k in range(len(grid))])

        @pl.when(first)
        def _():
            for cp in _copies(carry, cins, couts, send, recv):
                cp.start()

        body(*refs[:n_in], *refs[o0:o0 + n_out], *refs[s0:s0 + n_scr])

        @pl.when(last)
        def _():
            for cp in _copies(carry, cins, couts, send, recv):
                cp.wait()

    hbm = pl.BlockSpec(memory_space=pl.ANY)
    res = pl.pallas_call(
        wrapped, grid=grid, in_specs=list(in_specs) + [hbm] * nci, out_specs=list(out_specs) + [hbm] * nco,
        out_shape=list(out_shape) + carry.outs, scratch_shapes=list(scratch) + [pltpu.SemaphoreType.DMA((carry.n,))] * 2,
        input_output_aliases={n_in + i: n_out + j for i, j in carry.aliases.items()},
        compiler_params=pltpu.CompilerParams(dimension_semantics=("arbitrary",) * len(grid), has_side_effects=True),
        name=name)(*args, *carry.ins)
    return list(res[:n_out]), list(res[n_out:])


def _run(carry, name):
    nci, nco = len(carry.ins), len(carry.outs)

    def body(*refs):
        cps = _copies(carry, refs[:nci], refs[nci:nci + nco], refs[-2], refs[-1])
        for cp in cps:
            cp.start()
        for cp in cps:
            cp.wait()

    hbm = pl.BlockSpec(memory_space=pl.ANY)
    return pl.pallas_call(
        body, in_specs=[hbm] * nci, out_specs=[hbm] * nco, out_shape=carry.outs, input_output_aliases=carry.aliases,
        scratch_shapes=[pltpu.SemaphoreType.DMA((carry.n,))] * 2, compiler_params=pltpu.CompilerParams(has_side_effects=True),
        name=name)(*carry.ins)


def _ag_chips(shards):
    def plan(x, y, c, ins, outs):
        res = []
        for src, out in zip(ins, outs):
            dst = out.at[4 * x + 2 * y + c]
            res += [(src, dst, None)] + [(src, dst, (px, py, c)) for px, py in ((1 - x, y), (x, 1 - y), (1 - x, 1 - y))]
        return res

    return _Carry(shards, [jax.ShapeDtypeStruct((8,) + s.shape, s.dtype) for s in shards], {}, plan, 4 * len(shards))


def _ag_sibling(gs):
    def plan(x, y, c, ins, outs):
        return [(out.at[2 * q + c], out.at[2 * q + c], (x, y, 1 - c)) for out in outs for q in range(4)]

    return _Carry(gs, [jax.ShapeDtypeStruct(g.shape, g.dtype) for g in gs], {i: i for i in range(len(gs))}, plan, 4 * len(gs))


def _rs_sibling(full):
    def plan(x, y, c, ins, outs):
        return [(ins[0].at[2 * q + (1 - c)], outs[0].at[q], (x, y, 1 - c)) for q in range(4)]

    return _Carry([full], [jax.ShapeDtypeStruct((4,) + full.shape[1:], full.dtype)], {}, plan, 4)


def _rs_chips(part):
    def plan(x, y, c, ins, outs):
        res = []
        for k, (dx, dy) in enumerate(((1, 0), (0, 1), (1, 1))):
            tx, ty = x ^ dx, y ^ dy
            res.append((ins[0].at[2 * tx + ty], outs[0].at[k], (tx, ty, c)))
        return res

    return _Carry([part], [jax.ShapeDtypeStruct((3,) + part.shape[1:], part.dtype)], {}, plan, 3)


def _mm(a, b, dims, out_dtype, tm, tn, tk, name, add=None, carry=None):
    if dims == 'nn':
        (m, k), n = a.shape, b.shape[1]
        a_spec = pl.BlockSpec((tm, tk), lambda i, j, q: (i, q))
        b_spec = pl.BlockSpec((tk, tn), lambda i, j, q: (q, j))
        dn = NN
    elif dims == 'nt':
        (m, k), n = a.shape, b.shape[0]
        a_spec = pl.BlockSpec((tm, tk), lambda i, j, q: (i, q))
        b_spec = pl.BlockSpec((tn, tk), lambda i, j, q: (j, q))
        dn = NT
    else:
        (k, m), n = a.shape, b.shape[1]
        a_spec = pl.BlockSpec((tk, tm), lambda i, j, q: (q, i))
        b_spec = pl.BlockSpec((tk, tn), lambda i, j, q: (q, j))
        dn = TN
    nk = k // tk
    o_spec = pl.BlockSpec((tm, tn), lambda i, j, q: (i, j))

    def body(*refs):
        if add is None:
            a_ref, b_ref, o_ref, acc_ref = refs
        else:
            a_ref, b_ref, add_ref, o_ref, acc_ref = refs
        q = pl.program_id(2)
        part = _dot(a_ref[...], b_ref[...], dn)

        @pl.when(q == 0)
        def _():
            acc_ref[...] = part

        @pl.when(q > 0)
        def _():
            acc_ref[...] += part

        @pl.when(q == nk - 1)
        def _():
            r = acc_ref[...]
            if add is not None:
                r = r + add_ref[...]
            o_ref[...] = r.astype(o_ref.dtype)

    ins = [a, b] + ([] if add is None else [add])
    specs = [a_spec, b_spec] + ([] if add is None else [o_spec])
    (res,), cres = _call(body, (m // tm, n // tn, nk), specs, [o_spec], [jax.ShapeDtypeStruct((m, n), out_dtype)],
                         [pltpu.VMEM((tm, tn), F32)], ("parallel", "parallel", "arbitrary"), name, ins, carry)
    return res if carry is None else (res, cres)


def _rowwise(fn, rows, params, outs, tm, name):
    t = rows[0].shape[0]
    nr, npar = len(rows), len(params)

    def body(*refs):
        r = [x[...] for x in refs[:nr]]
        p = [x[...] for x in refs[nr:nr + npar]]
        res = fn(*r, *p)
        for o_ref, o in zip(refs[nr + npar:], res):
            o_ref[...] = o.astype(o_ref.dtype)

    in_specs = ([pl.BlockSpec((tm, a.shape[1]), lambda i: (i, 0)) for a in rows]
                + [pl.BlockSpec(p.shape, lambda i: (0, 0)) for p in params])
    return pl.pallas_call(
        body, grid=(t // tm,), in_specs=in_specs,
        out_specs=[pl.BlockSpec((tm, c), lambda i: (i, 0)) for c, _ in outs],
        out_shape=[jax.ShapeDtypeStruct((t, c), d) for c, d in outs],
        compiler_params=pltpu.CompilerParams(dimension_semantics=("parallel",)), name=name)(*rows, *params)


def _rowwise_vjp(fn, rows, params, cots, tm, name, carry=None):
    t = rows[0].shape[0]
    nr, npar, nc = len(rows), len(params), len(cots)

    def body(*refs):
        r = [x[...] for x in refs[:nr]]
        p = [x[...] for x in refs[nr:nr + npar]]
        c = [x[...] for x in refs[nr + npar:nr + npar + nc]]
        o_refs = refs[nr + npar + nc:]
        _, pull = jax.vjp(fn, *r, *p)
        grads = pull(tuple(c))
        for o_ref, g in zip(o_refs[:nr], grads[:nr]):
            o_ref[...] = g
        i = pl.program_id(0)

        @pl.when(i == 0)
        def _():
            for o_ref, g in zip(o_refs[nr:], grads[nr:]):
                o_ref[...] = g

        @pl.when(i > 0)
        def _():
            for o_ref, g in zip(o_refs[nr:], grads[nr:]):
                o_ref[...] += g

    row_spec = lambda a: pl.BlockSpec((tm, a.shape[1]), lambda i: (i, 0))
    par_spec = lambda p: pl.BlockSpec(p.shape, lambda i: (0, 0))
    res, cres = _call(
        body, (t // tm,),
        [row_spec(a) for a in rows] + [par_spec(p) for p in params] + [row_spec(a) for a in cots],
        [row_spec(a) for a in rows] + [par_spec(p) for p in params],
        [jax.ShapeDtypeStruct(a.shape, F32) for a in rows] + [jax.ShapeDtypeStruct(p.shape, F32) for p in params],
        [], ("arbitrary",), name, [*rows, *params, *cots], carry)
    return res[:nr], res[nr:], cres


def _ffn_fwd(x, wgt, wut, wd, ln_g, ln_b, tm, tf, carry=None):
    t, d = x.shape
    f = wgt.shape[0]
    nj = f // tf

    def body(x_ref, wg_ref, wu_ref, wd_ref, g_ref, b_ref, y_ref, pre_ref, gs_ref, us_ref, acc_ref):
        j = pl.program_id(1)
        xv = x_ref[...]
        xb = xv.astype(BF16)
        g = _dot(xb, wg_ref[...], NT)
        u = _dot(xb, wu_ref[...], NT)
        gs_ref[...] = g.astype(BF16)
        us_ref[...] = u.astype(BF16)
        part = _dot(g * _sig(g) * u, wd_ref[...], NN)

        @pl.when(j == 0)
        def _():
            acc_ref[...] = part

        @pl.when(j > 0)
        def _():
            acc_ref[...] += part

        @pl.when(j == nj - 1)
        def _():
            pre = ALPHA * xv + 0.5 * acc_ref[...]
            pre_ref[...] = pre
            y_ref[...] = _ln(pre, g_ref[...], b_ref[...])

    w_spec = pl.BlockSpec((tf, d), lambda i, j: (j, 0))
    x_spec = pl.BlockSpec((tm, d), lambda i, j: (i, 0))
    v_spec = pl.BlockSpec((1, d), lambda i, j: (0, 0))
    h_spec = pl.BlockSpec((tm, tf), lambda i, j: (i, j))
    return _call(
        body, (t // tm, nj), [x_spec, w_spec, w_spec, w_spec, v_spec, v_spec], [x_spec, x_spec, h_spec, h_spec],
        [jax.ShapeDtypeStruct((t, d), F32), jax.ShapeDtypeStruct((t, d), F32),
         jax.ShapeDtypeStruct((t, f), BF16), jax.ShapeDtypeStruct((t, f), BF16)],
        [pltpu.VMEM((tm, d), F32)], ("parallel", "arbitrary"), "ffn_fwd", [x, wgt, wut, wd, ln_g, ln_b], carry)


def _ffn_bwd(dpre, gs, us, wgt, wut, wd, tm, tf, carry=None):
    t, d = dpre.shape
    f = wgt.shape[0]
    nj = f // tf

    def body(dp_ref, gs_ref, us_ref, wg_ref, wu_ref, wd_ref, dx_ref, dg_ref, du_ref, hh_ref, acc_ref):
        j = pl.program_id(1)
        dp = dp_ref[...]
        dh = _dot(0.5 * dp, wd_ref[...], NT)
        g = gs_ref[...].astype(F32)
        u = us_ref[...].astype(F32)
        s = _sig(g)
        sl = g * s
        dg = (dh * u * (s * (1.0 + g * (1.0 - s)))).astype(BF16)
        du = (dh * sl).astype(BF16)
        dg_ref[...] = dg
        du_ref[...] = du
        hh_ref[...] = (0.5 * sl * u).astype(BF16)
        part = _dot(dg, wg_ref[...], NN) + _dot(du, wu_ref[...], NN)

        @pl.when(j == 0)
        def _():
            acc_ref[...] = part

        @pl.when(j > 0)
        def _():
            acc_ref[...] += part

        @pl.when(j == nj - 1)
        def _():
            dx_ref[...] = ALPHA * dp + acc_ref[...]

    w_spec = pl.BlockSpec((tf, d), lambda i, j: (j, 0))
    x_spec = pl.BlockSpec((tm, d), lambda i, j: (i, 0))
    h_spec = pl.BlockSpec((tm, tf), lambda i, j: (i, j))
    return _call(
        body, (t // tm, nj), [x_spec, h_spec, h_spec, w_spec, w_spec, w_spec], [x_spec, h_spec, h_spec, h_spec],
        [jax.ShapeDtypeStruct((t, d), F32)] + [jax.ShapeDtypeStruct((t, f), BF16)] * 3,
        [pltpu.VMEM((tm, d), F32)], ("parallel", "arbitrary"), "ffn_bwd", [dpre, gs, us, wgt, wut, wd], carry)


def _scan8(a_ref, b_ref, out_ref, t, reverse=False):
    w = out_ref.shape[-1]
    sub = lax.broadcasted_iota(jnp.int32, (8, w), 0)

    def step(g, carry):
        r0 = pl.multiple_of((t // 8 - 1 - g if reverse else g) * 8, 8)
        bv = b_ref[pl.ds(r0, 8), :]
        av = None if a_ref is None else a_ref[pl.ds(r0, 8), :]
        for s in (1, 2, 4):
            ok = (sub < 8 - s) if reverse else (sub >= s)
            shift = 8 - s if reverse else s
            b_sh = jnp.where(ok, pltpu.roll(bv, shift, 0), 0.0)
            if av is None:
                bv = bv + b_sh
            else:
                bv = av * b_sh + bv
                av = av * jnp.where(ok, pltpu.roll(av, shift, 0), 1.0)
        h = bv + carry if av is None else bv + av * carry
        out_ref[pl.ds(r0, 8), :] = h
        return jnp.sum(jnp.where(sub == (0 if reverse else 7), h, 0.0), axis=0, keepdims=True)

    lax.fori_loop(0, t // 8, step, jnp.zeros((1, w), F32))


def _rg_local(xa, wa, ba, wx, bx, lam):
    r = _sig(_bdot(xa, wa) + ba)
    i = _sig(_bdot(xa, wx) + bx)
    log_a = -RG_C * r * _softplus(-lam)
    a = jnp.exp(log_a)
    mult = jnp.sqrt(-jnp.tanh(log_a) * (a * a + 1.0))
    return a, mult * (i * xa)


def _conv_taps(ext, n):
    return [ext[8:, :]] + [pltpu.roll(ext, s, 0)[8:, :] for s in (1, 2, 3)]


def _rg_fwd(z, cw, cb, wa, ba, wx, bx, lam):
    t = z.shape[0]
    cr = _tile(t, 256)
    nb = D_A // LANES

    def body(ax_ref, ag_ref, cw_ref, cb_ref, wa_ref, ba_ref, wx_ref, bx_ref, lam_ref, out_ref, h_ref, axp, a_s, b_s):
        axp[pl.ds(0, 8), :] = jnp.zeros((8, LANES), F32)
        pltpu.sync_copy(ax_ref, axp.at[pl.ds(8, t)])
        w = [cw_ref[pl.ds(k, 1), :] for k in range(CONV_WIDTH)]

        def chunk(c, carry):
            t0 = pl.multiple_of(c * cr, cr)
            taps = _conv_taps(axp[pl.ds(t0, cr + 8), :], cr)
            xa = cb_ref[...] + w[3] * taps[0] + w[2] * taps[1] + w[1] * taps[2] + w[0] * taps[3]
            a, gated = _rg_local(xa, wa_ref[...], ba_ref[...], wx_ref[...], bx_ref[...], lam_ref[...])
            a_s[pl.ds(t0, cr), :] = a
            b_s[pl.ds(t0, cr), :] = gated
            return carry

        lax.fori_loop(0, t // cr, chunk, 0)

        _scan8(a_s, b_s, h_ref, t)

        def fin(c, carry):
            t0 = pl.multiple_of(c * cr, cr)
            out_ref[pl.ds(t0, cr), :] = _gelu(ag_ref[pl.ds(t0, cr), :]) * h_ref[pl.ds(t0, cr), :]
            return carry

        lax.fori_loop(0, t // cr, fin, 0)

    col = lambda off: pl.BlockSpec((t, LANES), lambda b: (0, off + b))
    vec = pl.BlockSpec((1, LANES), lambda b: (0, b))
    mat = pl.BlockSpec((None, LANES, LANES), lambda b: (b, 0, 0))
    return pl.pallas_call(
        body, grid=(nb,),
        in_specs=[col(0), col(nb), pl.BlockSpec((CONV_WIDTH, LANES), lambda b: (0, b)), vec, mat, vec, mat, vec, vec],
        out_specs=[col(0), col(0)],
        out_shape=[jax.ShapeDtypeStruct((t, D_A), F32), jax.ShapeDtypeStruct((t, D_A), F32)],
        scratch_shapes=[pltpu.VMEM((t + 8, LANES), F32), pltpu.VMEM((t, LANES), F32), pltpu.VMEM((t, LANES), F32)],
        compiler_params=pltpu.CompilerParams(dimension_semantics=("arbitrary",)), name="rglru_fwd")(
            z, z, cw, cb, wa, ba, wx, bx, lam)


def _rg_bwd(z, h, dout, cw, cb, wa, ba, wx, bx, lam):
    t = z.shape[0]
    cr = _tile(t, 256)
    nb = D_A // LANES

    def body(ax_ref, ag_ref, h_ref, do_ref, cw_ref, cb_ref, wa_ref, ba_ref, wx_ref, bx_ref, lam_ref,
             dax_ref, dag_ref, dcw_ref, dcb_ref, dwa_ref, dba_ref, dwx_ref, dbx_ref, dlam_ref,
             axp, hp, xa_s, a_s, g_s, dxa_s, u_s):
        zero8 = jnp.zeros((8, LANES), F32)
        axp[pl.ds(0, 8), :] = zero8
        hp[pl.ds(0, 8), :] = zero8
        dxa_s[pl.ds(t, 8), :] = zero8
        u_s[pl.ds(t, 8), :] = zero8
        pltpu.sync_copy(ax_ref, axp.at[pl.ds(8, t)])
        pltpu.sync_copy(h_ref, hp.at[pl.ds(8, t)])
        w = [cw_ref[pl.ds(k, 1), :] for k in range(CONV_WIDTH)]
        for ref in (dcw_ref, dcb_ref, dwa_ref, dba_ref, dwx_ref, dbx_ref, dlam_ref):
            ref[...] = jnp.zeros(ref.shape, F32)

        def p1(c, carry):
            t0 = pl.multiple_of(c * cr, cr)
            taps = _conv_taps(axp[pl.ds(t0, cr + 8), :], cr)
            xa = cb_ref[...] + w[3] * taps[0] + w[2] * taps[1] + w[1] * taps[2] + w[0] * taps[3]
            a, _ = _rg_local(xa, wa_ref[...], ba_ref[...], wx_ref[...], bx_ref[...], lam_ref[...])
            xa_s[pl.ds(t0, cr), :] = xa
            a_s[pl.ds(t0, cr), :] = a
            ag = ag_ref[pl.ds(t0, cr), :]
            dov = do_ref[pl.ds(t0, cr), :]
            gel, pull = jax.vjp(_gelu, ag)
            g_s[pl.ds(t0, cr), :] = dov * gel
            u_s[pl.ds(t0, cr), :] = a * (dov * gel)
            dag_ref[pl.ds(t0, cr), :] = pull(dov * h_ref[pl.ds(t0, cr), :])[0]
            return carry

        lax.fori_loop(0, t // cr, p1, 0)
        _scan8(a_s, u_s, u_s, t, reverse=True)

        def p3(c, carry):
            t0 = pl.multiple_of(c * cr, cr)
            g = g_s[pl.ds(t0, cr), :] + pltpu.roll(u_s[pl.ds(t0, cr + 8), :], cr + 7, 0)[:cr, :]
            h_prev = pltpu.roll(hp[pl.ds(t0, cr + 8), :], 1, 0)[8:, :]
            _, pull = jax.vjp(_rg_local, xa_s[pl.ds(t0, cr), :], wa_ref[...], ba_ref[...], wx_ref[...], bx_ref[...],
                              lam_ref[...])
            dxa, dwa, dba, dwx, dbx, dlam = pull((g * h_prev, g))
            dxa_s[pl.ds(t0, cr), :] = dxa
            dwa_ref[...] += dwa
            dba_ref[...] += dba
            dwx_ref[...] += dwx
            dbx_ref[...] += dbx
            dlam_ref[...] += dlam
            return carry

        lax.fori_loop(0, t // cr, p3, 0)

        def p4(c, carry):
            t0 = pl.multiple_of(c * cr, cr)
            ext = dxa_s[pl.ds(t0, cr + 8), :]
            n = cr + 8
            ahead = [ext[:cr, :]] + [pltpu.roll(ext, n - s, 0)[:cr, :] for s in (1, 2, 3)]
            dax_ref[pl.ds(t0, cr), :] = w[3] * ahead[0] + w[2] * ahead[1] + w[1] * ahead[2] + w[0] * ahead[3]
            taps = _conv_taps(axp[pl.ds(t0, cr + 8), :], cr)
            dxa = ahead[0]
            for k in range(CONV_WIDTH):
                dcw_ref[pl.ds(k, 1), :] += jnp.sum(dxa * taps[CONV_WIDTH - 1 - k], axis=0, keepdims=True)
            dcb_ref[...] += jnp.sum(dxa, axis=0, keepdims=True)
            return carry

        lax.fori_loop(0, t // cr, p4, 0)

    col = lambda off: pl.BlockSpec((t, LANES), lambda b: (0, off + b))
    vec = pl.BlockSpec((1, LANES), lambda b: (0, b))
    mat = pl.BlockSpec((None, LANES, LANES), lambda b: (b, 0, 0))
    cws = pl.BlockSpec((CONV_WIDTH, LANES), lambda b: (0, b))
    sds = jax.ShapeDtypeStruct
    return pl.pallas_call(
        body, grid=(nb,),
        in_specs=[col(0), col(nb), col(0), col(0), cws, vec, mat, vec, mat, vec, vec],
        out_specs=[col(0), col(0), cws, vec, mat, vec, mat, vec, vec],
        out_shape=[sds((t, D_A), F32), sds((t, D_A), F32), sds((CONV_WIDTH, D_A), F32), sds((1, D_A), F32),
                   sds((nb, LANES, LANES), F32), sds((1, D_A), F32), sds((nb, LANES, LANES), F32), sds((1, D_A), F32),
                   sds((1, D_A), F32)],
        scratch_shapes=[pltpu.VMEM((t + 8, LANES), F32), pltpu.VMEM((t + 8, LANES), F32), pltpu.VMEM((t, LANES), F32),
                        pltpu.VMEM((t, LANES), F32), pltpu.VMEM((t, LANES), F32), pltpu.VMEM((t + 8, LANES), F32),
                        pltpu.VMEM((t + 8, LANES), F32)],
        compiler_params=pltpu.CompilerParams(dimension_semantics=("arbitrary",)), name="rglru_bwd")(
            z, z, h, dout, cw, cb, wa, ba, wx, bx, lam)


def _fgate_fwd(z, bf):
    t = z.shape[0]

    def body(zf_ref, bf_ref, c_ref):
        c_ref[...] = -_softplus(-(zf_ref[...] + bf_ref[...]))
        _scan8(None, c_ref, c_ref, t)

    return pl.pallas_call(
        body, grid=(1,), in_specs=[pl.BlockSpec((t, LANES), lambda i: (0, Z_F // LANES)), pl.BlockSpec((1, LANES), lambda i: (0, 0))],
        out_specs=pl.BlockSpec((t, LANES), lambda i: (0, 0)), out_shape=jax.ShapeDtypeStruct((t, LANES), F32),
        compiler_params=pltpu.CompilerParams(dimension_semantics=("arbitrary",)), name="fgate_fwd")(z, bf)


def _fgate_bwd(z, bf, dc):
    t = z.shape[0]

    def body(zf_ref, bf_ref, dc_ref, dz_ref, db_ref):
        _scan8(None, dc_ref, dz_ref, t, reverse=True)
        dz = dz_ref[...] * _sig(-(zf_ref[...] + bf_ref[...]))
        dz_ref[...] = dz
        db_ref[...] = jnp.sum(dz, axis=0, keepdims=True)

    return pl.pallas_call(
        body, grid=(1,),
        in_specs=[pl.BlockSpec((t, LANES), lambda i: (0, Z_F // LANES)), pl.BlockSpec((1, LANES), lambda i: (0, 0)),
                  pl.BlockSpec((t, LANES), lambda i: (0, 0))],
        out_specs=[pl.BlockSpec((t, LANES), lambda i: (0, 0)), pl.BlockSpec((1, LANES), lambda i: (0, 0))],
        out_shape=[jax.ShapeDtypeStruct((t, LANES), F32), jax.ShapeDtypeStruct((1, LANES), F32)],
        compiler_params=pltpu.CompilerParams(dimension_semantics=("arbitrary",)), name="fgate_bwd")(z, bf, dc)


def _cast_rows(src_ref, dst_ref, t, rows, fn):
    def cp(c, carry):
        r0 = pl.multiple_of(c * rows, rows)
        dst_ref[pl.ds(r0, rows), :] = fn(src_ref[pl.ds(r0, rows), :]).astype(dst_ref.dtype)
        return carry

    lax.fori_loop(0, t // rows, cp, 0)


def _attn_groups(t):
    tq = _tile(t, 256)
    nq = t // tq
    grp = 4 if nq % 4 == 0 else 1
    return tq, nq, grp


def _attn_fwd(z, crow, carry=None):
    t = z.shape[0]
    tq, nq, grp = _attn_groups(t)
    tk = grp * tq
    scale = HEAD_DIM ** -0.5

    def body(q_ref, k_ref, v_ref, cr_ref, o_ref, lse_ref, kb_s, vb_s):
        lane = lax.broadcasted_iota(jnp.int32, (1, LANES), 1)
        hmask = [(lane // HEAD_DIM) == hh for hh in range(2)]
        _cast_rows(k_ref, kb_s, t, tq, lambda v: v)
        _cast_rows(v_ref, vb_s, t, tq, lambda v: v)

        def qblock(g, r):
            q0 = pl.multiple_of((g * grp + r) * tq, tq)
            qv = q_ref[pl.ds(q0, tq), :] * scale
            qa = [jnp.where(hmask[hh], qv, 0.0).astype(BF16) for hh in range(2)]

            def update(st, k0, width, off):
                kb = kb_s[pl.ds(k0, width), :]
                vb = vb_s[pl.ds(k0, width), :]
                new = []
                for hh in range(2):
                    m, l, acc = st[hh]
                    s = _dot(qa[hh], kb, NT) - cr_ref[hh, :, pl.ds(k0, width)]
                    if off is not None:
                        keep = (lax.broadcasted_iota(jnp.int32, (tq, width), 0) + off
                                >= lax.broadcasted_iota(jnp.int32, (tq, width), 1))
                        s = jnp.where(keep, s, NEG)
                    m_new = jnp.maximum(m, jnp.max(s, axis=-1, keepdims=True))
                    p = jnp.exp(s - m_new)
                    corr = jnp.exp(m - m_new)
                    new.append((m_new, corr * l + jnp.sum(p, axis=-1, keepdims=True), corr * acc + _dot(p, vb, NN)))
                return tuple(new)

            one = (jnp.full((tq, 1), NEG, F32), jnp.zeros((tq, 1), F32), jnp.zeros((tq, LANES), F32))
            st = lax.fori_loop(0, g, lambda j, st: update(st, pl.multiple_of(j * tk, tk), tk, None), (one, one))
            st = update(st, pl.multiple_of(g * tk, tk), (r + 1) * tq, r * tq)
            o_ref[pl.ds(q0, tq), :] = jnp.where(hmask[0], st[0][2] / st[0][1], st[1][2] / st[1][1])
            for hh in range(2):
                lse_ref[hh, pl.ds(q0, tq), :] = st[hh][0] + jnp.log(st[hh][1])

        def group(g, carry):
            for r in range(grp):
                qblock(g, r)
            return carry

        lax.fori_loop(0, nq // grp, group, 0)

    base = 2 * D_A // LANES
    nh = D_B // LANES
    col = lambda off: pl.BlockSpec((t, LANES), lambda p: (0, off + p))
    return _call(
        body, (nh,), [col(base), col(base + nh), col(base + 2 * nh), pl.BlockSpec((2, 1, t), lambda p: (p, 0, 0))],
        [col(0), pl.BlockSpec((2, t, 1), lambda p: (p, 0, 0))],
        [jax.ShapeDtypeStruct((t, D_B), F32), jax.ShapeDtypeStruct((N_HEADS, t, 1), F32)],
        [pltpu.VMEM((t, LANES), BF16), pltpu.VMEM((t, LANES), BF16)], ("parallel",), "attn_fwd", [z, z, z, crow], carry)


def _attn_bwd(z, crow, lse, do, carry=None):
    t = z.shape[0]
    tq, nq, grp = _attn_groups(t)
    tw = grp * tq
    scale = HEAD_DIM ** -0.5

    def body(q_ref, k_ref, v_ref, cr_ref, lse_ref, do_ref, dq_ref, dk_ref, dv_ref, dc_ref, qa_s, da_s, kb_s, vb_s, dl_s):
        lane = lax.broadcasted_iota(jnp.int32, (1, LANES), 1)
        hmask = [(lane // HEAD_DIM) == hh for hh in range(2)]
        _cast_rows(k_ref, kb_s, t, tq, lambda v: v)
        _cast_rows(v_ref, vb_s, t, tq, lambda v: v)
        for hh in range(2):
            _cast_rows(q_ref, qa_s.at[hh], t, tq, lambda v, hh=hh: jnp.where(hmask[hh], v * scale, 0.0))
            _cast_rows(do_ref, da_s.at[hh], t, tq, lambda v, hh=hh: jnp.where(hmask[hh], v, 0.0))
        _cast_rows(q_ref, dq_ref, t, tq, lambda v: jnp.zeros_like(v))

        def probs(hh, q0, nq_rows, k0, nk_rows, off):
            s = _dot(qa_s[hh, pl.ds(q0, nq_rows), :], kb_s[pl.ds(k0, nk_rows), :], NT) - cr_ref[hh, :, pl.ds(k0, nk_rows)]
            p = jnp.exp(s - lse_ref[hh, pl.ds(q0, nq_rows), :])
            if off is not None:
                keep = (lax.broadcasted_iota(jnp.int32, (nq_rows, nk_rows), 0) + off
                        >= lax.broadcasted_iota(jnp.int32, (nq_rows, nk_rows), 1))
                p = jnp.where(keep, p, 0.0)
            return p, _dot(da_s[hh, pl.ds(q0, nq_rows), :], vb_s[pl.ds(k0, nk_rows), :], NT)

        def delta(g, r):
            q0 = pl.multiple_of((g * grp + r) * tq, tq)

            def add(k0, width, off, acc):
                res = []
                for hh in range(2):
                    p, dp = probs(hh, q0, tq, k0, width, off)
                    res.append(acc[hh] + jnp.sum(p * dp, axis=-1, keepdims=True))
                return tuple(res)

            zcol = jnp.zeros((tq, 1), F32)
            acc = lax.fori_loop(0, g, lambda j, acc: add(pl.multiple_of(j * tw, tw), tw, None, acc), (zcol, zcol))
            acc = add(pl.multiple_of(g * tw, tw), (r + 1) * tq, r * tq, acc)
            for hh in range(2):
                dl_s[hh, pl.ds(q0, tq), :] = acc[hh]

        def delta_group(g, carry):
            for r in range(grp):
                delta(g, r)
            return carry

        lax.fori_loop(0, nq // grp, delta_group, 0)

        def kblock(g, r):
            k0 = pl.multiple_of((g * grp + r) * tq, tq)
            kb = kb_s[pl.ds(k0, tq), :]

            def upd(q0, height, off, st):
                dk, dv, dc = st[0], st[1], [st[2], st[3]]
                dqs = []
                for hh in range(2):
                    p, dp = probs(hh, q0, height, k0, tq, off)
                    ds = p * (dp - dl_s[hh, pl.ds(q0, height), :])
                    dv = dv + _dot(p, da_s[hh, pl.ds(q0, height), :], TN)
                    dk = dk + _dot(ds, qa_s[hh, pl.ds(q0, height), :], TN)
                    dqs.append(_dot(ds, kb, NN))
                    dc[hh] = dc[hh] - jnp.sum(ds, axis=0, keepdims=True)
                dq_ref[pl.ds(q0, height), :] += jnp.where(hmask[0], dqs[0], dqs[1]) * scale
                return dk, dv, dc[0], dc[1]

            zero = jnp.zeros((tq, LANES), F32)
            zrow = jnp.zeros((1, tq), F32)
            st = upd(k0, (grp - r) * tq, 0, (zero, zero, zrow, zrow))
            st = lax.fori_loop(g + 1, nq // grp, lambda i, st: upd(pl.multiple_of(i * tw, tw), tw, None, st), st)
            dk_ref[pl.ds(k0, tq), :] = st[0]
            dv_ref[pl.ds(k0, tq), :] = st[1]
            for hh in range(2):
                dc_ref[hh, :, pl.ds(k0, tq)] = st[2 + hh]

        def kgroup(g, carry):
            for r in range(grp):
                kblock(g, r)
            return carry

        lax.fori_loop(0, nq // grp, kgroup, 0)

    base = 2 * D_A // LANES
    nh = D_B // LANES
    col = lambda off: pl.BlockSpec((t, LANES), lambda p: (0, off + p))
    ccs = pl.BlockSpec((2, t, 1), lambda p: (p, 0, 0))
    crs = pl.BlockSpec((2, 1, t), lambda p: (p, 0, 0))
    return _call(
        body, (nh,), [col(base), col(base + nh), col(base + 2 * nh), crs, ccs, col(0)], [col(0), col(0), col(0), crs],
        [jax.ShapeDtypeStruct((t, D_B), F32)] * 3 + [jax.ShapeDtypeStruct((N_HEADS, 1, t), F32)],
        [pltpu.VMEM((2, t, LANES), BF16), pltpu.VMEM((2, t, LANES), BF16), pltpu.VMEM((t, LANES), BF16),
         pltpu.VMEM((t, LANES), BF16), pltpu.VMEM((2, t, 1), F32)], ("parallel",), "attn_bwd", [z, z, z, crow, lse, do], carry)


def _s5_disc(a_re, a_im, log_dt, b_re, b_im):
    dt = jnp.exp(log_dt)
    mag = jnp.exp(a_re * dt)
    ar = mag * jnp.cos(a_im * dt)
    ai = mag * jnp.sin(a_im * dt)
    den = a_re * a_re + a_im * a_im
    kr = ((ar - 1.0) * a_re + ai * a_im) / den
    ki = (ai * a_re - (ar - 1.0) * a_im) / den
    kr3, ki3 = kr[:, None, :], ki[:, None, :]
    return ar, ai, kr3 * b_re - ki3 * b_im, kr3 * b_im + ki3 * b_re


def _s5_prep(a_re, a_im, log_dt, b_re, b_im):
    g, p = a_re.shape
    gc = b_re.shape[1]

    def body(*refs):
        res = _s5_disc(*[r[...] for r in refs[:5]])
        for o_ref, v in zip(refs[5:], res):
            o_ref[...] = v

    sds = jax.ShapeDtypeStruct
    return pl.pallas_call(body, out_shape=[sds((g, p), F32), sds((g, p), F32), sds((g, gc, p), F32), sds((g, gc, p), F32)],
                          name="s5_prep")(a_re, a_im, log_dt, b_re, b_im)


def _s5_prep_bwd(a_re, a_im, log_dt, b_re, b_im, d_ar, d_ai, d_br, d_bi):
    ins = (a_re, a_im, log_dt, b_re, b_im)

    def body(*refs):
        vals = [r[...] for r in refs[:5]]
        cts = tuple(r[...] for r in refs[5:9])
        _, pull = jax.vjp(_s5_disc, *vals)
        for o_ref, v in zip(refs[9:], pull(cts)):
            o_ref[...] = v

    return pl.pallas_call(body, out_shape=[jax.ShapeDtypeStruct(a.shape, F32) for a in ins], name="s5_prep_bwd")(
        *ins, d_ar, d_ai, d_br, d_bi)


def _s5_scan_rows(t, ar, ai, hr_s, hi_s, off, reverse):
    n = ar.shape[1]
    if reverse:
        ai = -ai
    sub = lax.broadcasted_iota(jnp.int32, (8, n), 0)
    cmul = lambda xr, xi, yr, yi: (xr * yr - xi * yi, xr * yi + xi * yr)
    pw = [(ar, ai)]
    for _ in range(7):
        pw.append(cmul(*pw[-1], ar, ai))
    pr = jnp.zeros((8, n), F32)
    pi = jnp.zeros((8, n), F32)
    for r in range(8):
        k = 7 - r if reverse else r
        pr = jnp.where(sub == r, pw[k][0], pr)
        pi = jnp.where(sub == r, pw[k][1], pi)

    def step(g, carry):
        cr, ci = carry
        r0 = pl.multiple_of(off + (t // 8 - 1 - g if reverse else g) * 8, 8)
        br = hr_s[pl.ds(r0, 8), :]
        bi = hi_s[pl.ds(r0, 8), :]
        for s in (1, 2, 4):
            ok = (sub < 8 - s) if reverse else (sub >= s)
            shift = 8 - s if reverse else s
            sr = jnp.where(ok, pltpu.roll(br, shift, 0), 0.0)
            si = jnp.where(ok, pltpu.roll(bi, shift, 0), 0.0)
            mr, mi = cmul(pw[s - 1][0], pw[s - 1][1], sr, si)
            br, bi = br + mr, bi + mi
        mr, mi = cmul(pr, pi, cr, ci)
        br, bi = br + mr, bi + mi
        hr_s[pl.ds(r0, 8), :] = br
        hi_s[pl.ds(r0, 8), :] = bi
        edge = sub == (0 if reverse else 7)
        return (jnp.sum(jnp.where(edge, br, 0.0), axis=0, keepdims=True),
                jnp.sum(jnp.where(edge, bi, 0.0), axis=0, keepdims=True))

    zero = jnp.zeros((1, n), F32)
    lax.fori_loop(0, t // 8, step, (zero, zero))


def _s5_fwd(z, bd_re, bd_im, ab_re, ab_im, cd_re, cd_im, dvec, carry=None):
    t = z.shape[0]
    cr = _tile(t, 256)
    ns = N_STATE // 2

    def body(u_ref, br_ref, bi_ref, ar_ref, ai_ref, cre_ref, cim_ref, d_ref, y_ref, hr_s, hi_s):
        def p1(c, carry):
            t0 = pl.multiple_of(c * cr, cr)
            u = u_ref[pl.ds(t0, cr), :]
            hr_s[pl.ds(t0, cr), :] = _dot(u, br_ref[...], NN)
            hi_s[pl.ds(t0, cr), :] = _dot(u, bi_ref[...], NN)
            return carry

        lax.fori_loop(0, t // cr, p1, 0)
        _s5_scan_rows(t, ar_ref[...], ai_ref[...], hr_s, hi_s, 0, False)

        def p3(c, carry):
            t0 = pl.multiple_of(c * cr, cr)
            y_ref[pl.ds(t0, cr), :] = (_dot(hr_s[pl.ds(t0, cr), :], cre_ref[...], NN)
                                       - _dot(hi_s[pl.ds(t0, cr), :], cim_ref[...], NN)
                                       + d_ref[...] * u_ref[pl.ds(t0, cr), :])
            return carry

        lax.fori_loop(0, t // cr, p3, 0)

    blk = lambda r, c: pl.BlockSpec((None, r, c), lambda b: (b, 0, 0))
    return _call(
        body, (2,),
        [pl.BlockSpec((t, LANES), lambda b: (0, Z_U // LANES + b)), blk(LANES, ns), blk(LANES, ns), blk(1, ns),
         blk(1, ns), blk(ns, LANES), blk(ns, LANES), pl.BlockSpec((1, LANES), lambda b: (0, b))],
        [pl.BlockSpec((t, LANES), lambda b: (0, b))], [jax.ShapeDtypeStruct((t, D_C), F32)],
        [pltpu.VMEM((t, ns), F32), pltpu.VMEM((t, ns), F32)], ("arbitrary",), "s5_fwd",
        [z, bd_re, bd_im, ab_re, ab_im, cd_re, cd_im, dvec], carry)


def _s5_bwd(z, dy, bd_re, bd_im, ab_re, ab_im, cd_re, cd_im, dvec):
    t = z.shape[0]
    cr = _tile(t, 256)
    ns = N_STATE // 2

    def body(u_ref, dy_ref, br_ref, bi_ref, ar_ref, ai_ref, cre_ref, cim_ref, d_ref,
             du_ref, dbr_ref, dbi_ref, dar_ref, dai_ref, dcre_ref, dcim_ref, dd_ref, hr_s, hi_s, gr_s, gi_s):
        zero8 = jnp.zeros((8, ns), F32)
        hr_s[pl.ds(0, 8), :] = zero8
        hi_s[pl.ds(0, 8), :] = zero8
        for ref in (dbr_ref, dbi_ref, dar_ref, dai_ref, dcre_ref, dcim_ref, dd_ref):
            ref[...] = jnp.zeros(ref.shape, F32)

        def p1(c, carry):
            t0 = pl.multiple_of(c * cr, cr)
            u = u_ref[pl.ds(t0, cr), :]
            hr_s[pl.ds(t0 + 8, cr), :] = _dot(u, br_ref[...], NN)
            hi_s[pl.ds(t0 + 8, cr), :] = _dot(u, bi_ref[...], NN)
            return carry

        lax.fori_loop(0, t // cr, p1, 0)
        _s5_scan_rows(t, ar_ref[...], ai_ref[...], hr_s, hi_s, 8, False)

        def p3(c, carry):
            t0 = pl.multiple_of(c * cr, cr)
            dyv = dy_ref[pl.ds(t0, cr), :]
            u = u_ref[pl.ds(t0, cr), :]
            gr_s[pl.ds(t0, cr), :] = _dot(dyv, cre_ref[...], NT)
            gi_s[pl.ds(t0, cr), :] = -_dot(dyv, cim_ref[...], NT)
            dcre_ref[...] += _dot(hr_s[pl.ds(t0 + 8, cr), :], dyv, TN)
            dcim_ref[...] -= _dot(hi_s[pl.ds(t0 + 8, cr), :], dyv, TN)
            dd_ref[...] += jnp.sum(dyv * u, axis=0, keepdims=True)
            du_ref[pl.ds(t0, cr), :] = dyv * d_ref[...]
            return carry

        lax.fori_loop(0, t // cr, p3, 0)
        _s5_scan_rows(t, ar_ref[...], ai_ref[...], gr_s, gi_s, 0, True)

        def p5(c, carry):
            t0 = pl.multiple_of(c * cr, cr)
            u = u_ref[pl.ds(t0, cr), :]
            gr = gr_s[pl.ds(t0, cr), :]
            gi = gi_s[pl.ds(t0, cr), :]
            dbr_ref[...] += _dot(u, gr, TN)
            dbi_ref[...] += _dot(u, gi, TN)
            du_ref[pl.ds(t0, cr), :] += _dot(gr, br_ref[...], NT) + _dot(gi, bi_ref[...], NT)
            hpr = pltpu.roll(hr_s[pl.ds(t0, cr + 8), :], 1, 0)[8:, :]
            hpi = pltpu.roll(hi_s[pl.ds(t0, cr + 8), :], 1, 0)[8:, :]
            dar_ref[...] += jnp.sum(gr * hpr + gi * hpi, axis=0, keepdims=True)
            dai_ref[...] += jnp.sum(gi * hpr - gr * hpi, axis=0, keepdims=True)
            return carry

        lax.fori_loop(0, t // cr, p5, 0)

    blk = lambda r, c: pl.BlockSpec((None, r, c), lambda b: (b, 0, 0))
    ucol = pl.BlockSpec((t, LANES), lambda b: (0, Z_U // LANES + b))
    ycol = pl.BlockSpec((t, LANES), lambda b: (0, b))
    dsp = pl.BlockSpec((1, LANES), lambda b: (0, b))
    sds = jax.ShapeDtypeStruct
    return pl.pallas_call(
        body, grid=(2,),
        in_specs=[ucol, ycol, blk(LANES, ns), blk(LANES, ns), blk(1, ns), blk(1, ns), blk(ns, LANES), blk(ns, LANES), dsp],
        out_specs=[ycol, blk(LANES, ns), blk(LANES, ns), blk(1, ns), blk(1, ns), blk(ns, LANES), blk(ns, LANES), dsp],
        out_shape=[sds((t, D_C), F32), sds((2, LANES, ns), F32), sds((2, LANES, ns), F32), sds((2, 1, ns), F32),
                   sds((2, 1, ns), F32), sds((2, ns, LANES), F32), sds((2, ns, LANES), F32), sds((1, D_C), F32)],
        scratch_shapes=[pltpu.VMEM((t + 8, ns), F32), pltpu.VMEM((t + 8, ns), F32), pltpu.VMEM((t, ns), F32),
                        pltpu.VMEM((t, ns), F32)],
        compiler_params=pltpu.CompilerParams(dimension_semantics=("arbitrary",)), name="s5_bwd")(
            z, dy, bd_re, bd_im, ab_re, ab_im, cd_re, cd_im, dvec)


def _mix_out(out_a, out_b, yc, x1, ga, gb, gc, wglu, wout, ln_g, ln_b):
    yg = _gelu(yc)
    out_c = yg * _sig(_bdot(yg, wglu))
    o = jnp.concatenate([_rms(out_a, ga), _rms(out_b, gb), _rms(out_c, gc)], axis=-1)
    return (_ln(ALPHA * x1 + _bdot(o, wout), ln_g, ln_b),)


def _ln_only(pre, g, b):
    return (_ln(pre, g, b),)


def _loss_head(y, target, tm):
    t, d = y.shape

    def body(y_ref, t_ref, dy_ref, l_ref):
        i = pl.program_id(0)
        e = y_ref[...] - t_ref[...]
        dy_ref[...] = e * (1.0 / d)
        part = 0.5 * jnp.sum(jnp.sum(e * e, axis=-1, keepdims=True) * (1.0 / d), axis=0, keepdims=True)
        row = jnp.where(lax.broadcasted_iota(jnp.int32, (1, LANES), 1) == 0, part, 0.0)

        @pl.when(i == 0)
        def _():
            l_ref[...] = row

        @pl.when(i > 0)
        def _():
            l_ref[...] += row

    spec = pl.BlockSpec((tm, d), lambda i: (i, 0))
    return pl.pallas_call(
        body, grid=(t // tm,), in_specs=[spec, spec], out_specs=[spec, pl.BlockSpec((1, LANES), lambda i: (0, 0))],
        out_shape=[jax.ShapeDtypeStruct((t, d), F32), jax.ShapeDtypeStruct((1, LANES), F32)],
        compiler_params=pltpu.CompilerParams(dimension_semantics=("arbitrary",)), name="loss_head")(y, target)


def _adamw(w, g, m, v, name):
    r, c = w.shape
    tr = r
    for cand in (512, 256, 352, 128):
        if r % cand == 0:
            tr = cand
            break

    def body(w_ref, g_ref, m_ref, v_ref, d_ref, nm_ref, nv_ref):
        gv = g_ref[...]
        mn = ADAM_B1 * m_ref[...] + (1.0 - ADAM_B1) * gv
        vn = ADAM_B2 * v_ref[...] + (1.0 - ADAM_B2) * (gv * gv)
        m_hat = mn / (1.0 - ADAM_B1 ** ADAM_STEP)
        v_hat = vn / (1.0 - ADAM_B2 ** ADAM_STEP)
        d_ref[...] = -ADAM_LR * (m_hat / (jnp.sqrt(v_hat) + ADAM_EPS) + ADAM_WD * w_ref[...])
        nm_ref[...] = mn
        nv_ref[...] = vn

    spec = pl.BlockSpec((tr, c), lambda i: (i, 0))
    return pl.pallas_call(
        body, grid=(r // tr,), in_specs=[spec] * 4, out_specs=[spec] * 3,
        out_shape=[jax.ShapeDtypeStruct((r, c), F32)] * 3,
        compiler_params=pltpu.CompilerParams(dimension_semantics=("parallel",)), name=name)(w, g, m, v)


def _row_tile(r):
    for cand in (512, 448, 352, 256, 128):
        if r % cand == 0:
            return cand
    return r


def _pair_add(a, b, idx, name, out_dtype):
    _, r, w = a.shape
    tr = _row_tile(r)

    def body(i_ref, a_ref, b_ref, o_ref):
        o_ref[...] = (a_ref[...].astype(F32) + b_ref[...].astype(F32)).astype(o_ref.dtype)

    grid_spec = pltpu.PrefetchScalarGridSpec(
        num_scalar_prefetch=1, grid=(4, r // tr),
        in_specs=[pl.BlockSpec((None, tr, w), lambda q, i, s: (2 * q + s[0], i, 0)),
                  pl.BlockSpec((None, tr, w), lambda q, i, s: (q, i, 0))],
        out_specs=pl.BlockSpec((None, tr, w), lambda q, i, s: (q, i, 0)))
    return pl.pallas_call(body, grid_spec=grid_spec, out_shape=jax.ShapeDtypeStruct((4, r, w), out_dtype),
                          compiler_params=pltpu.CompilerParams(dimension_semantics=("parallel", "parallel")), name=name)(
                              idx, a, b)


def _quad_add(p, rb, idx, name):
    _, r, w = p.shape
    tr = _row_tile(r)

    def body(i_ref, p_ref, r0, r1, r2, o_ref):
        o_ref[...] = ((p_ref[...].astype(F32) + r0[...].astype(F32)) + r1[...].astype(F32)) + r2[...].astype(F32)

    grid_spec = pltpu.PrefetchScalarGridSpec(
        num_scalar_prefetch=1, grid=(r // tr,),
        in_specs=[pl.BlockSpec((None, tr, w), lambda i, s: (s[0], i, 0))]
        + [pl.BlockSpec((None, tr, w), functools.partial(lambda i, s, k: (k, i, 0), k=k)) for k in range(3)],
        out_specs=pl.BlockSpec((tr, w), lambda i, s: (i, 0)))
    return pl.pallas_call(body, grid_spec=grid_spec, out_shape=jax.ShapeDtypeStruct((r, w), F32),
                          compiler_params=pltpu.CompilerParams(dimension_semantics=("parallel",)), name=name)(
                              idx, p, rb, rb, rb)


def _gather_small(buf):
    def plan(x, y, c, ins, outs):
        dst = outs[0].at[4 * x + 2 * y + c]
        res = [(ins[0], dst, None)]
        for rel in range(1, 8):
            res.append((ins[0], dst, (x ^ (rel >> 2), y ^ ((rel >> 1) & 1), c ^ (rel & 1))))
        return res

    return _Carry([buf], [jax.ShapeDtypeStruct((8,) + buf.shape, buf.dtype)], {}, plan, 8)


def _sum_small(allb):
    _, r, w = allb.shape

    def body(a_ref, o_ref):
        s = a_ref[0]
        for k in range(1, 8):
            s = s + a_ref[k]
        o_ref[...] = s

    return pl.pallas_call(body, out_shape=jax.ShapeDtypeStruct((r, w), F32), name="ar_small_sum")(allb)


def _pad_rows(a, rows):
    return jnp.pad(a, ((0, rows - a.shape[0]), (0, 0)))


def _pack_small(arrs):
    rows, tail = [], []
    for a in arrs:
        if not tail and a.size % LANES == 0:
            rows.append(a.reshape(-1, LANES))
        else:
            tail.append(a.reshape(-1))
    n_rows = sum(r.shape[0] for r in rows)
    n_tail = sum(int(v.size) for v in tail)
    tail_rows = -(-n_tail // LANES)
    total_rows = n_rows + tail_rows + (-(n_rows + tail_rows)) % 8
    if tail:
        tail.append(jnp.zeros((tail_rows * LANES - n_tail,), F32))
        rows.append(jnp.concatenate(tail).reshape(tail_rows, LANES))
    if total_rows > n_rows + tail_rows:
        rows.append(jnp.zeros((total_rows - n_rows - tail_rows, LANES), F32))
    return jnp.concatenate(rows, axis=0)


def _unpack_small(buf, shapes):
    out, off = [], 0
    flat = None
    for s in shapes:
        n = int(np.prod(s))
        if off % LANES == 0 and n % LANES == 0:
            out.append(buf[off // LANES:(off + n) // LANES].reshape(s))
        else:
            flat = buf.reshape(-1) if flat is None else flat
            out.append(flat[off:off + n].reshape(s))
        off += n
    return out


def _block_diag(blocks, nb):
    m, r, c = blocks.shape
    n = m // nb
    eye = jnp.eye(n, dtype=blocks.dtype)
    return (blocks.reshape(nb, n, r, 1, c) * eye[None, :, None, :, None]).reshape(nb, n * r, n * c)


def _diag_blocks(dense, n):
    nb = dense.shape[0]
    r, c = dense.shape[1] // n, dense.shape[2] // n
    eye = jnp.eye(n, dtype=dense.dtype)
    return jnp.sum(dense.reshape(nb, n, r, n, c) * eye[None, :, None, :, None], axis=3).reshape(nb * n, r, c)


def kernel(x, ffn1_w_gate, ffn1_w_up, ffn1_w_down, ln1_g, ln1_b, w_in, conv_w, conv_b, rg_w_a, rg_b_a, rg_w_x, rg_b_x, rg_lambda, fox_b_f, s5_a_re, s5_a_im, s5_log_dt, s5_b_re, s5_b_im, s5_c_re, s5_c_im, s5_d, s5_w_glu, mix_norm_g, w_out, ln2_g, ln2_b, ffn2_w_gate, ffn2_w_up, ffn2_w_down, ln3_g, ln3_b, loss_target, m_ffn1_w_gate, m_ffn1_w_up, m_ffn1_w_down, m_ln1_g, m_ln1_b, m_w_in, m_conv_w, m_conv_b, m_rg_w_a, m_rg_b_a, m_rg_w_x, m_rg_b_x, m_rg_lambda, m_fox_b_f, m_s5_a_re, m_s5_a_im, m_s5_log_dt, m_s5_b_re, m_s5_b_im, m_s5_c_re, m_s5_c_im, m_s5_d, m_s5_w_glu, m_mix_norm_g, m_w_out, m_ln2_g, m_ln2_b, m_ffn2_w_gate, m_ffn2_w_up, m_ffn2_w_down, m_ln3_g, m_ln3_b, v_ffn1_w_gate, v_ffn1_w_up, v_ffn1_w_down, v_ln1_g, v_ln1_b, v_w_in, v_conv_w, v_conv_b, v_rg_w_a, v_rg_b_a, v_rg_w_x, v_rg_b_x, v_rg_lambda, v_fox_b_f, v_s5_a_re, v_s5_a_im, v_s5_log_dt, v_s5_b_re, v_s5_b_im, v_s5_c_re, v_s5_c_im, v_s5_d, v_s5_w_glu, v_mix_norm_g, v_w_out, v_ln2_g, v_ln2_b, v_ffn2_w_gate, v_ffn2_w_up, v_ffn2_w_down, v_ln3_g, v_ln3_b):
    a = dict(locals())
    w = {n: a[n] for n in WEIGHTS}
    t, d = x.shape[1], x.shape[2]
    f = ffn1_w_down.shape[1] * 8
    fs, ds = f // 8, d // 8
    mx, my, mc = lax.axis_index("x"), lax.axis_index("y"), lax.axis_index("c")
    me = 4 * mx + 2 * my + mc
    tm = _tile(t, 512)
    tf = f // 2
    win_rows = ds * Z_W // d

    FFN1, MIXW, FFN2 = ['g1', 'u1', 'd1'], ['win', 'wout', 'glu', 'conv'], ['g2', 'u2', 'd2']
    glu_rows = D_C * D_C // (8 * d)

    def shard_segs(l):
        wi = w['w_in'][l]
        win_p = jnp.concatenate([wi[:, :Z_F + N_HEADS], jnp.zeros((ds, Z_U - Z_F - N_HEADS), F32), wi[:, Z_F + N_HEADS:]], axis=1)
        conv_bits = lax.bitcast_convert_type(w['conv_w'][l], BF16).reshape(1, -1)
        segs = dict(g1=w['ffn1_w_gate'][l].T, u1=w['ffn1_w_up'][l].T, d1=w['ffn1_w_down'][l],
                    g2=w['ffn2_w_gate'][l].T, u2=w['ffn2_w_up'][l].T, d2=w['ffn2_w_down'][l],
                    win=win_p.reshape(win_rows, d), wout=w['w_out'][l], glu=_pad_rows(w['s5_w_glu'][l].reshape(-1, d), 16))
        segs = {k: v.astype(BF16) for k, v in segs.items()}
        segs['conv'] = _pad_rows(jnp.pad(conv_bits, ((0, 0), (0, d - conv_bits.shape[1]))), 16)
        return segs

    shards = [shard_segs(l) for l in range(DEPTH)]
    wts = {}

    def cat(keys):
        return [shards[l][k] for l, k in keys]

    def split(gs, keys):
        for (l, k), g in zip(keys, gs):
            wts[(l, k)] = g

    grp_a = [(0, k) for k in FFN1]
    grp_b = [(0, k) for k in MIXW]
    grp_c = [(0, k) for k in FFN2]
    grp_d = [(1, k) for k in FFN1]
    grp_e = [(1, k) for k in FFN2]
    grp_f = [(1, k) for k in MIXW]
    split(_run(_ag_sibling(_run(_ag_chips(cat(grp_a)), "ag_chips")), "ag_sibling"), grp_a)

    xs = x[0]
    saved = []
    cur = xs
    for l in range(DEPTH):
        row = lambda n: w[n][l].reshape(1, -1)
        ffn = lambda keys: tuple(wts[(l, k)].reshape(f, d) for k in keys)
        wa = _block_diag(w['rg_w_a'][l], 3)
        wx = _block_diag(w['rg_w_x'][l], 3)
        bf = jnp.pad(row('fox_b_f'), ((0, 0), (0, LANES - N_HEADS)))
        s5p = (w['s5_a_re'][l], w['s5_a_im'][l], w['s5_log_dt'][l].reshape(-1, 1),
               w['s5_b_re'][l].transpose(0, 2, 1), w['s5_b_im'][l].transpose(0, 2, 1))
        ab_re, ab_im, bb_re, bb_im = _s5_prep(*s5p)
        bd_re, bd_im = _block_diag(bb_re, 2), _block_diag(bb_im, 2)
        cd_re = _block_diag(w['s5_c_re'][l].transpose(0, 2, 1), 2)
        cd_im = _block_diag(w['s5_c_im'][l].transpose(0, 2, 1), 2)
        abr, abi = ab_re.reshape(2, 1, N_STATE // 2), ab_im.reshape(2, 1, N_STATE // 2)
        gm = row('mix_norm_g')
        ga, gb, gc = gm[:, :D_A], gm[:, D_A:D_A + D_B], gm[:, D_A + D_B:]

        x0 = cur
        ffn1 = ffn(FFN1)
        (x1, pre1, gs1, us1), cres = _ffn_fwd(x0, *ffn1, row('ln1_g'), row('ln1_b'), tm, tf,
                                              carry=_ag_chips(cat(grp_b if l == 0 else grp_e)))
        if l == 0:
            split(_run(_ag_sibling(cres), "ag_sibling"), grp_b)
        else:
            g_e = cres
        win = wts[(l, 'win')].reshape(d, Z_W)
        wout = wts[(l, 'wout')].reshape(d, d).astype(F32)
        wglu = wts[(l, 'glu')][:, :glu_rows].reshape(D_C, D_C).astype(F32)
        conv_full = lax.bitcast_convert_type(
            wts[(l, 'conv')][:, 0, :2 * CONV_WIDTH * D_A // 8].reshape(8, CONV_WIDTH, D_A // 8, 2), F32)
        conv_full = conv_full.transpose(1, 0, 2).reshape(CONV_WIDTH, D_A)
        z = _mm(x1, win, 'nn', F32, tm, Z_W, d, "mix_in")
        out_a, h_a = _rg_fwd(z, conv_full, row('conv_b'), wa, row('rg_b_a'), wx, row('rg_b_x'), row('rg_lambda'))
        cs = _fgate_fwd(z, bf)
        crow = cs[:, :N_HEADS].T.reshape(N_HEADS, 1, t)
        (out_b, lse), cres = _attn_fwd(z, crow, carry=_ag_chips(cat(grp_c)) if l == 0 else _ag_sibling(g_e))
        if l == 0:
            (yc,), cres = _s5_fwd(z, bd_re, bd_im, abr, abi, cd_re, cd_im, row('s5_d'),
                                  carry=_join(_ag_sibling(cres), _ag_chips(cat(grp_f))))
            split(cres[:len(grp_c)], grp_c)
            g_f = cres[len(grp_c):]
        else:
            split(cres, grp_e)
            (yc,), _ = _s5_fwd(z, bd_re, bd_im, abr, abi, cd_re, cd_im, row('s5_d'))
        mix_params = [ga, gb, gc, wglu, wout, row('ln2_g'), row('ln2_b')]
        (x2,) = _rowwise(_mix_out, [out_a, out_b, yc, x1], mix_params, [(d, F32)], tm, "mix_out")
        ffn2 = ffn(FFN2)
        (x3, pre3, gs2, us2), cres = _ffn_fwd(x2, *ffn2, row('ln3_g'), row('ln3_b'), tm, tf,
                                              carry=_join(_ag_chips(cat(grp_d)), _ag_sibling(g_f)) if l == 0 else None)
        if l == 0:
            split(cres[len(grp_d):], grp_f)
            split(_run(_ag_sibling(cres[:len(grp_d)]), "ag_sibling"), grp_d)
        saved.append(dict(x0=x0, x1=x1, pre1=pre1, gs1=gs1, us1=us1, z=z, out_a=out_a, h_a=h_a, crow=crow,
                          out_b=out_b, lse=lse, yc=yc, x2=x2, pre3=pre3, gs2=gs2, us2=us2, mix_params=mix_params,
                          ffn1=ffn1, ffn2=ffn2, win=win, conv_full=conv_full, wa=wa, wx=wx, bf=bf, s5p=s5p,
                          s5m=(bd_re, bd_im, abr, abi, cd_re, cd_im)))
        cur = x3

    dy, loss_row = _loss_head(cur, loss_target[0], tm)
    loss = lax.psum(loss_row[0, 0], ("x", "y", "c"))

    assert DEPTH == 2
    small_grads = {}
    small_names = ['conv_w'] + SMALL
    c_idx = jnp.reshape(mc, (1,)).astype(jnp.int32)
    chip_idx = jnp.reshape(2 * mx + my, (1,)).astype(jnp.int32)

    def blocks(arrs):
        return jnp.concatenate([v.astype(BF16).reshape(8, -1, d) for v in arrs], axis=1)

    def mixer_blocks(dwin, dwout, dwglu):
        glu = jnp.pad(dwglu.astype(BF16).reshape(8, glu_rows, d), ((0, 0), (0, 32 - glu_rows), (0, 0)))
        return jnp.concatenate([dwin.reshape(8, win_rows, d), dwout.astype(BF16).reshape(8, ds, d), glu], axis=1)

    def pair(full, ra):
        return _pair_add(full, ra, c_idx, "rs_add_sibling", BF16)

    for l in reversed(range(DEPTH)):
        s = saved[l]
        row = lambda n: w[n][l].reshape(1, -1)
        wg_tiles = dict(tm=tf, tn=d, tk=_tile(t, 1024))
        first = l == 0

        def ffn_back(dyv, pre, gs, us, wts3, xin, ln_g, ln_b, carry_fn=None, pipeline=None):
            (dpre,), (dlg, dlb), _ = _rowwise_vjp(_ln_only, [pre], [ln_g, ln_b], [dyv], tm, "ln_bwd")
            (dx, dg, du, hh), cres = _ffn_bwd(dpre, gs, us, *wts3, tm, tf, carry=carry_fn(dlg, dlb) if carry_fn else None)
            if pipeline is None:
                dwg = _mm(dg, xin, 'tn', BF16, name="ffn_dw_gate", **wg_tiles)
                dwu = _mm(du, xin, 'tn', BF16, name="ffn_dw_up", **wg_tiles)
                dwd = _mm(hh, dpre, 'tn', BF16, name="ffn_dw_down", **wg_tiles)
                return dx, (dwg, dwu, dwd), dlg, dlb, cres

            def front(dw):
                full = dw.reshape(8, fs, d)
                (ra,) = _run(_rs_sibling(full), "rs_sibling")
                return pair(full, ra)

            dwg, (rb_first,) = _mm(dg, xin, 'tn', BF16, name="ffn_dw_gate", carry=_rs_chips(pipeline), **wg_tiles)
            cres = list(cres) + [rb_first]
            p_g = front(dwg)
            dwu, (rb_g,) = _mm(du, xin, 'tn', BF16, name="ffn_dw_up", carry=_rs_chips(p_g), **wg_tiles)
            p_u = front(dwu)
            dwd, (rb_u,) = _mm(hh, dpre, 'tn', BF16, name="ffn_dw_down", carry=_rs_chips(p_u), **wg_tiles)
            p_d = front(dwd)
            (rb_d,) = _run(_rs_chips(p_d), "rs_chips")
            return dx, ((p_g, rb_g), (p_u, rb_u), (p_d, rb_d)), dlg, dlb, cres

        dx2, (dwg2, dwu2, dwd2), dl3g, dl3b, cres = ffn_back(
            dy, s['pre3'], s['gs2'], s['us2'], s['ffn2'], s['x2'], row('ln3_g'), row('ln3_b'),
            (lambda *_: _rs_sibling(full_l1)) if first else None)
        if first:
            part_l1 = pair(full_l1, cres[0])
            full_c2 = blocks([dwg2, dwu2, dwd2])
        (d_oa, d_ob, d_yc, d_x1), (dga, dgb, dgc, dwglu, dwout, dl2g, dl2b), cres = _rowwise_vjp(
            _mix_out, [s['out_a'], s['out_b'], s['yc'], s['x1']], s['mix_params'], [dx2], tm, "mix_out_bwd",
            carry=_rs_sibling(full_c2) if first else None)
        if first:
            part_c2 = pair(full_c2, cres[0])
        (d_ax, d_ag, dcw, dcb, dwa, dba, dwx, dbx, dlam) = _rg_bwd(
            s['z'], s['h_a'], d_oa, s['conv_full'], row('conv_b'), s['wa'], row('rg_b_a'), s['wx'], row('rg_b_x'), row('rg_lambda'))
        (dq, dk, dv, dcrow), cres = _attn_bwd(s['z'], s['crow'], s['lse'], d_ob,
                                              carry=_join(_rs_chips(part_l1), _rs_chips(part_c2)) if first else None)
        if first:
            rb_l1, rb_c2 = cres
        dc_pad = jnp.pad(dcrow.reshape(N_HEADS, t).T, ((0, 0), (0, LANES - N_HEADS)))
        dzf, dbf = _fgate_bwd(s['z'], s['bf'], dc_pad)
        du_c, dbd_re, dbd_im, dabr, dabi, dcd_re, dcd_im, dd = _s5_bwd(s['z'], d_yc, *s['s5m'], row('s5_d'))
        dz = jnp.concatenate([d_ax, d_ag, dq, dk, dv, dzf, du_c], axis=1)
        dx1 = _mm(dz, s['win'], 'nt', F32, tm, d, Z_W, "mix_in_dx", add=d_x1)
        dwin = _mm(s['x1'], dz, 'tn', BF16, d, Z_W, tm, "mix_in_dw")
        dbb_re, dbb_im = _diag_blocks(dbd_re, N_GROUPS // 2), _diag_blocks(dbd_im, N_GROUPS // 2)
        dcm_re, dcm_im = _diag_blocks(dcd_re, N_GROUPS // 2), _diag_blocks(dcd_im, N_GROUPS // 2)
        da_re, da_im, dlog_dt, db_re, db_im = _s5_prep_bwd(*s['s5p'], dabr.reshape(N_GROUPS, C_STATE), dabi.reshape(N_GROUPS, C_STATE), dbb_re, dbb_im)
        sg = dict(conv_w=dcw, conv_b=dcb, rg_w_a=_diag_blocks(dwa, 2), rg_b_a=dba,
                  rg_w_x=_diag_blocks(dwx, 2), rg_b_x=dbx, rg_lambda=dlam, fox_b_f=dbf[:, :N_HEADS],
                  s5_a_re=da_re, s5_a_im=da_im, s5_log_dt=dlog_dt, s5_b_re=db_re.transpose(0, 2, 1), s5_b_im=db_im.transpose(0, 2, 1),
                  s5_c_re=dcm_re.transpose(0, 2, 1), s5_c_im=dcm_im.transpose(0, 2, 1), s5_d=dd,
                  mix_norm_g=jnp.concatenate([dga, dgb, dgc], axis=1), ln2_g=dl2g, ln2_b=dl2b, ln3_g=dl3g, ln3_b=dl3b)
        small_grads[l] = sg

        if first:
            full_m = mixer_blocks(dwin, dwout, dwglu)
            (ra_m,) = _run(_rs_sibling(full_m), "rs_sibling")
            part_m = pair(full_m, ra_m)

            def last_carry(dlg, dlb):
                sg.update(ln1_g=dlg, ln1_b=dlb)
                packed = _pack_small([small_grads[ll][n] for n in small_names for ll in range(DEPTH)])
                return _gather_small(packed)

            dx0, tail, _, _, (all_small, rb_m) = ffn_back(dx1, s['pre1'], s['gs1'], s['us1'], s['ffn1'], s['x0'], row('ln1_g'),
                                                          row('ln1_b'), last_carry, pipeline=part_m)
        else:
            dx0, (dwg1, dwu1, dwd1), dl1g, dl1b, _ = ffn_back(dx1, s['pre1'], s['gs1'], s['us1'], s['ffn1'], s['x0'],
                                                              row('ln1_g'), row('ln1_b'))
            sg.update(ln1_g=dl1g, ln1_b=dl1b)
            full_l1 = jnp.concatenate([blocks([dwg1, dwu1, dwd1, dwg2, dwu2, dwd2]), mixer_blocks(dwin, dwout, dwglu)], axis=1)
        dy = dx0

    grad_x = dy.reshape(x.shape)
    quad = lambda part, rb: _quad_add(part, rb, chip_idx, "rs_add_chips")
    own = {}

    def take(rows_f32, keys, l):
        off = 0
        for k, r in keys:
            own[(l, k)] = rows_f32[off:off + r]
            off += r

    ffn_keys = lambda names: [(k, fs) for k in names]
    mix_keys = [('win', win_rows), ('wout', ds), ('glu', glu_rows)]
    take(quad(part_l1, rb_l1), ffn_keys(FFN1 + FFN2) + mix_keys, 1)
    take(quad(part_c2, rb_c2), ffn_keys(FFN2), 0)
    take(quad(part_m, rb_m), mix_keys, 0)
    for k, (part, rb) in zip(FFN1, tail):
        own[(0, k)] = quad(part, rb)

    grads = {}
    grads_t = {}
    for k, n in zip(FFN1 + FFN2, ['ffn1_w_gate', 'ffn1_w_up', 'ffn1_w_down', 'ffn2_w_gate', 'ffn2_w_up', 'ffn2_w_down']):
        stacked = jnp.stack([own[(l, k)] for l in range(DEPTH)])
        if 'down' in n:
            grads[n] = stacked
        else:
            grads_t[n] = stacked
            grads[n] = jnp.swapaxes(stacked, 1, 2)
    gwin = jnp.stack([own[(l, 'win')].reshape(ds, Z_W) for l in range(DEPTH)])
    grads['w_in'] = jnp.concatenate([gwin[:, :, :Z_F + N_HEADS], gwin[:, :, Z_U:]], axis=2)
    grads['w_out'] = jnp.stack([own[(l, 'wout')] for l in range(DEPTH)])
    grads['s5_w_glu'] = jnp.stack([own[(l, 'glu')].reshape(D_C // 8, D_C) for l in range(DEPTH)])

    small_shapes = [((DEPTH, CONV_WIDTH, D_A) if n == 'conv_w' else w[n].shape) for n in small_names]
    conv_zero = jnp.zeros((DEPTH, CONV_WIDTH, D_A), F32)
    summed = _sum_small(all_small)
    for n, g in zip(small_names, _unpack_small(summed, small_shapes)):
        grads[n] = g
    grads['conv_w'] = lax.dynamic_slice_in_dim(grads['conv_w'], me * (D_A // 8), D_A // 8, axis=2)

    delta, new_m, new_v = {}, {}, {}
    for n in BIG + ['conv_w']:
        if n in grads_t:
            sh = grads_t[n].shape
            two = lambda v: jnp.swapaxes(v, 1, 2).reshape(-1, sh[-1])
            back = lambda v: jnp.swapaxes(v.reshape(sh), 1, 2)
            g2 = grads_t[n].reshape(-1, sh[-1])
        else:
            sh = w[n].shape
            two = lambda v: v.reshape(-1, sh[-1])
            back = lambda v: v.reshape(sh)
            g2 = two(grads[n])
        dl, nm, nv = _adamw(two(w[n]), g2, two(a['m_' + n]), two(a['v_' + n]), "adamw_" + n)
        delta[n], new_m[n], new_v[n] = back(dl), back(nm), back(nv)
    dl, nm, nv = _adamw(_pack_small([conv_zero] + [w[n] for n in SMALL]), summed,
                        _pack_small([conv_zero] + [a['m_' + n] for n in SMALL]),
                        _pack_small([conv_zero] + [a['v_' + n] for n in SMALL]), "adamw_small")
    for n, v1, v2, v3 in zip(small_names[1:], _unpack_small(dl, small_shapes)[1:], _unpack_small(nm, small_shapes)[1:],
                             _unpack_small(nv, small_shapes)[1:]):
        delta[n], new_m[n], new_v[n] = v1, v2, v3

    return (loss, grad_x, *[grads[n] for n in WEIGHTS], *[delta[n] for n in WEIGHTS], *[new_m[n] for n in WEIGHTS],
            *[new_v[n] for n in WEIGHTS])
```

```python
import functools
import math

import jax
import jax.numpy as jnp
import numpy as np
from jax import lax
from jax.experimental import pallas as pl
from jax.experimental.pallas import tpu as pltpu

F32 = jnp.float32
BF16 = jnp.bfloat16
MESH = pl.DeviceIdType.MESH

DEPTH = 2
ALPHA = (2 * DEPTH) ** 0.25
LN_EPS = 1e-5
RMS_EPS = 1e-6
RG_C = 8.0
CONV_WIDTH = 4
HEAD_DIM = 64
C_GROUP = 16
C_STATE = 64
D_A = 384
D_B = 384
D_C = 256
N_HEADS = D_B // HEAD_DIM
N_GROUPS = D_C // C_GROUP
N_STATE = N_GROUPS * C_STATE
Z_F = 2 * D_A + 3 * D_B
Z_U = Z_F + 128
Z_W = Z_U + D_C
N_IN = Z_F + N_HEADS + D_C
ADAM_LR, ADAM_B1, ADAM_B2, ADAM_EPS, ADAM_WD, ADAM_STEP = 0.001, 0.9, 0.999, 1e-08, 0.01, 10
LANES = 128
NEG = -1e30

WEIGHTS = ['ffn1_w_gate', 'ffn1_w_up', 'ffn1_w_down', 'ln1_g', 'ln1_b', 'w_in', 'conv_w', 'conv_b', 'rg_w_a', 'rg_b_a',
           'rg_w_x', 'rg_b_x', 'rg_lambda', 'fox_b_f', 's5_a_re', 's5_a_im', 's5_log_dt', 's5_b_re', 's5_b_im', 's5_c_re',
           's5_c_im', 's5_d', 's5_w_glu', 'mix_norm_g', 'w_out', 'ln2_g', 'ln2_b', 'ffn2_w_gate', 'ffn2_w_up', 'ffn2_w_down',
           'ln3_g', 'ln3_b']
BIG = ['ffn1_w_gate', 'ffn1_w_up', 'ffn1_w_down', 'w_in', 's5_w_glu', 'w_out', 'ffn2_w_gate', 'ffn2_w_up', 'ffn2_w_down']
SMALL_TAIL = ['fox_b_f', 's5_log_dt']
SMALL = [n for n in WEIGHTS if n not in BIG and n != 'conv_w' and n not in SMALL_TAIL] + SMALL_TAIL


def _sig(x):
    return 1.0 / (1.0 + jnp.exp(-x))


def _gelu(x):
    return 0.5 * x * (1.0 + jnp.tanh(math.sqrt(2.0 / math.pi) * (x + 0.044715 * (x * x * x))))


def _softplus(x):
    return jnp.maximum(x, 0.0) + jnp.log(1.0 + jnp.exp(jnp.minimum(x, -x)))


def _dot(a, b, dims):
    return lax.dot_general(a.astype(BF16), b.astype(BF16), (dims, ((), ())), preferred_element_type=F32)


NN = ((1,), (0,))
NT = ((1,), (1,))
TN = ((0,), (0,))


@jax.custom_vjp
def _bdot(a, w):
    return _dot(a, w, NN)


def _bdot_fwd(a, w):
    return _dot(a, w, NN), (a, w)


def _bdot_bwd(res, ct):
    a, w = res
    return _dot(ct, w, NT), _dot(a, ct, TN)


_bdot.defvjp(_bdot_fwd, _bdot_bwd)


def _ln(pre, g, b):
    mu = jnp.mean(pre, axis=-1, keepdims=True)
    xc = pre - mu
    var = jnp.mean(xc * xc, axis=-1, keepdims=True)
    return xc * lax.rsqrt(var + LN_EPS) * g + b


def _rms(x, g):
    return x * lax.rsqrt(jnp.mean(x * x, axis=-1, keepdims=True) + RMS_EPS) * g


def _tile(n, want):
    return want if n % want == 0 else n


class _Carry:
    def __init__(self, ins, outs, aliases, plan, n):
        self.ins, self.outs, self.aliases, self.plan, self.n = list(ins), list(outs), dict(aliases), plan, n


def _join(a, b):
    na, ma = len(a.ins), len(a.outs)

    def plan(x, y, c, ins, outs):
        return a.plan(x, y, c, ins[:na], outs[:ma]) + b.plan(x, y, c, ins[na:], outs[ma:])

    aliases = dict(a.aliases)
    aliases.update({na + i: ma + j for i, j in b.aliases.items()})
    return _Carry(a.ins + b.ins, a.outs + b.outs, aliases, plan, a.n + b.n)


def _copies(carry, cins, couts, send, recv):
    x, y, c = lax.axis_index("x"), lax.axis_index("y"), lax.axis_index("c")
    res = []
    for k, (s, d, peer) in enumerate(carry.plan(x, y, c, cins, couts)):
        if peer is None:
            res.append(pltpu.make_async_copy(s, d, send.at[k]))
        else:
            res.append(pltpu.make_async_remote_copy(src_ref=s, dst_ref=d, send_sem=send.at[k], recv_sem=recv.at[k],
                                                    device_id=peer, device_id_type=MESH))
    return res


def _call(body, grid, in_specs, out_specs, out_shape, scratch, semantics, name, args, carry=None):
    n_in, n_out, n_scr = len(in_specs), len(out_specs), len(scratch)
    if carry is None:
        res = pl.pallas_call(body, grid=grid, in_specs=in_specs, out_specs=out_specs, out_shape=out_shape,
                             scratch_shapes=scratch, compiler_params=pltpu.CompilerParams(dimension_semantics=semantics),
                             name=name)(*args)
        return list(res), []
    nci, nco = len(carry.ins), len(carry.outs)

    def wrapped(*refs):
        o0 = n_in + nci
        s0 = o0 + n_out + nco
        cins, couts = refs[n_in:o0], refs[o0 + n_out:s0]
        send, recv = refs[s0 + n_scr:]
        first = functools.reduce(jnp.logical_and, [pl.program_id(k) == 0 for k in range(len(grid))])
        last = functools.reduce(jnp.logical_and, [pl.program_id(k) == grid[k] - 1 for k in range(len(grid))])

        @pl.when(first)
        def _():
            for cp in _copies(carry, cins, couts, send, recv):
                cp.start()

        body(*refs[:n_in], *refs[o0:o0 + n_out], *refs[s0:s0 + n_scr])

        @pl.when(last)
        def _():
            for cp in _copies(carry, cins, couts, send, recv):
                cp.wait()

    hbm = pl.BlockSpec(memory_space=pl.ANY)
    res = pl.pallas_call(
        wrapped, grid=grid, in_specs=list(in_specs) + [hbm] * nci, out_specs=list(out_specs) + [hbm] * nco,
        out_shape=list(out_shape) + carry.outs, scratch_shapes=list(scratch) + [pltpu.SemaphoreType.DMA((carry.n,))] * 2,
        input_output_aliases={n_in + i: n_out + j for i, j in carry.aliases.items()},
        compiler_params=pltpu.CompilerParams(dimension_semantics=("arbitrary",) * len(grid), has_side_effects=True),
        name=name)(*args, *carry.ins)
    return list(res[:n_out]), list(res[n_out:])


def _run(carry, name):
    nci, nco = len(carry.ins), len(carry.outs)

    def body(*refs):
        cps = _copies(carry, refs[:nci], refs[nci:nci + nco], refs[-2], refs[-1])
        for cp in cps:
            cp.start()
        for cp in cps:
            cp.wait()

    hbm = pl.BlockSpec(memory_space=pl.ANY)
    return pl.pallas_call(
        body, in_specs=[hbm] * nci, out_specs=[hbm] * nco, out_shape=carry.outs, input_output_aliases=carry.aliases,
        scratch_shapes=[pltpu.SemaphoreType.DMA((carry.n,))] * 2, compiler_params=pltpu.CompilerParams(has_side_effects=True),
        name=name)(*carry.ins)


def _ag_chips(shards):
    def plan(x, y, c, ins, outs):
        res = []
        for src, out in zip(ins, outs):
            dst = out.at[4 * x + 2 * y + c]
            res += [(src, dst, None)] + [(src, dst, (px, py, c)) for px, py in ((1 - x, y), (x, 1 - y), (1 - x, 1 - y))]
        return res

    return _Carry(shards, [jax.ShapeDtypeStruct((8,) + s.shape, s.dtype) for s in shards], {}, plan, 4 * len(shards))


def _ag_sibling(gs):
    def plan(x, y, c, ins, outs):
        return [(out.at[2 * q + c], out.at[2 * q + c], (x, y, 1 - c)) for out in outs for q in range(4)]

    return _Carry(gs, [jax.ShapeDtypeStruct(g.shape, g.dtype) for g in gs], {i: i for i in range(len(gs))}, plan, 4 * len(gs))


def _rs_sibling(full):
    def plan(x, y, c, ins, outs):
        return [(ins[0].at[2 * q + (1 - c)], outs[0].at[q], (x, y, 1 - c)) for q in range(4)]

    return _Carry([full], [jax.ShapeDtypeStruct((4,) + full.shape[1:], full.dtype)], {}, plan, 4)


def _rs_chips(part):
    def plan(x, y, c, ins, outs):
        res = []
        for k, (dx, dy) in enumerate(((1, 0), (0, 1), (1, 1))):
            tx, ty = x ^ dx, y ^ dy
            res.append((ins[0].at[2 * tx + ty], outs[0].at[k], (tx, ty, c)))
        return res

    return _Carry([part], [jax.ShapeDtypeStruct((3,) + part.shape[1:], part.dtype)], {}, plan, 3)


def _mm(a, b, dims, out_dtype, tm, tn, tk, name, add=None, carry=None):
    if dims == 'nn':
        (m, k), n = a.shape, b.shape[1]
        a_spec = pl.BlockSpec((tm, tk), lambda i, j, q: (i, q))
        b_spec = pl.BlockSpec((tk, tn), lambda i, j, q: (q, j))
        dn = NN
    elif dims == 'nt':
        (m, k), n = a.shape, b.shape[0]
        a_spec = pl.BlockSpec((tm, tk), lambda i, j, q: (i, q))
        b_spec = pl.BlockSpec((tn, tk), lambda i, j, q: (j, q))
        dn = NT
    else:
        (k, m), n = a.shape, b.shape[1]
        a_spec = pl.BlockSpec((tk, tm), lambda i, j, q: (q, i))
        b_spec = pl.BlockSpec((tk, tn), lambda i, j, q: (q, j))
        dn = TN
    nk = k // tk
    o_spec = pl.BlockSpec((tm, tn), lambda i, j, q: (i, j))

    def body(*refs):
        if add is None:
            a_ref, b_ref, o_ref, acc_ref = refs
        else:
            a_ref, b_ref, add_ref, o_ref, acc_ref = refs
        q = pl.program_id(2)
        part = _dot(a_ref[...], b_ref[...], dn)

        @pl.when(q == 0)
        def _():
            acc_ref[...] = part

        @pl.when(q > 0)
        def _():
            acc_ref[...] += part

        @pl.when(q == nk - 1)
        def _():
            r = acc_ref[...]
            if add is not None:
                r = r + add_ref[...]
            o_ref[...] = r.astype(o_ref.dtype)

    ins = [a, b] + ([] if add is None else [add])
    specs = [a_spec, b_spec] + ([] if add is None else [o_spec])
    (res,), cres = _call(body, (m // tm, n // tn, nk), specs, [o_spec], [jax.ShapeDtypeStruct((m, n), out_dtype)],
                         [pltpu.VMEM((tm, tn), F32)], ("parallel", "parallel", "arbitrary"), name, ins, carry)
    return res if carry is None else (res, cres)


def _rowwise(fn, rows, params, outs, tm, name):
    t = rows[0].shape[0]
    nr, npar = len(rows), len(params)

    def body(*refs):
        r = [x[...] for x in refs[:nr]]
        p = [x[...] for x in refs[nr:nr + npar]]
        res = fn(*r, *p)
        for o_ref, o in zip(refs[nr + npar:], res):
            o_ref[...] = o.astype(o_ref.dtype)

    in_specs = ([pl.BlockSpec((tm, a.shape[1]), lambda i: (i, 0)) for a in rows]
                + [pl.BlockSpec(p.shape, lambda i: (0, 0)) for p in params])
    return pl.pallas_call(
        body, grid=(t // tm,), in_specs=in_specs,
        out_specs=[pl.BlockSpec((tm, c), lambda i: (i, 0)) for c, _ in outs],
        out_shape=[jax.ShapeDtypeStruct((t, c), d) for c, d in outs],
        compiler_params=pltpu.CompilerParams(dimension_semantics=("parallel",)), name=name)(*rows, *params)


def _rowwise_vjp(fn, rows, params, cots, tm, name, carry=None):
    t = rows[0].shape[0]
    nr, npar, nc = len(rows), len(params), len(cots)

    def body(*refs):
        r = [x[...] for x in refs[:nr]]
        p = [x[...] for x in refs[nr:nr + npar]]
        c = [x[...] for x in refs[nr + npar:nr + npar + nc]]
        o_refs = refs[nr + npar + nc:]
        _, pull = jax.vjp(fn, *r, *p)
        grads = pull(tuple(c))
        for o_ref, g in zip(o_refs[:nr], grads[:nr]):
            o_ref[...] = g
        i = pl.program_id(0)

        @pl.when(i == 0)
        def _():
            for o_ref, g in zip(o_refs[nr:], grads[nr:]):
                o_ref[...] = g

        @pl.when(i > 0)
        def _():
            for o_ref, g in zip(o_refs[nr:], grads[nr:]):
                o_ref[...] += g

    row_spec = lambda a: pl.BlockSpec((tm, a.shape[1]), lambda i: (i, 0))
    par_spec = lambda p: pl.BlockSpec(p.shape, lambda i: (0, 0))
    res, cres = _call(
        body, (t // tm,),
        [row_spec(a) for a in rows] + [par_spec(p) for p in params] + [row_spec(a) for a in cots],
        [row_spec(a) for a in rows] + [par_spec(p) for p in params],
        [jax.ShapeDtypeStruct(a.shape, F32) for a in rows] + [jax.ShapeDtypeStruct(p.shape, F32) for p in params],
        [], ("arbitrary",), name, [*rows, *params, *cots], carry)
    return res[:nr], res[nr:], cres


def _ffn_fwd(x, wgt, wut, wd, ln_g, ln_b, tm, tf, carry=None):
    t, d = x.shape
    f = wgt.shape[0]
    nj = f // tf

    def body(x_ref, wg_ref, wu_ref, wd_ref, g_ref, b_ref, y_ref, pre_ref, gs_ref, us_ref, acc_ref):
        j = pl.program_id(1)
        xv = x_ref[...]
        xb = xv.astype(BF16)
        g = _dot(xb, wg_ref[...], NT)
        u = _dot(xb, wu_ref[...], NT)
        gs_ref[...] = g.astype(BF16)
        us_ref[...] = u.astype(BF16)
        part = _dot(g * _sig(g) * u, wd_ref[...], NN)

        @pl.when(j == 0)
        def _():
            acc_ref[...] = part

        @pl.when(j > 0)
        def _():
            acc_ref[...] += part

        @pl.when(j == nj - 1)
        def _():
            pre = ALPHA * xv + 0.5 * acc_ref[...]
            pre_ref[...] = pre
            y_ref[...] = _ln(pre, g_ref[...], b_ref[...])

    w_spec = pl.BlockSpec((tf, d), lambda i, j: (j, 0))
    x_spec = pl.BlockSpec((tm, d), lambda i, j: (i, 0))
    v_spec = pl.BlockSpec((1, d), lambda i, j: (0, 0))
    h_spec = pl.BlockSpec((tm, tf), lambda i, j: (i, j))
    return _call(
        body, (t // tm, nj), [x_spec, w_spec, w_spec, w_spec, v_spec, v_spec], [x_spec, x_spec, h_spec, h_spec],
        [jax.ShapeDtypeStruct((t, d), F32), jax.ShapeDtypeStruct((t, d), F32),
         jax.ShapeDtypeStruct((t, f), BF16), jax.ShapeDtypeStruct((t, f), BF16)],
        [pltpu.VMEM((tm, d), F32)], ("parallel", "arbitrary"), "ffn_fwd", [x, wgt, wut, wd, ln_g, ln_b], carry)


def _ffn_bwd(dpre, gs, us, wgt, wut, wd, tm, tf, carry=None):
    t, d = dpre.shape
    f = wgt.shape[0]
    nj = f // tf

    def body(dp_ref, gs_ref, us_ref, wg_ref, wu_ref, wd_ref, dx_ref, dg_ref, du_ref, hh_ref, acc_ref):
        j = pl.program_id(1)
        dp = dp_ref[...]
        dh = _dot(0.5 * dp, wd_ref[...], NT)
        g = gs_ref[...].astype(F32)
        u = us_ref[...].astype(F32)
        s = _sig(g)
        sl = g * s
        dg = (dh * u * (s * (1.0 + g * (1.0 - s)))).astype(BF16)
        du = (dh * sl).astype(BF16)
        dg_ref[...] = dg
        du_ref[...] = du
        hh_ref[...] = (0.5 * sl * u).astype(BF16)
        part = _dot(dg, wg_ref[...], NN) + _dot(du, wu_ref[...], NN)

        @pl.when(j == 0)
        def _():
            acc_ref[...] = part

        @pl.when(j > 0)
        def _():
            acc_ref[...] += part

        @pl.when(j == nj - 1)
        def _():
            dx_ref[...] = ALPHA * dp + acc_ref[...]

    w_spec = pl.BlockSpec((tf, d), lambda i, j: (j, 0))
    x_spec = pl.BlockSpec((tm, d), lambda i, j: (i, 0))
    h_spec = pl.BlockSpec((tm, tf), lambda i, j: (i, j))
    return _call(
        body, (t // tm, nj), [x_spec, h_spec, h_spec, w_spec, w_spec, w_spec], [x_spec, h_spec, h_spec, h_spec],
        [jax.ShapeDtypeStruct((t, d), F32)] + [jax.ShapeDtypeStruct((t, f), BF16)] * 3,
        [pltpu.VMEM((tm, d), F32)], ("parallel", "arbitrary"), "ffn_bwd", [dpre, gs, us, wgt, wut, wd], carry)


def _scan8(a_ref, b_ref, out_ref, t, reverse=False):
    w = out_ref.shape[-1]
    sub = lax.broadcasted_iota(jnp.int32, (8, w), 0)

    def step(g, carry):
        r0 = pl.multiple_of((t // 8 - 1 - g if reverse else g) * 8, 8)
        bv = b_ref[pl.ds(r0, 8), :]
        av = None if a_ref is None else a_ref[pl.ds(r0, 8), :]
        for s in (1, 2, 4):
            ok = (sub < 8 - s) if reverse else (sub >= s)
            shift = 8 - s if reverse else s
            b_sh = jnp.where(ok, pltpu.roll(bv, shift, 0), 0.0)
            if av is None:
                bv = bv + b_sh
            else:
                bv = av * b_sh + bv
                av = av * jnp.where(ok, pltpu.roll(av, shift, 0), 1.0)
        h = bv + carry if av is None else bv + av * carry
        out_ref[pl.ds(r0, 8), :] = h
        return jnp.sum(jnp.where(sub == (0 if reverse else 7), h, 0.0), axis=0, keepdims=True)

    lax.fori_loop(0, t // 8, step, jnp.zeros((1, w), F32))


def _rg_local(xa, wa, ba, wx, bx, lam):
    r = _sig(_bdot(xa, wa) + ba)
    i = _sig(_bdot(xa, wx) + bx)
    log_a = -RG_C * r * _softplus(-lam)
    a = jnp.exp(log_a)
    mult = jnp.sqrt(-jnp.tanh(log_a) * (a * a + 1.0))
    return a, mult * (i * xa)


def _conv_taps(ext, n):
    return [ext[8:, :]] + [pltpu.roll(ext, s, 0)[8:, :] for s in (1, 2, 3)]


def _rg_fwd(z, cw, cb, wa, ba, wx, bx, lam):
    t = z.shape[0]
    cr = _tile(t, 256)
    nb = D_A // LANES

    def body(ax_ref, ag_ref, cw_ref, cb_ref, wa_ref, ba_ref, wx_ref, bx_ref, lam_ref, out_ref, h_ref, axp, a_s, b_s):
        axp[pl.ds(0, 8), :] = jnp.zeros((8, LANES), F32)
        pltpu.sync_copy(ax_ref, axp.at[pl.ds(8, t)])
        w = [cw_ref[pl.ds(k, 1), :] for k in range(CONV_WIDTH)]

        def chunk(c, carry):
            t0 = pl.multiple_of(c * cr, cr)
            taps = _conv_taps(axp[pl.ds(t0, cr + 8), :], cr)
            xa = cb_ref[...] + w[3] * taps[0] + w[2] * taps[1] + w[1] * taps[2] + w[0] * taps[3]
            a, gated = _rg_local(xa, wa_ref[...], ba_ref[...], wx_ref[...], bx_ref[...], lam_ref[...])
            a_s[pl.ds(t0, cr), :] = a
            b_s[pl.ds(t0, cr), :] = gated
            return carry

        lax.fori_loop(0, t // cr, chunk, 0)

        _scan8(a_s, b_s, h_ref, t)

        def fin(c, carry):
            t0 = pl.multiple_of(c * cr, cr)
            out_ref[pl.ds(t0, cr), :] = _gelu(ag_ref[pl.ds(t0, cr), :]) * h_ref[pl.ds(t0, cr), :]
            return carry

        lax.fori_loop(0, t // cr, fin, 0)

    col = lambda off: pl.BlockSpec((t, LANES), lambda b: (0, off + b))
    vec = pl.BlockSpec((1, LANES), lambda b: (0, b))
    mat = pl.BlockSpec((None, LANES, LANES), lambda b: (b, 0, 0))
    return pl.pallas_call(
        body, grid=(nb,),
        in_specs=[col(0), col(nb), pl.BlockSpec((CONV_WIDTH, LANES), lambda b: (0, b)), vec, mat, vec, mat, vec, vec],
        out_specs=[col(0), col(0)],
        out_shape=[jax.ShapeDtypeStruct((t, D_A), F32), jax.ShapeDtypeStruct((t, D_A), F32)],
        scratch_shapes=[pltpu.VMEM((t + 8, LANES), F32), pltpu.VMEM((t, LANES), F32), pltpu.VMEM((t, LANES), F32)],
        compiler_params=pltpu.CompilerParams(dimension_semantics=("arbitrary",)), name="rglru_fwd")(
            z, z, cw, cb, wa, ba, wx, bx, lam)


def _rg_bwd(z, h, dout, cw, cb, wa, ba, wx, bx, lam):
    t = z.shape[0]
    cr = _tile(t, 256)
    nb = D_A // LANES

    def body(ax_ref, ag_ref, h_ref, do_ref, cw_ref, cb_ref, wa_ref, ba_ref, wx_ref, bx_ref, lam_ref,
             dax_ref, dag_ref, dcw_ref, dcb_ref, dwa_ref, dba_ref, dwx_ref, dbx_ref, dlam_ref,
             axp, hp, xa_s, a_s, g_s, dxa_s, u_s):
        zero8 = jnp.zeros((8, LANES), F32)
        axp[pl.ds(0, 8), :] = zero8
        hp[pl.ds(0, 8), :] = zero8
        dxa_s[pl.ds(t, 8), :] = zero8
        u_s[pl.ds(t, 8), :] = zero8
        pltpu.sync_copy(ax_ref, axp.at[pl.ds(8, t)])
        pltpu.sync_copy(h_ref, hp.at[pl.ds(8, t)])
        w = [cw_ref[pl.ds(k, 1), :] for k in range(CONV_WIDTH)]
        for ref in (dcw_ref, dcb_ref, dwa_ref, dba_ref, dwx_ref, dbx_ref, dlam_ref):
            ref[...] = jnp.zeros(ref.shape, F32)

        def p1(c, carry):
            t0 = pl.multiple_of(c * cr, cr)
            taps = _conv_taps(axp[pl.ds(t0, cr + 8), :], cr)
            xa = cb_ref[...] + w[3] * taps[0] + w[2] * taps[1] + w[1] * taps[2] + w[0] * taps[3]
            a, _ = _rg_local(xa, wa_ref[...], ba_ref[...], wx_ref[...], bx_ref[...], lam_ref[...])
            xa_s[pl.ds(t0, cr), :] = xa
            a_s[pl.ds(t0, cr), :] = a
            ag = ag_ref[pl.ds(t0, cr), :]
            dov = do_ref[pl.ds(t0, cr), :]
            gel, pull = jax.vjp(_gelu, ag)
            g_s[pl.ds(t0, cr), :] = dov * gel
            u_s[pl.ds(t0, cr), :] = a * (dov * gel)
            dag_ref[pl.ds(t0, cr), :] = pull(dov * h_ref[pl.ds(t0, cr), :])[0]
            return carry

        lax.fori_loop(0, t // cr, p1, 0)
        _scan8(a_s, u_s, u_s, t, reverse=True)

        def p3(c, carry):
            t0 = pl.multiple_of(c * cr, cr)
            g = g_s[pl.ds(t0, cr), :] + pltpu.roll(u_s[pl.ds(t0, cr + 8), :], cr + 7, 0)[:cr, :]
            h_prev = pltpu.roll(hp[pl.ds(t0, cr + 8), :], 1, 0)[8:, :]
            _, pull = jax.vjp(_rg_local, xa_s[pl.ds(t0, cr), :], wa_ref[...], ba_ref[...], wx_ref[...], bx_ref[...],
                              lam_ref[...])
            dxa, dwa, dba, dwx, dbx, dlam = pull((g * h_prev, g))
            dxa_s[pl.ds(t0, cr), :] = dxa
            dwa_ref[...] += dwa
            dba_ref[...] += dba
            dwx_ref[...] += dwx
            dbx_ref[...] += dbx
            dlam_ref[...] += dlam
            return carry

        lax.fori_loop(0, t // cr, p3, 0)

        def p4(c, carry):
            t0 = pl.multiple_of(c * cr, cr)
            ext = dxa_s[pl.ds(t0, cr + 8), :]
            n = cr + 8
            ahead = [ext[:cr, :]] + [pltpu.roll(ext, n - s, 0)[:cr, :] for s in (1, 2, 3)]
            dax_ref[pl.ds(t0, cr), :] = w[3] * ahead[0] + w[2] * ahead[1] + w[1] * ahead[2] + w[0] * ahead[3]
            taps = _conv_taps(axp[pl.ds(t0, cr + 8), :], cr)
            dxa = ahead[0]
            for k in range(CONV_WIDTH):
                dcw_ref[pl.ds(k, 1), :] += jnp.sum(dxa * taps[CONV_WIDTH - 1 - k], axis=0, keepdims=True)
            dcb_ref[...] += jnp.sum(dxa, axis=0, keepdims=True)
            return carry

        lax.fori_loop(0, t // cr, p4, 0)

    col = lambda off: pl.BlockSpec((t, LANES), lambda b: (0, off + b))
    vec = pl.BlockSpec((1, LANES), lambda b: (0, b))
    mat = pl.BlockSpec((None, LANES, LANES), lambda b: (b, 0, 0))
    cws = pl.BlockSpec((CONV_WIDTH, LANES), lambda b: (0, b))
    sds = jax.ShapeDtypeStruct
    return pl.pallas_call(
        body, grid=(nb,),
        in_specs=[col(0), col(nb), col(0), col(0), cws, vec, mat, vec, mat, vec, vec],
        out_specs=[col(0), col(0), cws, vec, mat, vec, mat, vec, vec],
        out_shape=[sds((t, D_A), F32), sds((t, D_A), F32), sds((CONV_WIDTH, D_A), F32), sds((1, D_A), F32),
                   sds((nb, LANES, LANES), F32), sds((1, D_A), F32), sds((nb, LANES, LANES), F32), sds((1, D_A), F32),
                   sds((1, D_A), F32)],
        scratch_shapes=[pltpu.VMEM((t + 8, LANES), F32), pltpu.VMEM((t + 8, LANES), F32), pltpu.VMEM((t, LANES), F32),
                        pltpu.VMEM((t, LANES), F32), pltpu.VMEM((t, LANES), F32), pltpu.VMEM((t + 8, LANES), F32),
                        pltpu.VMEM((t + 8, LANES), F32)],
        compiler_params=pltpu.CompilerParams(dimension_semantics=("arbitrary",)), name="rglru_bwd")(
            z, z, h, dout, cw, cb, wa, ba, wx, bx, lam)


def _fgate_fwd(z, bf):
    t = z.shape[0]

    def body(zf_ref, bf_ref, c_ref):
        c_ref[...] = -_softplus(-(zf_ref[...] + bf_ref[...]))
        _scan8(None, c_ref, c_ref, t)

    return pl.pallas_call(
        body, grid=(1,), in_specs=[pl.BlockSpec((t, LANES), lambda i: (0, Z_F // LANES)), pl.BlockSpec((1, LANES), lambda i: (0, 0))],
        out_specs=pl.BlockSpec((t, LANES), lambda i: (0, 0)), out_shape=jax.ShapeDtypeStruct((t, LANES), F32),
        compiler_params=pltpu.CompilerParams(dimension_semantics=("arbitrary",)), name="fgate_fwd")(z, bf)


def _fgate_bwd(z, bf, dc):
    t = z.shape[0]

    def body(zf_ref, bf_ref, dc_ref, dz_ref, db_ref):
        _scan8(None, dc_ref, dz_ref, t, reverse=True)
        dz = dz_ref[...] * _sig(-(zf_ref[...] + bf_ref[...]))
        dz_ref[...] = dz
        db_ref[...] = jnp.sum(dz, axis=0, keepdims=True)

    return pl.pallas_call(
        body, grid=(1,),
        in_specs=[pl.BlockSpec((t, LANES), lambda i: (0, Z_F // LANES)), pl.BlockSpec((1, LANES), lambda i: (0, 0)),
                  pl.BlockSpec((t, LANES), lambda i: (0, 0))],
        out_specs=[pl.BlockSpec((t, LANES), lambda i: (0, 0)), pl.BlockSpec((1, LANES), lambda i: (0, 0))],
        out_shape=[jax.ShapeDtypeStruct((t, LANES), F32), jax.ShapeDtypeStruct((1, LANES), F32)],
        compiler_params=pltpu.CompilerParams(dimension_semantics=("arbitrary",)), name="fgate_bwd")(z, bf, dc)


def _cast_rows(src_ref, dst_ref, t, rows, fn):
    def cp(c, carry):
        r0 = pl.multiple_of(c * rows, rows)
        dst_ref[pl.ds(r0, rows), :] = fn(src_ref[pl.ds(r0, rows), :]).astype(dst_ref.dtype)
        return carry

    lax.fori_loop(0, t // rows, cp, 0)


def _attn_groups(t):
    tq = _tile(t, 256)
    nq = t // tq
    grp = 4 if nq % 4 == 0 else 1
    return tq, nq, grp


def _attn_fwd(z, crow, carry=None):
    t = z.shape[0]
    tq, nq, grp = _attn_groups(t)
    tk = grp * tq
    scale = HEAD_DIM ** -0.5

    def body(q_ref, k_ref, v_ref, cr_ref, o_ref, lse_ref, kb_s, vb_s):
        lane = lax.broadcasted_iota(jnp.int32, (1, LANES), 1)
        hmask = [(lane // HEAD_DIM) == hh for hh in range(2)]
        _cast_rows(k_ref, kb_s, t, tq, lambda v: v)
        _cast_rows(v_ref, vb_s, t, tq, lambda v: v)

        def qblock(g, r):
            q0 = pl.multiple_of((g * grp + r) * tq, tq)
            qv = q_ref[pl.ds(q0, tq), :] * scale
            qa = [jnp.where(hmask[hh], qv, 0.0).astype(BF16) for hh in range(2)]

            def update(st, k0, width, off):
                kb = kb_s[pl.ds(k0, width), :]
                vb = vb_s[pl.ds(k0, width), :]
                new = []
                for hh in range(2):
                    m, l, acc = st[hh]
                    s = _dot(qa[hh], kb, NT) - cr_ref[hh, :, pl.ds(k0, width)]
                    if off is not None:
                        keep = (lax.broadcasted_iota(jnp.int32, (tq, width), 0) + off
                                >= lax.broadcasted_iota(jnp.int32, (tq, width), 1))
                        s = jnp.where(keep, s, NEG)
                    m_new = jnp.maximum(m, jnp.max(s, axis=-1, keepdims=True))
                    p = jnp.exp(s - m_new)
                    corr = jnp.exp(m - m_new)
                    new.append((m_new, corr * l + jnp.sum(p, axis=-1, keepdims=True), corr * acc + _dot(p, vb, NN)))
                return tuple(new)

            one = (jnp.full((tq, 1), NEG, F32), jnp.zeros((tq, 1), F32), jnp.zeros((tq, LANES), F32))
            st = lax.fori_loop(0, g, lambda j, st: update(st, pl.multiple_of(j * tk, tk), tk, None), (one, one))
            st = update(st, pl.multiple_of(g * tk, tk), (r + 1) * tq, r * tq)
            o_ref[pl.ds(q0, tq), :] = jnp.where(hmask[0], st[0][2] / st[0][1], st[1][2] / st[1][1])
            for hh in range(2):
                lse_ref[hh, pl.ds(q0, tq), :] = st[hh][0] + jnp.log(st[hh][1])

        def group(g, carry):
            for r in range(grp):
                qblock(g, r)
            return carry

        lax.fori_loop(0, nq // grp, group, 0)

    base = 2 * D_A // LANES
    nh = D_B // LANES
    col = lambda off: pl.BlockSpec((t, LANES), lambda p: (0, off + p))
    return _call(
        body, (nh,), [col(base), col(base + nh), col(base + 2 * nh), pl.BlockSpec((2, 1, t), lambda p: (p, 0, 0))],
        [col(0), pl.BlockSpec((2, t, 1), lambda p: (p, 0, 0))],
        [jax.ShapeDtypeStruct((t, D_B), F32), jax.ShapeDtypeStruct((N_HEADS, t, 1), F32)],
        [pltpu.VMEM((t, LANES), BF16), pltpu.VMEM((t, LANES), BF16)], ("parallel",), "attn_fwd", [z, z, z, crow], carry)


def _attn_bwd(z, crow, lse, do, carry=None):
    t = z.shape[0]
    tq, nq, grp = _attn_groups(t)
    tw = grp * tq
    scale = HEAD_DIM ** -0.5

    def body(q_ref, k_ref, v_ref, cr_ref, lse_ref, do_ref, dq_ref, dk_ref, dv_ref, dc_ref, qa_s, da_s, kb_s, vb_s, dl_s):
        lane = lax.broadcasted_iota(jnp.int32, (1, LANES), 1)
        hmask = [(lane // HEAD_DIM) == hh for hh in range(2)]
        _cast_rows(k_ref, kb_s, t, tq, lambda v: v)
        _cast_rows(v_ref, vb_s, t, tq, lambda v: v)
        for hh in range(2):
            _cast_rows(q_ref, qa_s.at[hh], t, tq, lambda v, hh=hh: jnp.where(hmask[hh], v * scale, 0.0))
            _cast_rows(do_ref, da_s.at[hh], t, tq, lambda v, hh=hh: jnp.where(hmask[hh], v, 0.0))
        _cast_rows(q_ref, dq_ref, t, tq, lambda v: jnp.zeros_like(v))

        def probs(hh, q0, nq_rows, k0, nk_rows, off):
            s = _dot(qa_s[hh, pl.ds(q0, nq_rows), :], kb_s[pl.ds(k0, nk_rows), :], NT) - cr_ref[hh, :, pl.ds(k0, nk_rows)]
            p = jnp.exp(s - lse_ref[hh, pl.ds(q0, nq_rows), :])
            if off is not None:
                keep = (lax.broadcasted_iota(jnp.int32, (nq_rows, nk_rows), 0) + off
                        >= lax.broadcasted_iota(jnp.int32, (nq_rows, nk_rows), 1))
                p = jnp.where(keep, p, 0.0)
            return p, _dot(da_s[hh, pl.ds(q0, nq_rows), :], vb_s[pl.ds(k0, nk_rows), :], NT)

        def delta(g, r):
            q0 = pl.multiple_of((g * grp + r) * tq, tq)

            def add(k0, width, off, acc):
                res = []
                for hh in range(2):
                    p, dp = probs(hh, q0, tq, k0, width, off)
                    res.append(acc[hh] + jnp.sum(p * dp, axis=-1, keepdims=True))
                return tuple(res)

            zcol = jnp.zeros((tq, 1), F32)
            acc = lax.fori_loop(0, g, lambda j, acc: add(pl.multiple_of(j * tw, tw), tw, None, acc), (zcol, zcol))
            acc = add(pl.multiple_of(g * tw, tw), (r + 1) * tq, r * tq, acc)
            for hh in range(2):
                dl_s[hh, pl.ds(q0, tq), :] = acc[hh]

        def delta_group(g, carry):
            for r in range(grp):
                delta(g, r)
            return carry

        lax.fori_loop(0, nq // grp, delta_group, 0)

        def kblock(g, r):
            k0 = pl.multiple_of((g * grp + r) * tq, tq)
            kb = kb_s[pl.ds(k0, tq), :]

            def upd(q0, height, off, st):
                dk, dv, dc = st[0], st[1], [st[2], st[3]]
                dqs = []
                for hh in range(2):
                    p, dp = probs(hh, q0, height, k0, tq, off)
                    ds = p * (dp - dl_s[hh, pl.ds(q0, height), :])
                    dv = dv + _dot(p, da_s[hh, pl.ds(q0, height), :], TN)
                    dk = dk + _dot(ds, qa_s[hh, pl.ds(q0, height), :], TN)
                    dqs.append(_dot(ds, kb, NN))
                    dc[hh] = dc[hh] - jnp.sum(ds, axis=0, keepdims=True)
                dq_ref[pl.ds(q0, height), :] += jnp.where(hmask[0], dqs[0], dqs[1]) * scale
                return dk, dv, dc[0], dc[1]

            zero = jnp.zeros((tq, LANES), F32)
            zrow = jnp.zeros((1, tq), F32)
            st = upd(k0, (grp - r) * tq, 0, (zero, zero, zrow, zrow))
            st = lax.fori_loop(g + 1, nq // grp, lambda i, st: upd(pl.multiple_of(i * tw, tw), tw, None, st), st)
            dk_ref[pl.ds(k0, tq), :] = st[0]
            dv_ref[pl.ds(k0, tq), :] = st[1]
            for hh in range(2):
                dc_ref[hh, :, pl.ds(k0, tq)] = st[2 + hh]

        def kgroup(g, carry):
            for r in range(grp):
                kblock(g, r)
            return carry

        lax.fori_loop(0, nq // grp, kgroup, 0)

    base = 2 * D_A // LANES
    nh = D_B // LANES
    col = lambda off: pl.BlockSpec((t, LANES), lambda p: (0, off + p))
    ccs = pl.BlockSpec((2, t, 1), lambda p: (p, 0, 0))
    crs = pl.BlockSpec((2, 1, t), lambda p: (p, 0, 0))
    return _call(
        body, (nh,), [col(base), col(base + nh), col(base + 2 * nh), crs, ccs, col(0)], [col(0), col(0), col(0), crs],
        [jax.ShapeDtypeStruct((t, D_B), F32)] * 3 + [jax.ShapeDtypeStruct((N_HEADS, 1, t), F32)],
        [pltpu.VMEM((2, t, LANES), BF16), pltpu.VMEM((2, t, LANES), BF16), pltpu.VMEM((t, LANES), BF16),
         pltpu.VMEM((t, LANES), BF16), pltpu.VMEM((2, t, 1), F32)], ("parallel",), "attn_bwd", [z, z, z, crow, lse, do], carry)


def _s5_disc(a_re, a_im, log_dt, b_re, b_im):
    dt = jnp.exp(log_dt)
    mag = jnp.exp(a_re * dt)
    ar = mag * jnp.cos(a_im * dt)
    ai = mag * jnp.sin(a_im * dt)
    den = a_re * a_re + a_im * a_im
    kr = ((ar - 1.0) * a_re + ai * a_im) / den
    ki = (ai * a_re - (ar - 1.0) * a_im) / den
    kr3, ki3 = kr[:, None, :], ki[:, None, :]
    return ar, ai, kr3 * b_re - ki3 * b_im, kr3 * b_im + ki3 * b_re


def _s5_prep(a_re, a_im, log_dt, b_re, b_im):
    g, p = a_re.shape
    gc = b_re.shape[1]

    def body(*refs):
        res = _s5_disc(*[r[...] for r in refs[:5]])
        for o_ref, v in zip(refs[5:], res):
            o_ref[...] = v

    sds = jax.ShapeDtypeStruct
    return pl.pallas_call(body, out_shape=[sds((g, p), F32), sds((g, p), F32), sds((g, gc, p), F32), sds((g, gc, p), F32)],
                          name="s5_prep")(a_re, a_im, log_dt, b_re, b_im)


def _s5_prep_bwd(a_re, a_im, log_dt, b_re, b_im, d_ar, d_ai, d_br, d_bi):
    ins = (a_re, a_im, log_dt, b_re, b_im)

    def body(*refs):
        vals = [r[...] for r in refs[:5]]
        cts = tuple(r[...] for r in refs[5:9])
        _, pull = jax.vjp(_s5_disc, *vals)
        for o_ref, v in zip(refs[9:], pull(cts)):
            o_ref[...] = v

    return pl.pallas_call(body, out_shape=[jax.ShapeDtypeStruct(a.shape, F32) for a in ins], name="s5_prep_bwd")(
        *ins, d_ar, d_ai, d_br, d_bi)


def _s5_scan_rows(t, ar, ai, hr_s, hi_s, off, reverse):
    n = ar.shape[1]
    if reverse:
        ai = -ai
    sub = lax.broadcasted_iota(jnp.int32, (8, n), 0)
    cmul = lambda xr, xi, yr, yi: (xr * yr - xi * yi, xr * yi + xi * yr)
    pw = [(ar, ai)]
    for _ in range(7):
        pw.append(cmul(*pw[-1], ar, ai))
    pr = jnp.zeros((8, n), F32)
    pi = jnp.zeros((8, n), F32)
    for r in range(8):
        k = 7 - r if reverse else r
        pr = jnp.where(sub == r, pw[k][0], pr)
        pi = jnp.where(sub == r, pw[k][1], pi)

    def step(g, carry):
        cr, ci = carry
        r0 = pl.multiple_of(off + (t // 8 - 1 - g if reverse else g) * 8, 8)
        br = hr_s[pl.ds(r0, 8), :]
        bi = hi_s[pl.ds(r0, 8), :]
        for s in (1, 2, 4):
            ok = (sub < 8 - s) if reverse else (sub >= s)
            shift = 8 - s if reverse else s
            sr = jnp.where(ok, pltpu.roll(br, shift, 0), 0.0)
            si = jnp.where(ok, pltpu.roll(bi, shift, 0), 0.0)
            mr, mi = cmul(pw[s - 1][0], pw[s - 1][1], sr, si)
            br, bi = br + mr, bi + mi
        mr, mi = cmul(pr, pi, cr, ci)
        br, bi = br + mr, bi + mi
        hr_s[pl.ds(r0, 8), :] = br
        hi_s[pl.ds(r0, 8), :] = bi
        edge = sub == (0 if reverse else 7)
        return (jnp.sum(jnp.where(edge, br, 0.0), axis=0, keepdims=True),
                jnp.sum(jnp.where(edge, bi, 0.0), axis=0, keepdims=True))

    zero = jnp.zeros((1, n), F32)
    lax.fori_loop(0, t // 8, step, (zero, zero))


def _s5_fwd(z, bd_re, bd_im, ab_re, ab_im, cd_re, cd_im, dvec, carry=None):
    t = z.shape[0]
    cr = _tile(t, 256)
    ns = N_STATE // 2

    def body(u_ref, br_ref, bi_ref, ar_ref, ai_ref, cre_ref, cim_ref, d_ref, y_ref, hr_s, hi_s):
        def p1(c, carry):
            t0 = pl.multiple_of(c * cr, cr)
            u = u_ref[pl.ds(t0, cr), :]
            hr_s[pl.ds(t0, cr), :] = _dot(u, br_ref[...], NN)
            hi_s[pl.ds(t0, cr), :] = _dot(u, bi_ref[...], NN)
            return carry

        lax.fori_loop(0, t // cr, p1, 0)
        _s5_scan_rows(t, ar_ref[...], ai_ref[...], hr_s, hi_s, 0, False)

        def p3(c, carry):
            t0 = pl.multiple_of(c * cr, cr)
            y_ref[pl.ds(t0, cr), :] = (_dot(hr_s[pl.ds(t0, cr), :], cre_ref[...], NN)
                                       - _dot(hi_s[pl.ds(t0, cr), :], cim_ref[...], NN)
                                       + d_ref[...] * u_ref[pl.ds(t0, cr), :])
            return carry

        lax.fori_loop(0, t // cr, p3, 0)

    blk = lambda r, c: pl.BlockSpec((None, r, c), lambda b: (b, 0, 0))
    return _call(
        body, (2,),
        [pl.BlockSpec((t, LANES), lambda b: (0, Z_U // LANES + b)), blk(LANES, ns), blk(LANES, ns), blk(1, ns),
         blk(1, ns), blk(ns, LANES), blk(ns, LANES), pl.BlockSpec((1, LANES), lambda b: (0, b))],
        [pl.BlockSpec((t, LANES), lambda b: (0, b))], [jax.ShapeDtypeStruct((t, D_C), F32)],
        [pltpu.VMEM((t, ns), F32), pltpu.VMEM((t, ns), F32)], ("arbitrary",), "s5_fwd",
        [z, bd_re, bd_im, ab_re, ab_im, cd_re, cd_im, dvec], carry)


def _s5_bwd(z, dy, bd_re, bd_im, ab_re, ab_im, cd_re, cd_im, dvec):
    t = z.shape[0]
    cr = _tile(t, 256)
    ns = N_STATE // 2

    def body(u_ref, dy_ref, br_ref, bi_ref, ar_ref, ai_ref, cre_ref, cim_ref, d_ref,
             du_ref, dbr_ref, dbi_ref, dar_ref, dai_ref, dcre_ref, dcim_ref, dd_ref, hr_s, hi_s, gr_s, gi_s):
        zero8 = jnp.zeros((8, ns), F32)
        hr_s[pl.ds(0, 8), :] = zero8
        hi_s[pl.ds(0, 8), :] = zero8
        for ref in (dbr_ref, dbi_ref, dar_ref, dai_ref, dcre_ref, dcim_ref, dd_ref):
            ref[...] = jnp.zeros(ref.shape, F32)

        def p1(c, carry):
            t0 = pl.multiple_of(c * cr, cr)
            u = u_ref[pl.ds(t0, cr), :]
            hr_s[pl.ds(t0 + 8, cr), :] = _dot(u, br_ref[...], NN)
            hi_s[pl.ds(t0 + 8, cr), :] = _dot(u, bi_ref[...], NN)
            return carry

        lax.fori_loop(0, t // cr, p1, 0)
        _s5_scan_rows(t, ar_ref[...], ai_ref[...], hr_s, hi_s, 8, False)

        def p3(c, carry):
            t0 = pl.multiple_of(c * cr, cr)
            dyv = dy_ref[pl.ds(t0, cr), :]
            u = u_ref[pl.ds(t0, cr), :]
            gr_s[pl.ds(t0, cr), :] = _dot(dyv, cre_ref[...], NT)
            gi_s[pl.ds(t0, cr), :] = -_dot(dyv, cim_ref[...], NT)
            dcre_ref[...] += _dot(hr_s[pl.ds(t0 + 8, cr), :], dyv, TN)
            dcim_ref[...] -= _dot(hi_s[pl.ds(t0 + 8, cr), :], dyv, TN)
            dd_ref[...] += jnp.sum(dyv * u, axis=0, keepdims=True)
            du_ref[pl.ds(t0, cr), :] = dyv * d_ref[...]
            return carry

        lax.fori_loop(0, t // cr, p3, 0)
        _s5_scan_rows(t, ar_ref[...], ai_ref[...], gr_s, gi_s, 0, True)

        def p5(c, carry):
            t0 = pl.multiple_of(c * cr, cr)
            u = u_ref[pl.ds(t0, cr), :]
            gr = gr_s[pl.ds(t0, cr), :]
            gi = gi_s[pl.ds(t0, cr), :]
            dbr_ref[...] += _dot(u, gr, TN)
            dbi_ref[...] += _dot(u, gi, TN)
            du_ref[pl.ds(t0, cr), :] += _dot(gr, br_ref[...], NT) + _dot(gi, bi_ref[...], NT)
            hpr = pltpu.roll(hr_s[pl.ds(t0, cr + 8), :], 1, 0)[8:, :]
            hpi = pltpu.roll(hi_s[pl.ds(t0, cr + 8), :], 1, 0)[8:, :]
            dar_ref[...] += jnp.sum(gr * hpr + gi * hpi, axis=0, keepdims=True)
            dai_ref[...] += jnp.sum(gi * hpr - gr * hpi, axis=0, keepdims=True)
            return carry

        lax.fori_loop(0, t // cr, p5, 0)

    blk = lambda r, c: pl.BlockSpec((None, r, c), lambda b: (b, 0, 0))
    ucol = pl.BlockSpec((t, LANES), lambda b: (0, Z_U // LANES + b))
    ycol = pl.BlockSpec((t, LANES), lambda b: (0, b))
    dsp = pl.BlockSpec((1, LANES), lambda b: (0, b))
    sds = jax.ShapeDtypeStruct
    return pl.pallas_call(
        body, grid=(2,),
        in_specs=[ucol, ycol, blk(LANES, ns), blk(LANES, ns), blk(1, ns), blk(1, ns), blk(ns, LANES), blk(ns, LANES), dsp],
        out_specs=[ycol, blk(LANES, ns), blk(LANES, ns), blk(1, ns), blk(1, ns), blk(ns, LANES), blk(ns, LANES), dsp],
        out_shape=[sds((t, D_C), F32), sds((2, LANES, ns), F32), sds((2, LANES, ns), F32), sds((2, 1, ns), F32),
                   sds((2, 1, ns), F32), sds((2, ns, LANES), F32), sds((2, ns, LANES), F32), sds((1, D_C), F32)],
        scratch_shapes=[pltpu.VMEM((t + 8, ns), F32), pltpu.VMEM((t + 8, ns), F32), pltpu.VMEM((t, ns), F32),
                        pltpu.VMEM((t, ns), F32)],
        compiler_params=pltpu.CompilerParams(dimension_semantics=("arbitrary",)), name="s5_bwd")(
            z, dy, bd_re, bd_im, ab_re, ab_im, cd_re, cd_im, dvec)


def _mix_out(out_a, out_b, yc, x1, ga, gb, gc, wglu, wout, ln_g, ln_b):
    yg = _gelu(yc)
    out_c = yg * _sig(_bdot(yg, wglu))
    o = jnp.concatenate([_rms(out_a, ga), _rms(out_b, gb), _rms(out_c, gc)], axis=-1)
    return (_ln(ALPHA * x1 + _bdot(o, wout), ln_g, ln_b),)


def _ln_only(pre, g, b):
    return (_ln(pre, g, b),)


def _loss_head(y, target, tm):
    t, d = y.shape

    def body(y_ref, t_ref, dy_ref, l_ref):
        i = pl.program_id(0)
        e = y_ref[...] - t_ref[...]
        dy_ref[...] = e * (1.0 / d)
        part = 0.5 * jnp.sum(jnp.sum(e * e, axis=-1, keepdims=True) * (1.0 / d), axis=0, keepdims=True)
        row = jnp.where(lax.broadcasted_iota(jnp.int32, (1, LANES), 1) == 0, part, 0.0)

        @pl.when(i == 0)
        def _():
            l_ref[...] = row

        @pl.when(i > 0)
        def _():
            l_ref[...] += row

    spec = pl.BlockSpec((tm, d), lambda i: (i, 0))
    return pl.pallas_call(
        body, grid=(t // tm,), in_specs=[spec, spec], out_specs=[spec, pl.BlockSpec((1, LANES), lambda i: (0, 0))],
        out_shape=[jax.ShapeDtypeStruct((t, d), F32), jax.ShapeDtypeStruct((1, LANES), F32)],
        compiler_params=pltpu.CompilerParams(dimension_semantics=("arbitrary",)), name="loss_head")(y, target)


def _adamw(w, g, m, v, name):
    r, c = w.shape
    tr = r
    for cand in (512, 256, 352, 128):
        if r % cand == 0:
            tr = cand
            break

    def body(w_ref, g_ref, m_ref, v_ref, d_ref, nm_ref, nv_ref):
        gv = g_ref[...]
        mn = ADAM_B1 * m_ref[...] + (1.0 - ADAM_B1) * gv
        vn = ADAM_B2 * v_ref[...] + (1.0 - ADAM_B2) * (gv * gv)
        m_hat = mn / (1.0 - ADAM_B1 ** ADAM_STEP)
        v_hat = vn / (1.0 - ADAM_B2 ** ADAM_STEP)
        d_ref[...] = -ADAM_LR * (m_hat / (jnp.sqrt(v_hat) + ADAM_EPS) + ADAM_WD * w_ref[...])
        nm_ref[...] = mn
        nv_ref[...] = vn

    spec = pl.BlockSpec((tr, c), lambda i: (i, 0))
    return pl.pallas_call(
        body, grid=(r // tr,), in_specs=[spec] * 4, out_specs=[spec] * 3,
        out_shape=[jax.ShapeDtypeStruct((r, c), F32)] * 3,
        compiler_params=pltpu.CompilerParams(dimension_semantics=("parallel",)), name=name)(w, g, m, v)


def _row_tile(r):
    for cand in (512, 448, 352, 256, 128):
        if r % cand == 0:
            return cand
    return r


def _pair_add(a, b, idx, name, out_dtype):
    _, r, w = a.shape
    tr = _row_tile(r)

    def body(i_ref, a_ref, b_ref, o_ref):
        o_ref[...] = (a_ref[...].astype(F32) + b_ref[...].astype(F32)).astype(o_ref.dtype)

    grid_spec = pltpu.PrefetchScalarGridSpec(
        num_scalar_prefetch=1, grid=(4, r // tr),
        in_specs=[pl.BlockSpec((None, tr, w), lambda q, i, s: (2 * q + s[0], i, 0)),
                  pl.BlockSpec((None, tr, w), lambda q, i, s: (q, i, 0))],
        out_specs=pl.BlockSpec((None, tr, w), lambda q, i, s: (q, i, 0)))
    return pl.pallas_call(body, grid_spec=grid_spec, out_shape=jax.ShapeDtypeStruct((4, r, w), out_dtype),
                          compiler_params=pltpu.CompilerParams(dimension_semantics=("parallel", "parallel")), name=name)(
                              idx, a, b)


def _quad_add(p, rb, idx, name):
    _, r, w = p.shape
    tr = _row_tile(r)

    def body(i_ref, p_ref, r0, r1, r2, o_ref):
        o_ref[...] = ((p_ref[...].astype(F32) + r0[...].astype(F32)) + r1[...].astype(F32)) + r2[...].astype(F32)

    grid_spec = pltpu.PrefetchScalarGridSpec(
        num_scalar_prefetch=1, grid=(r // tr,),
        in_specs=[pl.BlockSpec((None, tr, w), lambda i, s: (s[0], i, 0))]
        + [pl.BlockSpec((None, tr, w), functools.partial(lambda i, s, k: (k, i, 0), k=k)) for k in range(3)],
        out_specs=pl.BlockSpec((tr, w), lambda i, s: (i, 0)))
    return pl.pallas_call(body, grid_spec=grid_spec, out_shape=jax.ShapeDtypeStruct((r, w), F32),
                          compiler_params=pltpu.CompilerParams(dimension_semantics=("parallel",)), name=name)(
                              idx, p, rb, rb, rb)


def _gather_small(buf):
    def plan(x, y, c, ins, outs):
        dst = outs[0].at[4 * x + 2 * y + c]
        res = [(ins[0], dst, None)]
        for rel in range(1, 8):
            res.append((ins[0], dst, (x ^ (rel >> 2), y ^ ((rel >> 1) & 1), c ^ (rel & 1))))
        return res

    return _Carry([buf], [jax.ShapeDtypeStruct((8,) + buf.shape, buf.dtype)], {}, plan, 8)


def _sum_small(allb):
    _, r, w = allb.shape

    def body(a_ref, o_ref):
        s = a_ref[0]
        for k in range(1, 8):
            s = s + a_ref[k]
        o_ref[...] = s

    return pl.pallas_call(body, out_shape=jax.ShapeDtypeStruct((r, w), F32), name="ar_small_sum")(allb)


def _pad_rows(a, rows):
    return jnp.pad(a, ((0, rows - a.shape[0]), (0, 0)))


def _pack_small(arrs):
    rows, tail = [], []
    for a in arrs:
        if not tail and a.size % LANES == 0:
            rows.append(a.reshape(-1, LANES))
        else:
            tail.append(a.reshape(-1))
    n_rows = sum(r.shape[0] for r in rows)
    n_tail = sum(int(v.size) for v in tail)
    tail_rows = -(-n_tail // LANES)
    total_rows = n_rows + tail_rows + (-(n_rows + tail_rows)) % 8
    if tail:
        tail.append(jnp.zeros((tail_rows * LANES - n_tail,), F32))
        rows.append(jnp.concatenate(tail).reshape(tail_rows, LANES))
    if total_rows > n_rows + tail_rows:
        rows.append(jnp.zeros((total_rows - n_rows - tail_rows, LANES), F32))
    return jnp.concatenate(rows, axis=0)


def _unpack_small(buf, shapes):
    out, off = [], 0
    flat = None
    for s in shapes:
        n = int(np.prod(s))
        if off % LANES == 0 and n % LANES == 0:
            out.append(buf[off // LANES:(off + n) // LANES].reshape(s))
        else:
            flat = buf.reshape(-1) if flat is None else flat
            out.append(flat[off:off + n].reshape(s))
        off += n
    return out


def _block_diag(blocks, nb):
    m, r, c = blocks.shape
    n = m // nb
    eye = jnp.eye(n, dtype=blocks.dtype)
    return (blocks.reshape(nb, n, r, 1, c) * eye[None, :, None, :, None]).reshape(nb, n * r, n * c)


def _diag_blocks(dense, n):
    nb = dense.shape[0]
    r, c = dense.shape[1] // n, dense.shape[2] // n
    eye = jnp.eye(n, dtype=dense.dtype)
    return jnp.sum(dense.reshape(nb, n, r, n, c) * eye[None, :, None, :, None], axis=3).reshape(nb * n, r, c)


def kernel(x, ffn1_w_gate, ffn1_w_up, ffn1_w_down, ln1_g, ln1_b, w_in, conv_w, conv_b, rg_w_a, rg_b_a, rg_w_x, rg_b_x, rg_lambda, fox_b_f, s5_a_re, s5_a_im, s5_log_dt, s5_b_re, s5_b_im, s5_c_re, s5_c_im, s5_d, s5_w_glu, mix_norm_g, w_out, ln2_g, ln2_b, ffn2_w_gate, ffn2_w_up, ffn2_w_down, ln3_g, ln3_b, loss_target, m_ffn1_w_gate, m_ffn1_w_up, m_ffn1_w_down, m_ln1_g, m_ln1_b, m_w_in, m_conv_w, m_conv_b, m_rg_w_a, m_rg_b_a, m_rg_w_x, m_rg_b_x, m_rg_lambda, m_fox_b_f, m_s5_a_re, m_s5_a_im, m_s5_log_dt, m_s5_b_re, m_s5_b_im, m_s5_c_re, m_s5_c_im, m_s5_d, m_s5_w_glu, m_mix_norm_g, m_w_out, m_ln2_g, m_ln2_b, m_ffn2_w_gate, m_ffn2_w_up, m_ffn2_w_down, m_ln3_g, m_ln3_b, v_ffn1_w_gate, v_ffn1_w_up, v_ffn1_w_down, v_ln1_g, v_ln1_b, v_w_in, v_conv_w, v_conv_b, v_rg_w_a, v_rg_b_a, v_rg_w_x, v_rg_b_x, v_rg_lambda, v_fox_b_f, v_s5_a_re, v_s5_a_im, v_s5_log_dt, v_s5_b_re, v_s5_b_im, v_s5_c_re, v_s5_c_im, v_s5_d, v_s5_w_glu, v_mix_norm_g, v_w_out, v_ln2_g, v_ln2_b, v_ffn2_w_gate, v_ffn2_w_up, v_ffn2_w_down, v_ln3_g, v_ln3_b):
    a = dict(locals())
    w = {n: a[n] for n in WEIGHTS}
    t, d = x.shape[1], x.shape[2]
    f = ffn1_w_down.shape[1] * 8
    fs, ds = f // 8, d // 8
    mx, my, mc = lax.axis_index("x"), lax.axis_index("y"), lax.axis_index("c")
    me = 4 * mx + 2 * my + mc
    tm = _tile(t, 512)
    tf = f // 2
    win_rows = ds * Z_W // d

    FFN1, MIXW, FFN2 = ['g1', 'u1', 'd1'], ['win', 'wout', 'glu', 'conv'], ['g2', 'u2', 'd2']
    glu_rows = D_C * D_C // (8 * d)

    def shard_segs(l):
        wi = w['w_in'][l]
        win_p = jnp.concatenate([wi[:, :Z_F + N_HEADS], jnp.zeros((ds, Z_U - Z_F - N_HEADS), F32), wi[:, Z_F + N_HEADS:]], axis=1)
        conv_bits = lax.bitcast_convert_type(w['conv_w'][l], BF16).reshape(1, -1)
        segs = dict(g1=w['ffn1_w_gate'][l].T, u1=w['ffn1_w_up'][l].T, d1=w['ffn1_w_down'][l],
                    g2=w['ffn2_w_gate'][l].T, u2=w['ffn2_w_up'][l].T, d2=w['ffn2_w_down'][l],
                    win=win_p.reshape(win_rows, d), wout=w['w_out'][l], glu=_pad_rows(w['s5_w_glu'][l].reshape(-1, d), 16))
        segs = {k: v.astype(BF16) for k, v in segs.items()}
        segs['conv'] = _pad_rows(jnp.pad(conv_bits, ((0, 0), (0, d - conv_bits.shape[1]))), 16)
        return segs

    shards = [shard_segs(l) for l in range(DEPTH)]
    wts = {}

    def cat(keys):
        return [shards[l][k] for l, k in keys]

    def split(gs, keys):
        for (l, k), g in zip(keys, gs):
            wts[(l, k)] = g

    grp_a = [(0, k) for k in FFN1]
    grp_b = [(0, k) for k in MIXW]
    grp_c = [(0, k) for k in FFN2]
    grp_d = [(1, k) for k in FFN1]
    grp_e = [(1, k) for k in FFN2]
    grp_f = [(1, k) for k in MIXW]
    split(_run(_ag_sibling(_run(_ag_chips(cat(grp_a)), "ag_chips")), "ag_sibling"), grp_a)

    xs = x[0]
    saved = []
    cur = xs
    for l in range(DEPTH):
        row = lambda n: w[n][l].reshape(1, -1)
        ffn = lambda keys: tuple(wts[(l, k)].reshape(f, d) for k in keys)
        wa = _block_diag(w['rg_w_a'][l], 3)
        wx = _block_diag(w['rg_w_x'][l], 3)
        bf = jnp.pad(row('fox_b_f'), ((0, 0), (0, LANES - N_HEADS)))
        s5p = (w['s5_a_re'][l], w['s5_a_im'][l], w['s5_log_dt'][l].reshape(-1, 1),
               w['s5_b_re'][l].transpose(0, 2, 1), w['s5_b_im'][l].transpose(0, 2, 1))
        ab_re, ab_im, bb_re, bb_im = _s5_prep(*s5p)
        bd_re, bd_im = _block_diag(bb_re, 2), _block_diag(bb_im, 2)
        cd_re = _block_diag(w['s5_c_re'][l].transpose(0, 2, 1), 2)
        cd_im = _block_diag(w['s5_c_im'][l].transpose(0, 2, 1), 2)
        abr, abi = ab_re.reshape(2, 1, N_STATE // 2), ab_im.reshape(2, 1, N_STATE // 2)
        gm = row('mix_norm_g')
        ga, gb, gc = gm[:, :D_A], gm[:, D_A:D_A + D_B], gm[:, D_A + D_B:]

        x0 = cur
        ffn1 = ffn(FFN1)
        (x1, pre1, gs1, us1), cres = _ffn_fwd(x0, *ffn1, row('ln1_g'), row('ln1_b'), tm, tf,
                                              carry=_ag_chips(cat(grp_b if l == 0 else grp_e)))
        if l == 0:
            split(_run(_ag_sibling(cres), "ag_sibling"), grp_b)
        else:
            g_e = cres
        win = wts[(l, 'win')].reshape(d, Z_W)
        wout = wts[(l, 'wout')].reshape(d, d).astype(F32)
        wglu = wts[(l, 'glu')][:, :glu_rows].reshape(D_C, D_C).astype(F32)
        conv_full = lax.bitcast_convert_type(
            wts[(l, 'conv')][:, 0, :2 * CONV_WIDTH * D_A // 8].reshape(8, CONV_WIDTH, D_A // 8, 2), F32)
        conv_full = conv_full.transpose(1, 0, 2).reshape(CONV_WIDTH, D_A)
        z = _mm(x1, win, 'nn', F32, tm, Z_W, d, "mix_in")
        out_a, h_a = _rg_fwd(z, conv_full, row('conv_b'), wa, row('rg_b_a'), wx, row('rg_b_x'), row('rg_lambda'))
        cs = _fgate_fwd(z, bf)
        crow = cs[:, :N_HEADS].T.reshape(N_HEADS, 1, t)
        (out_b, lse), cres = _attn_fwd(z, crow, carry=_ag_chips(cat(grp_c)) if l == 0 else _ag_sibling(g_e))
        if l == 0:
            (yc,), cres = _s5_fwd(z, bd_re, bd_im, abr, abi, cd_re, cd_im, row('s5_d'),
                                  carry=_join(_ag_sibling(cres), _ag_chips(cat(grp_f))))
            split(cres[:len(grp_c)], grp_c)
            g_f = cres[len(grp_c):]
        else:
            split(cres, grp_e)
            (yc,), _ = _s5_fwd(z, bd_re, bd_im, abr, abi, cd_re, cd_im, row('s5_d'))
        mix_params = [ga, gb, gc, wglu, wout, row('ln2_g'), row('ln2_b')]
        (x2,) = _rowwise(_mix_out, [out_a, out_b, yc, x1], mix_params, [(d, F32)], tm, "mix_out")
        ffn2 = ffn(FFN2)
        (x3, pre3, gs2, us2), cres = _ffn_fwd(x2, *ffn2, row('ln3_g'), row('ln3_b'), tm, tf,
                                              carry=_join(_ag_chips(cat(grp_d)), _ag_sibling(g_f)) if l == 0 else None)
        if l == 0:
            split(cres[len(grp_d):], grp_f)
            split(_run(_ag_sibling(cres[:len(grp_d)]), "ag_sibling"), grp_d)
        saved.append(dict(x0=x0, x1=x1, pre1=pre1, gs1=gs1, us1=us1, z=z, out_a=out_a, h_a=h_a, crow=crow,
                          out_b=out_b, lse=lse, yc=yc, x2=x2, pre3=pre3, gs2=gs2, us2=us2, mix_params=mix_params,
                          ffn1=ffn1, ffn2=ffn2, win=win, conv_full=conv_full, wa=wa, wx=wx, bf=bf, s5p=s5p,
                          s5m=(bd_re, bd_im, abr, abi, cd_re, cd_im)))
        cur = x3

    dy, loss_row = _loss_head(cur, loss_target[0], tm)
    loss = lax.psum(loss_row[0, 0], ("x", "y", "c"))

    assert DEPTH == 2
    small_grads = {}
    small_names = ['conv_w'] + SMALL
    c_idx = jnp.reshape(mc, (1,)).astype(jnp.int32)
    chip_idx = jnp.reshape(2 * mx + my, (1,)).astype(jnp.int32)

    def blocks(arrs):
        return jnp.concatenate([v.astype(BF16).reshape(8, -1, d) for v in arrs], axis=1)

    def mixer_blocks(dwin, dwout, dwglu):
        glu = jnp.pad(dwglu.astype(BF16).reshape(8, glu_rows, d), ((0, 0), (0, 32 - glu_rows), (0, 0)))
        return jnp.concatenate([dwin.reshape(8, win_rows, d), dwout.astype(BF16).reshape(8, ds, d), glu], axis=1)

    def pair(full, ra):
        return _pair_add(full, ra, c_idx, "rs_add_sibling", BF16)

    for l in reversed(range(DEPTH)):
        s = saved[l]
        row = lambda n: w[n][l].reshape(1, -1)
        wg_tiles = dict(tm=tf, tn=d, tk=_tile(t, 2048))
        first = l == 0

        def ffn_back(dyv, pre, gs, us, wts3, xin, ln_g, ln_b, carry_fn=None, pipeline=None):
            (dpre,), (dlg, dlb), _ = _rowwise_vjp(_ln_only, [pre], [ln_g, ln_b], [dyv], tm, "ln_bwd")
            (dx, dg, du, hh), cres = _ffn_bwd(dpre, gs, us, *wts3, tm, tf, carry=carry_fn(dlg, dlb) if carry_fn else None)
            if pipeline is None:
                dwg = _mm(dg, xin, 'tn', BF16, name="ffn_dw_gate", **wg_tiles)
                dwu = _mm(du, xin, 'tn', BF16, name="ffn_dw_up", **wg_tiles)
                dwd = _mm(hh, dpre, 'tn', BF16, name="ffn_dw_down", **wg_tiles)
                return dx, (dwg, dwu, dwd), dlg, dlb, cres

            def front(dw):
                full = dw.reshape(8, fs, d)
                (ra,) = _run(_rs_sibling(full), "rs_sibling")
                return pair(full, ra)

            dwg, (rb_first,) = _mm(dg, xin, 'tn', BF16, name="ffn_dw_gate", carry=_rs_chips(pipeline), **wg_tiles)
            cres = list(cres) + [rb_first]
            p_g = front(dwg)
            dwu, (rb_g,) = _mm(du, xin, 'tn', BF16, name="ffn_dw_up", carry=_rs_chips(p_g), **wg_tiles)
            p_u = front(dwu)
            dwd, (rb_u,) = _mm(hh, dpre, 'tn', BF16, name="ffn_dw_down", carry=_rs_chips(p_u), **wg_tiles)
            p_d = front(dwd)
            (rb_d,) = _run(_rs_chips(p_d), "rs_chips")
            return dx, ((p_g, rb_g), (p_u, rb_u), (p_d, rb_d)), dlg, dlb, cres

        dx2, (dwg2, dwu2, dwd2), dl3g, dl3b, cres = ffn_back(
            dy, s['pre3'], s['gs2'], s['us2'], s['ffn2'], s['x2'], row('ln3_g'), row('ln3_b'),
            (lambda *_: _rs_sibling(full_l1)) if first else None)
        if first:
            part_l1 = pair(full_l1, cres[0])
            full_c2 = blocks([dwg2, dwu2, dwd2])
        (d_oa, d_ob, d_yc, d_x1), (dga, dgb, dgc, dwglu, dwout, dl2g, dl2b), cres = _rowwise_vjp(
            _mix_out, [s['out_a'], s['out_b'], s['yc'], s['x1']], s['mix_params'], [dx2], tm, "mix_out_bwd",
            carry=_rs_sibling(full_c2) if first else None)
        if first:
            part_c2 = pair(full_c2, cres[0])
        (d_ax, d_ag, dcw, dcb, dwa, dba, dwx, dbx, dlam) = _rg_bwd(
            s['z'], s['h_a'], d_oa, s['conv_full'], row('conv_b'), s['wa'], row('rg_b_a'), s['wx'], row('rg_b_x'), row('rg_lambda'))
        (dq, dk, dv, dcrow), cres = _attn_bwd(s['z'], s['crow'], s['lse'], d_ob,
                                              carry=_join(_rs_chips(part_l1), _rs_chips(part_c2)) if first else None)
        if first:
            rb_l1, rb_c2 = cres
        dc_pad = jnp.pad(dcrow.reshape(N_HEADS, t).T, ((0, 0), (0, LANES - N_HEADS)))
        dzf, dbf = _fgate_bwd(s['z'], s['bf'], dc_pad)
        du_c, dbd_re, dbd_im, dabr, dabi, dcd_re, dcd_im, dd = _s5_bwd(s['z'], d_yc, *s['s5m'], row('s5_d'))
        dz = jnp.concatenate([d_ax, d_ag, dq, dk, dv, dzf, du_c], axis=1)
        dx1 = _mm(dz, s['win'], 'nt', F32, tm, d, Z_W, "mix_in_dx", add=d_x1)
        dwin = _mm(s['x1'], dz, 'tn', BF16, d, Z_W, tm, "mix_in_dw")
        dbb_re, dbb_im = _diag_blocks(dbd_re, N_GROUPS // 2), _diag_blocks(dbd_im, N_GROUPS // 2)
        dcm_re, dcm_im = _diag_blocks(dcd_re, N_GROUPS // 2), _diag_blocks(dcd_im, N_GROUPS // 2)
        da_re, da_im, dlog_dt, db_re, db_im = _s5_prep_bwd(*s['s5p'], dabr.reshape(N_GROUPS, C_STATE), dabi.reshape(N_GROUPS, C_STATE), dbb_re, dbb_im)
        sg = dict(conv_w=dcw, conv_b=dcb, rg_w_a=_diag_blocks(dwa, 2), rg_b_a=dba,
                  rg_w_x=_diag_blocks(dwx, 2), rg_b_x=dbx, rg_lambda=dlam, fox_b_f=dbf[:, :N_HEADS],
                  s5_a_re=da_re, s5_a_im=da_im, s5_log_dt=dlog_dt, s5_b_re=db_re.transpose(0, 2, 1), s5_b_im=db_im.transpose(0, 2, 1),
                  s5_c_re=dcm_re.transpose(0, 2, 1), s5_c_im=dcm_im.transpose(0, 2, 1), s5_d=dd,
                  mix_norm_g=jnp.concatenate([dga, dgb, dgc], axis=1), ln2_g=dl2g, ln2_b=dl2b, ln3_g=dl3g, ln3_b=dl3b)
        small_grads[l] = sg

        if first:
            full_m = mixer_blocks(dwin, dwout, dwglu)
            (ra_m,) = _run(_rs_sibling(full_m), "rs_sibling")
            part_m = pair(full_m, ra_m)

            def last_carry(dlg, dlb):
                sg.update(ln1_g=dlg, ln1_b=dlb)
                packed = _pack_small([small_grads[ll][n] for n in small_names for ll in range(DEPTH)])
                return _gather_small(packed)

            dx0, tail, _, _, (all_small, rb_m) = ffn_back(dx1, s['pre1'], s['gs1'], s['us1'], s['ffn1'], s['x0'], row('ln1_g'),
                                                          row('ln1_b'), last_carry, pipeline=part_m)
        else:
            dx0, (dwg1, dwu1, dwd1), dl1g, dl1b, _ = ffn_back(dx1, s['pre1'], s['gs1'], s['us1'], s['ffn1'], s['x0'],
                                                              row('ln1_g'), row('ln1_b'))
            sg.update(ln1_g=dl1g, ln1_b=dl1b)
            full_l1 = jnp.concatenate([blocks([dwg1, dwu1, dwd1, dwg2, dwu2, dwd2]), mixer_blocks(dwin, dwout, dwglu)], axis=1)
        dy = dx0

    grad_x = dy.reshape(x.shape)
    quad = lambda part, rb: _quad_add(part, rb, chip_idx, "rs_add_chips")
    own = {}

    def take(rows_f32, keys, l):
        off = 0
        for k, r in keys:
            own[(l, k)] = rows_f32[off:off + r]
            off += r

    ffn_keys = lambda names: [(k, fs) for k in names]
    mix_keys = [('win', win_rows), ('wout', ds), ('glu', glu_rows)]
    take(quad(part_l1, rb_l1), ffn_keys(FFN1 + FFN2) + mix_keys, 1)
    take(quad(part_c2, rb_c2), ffn_keys(FFN2), 0)
    take(quad(part_m, rb_m), mix_keys, 0)
    for k, (part, rb) in zip(FFN1, tail):
        own[(0, k)] = quad(part, rb)

    grads = {}
    grads_t = {}
    for k, n in zip(FFN1 + FFN2, ['ffn1_w_gate', 'ffn1_w_up', 'ffn1_w_down', 'ffn2_w_gate', 'ffn2_w_up', 'ffn2_w_down']):
        stacked = jnp.stack([own[(l, k)] for l in range(DEPTH)])
        if 'down' in n:
            grads[n] = stacked
        else:
            grads_t[n] = stacked
            grads[n] = jnp.swapaxes(stacked, 1, 2)
    gwin = jnp.stack([own[(l, 'win')].reshape(ds, Z_W) for l in range(DEPTH)])
    grads['w_in'] = jnp.concatenate([gwin[:, :, :Z_F + N_HEADS], gwin[:, :, Z_U:]], axis=2)
    grads['w_out'] = jnp.stack([own[(l, 'wout')] for l in range(DEPTH)])
    grads['s5_w_glu'] = jnp.stack([own[(l, 'glu')].reshape(D_C // 8, D_C) for l in range(DEPTH)])

    small_shapes = [((DEPTH, CONV_WIDTH, D_A) if n == 'conv_w' else w[n].shape) for n in small_names]
    conv_zero = jnp.zeros((DEPTH, CONV_WIDTH, D_A), F32)
    summed = _sum_small(all_small)
    for n, g in zip(small_names, _unpack_small(summed, small_shapes)):
        grads[n] = g
    grads['conv_w'] = lax.dynamic_slice_in_dim(grads['conv_w'], me * (D_A // 8), D_A // 8, axis=2)

    delta, new_m, new_v = {}, {}, {}
    for n in BIG + ['conv_w']:
        if n in grads_t:
            sh = grads_t[n].shape
            two = lambda v: jnp.swapaxes(v, 1, 2).reshape(-1, sh[-1])
            back = lambda v: jnp.swapaxes(v.reshape(sh), 1, 2)
            g2 = grads_t[n].reshape(-1, sh[-1])
        else:
            sh = w[n].shape
            two = lambda v: v.reshape(-1, sh[-1])
            back = lambda v: v.reshape(sh)
            g2 = two(grads[n])
        dl, nm, nv = _adamw(two(w[n]), g2, two(a['m_' + n]), two(a['v_' + n]), "adamw_" + n)
        delta[n], new_m[n], new_v[n] = back(dl), back(nm), back(nv)
    dl, nm, nv = _adamw(_pack_small([conv_zero] + [w[n] for n in SMALL]), summed,
                        _pack_small([conv_zero] + [a['m_' + n] for n in SMALL]),
                        _pack_small([conv_zero] + [a['v_' + n] for n in SMALL]), "adamw_small")
    for n, v1, v2, v3 in zip(small_names[1:], _unpack_small(dl, small_shapes)[1:], _unpack_small(nm, small_shapes)[1:],
                             _unpack_small(nv, small_shapes)[1:]):
        delta[n], new_m[n], new_v[n] = v1, v2, v3

    return (loss, grad_x, *[grads[n] for n in WEIGHTS], *[delta[n] for n in WEIGHTS], *[new_m[n] for n in WEIGHTS],
            *[new_v[n] for n in WEIGHTS])
```

```python
import functools
import math

import jax
import jax.numpy as jnp
import numpy as np
from jax import lax
from jax.experimental import pallas as pl
from jax.experimental.pallas import tpu as pltpu

F32 = jnp.float32
BF16 = jnp.bfloat16
MESH = pl.DeviceIdType.MESH

DEPTH = 2
ALPHA = (2 * DEPTH) ** 0.25
LN_EPS = 1e-5
RMS_EPS = 1e-6
RG_C = 8.0
CONV_WIDTH = 4
HEAD_DIM = 64
C_GROUP = 16
C_STATE = 64
D_A = 384
D_B = 384
D_C = 256
N_HEADS = D_B // HEAD_DIM
N_GROUPS = D_C // C_GROUP
N_STATE = N_GROUPS * C_STATE
Z_F = 2 * D_A + 3 * D_B
Z_U = Z_F + 128
Z_W = Z_U + D_C
N_IN = Z_F + N_HEADS + D_C
ADAM_LR, ADAM_B1, ADAM_B2, ADAM_EPS, ADAM_WD, ADAM_STEP = 0.001, 0.9, 0.999, 1e-08, 0.01, 10
LANES = 128
NEG = -1e30

WEIGHTS = ['ffn1_w_gate', 'ffn1_w_up', 'ffn1_w_down', 'ln1_g', 'ln1_b', 'w_in', 'conv_w', 'conv_b', 'rg_w_a', 'rg_b_a',
           'rg_w_x', 'rg_b_x', 'rg_lambda', 'fox_b_f', 's5_a_re', 's5_a_im', 's5_log_dt', 's5_b_re', 's5_b_im', 's5_c_re',
           's5_c_im', 's5_d', 's5_w_glu', 'mix_norm_g', 'w_out', 'ln2_g', 'ln2_b', 'ffn2_w_gate', 'ffn2_w_up', 'ffn2_w_down',
           'ln3_g', 'ln3_b']
BIG = ['ffn1_w_gate', 'ffn1_w_up', 'ffn1_w_down', 'w_in', 's5_w_glu', 'w_out', 'ffn2_w_gate', 'ffn2_w_up', 'ffn2_w_down']
SMALL_TAIL = ['fox_b_f', 's5_log_dt']
SMALL = [n for n in WEIGHTS if n not in BIG and n != 'conv_w' and n not in SMALL_TAIL] + SMALL_TAIL


def _sig(x):
    return 1.0 / (1.0 + jnp.exp(-x))


def _gelu(x):
    return 0.5 * x * (1.0 + jnp.tanh(math.sqrt(2.0 / math.pi) * (x + 0.044715 * (x * x * x))))


def _softplus(x):
    return jnp.maximum(x, 0.0) + jnp.log(1.0 + jnp.exp(jnp.minimum(x, -x)))


def _dot(a, b, dims):
    return lax.dot_general(a.astype(BF16), b.astype(BF16), (dims, ((), ())), preferred_element_type=F32)


NN = ((1,), (0,))
NT = ((1,), (1,))
TN = ((0,), (0,))


@jax.custom_vjp
def _bdot(a, w):
    return _dot(a, w, NN)


def _bdot_fwd(a, w):
    return _dot(a, w, NN), (a, w)


def _bdot_bwd(res, ct):
    a, w = res
    return _dot(ct, w, NT), _dot(a, ct, TN)


_bdot.defvjp(_bdot_fwd, _bdot_bwd)


def _ln(pre, g, b):
    mu = jnp.mean(pre, axis=-1, keepdims=True)
    xc = pre - mu
    var = jnp.mean(xc * xc, axis=-1, keepdims=True)
    return xc * lax.rsqrt(var + LN_EPS) * g + b


def _rms(x, g):
    return x * lax.rsqrt(jnp.mean(x * x, axis=-1, keepdims=True) + RMS_EPS) * g


def _tile(n, want):
    return want if n % want == 0 else n


class _Carry:
    def __init__(self, ins, outs, aliases, plan, n):
        self.ins, self.outs, self.aliases, self.plan, self.n = list(ins), list(outs), dict(aliases), plan, n


def _join(a, b):
    na, ma = len(a.ins), len(a.outs)

    def plan(x, y, c, ins, outs):
        return a.plan(x, y, c, ins[:na], outs[:ma]) + b.plan(x, y, c, ins[na:], outs[ma:])

    aliases = dict(a.aliases)
    aliases.update({na + i: ma + j for i, j in b.aliases.items()})
    return _Carry(a.ins + b.ins, a.outs + b.outs, aliases, plan, a.n + b.n)


def _copies(carry, cins, couts, send, recv):
    x, y, c = lax.axis_index("x"), lax.axis_index("y"), lax.axis_index("c")
    res = []
    for k, (s, d, peer) in enumerate(carry.plan(x, y, c, cins, couts)):
        if peer is None:
            res.append(pltpu.make_async_copy(s, d, send.at[k]))
        else:
            res.append(pltpu.make_async_remote_copy(src_ref=s, dst_ref=d, send_sem=send.at[k], recv_sem=recv.at[k],
                                                    device_id=peer, device_id_type=MESH))
    return res


def _call(body, grid, in_specs, out_specs, out_shape, scratch, semantics, name, args, carry=None):
    n_in, n_out, n_scr = len(in_specs), len(out_specs), len(scratch)
    if carry is None:
        res = pl.pallas_call(body, grid=grid, in_specs=in_specs, out_specs=out_specs, out_shape=out_shape,
                             scratch_shapes=scratch, compiler_params=pltpu.CompilerParams(dimension_semantics=semantics),
                             name=name)(*args)
        return list(res), []
    nci, nco = len(carry.ins), len(carry.outs)

    def wrapped(*refs):
        o0 = n_in + nci
        s0 = o0 + n_out + nco
        cins, couts = refs[n_in:o0], refs[o0 + n_out:s0]
        send, recv = refs[s0 + n_scr:]
        first = functools.reduce(jnp.logical_and, [pl.program_id(k) == 0 for k in range(len(grid))])
        last = functools.reduce(jnp.logical_and, [pl.program_id(k) == grid[k] - 1 for k in range(len(grid))])

        @pl.when(first)
        def _():
            for cp in _copies(carry, cins, couts, send, recv):
                cp.start()

        body(*refs[:n_in], *refs[o0:o0 + n_out], *refs[s0:s0 + n_scr])

        @pl.when(last)
        def _():
            for cp in _copies(carry, cins, couts, send, recv):
                cp.wait()

    hbm = pl.BlockSpec(memory_space=pl.ANY)
    res = pl.pallas_call(
        wrapped, grid=grid, in_specs=list(in_specs) + [hbm] * nci, out_specs=list(out_specs) + [hbm] * nco,
        out_shape=list(out_shape) + carry.outs, scratch_shapes=list(scratch) + [pltpu.SemaphoreType.DMA((carry.n,))] * 2,
        input_output_aliases={n_in + i: n_out + j for i, j in carry.aliases.items()},
        compiler_params=pltpu.CompilerParams(dimension_semantics=("arbitrary",) * len(grid), has_side_effects=True),
        name=name)(*args, *carry.ins)
    return list(res[:n_out]), list(res[n_out:])


def _run(carry, name):
    nci, nco = len(carry.ins), len(carry.outs)

    def body(*refs):
        cps = _copies(carry, refs[:nci], refs[nci:nci + nco], refs[-2], refs[-1])
        for cp in cps:
            cp.start()
        for cp in cps:
            cp.wait()

    hbm = pl.BlockSpec(memory_space=pl.ANY)
    return pl.pallas_call(
        body, in_specs=[hbm] * nci, out_specs=[hbm] * nco, out_shape=carry.outs, input_output_aliases=carry.aliases,
        scratch_shapes=[pltpu.SemaphoreType.DMA((carry.n,))] * 2, compiler_params=pltpu.CompilerParams(has_side_effects=True),
        name=name)(*carry.ins)


def _ag_chips(shards):
    def plan(x, y, c, ins, outs):
        res = []
        for src, out in zip(ins, outs):
            dst = out.at[4 * x + 2 * y + c]
            res += [(src, dst, None)] + [(src, dst, (px, py, c)) for px, py in ((1 - x, y), (x, 1 - y), (1 - x, 1 - y))]
        return res

    return _Carry(shards, [jax.ShapeDtypeStruct((8,) + s.shape, s.dtype) for s in shards], {}, plan, 4 * len(shards))


def _ag_sibling(gs):
    def plan(x, y, c, ins, outs):
        return [(out.at[2 * q + c], out.at[2 * q + c], (x, y, 1 - c)) for out in outs for q in range(4)]

    return _Carry(gs, [jax.ShapeDtypeStruct(g.shape, g.dtype) for g in gs], {i: i for i in range(len(gs))}, plan, 4 * len(gs))


def _rs_sibling(full):
    def plan(x, y, c, ins, outs):
        return [(ins[0].at[2 * q + (1 - c)], outs[0].at[q], (x, y, 1 - c)) for q in range(4)]

    return _Carry([full], [jax.ShapeDtypeStruct((4,) + full.shape[1:], full.dtype)], {}, plan, 4)


def _rs_chips(part):
    def plan(x, y, c, ins, outs):
        res = []
        for k, (dx, dy) in enumerate(((1, 0), (0, 1), (1, 1))):
            tx, ty = x ^ dx, y ^ dy
            res.append((ins[0].at[2 * tx + ty], outs[0].at[k], (tx, ty, c)))
        return res

    return _Carry([part], [jax.ShapeDtypeStruct((3,) + part.shape[1:], part.dtype)], {}, plan, 3)


def _mm(a, b, dims, out_dtype, tm, tn, tk, name, add=None, carry=None):
    if dims == 'nn':
        (m, k), n = a.shape, b.shape[1]
        a_spec = pl.BlockSpec((tm, tk), lambda i, j, q: (i, q))
        b_spec = pl.BlockSpec((tk, tn), lambda i, j, q: (q, j))
        dn = NN
    elif dims == 'nt':
        (m, k), n = a.shape, b.shape[0]
        a_spec = pl.BlockSpec((tm, tk), lambda i, j, q: (i, q))
        b_spec = pl.BlockSpec((tn, tk), lambda i, j, q: (j, q))
        dn = NT
    else:
        (k, m), n = a.shape, b.shape[1]
        a_spec = pl.BlockSpec((tk, tm), lambda i, j, q: (q, i))
        b_spec = pl.BlockSpec((tk, tn), lambda i, j, q: (q, j))
        dn = TN
    nk = k // tk
    o_spec = pl.BlockSpec((tm, tn), lambda i, j, q: (i, j))

    def body(*refs):
        if add is None:
            a_ref, b_ref, o_ref, acc_ref = refs
        else:
            a_ref, b_ref, add_ref, o_ref, acc_ref = refs
        q = pl.program_id(2)
        part = _dot(a_ref[...], b_ref[...], dn)

        @pl.when(q == 0)
        def _():
            acc_ref[...] = part

        @pl.when(q > 0)
        def _():
            acc_ref[...] += part

        @pl.when(q == nk - 1)
        def _():
            r = acc_ref[...]
            if add is not None:
                r = r + add_ref[...]
            o_ref[...] = r.astype(o_ref.dtype)

    ins = [a, b] + ([] if add is None else [add])
    specs = [a_spec, b_spec] + ([] if add is None else [o_spec])
    (res,), cres = _call(body, (m // tm, n // tn, nk), specs, [o_spec], [jax.ShapeDtypeStruct((m, n), out_dtype)],
                         [pltpu.VMEM((tm, tn), F32)], ("parallel", "parallel", "arbitrary"), name, ins, carry)
    return res if carry is None else (res, cres)


def _rowwise(fn, rows, params, outs, tm, name):
    t = rows[0].shape[0]
    nr, npar = len(rows), len(params)

    def body(*refs):
        r = [x[...] for x in refs[:nr]]
        p = [x[...] for x in refs[nr:nr + npar]]
        res = fn(*r, *p)
        for o_ref, o in zip(refs[nr + npar:], res):
            o_ref[...] = o.astype(o_ref.dtype)

    in_specs = ([pl.BlockSpec((tm, a.shape[1]), lambda i: (i, 0)) for a in rows]
                + [pl.BlockSpec(p.shape, lambda i: (0, 0)) for p in params])
    return pl.pallas_call(
        body, grid=(t // tm,), in_specs=in_specs,
        out_specs=[pl.BlockSpec((tm, c), lambda i: (i, 0)) for c, _ in outs],
        out_shape=[jax.ShapeDtypeStruct((t, c), d) for c, d in outs],
        compiler_params=pltpu.CompilerParams(dimension_semantics=("parallel",)), name=name)(*rows, *params)


def _rowwise_vjp(fn, rows, params, cots, tm, name, carry=None):
    t = rows[0].shape[0]
    nr, npar, nc = len(rows), len(params), len(cots)

    def body(*refs):
        r = [x[...] for x in refs[:nr]]
        p = [x[...] for x in refs[nr:nr + npar]]
        c = [x[...] for x in refs[nr + npar:nr + npar + nc]]
        o_refs = refs[nr + npar + nc:]
        _, pull = jax.vjp(fn, *r, *p)
        grads = pull(tuple(c))
        for o_ref, g in zip(o_refs[:nr], grads[:nr]):
            o_ref[...] = g
        i = pl.program_id(0)

        @pl.when(i == 0)
        def _():
            for o_ref, g in zip(o_refs[nr:], grads[nr:]):
                o_ref[...] = g

        @pl.when(i > 0)
        def _():
            for o_ref, g in zip(o_refs[nr:], grads[nr:]):
                o_ref[...] += g

    row_spec = lambda a: pl.BlockSpec((tm, a.shape[1]), lambda i: (i, 0))
    par_spec = lambda p: pl.BlockSpec(p.shape, lambda i: (0, 0))
    res, cres = _call(
        body, (t // tm,),
        [row_spec(a) for a in rows] + [par_spec(p) for p in params] + [row_spec(a) for a in cots],
        [row_spec(a) for a in rows] + [par_spec(p) for p in params],
        [jax.ShapeDtypeStruct(a.shape, F32) for a in rows] + [jax.ShapeDtypeStruct(p.shape, F32) for p in params],
        [], ("arbitrary",), name, [*rows, *params, *cots], carry)
    return res[:nr], res[nr:], cres


def _ffn_fwd(x, wgt, wut, wd, ln_g, ln_b, tm, tf, carry=None):
    t, d = x.shape
    f = wgt.shape[0]
    nj = f // tf

    def body(x_ref, wg_ref, wu_ref, wd_ref, g_ref, b_ref, y_ref, pre_ref, gs_ref, us_ref, acc_ref):
        j = pl.program_id(1)
        xv = x_ref[...]
        xb = xv.astype(BF16)
        g = _dot(xb, wg_ref[...], NT)
        u = _dot(xb, wu_ref[...], NT)
        gs_ref[...] = g.astype(BF16)
        us_ref[...] = u.astype(BF16)
        part = _dot(g * _sig(g) * u, wd_ref[...], NN)

        @pl.when(j == 0)
        def _():
            acc_ref[...] = part

        @pl.when(j > 0)
        def _():
            acc_ref[...] += part

        @pl.when(j == nj - 1)
        def _():
            pre = ALPHA * xv + 0.5 * acc_ref[...]
            pre_ref[...] = pre
            y_ref[...] = _ln(pre, g_ref[...], b_ref[...])

    w_spec = pl.BlockSpec((tf, d), lambda i, j: (j, 0))
    x_spec = pl.BlockSpec((tm, d), lambda i, j: (i, 0))
    v_spec = pl.BlockSpec((1, d), lambda i, j: (0, 0))
    h_spec = pl.BlockSpec((tm, tf), lambda i, j: (i, j))
    return _call(
        body, (t // tm, nj), [x_spec, w_spec, w_spec, w_spec, v_spec, v_spec], [x_spec, x_spec, h_spec, h_spec],
        [jax.ShapeDtypeStruct((t, d), F32), jax.ShapeDtypeStruct((t, d), F32),
         jax.ShapeDtypeStruct((t, f), BF16), jax.ShapeDtypeStruct((t, f), BF16)],
        [pltpu.VMEM((tm, d), F32)], ("parallel", "arbitrary"), "ffn_fwd", [x, wgt, wut, wd, ln_g, ln_b], carry)


def _ffn_bwd(dpre, gs, us, wgt, wut, wd, tm, tf, carry=None):
    t, d = dpre.shape
    f = wgt.shape[0]
    nj = f // tf

    def body(dp_ref, gs_ref, us_ref, wg_ref, wu_ref, wd_ref, dx_ref, dg_ref, du_ref, hh_ref, acc_ref):
        j = pl.program_id(1)
        dp = dp_ref[...]
        dh = _dot(0.5 * dp, wd_ref[...], NT)
        g = gs_ref[...].astype(F32)
        u = us_ref[...].astype(F32)
        s = _sig(g)
        sl = g * s
        dg = (dh * u * (s * (1.0 + g * (1.0 - s)))).astype(BF16)
        du = (dh * sl).astype(BF16)
        dg_ref[...] = dg
        du_ref[...] = du
        hh_ref[...] = (0.5 * sl * u).astype(BF16)
        part = _dot(dg, wg_ref[...], NN) + _dot(du, wu_ref[...], NN)

        @pl.when(j == 0)
        def _():
            acc_ref[...] = part

        @pl.when(j > 0)
        def _():
            acc_ref[...] += part

        @pl.when(j == nj - 1)
        def _():
            dx_ref[...] = ALPHA * dp + acc_ref[...]

    w_spec = pl.BlockSpec((tf, d), lambda i, j: (j, 0))
    x_spec = pl.BlockSpec((tm, d), lambda i, j: (i, 0))
    h_spec = pl.BlockSpec((tm, tf), lambda i, j: (i, j))
    return _call(
        body, (t // tm, nj), [x_spec, h_spec, h_spec, w_spec, w_spec, w_spec], [x_spec, h_spec, h_spec, h_spec],
        [jax.ShapeDtypeStruct((t, d), F32)] + [jax.ShapeDtypeStruct((t, f), BF16)] * 3,
        [pltpu.VMEM((tm, d), F32)], ("parallel", "arbitrary"), "ffn_bwd", [dpre, gs, us, wgt, wut, wd], carry)


def _scan8(a_ref, b_ref, out_ref, t, reverse=False):
    w = out_ref.shape[-1]
    sub = lax.broadcasted_iota(jnp.int32, (8, w), 0)

    def step(g, carry):
        r0 = pl.multiple_of((t // 8 - 1 - g if reverse else g) * 8, 8)
        bv = b_ref[pl.ds(r0, 8), :]
        av = None if a_ref is None else a_ref[pl.ds(r0, 8), :]
        for s in (1, 2, 4):
            ok = (sub < 8 - s) if reverse else (sub >= s)
            shift = 8 - s if reverse else s
            b_sh = jnp.where(ok, pltpu.roll(bv, shift, 0), 0.0)
            if av is None:
                bv = bv + b_sh
            else:
                bv = av * b_sh + bv
                av = av * jnp.where(ok, pltpu.roll(av, shift, 0), 1.0)
        h = bv + carry if av is None else bv + av * carry
        out_ref[pl.ds(r0, 8), :] = h
        return jnp.sum(jnp.where(sub == (0 if reverse else 7), h, 0.0), axis=0, keepdims=True)

    lax.fori_loop(0, t // 8, step, jnp.zeros((1, w), F32))


def _rg_local(xa, wa, ba, wx, bx, lam):
    r = _sig(_bdot(xa, wa) + ba)
    i = _sig(_bdot(xa, wx) + bx)
    log_a = -RG_C * r * _softplus(-lam)
    a = jnp.exp(log_a)
    mult = jnp.sqrt(-jnp.tanh(log_a) * (a * a + 1.0))
    return a, mult * (i * xa)


def _conv_taps(ext, n):
    return [ext[8:, :]] + [pltpu.roll(ext, s, 0)[8:, :] for s in (1, 2, 3)]


def _rg_fwd(z, cw, cb, wa, ba, wx, bx, lam):
    t = z.shape[0]
    cr = _tile(t, 256)
    nb = D_A // LANES

    def body(ax_ref, ag_ref, cw_ref, cb_ref, wa_ref, ba_ref, wx_ref, bx_ref, lam_ref, out_ref, h_ref, axp, a_s, b_s):
        axp[pl.ds(0, 8), :] = jnp.zeros((8, LANES), F32)
        pltpu.sync_copy(ax_ref, axp.at[pl.ds(8, t)])
        w = [cw_ref[pl.ds(k, 1), :] for k in range(CONV_WIDTH)]

        def chunk(c, carry):
            t0 = pl.multiple_of(c * cr, cr)
            taps = _conv_taps(axp[pl.ds(t0, cr + 8), :], cr)
            xa = cb_ref[...] + w[3] * taps[0] + w[2] * taps[1] + w[1] * taps[2] + w[0] * taps[3]
            a, gated = _rg_local(xa, wa_ref[...], ba_ref[...], wx_ref[...], bx_ref[...], lam_ref[...])
            a_s[pl.ds(t0, cr), :] = a
            b_s[pl.ds(t0, cr), :] = gated
            return carry

        lax.fori_loop(0, t // cr, chunk, 0)

        _scan8(a_s, b_s, h_ref, t)

        def fin(c, carry):
            t0 = pl.multiple_of(c * cr, cr)
            out_ref[pl.ds(t0, cr), :] = _gelu(ag_ref[pl.ds(t0, cr), :]) * h_ref[pl.ds(t0, cr), :]
            return carry

        lax.fori_loop(0, t // cr, fin, 0)

    col = lambda off: pl.BlockSpec((t, LANES), lambda b: (0, off + b))
    vec = pl.BlockSpec((1, LANES), lambda b: (0, b))
    mat = pl.BlockSpec((None, LANES, LANES), lambda b: (b, 0, 0))
    return pl.pallas_call(
        body, grid=(nb,),
        in_specs=[col(0), col(nb), pl.BlockSpec((CONV_WIDTH, LANES), lambda b: (0, b)), vec, mat, vec, mat, vec, vec],
        out_specs=[col(0), col(0)],
        out_shape=[jax.ShapeDtypeStruct((t, D_A), F32), jax.ShapeDtypeStruct((t, D_A), F32)],
        scratch_shapes=[pltpu.VMEM((t + 8, LANES), F32), pltpu.VMEM((t, LANES), F32), pltpu.VMEM((t, LANES), F32)],
        compiler_params=pltpu.CompilerParams(dimension_semantics=("arbitrary",)), name="rglru_fwd")(
            z, z, cw, cb, wa, ba, wx, bx, lam)


def _rg_bwd(z, h, dout, cw, cb, wa, ba, wx, bx, lam):
    t = z.shape[0]
    cr = _tile(t, 256)
    nb = D_A // LANES

    def body(ax_ref, ag_ref, h_ref, do_ref, cw_ref, cb_ref, wa_ref, ba_ref, wx_ref, bx_ref, lam_ref,
             dax_ref, dag_ref, dcw_ref, dcb_ref, dwa_ref, dba_ref, dwx_ref, dbx_ref, dlam_ref,
             axp, hp, xa_s, a_s, g_s, dxa_s, u_s):
        zero8 = jnp.zeros((8, LANES), F32)
        axp[pl.ds(0, 8), :] = zero8
        hp[pl.ds(0, 8), :] = zero8
        dxa_s[pl.ds(t, 8), :] = zero8
        u_s[pl.ds(t, 8), :] = zero8
        pltpu.sync_copy(ax_ref, axp.at[pl.ds(8, t)])
        pltpu.sync_copy(h_ref, hp.at[pl.ds(8, t)])
        w = [cw_ref[pl.ds(k, 1), :] for k in range(CONV_WIDTH)]
        for ref in (dcw_ref, dcb_ref, dwa_ref, dba_ref, dwx_ref, dbx_ref, dlam_ref):
            ref[...] = jnp.zeros(ref.shape, F32)

        def p1(c, carry):
            t0 = pl.multiple_of(c * cr, cr)
            taps = _conv_taps(axp[pl.ds(t0, cr + 8), :], cr)
            xa = cb_ref[...] + w[3] * taps[0] + w[2] * taps[1] + w[1] * taps[2] + w[0] * taps[3]
            a, _ = _rg_local(xa, wa_ref[...], ba_ref[...], wx_ref[...], bx_ref[...], lam_ref[...])
            xa_s[pl.ds(t0, cr), :] = xa
            a_s[pl.ds(t0, cr), :] = a
            ag = ag_ref[pl.ds(t0, cr), :]
            dov = do_ref[pl.ds(t0, cr), :]
            gel, pull = jax.vjp(_gelu, ag)
            g_s[pl.ds(t0, cr), :] = dov * gel
            u_s[pl.ds(t0, cr), :] = a * (dov * gel)
            dag_ref[pl.ds(t0, cr), :] = pull(dov * h_ref[pl.ds(t0, cr), :])[0]
            return carry

        lax.fori_loop(0, t // cr, p1, 0)
        _scan8(a_s, u_s, u_s, t, reverse=True)

        def p3(c, carry):
            t0 = pl.multiple_of(c * cr, cr)
            g = g_s[pl.ds(t0, cr), :] + pltpu.roll(u_s[pl.ds(t0, cr + 8), :], cr + 7, 0)[:cr, :]
            h_prev = pltpu.roll(hp[pl.ds(t0, cr + 8), :], 1, 0)[8:, :]
            _, pull = jax.vjp(_rg_local, xa_s[pl.ds(t0, cr), :], wa_ref[...], ba_ref[...], wx_ref[...], bx_ref[...],
                              lam_ref[...])
            dxa, dwa, dba, dwx, dbx, dlam = pull((g * h_prev, g))
            dxa_s[pl.ds(t0, cr), :] = dxa
            dwa_ref[...] += dwa
            dba_ref[...] += dba
            dwx_ref[...] += dwx
            dbx_ref[...] += dbx
            dlam_ref[...] += dlam
            return carry

        lax.fori_loop(0, t // cr, p3, 0)

        def p4(c, carry):
            t0 = pl.multiple_of(c * cr, cr)
            ext = dxa_s[pl.ds(t0, cr + 8), :]
            n = cr + 8
            ahead = [ext[:cr, :]] + [pltpu.roll(ext, n - s, 0)[:cr, :] for s in (1, 2, 3)]
            dax_ref[pl.ds(t0, cr), :] = w[3] * ahead[0] + w[2] * ahead[1] + w[1] * ahead[2] + w[0] * ahead[3]
            taps = _conv_taps(axp[pl.ds(t0, cr + 8), :], cr)
            dxa = ahead[0]
            for k in range(CONV_WIDTH):
                dcw_ref[pl.ds(k, 1), :] += jnp.sum(dxa * taps[CONV_WIDTH - 1 - k], axis=0, keepdims=True)
            dcb_ref[...] += jnp.sum(dxa, axis=0, keepdims=True)
            return carry

        lax.fori_loop(0, t // cr, p4, 0)

    col = lambda off: pl.BlockSpec((t, LANES), lambda b: (0, off + b))
    vec = pl.BlockSpec((1, LANES), lambda b: (0, b))
    mat = pl.BlockSpec((None, LANES, LANES), lambda b: (b, 0, 0))
    cws = pl.BlockSpec((CONV_WIDTH, LANES), lambda b: (0, b))
    sds = jax.ShapeDtypeStruct
    return pl.pallas_call(
        body, grid=(nb,),
        in_specs=[col(0), col(nb), col(0), col(0), cws, vec, mat, vec, mat, vec, vec],
        out_specs=[col(0), col(0), cws, vec, mat, vec, mat, vec, vec],
        out_shape=[sds((t, D_A), F32), sds((t, D_A), F32), sds((CONV_WIDTH, D_A), F32), sds((1, D_A), F32),
                   sds((nb, LANES, LANES), F32), sds((1, D_A), F32), sds((nb, LANES, LANES), F32), sds((1, D_A), F32),
                   sds((1, D_A), F32)],
        scratch_shapes=[pltpu.VMEM((t + 8, LANES), F32), pltpu.VMEM((t + 8, LANES), F32), pltpu.VMEM((t, LANES), F32),
                        pltpu.VMEM((t, LANES), F32), pltpu.VMEM((t, LANES), F32), pltpu.VMEM((t + 8, LANES), F32),
                        pltpu.VMEM((t + 8, LANES), F32)],
        compiler_params=pltpu.CompilerParams(dimension_semantics=("arbitrary",)), name="rglru_bwd")(
            z, z, h, dout, cw, cb, wa, ba, wx, bx, lam)


def _fgate_fwd(z, bf):
    t = z.shape[0]

    def body(zf_ref, bf_ref, c_ref):
        c_ref[...] = -_softplus(-(zf_ref[...] + bf_ref[...]))
        _scan8(None, c_ref, c_ref, t)

    return pl.pallas_call(
        body, grid=(1,), in_specs=[pl.BlockSpec((t, LANES), lambda i: (0, Z_F // LANES)), pl.BlockSpec((1, LANES), lambda i: (0, 0))],
        out_specs=pl.BlockSpec((t, LANES), lambda i: (0, 0)), out_shape=jax.ShapeDtypeStruct((t, LANES), F32),
        compiler_params=pltpu.CompilerParams(dimension_semantics=("arbitrary",)), name="fgate_fwd")(z, bf)


def _fgate_bwd(z, bf, dc):
    t = z.shape[0]

    def body(zf_ref, bf_ref, dc_ref, dz_ref, db_ref):
        _scan8(None, dc_ref, dz_ref, t, reverse=True)
        dz = dz_ref[...] * _sig(-(zf_ref[...] + bf_ref[...]))
        dz_ref[...] = dz
        db_ref[...] = jnp.sum(dz, axis=0, keepdims=True)

    return pl.pallas_call(
        body, grid=(1,),
        in_specs=[pl.BlockSpec((t, LANES), lambda i: (0, Z_F // LANES)), pl.BlockSpec((1, LANES), lambda i: (0, 0)),
                  pl.BlockSpec((t, LANES), lambda i: (0, 0))],
        out_specs=[pl.BlockSpec((t, LANES), lambda i: (0, 0)), pl.BlockSpec((1, LANES), lambda i: (0, 0))],
        out_shape=[jax.ShapeDtypeStruct((t, LANES), F32), jax.ShapeDtypeStruct((1, LANES), F32)],
        compiler_params=pltpu.CompilerParams(dimension_semantics=("arbitrary",)), name="fgate_bwd")(z, bf, dc)


def _cast_rows(src_ref, dst_ref, t, rows, fn):
    def cp(c, carry):
        r0 = pl.multiple_of(c * rows, rows)
        dst_ref[pl.ds(r0, rows), :] = fn(src_ref[pl.ds(r0, rows), :]).astype(dst_ref.dtype)
        return carry

    lax.fori_loop(0, t // rows, cp, 0)


def _attn_groups(t):
    tq = _tile(t, 256)
    nq = t // tq
    grp = 4 if nq % 4 == 0 else 1
    return tq, nq, grp


def _attn_fwd(z, crow, carry=None):
    t = z.shape[0]
    tq, nq, grp = _attn_groups(t)
    tk = grp * tq
    scale = HEAD_DIM ** -0.5

    def body(q_ref, k_ref, v_ref, cr_ref, o_ref, lse_ref, kb_s, vb_s):
        lane = lax.broadcasted_iota(jnp.int32, (1, LANES), 1)
        hmask = [(lane // HEAD_DIM) == hh for hh in range(2)]
        _cast_rows(k_ref, kb_s, t, tq, lambda v: v)
        _cast_rows(v_ref, vb_s, t, tq, lambda v: v)

        def qblock(g, r):
            q0 = pl.multiple_of((g * grp + r) * tq, tq)
            qv = q_ref[pl.ds(q0, tq), :] * scale
            qa = [jnp.where(hmask[hh], qv, 0.0).astype(BF16) for hh in range(2)]

            def update(st, k0, width, off):
                kb = kb_s[pl.ds(k0, width), :]
                vb = vb_s[pl.ds(k0, width), :]
                new = []
                for hh in range(2):
                    m, l, acc = st[hh]
                    s = _dot(qa[hh], kb, NT) - cr_ref[hh, :, pl.ds(k0, width)]
                    if off is not None:
                        keep = (lax.broadcasted_iota(jnp.int32, (tq, width), 0) + off
                                >= lax.broadcasted_iota(jnp.int32, (tq, width), 1))
                        s = jnp.where(keep, s, NEG)
                    m_new = jnp.maximum(m, jnp.max(s, axis=-1, keepdims=True))
                    p = jnp.exp(s - m_new)
                    corr = jnp.exp(m - m_new)
                    new.append((m_new, corr * l + jnp.sum(p, axis=-1, keepdims=True), corr * acc + _dot(p, vb, NN)))
                return tuple(new)

            one = (jnp.full((tq, 1), NEG, F32), jnp.zeros((tq, 1), F32), jnp.zeros((tq, LANES), F32))
            st = lax.fori_loop(0, g, lambda j, st: update(st, pl.multiple_of(j * tk, tk), tk, None), (one, one))
            st = update(st, pl.multiple_of(g * tk, tk), (r + 1) * tq, r * tq)
            o_ref[pl.ds(q0, tq), :] = jnp.where(hmask[0], st[0][2] / st[0][1], st[1][2] / st[1][1])
            for hh in range(2):
                lse_ref[hh, pl.ds(q0, tq), :] = st[hh][0] + jnp.log(st[hh][1])

        def group(g, carry):
            for r in range(grp):
                qblock(g, r)
            return carry

        lax.fori_loop(0, nq // grp, group, 0)

    base = 2 * D_A // LANES
    nh = D_B // LANES
    col = lambda off: pl.BlockSpec((t, LANES), lambda p: (0, off + p))
    return _call(
        body, (nh,), [col(base), col(base + nh), col(base + 2 * nh), pl.BlockSpec((2, 1, t), lambda p: (p, 0, 0))],
        [col(0), pl.BlockSpec((2, t, 1), lambda p: (p, 0, 0))],
        [jax.ShapeDtypeStruct((t, D_B), F32), jax.ShapeDtypeStruct((N_HEADS, t, 1), F32)],
        [pltpu.VMEM((t, LANES), BF16), pltpu.VMEM((t, LANES), BF16)], ("parallel",), "attn_fwd", [z, z, z, crow], carry)


def _attn_bwd(z, crow, lse, do, carry=None):
    t = z.shape[0]
    tq, nq, grp = _attn_groups(t)
    tw = grp * tq
    scale = HEAD_DIM ** -0.5

    def body(q_ref, k_ref, v_ref, cr_ref, lse_ref, do_ref, dq_ref, dk_ref, dv_ref, dc_ref, qa_s, da_s, kb_s, vb_s, dl_s):
        lane = lax.broadcasted_iota(jnp.int32, (1, LANES), 1)
        hmask = [(lane // HEAD_DIM) == hh for hh in range(2)]
        _cast_rows(k_ref, kb_s, t, tq, lambda v: v)
        _cast_rows(v_ref, vb_s, t, tq, lambda v: v)
        for hh in range(2):
            _cast_rows(q_ref, qa_s.at[hh], t, tq, lambda v, hh=hh: jnp.where(hmask[hh], v * scale, 0.0))
            _cast_rows(do_ref, da_s.at[hh], t, tq, lambda v, hh=hh: jnp.where(hmask[hh], v, 0.0))
        _cast_rows(q_ref, dq_ref, t, tq, lambda v: jnp.zeros_like(v))

        def probs(hh, q0, nq_rows, k0, nk_rows, off):
            s = _dot(qa_s[hh, pl.ds(q0, nq_rows), :], kb_s[pl.ds(k0, nk_rows), :], NT) - cr_ref[hh, :, pl.ds(k0, nk_rows)]
            p = jnp.exp(s - lse_ref[hh, pl.ds(q0, nq_rows), :])
            if off is not None:
                keep = (lax.broadcasted_iota(jnp.int32, (nq_rows, nk_rows), 0) + off
                        >= lax.broadcasted_iota(jnp.int32, (nq_rows, nk_rows), 1))
                p = jnp.where(keep, p, 0.0)
            return p, _dot(da_s[hh, pl.ds(q0, nq_rows), :], vb_s[pl.ds(k0, nk_rows), :], NT)

        def delta(g, r):
            q0 = pl.multiple_of((g * grp + r) * tq, tq)

            def add(k0, width, off, acc):
                res = []
                for hh in range(2):
                    p, dp = probs(hh, q0, tq, k0, width, off)
                    res.append(acc[hh] + jnp.sum(p * dp, axis=-1, keepdims=True))
                return tuple(res)

            zcol = jnp.zeros((tq, 1), F32)
            acc = lax.fori_loop(0, g, lambda j, acc: add(pl.multiple_of(j * tw, tw), tw, None, acc), (zcol, zcol))
            acc = add(pl.multiple_of(g * tw, tw), (r + 1) * tq, r * tq, acc)
            for hh in range(2):
                dl_s[hh, pl.ds(q0, tq), :] = acc[hh]

        def delta_group(g, carry):
            for r in range(grp):
                delta(g, r)
            return carry

        lax.fori_loop(0, nq // grp, delta_group, 0)

        def kblock(g, r):
            k0 = pl.multiple_of((g * grp + r) * tq, tq)
            kb = kb_s[pl.ds(k0, tq), :]

            def upd(q0, height, off, st):
                dk, dv, dc = st[0], st[1], [st[2], st[3]]
                dqs = []
                for hh in range(2):
                    p, dp = probs(hh, q0, height, k0, tq, off)
                    ds = p * (dp - dl_s[hh, pl.ds(q0, height), :])
                    dv = dv + _dot(p, da_s[hh, pl.ds(q0, height), :], TN)
                    dk = dk + _dot(ds, qa_s[hh, pl.ds(q0, height), :], TN)
                    dqs.append(_dot(ds, kb, NN))
                    dc[hh] = dc[hh] - jnp.sum(ds, axis=0, keepdims=True)
                dq_ref[pl.ds(q0, height), :] += jnp.where(hmask[0], dqs[0], dqs[1]) * scale
                return dk, dv, dc[0], dc[1]

            zero = jnp.zeros((tq, LANES), F32)
            zrow = jnp.zeros((1, tq), F32)
            st = upd(k0, (grp - r) * tq, 0, (zero, zero, zrow, zrow))
            st = lax.fori_loop(g + 1, nq // grp, lambda i, st: upd(pl.multiple_of(i * tw, tw), tw, None, st), st)
            dk_ref[pl.ds(k0, tq), :] = st[0]
            dv_ref[pl.ds(k0, tq), :] = st[1]
            for hh in range(2):
                dc_ref[hh, :, pl.ds(k0, tq)] = st[2 + hh]

        def kgroup(g, carry):
            for r in range(grp):
                kblock(g, r)
            return carry

        lax.fori_loop(0, nq // grp, kgroup, 0)

    base = 2 * D_A // LANES
    nh = D_B // LANES
    col = lambda off: pl.BlockSpec((t, LANES), lambda p: (0, off + p))
    ccs = pl.BlockSpec((2, t, 1), lambda p: (p, 0, 0))
    crs = pl.BlockSpec((2, 1, t), lambda p: (p, 0, 0))
    return _call(
        body, (nh,), [col(base), col(base + nh), col(base + 2 * nh), crs, ccs, col(0)], [col(0), col(0), col(0), crs],
        [jax.ShapeDtypeStruct((t, D_B), F32)] * 3 + [jax.ShapeDtypeStruct((N_HEADS, 1, t), F32)],
        [pltpu.VMEM((2, t, LANES), BF16), pltpu.VMEM((2, t, LANES), BF16), pltpu.VMEM((t, LANES), BF16),
         pltpu.VMEM((t, LANES), BF16), pltpu.VMEM((2, t, 1), F32)], ("parallel",), "attn_bwd", [z, z, z, crow, lse, do], carry)


def _s5_disc(a_re, a_im, log_dt, b_re, b_im):
    dt = jnp.exp(log_dt)
    mag = jnp.exp(a_re * dt)
    ar = mag * jnp.cos(a_im * dt)
    ai = mag * jnp.sin(a_im * dt)
    den = a_re * a_re + a_im * a_im
    kr = ((ar - 1.0) * a_re + ai * a_im) / den
    ki = (ai * a_re - (ar - 1.0) * a_im) / den
    kr3, ki3 = kr[:, None, :], ki[:, None, :]
    return ar, ai, kr3 * b_re - ki3 * b_im, kr3 * b_im + ki3 * b_re


def _s5_prep(a_re, a_im, log_dt, b_re, b_im):
    g, p = a_re.shape
    gc = b_re.shape[1]

    def body(*refs):
        res = _s5_disc(*[r[...] for r in refs[:5]])
        for o_ref, v in zip(refs[5:], res):
            o_ref[...] = v

    sds = jax.ShapeDtypeStruct
    return pl.pallas_call(body, out_shape=[sds((g, p), F32), sds((g, p), F32), sds((g, gc, p), F32), sds((g, gc, p), F32)],
                          name="s5_prep")(a_re, a_im, log_dt, b_re, b_im)


def _s5_prep_bwd(a_re, a_im, log_dt, b_re, b_im, d_ar, d_ai, d_br, d_bi):
    ins = (a_re, a_im, log_dt, b_re, b_im)

    def body(*refs):
        vals = [r[...] for r in refs[:5]]
        cts = tuple(r[...] for r in refs[5:9])
        _, pull = jax.vjp(_s5_disc, *vals)
        for o_ref, v in zip(refs[9:], pull(cts)):
            o_ref[...] = v

    return pl.pallas_call(body, out_shape=[jax.ShapeDtypeStruct(a.shape, F32) for a in ins], name="s5_prep_bwd")(
        *ins, d_ar, d_ai, d_br, d_bi)


def _s5_scan_rows(t, ar, ai, hr_s, hi_s, off, reverse):
    n = ar.shape[1]
    if reverse:
        ai = -ai
    sub = lax.broadcasted_iota(jnp.int32, (8, n), 0)
    cmul = lambda xr, xi, yr, yi: (xr * yr - xi * yi, xr * yi + xi * yr)
    pw = [(ar, ai)]
    for _ in range(7):
        pw.append(cmul(*pw[-1], ar, ai))
    pr = jnp.zeros((8, n), F32)
    pi = jnp.zeros((8, n), F32)
    for r in range(8):
        k = 7 - r if reverse else r
        pr = jnp.where(sub == r, pw[k][0], pr)
        pi = jnp.where(sub == r, pw[k][1], pi)

    def step(g, carry):
        cr, ci = carry
        r0 = pl.multiple_of(off + (t // 8 - 1 - g if reverse else g) * 8, 8)
        br = hr_s[pl.ds(r0, 8), :]
        bi = hi_s[pl.ds(r0, 8), :]
        for s in (1, 2, 4):
            ok = (sub < 8 - s) if reverse else (sub >= s)
            shift = 8 - s if reverse else s
            sr = jnp.where(ok, pltpu.roll(br, shift, 0), 0.0)
            si = jnp.where(ok, pltpu.roll(bi, shift, 0), 0.0)
            mr, mi = cmul(pw[s - 1][0], pw[s - 1][1], sr, si)
            br, bi = br + mr, bi + mi
        mr, mi = cmul(pr, pi, cr, ci)
        br, bi = br + mr, bi + mi
        hr_s[pl.ds(r0, 8), :] = br
        hi_s[pl.ds(r0, 8), :] = bi
        edge = sub == (0 if reverse else 7)
        return (jnp.sum(jnp.where(edge, br, 0.0), axis=0, keepdims=True),
                jnp.sum(jnp.where(edge, bi, 0.0), axis=0, keepdims=True))

    zero = jnp.zeros((1, n), F32)
    lax.fori_loop(0, t // 8, step, (zero, zero))


def _s5_fwd(z, bd_re, bd_im, ab_re, ab_im, cd_re, cd_im, dvec, carry=None):
    t = z.shape[0]
    cr = _tile(t, 256)
    ns = N_STATE // 2

    def body(u_ref, br_ref, bi_ref, ar_ref, ai_ref, cre_ref, cim_ref, d_ref, y_ref, hr_s, hi_s):
        def p1(c, carry):
            t0 = pl.multiple_of(c * cr, cr)
            u = u_ref[pl.ds(t0, cr), :]
            hr_s[pl.ds(t0, cr), :] = _dot(u, br_ref[...], NN)
            hi_s[pl.ds(t0, cr), :] = _dot(u, bi_ref[...], NN)
            return carry

        lax.fori_loop(0, t // cr, p1, 0)
        _s5_scan_rows(t, ar_ref[...], ai_ref[...], hr_s, hi_s, 0, False)

        def p3(c, carry):
            t0 = pl.multiple_of(c * cr, cr)
            y_ref[pl.ds(t0, cr), :] = (_dot(hr_s[pl.ds(t0, cr), :], cre_ref[...], NN)
                                       - _dot(hi_s[pl.ds(t0, cr), :], cim_ref[...], NN)
                                       + d_ref[...] * u_ref[pl.ds(t0, cr), :])
            return carry

        lax.fori_loop(0, t // cr, p3, 0)

    blk = lambda r, c: pl.BlockSpec((None, r, c), lambda b: (b, 0, 0))
    return _call(
        body, (2,),
        [pl.BlockSpec((t, LANES), lambda b: (0, Z_U // LANES + b)), blk(LANES, ns), blk(LANES, ns), blk(1, ns),
         blk(1, ns), blk(ns, LANES), blk(ns, LANES), pl.BlockSpec((1, LANES), lambda b: (0, b))],
        [pl.BlockSpec((t, LANES), lambda b: (0, b))], [jax.ShapeDtypeStruct((t, D_C), F32)],
        [pltpu.VMEM((t, ns), F32), pltpu.VMEM((t, ns), F32)], ("arbitrary",), "s5_fwd",
        [z, bd_re, bd_im, ab_re, ab_im, cd_re, cd_im, dvec], carry)


def _s5_bwd(z, dy, bd_re, bd_im, ab_re, ab_im, cd_re, cd_im, dvec):
    t = z.shape[0]
    cr = _tile(t, 256)
    ns = N_STATE // 2

    def body(u_ref, dy_ref, br_ref, bi_ref, ar_ref, ai_ref, cre_ref, cim_ref, d_ref,
             du_ref, dbr_ref, dbi_ref, dar_ref, dai_ref, dcre_ref, dcim_ref, dd_ref, hr_s, hi_s, gr_s, gi_s):
        zero8 = jnp.zeros((8, ns), F32)
        hr_s[pl.ds(0, 8), :] = zero8
        hi_s[pl.ds(0, 8), :] = zero8
        for ref in (dbr_ref, dbi_ref, dar_ref, dai_ref, dcre_ref, dcim_ref, dd_ref):
            ref[...] = jnp.zeros(ref.shape, F32)

        def p1(c, carry):
            t0 = pl.multiple_of(c * cr, cr)
            u = u_ref[pl.ds(t0, cr), :]
            hr_s[pl.ds(t0 + 8, cr), :] = _dot(u, br_ref[...], NN)
            hi_s[pl.ds(t0 + 8, cr), :] = _dot(u, bi_ref[...], NN)
            return carry

        lax.fori_loop(0, t // cr, p1, 0)
        _s5_scan_rows(t, ar_ref[...], ai_ref[...], hr_s, hi_s, 8, False)

        def p3(c, carry):
            t0 = pl.multiple_of(c * cr, cr)
            dyv = dy_ref[pl.ds(t0, cr), :]
            u = u_ref[pl.ds(t0, cr), :]
            gr_s[pl.ds(t0, cr), :] = _dot(dyv, cre_ref[...], NT)
            gi_s[pl.ds(t0, cr), :] = -_dot(dyv, cim_ref[...], NT)
            dcre_ref[...] += _dot(hr_s[pl.ds(t0 + 8, cr), :], dyv, TN)
            dcim_ref[...] -= _dot(hi_s[pl.ds(t0 + 8, cr), :], dyv, TN)
            dd_ref[...] += jnp.sum(dyv * u, axis=0, keepdims=True)
            du_ref[pl.ds(t0, cr), :] = dyv * d_ref[...]
            return carry

        lax.fori_loop(0, t // cr, p3, 0)
        _s5_scan_rows(t, ar_ref[...], ai_ref[...], gr_s, gi_s, 0, True)

        def p5(c, carry):
            t0 = pl.multiple_of(c * cr, cr)
            u = u_ref[pl.ds(t0, cr), :]
            gr = gr_s[pl.ds(t0, cr), :]
            gi = gi_s[pl.ds(t0, cr), :]
            dbr_ref[...] += _dot(u, gr, TN)
            dbi_ref[...] += _dot(u, gi, TN)
            du_ref[pl.ds(t0, cr), :] += _dot(gr, br_ref[...], NT) + _dot(gi, bi_ref[...], NT)
            hpr = pltpu.roll(hr_s[pl.ds(t0, cr + 8), :], 1, 0)[8:, :]
            hpi = pltpu.roll(hi_s[pl.ds(t0, cr + 8), :], 1, 0)[8:, :]
            dar_ref[...] += jnp.sum(gr * hpr + gi * hpi, axis=0, keepdims=True)
            dai_ref[...] += jnp.sum(gi * hpr - gr * hpi, axis=0, keepdims=True)
            return carry

        lax.fori_loop(0, t // cr, p5, 0)

    blk = lambda r, c: pl.BlockSpec((None, r, c), lambda b: (b, 0, 0))
    ucol = pl.BlockSpec((t, LANES), lambda b: (0, Z_U // LANES + b))
    ycol = pl.BlockSpec((t, LANES), lambda b: (0, b))
    dsp = pl.BlockSpec((1, LANES), lambda b: (0, b))
    sds = jax.ShapeDtypeStruct
    return pl.pallas_call(
        body, grid=(2,),
        in_specs=[ucol, ycol, blk(LANES, ns), blk(LANES, ns), blk(1, ns), blk(1, ns), blk(ns, LANES), blk(ns, LANES), dsp],
        out_specs=[ycol, blk(LANES, ns), blk(LANES, ns), blk(1, ns), blk(1, ns), blk(ns, LANES), blk(ns, LANES), dsp],
        out_shape=[sds((t, D_C), F32), sds((2, LANES, ns), F32), sds((2, LANES, ns), F32), sds((2, 1, ns), F32),
                   sds((2, 1, ns), F32), sds((2, ns, LANES), F32), sds((2, ns, LANES), F32), sds((1, D_C), F32)],
        scratch_shapes=[pltpu.VMEM((t + 8, ns), F32), pltpu.VMEM((t + 8, ns), F32), pltpu.VMEM((t, ns), F32),
                        pltpu.VMEM((t, ns), F32)],
        compiler_params=pltpu.CompilerParams(dimension_semantics=("arbitrary",)), name="s5_bwd")(
            z, dy, bd_re, bd_im, ab_re, ab_im, cd_re, cd_im, dvec)


def _mix_out(out_a, out_b, yc, x1, ga, gb, gc, wglu, wout, ln_g, ln_b):
    yg = _gelu(yc)
    out_c = yg * _sig(_bdot(yg, wglu))
    o = jnp.concatenate([_rms(out_a, ga), _rms(out_b, gb), _rms(out_c, gc)], axis=-1)
    return (_ln(ALPHA * x1 + _bdot(o, wout), ln_g, ln_b),)


def _ln_only(pre, g, b):
    return (_ln(pre, g, b),)


def _loss_head(y, target, tm):
    t, d = y.shape

    def body(y_ref, t_ref, dy_ref, l_ref):
        i = pl.program_id(0)
        e = y_ref[...] - t_ref[...]
        dy_ref[...] = e * (1.0 / d)
        part = 0.5 * jnp.sum(jnp.sum(e * e, axis=-1, keepdims=True) * (1.0 / d), axis=0, keepdims=True)
        row = jnp.where(lax.broadcasted_iota(jnp.int32, (1, LANES), 1) == 0, part, 0.0)

        @pl.when(i == 0)
        def _():
            l_ref[...] = row

        @pl.when(i > 0)
        def _():
            l_ref[...] += row

    spec = pl.BlockSpec((tm, d), lambda i: (i, 0))
    return pl.pallas_call(
        body, grid=(t // tm,), in_specs=[spec, spec], out_specs=[spec, pl.BlockSpec((1, LANES), lambda i: (0, 0))],
        out_shape=[jax.ShapeDtypeStruct((t, d), F32), jax.ShapeDtypeStruct((1, LANES), F32)],
        compiler_params=pltpu.CompilerParams(dimension_semantics=("arbitrary",)), name="loss_head")(y, target)


def _adamw(w, g, m, v, name):
    r, c = w.shape
    tr = r
    for cand in (512, 256, 352, 128):
        if r % cand == 0:
            tr = cand
            break

    def body(w_ref, g_ref, m_ref, v_ref, d_ref, nm_ref, nv_ref):
        gv = g_ref[...]
        mn = ADAM_B1 * m_ref[...] + (1.0 - ADAM_B1) * gv
        vn = ADAM_B2 * v_ref[...] + (1.0 - ADAM_B2) * (gv * gv)
        m_hat = mn / (1.0 - ADAM_B1 ** ADAM_STEP)
        v_hat = vn / (1.0 - ADAM_B2 ** ADAM_STEP)
        d_ref[...] = -ADAM_LR * (m_hat / (jnp.sqrt(v_hat) + ADAM_EPS) + ADAM_WD * w_ref[...])
        nm_ref[...] = mn
        nv_ref[...] = vn

    spec = pl.BlockSpec((tr, c), lambda i: (i, 0))
    return pl.pallas_call(
        body, grid=(r // tr,), in_specs=[spec] * 4, out_specs=[spec] * 3,
        out_shape=[jax.ShapeDtypeStruct((r, c), F32)] * 3,
        compiler_params=pltpu.CompilerParams(dimension_semantics=("parallel",)), name=name)(w, g, m, v)


def _row_tile(r):
    for cand in (512, 448, 352, 256, 128):
        if r % cand == 0:
            return cand
    return r


def _pair_add(a, b, idx, name, out_dtype):
    _, r, w = a.shape
    tr = _row_tile(r)

    def body(i_ref, a_ref, b_ref, o_ref):
        o_ref[...] = (a_ref[...].astype(F32) + b_ref[...].astype(F32)).astype(o_ref.dtype)

    grid_spec = pltpu.PrefetchScalarGridSpec(
        num_scalar_prefetch=1, grid=(4, r // tr),
        in_specs=[pl.BlockSpec((None, tr, w), lambda q, i, s: (2 * q + s[0], i, 0)),
                  pl.BlockSpec((None, tr, w), lambda q, i, s: (q, i, 0))],
        out_specs=pl.BlockSpec((None, tr, w), lambda q, i, s: (q, i, 0)))
    return pl.pallas_call(body, grid_spec=grid_spec, out_shape=jax.ShapeDtypeStruct((4, r, w), out_dtype),
                          compiler_params=pltpu.CompilerParams(dimension_semantics=("parallel", "parallel")), name=name)(
                              idx, a, b)


def _quad_add(p, rb, idx, name):
    _, r, w = p.shape
    tr = _row_tile(r)

    def body(i_ref, p_ref, r0, r1, r2, o_ref):
        o_ref[...] = ((p_ref[...].astype(F32) + r0[...].astype(F32)) + r1[...].astype(F32)) + r2[...].astype(F32)

    grid_spec = pltpu.PrefetchScalarGridSpec(
        num_scalar_prefetch=1, grid=(r // tr,),
        in_specs=[pl.BlockSpec((None, tr, w), lambda i, s: (s[0], i, 0))]
        + [pl.BlockSpec((None, tr, w), functools.partial(lambda i, s, k: (k, i, 0), k=k)) for k in range(3)],
        out_specs=pl.BlockSpec((tr, w), lambda i, s: (i, 0)))
    return pl.pallas_call(body, grid_spec=grid_spec, out_shape=jax.ShapeDtypeStruct((r, w), F32),
                          compiler_params=pltpu.CompilerParams(dimension_semantics=("parallel",)), name=name)(
                              idx, p, rb, rb, rb)


def _gather_small(buf):
    def plan(x, y, c, ins, outs):
        dst = outs[0].at[4 * x + 2 * y + c]
        res = [(ins[0], dst, None)]
        for rel in range(1, 8):
            res.append((ins[0], dst, (x ^ (rel >> 2), y ^ ((rel >> 1) & 1), c ^ (rel & 1))))
        return res

    return _Carry([buf], [jax.ShapeDtypeStruct((8,) + buf.shape, buf.dtype)], {}, plan, 8)


def _sum_small(allb):
    _, r, w = allb.shape

    def body(a_ref, o_ref):
        s = a_ref[0]
        for k in range(1, 8):
            s = s + a_ref[k]
        o_ref[...] = s

    return pl.pallas_call(body, out_shape=jax.ShapeDtypeStruct((r, w), F32), name="ar_small_sum")(allb)


def _pad_rows(a, rows):
    return jnp.pad(a, ((0, rows - a.shape[0]), (0, 0)))


def _pack_small(arrs):
    rows, tail = [], []
    for a in arrs:
        if not tail and a.size % LANES == 0:
            rows.append(a.reshape(-1, LANES))
        else:
            tail.append(a.reshape(-1))
    n_rows = sum(r.shape[0] for r in rows)
    n_tail = sum(int(v.size) for v in tail)
    tail_rows = -(-n_tail // LANES)
    total_rows = n_rows + tail_rows + (-(n_rows + tail_rows)) % 8
    if tail:
        tail.append(jnp.zeros((tail_rows * LANES - n_tail,), F32))
        rows.append(jnp.concatenate(tail).reshape(tail_rows, LANES))
    if total_rows > n_rows + tail_rows:
        rows.append(jnp.zeros((total_rows - n_rows - tail_rows, LANES), F32))
    return jnp.concatenate(rows, axis=0)


def _unpack_small(buf, shapes):
    out, off = [], 0
    flat = None
    for s in shapes:
        n = int(np.prod(s))
        if off % LANES == 0 and n % LANES == 0:
            out.append(buf[off // LANES:(off + n) // LANES].reshape(s))
        else:
            flat = buf.reshape(-1) if flat is None else flat
            out.append(flat[off:off + n].reshape(s))
        off += n
    return out


def _block_diag(blocks, nb):
    m, r, c = blocks.shape
    n = m // nb
    eye = jnp.eye(n, dtype=blocks.dtype)
    return (blocks.reshape(nb, n, r, 1, c) * eye[None, :, None, :, None]).reshape(nb, n * r, n * c)


def _diag_blocks(dense, n):
    nb = dense.shape[0]
    r, c = dense.shape[1] // n, dense.shape[2] // n
    eye = jnp.eye(n, dtype=dense.dtype)
    return jnp.sum(dense.reshape(nb, n, r, n, c) * eye[None, :, None, :, None], axis=3).reshape(nb * n, r, c)


def kernel(x, ffn1_w_gate, ffn1_w_up, ffn1_w_down, ln1_g, ln1_b, w_in, conv_w, conv_b, rg_w_a, rg_b_a, rg_w_x, rg_b_x, rg_lambda, fox_b_f, s5_a_re, s5_a_im, s5_log_dt, s5_b_re, s5_b_im, s5_c_re, s5_c_im, s5_d, s5_w_glu, mix_norm_g, w_out, ln2_g, ln2_b, ffn2_w_gate, ffn2_w_up, ffn2_w_down, ln3_g, ln3_b, loss_target, m_ffn1_w_gate, m_ffn1_w_up, m_ffn1_w_down, m_ln1_g, m_ln1_b, m_w_in, m_conv_w, m_conv_b, m_rg_w_a, m_rg_b_a, m_rg_w_x, m_rg_b_x, m_rg_lambda, m_fox_b_f, m_s5_a_re, m_s5_a_im, m_s5_log_dt, m_s5_b_re, m_s5_b_im, m_s5_c_re, m_s5_c_im, m_s5_d, m_s5_w_glu, m_mix_norm_g, m_w_out, m_ln2_g, m_ln2_b, m_ffn2_w_gate, m_ffn2_w_up, m_ffn2_w_down, m_ln3_g, m_ln3_b, v_ffn1_w_gate, v_ffn1_w_up, v_ffn1_w_down, v_ln1_g, v_ln1_b, v_w_in, v_conv_w, v_conv_b, v_rg_w_a, v_rg_b_a, v_rg_w_x, v_rg_b_x, v_rg_lambda, v_fox_b_f, v_s5_a_re, v_s5_a_im, v_s5_log_dt, v_s5_b_re, v_s5_b_im, v_s5_c_re, v_s5_c_im, v_s5_d, v_s5_w_glu, v_mix_norm_g, v_w_out, v_ln2_g, v_ln2_b, v_ffn2_w_gate, v_ffn2_w_up, v_ffn2_w_down, v_ln3_g, v_ln3_b):
    a = dict(locals())
    w = {n: a[n] for n in WEIGHTS}
    t, d = x.shape[1], x.shape[2]
    f = ffn1_w_down.shape[1] * 8
    fs, ds = f // 8, d // 8
    mx, my, mc = lax.axis_index("x"), lax.axis_index("y"), lax.axis_index("c")
    me = 4 * mx + 2 * my + mc
    tm = _tile(t, 512)
    tf = f // 2
    win_rows = ds * Z_W // d

    FFN1, MIXW, FFN2 = ['g1', 'u1', 'd1'], ['win', 'wout', 'glu', 'conv'], ['g2', 'u2', 'd2']
    glu_rows = D_C * D_C // (8 * d)

    def shard_segs(l):
        wi = w['w_in'][l]
        win_p = jnp.concatenate([wi[:, :Z_F + N_HEADS], jnp.zeros((ds, Z_U - Z_F - N_HEADS), F32), wi[:, Z_F + N_HEADS:]], axis=1)
        conv_bits = lax.bitcast_convert_type(w['conv_w'][l], BF16).reshape(1, -1)
        segs = dict(g1=w['ffn1_w_gate'][l].T, u1=w['ffn1_w_up'][l].T, d1=w['ffn1_w_down'][l],
                    g2=w['ffn2_w_gate'][l].T, u2=w['ffn2_w_up'][l].T, d2=w['ffn2_w_down'][l],
                    win=win_p.reshape(win_rows, d), wout=w['w_out'][l], glu=_pad_rows(w['s5_w_glu'][l].reshape(-1, d), 16))
        segs = {k: v.astype(BF16) for k, v in segs.items()}
        segs['conv'] = _pad_rows(jnp.pad(conv_bits, ((0, 0), (0, d - conv_bits.shape[1]))), 16)
        return segs

    shards = [shard_segs(l) for l in range(DEPTH)]
    wts = {}

    def cat(keys):
        return [shards[l][k] for l, k in keys]

    def split(gs, keys):
        for (l, k), g in zip(keys, gs):
            wts[(l, k)] = g

    grp_a = [(0, k) for k in FFN1]
    grp_b = [(0, k) for k in MIXW]
    grp_c = [(0, k) for k in FFN2]
    grp_d = [(1, k) for k in FFN1]
    grp_e = [(1, k) for k in FFN2]
    grp_f = [(1, k) for k in MIXW]
    split(_run(_ag_sibling(_run(_ag_chips(cat(grp_a)), "ag_chips")), "ag_sibling"), grp_a)

    xs = x[0]
    saved = []
    cur = xs
    for l in range(DEPTH):
        row = lambda n: w[n][l].reshape(1, -1)
        ffn = lambda keys: tuple(wts[(l, k)].reshape(f, d) for k in keys)
        wa = _block_diag(w['rg_w_a'][l], 3)
        wx = _block_diag(w['rg_w_x'][l], 3)
        bf = jnp.pad(row('fox_b_f'), ((0, 0), (0, LANES - N_HEADS)))
        s5p = (w['s5_a_re'][l], w['s5_a_im'][l], w['s5_log_dt'][l].reshape(-1, 1),
               w['s5_b_re'][l].transpose(0, 2, 1), w['s5_b_im'][l].transpose(0, 2, 1))
        ab_re, ab_im, bb_re, bb_im = _s5_prep(*s5p)
        bd_re, bd_im = _block_diag(bb_re, 2), _block_diag(bb_im, 2)
        cd_re = _block_diag(w['s5_c_re'][l].transpose(0, 2, 1), 2)
        cd_im = _block_diag(w['s5_c_im'][l].transpose(0, 2, 1), 2)
        abr, abi = ab_re.reshape(2, 1, N_STATE // 2), ab_im.reshape(2, 1, N_STATE // 2)
        gm = row('mix_norm_g')
        ga, gb, gc = gm[:, :D_A], gm[:, D_A:D_A + D_B], gm[:, D_A + D_B:]

        x0 = cur
        ffn1 = ffn(FFN1)
        (x1, pre1, gs1, us1), cres = _ffn_fwd(x0, *ffn1, row('ln1_g'), row('ln1_b'), tm, tf,
                                              carry=_ag_chips(cat(grp_b if l == 0 else grp_e)))
        if l == 0:
            split(_run(_ag_sibling(cres), "ag_sibling"), grp_b)
        else:
            g_e = cres
        win = wts[(l, 'win')].reshape(d, Z_W)
        wout = wts[(l, 'wout')].reshape(d, d).astype(F32)
        wglu = wts[(l, 'glu')][:, :glu_rows].reshape(D_C, D_C).astype(F32)
        conv_full = lax.bitcast_convert_type(
            wts[(l, 'conv')][:, 0, :2 * CONV_WIDTH * D_A // 8].reshape(8, CONV_WIDTH, D_A // 8, 2), F32)
        conv_full = conv_full.transpose(1, 0, 2).reshape(CONV_WIDTH, D_A)
        z = _mm(x1, win, 'nn', F32, tm, Z_W, d, "mix_in")
        out_a, h_a = _rg_fwd(z, conv_full, row('conv_b'), wa, row('rg_b_a'), wx, row('rg_b_x'), row('rg_lambda'))
        cs = _fgate_fwd(z, bf)
        crow = cs[:, :N_HEADS].T.reshape(N_HEADS, 1, t)
        (out_b, lse), cres = _attn_fwd(z, crow, carry=_ag_chips(cat(grp_c)) if l == 0 else _ag_sibling(g_e))
        if l == 0:
            (yc,), cres = _s5_fwd(z, bd_re, bd_im, abr, abi, cd_re, cd_im, row('s5_d'),
                                  carry=_join(_ag_sibling(cres), _ag_chips(cat(grp_f))))
            split(cres[:len(grp_c)], grp_c)
            g_f = cres[len(grp_c):]
        else:
            split(cres, grp_e)
            (yc,), _ = _s5_fwd(z, bd_re, bd_im, abr, abi, cd_re, cd_im, row('s5_d'))
        mix_params = [ga, gb, gc, wglu, wout, row('ln2_g'), row('ln2_b')]
        (x2,) = _rowwise(_mix_out, [out_a, out_b, yc, x1], mix_params, [(d, F32)], tm, "mix_out")
        ffn2 = ffn(FFN2)
        (x3, pre3, gs2, us2), cres = _ffn_fwd(x2, *ffn2, row('ln3_g'), row('ln3_b'), tm, tf,
                                              carry=_join(_ag_chips(cat(grp_d)), _ag_sibling(g_f)) if l == 0 else None)
        if l == 0:
            split(cres[len(grp_d):], grp_f)
            split(_run(_ag_sibling(cres[:len(grp_d)]), "ag_sibling"), grp_d)
        saved.append(dict(x0=x0, x1=x1, pre1=pre1, gs1=gs1, us1=us1, z=z, out_a=out_a, h_a=h_a, crow=crow,
                          out_b=out_b, lse=lse, yc=yc, x2=x2, pre3=pre3, gs2=gs2, us2=us2, mix_params=mix_params,
                          ffn1=ffn1, ffn2=ffn2, win=win, conv_full=conv_full, wa=wa, wx=wx, bf=bf, s5p=s5p,
                          s5m=(bd_re, bd_im, abr, abi, cd_re, cd_im)))
        cur = x3

    dy, loss_row = _loss_head(cur, loss_target[0], tm)
    loss = lax.psum(loss_row[0, 0], ("x", "y", "c"))

    assert DEPTH == 2
    small_grads = {}
    small_names = ['conv_w'] + SMALL
    c_idx = jnp.reshape(mc, (1,)).astype(jnp.int32)
    chip_idx = jnp.reshape(2 * mx + my, (1,)).astype(jnp.int32)

    def blocks(arrs):
        return jnp.concatenate([v.astype(BF16).reshape(8, -1, d) for v in arrs], axis=1)

    def mixer_blocks(dwin, dwout, dwglu):
        glu = jnp.pad(dwglu.astype(BF16).reshape(8, glu_rows, d), ((0, 0), (0, 32 - glu_rows), (0, 0)))
        return jnp.concatenate([dwin.reshape(8, win_rows, d), dwout.astype(BF16).reshape(8, ds, d), glu], axis=1)

    def pair(full, ra):
        return _pair_add(full, ra, c_idx, "rs_add_sibling", BF16)

    for l in reversed(range(DEPTH)):
        s = saved[l]
        row = lambda n: w[n][l].reshape(1, -1)
        wg_tiles = dict(tm=tf, tn=d, tk=_tile(t, 1024))
        first = l == 0

        def ffn_back(dyv, pre, gs, us, wts3, xin, ln_g, ln_b, carry_fn=None, pipeline=None):
            (dpre,), (dlg, dlb), _ = _rowwise_vjp(_ln_only, [pre], [ln_g, ln_b], [dyv], tm, "ln_bwd")
            (dx, dg, du, hh), cres = _ffn_bwd(dpre, gs, us, *wts3, tm, tf, carry=carry_fn(dlg, dlb) if carry_fn else None)
            if pipeline is None:
                dwg = _mm(dg, xin, 'tn', BF16, name="ffn_dw_gate", **wg_tiles)
                dwu = _mm(du, xin, 'tn', BF16, name="ffn_dw_up", **wg_tiles)
                dwd = _mm(hh, dpre, 'tn', BF16, name="ffn_dw_down", **wg_tiles)
                return dx, (dwg, dwu, dwd), dlg, dlb, cres

            def front(dw):
                full = dw.reshape(8, fs, d)
                (ra,) = _run(_rs_sibling(full), "rs_sibling")
                return pair(full, ra)

            dwg, (rb_first,) = _mm(dg, xin, 'tn', BF16, name="ffn_dw_gate", carry=_rs_chips(pipeline), **wg_tiles)
            cres = list(cres) + [rb_first]
            p_g = front(dwg)
            dwu, (rb_g,) = _mm(du, xin, 'tn', BF16, name="ffn_dw_up", carry=_rs_chips(p_g), **wg_tiles)
            p_u = front(dwu)
            dwd, (rb_u,) = _mm(hh, dpre, 'tn', BF16, name="ffn_dw_down", carry=_rs_chips(p_u), **wg_tiles)
            p_d = front(dwd)
            (rb_d,) = _run(_rs_chips(p_d), "rs_chips")
            return dx, ((p_g, rb_g), (p_u, rb_u), (p_d, rb_d)), dlg, dlb, cres

        dx2, (dwg2, dwu2, dwd2), dl3g, dl3b, cres = ffn_back(
            dy, s['pre3'], s['gs2'], s['us2'], s['ffn2'], s['x2'], row('ln3_g'), row('ln3_b'),
            (lambda *_: _rs_sibling(full_l1)) if first else None)
        if first:
            part_l1 = pair(full_l1, cres[0])
            full_c2 = blocks([dwg2, dwu2, dwd2])
        (d_oa, d_ob, d_yc, d_x1), (dga, dgb, dgc, dwglu, dwout, dl2g, dl2b), cres = _rowwise_vjp(
            _mix_out, [s['out_a'], s['out_b'], s['yc'], s['x1']], s['mix_params'], [dx2], tm, "mix_out_bwd",
            carry=_rs_sibling(full_c2) if first else None)
        if first:
            part_c2 = pair(full_c2, cres[0])
        (d_ax, d_ag, dcw, dcb, dwa, dba, dwx, dbx, dlam) = _rg_bwd(
            s['z'], s['h_a'], d_oa, s['conv_full'], row('conv_b'), s['wa'], row('rg_b_a'), s['wx'], row('rg_b_x'), row('rg_lambda'))
        (dq, dk, dv, dcrow), cres = _attn_bwd(s['z'], s['crow'], s['lse'], d_ob,
                                              carry=_join(_rs_chips(part_l1), _rs_chips(part_c2)) if first else None)
        if first:
            rb_l1, rb_c2 = cres
        dc_pad = jnp.pad(dcrow.reshape(N_HEADS, t).T, ((0, 0), (0, LANES - N_HEADS)))
        dzf, dbf = _fgate_bwd(s['z'], s['bf'], dc_pad)
        du_c, dbd_re, dbd_im, dabr, dabi, dcd_re, dcd_im, dd = _s5_bwd(s['z'], d_yc, *s['s5m'], row('s5_d'))
        dz = jnp.concatenate([d_ax, d_ag, dq, dk, dv, dzf, du_c], axis=1)
        dx1 = _mm(dz, s['win'], 'nt', F32, tm, d, Z_W, "mix_in_dx", add=d_x1)
        dwin = _mm(s['x1'], dz, 'tn', BF16, d, Z_W // 2, _tile(t, 1024), "mix_in_dw")
        dbb_re, dbb_im = _diag_blocks(dbd_re, N_GROUPS // 2), _diag_blocks(dbd_im, N_GROUPS // 2)
        dcm_re, dcm_im = _diag_blocks(dcd_re, N_GROUPS // 2), _diag_blocks(dcd_im, N_GROUPS // 2)
        da_re, da_im, dlog_dt, db_re, db_im = _s5_prep_bwd(*s['s5p'], dabr.reshape(N_GROUPS, C_STATE), dabi.reshape(N_GROUPS, C_STATE), dbb_re, dbb_im)
        sg = dict(conv_w=dcw, conv_b=dcb, rg_w_a=_diag_blocks(dwa, 2), rg_b_a=dba,
                  rg_w_x=_diag_blocks(dwx, 2), rg_b_x=dbx, rg_lambda=dlam, fox_b_f=dbf[:, :N_HEADS],
                  s5_a_re=da_re, s5_a_im=da_im, s5_log_dt=dlog_dt, s5_b_re=db_re.transpose(0, 2, 1), s5_b_im=db_im.transpose(0, 2, 1),
                  s5_c_re=dcm_re.transpose(0, 2, 1), s5_c_im=dcm_im.transpose(0, 2, 1), s5_d=dd,
                  mix_norm_g=jnp.concatenate([dga, dgb, dgc], axis=1), ln2_g=dl2g, ln2_b=dl2b, ln3_g=dl3g, ln3_b=dl3b)
        small_grads[l] = sg

        if first:
            full_m = mixer_blocks(dwin, dwout, dwglu)
            (ra_m,) = _run(_rs_sibling(full_m), "rs_sibling")
            part_m = pair(full_m, ra_m)

            def last_carry(dlg, dlb):
                sg.update(ln1_g=dlg, ln1_b=dlb)
                packed = _pack_small([small_grads[ll][n] for n in small_names for ll in range(DEPTH)])
                return _gather_small(packed)

            dx0, tail, _, _, (all_small, rb_m) = ffn_back(dx1, s['pre1'], s['gs1'], s['us1'], s['ffn1'], s['x0'], row('ln1_g'),
                                                          row('ln1_b'), last_carry, pipeline=part_m)
        else:
            dx0, (dwg1, dwu1, dwd1), dl1g, dl1b, _ = ffn_back(dx1, s['pre1'], s['gs1'], s['us1'], s['ffn1'], s['x0'],
                                                              row('ln1_g'), row('ln1_b'))
            sg.update(ln1_g=dl1g, ln1_b=dl1b)
            full_l1 = jnp.concatenate([blocks([dwg1, dwu1, dwd1, dwg2, dwu2, dwd2]), mixer_blocks(dwin, dwout, dwglu)], axis=1)
        dy = dx0

    grad_x = dy.reshape(x.shape)
    quad = lambda part, rb: _quad_add(part, rb, chip_idx, "rs_add_chips")
    own = {}

    def take(rows_f32, keys, l):
        off = 0
        for k, r in keys:
            own[(l, k)] = rows_f32[off:off + r]
            off += r

    ffn_keys = lambda names: [(k, fs) for k in names]
    mix_keys = [('win', win_rows), ('wout', ds), ('glu', glu_rows)]
    take(quad(part_l1, rb_l1), ffn_keys(FFN1 + FFN2) + mix_keys, 1)
    take(quad(part_c2, rb_c2), ffn_keys(FFN2), 0)
    take(quad(part_m, rb_m), mix_keys, 0)
    for k, (part, rb) in zip(FFN1, tail):
        own[(0, k)] = quad(part, rb)

    grads = {}
    grads_t = {}
    for k, n in zip(FFN1 + FFN2, ['ffn1_w_gate', 'ffn1_w_up', 'ffn1_w_down', 'ffn2_w_gate', 'ffn2_w_up', 'ffn2_w_down']):
        stacked = jnp.stack([own[(l, k)] for l in range(DEPTH)])
        if 'down' in n:
            grads[n] = stacked
        else:
            grads_t[n] = stacked
            grads[n] = jnp.swapaxes(stacked, 1, 2)
    gwin = jnp.stack([own[(l, 'win')].reshape(ds, Z_W) for l in range(DEPTH)])
    grads['w_in'] = jnp.concatenate([gwin[:, :, :Z_F + N_HEADS], gwin[:, :, Z_U:]], axis=2)
    grads['w_out'] = jnp.stack([own[(l, 'wout')] for l in range(DEPTH)])
    grads['s5_w_glu'] = jnp.stack([own[(l, 'glu')].reshape(D_C // 8, D_C) for l in range(DEPTH)])

    small_shapes = [((DEPTH, CONV_WIDTH, D_A) if n == 'conv_w' else w[n].shape) for n in small_names]
    conv_zero = jnp.zeros((DEPTH, CONV_WIDTH, D_A), F32)
    summed = _sum_small(all_small)
    for n, g in zip(small_names, _unpack_small(summed, small_shapes)):
        grads[n] = g
    grads['conv_w'] = lax.dynamic_slice_in_dim(grads['conv_w'], me * (D_A // 8), D_A // 8, axis=2)

    delta, new_m, new_v = {}, {}, {}
    for n in BIG + ['conv_w']:
        if n in grads_t:
            sh = grads_t[n].shape
            two = lambda v: jnp.swapaxes(v, 1, 2).reshape(-1, sh[-1])
            back = lambda v: jnp.swapaxes(v.reshape(sh), 1, 2)
            g2 = grads_t[n].reshape(-1, sh[-1])
        else:
            sh = w[n].shape
            two = lambda v: v.reshape(-1, sh[-1])
            back = lambda v: v.reshape(sh)
            g2 = two(grads[n])
        dl, nm, nv = _adamw(two(w[n]), g2, two(a['m_' + n]), two(a['v_' + n]), "adamw_" + n)
        delta[n], new_m[n], new_v[n] = back(dl), back(nm), back(nv)
    dl, nm, nv = _adamw(_pack_small([conv_zero] + [w[n] for n in SMALL]), summed,
                        _pack_small([conv_zero] + [a['m_' + n] for n in SMALL]),
                        _pack_small([conv_zero] + [a['v_' + n] for n in SMALL]), "adamw_small")
    for n, v1, v2, v3 in zip(small_names[1:], _unpack_small(dl, small_shapes)[1:], _unpack_small(nm, small_shapes)[1:],
                             _unpack_small(nv, small_shapes)[1:]):
        delta[n], new_m[n], new_v[n] = v1, v2, v3

    return (loss, grad_x, *[grads[n] for n in WEIGHTS], *[delta[n] for n in WEIGHTS], *[new_m[n] for n in WEIGHTS],
            *[new_v[n] for n in WEIGHTS])
```

```python
import functools
import math

import jax
import jax.numpy as jnp
import numpy as np
from jax import lax
from jax.experimental import pallas as pl
from jax.experimental.pallas import tpu as pltpu

F32 = jnp.float32
BF16 = jnp.bfloat16
MESH = pl.DeviceIdType.MESH

DEPTH = 2
ALPHA = (2 * DEPTH) ** 0.25
LN_EPS = 1e-5
RMS_EPS = 1e-6
RG_C = 8.0
CONV_WIDTH = 4
HEAD_DIM = 64
C_GROUP = 16
C_STATE = 64
D_A = 384
D_B = 384
D_C = 256
N_HEADS = D_B // HEAD_DIM
N_GROUPS = D_C // C_GROUP
N_STATE = N_GROUPS * C_STATE
Z_F = 2 * D_A + 3 * D_B
Z_U = Z_F + 128
Z_W = Z_U + D_C
N_IN = Z_F + N_HEADS + D_C
ADAM_LR, ADAM_B1, ADAM_B2, ADAM_EPS, ADAM_WD, ADAM_STEP = 0.001, 0.9, 0.999, 1e-08, 0.01, 10
LANES = 128
NEG = -1e30

WEIGHTS = ['ffn1_w_gate', 'ffn1_w_up', 'ffn1_w_down', 'ln1_g', 'ln1_b', 'w_in', 'conv_w', 'conv_b', 'rg_w_a', 'rg_b_a',
           'rg_w_x', 'rg_b_x', 'rg_lambda', 'fox_b_f', 's5_a_re', 's5_a_im', 's5_log_dt', 's5_b_re', 's5_b_im', 's5_c_re',
           's5_c_im', 's5_d', 's5_w_glu', 'mix_norm_g', 'w_out', 'ln2_g', 'ln2_b', 'ffn2_w_gate', 'ffn2_w_up', 'ffn2_w_down',
           'ln3_g', 'ln3_b']
BIG = ['ffn1_w_gate', 'ffn1_w_up', 'ffn1_w_down', 'w_in', 's5_w_glu', 'w_out', 'ffn2_w_gate', 'ffn2_w_up', 'ffn2_w_down']
SMALL_TAIL = ['fox_b_f', 's5_log_dt']
SMALL = [n for n in WEIGHTS if n not in BIG and n != 'conv_w' and n not in SMALL_TAIL] + SMALL_TAIL


def _sig(x):
    return 1.0 / (1.0 + jnp.exp(-x))


def _gelu(x):
    return 0.5 * x * (1.0 + jnp.tanh(math.sqrt(2.0 / math.pi) * (x + 0.044715 * (x * x * x))))


def _softplus(x):
    return jnp.maximum(x, 0.0) + jnp.log(1.0 + jnp.exp(jnp.minimum(x, -x)))


def _dot(a, b, dims):
    return lax.dot_general(a.astype(BF16), b.astype(BF16), (dims, ((), ())), preferred_element_type=F32)


NN = ((1,), (0,))
NT = ((1,), (1,))
TN = ((0,), (0,))


@jax.custom_vjp
def _bdot(a, w):
    return _dot(a, w, NN)


def _bdot_fwd(a, w):
    return _dot(a, w, NN), (a, w)


def _bdot_bwd(res, ct):
    a, w = res
    return _dot(ct, w, NT), _dot(a, ct, TN)


_bdot.defvjp(_bdot_fwd, _bdot_bwd)


def _ln(pre, g, b):
    mu = jnp.mean(pre, axis=-1, keepdims=True)
    xc = pre - mu
    var = jnp.mean(xc * xc, axis=-1, keepdims=True)
    return xc * lax.rsqrt(var + LN_EPS) * g + b


def _rms(x, g):
    return x * lax.rsqrt(jnp.mean(x * x, axis=-1, keepdims=True) + RMS_EPS) * g


def _tile(n, want):
    return want if n % want == 0 else n


class _Carry:
    def __init__(self, ins, outs, aliases, plan, n):
        self.ins, self.outs, self.aliases, self.plan, self.n = list(ins), list(outs), dict(aliases), plan, n


def _join(a, b):
    na, ma = len(a.ins), len(a.outs)

    def plan(x, y, c, ins, outs):
        return a.plan(x, y, c, ins[:na], outs[:ma]) + b.plan(x, y, c, ins[na:], outs[ma:])

    aliases = dict(a.aliases)
    aliases.update({na + i: ma + j for i, j in b.aliases.items()})
    return _Carry(a.ins + b.ins, a.outs + b.outs, aliases, plan, a.n + b.n)


def _copies(carry, cins, couts, send, recv):
    x, y, c = lax.axis_index("x"), lax.axis_index("y"), lax.axis_index("c")
    res = []
    for k, (s, d, peer) in enumerate(carry.plan(x, y, c, cins, couts)):
        if peer is None:
            res.append(pltpu.make_async_copy(s, d, send.at[k]))
        else:
            res.append(pltpu.make_async_remote_copy(src_ref=s, dst_ref=d, send_sem=send.at[k], recv_sem=recv.at[k],
                                                    device_id=peer, device_id_type=MESH))
    return res


def _call(body, grid, in_specs, out_specs, out_shape, scratch, semantics, name, args, carry=None):
    n_in, n_out, n_scr = len(in_specs), len(out_specs), len(scratch)
    if carry is None:
        res = pl.pallas_call(body, grid=grid, in_specs=in_specs, out_specs=out_specs, out_shape=out_shape,
                             scratch_shapes=scratch, compiler_params=pltpu.CompilerParams(dimension_semantics=semantics),
                             name=name)(*args)
        return list(res), []
    nci, nco = len(carry.ins), len(carry.outs)

    def wrapped(*refs):
        o0 = n_in + nci
        s0 = o0 + n_out + nco
        cins, couts = refs[n_in:o0], refs[o0 + n_out:s0]
        send, recv = refs[s0 + n_scr:]
        first = functools.reduce(jnp.logical_and, [pl.program_id(k) == 0 for k in range(len(grid))])
        last = functools.reduce(jnp.logical_and, [pl.program_id(k) == grid[k] - 1 for k in range(len(grid))])

        @pl.when(first)
        def _():
            for cp in _copies(carry, cins, couts, send, recv):
                cp.start()

        body(*refs[:n_in], *refs[o0:o0 + n_out], *refs[s0:s0 + n_scr])

        @pl.when(last)
        def _():
            for cp in _copies(carry, cins, couts, send, recv):
                cp.wait()

    hbm = pl.BlockSpec(memory_space=pl.ANY)
    res = pl.pallas_call(
        wrapped, grid=grid, in_specs=list(in_specs) + [hbm] * nci, out_specs=list(out_specs) + [hbm] * nco,
        out_shape=list(out_shape) + carry.outs, scratch_shapes=list(scratch) + [pltpu.SemaphoreType.DMA((carry.n,))] * 2,
        input_output_aliases={n_in + i: n_out + j for i, j in carry.aliases.items()},
        compiler_params=pltpu.CompilerParams(dimension_semantics=("arbitrary",) * len(grid), has_side_effects=True),
        name=name)(*args, *carry.ins)
    return list(res[:n_out]), list(res[n_out:])


def _run(carry, name):
    nci, nco = len(carry.ins), len(carry.outs)

    def body(*refs):
        cps = _copies(carry, refs[:nci], refs[nci:nci + nco], refs[-2], refs[-1])
        for cp in cps:
            cp.start()
        for cp in cps:
            cp.wait()

    hbm = pl.BlockSpec(memory_space=pl.ANY)
    return pl.pallas_call(
        body, in_specs=[hbm] * nci, out_specs=[hbm] * nco, out_shape=carry.outs, input_output_aliases=carry.aliases,
        scratch_shapes=[pltpu.SemaphoreType.DMA((carry.n,))] * 2, compiler_params=pltpu.CompilerParams(has_side_effects=True),
        name=name)(*carry.ins)


def _ag_chips(shards):
    def plan(x, y, c, ins, outs):
        res = []
        for src, out in zip(ins, outs):
            dst = out.at[4 * x + 2 * y + c]
            res += [(src, dst, None)] + [(src, dst, (px, py, c)) for px, py in ((1 - x, y), (x, 1 - y), (1 - x, 1 - y))]
        return res

    return _Carry(shards, [jax.ShapeDtypeStruct((8,) + s.shape, s.dtype) for s in shards], {}, plan, 4 * len(shards))


def _ag_sibling(gs):
    def plan(x, y, c, ins, outs):
        return [(out.at[2 * q + c], out.at[2 * q + c], (x, y, 1 - c)) for out in outs for q in range(4)]

    return _Carry(gs, [jax.ShapeDtypeStruct(g.shape, g.dtype) for g in gs], {i: i for i in range(len(gs))}, plan, 4 * len(gs))


def _rs_sibling(full):
    def plan(x, y, c, ins, outs):
        return [(ins[0].at[2 * q + (1 - c)], outs[0].at[q], (x, y, 1 - c)) for q in range(4)]

    return _Carry([full], [jax.ShapeDtypeStruct((4,) + full.shape[1:], full.dtype)], {}, plan, 4)


def _rs_chips(part):
    def plan(x, y, c, ins, outs):
        res = []
        for k, (dx, dy) in enumerate(((1, 0), (0, 1), (1, 1))):
            tx, ty = x ^ dx, y ^ dy
            res.append((ins[0].at[2 * tx + ty], outs[0].at[k], (tx, ty, c)))
        return res

    return _Carry([part], [jax.ShapeDtypeStruct((3,) + part.shape[1:], part.dtype)], {}, plan, 3)


def _mm(a, b, dims, out_dtype, tm, tn, tk, name, add=None, carry=None):
    if dims == 'nn':
        (m, k), n = a.shape, b.shape[1]
        a_spec = pl.BlockSpec((tm, tk), lambda i, j, q: (i, q))
        b_spec = pl.BlockSpec((tk, tn), lambda i, j, q: (q, j))
        dn = NN
    elif dims == 'nt':
        (m, k), n = a.shape, b.shape[0]
        a_spec = pl.BlockSpec((tm, tk), lambda i, j, q: (i, q))
        b_spec = pl.BlockSpec((tn, tk), lambda i, j, q: (j, q))
        dn = NT
    else:
        (k, m), n = a.shape, b.shape[1]
        a_spec = pl.BlockSpec((tk, tm), lambda i, j, q: (q, i))
        b_spec = pl.BlockSpec((tk, tn), lambda i, j, q: (q, j))
        dn = TN
    nk = k // tk
    o_spec = pl.BlockSpec((tm, tn), lambda i, j, q: (i, j))

    def body(*refs):
        if add is None:
            a_ref, b_ref, o_ref, acc_ref = refs
        else:
            a_ref, b_ref, add_ref, o_ref, acc_ref = refs
        q = pl.program_id(2)
        part = _dot(a_ref[...], b_ref[...], dn)

        @pl.when(q == 0)
        def _():
            acc_ref[...] = part

        @pl.when(q > 0)
        def _():
            acc_ref[...] += part

        @pl.when(q == nk - 1)
        def _():
            r = acc_ref[...]
            if add is not None:
                r = r + add_ref[...]
            o_ref[...] = r.astype(o_ref.dtype)

    ins = [a, b] + ([] if add is None else [add])
    specs = [a_spec, b_spec] + ([] if add is None else [o_spec])
    (res,), cres = _call(body, (m // tm, n // tn, nk), specs, [o_spec], [jax.ShapeDtypeStruct((m, n), out_dtype)],
                         [pltpu.VMEM((tm, tn), F32)], ("parallel", "parallel", "arbitrary"), name, ins, carry)
    return res if carry is None else (res, cres)


def _rowwise(fn, rows, params, outs, tm, name):
    t = rows[0].shape[0]
    nr, npar = len(rows), len(params)

    def body(*refs):
        r = [x[...] for x in refs[:nr]]
        p = [x[...] for x in refs[nr:nr + npar]]
        res = fn(*r, *p)
        for o_ref, o in zip(refs[nr + npar:], res):
            o_ref[...] = o.astype(o_ref.dtype)

    in_specs = ([pl.BlockSpec((tm, a.shape[1]), lambda i: (i, 0)) for a in rows]
                + [pl.BlockSpec(p.shape, lambda i: (0, 0)) for p in params])
    return pl.pallas_call(
        body, grid=(t // tm,), in_specs=in_specs,
        out_specs=[pl.BlockSpec((tm, c), lambda i: (i, 0)) for c, _ in outs],
        out_shape=[jax.ShapeDtypeStruct((t, c), d) for c, d in outs],
        compiler_params=pltpu.CompilerParams(dimension_semantics=("parallel",)), name=name)(*rows, *params)


def _rowwise_vjp(fn, rows, params, cots, tm, name, carry=None):
    t = rows[0].shape[0]
    nr, npar, nc = len(rows), len(params), len(cots)

    def body(*refs):
        r = [x[...] for x in refs[:nr]]
        p = [x[...] for x in refs[nr:nr + npar]]
        c = [x[...] for x in refs[nr + npar:nr + npar + nc]]
        o_refs = refs[nr + npar + nc:]
        _, pull = jax.vjp(fn, *r, *p)
        grads = pull(tuple(c))
        for o_ref, g in zip(o_refs[:nr], grads[:nr]):
            o_ref[...] = g
        i = pl.program_id(0)

        @pl.when(i == 0)
        def _():
            for o_ref, g in zip(o_refs[nr:], grads[nr:]):
                o_ref[...] = g

        @pl.when(i > 0)
        def _():
            for o_ref, g in zip(o_refs[nr:], grads[nr:]):
                o_ref[...] += g

    row_spec = lambda a: pl.BlockSpec((tm, a.shape[1]), lambda i: (i, 0))
    par_spec = lambda p: pl.BlockSpec(p.shape, lambda i: (0, 0))
    res, cres = _call(
        body, (t // tm,),
        [row_spec(a) for a in rows] + [par_spec(p) for p in params] + [row_spec(a) for a in cots],
        [row_spec(a) for a in rows] + [par_spec(p) for p in params],
        [jax.ShapeDtypeStruct(a.shape, F32) for a in rows] + [jax.ShapeDtypeStruct(p.shape, F32) for p in params],
        [], ("arbitrary",), name, [*rows, *params, *cots], carry)
    return res[:nr], res[nr:], cres


def _ffn_fwd(x, wgt, wut, wd, ln_g, ln_b, tm, tf, carry=None):
    t, d = x.shape
    f = wgt.shape[0]
    nj = f // tf

    def body(x_ref, wg_ref, wu_ref, wd_ref, g_ref, b_ref, y_ref, pre_ref, gs_ref, us_ref, acc_ref):
        j = pl.program_id(1)
        xv = x_ref[...]
        xb = xv.astype(BF16)
        g = _dot(xb, wg_ref[...], NT)
        u = _dot(xb, wu_ref[...], NT)
        gs_ref[...] = g.astype(BF16)
        us_ref[...] = u.astype(BF16)
        part = _dot(g * _sig(g) * u, wd_ref[...], NN)

        @pl.when(j == 0)
        def _():
            acc_ref[...] = part

        @pl.when(j > 0)
        def _():
            acc_ref[...] += part

        @pl.when(j == nj - 1)
        def _():
            pre = ALPHA * xv + 0.5 * acc_ref[...]
            pre_ref[...] = pre
            y_ref[...] = _ln(pre, g_ref[...], b_ref[...])

    w_spec = pl.BlockSpec((tf, d), lambda i, j: (j, 0))
    x_spec = pl.BlockSpec((tm, d), lambda i, j: (i, 0))
    v_spec = pl.BlockSpec((1, d), lambda i, j: (0, 0))
    h_spec = pl.BlockSpec((tm, tf), lambda i, j: (i, j))
    return _call(
        body, (t // tm, nj), [x_spec, w_spec, w_spec, w_spec, v_spec, v_spec], [x_spec, x_spec, h_spec, h_spec],
        [jax.ShapeDtypeStruct((t, d), F32), jax.ShapeDtypeStruct((t, d), F32),
         jax.ShapeDtypeStruct((t, f), BF16), jax.ShapeDtypeStruct((t, f), BF16)],
        [pltpu.VMEM((tm, d), F32)], ("parallel", "arbitrary"), "ffn_fwd", [x, wgt, wut, wd, ln_g, ln_b], carry)


def _ffn_bwd(dpre, gs, us, wgt, wut, wd, tm, tf, carry=None):
    t, d = dpre.shape
    f = wgt.shape[0]
    nj = f // tf

    def body(dp_ref, gs_ref, us_ref, wg_ref, wu_ref, wd_ref, dx_ref, dg_ref, du_ref, hh_ref, acc_ref):
        j = pl.program_id(1)
        dp = dp_ref[...]
        dh = _dot(0.5 * dp, wd_ref[...], NT)
        g = gs_ref[...].astype(F32)
        u = us_ref[...].astype(F32)
        s = _sig(g)
        sl = g * s
        dg = (dh * u * (s * (1.0 + g * (1.0 - s)))).astype(BF16)
        du = (dh * sl).astype(BF16)
        dg_ref[...] = dg
        du_ref[...] = du
        hh_ref[...] = (0.5 * sl * u).astype(BF16)
        part = _dot(dg, wg_ref[...], NN) + _dot(du, wu_ref[...], NN)

        @pl.when(j == 0)
        def _():
            acc_ref[...] = part

        @pl.when(j > 0)
        def _():
            acc_ref[...] += part

        @pl.when(j == nj - 1)
        def _():
            dx_ref[...] = ALPHA * dp + acc_ref[...]

    w_spec = pl.BlockSpec((tf, d), lambda i, j: (j, 0))
    x_spec = pl.BlockSpec((tm, d), lambda i, j: (i, 0))
    h_spec = pl.BlockSpec((tm, tf), lambda i, j: (i, j))
    return _call(
        body, (t // tm, nj), [x_spec, h_spec, h_spec, w_spec, w_spec, w_spec], [x_spec, h_spec, h_spec, h_spec],
        [jax.ShapeDtypeStruct((t, d), F32)] + [jax.ShapeDtypeStruct((t, f), BF16)] * 3,
        [pltpu.VMEM((tm, d), F32)], ("parallel", "arbitrary"), "ffn_bwd", [dpre, gs, us, wgt, wut, wd], carry)


def _scan8(a_ref, b_ref, out_ref, t, reverse=False):
    w = out_ref.shape[-1]
    sub = lax.broadcasted_iota(jnp.int32, (8, w), 0)

    def step(g, carry):
        r0 = pl.multiple_of((t // 8 - 1 - g if reverse else g) * 8, 8)
        bv = b_ref[pl.ds(r0, 8), :]
        av = None if a_ref is None else a_ref[pl.ds(r0, 8), :]
        for s in (1, 2, 4):
            ok = (sub < 8 - s) if reverse else (sub >= s)
            shift = 8 - s if reverse else s
            b_sh = jnp.where(ok, pltpu.roll(bv, shift, 0), 0.0)
            if av is None:
                bv = bv + b_sh
            else:
                bv = av * b_sh + bv
                av = av * jnp.where(ok, pltpu.roll(av, shift, 0), 1.0)
        h = bv + carry if av is None else bv + av * carry
        out_ref[pl.ds(r0, 8), :] = h
        return jnp.sum(jnp.where(sub == (0 if reverse else 7), h, 0.0), axis=0, keepdims=True)

    lax.fori_loop(0, t // 8, step, jnp.zeros((1, w), F32))


def _rg_local(xa, wa, ba, wx, bx, lam):
    r = _sig(_bdot(xa, wa) + ba)
    i = _sig(_bdot(xa, wx) + bx)
    log_a = -RG_C * r * _softplus(-lam)
    a = jnp.exp(log_a)
    mult = jnp.sqrt(-jnp.tanh(log_a) * (a * a + 1.0))
    return a, mult * (i * xa)


def _conv_taps(ext, n):
    return [ext[8:, :]] + [pltpu.roll(ext, s, 0)[8:, :] for s in (1, 2, 3)]


def _rg_fwd(z, cw, cb, wa, ba, wx, bx, lam):
    t = z.shape[0]
    cr = _tile(t, 256)
    nb = D_A // LANES

    def body(ax_ref, ag_ref, cw_ref, cb_ref, wa_ref, ba_ref, wx_ref, bx_ref, lam_ref, out_ref, h_ref, axp, a_s, b_s):
        axp[pl.ds(0, 8), :] = jnp.zeros((8, LANES), F32)
        pltpu.sync_copy(ax_ref, axp.at[pl.ds(8, t)])
        w = [cw_ref[pl.ds(k, 1), :] for k in range(CONV_WIDTH)]

        def chunk(c, carry):
            t0 = pl.multiple_of(c * cr, cr)
            taps = _conv_taps(axp[pl.ds(t0, cr + 8), :], cr)
            xa = cb_ref[...] + w[3] * taps[0] + w[2] * taps[1] + w[1] * taps[2] + w[0] * taps[3]
            a, gated = _rg_local(xa, wa_ref[...], ba_ref[...], wx_ref[...], bx_ref[...], lam_ref[...])
            a_s[pl.ds(t0, cr), :] = a
            b_s[pl.ds(t0, cr), :] = gated
            return carry

        lax.fori_loop(0, t // cr, chunk, 0)

        _scan8(a_s, b_s, h_ref, t)

        def fin(c, carry):
            t0 = pl.multiple_of(c * cr, cr)
            out_ref[pl.ds(t0, cr), :] = _gelu(ag_ref[pl.ds(t0, cr), :]) * h_ref[pl.ds(t0, cr), :]
            return carry

        lax.fori_loop(0, t // cr, fin, 0)

    col = lambda off: pl.BlockSpec((t, LANES), lambda b: (0, off + b))
    vec = pl.BlockSpec((1, LANES), lambda b: (0, b))
    mat = pl.BlockSpec((None, LANES, LANES), lambda b: (b, 0, 0))
    return pl.pallas_call(
        body, grid=(nb,),
        in_specs=[col(0), col(nb), pl.BlockSpec((CONV_WIDTH, LANES), lambda b: (0, b)), vec, mat, vec, mat, vec, vec],
        out_specs=[col(0), col(0)],
        out_shape=[jax.ShapeDtypeStruct((t, D_A), F32), jax.ShapeDtypeStruct((t, D_A), F32)],
        scratch_shapes=[pltpu.VMEM((t + 8, LANES), F32), pltpu.VMEM((t, LANES), F32), pltpu.VMEM((t, LANES), F32)],
        compiler_params=pltpu.CompilerParams(dimension_semantics=("arbitrary",)), name="rglru_fwd")(
            z, z, cw, cb, wa, ba, wx, bx, lam)


def _rg_bwd(z, h, dout, cw, cb, wa, ba, wx, bx, lam):
    t = z.shape[0]
    cr = _tile(t, 256)
    nb = D_A // LANES

    def body(ax_ref, ag_ref, h_ref, do_ref, cw_ref, cb_ref, wa_ref, ba_ref, wx_ref, bx_ref, lam_ref,
             dax_ref, dag_ref, dcw_ref, dcb_ref, dwa_ref, dba_ref, dwx_ref, dbx_ref, dlam_ref,
             axp, hp, xa_s, a_s, g_s, dxa_s, u_s):
        zero8 = jnp.zeros((8, LANES), F32)
        axp[pl.ds(0, 8), :] = zero8
        hp[pl.ds(0, 8), :] = zero8
        dxa_s[pl.ds(t, 8), :] = zero8
        u_s[pl.ds(t, 8), :] = zero8
        pltpu.sync_copy(ax_ref, axp.at[pl.ds(8, t)])
        pltpu.sync_copy(h_ref, hp.at[pl.ds(8, t)])
        w = [cw_ref[pl.ds(k, 1), :] for k in range(CONV_WIDTH)]
        for ref in (dcw_ref, dcb_ref, dwa_ref, dba_ref, dwx_ref, dbx_ref, dlam_ref):
            ref[...] = jnp.zeros(ref.shape, F32)

        def p1(c, carry):
            t0 = pl.multiple_of(c * cr, cr)
            taps = _conv_taps(axp[pl.ds(t0, cr + 8), :], cr)
            xa = cb_ref[...] + w[3] * taps[0] + w[2] * taps[1] + w[1] * taps[2] + w[0] * taps[3]
            a, _ = _rg_local(xa, wa_ref[...], ba_ref[...], wx_ref[...], bx_ref[...], lam_ref[...])
            xa_s[pl.ds(t0, cr), :] = xa
            a_s[pl.ds(t0, cr), :] = a
            ag = ag_ref[pl.ds(t0, cr), :]
            dov = do_ref[pl.ds(t0, cr), :]
            gel, pull = jax.vjp(_gelu, ag)
            g_s[pl.ds(t0, cr), :] = dov * gel
            u_s[pl.ds(t0, cr), :] = a * (dov * gel)
            dag_ref[pl.ds(t0, cr), :] = pull(dov * h_ref[pl.ds(t0, cr), :])[0]
            return carry

        lax.fori_loop(0, t // cr, p1, 0)
        _scan8(a_s, u_s, u_s, t, reverse=True)

        def p3(c, carry):
            t0 = pl.multiple_of(c * cr, cr)
            g = g_s[pl.ds(t0, cr), :] + pltpu.roll(u_s[pl.ds(t0, cr + 8), :], cr + 7, 0)[:cr, :]
            h_prev = pltpu.roll(hp[pl.ds(t0, cr + 8), :], 1, 0)[8:, :]
            _, pull = jax.vjp(_rg_local, xa_s[pl.ds(t0, cr), :], wa_ref[...], ba_ref[...], wx_ref[...], bx_ref[...],
                              lam_ref[...])
            dxa, dwa, dba, dwx, dbx, dlam = pull((g * h_prev, g))
            dxa_s[pl.ds(t0, cr), :] = dxa
            dwa_ref[...] += dwa
            dba_ref[...] += dba
            dwx_ref[...] += dwx
            dbx_ref[...] += dbx
            dlam_ref[...] += dlam
            return carry

        lax.fori_loop(0, t // cr, p3, 0)

        def p4(c, carry):
            t0 = pl.multiple_of(c * cr, cr)
            ext = dxa_s[pl.ds(t0, cr + 8), :]
            n = cr + 8
            ahead = [ext[:cr, :]] + [pltpu.roll(ext, n - s, 0)[:cr, :] for s in (1, 2, 3)]
            dax_ref[pl.ds(t0, cr), :] = w[3] * ahead[0] + w[2] * ahead[1] + w[1] * ahead[2] + w[0] * ahead[3]
            taps = _conv_taps(axp[pl.ds(t0, cr + 8), :], cr)
            dxa = ahead[0]
            for k in range(CONV_WIDTH):
                dcw_ref[pl.ds(k, 1), :] += jnp.sum(dxa * taps[CONV_WIDTH - 1 - k], axis=0, keepdims=True)
            dcb_ref[...] += jnp.sum(dxa, axis=0, keepdims=True)
            return carry

        lax.fori_loop(0, t // cr, p4, 0)

    col = lambda off: pl.BlockSpec((t, LANES), lambda b: (0, off + b))
    vec = pl.BlockSpec((1, LANES), lambda b: (0, b))
    mat = pl.BlockSpec((None, LANES, LANES), lambda b: (b, 0, 0))
    cws = pl.BlockSpec((CONV_WIDTH, LANES), lambda b: (0, b))
    sds = jax.ShapeDtypeStruct
    return pl.pallas_call(
        body, grid=(nb,),
        in_specs=[col(0), col(nb), col(0), col(0), cws, vec, mat, vec, mat, vec, vec],
        out_specs=[col(0), col(0), cws, vec, mat, vec, mat, vec, vec],
        out_shape=[sds((t, D_A), F32), sds((t, D_A), F32), sds((CONV_WIDTH, D_A), F32), sds((1, D_A), F32),
                   sds((nb, LANES, LANES), F32), sds((1, D_A), F32), sds((nb, LANES, LANES), F32), sds((1, D_A), F32),
                   sds((1, D_A), F32)],
        scratch_shapes=[pltpu.VMEM((t + 8, LANES), F32), pltpu.VMEM((t + 8, LANES), F32), pltpu.VMEM((t, LANES), F32),
                        pltpu.VMEM((t, LANES), F32), pltpu.VMEM((t, LANES), F32), pltpu.VMEM((t + 8, LANES), F32),
                        pltpu.VMEM((t + 8, LANES), F32)],
        compiler_params=pltpu.CompilerParams(dimension_semantics=("arbitrary",)), name="rglru_bwd")(
            z, z, h, dout, cw, cb, wa, ba, wx, bx, lam)


def _fgate_fwd(z, bf):
    t = z.shape[0]

    def body(zf_ref, bf_ref, c_ref):
        c_ref[...] = -_softplus(-(zf_ref[...] + bf_ref[...]))
        _scan8(None, c_ref, c_ref, t)

    return pl.pallas_call(
        body, grid=(1,), in_specs=[pl.BlockSpec((t, LANES), lambda i: (0, Z_F // LANES)), pl.BlockSpec((1, LANES), lambda i: (0, 0))],
        out_specs=pl.BlockSpec((t, LANES), lambda i: (0, 0)), out_shape=jax.ShapeDtypeStruct((t, LANES), F32),
        compiler_params=pltpu.CompilerParams(dimension_semantics=("arbitrary",)), name="fgate_fwd")(z, bf)


def _fgate_bwd(z, bf, dc):
    t = z.shape[0]

    def body(zf_ref, bf_ref, dc_ref, dz_ref, db_ref):
        _scan8(None, dc_ref, dz_ref, t, reverse=True)
        dz = dz_ref[...] * _sig(-(zf_ref[...] + bf_ref[...]))
        dz_ref[...] = dz
        db_ref[...] = jnp.sum(dz, axis=0, keepdims=True)

    return pl.pallas_call(
        body, grid=(1,),
        in_specs=[pl.BlockSpec((t, LANES), lambda i: (0, Z_F // LANES)), pl.BlockSpec((1, LANES), lambda i: (0, 0)),
                  pl.BlockSpec((t, LANES), lambda i: (0, 0))],
        out_specs=[pl.BlockSpec((t, LANES), lambda i: (0, 0)), pl.BlockSpec((1, LANES), lambda i: (0, 0))],
        out_shape=[jax.ShapeDtypeStruct((t, LANES), F32), jax.ShapeDtypeStruct((1, LANES), F32)],
        compiler_params=pltpu.CompilerParams(dimension_semantics=("arbitrary",)), name="fgate_bwd")(z, bf, dc)


def _cast_rows(src_ref, dst_ref, t, rows, fn):
    def cp(c, carry):
        r0 = pl.multiple_of(c * rows, rows)
        dst_ref[pl.ds(r0, rows), :] = fn(src_ref[pl.ds(r0, rows), :]).astype(dst_ref.dtype)
        return carry

    lax.fori_loop(0, t // rows, cp, 0)


def _attn_groups(t):
    tq = _tile(t, 256)
    nq = t // tq
    grp = 4 if nq % 4 == 0 else 1
    return tq, nq, grp


def _attn_fwd(z, crow, carry=None):
    t = z.shape[0]
    tq, nq, grp = _attn_groups(t)
    tk = grp * tq
    scale = HEAD_DIM ** -0.5

    def body(q_ref, k_ref, v_ref, cr_ref, o_ref, lse_ref, kb_s, vb_s):
        lane = lax.broadcasted_iota(jnp.int32, (1, LANES), 1)
        hmask = [(lane // HEAD_DIM) == hh for hh in range(2)]
        _cast_rows(k_ref, kb_s, t, tq, lambda v: v)
        _cast_rows(v_ref, vb_s, t, tq, lambda v: v)

        def qblock(g, r):
            q0 = pl.multiple_of((g * grp + r) * tq, tq)
            qv = q_ref[pl.ds(q0, tq), :] * scale
            qa = [jnp.where(hmask[hh], qv, 0.0).astype(BF16) for hh in range(2)]

            def update(st, k0, width, off):
                kb = kb_s[pl.ds(k0, width), :]
                vb = vb_s[pl.ds(k0, width), :]
                new = []
                for hh in range(2):
                    m, l, acc = st[hh]
                    s = _dot(qa[hh], kb, NT) - cr_ref[hh, :, pl.ds(k0, width)]
                    if off is not None:
                        keep = (lax.broadcasted_iota(jnp.int32, (tq, width), 0) + off
                                >= lax.broadcasted_iota(jnp.int32, (tq, width), 1))
                        s = jnp.where(keep, s, NEG)
                    m_new = jnp.maximum(m, jnp.max(s, axis=-1, keepdims=True))
                    p = jnp.exp(s - m_new)
                    corr = jnp.exp(m - m_new)
                    new.append((m_new, corr * l + jnp.sum(p, axis=-1, keepdims=True), corr * acc + _dot(p, vb, NN)))
                return tuple(new)

            one = (jnp.full((tq, 1), NEG, F32), jnp.zeros((tq, 1), F32), jnp.zeros((tq, LANES), F32))
            st = lax.fori_loop(0, g, lambda j, st: update(st, pl.multiple_of(j * tk, tk), tk, None), (one, one))
            st = update(st, pl.multiple_of(g * tk, tk), (r + 1) * tq, r * tq)
            o_ref[pl.ds(q0, tq), :] = jnp.where(hmask[0], st[0][2] / st[0][1], st[1][2] / st[1][1])
            for hh in range(2):
                lse_ref[hh, pl.ds(q0, tq), :] = st[hh][0] + jnp.log(st[hh][1])

        def group(g, carry):
            for r in range(grp):
                qblock(g, r)
            return carry

        lax.fori_loop(0, nq // grp, group, 0)

    base = 2 * D_A // LANES
    nh = D_B // LANES
    col = lambda off: pl.BlockSpec((t, LANES), lambda p: (0, off + p))
    return _call(
        body, (nh,), [col(base), col(base + nh), col(base + 2 * nh), pl.BlockSpec((2, 1, t), lambda p: (p, 0, 0))],
        [col(0), pl.BlockSpec((2, t, 1), lambda p: (p, 0, 0))],
        [jax.ShapeDtypeStruct((t, D_B), F32), jax.ShapeDtypeStruct((N_HEADS, t, 1), F32)],
        [pltpu.VMEM((t, LANES), BF16), pltpu.VMEM((t, LANES), BF16)], ("parallel",), "attn_fwd", [z, z, z, crow], carry)


def _attn_bwd(z, crow, lse, do, carry=None):
    t = z.shape[0]
    tq, nq, grp = _attn_groups(t)
    tw = grp * tq
    scale = HEAD_DIM ** -0.5

    def body(q_ref, k_ref, v_ref, cr_ref, lse_ref, do_ref, dq_ref, dk_ref, dv_ref, dc_ref, qa_s, da_s, kb_s, vb_s, dl_s):
        lane = lax.broadcasted_iota(jnp.int32, (1, LANES), 1)
        hmask = [(lane // HEAD_DIM) == hh for hh in range(2)]
        _cast_rows(k_ref, kb_s, t, tq, lambda v: v)
        _cast_rows(v_ref, vb_s, t, tq, lambda v: v)
        for hh in range(2):
            _cast_rows(q_ref, qa_s.at[hh], t, tq, lambda v, hh=hh: jnp.where(hmask[hh], v * scale, 0.0))
            _cast_rows(do_ref, da_s.at[hh], t, tq, lambda v, hh=hh: jnp.where(hmask[hh], v, 0.0))
        _cast_rows(q_ref, dq_ref, t, tq, lambda v: jnp.zeros_like(v))

        def probs(hh, q0, nq_rows, k0, nk_rows, off):
            s = _dot(qa_s[hh, pl.ds(q0, nq_rows), :], kb_s[pl.ds(k0, nk_rows), :], NT) - cr_ref[hh, :, pl.ds(k0, nk_rows)]
            p = jnp.exp(s - lse_ref[hh, pl.ds(q0, nq_rows), :])
            if off is not None:
                keep = (lax.broadcasted_iota(jnp.int32, (nq_rows, nk_rows), 0) + off
                        >= lax.broadcasted_iota(jnp.int32, (nq_rows, nk_rows), 1))
                p = jnp.where(keep, p, 0.0)
            return p, _dot(da_s[hh, pl.ds(q0, nq_rows), :], vb_s[pl.ds(k0, nk_rows), :], NT)

        def delta(g, r):
            q0 = pl.multiple_of((g * grp + r) * tq, tq)

            def add(k0, width, off, acc):
                res = []
                for hh in range(2):
                    p, dp = probs(hh, q0, tq, k0, width, off)
                    res.append(acc[hh] + jnp.sum(p * dp, axis=-1, keepdims=True))
                return tuple(res)

            zcol = jnp.zeros((tq, 1), F32)
            acc = lax.fori_loop(0, g, lambda j, acc: add(pl.multiple_of(j * tw, tw), tw, None, acc), (zcol, zcol))
            acc = add(pl.multiple_of(g * tw, tw), (r + 1) * tq, r * tq, acc)
            for hh in range(2):
                dl_s[hh, pl.ds(q0, tq), :] = acc[hh]

        def delta_group(g, carry):
            for r in range(grp):
                delta(g, r)
            return carry

        lax.fori_loop(0, nq // grp, delta_group, 0)

        def kblock(g, r):
            k0 = pl.multiple_of((g * grp + r) * tq, tq)
            kb = kb_s[pl.ds(k0, tq), :]

            def upd(q0, height, off, st):
                dk, dv, dc = st[0], st[1], [st[2], st[3]]
                dqs = []
                for hh in range(2):
                    p, dp = probs(hh, q0, height, k0, tq, off)
                    ds = p * (dp - dl_s[hh, pl.ds(q0, height), :])
                    dv = dv + _dot(p, da_s[hh, pl.ds(q0, height), :], TN)
                    dk = dk + _dot(ds, qa_s[hh, pl.ds(q0, height), :], TN)
                    dqs.append(_dot(ds, kb, NN))
                    dc[hh] = dc[hh] - jnp.sum(ds, axis=0, keepdims=True)
                dq_ref[pl.ds(q0, height), :] += jnp.where(hmask[0], dqs[0], dqs[1]) * scale
                return dk, dv, dc[0], dc[1]

            zero = jnp.zeros((tq, LANES), F32)
            zrow = jnp.zeros((1, tq), F32)
            st = upd(k0, (grp - r) * tq, 0, (zero, zero, zrow, zrow))
            st = lax.fori_loop(g + 1, nq // grp, lambda i, st: upd(pl.multiple_of(i * tw, tw), tw, None, st), st)
            dk_ref[pl.ds(k0, tq), :] = st[0]
            dv_ref[pl.ds(k0, tq), :] = st[1]
            for hh in range(2):
                dc_ref[hh, :, pl.ds(k0, tq)] = st[2 + hh]

        def kgroup(g, carry):
            for r in range(grp):
                kblock(g, r)
            return carry

        lax.fori_loop(0, nq // grp, kgroup, 0)

    base = 2 * D_A // LANES
    nh = D_B // LANES
    col = lambda off: pl.BlockSpec((t, LANES), lambda p: (0, off + p))
    ccs = pl.BlockSpec((2, t, 1), lambda p: (p, 0, 0))
    crs = pl.BlockSpec((2, 1, t), lambda p: (p, 0, 0))
    return _call(
        body, (nh,), [col(base), col(base + nh), col(base + 2 * nh), crs, ccs, col(0)], [col(0), col(0), col(0), crs],
        [jax.ShapeDtypeStruct((t, D_B), F32)] * 3 + [jax.ShapeDtypeStruct((N_HEADS, 1, t), F32)],
        [pltpu.VMEM((2, t, LANES), BF16), pltpu.VMEM((2, t, LANES), BF16), pltpu.VMEM((t, LANES), BF16),
         pltpu.VMEM((t, LANES), BF16), pltpu.VMEM((2, t, 1), F32)], ("parallel",), "attn_bwd", [z, z, z, crow, lse, do], carry)


def _s5_disc(a_re, a_im, log_dt, b_re, b_im):
    dt = jnp.exp(log_dt)
    mag = jnp.exp(a_re * dt)
    ar = mag * jnp.cos(a_im * dt)
    ai = mag * jnp.sin(a_im * dt)
    den = a_re * a_re + a_im * a_im
    kr = ((ar - 1.0) * a_re + ai * a_im) / den
    ki = (ai * a_re - (ar - 1.0) * a_im) / den
    kr3, ki3 = kr[:, None, :], ki[:, None, :]
    return ar, ai, kr3 * b_re - ki3 * b_im, kr3 * b_im + ki3 * b_re


def _s5_prep(a_re, a_im, log_dt, b_re, b_im):
    g, p = a_re.shape
    gc = b_re.shape[1]

    def body(*refs):
        res = _s5_disc(*[r[...] for r in refs[:5]])
        for o_ref, v in zip(refs[5:], res):
            o_ref[...] = v

    sds = jax.ShapeDtypeStruct
    return pl.pallas_call(body, out_shape=[sds((g, p), F32), sds((g, p), F32), sds((g, gc, p), F32), sds((g, gc, p), F32)],
                          name="s5_prep")(a_re, a_im, log_dt, b_re, b_im)


def _s5_prep_bwd(a_re, a_im, log_dt, b_re, b_im, d_ar, d_ai, d_br, d_bi):
    ins = (a_re, a_im, log_dt, b_re, b_im)

    def body(*refs):
        vals = [r[...] for r in refs[:5]]
        cts = tuple(r[...] for r in refs[5:9])
        _, pull = jax.vjp(_s5_disc, *vals)
        for o_ref, v in zip(refs[9:], pull(cts)):
            o_ref[...] = v

    return pl.pallas_call(body, out_shape=[jax.ShapeDtypeStruct(a.shape, F32) for a in ins], name="s5_prep_bwd")(
        *ins, d_ar, d_ai, d_br, d_bi)


def _s5_scan_rows(t, ar, ai, hr_s, hi_s, off, reverse):
    n = ar.shape[1]
    if reverse:
        ai = -ai
    sub = lax.broadcasted_iota(jnp.int32, (8, n), 0)
    cmul = lambda xr, xi, yr, yi: (xr * yr - xi * yi, xr * yi + xi * yr)
    pw = [(ar, ai)]
    for _ in range(7):
        pw.append(cmul(*pw[-1], ar, ai))
    pr = jnp.zeros((8, n), F32)
    pi = jnp.zeros((8, n), F32)
    for r in range(8):
        k = 7 - r if reverse else r
        pr = jnp.where(sub == r, pw[k][0], pr)
        pi = jnp.where(sub == r, pw[k][1], pi)
    step_r, step_i = {}, {}
    for s in (1, 2, 4):
        ok = (sub < 8 - s) if reverse else (sub >= s)
        step_r[s] = jnp.where(ok, pw[s - 1][0], 0.0)
        step_i[s] = jnp.where(ok, pw[s - 1][1], 0.0)

    def step(g, carry):
        cr, ci = carry
        r0 = pl.multiple_of(off + (t // 8 - 1 - g if reverse else g) * 8, 8)
        br = hr_s[pl.ds(r0, 8), :]
        bi = hi_s[pl.ds(r0, 8), :]
        for s in (1, 2, 4):
            shift = 8 - s if reverse else s
            mr, mi = cmul(step_r[s], step_i[s], pltpu.roll(br, shift, 0), pltpu.roll(bi, shift, 0))
            br, bi = br + mr, bi + mi
        mr, mi = cmul(pr, pi, cr, ci)
        br, bi = br + mr, bi + mi
        hr_s[pl.ds(r0, 8), :] = br
        hi_s[pl.ds(r0, 8), :] = bi
        edge = sub == (0 if reverse else 7)
        return (jnp.sum(jnp.where(edge, br, 0.0), axis=0, keepdims=True),
                jnp.sum(jnp.where(edge, bi, 0.0), axis=0, keepdims=True))

    zero = jnp.zeros((1, n), F32)
    lax.fori_loop(0, t // 8, step, (zero, zero))


def _s5_fwd(z, bd_re, bd_im, ab_re, ab_im, cd_re, cd_im, dvec, carry=None):
    t = z.shape[0]
    cr = _tile(t, 256)
    ns = N_STATE // 2

    def body(u_ref, br_ref, bi_ref, ar_ref, ai_ref, cre_ref, cim_ref, d_ref, y_ref, hr_s, hi_s):
        def p1(c, carry):
            t0 = pl.multiple_of(c * cr, cr)
            u = u_ref[pl.ds(t0, cr), :]
            hr_s[pl.ds(t0, cr), :] = _dot(u, br_ref[...], NN)
            hi_s[pl.ds(t0, cr), :] = _dot(u, bi_ref[...], NN)
            return carry

        lax.fori_loop(0, t // cr, p1, 0)
        _s5_scan_rows(t, ar_ref[...], ai_ref[...], hr_s, hi_s, 0, False)

        def p3(c, carry):
            t0 = pl.multiple_of(c * cr, cr)
            y_ref[pl.ds(t0, cr), :] = (_dot(hr_s[pl.ds(t0, cr), :], cre_ref[...], NN)
                                       - _dot(hi_s[pl.ds(t0, cr), :], cim_ref[...], NN)
                                       + d_ref[...] * u_ref[pl.ds(t0, cr), :])
            return carry

        lax.fori_loop(0, t // cr, p3, 0)

    blk = lambda r, c: pl.BlockSpec((None, r, c), lambda b: (b, 0, 0))
    return _call(
        body, (2,),
        [pl.BlockSpec((t, LANES), lambda b: (0, Z_U // LANES + b)), blk(LANES, ns), blk(LANES, ns), blk(1, ns),
         blk(1, ns), blk(ns, LANES), blk(ns, LANES), pl.BlockSpec((1, LANES), lambda b: (0, b))],
        [pl.BlockSpec((t, LANES), lambda b: (0, b))], [jax.ShapeDtypeStruct((t, D_C), F32)],
        [pltpu.VMEM((t, ns), F32), pltpu.VMEM((t, ns), F32)], ("arbitrary",), "s5_fwd",
        [z, bd_re, bd_im, ab_re, ab_im, cd_re, cd_im, dvec], carry)


def _s5_bwd(z, dy, bd_re, bd_im, ab_re, ab_im, cd_re, cd_im, dvec):
    t = z.shape[0]
    cr = _tile(t, 256)
    ns = N_STATE // 2

    def body(u_ref, dy_ref, br_ref, bi_ref, ar_ref, ai_ref, cre_ref, cim_ref, d_ref,
             du_ref, dbr_ref, dbi_ref, dar_ref, dai_ref, dcre_ref, dcim_ref, dd_ref, hr_s, hi_s, gr_s, gi_s):
        zero8 = jnp.zeros((8, ns), F32)
        hr_s[pl.ds(0, 8), :] = zero8
        hi_s[pl.ds(0, 8), :] = zero8
        for ref in (dbr_ref, dbi_ref, dar_ref, dai_ref, dcre_ref, dcim_ref, dd_ref):
            ref[...] = jnp.zeros(ref.shape, F32)

        def p1(c, carry):
            t0 = pl.multiple_of(c * cr, cr)
            u = u_ref[pl.ds(t0, cr), :]
            hr_s[pl.ds(t0 + 8, cr), :] = _dot(u, br_ref[...], NN)
            hi_s[pl.ds(t0 + 8, cr), :] = _dot(u, bi_ref[...], NN)
            return carry

        lax.fori_loop(0, t // cr, p1, 0)
        _s5_scan_rows(t, ar_ref[...], ai_ref[...], hr_s, hi_s, 8, False)

        def p3(c, carry):
            t0 = pl.multiple_of(c * cr, cr)
            dyv = dy_ref[pl.ds(t0, cr), :]
            u = u_ref[pl.ds(t0, cr), :]
            gr_s[pl.ds(t0, cr), :] = _dot(dyv, cre_ref[...], NT)
            gi_s[pl.ds(t0, cr), :] = -_dot(dyv, cim_ref[...], NT)
            dcre_ref[...] += _dot(hr_s[pl.ds(t0 + 8, cr), :], dyv, TN)
            dcim_ref[...] -= _dot(hi_s[pl.ds(t0 + 8, cr), :], dyv, TN)
            dd_ref[...] += jnp.sum(dyv * u, axis=0, keepdims=True)
            du_ref[pl.ds(t0, cr), :] = dyv * d_ref[...]
            return carry

        lax.fori_loop(0, t // cr, p3, 0)
        _s5_scan_rows(t, ar_ref[...], ai_ref[...], gr_s, gi_s, 0, True)

        def p5(c, carry):
            t0 = pl.multiple_of(c * cr, cr)
            u = u_ref[pl.ds(t0, cr), :]
            gr = gr_s[pl.ds(t0, cr), :]
            gi = gi_s[pl.ds(t0, cr), :]
            dbr_ref[...] += _dot(u, gr, TN)
            dbi_ref[...] += _dot(u, gi, TN)
            du_ref[pl.ds(t0, cr), :] += _dot(gr, br_ref[...], NT) + _dot(gi, bi_ref[...], NT)
            hpr = pltpu.roll(hr_s[pl.ds(t0, cr + 8), :], 1, 0)[8:, :]
            hpi = pltpu.roll(hi_s[pl.ds(t0, cr + 8), :], 1, 0)[8:, :]
            dar_ref[...] += jnp.sum(gr * hpr + gi * hpi, axis=0, keepdims=True)
            dai_ref[...] += jnp.sum(gi * hpr - gr * hpi, axis=0, keepdims=True)
            return carry

        lax.fori_loop(0, t // cr, p5, 0)

    blk = lambda r, c: pl.BlockSpec((None, r, c), lambda b: (b, 0, 0))
    ucol = pl.BlockSpec((t, LANES), lambda b: (0, Z_U // LANES + b))
    ycol = pl.BlockSpec((t, LANES), lambda b: (0, b))
    dsp = pl.BlockSpec((1, LANES), lambda b: (0, b))
    sds = jax.ShapeDtypeStruct
    return pl.pallas_call(
        body, grid=(2,),
        in_specs=[ucol, ycol, blk(LANES, ns), blk(LANES, ns), blk(1, ns), blk(1, ns), blk(ns, LANES), blk(ns, LANES), dsp],
        out_specs=[ycol, blk(LANES, ns), blk(LANES, ns), blk(1, ns), blk(1, ns), blk(ns, LANES), blk(ns, LANES), dsp],
        out_shape=[sds((t, D_C), F32), sds((2, LANES, ns), F32), sds((2, LANES, ns), F32), sds((2, 1, ns), F32),
                   sds((2, 1, ns), F32), sds((2, ns, LANES), F32), sds((2, ns, LANES), F32), sds((1, D_C), F32)],
        scratch_shapes=[pltpu.VMEM((t + 8, ns), F32), pltpu.VMEM((t + 8, ns), F32), pltpu.VMEM((t, ns), F32),
                        pltpu.VMEM((t, ns), F32)],
        compiler_params=pltpu.CompilerParams(dimension_semantics=("arbitrary",)), name="s5_bwd")(
            z, dy, bd_re, bd_im, ab_re, ab_im, cd_re, cd_im, dvec)


def _mix_out(out_a, out_b, yc, x1, ga, gb, gc, wglu, wout, ln_g, ln_b):
    yg = _gelu(yc)
    out_c = yg * _sig(_bdot(yg, wglu))
    o = jnp.concatenate([_rms(out_a, ga), _rms(out_b, gb), _rms(out_c, gc)], axis=-1)
    return (_ln(ALPHA * x1 + _bdot(o, wout), ln_g, ln_b),)


def _ln_only(pre, g, b):
    return (_ln(pre, g, b),)


def _loss_head(y, target, tm):
    t, d = y.shape

    def body(y_ref, t_ref, dy_ref, l_ref):
        i = pl.program_id(0)
        e = y_ref[...] - t_ref[...]
        dy_ref[...] = e * (1.0 / d)
        part = 0.5 * jnp.sum(jnp.sum(e * e, axis=-1, keepdims=True) * (1.0 / d), axis=0, keepdims=True)
        row = jnp.where(lax.broadcasted_iota(jnp.int32, (1, LANES), 1) == 0, part, 0.0)

        @pl.when(i == 0)
        def _():
            l_ref[...] = row

        @pl.when(i > 0)
        def _():
            l_ref[...] += row

    spec = pl.BlockSpec((tm, d), lambda i: (i, 0))
    return pl.pallas_call(
        body, grid=(t // tm,), in_specs=[spec, spec], out_specs=[spec, pl.BlockSpec((1, LANES), lambda i: (0, 0))],
        out_shape=[jax.ShapeDtypeStruct((t, d), F32), jax.ShapeDtypeStruct((1, LANES), F32)],
        compiler_params=pltpu.CompilerParams(dimension_semantics=("arbitrary",)), name="loss_head")(y, target)


def _adamw(w, g, m, v, name):
    r, c = w.shape
    tr = r
    for cand in (512, 256, 352, 128):
        if r % cand == 0:
            tr = cand
            break

    def body(w_ref, g_ref, m_ref, v_ref, d_ref, nm_ref, nv_ref):
        gv = g_ref[...]
        mn = ADAM_B1 * m_ref[...] + (1.0 - ADAM_B1) * gv
        vn = ADAM_B2 * v_ref[...] + (1.0 - ADAM_B2) * (gv * gv)
        m_hat = mn / (1.0 - ADAM_B1 ** ADAM_STEP)
        v_hat = vn / (1.0 - ADAM_B2 ** ADAM_STEP)
        d_ref[...] = -ADAM_LR * (m_hat / (jnp.sqrt(v_hat) + ADAM_EPS) + ADAM_WD * w_ref[...])
        nm_ref[...] = mn
        nv_ref[...] = vn

    spec = pl.BlockSpec((tr, c), lambda i: (i, 0))
    return pl.pallas_call(
        body, grid=(r // tr,), in_specs=[spec] * 4, out_specs=[spec] * 3,
        out_shape=[jax.ShapeDtypeStruct((r, c), F32)] * 3,
        compiler_params=pltpu.CompilerParams(dimension_semantics=("parallel",)), name=name)(w, g, m, v)


def _row_tile(r):
    for cand in (512, 448, 352, 256, 128):
        if r % cand == 0:
            return cand
    return r


def _pair_add(a, b, idx, name, out_dtype):
    _, r, w = a.shape
    tr = _row_tile(r)

    def body(i_ref, a_ref, b_ref, o_ref):
        o_ref[...] = (a_ref[...].astype(F32) + b_ref[...].astype(F32)).astype(o_ref.dtype)

    grid_spec = pltpu.PrefetchScalarGridSpec(
        num_scalar_prefetch=1, grid=(4, r // tr),
        in_specs=[pl.BlockSpec((None, tr, w), lambda q, i, s: (2 * q + s[0], i, 0)),
                  pl.BlockSpec((None, tr, w), lambda q, i, s: (q, i, 0))],
        out_specs=pl.BlockSpec((None, tr, w), lambda q, i, s: (q, i, 0)))
    return pl.pallas_call(body, grid_spec=grid_spec, out_shape=jax.ShapeDtypeStruct((4, r, w), out_dtype),
                          compiler_params=pltpu.CompilerParams(dimension_semantics=("parallel", "parallel")), name=name)(
                              idx, a, b)


def _quad_add(p, rb, idx, name):
    _, r, w = p.shape
    tr = _row_tile(r)

    def body(i_ref, p_ref, r0, r1, r2, o_ref):
        o_ref[...] = ((p_ref[...].astype(F32) + r0[...].astype(F32)) + r1[...].astype(F32)) + r2[...].astype(F32)

    grid_spec = pltpu.PrefetchScalarGridSpec(
        num_scalar_prefetch=1, grid=(r // tr,),
        in_specs=[pl.BlockSpec((None, tr, w), lambda i, s: (s[0], i, 0))]
        + [pl.BlockSpec((None, tr, w), functools.partial(lambda i, s, k: (k, i, 0), k=k)) for k in range(3)],
        out_specs=pl.BlockSpec((tr, w), lambda i, s: (i, 0)))
    return pl.pallas_call(body, grid_spec=grid_spec, out_shape=jax.ShapeDtypeStruct((r, w), F32),
                          compiler_params=pltpu.CompilerParams(dimension_semantics=("parallel",)), name=name)(
                              idx, p, rb, rb, rb)


def _gather_small(buf):
    def plan(x, y, c, ins, outs):
        dst = outs[0].at[4 * x + 2 * y + c]
        res = [(ins[0], dst, None)]
        for rel in range(1, 8):
            res.append((ins[0], dst, (x ^ (rel >> 2), y ^ ((rel >> 1) & 1), c ^ (rel & 1))))
        return res

    return _Carry([buf], [jax.ShapeDtypeStruct((8,) + buf.shape, buf.dtype)], {}, plan, 8)


def _sum_small(allb):
    _, r, w = allb.shape

    def body(a_ref, o_ref):
        s = a_ref[0]
        for k in range(1, 8):
            s = s + a_ref[k]
        o_ref[...] = s

    return pl.pallas_call(body, out_shape=jax.ShapeDtypeStruct((r, w), F32), name="ar_small_sum")(allb)


def _pad_rows(a, rows):
    return jnp.pad(a, ((0, rows - a.shape[0]), (0, 0)))


def _pack_small(arrs):
    rows, tail = [], []
    for a in arrs:
        if not tail and a.size % LANES == 0:
            rows.append(a.reshape(-1, LANES))
        else:
            tail.append(a.reshape(-1))
    n_rows = sum(r.shape[0] for r in rows)
    n_tail = sum(int(v.size) for v in tail)
    tail_rows = -(-n_tail // LANES)
    total_rows = n_rows + tail_rows + (-(n_rows + tail_rows)) % 8
    if tail:
        tail.append(jnp.zeros((tail_rows * LANES - n_tail,), F32))
        rows.append(jnp.concatenate(tail).reshape(tail_rows, LANES))
    if total_rows > n_rows + tail_rows:
        rows.append(jnp.zeros((total_rows - n_rows - tail_rows, LANES), F32))
    return jnp.concatenate(rows, axis=0)


def _unpack_small(buf, shapes):
    out, off = [], 0
    flat = None
    for s in shapes:
        n = int(np.prod(s))
        if off % LANES == 0 and n % LANES == 0:
            out.append(buf[off // LANES:(off + n) // LANES].reshape(s))
        else:
            flat = buf.reshape(-1) if flat is None else flat
            out.append(flat[off:off + n].reshape(s))
        off += n
    return out


def _block_diag(blocks, nb):
    m, r, c = blocks.shape
    n = m // nb
    eye = jnp.eye(n, dtype=blocks.dtype)
    return (blocks.reshape(nb, n, r, 1, c) * eye[None, :, None, :, None]).reshape(nb, n * r, n * c)


def _diag_blocks(dense, n):
    nb = dense.shape[0]
    r, c = dense.shape[1] // n, dense.shape[2] // n
    eye = jnp.eye(n, dtype=dense.dtype)
    return jnp.sum(dense.reshape(nb, n, r, n, c) * eye[None, :, None, :, None], axis=3).reshape(nb * n, r, c)


def kernel(x, ffn1_w_gate, ffn1_w_up, ffn1_w_down, ln1_g, ln1_b, w_in, conv_w, conv_b, rg_w_a, rg_b_a, rg_w_x, rg_b_x, rg_lambda, fox_b_f, s5_a_re, s5_a_im, s5_log_dt, s5_b_re, s5_b_im, s5_c_re, s5_c_im, s5_d, s5_w_glu, mix_norm_g, w_out, ln2_g, ln2_b, ffn2_w_gate, ffn2_w_up, ffn2_w_down, ln3_g, ln3_b, loss_target, m_ffn1_w_gate, m_ffn1_w_up, m_ffn1_w_down, m_ln1_g, m_ln1_b, m_w_in, m_conv_w, m_conv_b, m_rg_w_a, m_rg_b_a, m_rg_w_x, m_rg_b_x, m_rg_lambda, m_fox_b_f, m_s5_a_re, m_s5_a_im, m_s5_log_dt, m_s5_b_re, m_s5_b_im, m_s5_c_re, m_s5_c_im, m_s5_d, m_s5_w_glu, m_mix_norm_g, m_w_out, m_ln2_g, m_ln2_b, m_ffn2_w_gate, m_ffn2_w_up, m_ffn2_w_down, m_ln3_g, m_ln3_b, v_ffn1_w_gate, v_ffn1_w_up, v_ffn1_w_down, v_ln1_g, v_ln1_b, v_w_in, v_conv_w, v_conv_b, v_rg_w_a, v_rg_b_a, v_rg_w_x, v_rg_b_x, v_rg_lambda, v_fox_b_f, v_s5_a_re, v_s5_a_im, v_s5_log_dt, v_s5_b_re, v_s5_b_im, v_s5_c_re, v_s5_c_im, v_s5_d, v_s5_w_glu, v_mix_norm_g, v_w_out, v_ln2_g, v_ln2_b, v_ffn2_w_gate, v_ffn2_w_up, v_ffn2_w_down, v_ln3_g, v_ln3_b):
    a = dict(locals())
    w = {n: a[n] for n in WEIGHTS}
    t, d = x.shape[1], x.shape[2]
    f = ffn1_w_down.shape[1] * 8
    fs, ds = f // 8, d // 8
    mx, my, mc = lax.axis_index("x"), lax.axis_index("y"), lax.axis_index("c")
    me = 4 * mx + 2 * my + mc
    tm = _tile(t, 512)
    tf = f // 2
    win_rows = ds * Z_W // d

    FFN1, MIXW, FFN2 = ['g1', 'u1', 'd1'], ['win', 'wout', 'glu', 'conv'], ['g2', 'u2', 'd2']
    glu_rows = D_C * D_C // (8 * d)

    def shard_segs(l):
        wi = w['w_in'][l]
        win_p = jnp.concatenate([wi[:, :Z_F + N_HEADS], jnp.zeros((ds, Z_U - Z_F - N_HEADS), F32), wi[:, Z_F + N_HEADS:]], axis=1)
        conv_bits = lax.bitcast_convert_type(w['conv_w'][l], BF16).reshape(1, -1)
        segs = dict(g1=w['ffn1_w_gate'][l].T, u1=w['ffn1_w_up'][l].T, d1=w['ffn1_w_down'][l],
                    g2=w['ffn2_w_gate'][l].T, u2=w['ffn2_w_up'][l].T, d2=w['ffn2_w_down'][l],
                    win=win_p.reshape(win_rows, d), wout=w['w_out'][l], glu=_pad_rows(w['s5_w_glu'][l].reshape(-1, d), 16))
        segs = {k: v.astype(BF16) for k, v in segs.items()}
        segs['conv'] = _pad_rows(jnp.pad(conv_bits, ((0, 0), (0, d - conv_bits.shape[1]))), 16)
        return segs

    shards = [shard_segs(l) for l in range(DEPTH)]
    wts = {}

    def cat(keys):
        return [shards[l][k] for l, k in keys]

    def split(gs, keys):
        for (l, k), g in zip(keys, gs):
            wts[(l, k)] = g

    grp_a = [(0, k) for k in FFN1]
    grp_b = [(0, k) for k in MIXW]
    grp_c = [(0, k) for k in FFN2]
    grp_d = [(1, k) for k in FFN1]
    grp_e = [(1, k) for k in FFN2]
    grp_f = [(1, k) for k in MIXW]
    split(_run(_ag_sibling(_run(_ag_chips(cat(grp_a)), "ag_chips")), "ag_sibling"), grp_a)

    xs = x[0]
    saved = []
    cur = xs
    for l in range(DEPTH):
        row = lambda n: w[n][l].reshape(1, -1)
        ffn = lambda keys: tuple(wts[(l, k)].reshape(f, d) for k in keys)
        wa = _block_diag(w['rg_w_a'][l], 3)
        wx = _block_diag(w['rg_w_x'][l], 3)
        bf = jnp.pad(row('fox_b_f'), ((0, 0), (0, LANES - N_HEADS)))
        s5p = (w['s5_a_re'][l], w['s5_a_im'][l], w['s5_log_dt'][l].reshape(-1, 1),
               w['s5_b_re'][l].transpose(0, 2, 1), w['s5_b_im'][l].transpose(0, 2, 1))
        ab_re, ab_im, bb_re, bb_im = _s5_prep(*s5p)
        bd_re, bd_im = _block_diag(bb_re, 2), _block_diag(bb_im, 2)
        cd_re = _block_diag(w['s5_c_re'][l].transpose(0, 2, 1), 2)
        cd_im = _block_diag(w['s5_c_im'][l].transpose(0, 2, 1), 2)
        abr, abi = ab_re.reshape(2, 1, N_STATE // 2), ab_im.reshape(2, 1, N_STATE // 2)
        gm = row('mix_norm_g')
        ga, gb, gc = gm[:, :D_A], gm[:, D_A:D_A + D_B], gm[:, D_A + D_B:]

        x0 = cur
        ffn1 = ffn(FFN1)
        (x1, pre1, gs1, us1), cres = _ffn_fwd(x0, *ffn1, row('ln1_g'), row('ln1_b'), tm, tf,
                                              carry=_ag_chips(cat(grp_b if l == 0 else grp_e)))
        if l == 0:
            split(_run(_ag_sibling(cres), "ag_sibling"), grp_b)
        else:
            g_e = cres
        win = wts[(l, 'win')].reshape(d, Z_W)
        wout = wts[(l, 'wout')].reshape(d, d).astype(F32)
        wglu = wts[(l, 'glu')][:, :glu_rows].reshape(D_C, D_C).astype(F32)
        conv_full = lax.bitcast_convert_type(
            wts[(l, 'conv')][:, 0, :2 * CONV_WIDTH * D_A // 8].reshape(8, CONV_WIDTH, D_A // 8, 2), F32)
        conv_full = conv_full.transpose(1, 0, 2).reshape(CONV_WIDTH, D_A)
        z = _mm(x1, win, 'nn', F32, tm, Z_W, d, "mix_in")
        out_a, h_a = _rg_fwd(z, conv_full, row('conv_b'), wa, row('rg_b_a'), wx, row('rg_b_x'), row('rg_lambda'))
        cs = _fgate_fwd(z, bf)
        crow = cs[:, :N_HEADS].T.reshape(N_HEADS, 1, t)
        (out_b, lse), cres = _attn_fwd(z, crow, carry=_ag_chips(cat(grp_c)) if l == 0 else _ag_sibling(g_e))
        if l == 0:
            (yc,), cres = _s5_fwd(z, bd_re, bd_im, abr, abi, cd_re, cd_im, row('s5_d'),
                                  carry=_join(_ag_sibling(cres), _ag_chips(cat(grp_f))))
            split(cres[:len(grp_c)], grp_c)
            g_f = cres[len(grp_c):]
        else:
            split(cres, grp_e)
            (yc,), _ = _s5_fwd(z, bd_re, bd_im, abr, abi, cd_re, cd_im, row('s5_d'))
        mix_params = [ga, gb, gc, wglu, wout, row('ln2_g'), row('ln2_b')]
        (x2,) = _rowwise(_mix_out, [out_a, out_b, yc, x1], mix_params, [(d, F32)], tm, "mix_out")
        ffn2 = ffn(FFN2)
        (x3, pre3, gs2, us2), cres = _ffn_fwd(x2, *ffn2, row('ln3_g'), row('ln3_b'), tm, tf,
                                              carry=_join(_ag_chips(cat(grp_d)), _ag_sibling(g_f)) if l == 0 else None)
        if l == 0:
            split(cres[len(grp_d):], grp_f)
            split(_run(_ag_sibling(cres[:len(grp_d)]), "ag_sibling"), grp_d)
        saved.append(dict(x0=x0, x1=x1, pre1=pre1, gs1=gs1, us1=us1, z=z, out_a=out_a, h_a=h_a, crow=crow,
                          out_b=out_b, lse=lse, yc=yc, x2=x2, pre3=pre3, gs2=gs2, us2=us2, mix_params=mix_params,
                          ffn1=ffn1, ffn2=ffn2, win=win, conv_full=conv_full, wa=wa, wx=wx, bf=bf, s5p=s5p,
                          s5m=(bd_re, bd_im, abr, abi, cd_re, cd_im)))
        cur = x3

    dy, loss_row = _loss_head(cur, loss_target[0], tm)
    loss = lax.psum(loss_row[0, 0], ("x", "y", "c"))

    assert DEPTH == 2
    small_grads = {}
    small_names = ['conv_w'] + SMALL
    c_idx = jnp.reshape(mc, (1,)).astype(jnp.int32)
    chip_idx = jnp.reshape(2 * mx + my, (1,)).astype(jnp.int32)

    def blocks(arrs):
        return jnp.concatenate([v.astype(BF16).reshape(8, -1, d) for v in arrs], axis=1)

    def mixer_blocks(dwin, dwout, dwglu):
        glu = jnp.pad(dwglu.astype(BF16).reshape(8, glu_rows, d), ((0, 0), (0, 32 - glu_rows), (0, 0)))
        return jnp.concatenate([dwin.reshape(8, win_rows, d), dwout.astype(BF16).reshape(8, ds, d), glu], axis=1)

    def pair(full, ra):
        return _pair_add(full, ra, c_idx, "rs_add_sibling", BF16)

    for l in reversed(range(DEPTH)):
        s = saved[l]
        row = lambda n: w[n][l].reshape(1, -1)
        wg_tiles = dict(tm=tf, tn=d, tk=_tile(t, 1024))
        first = l == 0

        def ffn_back(dyv, pre, gs, us, wts3, xin, ln_g, ln_b, carry_fn=None, pipeline=None):
            (dpre,), (dlg, dlb), _ = _rowwise_vjp(_ln_only, [pre], [ln_g, ln_b], [dyv], tm, "ln_bwd")
            (dx, dg, du, hh), cres = _ffn_bwd(dpre, gs, us, *wts3, tm, tf, carry=carry_fn(dlg, dlb) if carry_fn else None)
            if pipeline is None:
                dwg = _mm(dg, xin, 'tn', BF16, name="ffn_dw_gate", **wg_tiles)
                dwu = _mm(du, xin, 'tn', BF16, name="ffn_dw_up", **wg_tiles)
                dwd = _mm(hh, dpre, 'tn', BF16, name="ffn_dw_down", **wg_tiles)
                return dx, (dwg, dwu, dwd), dlg, dlb, cres

            def front(dw):
                full = dw.reshape(8, fs, d)
                (ra,) = _run(_rs_sibling(full), "rs_sibling")
                return pair(full, ra)

            dwg, (rb_first,) = _mm(dg, xin, 'tn', BF16, name="ffn_dw_gate", carry=_rs_chips(pipeline), **wg_tiles)
            cres = list(cres) + [rb_first]
            p_g = front(dwg)
            dwu, (rb_g,) = _mm(du, xin, 'tn', BF16, name="ffn_dw_up", carry=_rs_chips(p_g), **wg_tiles)
            p_u = front(dwu)
            dwd, (rb_u,) = _mm(hh, dpre, 'tn', BF16, name="ffn_dw_down", carry=_rs_chips(p_u), **wg_tiles)
            p_d = front(dwd)
            (rb_d,) = _run(_rs_chips(p_d), "rs_chips")
            return dx, ((p_g, rb_g), (p_u, rb_u), (p_d, rb_d)), dlg, dlb, cres

        dx2, (dwg2, dwu2, dwd2), dl3g, dl3b, cres = ffn_back(
            dy, s['pre3'], s['gs2'], s['us2'], s['ffn2'], s['x2'], row('ln3_g'), row('ln3_b'),
            (lambda *_: _rs_sibling(full_l1)) if first else None)
        if first:
            part_l1 = pair(full_l1, cres[0])
            full_c2 = blocks([dwg2, dwu2, dwd2])
        (d_oa, d_ob, d_yc, d_x1), (dga, dgb, dgc, dwglu, dwout, dl2g, dl2b), cres = _rowwise_vjp(
            _mix_out, [s['out_a'], s['out_b'], s['yc'], s['x1']], s['mix_params'], [dx2], tm, "mix_out_bwd",
            carry=_rs_sibling(full_c2) if first else None)
        if first:
            part_c2 = pair(full_c2, cres[0])
        (d_ax, d_ag, dcw, dcb, dwa, dba, dwx, dbx, dlam) = _rg_bwd(
            s['z'], s['h_a'], d_oa, s['conv_full'], row('conv_b'), s['wa'], row('rg_b_a'), s['wx'], row('rg_b_x'), row('rg_lambda'))
        (dq, dk, dv, dcrow), cres = _attn_bwd(s['z'], s['crow'], s['lse'], d_ob,
                                              carry=_join(_rs_chips(part_l1), _rs_chips(part_c2)) if first else None)
        if first:
            rb_l1, rb_c2 = cres
        dc_pad = jnp.pad(dcrow.reshape(N_HEADS, t).T, ((0, 0), (0, LANES - N_HEADS)))
        dzf, dbf = _fgate_bwd(s['z'], s['bf'], dc_pad)
        du_c, dbd_re, dbd_im, dabr, dabi, dcd_re, dcd_im, dd = _s5_bwd(s['z'], d_yc, *s['s5m'], row('s5_d'))
        dz = jnp.concatenate([d_ax, d_ag, dq, dk, dv, dzf, du_c], axis=1)
        dx1 = _mm(dz, s['win'], 'nt', F32, tm, d, Z_W, "mix_in_dx", add=d_x1)
        dwin = _mm(s['x1'], dz, 'tn', BF16, d, Z_W // 2, _tile(t, 1024), "mix_in_dw")
        dbb_re, dbb_im = _diag_blocks(dbd_re, N_GROUPS // 2), _diag_blocks(dbd_im, N_GROUPS // 2)
        dcm_re, dcm_im = _diag_blocks(dcd_re, N_GROUPS // 2), _diag_blocks(dcd_im, N_GROUPS // 2)
        da_re, da_im, dlog_dt, db_re, db_im = _s5_prep_bwd(*s['s5p'], dabr.reshape(N_GROUPS, C_STATE), dabi.reshape(N_GROUPS, C_STATE), dbb_re, dbb_im)
        sg = dict(conv_w=dcw, conv_b=dcb, rg_w_a=_diag_blocks(dwa, 2), rg_b_a=dba,
                  rg_w_x=_diag_blocks(dwx, 2), rg_b_x=dbx, rg_lambda=dlam, fox_b_f=dbf[:, :N_HEADS],
                  s5_a_re=da_re, s5_a_im=da_im, s5_log_dt=dlog_dt, s5_b_re=db_re.transpose(0, 2, 1), s5_b_im=db_im.transpose(0, 2, 1),
                  s5_c_re=dcm_re.transpose(0, 2, 1), s5_c_im=dcm_im.transpose(0, 2, 1), s5_d=dd,
                  mix_norm_g=jnp.concatenate([dga, dgb, dgc], axis=1), ln2_g=dl2g, ln2_b=dl2b, ln3_g=dl3g, ln3_b=dl3b)
        small_grads[l] = sg

        if first:
            full_m = mixer_blocks(dwin, dwout, dwglu)
            (ra_m,) = _run(_rs_sibling(full_m), "rs_sibling")
            part_m = pair(full_m, ra_m)

            def last_carry(dlg, dlb):
                sg.update(ln1_g=dlg, ln1_b=dlb)
                packed = _pack_small([small_grads[ll][n] for n in small_names for ll in range(DEPTH)])
                return _gather_small(packed)

            dx0, tail, _, _, (all_small, rb_m) = ffn_back(dx1, s['pre1'], s['gs1'], s['us1'], s['ffn1'], s['x0'], row('ln1_g'),
                                                          row('ln1_b'), last_carry, pipeline=part_m)
        else:
            dx0, (dwg1, dwu1, dwd1), dl1g, dl1b, _ = ffn_back(dx1, s['pre1'], s['gs1'], s['us1'], s['ffn1'], s['x0'],
                                                              row('ln1_g'), row('ln1_b'))
            sg.update(ln1_g=dl1g, ln1_b=dl1b)
            full_l1 = jnp.concatenate([blocks([dwg1, dwu1, dwd1, dwg2, dwu2, dwd2]), mixer_blocks(dwin, dwout, dwglu)], axis=1)
        dy = dx0

    grad_x = dy.reshape(x.shape)
    quad = lambda part, rb: _quad_add(part, rb, chip_idx, "rs_add_chips")
    own = {}

    def take(rows_f32, keys, l):
        off = 0
        for k, r in keys:
            own[(l, k)] = rows_f32[off:off + r]
            off += r

    ffn_keys = lambda names: [(k, fs) for k in names]
    mix_keys = [('win', win_rows), ('wout', ds), ('glu', glu_rows)]
    take(quad(part_l1, rb_l1), ffn_keys(FFN1 + FFN2) + mix_keys, 1)
    take(quad(part_c2, rb_c2), ffn_keys(FFN2), 0)
    take(quad(part_m, rb_m), mix_keys, 0)
    for k, (part, rb) in zip(FFN1, tail):
        own[(0, k)] = quad(part, rb)

    grads = {}
    grads_t = {}
    for k, n in zip(FFN1 + FFN2, ['ffn1_w_gate', 'ffn1_w_up', 'ffn1_w_down', 'ffn2_w_gate', 'ffn2_w_up', 'ffn2_w_down']):
        stacked = jnp.stack([own[(l, k)] for l in range(DEPTH)])
        if 'down' in n:
            grads[n] = stacked
        else:
            grads_t[n] = stacked
            grads[n] = jnp.swapaxes(stacked, 1, 2)
    gwin = jnp.stack([own[(l, 'win')].reshape(ds, Z_W) for l in range(DEPTH)])
    grads['w_in'] = jnp.concatenate([gwin[:, :, :Z_F + N_HEADS], gwin[:, :, Z_U:]], axis=2)
    grads['w_out'] = jnp.stack([own[(l, 'wout')] for l in range(DEPTH)])
    grads['s5_w_glu'] = jnp.stack([own[(l, 'glu')].reshape(D_C // 8, D_C) for l in range(DEPTH)])

    small_shapes = [((DEPTH, CONV_WIDTH, D_A) if n == 'conv_w' else w[n].shape) for n in small_names]
    conv_zero = jnp.zeros((DEPTH, CONV_WIDTH, D_A), F32)
    summed = _sum_small(all_small)
    for n, g in zip(small_names, _unpack_small(summed, small_shapes)):
        grads[n] = g
    grads['conv_w'] = lax.dynamic_slice_in_dim(grads['conv_w'], me * (D_A // 8), D_A // 8, axis=2)

    delta, new_m, new_v = {}, {}, {}
    for n in BIG + ['conv_w']:
        if n in grads_t:
            sh = grads_t[n].shape
            two = lambda v: jnp.swapaxes(v, 1, 2).reshape(-1, sh[-1])
            back = lambda v: jnp.swapaxes(v.reshape(sh), 1, 2)
            g2 = grads_t[n].reshape(-1, sh[-1])
        else:
            sh = w[n].shape
            two = lambda v: v.reshape(-1, sh[-1])
            back = lambda v: v.reshape(sh)
            g2 = two(grads[n])
        dl, nm, nv = _adamw(two(w[n]), g2, two(a['m_' + n]), two(a['v_' + n]), "adamw_" + n)
        delta[n], new_m[n], new_v[n] = back(dl), back(nm), back(nv)
    dl, nm, nv = _adamw(_pack_small([conv_zero] + [w[n] for n in SMALL]), summed,
                        _pack_small([conv_zero] + [a['m_' + n] for n in SMALL]),
                        _pack_small([conv_zero] + [a['v_' + n] for n in SMALL]), "adamw_small")
    for n, v1, v2, v3 in zip(small_names[1:], _unpack_small(dl, small_shapes)[1:], _unpack_small(nm, small_shapes)[1:],
                             _unpack_small(nv, small_shapes)[1:]):
        delta[n], new_m[n], new_v[n] = v1, v2, v3

    return (loss, grad_x, *[grads[n] for n in WEIGHTS], *[delta[n] for n in WEIGHTS], *[new_m[n] for n in WEIGHTS],
            *[new_v[n] for n in WEIGHTS])
```
